```python
import math
import jax, jax.numpy as jnp
from jax import lax
import numpy as np

D_MODEL = 2048
BATCH = 8
SEQ = 2048
DEPTH = 1

W_A = D_MODEL // 2
DH_A = 128
H_A = W_A // DH_A
D_C = D_MODEL // 8
H_IDX = 16
D_IDX = 64
TOPK_MAX = 256
Q_BLOCK = 128
W_M = D_MODEL - W_A
H_M = 4
DV_M = W_M // H_M
DK_M = DV_M // 2
CONV_W = 4
CHUNK = 64
N_BUCKETS = 32
MAX_DIST = 128
ALPHA = (2 * DEPTH) ** 0.25
BETA = (8 * DEPTH) ** -0.25
LN_EPS = 1e-5

SPLITS = (
    W_A,
    D_C,
    W_A,
    H_IDX * D_IDX,
    D_IDX,
    H_IDX,
    H_M * DK_M,
    H_M * DK_M,
    W_M,
    H_M,
    H_M,
    W_M,
    W_M,
)
N_COLS = sum(SPLITS)

kernel_name = 'hymba_dsa_mlstm_deepnorm'


def layer_norm(x, g, b):
    xf = x.astype(jnp.float32)
    mu = jnp.mean(xf, axis=-1, keepdims=True)
    var = jnp.mean(jnp.square(xf - mu), axis=-1, keepdims=True)
    return ((xf - mu) * lax.rsqrt(var + LN_EPS) * g + b).astype(x.dtype)


def rms_norm(x, g):
    xf = x.astype(jnp.float32)
    return (xf * lax.rsqrt(jnp.mean(xf * xf, axis=-1, keepdims=True) + LN_EPS) * g).astype(x.dtype)


def causal_dwconv(x, w, b):
    T = x.shape[1]
    xp = jnp.pad(x, ((0, 0), (CONV_W - 1, 0), (0, 0)))
    return sum(w[j] * xp[:, j:j + T] for j in range(CONV_W)) + b


def t5_bucket(rel):
    max_exact = N_BUCKETS // 2
    n = jnp.maximum(rel, 0)
    nf = jnp.maximum(n, 1).astype(jnp.float32)
    large = max_exact + (jnp.log(nf / max_exact) / math.log(MAX_DIST / max_exact)
                         * (N_BUCKETS - max_exact)).astype(jnp.int32)
    large = jnp.minimum(large, N_BUCKETS - 1)
    return jnp.where(n < max_exact, n, large)


def dsa_attention(q_lat, ckv, q_idx, k_idx, w_idx, rel_bias):
    B, T = ckv.shape[:2]
    n_sel = min(TOPK_MAX, T // 4)
    nb = T // Q_BLOCK
    key_pos = jnp.arange(T)

    def block(args):
        qb, qib, wb, start = args
        qpos = start + jnp.arange(Q_BLOCK)
        sc = jnp.einsum('bqhd,bsd->bqhs', qib, k_idx)
        score = jnp.einsum('bqh,bqhs->bqs', wb, jax.nn.relu(sc)).astype(jnp.float32)
        causal = key_pos[None, :] <= qpos[:, None]
        score = jnp.where(causal[None], score, -jnp.inf)
        _, sel = lax.top_k(score, n_sel)
        c_sel = jax.vmap(lambda c, i: c[i])(ckv, sel)
        logits = jnp.einsum('bqhc,bqkc->bhqk', qb, c_sel).astype(jnp.float32) * (DH_A ** -0.5)
        rel = qpos[None, :, None] - sel
        bias = rel_bias[t5_bucket(rel)]
        logits = logits + jnp.transpose(bias, (0, 3, 1, 2)).astype(jnp.float32)
        logits = jnp.where((rel >= 0)[:, None], logits, -jnp.inf)
        p = jax.nn.softmax(logits, axis=-1).astype(ckv.dtype)
        return jnp.einsum('bhqk,bqkc->bqhc', p, c_sel)

    def to_blocks(a):
        return jnp.moveaxis(a.reshape((B, nb, Q_BLOCK) + a.shape[2:]), 1, 0)

    out = lax.map(block, (to_blocks(q_lat), to_blocks(q_idx), to_blocks(w_idx),
                          jnp.arange(nb) * Q_BLOCK))
    return jnp.moveaxis(out, 0, 1).reshape(B, T, H_A, D_C)


def mlstm_chunkwise(q, k, v, i_pre, f_pre):
    B, T, H, DK = q.shape
    DV = v.shape[-1]
    nc = T // CHUNK
    f32 = jnp.float32

    def chunks(a):
        a = a.reshape((B, nc, CHUNK, H) + a.shape[3:])
        return jnp.moveaxis(jnp.moveaxis(a, 1, 0), 3, 2)

    qc = chunks(q.astype(f32))
    kc = chunks(k.astype(f32) * (DK ** -0.5))
    vc = chunks(v.astype(f32))
    ic = chunks(i_pre.astype(f32))
    lfc = chunks(jax.nn.log_sigmoid(f_pre.astype(f32)))
    tri = jnp.tril(jnp.ones((CHUNK, CHUNK), dtype=bool))

    def step(carry, xs):
        C, n, m = carry
        qq, kk, vv, ii, lf = xs
        b = jnp.cumsum(lf, axis=-1)
        logD = jnp.where(tri, b[..., :, None] - b[..., None, :] + ii[..., None, :], -jnp.inf)
        g = b + m[..., None]
        m_t = jnp.maximum(jnp.max(logD, axis=-1), g)
        S = jnp.einsum('bhld,bhsd->bhls', qq, kk) * jnp.exp(logD - m_t[..., None])
        inter = jnp.exp(g - m_t)
        num = jnp.einsum('bhls,bhsv->bhlv', S, vv) + inter[..., None] * jnp.einsum('bhld,bhvd->bhlv', qq, C)
        den = jnp.sum(S, axis=-1) + inter * jnp.einsum('bhld,bhd->bhl', qq, n)
        h = num / jnp.maximum(jnp.abs(den), jnp.exp(-m_t))[..., None]
        bL = b[..., -1]
        a = bL[..., None] - b + ii
        m_new = jnp.maximum(bL + m, jnp.max(a, axis=-1))
        decay = jnp.exp(bL + m - m_new)
        wgt = jnp.exp(a - m_new[..., None])
        C_new = decay[..., None, None] * C + jnp.einsum('bhl,bhlv,bhld->bhvd', wgt, vv, kk)
        n_new = decay[..., None] * n + jnp.einsum('bhl,bhld->bhd', wgt, kk)
        return (C_new, n_new, m_new), h

    init = (jnp.zeros((B, H, DV, DK), f32), jnp.zeros((B, H, DK), f32), jnp.zeros((B, H), f32))
    _, hs = lax.scan(step, init, (qc, kc, vc, ic, lfc))
    return jnp.transpose(hs, (1, 0, 3, 2, 4)).reshape(B, T, H, DV).astype(q.dtype)


def setup_inputs(seed: int = 0) -> dict:
    key = jax.random.key(seed)
    ks = jax.random.split(key, 20)
    f32 = jnp.float32
    nrm = lambda k, s, sc: jax.random.normal(k, s, f32) * sc
    return {
        'x': nrm(ks[0], (BATCH, SEQ, D_MODEL), 1.0),
        'w_in': nrm(ks[1], (DEPTH, D_MODEL, N_COLS), D_MODEL ** -0.5),
        'b_igate': nrm(ks[2], (DEPTH, H_M), 0.1),
        'b_fgate': jnp.linspace(3.0, 6.0, H_M, dtype=f32)[None] + nrm(ks[3], (DEPTH, H_M), 0.1),
        'kv_norm_g': 1.0 + nrm(ks[4], (DEPTH, D_C), 0.02),
        'w_uk': nrm(ks[5], (DEPTH, H_A, DH_A, D_C), D_C ** -0.5),
        'w_uv': nrm(ks[6], (DEPTH, H_A, D_C, DH_A), D_C ** -0.5),
        'idx_k_ln_g': 1.0 + nrm(ks[7], (DEPTH, D_IDX), 0.02),
        'idx_k_ln_b': nrm(ks[8], (DEPTH, D_IDX), 0.02),
        'rel_bias': nrm(ks[9], (N_BUCKETS, H_A), 0.5),
        'conv_w': nrm(ks[10], (DEPTH, CONV_W, 2 * H_M * DK_M), CONV_W ** -0.5),
        'conv_b': nrm(ks[11], (DEPTH, 2 * H_M * DK_M), 0.01),
        'mh_norm_g': 1.0 + nrm(ks[12], (DEPTH, W_M), 0.02),
        'w_out': nrm(ks[13], (DEPTH, W_A + W_M, D_MODEL), BETA * (W_A + W_M) ** -0.5),
        'ln_g': 1.0 + nrm(ks[14], (DEPTH, D_MODEL), 0.02),
        'ln_b': nrm(ks[15], (DEPTH, D_MODEL), 0.02),
    }


def reference(x, w_in, b_igate, b_fgate, kv_norm_g, w_uk, w_uv, idx_k_ln_g, idx_k_ln_b,
              rel_bias, conv_w, conv_b, mh_norm_g, w_out, ln_g, ln_b):
    B, T, _ = x.shape
    offs = [int(o) for o in np.cumsum(SPLITS)[:-1]]
    for l in range(DEPTH):
        proj = x @ w_in[l]
        (q_a, c_kv, z_a, q_i, k_i, w_i, q_m, k_m, v_m,
         i_m, f_m, o_m, z_m) = jnp.split(proj, offs, axis=-1)

        c_kv = rms_norm(c_kv, kv_norm_g[l])
        q_lat = jnp.einsum('bthd,hdc->bthc', q_a.reshape(B, T, H_A, DH_A), w_uk[l])
        q_i = q_i.reshape(B, T, H_IDX, D_IDX) * (D_IDX ** -0.5)
        k_i = layer_norm(k_i, idx_k_ln_g[l], idx_k_ln_b[l])
        w_i = w_i * (H_IDX ** -0.5)
        o_lat = dsa_attention(q_lat, c_kv, q_i, k_i, w_i, rel_bias)
        y_a = jnp.einsum('bthc,hcd->bthd', o_lat, w_uv[l]).reshape(B, T, W_A) * jax.nn.silu(z_a)

        qk = jax.nn.silu(causal_dwconv(jnp.concatenate([q_m, k_m], axis=-1), conv_w[l], conv_b[l]))
        q_m, k_m = jnp.split(qk, 2, axis=-1)
        h_m = mlstm_chunkwise(q_m.reshape(B, T, H_M, DK_M), k_m.reshape(B, T, H_M, DK_M),
                              v_m.reshape(B, T, H_M, DV_M), i_m + b_igate[l], f_m + b_fgate[l])
        hf = h_m.astype(jnp.float32)
        mu = jnp.mean(hf, axis=-1, keepdims=True)
        var = jnp.mean(jnp.square(hf - mu), axis=-1, keepdims=True)
        h_m = ((hf - mu) * lax.rsqrt(var + LN_EPS)).reshape(B, T, W_M) * mh_norm_g[l]
        y_m = h_m.astype(x.dtype) * jax.nn.sigmoid(o_m) * jax.nn.silu(z_m)

        y = jnp.concatenate([y_a, y_m], axis=-1) @ w_out[l]
        x = layer_norm(ALPHA * x + y, ln_g[l], ln_b[l])
    return x
```

```python
import functools
import math

import numpy as np
import jax
import jax.numpy as jnp
from jax import lax
from jax.experimental import pallas as pl
from jax.experimental.pallas import tpu as pltpu

F32 = jnp.float32
BF16 = jnp.bfloat16

D_MODEL = 2048
W_A = 1024
DH_A = 128
H_A = 8
D_C = 256
H_IDX = 16
D_IDX = 64
TOPK = 256
W_M = 1024
H_M = 4
DV_M = 256
DK_M = 128
CONV_W = 4
N_BUCKETS = 32
MAX_DIST = 128
ALPHA = 2.0 ** 0.25
LN_EPS = 1e-5

LANES = 128
SUBLANES = 8
VMEM_LIMIT = 56 * 1024 * 1024

QB = 128
KB = 128
CNT_ROWS = 256
L_M = 256
HALO = 16
NEG = -1e30

C_QA, C_ZA, C_QI, C_QM, C_KM, C_VM, C_OM, C_ZM = 0, 1024, 2048, 3072, 3584, 4096, 5120, 6144
N_MAIN = 7168
N_SMALL = 384
S_KI, S_WI, S_IM, S_FM = 0, 64, 80, 84


def _t5_bucket_np(rel):
    max_exact = N_BUCKETS // 2
    n = np.maximum(rel, 0)
    nf = np.maximum(n, 1).astype(np.float32)
    large = max_exact + (np.log(nf / np.float32(max_exact)) / np.float32(math.log(MAX_DIST / max_exact))
                         * np.float32(N_BUCKETS - max_exact)).astype(np.int32)
    large = np.minimum(large, N_BUCKETS - 1)
    return np.where(n < max_exact, n, large).astype(np.int32)


def _bucket_tiles():
    i = np.arange(QB)[:, None]
    j = np.arange(KB)[None, :]
    t0 = _t5_bucket_np(i - j)
    t1 = _t5_bucket_np(i - j + KB)
    t2 = _t5_bucket_np(np.full((QB, KB), 2 * KB + 1))
    assert (t5 := _t5_bucket_np(np.arange(KB + 1, 4096))).min() == t5.max() == t2[0, 0]
    return np.stack([t0, t1, t2]).astype(np.int32)


def _bias_kernel(bucket_ref, rb_ref, out_ref):
    h = pl.program_id(0)
    for k in range(3):
        bk = bucket_ref[k]
        acc = jnp.zeros((QB, KB), F32)
        for b in range(N_BUCKETS):
            acc = jnp.where(bk == b, rb_ref[b, h], acc)
        out_ref[0, k] = acc


def _bias_tiles(rel_bias):
    bucket = jnp.asarray(_bucket_tiles())
    return pl.pallas_call(
        _bias_kernel,
        grid=(H_A,),
        in_specs=[pl.BlockSpec((3, QB, KB), lambda h: (0, 0, 0)),
                  pl.BlockSpec(memory_space=pltpu.SMEM)],
        out_specs=pl.BlockSpec((1, 3, QB, KB), lambda h: (h, 0, 0, 0)),
        out_shape=jax.ShapeDtypeStruct((H_A, 3, QB, KB), F32),
        name="bias_tiles",
    )(bucket, rel_bias)


def _proj_kernel(x_ref, w_ref, ws_ref, o_ref, os_ref, xb_ref):
    @pl.when(pl.program_id(1) == 0)
    def _():
        xb_ref[...] = x_ref[...].astype(BF16)
        os_ref[...] = jnp.dot(xb_ref[...], ws_ref[...], preferred_element_type=F32)

    o_ref[...] = jnp.dot(xb_ref[...], w_ref[...], preferred_element_type=F32).astype(BF16)


def _proj(x2d, w_main, w_small, tm=1024, tn=512):
    M = x2d.shape[0]
    return pl.pallas_call(
        _proj_kernel,
        grid=(M // tm, N_MAIN // tn),
        in_specs=[pl.BlockSpec((tm, D_MODEL), lambda i, j: (i, 0)),
                  pl.BlockSpec((D_MODEL, tn), lambda i, j: (0, j)),
                  pl.BlockSpec((D_MODEL, N_SMALL), lambda i, j: (0, 0))],
        out_specs=[pl.BlockSpec((tm, tn), lambda i, j: (i, j)),
                   pl.BlockSpec((tm, N_SMALL), lambda i, j: (i, 0))],
        out_shape=[jax.ShapeDtypeStruct((M, N_MAIN), BF16),
                   jax.ShapeDtypeStruct((M, N_SMALL), F32)],
        scratch_shapes=[pltpu.VMEM((tm, D_MODEL), BF16)],
        compiler_params=pltpu.CompilerParams(
            dimension_semantics=("arbitrary", "arbitrary"), vmem_limit_bytes=VMEM_LIMIT),
        name="proj",
    )(x2d, w_main, w_small)


def _prep_kernel(s_ref, kvg_ref, ig_ref, ib_ref, ckv_ref, kidx_ref, gt_ref):
    c = s_ref[:, 0:D_C]
    c = c * lax.rsqrt(jnp.mean(c * c, axis=-1, keepdims=True) + LN_EPS) * kvg_ref[...]
    ckv_ref[...] = c.astype(BF16)
    tile = s_ref[:, D_C:D_C + LANES]
    k = tile[:, S_KI:S_KI + D_IDX]
    mu = jnp.mean(k, axis=-1, keepdims=True)
    var = jnp.mean(jnp.square(k - mu), axis=-1, keepdims=True)
    kidx_ref[...] = ((k - mu) * lax.rsqrt(var + LN_EPS) * ig_ref[...] + ib_ref[...]).astype(BF16)
    gt_ref[...] = tile.T


def _prep(small, kv_g, idx_g, idx_b, tm=1024):
    M = small.shape[0]
    return pl.pallas_call(
        _prep_kernel,
        grid=(M // tm,),
        in_specs=[pl.BlockSpec((tm, N_SMALL), lambda i: (i, 0)),
                  pl.BlockSpec((1, D_C), lambda i: (0, 0)),
                  pl.BlockSpec((1, D_IDX), lambda i: (0, 0)),
                  pl.BlockSpec((1, D_IDX), lambda i: (0, 0))],
        out_specs=[pl.BlockSpec((tm, D_C), lambda i: (i, 0)),
                   pl.BlockSpec((tm, D_IDX), lambda i: (i, 0)),
                   pl.BlockSpec((LANES, tm), lambda i: (0, i))],
        out_shape=[jax.ShapeDtypeStruct((M, D_C), BF16),
                   jax.ShapeDtypeStruct((M, D_IDX), BF16),
                   jax.ShapeDtypeStruct((LANES, M), F32)],
        name="prep",
    )(small, kv_g, idx_g, idx_b)


def _key_to_float(key):
    bits = jnp.where(key < 0, key ^ jnp.int32(0x7FFFFFFF), key)
    return lax.bitcast_convert_type(bits, F32)


def _dsa_kernel(qa_ref, za_ref, qi_ref, gt_ref, ckv_ref, kidx_ref, wuk_ref, wuv_ref, bias_ref,
                y_ref, qall_ref, qr_ref, sc_ref, mb_ref, m_ref, l_ref, acc_ref):
    qi = pl.program_id(1)
    nkb = qi + 1

    for h in range(H_A):
        ql = jnp.dot(qa_ref[:, h * DH_A:(h + 1) * DH_A], wuk_ref[h], preferred_element_type=F32)
        qall_ref[h * QB:(h + 1) * QB, :] = (ql * (DH_A ** -0.5)).astype(BF16)

    @pl.when(qi * QB < TOPK)
    def _():
        row = lax.broadcasted_iota(jnp.int32, (QB, KB), 0)
        col = lax.broadcasted_iota(jnp.int32, (QB, KB), 1)
        diag = jnp.where(col <= row, 0.0, NEG).astype(F32)

        def fill(kb, carry):
            mb_ref[kb] = jnp.zeros((QB, KB), F32)
            return carry
        lax.fori_loop(0, qi, fill, 0)
        mb_ref[qi] = diag

    @pl.when(qi * QB >= TOPK)
    def _():
        for h in range(H_IDX):
            qr_ref[h * QB:(h + 1) * QB, :] = qi_ref[:, h * D_IDX:(h + 1) * D_IDX]
        w_all = gt_ref[S_WI:S_WI + H_IDX, :] * ((D_IDX ** -0.5) * (H_IDX ** -0.5))

        def scores(kb):
            k = kidx_ref[pl.ds(pl.multiple_of(kb * KB, KB), KB), :]
            acc = jnp.zeros((KB, QB), F32)
            for h in range(H_IDX):
                s = lax.dot_general(k, qr_ref[h * QB:(h + 1) * QB, :], (((1,), (1,)), ((), ())),
                                    preferred_element_type=F32)
                acc = acc + jnp.maximum(s, 0.0) * w_all[h:h + 1, :]
            return acc

        pad_kb = jnp.minimum(nkb, sc_ref.shape[0] // KB - 1)
        sc_ref[pl.ds(pl.multiple_of(pad_kb * KB, KB), KB), :] = jnp.full((KB, QB), -jnp.inf, F32)

        def sc_body(kb, carry):
            sc_ref[pl.ds(pl.multiple_of(kb * KB, KB), KB), :] = scores(kb)
            return carry
        lax.fori_loop(0, qi, sc_body, 0)
        krow = lax.broadcasted_iota(jnp.int32, (KB, QB), 0)
        qcol = lax.broadcasted_iota(jnp.int32, (KB, QB), 1)
        sc_ref[pl.ds(pl.multiple_of(qi * KB, KB), KB), :] = jnp.where(krow <= qcol, scores(qi), -jnp.inf)

        n_cnt = (nkb * KB + CNT_ROWS - 1) // CNT_ROWS

        def count_ge(cand):
            def body(c, acc):
                blk = sc_ref[pl.ds(pl.multiple_of(c * CNT_ROWS, CNT_ROWS), CNT_ROWS), :]
                ge = jnp.where(blk >= cand, 1.0, 0.0).astype(F32)
                return acc + jnp.sum(ge.reshape(CNT_ROWS // SUBLANES, SUBLANES, QB), axis=0)
            acc = lax.fori_loop(0, n_cnt, body, jnp.zeros((SUBLANES, QB), F32))
            return jnp.sum(acc, axis=0, keepdims=True)

        def bit_body(i, u):
            trial = u | lax.shift_left(jnp.int32(1), 31 - i)
            cnt = count_ge(_key_to_float(trial ^ jnp.int32(-2 ** 31)))
            return jnp.where(cnt >= float(TOPK), trial, u)
        u = lax.fori_loop(0, 32, bit_body, jnp.zeros((1, QB), jnp.int32))
        thr = _key_to_float(u ^ jnp.int32(-2 ** 31))

        def mb_body(kb, carry):
            blk = sc_ref[pl.ds(pl.multiple_of(kb * KB, KB), KB), :]
            mb_ref[kb] = jnp.where(blk >= thr, 0.0, NEG).astype(F32).T
            return carry
        lax.fori_loop(0, nkb, mb_body, 0)

    m_ref[...] = jnp.full(m_ref.shape, NEG, F32)
    l_ref[...] = jnp.zeros(l_ref.shape, F32)
    acc_ref[...] = jnp.zeros(acc_ref.shape, F32)

    def att_body(kb, carry):
        kv = ckv_ref[pl.ds(pl.multiple_of(kb * KB, KB), KB), :]
        s_all = lax.dot_general(qall_ref[...], kv, (((1,), (1,)), ((), ())),
                                preferred_element_type=F32)
        mb = mb_ref[kb]
        tile = jnp.minimum(qi - kb, 2)
        ps = []
        for h in range(H_A):
            rows = slice(h * QB, (h + 1) * QB)
            s = s_all[rows] + bias_ref[h, tile] + mb
            m_prev = m_ref[rows]
            m_new = jnp.maximum(m_prev, jnp.max(s, axis=-1, keepdims=True))
            alpha = jnp.exp(m_prev - m_new)
            p = jnp.exp(s - m_new)
            l_ref[rows] = alpha * l_ref[rows] + jnp.sum(p, axis=-1, keepdims=True)
            acc_ref[rows] = alpha * acc_ref[rows]
            m_ref[rows] = m_new
            ps.append(p.astype(BF16))
        p_all = jnp.concatenate(ps, axis=0)
        acc_ref[...] += jnp.dot(p_all, kv, preferred_element_type=F32)
        return carry
    lax.fori_loop(0, nkb, att_body, 0)

    for h in range(H_A):
        rows = slice(h * QB, (h + 1) * QB)
        o = (acc_ref[rows] / l_ref[rows]).astype(BF16)
        ya = jnp.dot(o, wuv_ref[h], preferred_element_type=F32)
        z = za_ref[:, h * DH_A:(h + 1) * DH_A].astype(F32)
        y_ref[:, h * DH_A:(h + 1) * DH_A] = (ya * (z * jax.nn.sigmoid(z))).astype(BF16)


def _dsa(main, gate_t, ckv_n, kidx_n, w_uk, w_uv, bias, B, T):
    nq = T // QB
    return pl.pallas_call(
        _dsa_kernel,
        grid=(B, nq),
        in_specs=[pl.BlockSpec((QB, W_A), lambda b, q: (b * nq + q, C_QA // W_A)),
                  pl.BlockSpec((QB, W_A), lambda b, q: (b * nq + q, C_ZA // W_A)),
                  pl.BlockSpec((QB, H_IDX * D_IDX), lambda b, q: (b * nq + q, C_QI // (H_IDX * D_IDX))),
                  pl.BlockSpec((LANES, QB), lambda b, q: (0, b * nq + q)),
                  pl.BlockSpec((T, D_C), lambda b, q: (b, 0)),
                  pl.BlockSpec((T, D_IDX), lambda b, q: (b, 0)),
                  pl.BlockSpec((H_A, DH_A, D_C), lambda b, q: (0, 0, 0)),
                  pl.BlockSpec((H_A, D_C, DH_A), lambda b, q: (0, 0, 0)),
                  pl.BlockSpec((H_A, 3, QB, KB), lambda b, q: (0, 0, 0, 0))],
        out_specs=pl.BlockSpec((QB, W_A), lambda b, q: (b * nq + q, 0)),
        out_shape=jax.ShapeDtypeStruct((B * T, W_A), BF16),
        scratch_shapes=[pltpu.VMEM((H_A * QB, D_C), BF16),
                        pltpu.VMEM((H_IDX * QB, D_IDX), BF16),
                        pltpu.VMEM((T, QB), F32),
                        pltpu.VMEM((T // KB, QB, KB), F32),
                        pltpu.VMEM((H_A * QB, 1), F32),
                        pltpu.VMEM((H_A * QB, 1), F32),
                        pltpu.VMEM((H_A * QB, D_C), F32)],
        compiler_params=pltpu.CompilerParams(
            dimension_semantics=("arbitrary", "arbitrary"), vmem_limit_bytes=VMEM_LIMIT),
        name="dsa",
    )(main, main, main, gate_t, ckv_n, kidx_n, w_uk, w_uv, bias)


def _split_dot(tri, x):
    hi = x.astype(BF16)
    lo = (x - hi.astype(F32)).astype(BF16)
    return jnp.dot(tri, hi, preferred_element_type=F32) + jnp.dot(tri, lo, preferred_element_type=F32)


def _log_sigmoid(x):
    return jnp.minimum(x, 0.0) - jnp.log1p(jnp.exp(-jnp.abs(x)))


def _mlstm_kernel(q_ref, k_ref, qh_ref, kh_ref, v_ref, o_ref, z_ref, g_ref, gt_ref,
                  cw_ref, cb_ref, gbr_ref, gbc_ref, ng_ref, y_ref, ct_ref, n_ref, m_ref):
    c = pl.program_id(1)
    L = L_M

    @pl.when(c == 0)
    def _():
        ct_ref[...] = jnp.zeros(ct_ref.shape, F32)
        n_ref[...] = jnp.zeros(n_ref.shape, F32)
        m_ref[...] = jnp.zeros(m_ref.shape, F32)

    def conv_silu(x_ref, halo_ref, lo):
        halo = jnp.where(c > 0, halo_ref[...].astype(F32), 0.0)
        xe = jnp.concatenate([halo, x_ref[...].astype(F32)], axis=0)
        y = cb_ref[:, lo:lo + H_M * DK_M]
        for j in range(CONV_W):
            off = HALO - (CONV_W - 1) + j
            y = y + cw_ref[j:j + 1, lo:lo + H_M * DK_M] * xe[off:off + L]
        return y * jax.nn.sigmoid(y)

    q_all = conv_silu(q_ref, qh_ref, 0)
    k_all = conv_silu(k_ref, kh_ref, H_M * DK_M) * (DK_M ** -0.5)

    gc = g_ref[...] + gbr_ref[...]
    gr = gt_ref[S_IM:S_IM + 2 * H_M, :] + gbc_ref[S_IM:S_IM + 2 * H_M, :]
    r = lax.broadcasted_iota(jnp.int32, (L, L), 0)
    s = lax.broadcasted_iota(jnp.int32, (L, L), 1)
    causal = s <= r
    tri_l = jnp.where(causal, 1.0, 0.0).astype(BF16)
    tri_u = jnp.where(r <= s, 1.0, 0.0).astype(BF16)
    b_cols = _split_dot(tri_l, _log_sigmoid(gc))
    b_rows = jnp.dot(_log_sigmoid(gr).astype(BF16), tri_u, preferred_element_type=F32) \
        + jnp.dot((_log_sigmoid(gr) - _log_sigmoid(gr).astype(BF16).astype(F32)).astype(BF16), tri_u,
                  preferred_element_type=F32)

    for h in range(H_M):
        q = q_all[:, h * DK_M:(h + 1) * DK_M]
        k = k_all[:, h * DK_M:(h + 1) * DK_M]
        v = v_ref[:, h * DV_M:(h + 1) * DV_M]
        qb = q.astype(BF16)
        b_c = b_cols[:, S_FM + h:S_FM + h + 1]
        i_c = gc[:, S_IM + h:S_IM + h + 1]
        b_r = b_rows[H_M + h:H_M + h + 1, :]
        i_r = gr[h:h + 1, :]
        m_prev = m_ref[h]
        ct = ct_ref[h]
        n = n_ref[h]

        log_d = jnp.where(causal, b_c - b_r + i_r, -jnp.inf)
        g = b_c + m_prev
        m_t = jnp.maximum(jnp.max(log_d, axis=-1, keepdims=True), g)
        qk = lax.dot_general(qb, k.astype(BF16), (((1,), (1,)), ((), ())), preferred_element_type=F32)
        s_mat = qk * jnp.exp(log_d - m_t)
        inter = jnp.exp(g - m_t)
        num = jnp.dot(s_mat.astype(BF16), v, preferred_element_type=F32) \
            + inter * jnp.dot(qb, ct.astype(BF16), preferred_element_type=F32)
        den = jnp.sum(s_mat, axis=-1, keepdims=True) + inter * jnp.sum(q * n, axis=-1, keepdims=True)
        hh = num / jnp.maximum(jnp.abs(den), jnp.exp(-m_t))

        b_last = b_c[L - 1:L, :]
        a_r = b_last - b_r + i_r
        m_new = jnp.maximum(b_last + m_prev, jnp.max(a_r, axis=-1, keepdims=True))
        decay = jnp.exp(b_last + m_prev - m_new)
        wgt_c = jnp.exp(b_last - b_c + i_c - m_new)
        kw = k * wgt_c
        ct_ref[h] = decay * ct + jnp.dot(kw.T.astype(BF16), v, preferred_element_type=F32)
        n_ref[h] = decay * n + jnp.sum(kw, axis=0, keepdims=True)
        m_ref[h] = m_new

        mu = jnp.mean(hh, axis=-1, keepdims=True)
        var = jnp.mean(jnp.square(hh - mu), axis=-1, keepdims=True)
        hn = (hh - mu) * lax.rsqrt(var + LN_EPS) * ng_ref[:, h * DV_M:(h + 1) * DV_M]
        og = o_ref[:, h * DV_M:(h + 1) * DV_M].astype(F32)
        zg = z_ref[:, h * DV_M:(h + 1) * DV_M].astype(F32)
        y_ref[:, h * DV_M:(h + 1) * DV_M] = (hn * jax.nn.sigmoid(og) * (zg * jax.nn.sigmoid(zg))).astype(BF16)


def _mlstm(main, small, gate_t, conv_w, conv_b, gb_row, gb_col, norm_g, B, T):
    nc = T // L_M
    hb = L_M // HALO
    qk_w = H_M * DK_M

    def halo_map(col):
        return lambda b, c: (jnp.maximum((b * nc + c) * hb - 1, 0), col)

    return pl.pallas_call(
        _mlstm_kernel,
        grid=(B, nc),
        in_specs=[pl.BlockSpec((L_M, qk_w), lambda b, c: (b * nc + c, C_QM // qk_w)),
                  pl.BlockSpec((L_M, qk_w), lambda b, c: (b * nc + c, C_KM // qk_w)),
                  pl.BlockSpec((HALO, qk_w), halo_map(C_QM // qk_w)),
                  pl.BlockSpec((HALO, qk_w), halo_map(C_KM // qk_w)),
                  pl.BlockSpec((L_M, W_M), lambda b, c: (b * nc + c, C_VM // W_M)),
                  pl.BlockSpec((L_M, W_M), lambda b, c: (b * nc + c, C_OM // W_M)),
                  pl.BlockSpec((L_M, W_M), lambda b, c: (b * nc + c, C_ZM // W_M)),
                  pl.BlockSpec((L_M, LANES), lambda b, c: (b * nc + c, D_C // LANES)),
                  pl.BlockSpec((LANES, L_M), lambda b, c: (0, b * nc + c)),
                  pl.BlockSpec((CONV_W, 2 * qk_w), lambda b, c: (0, 0)),
                  pl.BlockSpec((1, 2 * qk_w), lambda b, c: (0, 0)),
                  pl.BlockSpec((1, LANES), lambda b, c: (0, 0)),
                  pl.BlockSpec((LANES, 1), lambda b, c: (0, 0)),
                  pl.BlockSpec((1, W_M), lambda b, c: (0, 0))],
        out_specs=pl.BlockSpec((L_M, W_M), lambda b, c: (b * nc + c, 0)),
        out_shape=jax.ShapeDtypeStruct((B * T, W_M), BF16),
        scratch_shapes=[pltpu.VMEM((H_M, DK_M, DV_M), F32),
                        pltpu.VMEM((H_M, 1, DK_M), F32),
                        pltpu.VMEM((H_M, 1, 1), F32)],
        compiler_params=pltpu.CompilerParams(
            dimension_semantics=("arbitrary", "arbitrary"), vmem_limit_bytes=VMEM_LIMIT),
        name="mlstm",
    )(main, main, main, main, main, main, main, small, gate_t,
      conv_w, conv_b, gb_row, gb_col, norm_g)


def _out_kernel(ya_ref, ym_ref, x_ref, w_ref, g_ref, b_ref, o_ref):
    y = jnp.dot(ya_ref[...], w_ref[0:W_A, :], preferred_element_type=F32)
    y = y + jnp.dot(ym_ref[...], w_ref[W_A:W_A + W_M, :], preferred_element_type=F32)
    r = ALPHA * x_ref[...] + y
    mu = jnp.mean(r, axis=-1, keepdims=True)
    var = jnp.mean(jnp.square(r - mu), axis=-1, keepdims=True)
    o_ref[...] = (r - mu) * lax.rsqrt(var + LN_EPS) * g_ref[...] + b_ref[...]


def _out(ya, ym, x2d, w_out, ln_g, ln_b, tm=256):
    M = x2d.shape[0]
    return pl.pallas_call(
        _out_kernel,
        grid=(M // tm,),
        in_specs=[pl.BlockSpec((tm, W_A), lambda i: (i, 0)),
                  pl.BlockSpec((tm, W_M), lambda i: (i, 0)),
                  pl.BlockSpec((tm, D_MODEL), lambda i: (i, 0)),
                  pl.BlockSpec((W_A + W_M, D_MODEL), lambda i: (0, 0)),
                  pl.BlockSpec((1, D_MODEL), lambda i: (0, 0)),
                  pl.BlockSpec((1, D_MODEL), lambda i: (0, 0))],
        out_specs=pl.BlockSpec((tm, D_MODEL), lambda i: (i, 0)),
        out_shape=jax.ShapeDtypeStruct((M, D_MODEL), F32),
        compiler_params=pltpu.CompilerParams(
            dimension_semantics=("arbitrary",), vmem_limit_bytes=VMEM_LIMIT),
        name="out",
    )(ya, ym, x2d, w_out, ln_g, ln_b)


def _repack_w_in(w):
    offs = np.cumsum([0, W_A, D_C, W_A, H_IDX * D_IDX, D_IDX, H_IDX, H_M * DK_M, H_M * DK_M,
                      W_M, H_M, H_M, W_M, W_M])
    seg = [w[:, offs[i]:offs[i + 1]] for i in range(13)]
    (q_a, c_kv, z_a, q_i, k_i, w_i, q_m, k_m, v_m, i_m, f_m, o_m, z_m) = seg
    main = jnp.concatenate([q_a, z_a, q_i, q_m, k_m, v_m, o_m, z_m], axis=1).astype(BF16)
    pad = jnp.zeros((w.shape[0], N_SMALL - D_C - D_IDX - H_IDX - 2 * H_M), w.dtype)
    small = jnp.concatenate([c_kv, k_i, w_i, i_m, f_m, pad], axis=1).astype(BF16)
    return main, small


def kernel(x, w_in, b_igate, b_fgate, kv_norm_g, w_uk, w_uv, idx_k_ln_g, idx_k_ln_b, rel_bias,
           conv_w, conv_b, mh_norm_g, w_out, ln_g, ln_b):
    B, T, D = x.shape
    assert D == D_MODEL and T % L_M == 0 and T % QB == 0 and w_in.shape[0] == 1
    bias = _bias_tiles(rel_bias)
    x2d = x.reshape(B * T, D)
    w_main, w_small = _repack_w_in(w_in[0])
    main, small = _proj(x2d, w_main, w_small)
    ckv_n, kidx_n, gate_t = _prep(small, kv_norm_g[0][None], idx_k_ln_g[0][None], idx_k_ln_b[0][None])
    ya = _dsa(main, gate_t, ckv_n, kidx_n, w_uk[0].astype(BF16), w_uv[0].astype(BF16), bias, B, T)
    gb = jnp.zeros((LANES,), F32).at[S_IM:S_IM + H_M].set(b_igate[0]).at[S_FM:S_FM + H_M].set(b_fgate[0])
    ym = _mlstm(main, small, gate_t, conv_w[0], conv_b[0][None], gb[None, :], gb[:, None],
                mh_norm_g[0][None], B, T)
    out = _out(ya, ym, x2d, w_out[0].astype(BF16), ln_g[0][None], ln_b[0][None])
    return out.reshape(B, T, D)
```

```python
import functools
import math

import numpy as np
import jax
import jax.numpy as jnp
from jax import lax
from jax.experimental import pallas as pl
from jax.experimental.pallas import tpu as pltpu

F32 = jnp.float32
BF16 = jnp.bfloat16

D_MODEL = 2048
W_A = 1024
DH_A = 128
H_A = 8
D_C = 256
H_IDX = 16
D_IDX = 64
TOPK = 256
W_M = 1024
H_M = 4
DV_M = 256
DK_M = 128
CONV_W = 4
N_BUCKETS = 32
MAX_DIST = 128
ALPHA = 2.0 ** 0.25
LN_EPS = 1e-5

LANES = 128
SUBLANES = 8
VMEM_LIMIT = 56 * 1024 * 1024

QB = 128
KB = 128
KB2 = 256
CNT_ROWS = 256
L_M = 256
HALO = 16
NEG = -1e30

C_QA, C_ZA, C_QI, C_QM, C_KM, C_VM, C_OM, C_ZM = 0, 1024, 2048, 3072, 3584, 4096, 5120, 6144
N_MAIN = 7168
N_SMALL = 384
S_KI, S_WI, S_IM, S_FM = 0, 64, 80, 84


def _t5_bucket_np(rel):
    max_exact = N_BUCKETS // 2
    n = np.maximum(rel, 0)
    nf = np.maximum(n, 1).astype(np.float32)
    large = max_exact + (np.log(nf / np.float32(max_exact)) / np.float32(math.log(MAX_DIST / max_exact))
                         * np.float32(N_BUCKETS - max_exact)).astype(np.int32)
    large = np.minimum(large, N_BUCKETS - 1)
    return np.where(n < max_exact, n, large).astype(np.int32)


def _bucket_tiles():
    i = np.arange(QB)[None, :]
    j = np.arange(KB)[:, None]
    t0 = _t5_bucket_np(i - j)
    t1 = _t5_bucket_np(i - j + KB)
    t2 = _t5_bucket_np(np.full((KB, QB), 2 * KB + 1))
    assert (t5 := _t5_bucket_np(np.arange(KB + 1, 4096))).min() == t5.max() == t2[0, 0]
    return np.stack([t0, t1, t2]).astype(np.int32)


def _bias_kernel(bucket_ref, rb_ref, out_ref):
    h = pl.program_id(0)
    for k in range(3):
        bk = bucket_ref[k]
        acc = jnp.zeros((KB, QB), F32)
        for b in range(N_BUCKETS):
            acc = jnp.where(bk == b, rb_ref[b, h], acc)
        out_ref[0, k] = acc


def _bias_tiles(rel_bias):
    bucket = jnp.asarray(_bucket_tiles())
    return pl.pallas_call(
        _bias_kernel,
        grid=(H_A,),
        in_specs=[pl.BlockSpec((3, KB, QB), lambda h: (0, 0, 0)),
                  pl.BlockSpec(memory_space=pltpu.SMEM)],
        out_specs=pl.BlockSpec((1, 3, KB, QB), lambda h: (h, 0, 0, 0)),
        out_shape=jax.ShapeDtypeStruct((H_A, 3, KB, QB), F32),
        name="bias_tiles",
    )(bucket, rel_bias)


def _proj_kernel(x_ref, w_ref, ws_ref, o_ref, os_ref, xb_ref):
    @pl.when(pl.program_id(1) == 0)
    def _():
        xb_ref[...] = x_ref[...].astype(BF16)
        os_ref[...] = jnp.dot(xb_ref[...], ws_ref[...], preferred_element_type=F32)

    o_ref[...] = jnp.dot(xb_ref[...], w_ref[...], preferred_element_type=F32).astype(BF16)


def _proj(x2d, w_main, w_small, tm=1024, tn=512):
    M = x2d.shape[0]
    return pl.pallas_call(
        _proj_kernel,
        grid=(M // tm, N_MAIN // tn),
        in_specs=[pl.BlockSpec((tm, D_MODEL), lambda i, j: (i, 0)),
                  pl.BlockSpec((D_MODEL, tn), lambda i, j: (0, j)),
                  pl.BlockSpec((D_MODEL, N_SMALL), lambda i, j: (0, 0))],
        out_specs=[pl.BlockSpec((tm, tn), lambda i, j: (i, j)),
                   pl.BlockSpec((tm, N_SMALL), lambda i, j: (i, 0))],
        out_shape=[jax.ShapeDtypeStruct((M, N_MAIN), BF16),
                   jax.ShapeDtypeStruct((M, N_SMALL), F32)],
        scratch_shapes=[pltpu.VMEM((tm, D_MODEL), BF16)],
        compiler_params=pltpu.CompilerParams(
            dimension_semantics=("arbitrary", "arbitrary"), vmem_limit_bytes=VMEM_LIMIT),
        name="proj",
    )(x2d, w_main, w_small)


def _prep_kernel(s_ref, kvg_ref, ig_ref, ib_ref, ckv_ref, ckvt_ref, kidx_ref, gt_ref):
    c = s_ref[:, 0:D_C]
    c = c * lax.rsqrt(jnp.mean(c * c, axis=-1, keepdims=True) + LN_EPS) * kvg_ref[...]
    ckv_ref[...] = c.astype(BF16)
    for r in range(ckvt_ref.shape[0]):
        ckvt_ref[r] = c[r * KB2:(r + 1) * KB2, :].T.astype(BF16)
    tile = s_ref[:, D_C:D_C + LANES]
    k = tile[:, S_KI:S_KI + D_IDX]
    mu = jnp.mean(k, axis=-1, keepdims=True)
    var = jnp.mean(jnp.square(k - mu), axis=-1, keepdims=True)
    kidx_ref[...] = ((k - mu) * lax.rsqrt(var + LN_EPS) * ig_ref[...] + ib_ref[...]).astype(BF16)
    gt_ref[...] = tile.T


def _prep(small, kv_g, idx_g, idx_b, tm=1024):
    M = small.shape[0]
    return pl.pallas_call(
        _prep_kernel,
        grid=(M // tm,),
        in_specs=[pl.BlockSpec((tm, N_SMALL), lambda i: (i, 0)),
                  pl.BlockSpec((1, D_C), lambda i: (0, 0)),
                  pl.BlockSpec((1, D_IDX), lambda i: (0, 0)),
                  pl.BlockSpec((1, D_IDX), lambda i: (0, 0))],
        out_specs=[pl.BlockSpec((tm, D_C), lambda i: (i, 0)),
                   pl.BlockSpec((tm // KB2, D_C, KB2), lambda i: (i, 0, 0)),
                   pl.BlockSpec((tm, D_IDX), lambda i: (i, 0)),
                   pl.BlockSpec((LANES, tm), lambda i: (0, i))],
        out_shape=[jax.ShapeDtypeStruct((M, D_C), BF16),
                   jax.ShapeDtypeStruct((M // KB2, D_C, KB2), BF16),
                   jax.ShapeDtypeStruct((M, D_IDX), BF16),
                   jax.ShapeDtypeStruct((LANES, M), F32)],
        name="prep",
    )(small, kv_g, idx_g, idx_b)


def _key_to_float(key):
    bits = jnp.where(key < 0, key ^ jnp.int32(0x7FFFFFFF), key)
    return lax.bitcast_convert_type(bits, F32)


def _dsa_kernel(qa_ref, za_ref, qi_ref, gt_ref, ckv_ref, ckvt_ref, kidx_ref, wukt_ref, wuvt_ref, bias_ref,
                y_ref, qall_ref, qr_ref, sc_ref, mb_ref, p_ref, m_ref, l_ref, acc_ref):
    qi = pl.program_id(1)
    nkb = qi + 1
    krow = lax.broadcasted_iota(jnp.int32, (KB, QB), 0)
    qcol = lax.broadcasted_iota(jnp.int32, (KB, QB), 1)

    for h in range(H_A):
        ql = lax.dot_general(wukt_ref[h], qa_ref[:, h * DH_A:(h + 1) * DH_A], (((1,), (1,)), ((), ())),
                             preferred_element_type=F32)
        qall_ref[:, h * QB:(h + 1) * QB] = (ql * (DH_A ** -0.5)).astype(BF16)

    mb_ref[jnp.minimum(nkb, mb_ref.shape[0] - 1)] = jnp.full((KB, QB), NEG, F32)

    @pl.when(qi * QB < TOPK)
    def _():
        def fill(kb, carry):
            mb_ref[kb] = jnp.zeros((KB, QB), F32)
            return carry
        lax.fori_loop(0, qi, fill, 0)
        mb_ref[qi] = jnp.where(krow <= qcol, 0.0, NEG).astype(F32)

    @pl.when(qi * QB >= TOPK)
    def _():
        for h in range(H_IDX):
            qr_ref[h * QB:(h + 1) * QB, :] = qi_ref[:, h * D_IDX:(h + 1) * D_IDX]
        w_all = gt_ref[S_WI:S_WI + H_IDX, :] * ((D_IDX ** -0.5) * (H_IDX ** -0.5))

        def scores(kb):
            k = kidx_ref[pl.ds(pl.multiple_of(kb * KB, KB), KB), :]
            acc = jnp.zeros((KB, QB), F32)
            for h in range(H_IDX):
                s = lax.dot_general(k, qr_ref[h * QB:(h + 1) * QB, :], (((1,), (1,)), ((), ())),
                                    preferred_element_type=F32)
                acc = acc + jnp.maximum(s, 0.0) * w_all[h:h + 1, :]
            return acc

        pad_kb = jnp.minimum(nkb, sc_ref.shape[0] // KB - 1)
        sc_ref[pl.ds(pl.multiple_of(pad_kb * KB, KB), KB), :] = jnp.full((KB, QB), -jnp.inf, F32)

        def sc_body(kb, carry):
            sc_ref[pl.ds(pl.multiple_of(kb * KB, KB), KB), :] = scores(kb)
            return carry
        lax.fori_loop(0, qi, sc_body, 0)
        sc_ref[pl.ds(pl.multiple_of(qi * KB, KB), KB), :] = jnp.where(krow <= qcol, scores(qi), -jnp.inf)

        n_cnt = (nkb * KB + CNT_ROWS - 1) // CNT_ROWS

        def count_ge(cand):
            def body(c, acc):
                blk = sc_ref[pl.ds(pl.multiple_of(c * CNT_ROWS, CNT_ROWS), CNT_ROWS), :]
                ge = jnp.where(blk >= cand, 1.0, 0.0).astype(F32)
                return acc + jnp.sum(ge.reshape(CNT_ROWS // SUBLANES, SUBLANES, QB), axis=0)
            acc = lax.fori_loop(0, n_cnt, body, jnp.zeros((SUBLANES, QB), F32))
            return jnp.sum(acc, axis=0, keepdims=True)

        def bit_body(i, u):
            trial = u | lax.shift_left(jnp.int32(1), 31 - i)
            cnt = count_ge(_key_to_float(trial ^ jnp.int32(-2 ** 31)))
            return jnp.where(cnt >= float(TOPK), trial, u)
        u = lax.fori_loop(0, 32, bit_body, jnp.zeros((1, QB), jnp.int32))
        thr = _key_to_float(u ^ jnp.int32(-2 ** 31))

        def mb_body(kb, carry):
            blk = sc_ref[pl.ds(pl.multiple_of(kb * KB, KB), KB), :]
            mb_ref[kb] = jnp.where(blk >= thr, 0.0, NEG).astype(F32)
            return carry
        lax.fori_loop(0, nkb, mb_body, 0)

    m_ref[...] = jnp.full(m_ref.shape, NEG, F32)
    l_ref[...] = jnp.zeros(l_ref.shape, F32)
    acc_ref[...] = jnp.zeros(acc_ref.shape, F32)
    sub = KB2 // KB

    def att_body(c, carry):
        kv = ckv_ref[pl.ds(pl.multiple_of(c * KB2, KB2), KB2), :]
        s_all = jnp.dot(kv, qall_ref[...], preferred_element_type=F32)
        alphas = []
        for h in range(H_A):
            cols = slice(h * QB, (h + 1) * QB)
            s = []
            for j in range(sub):
                kb = c * sub + j
                tile = jnp.clip(qi - kb, 0, 2)
                s.append(s_all[j * KB:(j + 1) * KB, cols] + bias_ref[h, tile] + mb_ref[kb])
            m_prev = m_ref[:, cols]
            m_blk = functools.reduce(jnp.maximum, [jnp.max(x, axis=0, keepdims=True) for x in s])
            m_new = jnp.maximum(m_prev, m_blk)
            alpha = jnp.exp(m_prev - m_new)
            p = [jnp.exp(x - m_new) for x in s]
            l_ref[:, cols] = alpha * l_ref[:, cols] + sum(jnp.sum(x, axis=0, keepdims=True) for x in p)
            m_ref[:, cols] = m_new
            for j in range(sub):
                p_ref[j * KB:(j + 1) * KB, cols] = p[j].astype(BF16)
            alphas.append(alpha)
        alpha_row = jnp.concatenate(alphas, axis=1)
        acc_ref[...] = acc_ref[...] * alpha_row + jnp.dot(ckvt_ref[c], p_ref[...], preferred_element_type=F32)
        return carry
    lax.fori_loop(0, (nkb + sub - 1) // sub, att_body, 0)

    for h in range(H_A):
        cols = slice(h * QB, (h + 1) * QB)
        ya_t = jnp.dot(wuvt_ref[h], acc_ref[:, cols].astype(BF16), preferred_element_type=F32)
        ya = (ya_t / l_ref[:, cols]).T
        z = za_ref[:, h * DH_A:(h + 1) * DH_A].astype(F32)
        y_ref[:, h * DH_A:(h + 1) * DH_A] = (ya * (z * jax.nn.sigmoid(z))).astype(BF16)


def _dsa(main, gate_t, ckv_n, ckv_t, kidx_n, w_uk_t, w_uv_t, bias, B, T):
    nq = T // QB
    return pl.pallas_call(
        _dsa_kernel,
        grid=(B, nq),
        in_specs=[pl.BlockSpec((QB, W_A), lambda b, q: (b * nq + q, C_QA // W_A)),
                  pl.BlockSpec((QB, W_A), lambda b, q: (b * nq + q, C_ZA // W_A)),
                  pl.BlockSpec((QB, H_IDX * D_IDX), lambda b, q: (b * nq + q, C_QI // (H_IDX * D_IDX))),
                  pl.BlockSpec((LANES, QB), lambda b, q: (0, b * nq + q)),
                  pl.BlockSpec((T, D_C), lambda b, q: (b, 0)),
                  pl.BlockSpec((T // KB2, D_C, KB2), lambda b, q: (b, 0, 0)),
                  pl.BlockSpec((T, D_IDX), lambda b, q: (b, 0)),
                  pl.BlockSpec((H_A, D_C, DH_A), lambda b, q: (0, 0, 0)),
                  pl.BlockSpec((H_A, DH_A, D_C), lambda b, q: (0, 0, 0)),
                  pl.BlockSpec((H_A, 3, KB, QB), lambda b, q: (0, 0, 0, 0))],
        out_specs=pl.BlockSpec((QB, W_A), lambda b, q: (b * nq + q, 0)),
        out_shape=jax.ShapeDtypeStruct((B * T, W_A), BF16),
        scratch_shapes=[pltpu.VMEM((D_C, H_A * QB), BF16),
                        pltpu.VMEM((H_IDX * QB, D_IDX), BF16),
                        pltpu.VMEM((T, QB), F32),
                        pltpu.VMEM((T // KB, KB, QB), F32),
                        pltpu.VMEM((KB2, H_A * QB), BF16),
                        pltpu.VMEM((1, H_A * QB), F32),
                        pltpu.VMEM((1, H_A * QB), F32),
                        pltpu.VMEM((D_C, H_A * QB), F32)],
        compiler_params=pltpu.CompilerParams(
            dimension_semantics=("arbitrary", "arbitrary"), vmem_limit_bytes=VMEM_LIMIT),
        name="dsa",
    )(main, main, main, gate_t, ckv_n, ckv_t, kidx_n, w_uk_t, w_uv_t, bias)


def _split_dot(tri, x):
    hi = x.astype(BF16)
    lo = (x - hi.astype(F32)).astype(BF16)
    return jnp.dot(tri, hi, preferred_element_type=F32) + jnp.dot(tri, lo, preferred_element_type=F32)


def _log_sigmoid(x):
    return jnp.minimum(x, 0.0) - jnp.log1p(jnp.exp(-jnp.abs(x)))


def _mlstm_kernel(q_ref, k_ref, qh_ref, kh_ref, v_ref, o_ref, z_ref, g_ref, gt_ref,
                  cw_ref, cb_ref, gbr_ref, gbc_ref, ng_ref, y_ref, ct_ref, n_ref, m_ref):
    c = pl.program_id(1)
    L = L_M

    @pl.when(c == 0)
    def _():
        ct_ref[...] = jnp.zeros(ct_ref.shape, F32)
        n_ref[...] = jnp.zeros(n_ref.shape, F32)
        m_ref[...] = jnp.zeros(m_ref.shape, F32)

    def conv_silu(x_ref, halo_ref, lo):
        halo = jnp.where(c > 0, halo_ref[...].astype(F32), 0.0)
        xe = jnp.concatenate([halo, x_ref[...].astype(F32)], axis=0)
        y = cb_ref[:, lo:lo + H_M * DK_M]
        for j in range(CONV_W):
            off = HALO - (CONV_W - 1) + j
            y = y + cw_ref[j:j + 1, lo:lo + H_M * DK_M] * xe[off:off + L]
        return y * jax.nn.sigmoid(y)

    q_all = conv_silu(q_ref, qh_ref, 0)
    k_all = conv_silu(k_ref, kh_ref, H_M * DK_M) * (DK_M ** -0.5)

    gc = g_ref[...] + gbr_ref[...]
    gr = gt_ref[S_IM:S_IM + 2 * H_M, :] + gbc_ref[S_IM:S_IM + 2 * H_M, :]
    r = lax.broadcasted_iota(jnp.int32, (L, L), 0)
    s = lax.broadcasted_iota(jnp.int32, (L, L), 1)
    causal = s <= r
    tri_l = jnp.where(causal, 1.0, 0.0).astype(BF16)
    tri_u = jnp.where(r <= s, 1.0, 0.0).astype(BF16)
    b_cols = _split_dot(tri_l, _log_sigmoid(gc))
    b_rows = jnp.dot(_log_sigmoid(gr).astype(BF16), tri_u, preferred_element_type=F32) \
        + jnp.dot((_log_sigmoid(gr) - _log_sigmoid(gr).astype(BF16).astype(F32)).astype(BF16), tri_u,
                  preferred_element_type=F32)

    for h in range(H_M):
        q = q_all[:, h * DK_M:(h + 1) * DK_M]
        k = k_all[:, h * DK_M:(h + 1) * DK_M]
        v = v_ref[:, h * DV_M:(h + 1) * DV_M]
        qb = q.astype(BF16)
        b_c = b_cols[:, S_FM + h:S_FM + h + 1]
        i_c = gc[:, S_IM + h:S_IM + h + 1]
        b_r = b_rows[H_M + h:H_M + h + 1, :]
        i_r = gr[h:h + 1, :]
        m_prev = m_ref[h]
        ct = ct_ref[h]
        n = n_ref[h]

        log_d = jnp.where(causal, b_c - b_r + i_r, -jnp.inf)
        g = b_c + m_prev
        m_t = jnp.maximum(jnp.max(log_d, axis=-1, keepdims=True), g)
        qk = lax.dot_general(qb, k.astype(BF16), (((1,), (1,)), ((), ())), preferred_element_type=F32)
        s_mat = qk * jnp.exp(log_d - m_t)
        inter = jnp.exp(g - m_t)
        num = jnp.dot(s_mat.astype(BF16), v, preferred_element_type=F32) \
            + inter * jnp.dot(qb, ct.astype(BF16), preferred_element_type=F32)
        den = jnp.sum(s_mat, axis=-1, keepdims=True) + inter * jnp.sum(q * n, axis=-1, keepdims=True)
        hh = num / jnp.maximum(jnp.abs(den), jnp.exp(-m_t))

        b_last = b_c[L - 1:L, :]
        a_r = b_last - b_r + i_r
        m_new = jnp.maximum(b_last + m_prev, jnp.max(a_r, axis=-1, keepdims=True))
        decay = jnp.exp(b_last + m_prev - m_new)
        wgt_c = jnp.exp(b_last - b_c + i_c - m_new)
        kw = k * wgt_c
        ct_ref[h] = decay * ct + jnp.dot(kw.T.astype(BF16), v, preferred_element_type=F32)
        n_ref[h] = decay * n + jnp.sum(kw, axis=0, keepdims=True)
        m_ref[h] = m_new

        mu = jnp.mean(hh, axis=-1, keepdims=True)
        var = jnp.mean(jnp.square(hh - mu), axis=-1, keepdims=True)
        hn = (hh - mu) * lax.rsqrt(var + LN_EPS) * ng_ref[:, h * DV_M:(h + 1) * DV_M]
        og = o_ref[:, h * DV_M:(h + 1) * DV_M].astype(F32)
        zg = z_ref[:, h * DV_M:(h + 1) * DV_M].astype(F32)
        y_ref[:, h * DV_M:(h + 1) * DV_M] = (hn * jax.nn.sigmoid(og) * (zg * jax.nn.sigmoid(zg))).astype(BF16)


def _mlstm(main, small, gate_t, conv_w, conv_b, gb_row, gb_col, norm_g, B, T):
    nc = T // L_M
    hb = L_M // HALO
    qk_w = H_M * DK_M

    def halo_map(col):
        return lambda b, c: (jnp.maximum((b * nc + c) * hb - 1, 0), col)

    return pl.pallas_call(
        _mlstm_kernel,
        grid=(B, nc),
        in_specs=[pl.BlockSpec((L_M, qk_w), lambda b, c: (b * nc + c, C_QM // qk_w)),
                  pl.BlockSpec((L_M, qk_w), lambda b, c: (b * nc + c, C_KM // qk_w)),
                  pl.BlockSpec((HALO, qk_w), halo_map(C_QM // qk_w)),
                  pl.BlockSpec((HALO, qk_w), halo_map(C_KM // qk_w)),
                  pl.BlockSpec((L_M, W_M), lambda b, c: (b * nc + c, C_VM // W_M)),
                  pl.BlockSpec((L_M, W_M), lambda b, c: (b * nc + c, C_OM // W_M)),
                  pl.BlockSpec((L_M, W_M), lambda b, c: (b * nc + c, C_ZM // W_M)),
                  pl.BlockSpec((L_M, LANES), lambda b, c: (b * nc + c, D_C // LANES)),
                  pl.BlockSpec((LANES, L_M), lambda b, c: (0, b * nc + c)),
                  pl.BlockSpec((CONV_W, 2 * qk_w), lambda b, c: (0, 0)),
                  pl.BlockSpec((1, 2 * qk_w), lambda b, c: (0, 0)),
                  pl.BlockSpec((1, LANES), lambda b, c: (0, 0)),
                  pl.BlockSpec((LANES, 1), lambda b, c: (0, 0)),
                  pl.BlockSpec((1, W_M), lambda b, c: (0, 0))],
        out_specs=pl.BlockSpec((L_M, W_M), lambda b, c: (b * nc + c, 0)),
        out_shape=jax.ShapeDtypeStruct((B * T, W_M), BF16),
        scratch_shapes=[pltpu.VMEM((H_M, DK_M, DV_M), F32),
                        pltpu.VMEM((H_M, 1, DK_M), F32),
                        pltpu.VMEM((H_M, 1, 1), F32)],
        compiler_params=pltpu.CompilerParams(
            dimension_semantics=("arbitrary", "arbitrary"), vmem_limit_bytes=VMEM_LIMIT),
        name="mlstm",
    )(main, main, main, main, main, main, main, small, gate_t,
      conv_w, conv_b, gb_row, gb_col, norm_g)


def _out_kernel(ya_ref, ym_ref, x_ref, w_ref, g_ref, b_ref, o_ref):
    y = jnp.dot(ya_ref[...], w_ref[0:W_A, :], preferred_element_type=F32)
    y = y + jnp.dot(ym_ref[...], w_ref[W_A:W_A + W_M, :], preferred_element_type=F32)
    r = ALPHA * x_ref[...] + y
    mu = jnp.mean(r, axis=-1, keepdims=True)
    var = jnp.mean(jnp.square(r - mu), axis=-1, keepdims=True)
    o_ref[...] = (r - mu) * lax.rsqrt(var + LN_EPS) * g_ref[...] + b_ref[...]


def _out(ya, ym, x2d, w_out, ln_g, ln_b, tm=256):
    M = x2d.shape[0]
    return pl.pallas_call(
        _out_kernel,
        grid=(M // tm,),
        in_specs=[pl.BlockSpec((tm, W_A), lambda i: (i, 0)),
                  pl.BlockSpec((tm, W_M), lambda i: (i, 0)),
                  pl.BlockSpec((tm, D_MODEL), lambda i: (i, 0)),
                  pl.BlockSpec((W_A + W_M, D_MODEL), lambda i: (0, 0)),
                  pl.BlockSpec((1, D_MODEL), lambda i: (0, 0)),
                  pl.BlockSpec((1, D_MODEL), lambda i: (0, 0))],
        out_specs=pl.BlockSpec((tm, D_MODEL), lambda i: (i, 0)),
        out_shape=jax.ShapeDtypeStruct((M, D_MODEL), F32),
        compiler_params=pltpu.CompilerParams(
            dimension_semantics=("arbitrary",), vmem_limit_bytes=VMEM_LIMIT),
        name="out",
    )(ya, ym, x2d, w_out, ln_g, ln_b)


def _repack_w_in(w):
    offs = np.cumsum([0, W_A, D_C, W_A, H_IDX * D_IDX, D_IDX, H_IDX, H_M * DK_M, H_M * DK_M,
                      W_M, H_M, H_M, W_M, W_M])
    seg = [w[:, offs[i]:offs[i + 1]] for i in range(13)]
    (q_a, c_kv, z_a, q_i, k_i, w_i, q_m, k_m, v_m, i_m, f_m, o_m, z_m) = seg
    main = jnp.concatenate([q_a, z_a, q_i, q_m, k_m, v_m, o_m, z_m], axis=1).astype(BF16)
    pad = jnp.zeros((w.shape[0], N_SMALL - D_C - D_IDX - H_IDX - 2 * H_M), w.dtype)
    small = jnp.concatenate([c_kv, k_i, w_i, i_m, f_m, pad], axis=1).astype(BF16)
    return main, small


def kernel(x, w_in, b_igate, b_fgate, kv_norm_g, w_uk, w_uv, idx_k_ln_g, idx_k_ln_b, rel_bias,
           conv_w, conv_b, mh_norm_g, w_out, ln_g, ln_b):
    B, T, D = x.shape
    assert D == D_MODEL and T % L_M == 0 and T % QB == 0 and w_in.shape[0] == 1
    bias = _bias_tiles(rel_bias)
    x2d = x.reshape(B * T, D)
    w_main, w_small = _repack_w_in(w_in[0])
    main, small = _proj(x2d, w_main, w_small)
    ckv_n, ckv_t, kidx_n, gate_t = _prep(small, kv_norm_g[0][None], idx_k_ln_g[0][None], idx_k_ln_b[0][None])
    w_uk_t = jnp.transpose(w_uk[0], (0, 2, 1)).astype(BF16)
    w_uv_t = jnp.transpose(w_uv[0], (0, 2, 1)).astype(BF16)
    ya = _dsa(main, gate_t, ckv_n, ckv_t, kidx_n, w_uk_t, w_uv_t, bias, B, T)
    gb = jnp.zeros((LANES,), F32).at[S_IM:S_IM + H_M].set(b_igate[0]).at[S_FM:S_FM + H_M].set(b_fgate[0])
    ym = _mlstm(main, small, gate_t, conv_w[0], conv_b[0][None], gb[None, :], gb[:, None],
                mh_norm_g[0][None], B, T)
    out = _out(ya, ym, x2d, w_out[0].astype(BF16), ln_g[0][None], ln_b[0][None])
    return out.reshape(B, T, D)
```

```python
import functools
import math

import numpy as np
import jax
import jax.numpy as jnp
from jax import lax
from jax.experimental import pallas as pl
from jax.experimental.pallas import tpu as pltpu

F32 = jnp.float32
BF16 = jnp.bfloat16

D_MODEL = 2048
W_A = 1024
DH_A = 128
H_A = 8
D_C = 256
H_IDX = 16
D_IDX = 64
TOPK = 256
W_M = 1024
H_M = 4
DV_M = 256
DK_M = 128
CONV_W = 4
N_BUCKETS = 32
MAX_DIST = 128
ALPHA = 2.0 ** 0.25
LN_EPS = 1e-5

LANES = 128
SUBLANES = 8
VMEM_LIMIT = 56 * 1024 * 1024

QB = 128
KB = 128
KB2 = 256
CNT_ROWS = 512
CNT_ACC = 4 * SUBLANES
L_M = 256
HALO = 16
NEG = -1e30
LOG2E = math.log2(math.e)
ONES_ROWS = 16

C_QA, C_ZA, C_QI, C_QM, C_KM, C_VM, C_OM, C_ZM = 0, 1024, 2048, 3072, 3584, 4096, 5120, 6144
N_MAIN = 7168
N_SMALL = 384
S_KI, S_WI, S_IM, S_FM = 0, 64, 80, 84


def _t5_bucket_np(rel):
    max_exact = N_BUCKETS // 2
    n = np.maximum(rel, 0)
    nf = np.maximum(n, 1).astype(np.float32)
    large = max_exact + (np.log(nf / np.float32(max_exact)) / np.float32(math.log(MAX_DIST / max_exact))
                         * np.float32(N_BUCKETS - max_exact)).astype(np.int32)
    large = np.minimum(large, N_BUCKETS - 1)
    return np.where(n < max_exact, n, large).astype(np.int32)


def _bucket_tiles():
    i = np.arange(QB)[None, :]
    j = np.arange(KB)[:, None]
    t0 = _t5_bucket_np(i - j)
    t1 = _t5_bucket_np(i - j + KB)
    t2 = _t5_bucket_np(np.full((KB, QB), 2 * KB + 1))
    assert (t5 := _t5_bucket_np(np.arange(KB + 1, 4096))).min() == t5.max() == t2[0, 0] == FAR_BUCKET
    return np.stack([t0, t1, t2]).astype(np.int32)


FAR_BUCKET = int(_t5_bucket_np(np.array(2 * KB + 1)))


def _bias_kernel(bucket_ref, rb_ref, out_ref):
    h = pl.program_id(0)
    for k in range(3):
        bk = bucket_ref[k]
        acc = jnp.zeros((KB, QB), F32)
        for b in range(N_BUCKETS):
            acc = jnp.where(bk == b, rb_ref[b, h], acc)
        out_ref[0, k] = (acc - rb_ref[FAR_BUCKET, h]) * LOG2E


def _bias_tiles(rel_bias):
    bucket = jnp.asarray(_bucket_tiles())
    return pl.pallas_call(
        _bias_kernel,
        grid=(H_A,),
        in_specs=[pl.BlockSpec((3, KB, QB), lambda h: (0, 0, 0)),
                  pl.BlockSpec(memory_space=pltpu.SMEM)],
        out_specs=pl.BlockSpec((1, 3, KB, QB), lambda h: (h, 0, 0, 0)),
        out_shape=jax.ShapeDtypeStruct((H_A, 3, KB, QB), F32),
        name="bias_tiles",
    )(bucket, rel_bias)


def _proj_kernel(x_ref, w_ref, ws_ref, o_ref, os_ref, xb_ref):
    @pl.when(pl.program_id(1) == 0)
    def _():
        xb_ref[...] = x_ref[...].astype(BF16)
        os_ref[...] = jnp.dot(xb_ref[...], ws_ref[...], preferred_element_type=F32)

    o_ref[...] = jnp.dot(xb_ref[...], w_ref[...], preferred_element_type=F32).astype(BF16)


def _proj(x2d, w_main, w_small, tm=1024, tn=512):
    M = x2d.shape[0]
    return pl.pallas_call(
        _proj_kernel,
        grid=(M // tm, N_MAIN // tn),
        in_specs=[pl.BlockSpec((tm, D_MODEL), lambda i, j: (i, 0)),
                  pl.BlockSpec((D_MODEL, tn), lambda i, j: (0, j)),
                  pl.BlockSpec((D_MODEL, N_SMALL), lambda i, j: (0, 0))],
        out_specs=[pl.BlockSpec((tm, tn), lambda i, j: (i, j)),
                   pl.BlockSpec((tm, N_SMALL), lambda i, j: (i, 0))],
        out_shape=[jax.ShapeDtypeStruct((M, N_MAIN), BF16),
                   jax.ShapeDtypeStruct((M, N_SMALL), F32)],
        scratch_shapes=[pltpu.VMEM((tm, D_MODEL), BF16)],
        compiler_params=pltpu.CompilerParams(
            dimension_semantics=("arbitrary", "arbitrary"), vmem_limit_bytes=VMEM_LIMIT),
        name="proj",
    )(x2d, w_main, w_small)


def _prep_kernel(s_ref, kvg_ref, ig_ref, ib_ref, ckv_ref, ckvt_ref, kidx_ref, gt_ref):
    c = s_ref[:, 0:D_C]
    c = c * lax.rsqrt(jnp.mean(c * c, axis=-1, keepdims=True) + LN_EPS) * kvg_ref[...]
    ckv_ref[...] = c.astype(BF16)
    for r in range(ckvt_ref.shape[0]):
        ckvt_ref[r, 0:D_C, :] = c[r * KB2:(r + 1) * KB2, :].T.astype(BF16)
        ckvt_ref[r, D_C:D_C + ONES_ROWS, :] = jnp.ones((ONES_ROWS, KB2), BF16)
    tile = s_ref[:, D_C:D_C + LANES]
    k = tile[:, S_KI:S_KI + D_IDX]
    mu = jnp.mean(k, axis=-1, keepdims=True)
    var = jnp.mean(jnp.square(k - mu), axis=-1, keepdims=True)
    kidx_ref[...] = ((k - mu) * lax.rsqrt(var + LN_EPS) * ig_ref[...] + ib_ref[...]).astype(BF16)
    gt_ref[...] = tile.T


def _prep(small, kv_g, idx_g, idx_b, tm=1024):
    M = small.shape[0]
    return pl.pallas_call(
        _prep_kernel,
        grid=(M // tm,),
        in_specs=[pl.BlockSpec((tm, N_SMALL), lambda i: (i, 0)),
                  pl.BlockSpec((1, D_C), lambda i: (0, 0)),
                  pl.BlockSpec((1, D_IDX), lambda i: (0, 0)),
                  pl.BlockSpec((1, D_IDX), lambda i: (0, 0))],
        out_specs=[pl.BlockSpec((tm, D_C), lambda i: (i, 0)),
                   pl.BlockSpec((tm // KB2, D_C + ONES_ROWS, KB2), lambda i: (i, 0, 0)),
                   pl.BlockSpec((tm, D_IDX), lambda i: (i, 0)),
                   pl.BlockSpec((LANES, tm), lambda i: (0, i))],
        out_shape=[jax.ShapeDtypeStruct((M, D_C), BF16),
                   jax.ShapeDtypeStruct((M // KB2, D_C + ONES_ROWS, KB2), BF16),
                   jax.ShapeDtypeStruct((M, D_IDX), BF16),
                   jax.ShapeDtypeStruct((LANES, M), F32)],
        name="prep",
    )(small, kv_g, idx_g, idx_b)


def _key_to_float(key):
    bits = jnp.where(key < 0, key ^ jnp.int32(0x7FFFFFFF), key)
    return lax.bitcast_convert_type(bits, F32)


def _dsa_kernel(qa_ref, za_ref, qi_ref, gt_ref, ckv_ref, ckvt_ref, kidx_ref, wukt_ref, wuvt_ref, bias_ref,
                y_ref, qall_ref, qr_ref, ha_ref, hb_ref, sc_ref, mb_ref, p_ref, m_ref, acc_ref):
    qi = pl.program_id(1)
    nkb = qi + 1
    krow = lax.broadcasted_iota(jnp.int32, (KB, QB), 0)
    qcol = lax.broadcasted_iota(jnp.int32, (KB, QB), 1)

    for h in range(H_A):
        ql = lax.dot_general(wukt_ref[h], qa_ref[:, h * DH_A:(h + 1) * DH_A], (((1,), (1,)), ((), ())),
                             preferred_element_type=F32)
        qall_ref[:, h * QB:(h + 1) * QB] = (ql * (DH_A ** -0.5 * LOG2E)).astype(BF16)

    mb_ref[jnp.minimum(nkb, mb_ref.shape[0] - 1)] = jnp.full((KB, QB), NEG, F32)

    @pl.when(qi * QB < TOPK)
    def _():
        def fill(kb, carry):
            mb_ref[kb] = jnp.zeros((KB, QB), F32)
            return carry
        lax.fori_loop(0, qi, fill, 0)
        mb_ref[qi] = jnp.where(krow <= qcol, 0.0, NEG).astype(F32)

    @pl.when(qi * QB >= TOPK)
    def _():
        for h in range(H_IDX):
            qr_ref[h * QB:(h + 1) * QB, :] = qi_ref[:, h * D_IDX:(h + 1) * D_IDX]
        w_all = gt_ref[S_WI:S_WI + H_IDX, :] * ((D_IDX ** -0.5) * (H_IDX ** -0.5))

        last_kb = sc_ref.shape[0] // KB - 1

        def head_dots(kb, dst_ref):
            k = kidx_ref[pl.ds(pl.multiple_of(jnp.minimum(kb, last_kb) * KB, KB), KB), :]
            dst_ref[...] = lax.dot_general(k, qr_ref[...], (((1,), (1,)), ((), ())),
                                           preferred_element_type=F32)

        def reduce_heads(src_ref, kb):
            acc = jnp.zeros((KB, QB), F32)
            for h in range(H_IDX):
                acc = acc + jnp.maximum(src_ref[:, h * QB:(h + 1) * QB], 0.0) * w_all[h:h + 1, :]
            sc_ref[pl.ds(pl.multiple_of(kb * KB, KB), KB), :] = jnp.where(
                krow + (kb - qi) * KB <= qcol, acc, -jnp.inf)

        for j in range(CNT_ROWS // KB - 1):
            pad_kb = jnp.minimum(nkb + j, last_kb)
            sc_ref[pl.ds(pl.multiple_of(pad_kb * KB, KB), KB), :] = jnp.full((KB, QB), -jnp.inf, F32)

        head_dots(0, ha_ref)

        def sc_body(j, carry):
            head_dots(2 * j + 1, hb_ref)
            reduce_heads(ha_ref, 2 * j)
            head_dots(2 * j + 2, ha_ref)
            reduce_heads(hb_ref, 2 * j + 1)
            return carry
        lax.fori_loop(0, (nkb + 1) // 2, sc_body, 0)

        n_cnt = (nkb * KB + CNT_ROWS - 1) // CNT_ROWS

        def count_ge(cand):
            def body(c, acc):
                blk = sc_ref[pl.ds(pl.multiple_of(c * CNT_ROWS, CNT_ROWS), CNT_ROWS), :]
                ge = jnp.where(blk >= cand, 1.0, 0.0).astype(F32)
                return acc + jnp.sum(ge.reshape(CNT_ROWS // CNT_ACC, CNT_ACC, QB), axis=0)
            acc = lax.fori_loop(0, n_cnt, body, jnp.zeros((CNT_ACC, QB), F32))
            return jnp.sum(acc, axis=0, keepdims=True)

        def bit_body(i, u):
            trial = u | lax.shift_left(jnp.int32(1), 31 - i)
            cnt = count_ge(_key_to_float(trial ^ jnp.int32(-2 ** 31)))
            return jnp.where(cnt >= float(TOPK), trial, u)
        u = lax.fori_loop(0, 32, bit_body, jnp.zeros((1, QB), jnp.int32))
        thr = _key_to_float(u ^ jnp.int32(-2 ** 31))

        def mb_body(kb, carry):
            blk = sc_ref[pl.ds(pl.multiple_of(kb * KB, KB), KB), :]
            mb_ref[kb] = jnp.where(blk >= thr, 0.0, NEG).astype(F32)
            return carry
        lax.fori_loop(0, nkb, mb_body, 0)

    m_ref[...] = jnp.full(m_ref.shape, NEG, F32)
    acc_ref[...] = jnp.zeros(acc_ref.shape, F32)
    sub = KB2 // KB

    def att_body(c, carry, with_bias):
        kv = ckv_ref[pl.ds(pl.multiple_of(c * KB2, KB2), KB2), :]
        s_all = jnp.dot(kv, qall_ref[...], preferred_element_type=F32)
        alphas = []
        for h in range(H_A):
            cols = slice(h * QB, (h + 1) * QB)
            s = []
            for j in range(sub):
                kb = c * sub + j
                x = s_all[j * KB:(j + 1) * KB, cols] + mb_ref[kb]
                if with_bias:
                    x = x + bias_ref[h, jnp.clip(qi - kb, 0, 2)]
                s.append(x)
            m_prev = m_ref[:, cols]
            m_blk = functools.reduce(jnp.maximum, [jnp.max(x, axis=0, keepdims=True) for x in s])
            m_new = jnp.maximum(m_prev, m_blk)
            alphas.append(jnp.exp2(m_prev - m_new))
            m_ref[:, cols] = m_new
            for j in range(sub):
                p_ref[j * KB:(j + 1) * KB, cols] = jnp.exp2(s[j] - m_new).astype(BF16)
        alpha_row = jnp.concatenate(alphas, axis=1)
        acc_ref[...] = acc_ref[...] * alpha_row + jnp.dot(ckvt_ref[c], p_ref[...], preferred_element_type=F32)
        return carry

    n_all = (nkb + sub - 1) // sub
    n_far = jnp.maximum(qi - 1, 0) // sub
    lax.fori_loop(0, n_far, functools.partial(att_body, with_bias=False), 0)
    lax.fori_loop(n_far, n_all, functools.partial(att_body, with_bias=True), 0)

    for h in range(H_A):
        cols = slice(h * QB, (h + 1) * QB)
        ya_t = jnp.dot(wuvt_ref[h], acc_ref[0:D_C, cols].astype(BF16), preferred_element_type=F32)
        ya = (ya_t / acc_ref[D_C:D_C + 1, cols]).T
        z = za_ref[:, h * DH_A:(h + 1) * DH_A].astype(F32)
        y_ref[:, h * DH_A:(h + 1) * DH_A] = (ya * (z * jax.nn.sigmoid(z))).astype(BF16)


def _dsa(main, gate_t, ckv_n, ckv_t, kidx_n, w_uk_t, w_uv_t, bias, B, T):
    nq = T // QB
    return pl.pallas_call(
        _dsa_kernel,
        grid=(B, nq),
        in_specs=[pl.BlockSpec((QB, W_A), lambda b, q: (b * nq + q, C_QA // W_A)),
                  pl.BlockSpec((QB, W_A), lambda b, q: (b * nq + q, C_ZA // W_A)),
                  pl.BlockSpec((QB, H_IDX * D_IDX), lambda b, q: (b * nq + q, C_QI // (H_IDX * D_IDX))),
                  pl.BlockSpec((LANES, QB), lambda b, q: (0, b * nq + q)),
                  pl.BlockSpec((T, D_C), lambda b, q: (b, 0)),
                  pl.BlockSpec((T // KB2, D_C + ONES_ROWS, KB2), lambda b, q: (b, 0, 0)),
                  pl.BlockSpec((T, D_IDX), lambda b, q: (b, 0)),
                  pl.BlockSpec((H_A, D_C, DH_A), lambda b, q: (0, 0, 0)),
                  pl.BlockSpec((H_A, DH_A, D_C), lambda b, q: (0, 0, 0)),
                  pl.BlockSpec((H_A, 3, KB, QB), lambda b, q: (0, 0, 0, 0))],
        out_specs=pl.BlockSpec((QB, W_A), lambda b, q: (b * nq + q, 0)),
        out_shape=jax.ShapeDtypeStruct((B * T, W_A), BF16),
        scratch_shapes=[pltpu.VMEM((D_C, H_A * QB), BF16),
                        pltpu.VMEM((H_IDX * QB, D_IDX), BF16),
                        pltpu.VMEM((KB, H_IDX * QB), F32),
                        pltpu.VMEM((KB, H_IDX * QB), F32),
                        pltpu.VMEM((T, QB), F32),
                        pltpu.VMEM((T // KB, KB, QB), F32),
                        pltpu.VMEM((KB2, H_A * QB), BF16),
                        pltpu.VMEM((1, H_A * QB), F32),
                        pltpu.VMEM((D_C + ONES_ROWS, H_A * QB), F32)],
        compiler_params=pltpu.CompilerParams(
            dimension_semantics=("arbitrary", "arbitrary"), vmem_limit_bytes=VMEM_LIMIT),
        name="dsa",
    )(main, main, main, gate_t, ckv_n, ckv_t, kidx_n, w_uk_t, w_uv_t, bias)


def _split_dot(tri, x):
    hi = x.astype(BF16)
    lo = (x - hi.astype(F32)).astype(BF16)
    return jnp.dot(tri, hi, preferred_element_type=F32) + jnp.dot(tri, lo, preferred_element_type=F32)


def _log_sigmoid(x):
    return jnp.minimum(x, 0.0) - jnp.log1p(jnp.exp(-jnp.abs(x)))


def _mlstm_kernel(q_ref, k_ref, qh_ref, kh_ref, v_ref, o_ref, z_ref, g_ref, gt_ref,
                  cw_ref, cb_ref, gbr_ref, gbc_ref, ng_ref, y_ref, ct_ref, n_ref, m_ref):
    c = pl.program_id(1)
    L = L_M

    @pl.when(c == 0)
    def _():
        ct_ref[...] = jnp.zeros(ct_ref.shape, F32)
        n_ref[...] = jnp.zeros(n_ref.shape, F32)
        m_ref[...] = jnp.zeros(m_ref.shape, F32)

    def conv_silu(x_ref, halo_ref, lo):
        halo = jnp.where(c > 0, halo_ref[...].astype(F32), 0.0)
        xe = jnp.concatenate([halo, x_ref[...].astype(F32)], axis=0)
        y = cb_ref[:, lo:lo + H_M * DK_M]
        for j in range(CONV_W):
            off = HALO - (CONV_W - 1) + j
            y = y + cw_ref[j:j + 1, lo:lo + H_M * DK_M] * xe[off:off + L]
        return y * jax.nn.sigmoid(y)

    q_all = conv_silu(q_ref, qh_ref, 0)
    k_all = conv_silu(k_ref, kh_ref, H_M * DK_M) * (DK_M ** -0.5)

    gc = g_ref[...] + gbr_ref[...]
    gr = gt_ref[S_IM:S_IM + 2 * H_M, :] + gbc_ref[S_IM:S_IM + 2 * H_M, :]
    r = lax.broadcasted_iota(jnp.int32, (L, L), 0)
    s = lax.broadcasted_iota(jnp.int32, (L, L), 1)
    causal = s <= r
    tri_l = jnp.where(causal, 1.0, 0.0).astype(BF16)
    tri_u = jnp.where(r <= s, 1.0, 0.0).astype(BF16)
    b_cols = _split_dot(tri_l, _log_sigmoid(gc))
    b_rows = jnp.dot(_log_sigmoid(gr).astype(BF16), tri_u, preferred_element_type=F32) \
        + jnp.dot((_log_sigmoid(gr) - _log_sigmoid(gr).astype(BF16).astype(F32)).astype(BF16), tri_u,
                  preferred_element_type=F32)

    for h in range(H_M):
        q = q_all[:, h * DK_M:(h + 1) * DK_M]
        k = k_all[:, h * DK_M:(h + 1) * DK_M]
        v = v_ref[:, h * DV_M:(h + 1) * DV_M]
        qb = q.astype(BF16)
        b_c = b_cols[:, S_FM + h:S_FM + h + 1]
        i_c = gc[:, S_IM + h:S_IM + h + 1]
        b_r = b_rows[H_M + h:H_M + h + 1, :]
        i_r = gr[h:h + 1, :]
        m_prev = m_ref[h]
        ct = ct_ref[h]
        n = n_ref[h]

        log_d = jnp.where(causal, b_c - b_r + i_r, -jnp.inf)
        g = b_c + m_prev
        m_t = jnp.maximum(jnp.max(log_d, axis=-1, keepdims=True), g)
        qk = lax.dot_general(qb, k.astype(BF16), (((1,), (1,)), ((), ())), preferred_element_type=F32)
        s_mat = qk * jnp.exp(log_d - m_t)
        inter = jnp.exp(g - m_t)
        num = jnp.dot(s_mat.astype(BF16), v, preferred_element_type=F32) \
            + inter * jnp.dot(qb, ct.astype(BF16), preferred_element_type=F32)
        den = jnp.sum(s_mat, axis=-1, keepdims=True) + inter * jnp.sum(q * n, axis=-1, keepdims=True)
        hh = num / jnp.maximum(jnp.abs(den), jnp.exp(-m_t))

        b_last = b_c[L - 1:L, :]
        a_r = b_last - b_r + i_r
        m_new = jnp.maximum(b_last + m_prev, jnp.max(a_r, axis=-1, keepdims=True))
        decay = jnp.exp(b_last + m_prev - m_new)
        wgt_c = jnp.exp(b_last - b_c + i_c - m_new)
        kw = k * wgt_c
        ct_ref[h] = decay * ct + jnp.dot(kw.T.astype(BF16), v, preferred_element_type=F32)
        n_ref[h] = decay * n + jnp.sum(kw, axis=0, keepdims=True)
        m_ref[h] = m_new

        mu = jnp.mean(hh, axis=-1, keepdims=True)
        var = jnp.mean(jnp.square(hh - mu), axis=-1, keepdims=True)
        hn = (hh - mu) * lax.rsqrt(var + LN_EPS) * ng_ref[:, h * DV_M:(h + 1) * DV_M]
        og = o_ref[:, h * DV_M:(h + 1) * DV_M].astype(F32)
        zg = z_ref[:, h * DV_M:(h + 1) * DV_M].astype(F32)
        y_ref[:, h * DV_M:(h + 1) * DV_M] = (hn * jax.nn.sigmoid(og) * (zg * jax.nn.sigmoid(zg))).astype(BF16)


def _mlstm(main, small, gate_t, conv_w, conv_b, gb_row, gb_col, norm_g, B, T):
    nc = T // L_M
    hb = L_M // HALO
    qk_w = H_M * DK_M

    def halo_map(col):
        return lambda b, c: (jnp.maximum((b * nc + c) * hb - 1, 0), col)

    return pl.pallas_call(
        _mlstm_kernel,
        grid=(B, nc),
        in_specs=[pl.BlockSpec((L_M, qk_w), lambda b, c: (b * nc + c, C_QM // qk_w)),
                  pl.BlockSpec((L_M, qk_w), lambda b, c: (b * nc + c, C_KM // qk_w)),
                  pl.BlockSpec((HALO, qk_w), halo_map(C_QM // qk_w)),
                  pl.BlockSpec((HALO, qk_w), halo_map(C_KM // qk_w)),
                  pl.BlockSpec((L_M, W_M), lambda b, c: (b * nc + c, C_VM // W_M)),
                  pl.BlockSpec((L_M, W_M), lambda b, c: (b * nc + c, C_OM // W_M)),
                  pl.BlockSpec((L_M, W_M), lambda b, c: (b * nc + c, C_ZM // W_M)),
                  pl.BlockSpec((L_M, LANES), lambda b, c: (b * nc + c, D_C // LANES)),
                  pl.BlockSpec((LANES, L_M), lambda b, c: (0, b * nc + c)),
                  pl.BlockSpec((CONV_W, 2 * qk_w), lambda b, c: (0, 0)),
                  pl.BlockSpec((1, 2 * qk_w), lambda b, c: (0, 0)),
                  pl.BlockSpec((1, LANES), lambda b, c: (0, 0)),
                  pl.BlockSpec((LANES, 1), lambda b, c: (0, 0)),
                  pl.BlockSpec((1, W_M), lambda b, c: (0, 0))],
        out_specs=pl.BlockSpec((L_M, W_M), lambda b, c: (b * nc + c, 0)),
        out_shape=jax.ShapeDtypeStruct((B * T, W_M), BF16),
        scratch_shapes=[pltpu.VMEM((H_M, DK_M, DV_M), F32),
                        pltpu.VMEM((H_M, 1, DK_M), F32),
                        pltpu.VMEM((H_M, 1, 1), F32)],
        compiler_params=pltpu.CompilerParams(
            dimension_semantics=("arbitrary", "arbitrary"), vmem_limit_bytes=VMEM_LIMIT),
        name="mlstm",
    )(main, main, main, main, main, main, main, small, gate_t,
      conv_w, conv_b, gb_row, gb_col, norm_g)


def _out_kernel(ya_ref, ym_ref, x_ref, w_ref, g_ref, b_ref, o_ref):
    y = jnp.dot(ya_ref[...], w_ref[0:W_A, :], preferred_element_type=F32)
    y = y + jnp.dot(ym_ref[...], w_ref[W_A:W_A + W_M, :], preferred_element_type=F32)
    r = ALPHA * x_ref[...] + y
    mu = jnp.mean(r, axis=-1, keepdims=True)
    var = jnp.mean(jnp.square(r - mu), axis=-1, keepdims=True)
    o_ref[...] = (r - mu) * lax.rsqrt(var + LN_EPS) * g_ref[...] + b_ref[...]


def _out(ya, ym, x2d, w_out, ln_g, ln_b, tm=256):
    M = x2d.shape[0]
    return pl.pallas_call(
        _out_kernel,
        grid=(M // tm,),
        in_specs=[pl.BlockSpec((tm, W_A), lambda i: (i, 0)),
                  pl.BlockSpec((tm, W_M), lambda i: (i, 0)),
                  pl.BlockSpec((tm, D_MODEL), lambda i: (i, 0)),
                  pl.BlockSpec((W_A + W_M, D_MODEL), lambda i: (0, 0)),
                  pl.BlockSpec((1, D_MODEL), lambda i: (0, 0)),
                  pl.BlockSpec((1, D_MODEL), lambda i: (0, 0))],
        out_specs=pl.BlockSpec((tm, D_MODEL), lambda i: (i, 0)),
        out_shape=jax.ShapeDtypeStruct((M, D_MODEL), F32),
        compiler_params=pltpu.CompilerParams(
            dimension_semantics=("arbitrary",), vmem_limit_bytes=VMEM_LIMIT),
        name="out",
    )(ya, ym, x2d, w_out, ln_g, ln_b)


def _repack_w_in(w):
    offs = np.cumsum([0, W_A, D_C, W_A, H_IDX * D_IDX, D_IDX, H_IDX, H_M * DK_M, H_M * DK_M,
                      W_M, H_M, H_M, W_M, W_M])
    seg = [w[:, offs[i]:offs[i + 1]] for i in range(13)]
    (q_a, c_kv, z_a, q_i, k_i, w_i, q_m, k_m, v_m, i_m, f_m, o_m, z_m) = seg
    main = jnp.concatenate([q_a, z_a, q_i, q_m, k_m, v_m, o_m, z_m], axis=1).astype(BF16)
    pad = jnp.zeros((w.shape[0], N_SMALL - D_C - D_IDX - H_IDX - 2 * H_M), w.dtype)
    small = jnp.concatenate([c_kv, k_i, w_i, i_m, f_m, pad], axis=1).astype(BF16)
    return main, small


def kernel(x, w_in, b_igate, b_fgate, kv_norm_g, w_uk, w_uv, idx_k_ln_g, idx_k_ln_b, rel_bias,
           conv_w, conv_b, mh_norm_g, w_out, ln_g, ln_b):
    B, T, D = x.shape
    assert D == D_MODEL and T % L_M == 0 and T % QB == 0 and w_in.shape[0] == 1
    bias = _bias_tiles(rel_bias)
    x2d = x.reshape(B * T, D)
    w_main, w_small = _repack_w_in(w_in[0])
    main, small = _proj(x2d, w_main, w_small)
    ckv_n, ckv_t, kidx_n, gate_t = _prep(small, kv_norm_g[0][None], idx_k_ln_g[0][None], idx_k_ln_b[0][None])
    w_uk_t = jnp.transpose(w_uk[0], (0, 2, 1)).astype(BF16)
    w_uv_t = jnp.transpose(w_uv[0], (0, 2, 1)).astype(BF16)
    ya = _dsa(main, gate_t, ckv_n, ckv_t, kidx_n, w_uk_t, w_uv_t, bias, B, T)
    gb = jnp.zeros((LANES,), F32).at[S_IM:S_IM + H_M].set(b_igate[0]).at[S_FM:S_FM + H_M].set(b_fgate[0])
    ym = _mlstm(main, small, gate_t, conv_w[0], conv_b[0][None], gb[None, :], gb[:, None],
                mh_norm_g[0][None], B, T)
    out = _out(ya, ym, x2d, w_out[0].astype(BF16), ln_g[0][None], ln_b[0][None])
    return out.reshape(B, T, D)
```

```python
import functools
import math

import numpy as np
import jax
import jax.numpy as jnp
from jax import lax
from jax.experimental import pallas as pl
from jax.experimental.pallas import tpu as pltpu

F32 = jnp.float32
BF16 = jnp.bfloat16

D_MODEL = 2048
W_A = 1024
DH_A = 128
H_A = 8
D_C = 256
H_IDX = 16
D_IDX = 64
TOPK = 256
W_M = 1024
H_M = 4
DV_M = 256
DK_M = 128
CONV_W = 4
N_BUCKETS = 32
MAX_DIST = 128
ALPHA = 2.0 ** 0.25
LN_EPS = 1e-5

LANES = 128
SUBLANES = 8
VMEM_LIMIT = 56 * 1024 * 1024

QB = 256
KB = 256
KI = 128
CNT_ROWS = 512
CNT_ACC = 4 * SUBLANES
L_M = 256
HALO = 16
NEG = -1e30
LOG2E = math.log2(math.e)
ONES_ROWS = 16

C_QA, C_ZA, C_QI, C_QM, C_KM, C_VM, C_OM, C_ZM = 0, 1024, 2048, 3072, 3584, 4096, 5120, 6144
N_MAIN = 7168
N_SMALL = 384
S_KI, S_WI, S_IM, S_FM = 0, 64, 80, 84


def _t5_bucket_np(rel):
    max_exact = N_BUCKETS // 2
    n = np.maximum(rel, 0)
    nf = np.maximum(n, 1).astype(np.float32)
    large = max_exact + (np.log(nf / np.float32(max_exact)) / np.float32(math.log(MAX_DIST / max_exact))
                         * np.float32(N_BUCKETS - max_exact)).astype(np.int32)
    large = np.minimum(large, N_BUCKETS - 1)
    return np.where(n < max_exact, n, large).astype(np.int32)


FAR_BUCKET = int(_t5_bucket_np(np.array(2 * KB + 1)))


def _bucket_tiles():
    i = np.arange(QB)[None, :]
    j = np.arange(KB)[:, None]
    t0 = _t5_bucket_np(i - j)
    t1 = _t5_bucket_np(i - j + KB)
    assert (t5 := _t5_bucket_np(np.arange(KB + 1, 4096))).min() == t5.max() == FAR_BUCKET
    return np.stack([t0, t1]).astype(np.int32)


def _bias_kernel(bucket_ref, rb_ref, out_ref):
    h = pl.program_id(0)
    far = rb_ref[FAR_BUCKET, h]
    for k in range(2):
        bk = bucket_ref[k]
        acc = jnp.zeros((KB, QB), F32)
        for b in range(N_BUCKETS):
            acc = jnp.where(bk == b, rb_ref[b, h] - far, acc)
        out_ref[0, k] = acc * LOG2E
    out_ref[0, 2] = jnp.zeros((KB, QB), F32)


def _bias_tiles(rel_bias):
    bucket = jnp.asarray(_bucket_tiles())
    return pl.pallas_call(
        _bias_kernel,
        grid=(H_A,),
        in_specs=[pl.BlockSpec((2, KB, QB), lambda h: (0, 0, 0)),
                  pl.BlockSpec(memory_space=pltpu.SMEM)],
        out_specs=pl.BlockSpec((1, 3, KB, QB), lambda h: (h, 0, 0, 0)),
        out_shape=jax.ShapeDtypeStruct((H_A, 3, KB, QB), F32),
        name="bias_tiles",
    )(bucket, rel_bias)


def _proj_kernel(x_ref, w_ref, ws_ref, o_ref, os_ref, xb_ref):
    @pl.when(pl.program_id(1) == 0)
    def _():
        xb_ref[...] = x_ref[...].astype(BF16)
        os_ref[...] = jnp.dot(xb_ref[...], ws_ref[...], preferred_element_type=F32)

    o_ref[...] = jnp.dot(xb_ref[...], w_ref[...], preferred_element_type=F32).astype(BF16)


def _proj(x2d, w_main, w_small, tm=1024, tn=512):
    M = x2d.shape[0]
    return pl.pallas_call(
        _proj_kernel,
        grid=(M // tm, N_MAIN // tn),
        in_specs=[pl.BlockSpec((tm, D_MODEL), lambda i, j: (i, 0)),
                  pl.BlockSpec((D_MODEL, tn), lambda i, j: (0, j)),
                  pl.BlockSpec((D_MODEL, N_SMALL), lambda i, j: (0, 0))],
        out_specs=[pl.BlockSpec((tm, tn), lambda i, j: (i, j)),
                   pl.BlockSpec((tm, N_SMALL), lambda i, j: (i, 0))],
        out_shape=[jax.ShapeDtypeStruct((M, N_MAIN), BF16),
                   jax.ShapeDtypeStruct((M, N_SMALL), F32)],
        scratch_shapes=[pltpu.VMEM((tm, D_MODEL), BF16)],
        compiler_params=pltpu.CompilerParams(
            dimension_semantics=("arbitrary", "arbitrary"), vmem_limit_bytes=VMEM_LIMIT),
        name="proj",
    )(x2d, w_main, w_small)


def _prep_kernel(s_ref, kvg_ref, ig_ref, ib_ref, ckv_ref, ckvt_ref, kidx_ref, gt_ref):
    c = s_ref[:, 0:D_C]
    c = c * lax.rsqrt(jnp.mean(c * c, axis=-1, keepdims=True) + LN_EPS) * kvg_ref[...]
    ckv_ref[...] = c.astype(BF16)
    for r in range(ckvt_ref.shape[0]):
        ckvt_ref[r, 0:D_C, :] = c[r * KB:(r + 1) * KB, :].T.astype(BF16)
        ckvt_ref[r, D_C:D_C + ONES_ROWS, :] = jnp.ones((ONES_ROWS, KB), BF16)
    tile = s_ref[:, D_C:D_C + LANES]
    k = tile[:, S_KI:S_KI + D_IDX]
    mu = jnp.mean(k, axis=-1, keepdims=True)
    var = jnp.mean(jnp.square(k - mu), axis=-1, keepdims=True)
    kidx_ref[...] = ((k - mu) * lax.rsqrt(var + LN_EPS) * ig_ref[...] + ib_ref[...]).astype(BF16)
    gt_ref[...] = tile.T


def _prep(small, kv_g, idx_g, idx_b, tm=1024):
    M = small.shape[0]
    return pl.pallas_call(
        _prep_kernel,
        grid=(M // tm,),
        in_specs=[pl.BlockSpec((tm, N_SMALL), lambda i: (i, 0)),
                  pl.BlockSpec((1, D_C), lambda i: (0, 0)),
                  pl.BlockSpec((1, D_IDX), lambda i: (0, 0)),
                  pl.BlockSpec((1, D_IDX), lambda i: (0, 0))],
        out_specs=[pl.BlockSpec((tm, D_C), lambda i: (i, 0)),
                   pl.BlockSpec((tm // KB, D_C + ONES_ROWS, KB), lambda i: (i, 0, 0)),
                   pl.BlockSpec((tm, D_IDX), lambda i: (i, 0)),
                   pl.BlockSpec((LANES, tm), lambda i: (0, i))],
        out_shape=[jax.ShapeDtypeStruct((M, D_C), BF16),
                   jax.ShapeDtypeStruct((M // KB, D_C + ONES_ROWS, KB), BF16),
                   jax.ShapeDtypeStruct((M, D_IDX), BF16),
                   jax.ShapeDtypeStruct((LANES, M), F32)],
        name="prep",
    )(small, kv_g, idx_g, idx_b)


def _key_to_float(key):
    bits = jnp.where(key < 0, key ^ jnp.int32(0x7FFFFFFF), key)
    return lax.bitcast_convert_type(bits, F32)


def _dsa_kernel(qa_ref, za_ref, qi_ref, gt_ref, ckv_ref, ckvt_ref, kidx_ref, wukt_ref, wuvt_ref, bias_ref,
                y_ref, qall_ref, qr_ref, ha_ref, hb_ref, sc_ref, mb_ref, sa_ref, sb_ref, pa_ref, pb_ref,
                m_ref, al_ref, acc_ref, w_ref, cand_ref):
    qi = pl.program_id(1)
    nkb = qi + 1
    last_kb = mb_ref.shape[0] - 1

    for h in range(H_A):
        ql = lax.dot_general(wukt_ref[h], qa_ref[:, h * DH_A:(h + 1) * DH_A], (((1,), (1,)), ((), ())),
                             preferred_element_type=F32)
        qall_ref[h // (H_A // 2), :, (h % (H_A // 2)) * QB:(h % (H_A // 2) + 1) * QB] = (
            ql * (DH_A ** -0.5 * LOG2E)).astype(BF16)


    @pl.when(qi * QB < TOPK)
    def _():
        krow = lax.broadcasted_iota(jnp.int32, (KB, QB), 0)
        qcol = lax.broadcasted_iota(jnp.int32, (KB, QB), 1)
        mb_ref[0] = jnp.where(krow <= qcol, 0.0, NEG).astype(F32)

    @pl.when(qi * QB >= TOPK)
    def _():
        for h in range(H_IDX):
            qr_ref[h * QB:(h + 1) * QB, :] = qi_ref[:, h * D_IDX:(h + 1) * D_IDX]
        w_ref[...] = gt_ref[S_WI:S_WI + H_IDX, :] * ((D_IDX ** -0.5) * (H_IDX ** -0.5))
        krow = lax.broadcasted_iota(jnp.int32, (KI, LANES), 0)
        qcol = lax.broadcasted_iota(jnp.int32, (KI, LANES), 1)
        n_ki = nkb * (KB // KI)
        last_ki = sc_ref.shape[0] // KI - 1

        def head_dots(ki, dst_ref):
            k = kidx_ref[pl.ds(pl.multiple_of(jnp.minimum(ki, last_ki) * KI, KI), KI), :]
            dst_ref[...] = lax.dot_general(k, qr_ref[...], (((1,), (1,)), ((), ())),
                                           preferred_element_type=F32)

        def reduce_heads(src_ref, ki):
            for g in range(QB // LANES):
                lanes = slice(g * LANES, (g + 1) * LANES)
                acc = jnp.zeros((KI, LANES), F32)
                for h in range(H_IDX):
                    acc = acc + (jnp.maximum(src_ref[:, h * QB + g * LANES:h * QB + (g + 1) * LANES], 0.0)
                                 * w_ref[h:h + 1, lanes])
                sc_ref[pl.ds(pl.multiple_of(ki * KI, KI), KI), lanes] = jnp.where(
                    krow + (ki * KI - qi * QB - g * LANES) <= qcol, acc, -jnp.inf)

        pad_kb = jnp.minimum(nkb, last_kb)
        sc_ref[pl.ds(pl.multiple_of(pad_kb * KB, KB), KB), :] = jnp.full((KB, QB), -jnp.inf, F32)

        head_dots(0, ha_ref)

        def sc_body(j, carry):
            head_dots(2 * j + 1, hb_ref)
            reduce_heads(ha_ref, 2 * j)
            head_dots(2 * j + 2, ha_ref)
            reduce_heads(hb_ref, 2 * j + 1)
            return carry
        lax.fori_loop(0, n_ki // 2, sc_body, 0)

        n_cnt = (nkb * KB + CNT_ROWS - 1) // CNT_ROWS

        def count_ge(cand):
            cand_ref[...] = cand

            def body(c, acc):
                parts = []
                for g in range(QB // LANES):
                    lanes = slice(g * LANES, (g + 1) * LANES)
                    blk = sc_ref[pl.ds(pl.multiple_of(c * CNT_ROWS, CNT_ROWS), CNT_ROWS), lanes]
                    ge = jnp.where(blk >= cand_ref[:, lanes], 1.0, 0.0).astype(F32)
                    parts.append(jnp.sum(ge.reshape(CNT_ROWS // CNT_ACC, CNT_ACC, LANES), axis=0))
                return acc + jnp.concatenate(parts, axis=1)
            acc = lax.fori_loop(0, n_cnt, body, jnp.zeros((CNT_ACC, QB), F32))
            return jnp.sum(acc, axis=0, keepdims=True)

        def bit_body(i, u):
            trial = u | lax.shift_left(jnp.int32(1), 31 - i)
            cnt = count_ge(_key_to_float(trial ^ jnp.int32(-2 ** 31)))
            return jnp.where(cnt >= float(TOPK), trial, u)
        u = lax.fori_loop(0, 32, bit_body, jnp.zeros((1, QB), jnp.int32))
        thr = _key_to_float(u ^ jnp.int32(-2 ** 31))

        def mb_body(kb, carry):
            blk = sc_ref[pl.ds(pl.multiple_of(kb * KB, KB), KB), :]
            mb_ref[kb] = jnp.where(blk >= thr, 0.0, NEG).astype(F32)
            return carry
        lax.fori_loop(0, nkb, mb_body, 0)

    m_ref[...] = jnp.full(m_ref.shape, NEG, F32)
    acc_ref[...] = jnp.zeros(acc_ref.shape, F32)

    hh = H_A // 2
    s_refs, p_refs = (sa_ref, sb_ref), (pa_ref, pb_ref)
    pb_ref[...] = jnp.zeros(pb_ref.shape, BF16)
    al_ref[...] = jnp.ones(al_ref.shape, F32)

    def logits(kb, half):
        kv = ckv_ref[pl.ds(pl.multiple_of(jnp.minimum(kb, last_kb) * KB, KB), KB), :]
        s_refs[half][...] = jnp.dot(kv, qall_ref[half], preferred_element_type=F32)

    def softmax(kb, half, with_bias):
        tile = jnp.clip(qi - kb, 0, 2)
        for j in range(hh):
            for g in range(QB // LANES):
                lanes = slice(g * LANES, (g + 1) * LANES)
                cols = slice(j * QB + g * LANES, j * QB + (g + 1) * LANES)
                x = s_refs[half][:, cols] + mb_ref[kb, :, lanes]
                if with_bias:
                    x = x + bias_ref[half * hh + j, tile, :, lanes]
                s_refs[half][:, cols] = x
                m_prev = m_ref[half, :, cols]
                m_blk = jnp.max(x.reshape(KB // CNT_ACC, CNT_ACC, LANES), axis=0)
                m_new = jnp.maximum(m_prev, jnp.max(m_blk, axis=0, keepdims=True))
                al_ref[half, :, cols] = jnp.exp2(m_prev - m_new)
                m_ref[half, :, cols] = m_new
                p_refs[half][:, cols] = jnp.exp2(s_refs[half][:, cols] - m_new).astype(BF16)

    def accumulate(kb, half):
        acc_ref[half] = acc_ref[half] * al_ref[half]
        acc_ref[half] += jnp.dot(ckvt_ref[jnp.maximum(kb, 0)], p_refs[half][...], preferred_element_type=F32)

    def sweep(first_kb, end_kb, with_bias):
        def body(kb, carry):
            logits(kb, 1)
            softmax(kb, 0, with_bias)
            accumulate(kb - 1, 1)
            logits(kb + 1, 0)
            softmax(kb, 1, with_bias)
            accumulate(kb, 0)
            return carry
        lax.fori_loop(first_kb, end_kb, body, 0)

    n_far = jnp.maximum(qi - 1, 0)
    logits(0, 0)
    sweep(0, n_far, False)
    sweep(n_far, nkb, True)
    accumulate(nkb - 1, 1)

    for h in range(H_A):
        half, cols = h // hh, slice((h % hh) * QB, (h % hh + 1) * QB)
        ya_t = jnp.dot(wuvt_ref[h], acc_ref[half, 0:D_C, cols].astype(BF16), preferred_element_type=F32)
        ya = (ya_t / acc_ref[half, D_C:D_C + 1, cols]).T
        z = za_ref[:, h * DH_A:(h + 1) * DH_A].astype(F32)
        y_ref[:, h * DH_A:(h + 1) * DH_A] = (ya * (z * jax.nn.sigmoid(z))).astype(BF16)


def _dsa(main, gate_t, ckv_n, ckv_t, kidx_n, w_uk_t, w_uv_t, bias, B, T):
    nq = T // QB
    return pl.pallas_call(
        _dsa_kernel,
        grid=(B, nq),
        in_specs=[pl.BlockSpec((QB, W_A), lambda b, q: (b * nq + q, C_QA // W_A)),
                  pl.BlockSpec((QB, W_A), lambda b, q: (b * nq + q, C_ZA // W_A)),
                  pl.BlockSpec((QB, H_IDX * D_IDX), lambda b, q: (b * nq + q, C_QI // (H_IDX * D_IDX))),
                  pl.BlockSpec((LANES, QB), lambda b, q: (0, b * nq + q)),
                  pl.BlockSpec((T, D_C), lambda b, q: (b, 0)),
                  pl.BlockSpec((T // KB, D_C + ONES_ROWS, KB), lambda b, q: (b, 0, 0)),
                  pl.BlockSpec((T, D_IDX), lambda b, q: (b, 0)),
                  pl.BlockSpec((H_A, D_C, DH_A), lambda b, q: (0, 0, 0)),
                  pl.BlockSpec((H_A, DH_A, D_C), lambda b, q: (0, 0, 0)),
                  pl.BlockSpec((H_A, 3, KB, QB), lambda b, q: (0, 0, 0, 0))],
        out_specs=pl.BlockSpec((QB, W_A), lambda b, q: (b * nq + q, 0)),
        out_shape=jax.ShapeDtypeStruct((B * T, W_A), BF16),
        scratch_shapes=[pltpu.VMEM((2, D_C, H_A // 2 * QB), BF16),
                        pltpu.VMEM((H_IDX * QB, D_IDX), BF16),
                        pltpu.VMEM((KI, H_IDX * QB), F32),
                        pltpu.VMEM((KI, H_IDX * QB), F32),
                        pltpu.VMEM((T, QB), F32),
                        pltpu.VMEM((T // KB, KB, QB), F32),
                        pltpu.VMEM((KB, H_A // 2 * QB), F32),
                        pltpu.VMEM((KB, H_A // 2 * QB), F32),
                        pltpu.VMEM((KB, H_A // 2 * QB), BF16),
                        pltpu.VMEM((KB, H_A // 2 * QB), BF16),
                        pltpu.VMEM((2, 1, H_A // 2 * QB), F32),
                        pltpu.VMEM((2, 1, H_A // 2 * QB), F32),
                        pltpu.VMEM((2, D_C + ONES_ROWS, H_A // 2 * QB), F32),
                        pltpu.VMEM((H_IDX, QB), F32),
                        pltpu.VMEM((1, QB), F32)],
        compiler_params=pltpu.CompilerParams(
            dimension_semantics=("arbitrary", "arbitrary"), vmem_limit_bytes=VMEM_LIMIT),
        name="dsa",
    )(main, main, main, gate_t, ckv_n, ckv_t, kidx_n, w_uk_t, w_uv_t, bias)


def _split_dot(tri, x):
    hi = x.astype(BF16)
    lo = (x - hi.astype(F32)).astype(BF16)
    return jnp.dot(tri, hi, preferred_element_type=F32) + jnp.dot(tri, lo, preferred_element_type=F32)


def _log_sigmoid(x):
    return jnp.minimum(x, 0.0) - jnp.log1p(jnp.exp(-jnp.abs(x)))


def _mlstm_kernel(q_ref, k_ref, qh_ref, kh_ref, v_ref, o_ref, z_ref, g_ref, gt_ref,
                  cw_ref, cb_ref, gbr_ref, gbc_ref, ng_ref, y_ref, ct_ref, n_ref, m_ref):
    c = pl.program_id(1)
    L = L_M

    @pl.when(c == 0)
    def _():
        ct_ref[...] = jnp.zeros(ct_ref.shape, F32)
        n_ref[...] = jnp.zeros(n_ref.shape, F32)
        m_ref[...] = jnp.zeros(m_ref.shape, F32)

    def conv_silu(x_ref, halo_ref, lo):
        halo = jnp.where(c > 0, halo_ref[...].astype(F32), 0.0)
        xe = jnp.concatenate([halo, x_ref[...].astype(F32)], axis=0)
        y = cb_ref[:, lo:lo + H_M * DK_M]
        for j in range(CONV_W):
            off = HALO - (CONV_W - 1) + j
            y = y + cw_ref[j:j + 1, lo:lo + H_M * DK_M] * xe[off:off + L]
        return y * jax.nn.sigmoid(y)

    q_all = conv_silu(q_ref, qh_ref, 0)
    k_all = conv_silu(k_ref, kh_ref, H_M * DK_M) * (DK_M ** -0.5)

    gc = g_ref[...] + gbr_ref[...]
    gr = gt_ref[S_IM:S_IM + 2 * H_M, :] + gbc_ref[S_IM:S_IM + 2 * H_M, :]
    r = lax.broadcasted_iota(jnp.int32, (L, L), 0)
    s = lax.broadcasted_iota(jnp.int32, (L, L), 1)
    causal = s <= r
    tri_l = jnp.where(causal, 1.0, 0.0).astype(BF16)
    tri_u = jnp.where(r <= s, 1.0, 0.0).astype(BF16)
    b_cols = _split_dot(tri_l, _log_sigmoid(gc))
    b_rows = jnp.dot(_log_sigmoid(gr).astype(BF16), tri_u, preferred_element_type=F32) \
        + jnp.dot((_log_sigmoid(gr) - _log_sigmoid(gr).astype(BF16).astype(F32)).astype(BF16), tri_u,
                  preferred_element_type=F32)

    for h in range(H_M):
        q = q_all[:, h * DK_M:(h + 1) * DK_M]
        k = k_all[:, h * DK_M:(h + 1) * DK_M]
        v = v_ref[:, h * DV_M:(h + 1) * DV_M]
        qb = q.astype(BF16)
        b_c = b_cols[:, S_FM + h:S_FM + h + 1]
        i_c = gc[:, S_IM + h:S_IM + h + 1]
        b_r = b_rows[H_M + h:H_M + h + 1, :]
        i_r = gr[h:h + 1, :]
        m_prev = m_ref[h]
        ct = ct_ref[h]
        n = n_ref[h]

        log_d = jnp.where(causal, b_c - b_r + i_r, -jnp.inf)
        g = b_c + m_prev
        m_t = jnp.maximum(jnp.max(log_d, axis=-1, keepdims=True), g)
        qk = lax.dot_general(qb, k.astype(BF16), (((1,), (1,)), ((), ())), preferred_element_type=F32)
        s_mat = qk * jnp.exp(log_d - m_t)
        inter = jnp.exp(g - m_t)
        num = jnp.dot(s_mat.astype(BF16), v, preferred_element_type=F32) \
            + inter * jnp.dot(qb, ct.astype(BF16), preferred_element_type=F32)
        den = jnp.sum(s_mat, axis=-1, keepdims=True) + inter * jnp.sum(q * n, axis=-1, keepdims=True)
        hh = num / jnp.maximum(jnp.abs(den), jnp.exp(-m_t))

        b_last = b_c[L - 1:L, :]
        a_r = b_last - b_r + i_r
        m_new = jnp.maximum(b_last + m_prev, jnp.max(a_r, axis=-1, keepdims=True))
        decay = jnp.exp(b_last + m_prev - m_new)
        wgt_c = jnp.exp(b_last - b_c + i_c - m_new)
        kw = k * wgt_c
        ct_ref[h] = decay * ct + jnp.dot(kw.T.astype(BF16), v, preferred_element_type=F32)
        n_ref[h] = decay * n + jnp.sum(kw, axis=0, keepdims=True)
        m_ref[h] = m_new

        mu = jnp.mean(hh, axis=-1, keepdims=True)
        var = jnp.mean(jnp.square(hh - mu), axis=-1, keepdims=True)
        hn = (hh - mu) * lax.rsqrt(var + LN_EPS) * ng_ref[:, h * DV_M:(h + 1) * DV_M]
        og = o_ref[:, h * DV_M:(h + 1) * DV_M].astype(F32)
        zg = z_ref[:, h * DV_M:(h + 1) * DV_M].astype(F32)
        y_ref[:, h * DV_M:(h + 1) * DV_M] = (hn * jax.nn.sigmoid(og) * (zg * jax.nn.sigmoid(zg))).astype(BF16)


def _mlstm(main, small, gate_t, conv_w, conv_b, gb_row, gb_col, norm_g, B, T):
    nc = T // L_M
    hb = L_M // HALO
    qk_w = H_M * DK_M

    def halo_map(col):
        return lambda b, c: (jnp.maximum((b * nc + c) * hb - 1, 0), col)

    return pl.pallas_call(
        _mlstm_kernel,
        grid=(B, nc),
        in_specs=[pl.BlockSpec((L_M, qk_w), lambda b, c: (b * nc + c, C_QM // qk_w)),
                  pl.BlockSpec((L_M, qk_w), lambda b, c: (b * nc + c, C_KM // qk_w)),
                  pl.BlockSpec((HALO, qk_w), halo_map(C_QM // qk_w)),
                  pl.BlockSpec((HALO, qk_w), halo_map(C_KM // qk_w)),
                  pl.BlockSpec((L_M, W_M), lambda b, c: (b * nc + c, C_VM // W_M)),
                  pl.BlockSpec((L_M, W_M), lambda b, c: (b * nc + c, C_OM // W_M)),
                  pl.BlockSpec((L_M, W_M), lambda b, c: (b * nc + c, C_ZM // W_M)),
                  pl.BlockSpec((L_M, LANES), lambda b, c: (b * nc + c, D_C // LANES)),
                  pl.BlockSpec((LANES, L_M), lambda b, c: (0, b * nc + c)),
                  pl.BlockSpec((CONV_W, 2 * qk_w), lambda b, c: (0, 0)),
                  pl.BlockSpec((1, 2 * qk_w), lambda b, c: (0, 0)),
                  pl.BlockSpec((1, LANES), lambda b, c: (0, 0)),
                  pl.BlockSpec((LANES, 1), lambda b, c: (0, 0)),
                  pl.BlockSpec((1, W_M), lambda b, c: (0, 0))],
        out_specs=pl.BlockSpec((L_M, W_M), lambda b, c: (b * nc + c, 0)),
        out_shape=jax.ShapeDtypeStruct((B * T, W_M), BF16),
        scratch_shapes=[pltpu.VMEM((H_M, DK_M, DV_M), F32),
                        pltpu.VMEM((H_M, 1, DK_M), F32),
                        pltpu.VMEM((H_M, 1, 1), F32)],
        compiler_params=pltpu.CompilerParams(
            dimension_semantics=("arbitrary", "arbitrary"), vmem_limit_bytes=VMEM_LIMIT),
        name="mlstm",
    )(main, main, main, main, main, main, main, small, gate_t,
      conv_w, conv_b, gb_row, gb_col, norm_g)


def _out_kernel(ya_ref, ym_ref, x_ref, w_ref, g_ref, b_ref, o_ref):
    y = jnp.dot(ya_ref[...], w_ref[0:W_A, :], preferred_element_type=F32)
    y = y + jnp.dot(ym_ref[...], w_ref[W_A:W_A + W_M, :], preferred_element_type=F32)
    r = ALPHA * x_ref[...] + y
    mu = jnp.mean(r, axis=-1, keepdims=True)
    var = jnp.mean(jnp.square(r - mu), axis=-1, keepdims=True)
    o_ref[...] = (r - mu) * lax.rsqrt(var + LN_EPS) * g_ref[...] + b_ref[...]


def _out(ya, ym, x2d, w_out, ln_g, ln_b, tm=512):
    M = x2d.shape[0]
    return pl.pallas_call(
        _out_kernel,
        grid=(M // tm,),
        in_specs=[pl.BlockSpec((tm, W_A), lambda i: (i, 0)),
                  pl.BlockSpec((tm, W_M), lambda i: (i, 0)),
                  pl.BlockSpec((tm, D_MODEL), lambda i: (i, 0)),
                  pl.BlockSpec((W_A + W_M, D_MODEL), lambda i: (0, 0)),
                  pl.BlockSpec((1, D_MODEL), lambda i: (0, 0)),
                  pl.BlockSpec((1, D_MODEL), lambda i: (0, 0))],
        out_specs=pl.BlockSpec((tm, D_MODEL), lambda i: (i, 0)),
        out_shape=jax.ShapeDtypeStruct((M, D_MODEL), F32),
        compiler_params=pltpu.CompilerParams(
            dimension_semantics=("arbitrary",), vmem_limit_bytes=VMEM_LIMIT),
        name="out",
    )(ya, ym, x2d, w_out, ln_g, ln_b)


def _repack_w_in(w):
    offs = np.cumsum([0, W_A, D_C, W_A, H_IDX * D_IDX, D_IDX, H_IDX, H_M * DK_M, H_M * DK_M,
                      W_M, H_M, H_M, W_M, W_M])
    seg = [w[:, offs[i]:offs[i + 1]] for i in range(13)]
    (q_a, c_kv, z_a, q_i, k_i, w_i, q_m, k_m, v_m, i_m, f_m, o_m, z_m) = seg
    main = jnp.concatenate([q_a, z_a, q_i, q_m, k_m, v_m, o_m, z_m], axis=1).astype(BF16)
    pad = jnp.zeros((w.shape[0], N_SMALL - D_C - D_IDX - H_IDX - 2 * H_M), w.dtype)
    small = jnp.concatenate([c_kv, k_i, w_i, i_m, f_m, pad], axis=1).astype(BF16)
    return main, small


def kernel(x, w_in, b_igate, b_fgate, kv_norm_g, w_uk, w_uv, idx_k_ln_g, idx_k_ln_b, rel_bias,
           conv_w, conv_b, mh_norm_g, w_out, ln_g, ln_b):
    B, T, D = x.shape
    assert D == D_MODEL and T % L_M == 0 and T % (2 * KB) == 0 and w_in.shape[0] == 1
    bias = _bias_tiles(rel_bias)
    x2d = x.reshape(B * T, D)
    w_main, w_small = _repack_w_in(w_in[0])
    main, small = _proj(x2d, w_main, w_small)
    ckv_n, ckv_t, kidx_n, gate_t = _prep(small, kv_norm_g[0][None], idx_k_ln_g[0][None], idx_k_ln_b[0][None])
    w_uk_t = jnp.transpose(w_uk[0], (0, 2, 1)).astype(BF16)
    w_uv_t = jnp.transpose(w_uv[0], (0, 2, 1)).astype(BF16)
    ya = _dsa(main, gate_t, ckv_n, ckv_t, kidx_n, w_uk_t, w_uv_t, bias, B, T)
    gb = jnp.zeros((LANES,), F32).at[S_IM:S_IM + H_M].set(b_igate[0]).at[S_FM:S_FM + H_M].set(b_fgate[0])
    ym = _mlstm(main, small, gate_t, conv_w[0], conv_b[0][None], gb[None, :], gb[:, None],
                mh_norm_g[0][None], B, T)
    out = _out(ya, ym, x2d, w_out[0].astype(BF16), ln_g[0][None], ln_b[0][None])
    return out.reshape(B, T, D)
```

```python
import functools
import math

import numpy as np
import jax
import jax.numpy as jnp
from jax import lax
from jax.experimental import pallas as pl
from jax.experimental.pallas import tpu as pltpu

F32 = jnp.float32
BF16 = jnp.bfloat16

D_MODEL = 2048
W_A = 1024
DH_A = 128
H_A = 8
D_C = 256
H_IDX = 16
D_IDX = 64
TOPK = 256
W_M = 1024
H_M = 4
DV_M = 256
DK_M = 128
CONV_W = 4
N_BUCKETS = 32
MAX_DIST = 128
ALPHA = 2.0 ** 0.25
LN_EPS = 1e-5

LANES = 128
SUBLANES = 8
VMEM_LIMIT = 56 * 1024 * 1024

QB = 256
KB = 256
KI = 128
CNT_ROWS = 512
CNT_ACC = 4 * SUBLANES
L_M = 256
HALO = 16
NEG = -1e30
LOG2E = math.log2(math.e)
ONES_ROWS = 16

C_QA, C_ZA, C_QI, C_QM, C_KM, C_VM, C_OM, C_ZM = 0, 1024, 2048, 3072, 3584, 4096, 5120, 6144
N_MAIN = 7168
N_SMALL = 384
S_KI, S_WI, S_IM, S_FM = 0, 64, 80, 84


def _t5_bucket_np(rel):
    max_exact = N_BUCKETS // 2
    n = np.maximum(rel, 0)
    nf = np.maximum(n, 1).astype(np.float32)
    large = max_exact + (np.log(nf / np.float32(max_exact)) / np.float32(math.log(MAX_DIST / max_exact))
                         * np.float32(N_BUCKETS - max_exact)).astype(np.int32)
    large = np.minimum(large, N_BUCKETS - 1)
    return np.where(n < max_exact, n, large).astype(np.int32)


FAR_BUCKET = int(_t5_bucket_np(np.array(2 * KB + 1)))


def _bucket_tiles():
    i = np.arange(QB)[None, :]
    j = np.arange(KB)[:, None]
    t0 = _t5_bucket_np(i - j)
    t1 = _t5_bucket_np(i - j + KB)
    assert (t5 := _t5_bucket_np(np.arange(KB + 1, 4096))).min() == t5.max() == FAR_BUCKET
    return np.stack([t0, t1]).astype(np.int32)


def _bias_kernel(bucket_ref, rb_ref, out_ref):
    h = pl.program_id(0)
    far = rb_ref[FAR_BUCKET, h]
    for k in range(2):
        bk = bucket_ref[k]
        acc = jnp.zeros((KB, QB), F32)
        for b in range(N_BUCKETS):
            acc = jnp.where(bk == b, rb_ref[b, h] - far, acc)
        out_ref[0, k] = acc * LOG2E
    out_ref[0, 2] = jnp.zeros((KB, QB), F32)


def _bias_tiles(rel_bias):
    bucket = jnp.asarray(_bucket_tiles())
    return pl.pallas_call(
        _bias_kernel,
        grid=(H_A,),
        in_specs=[pl.BlockSpec((2, KB, QB), lambda h: (0, 0, 0)),
                  pl.BlockSpec(memory_space=pltpu.SMEM)],
        out_specs=pl.BlockSpec((1, 3, KB, QB), lambda h: (h, 0, 0, 0)),
        out_shape=jax.ShapeDtypeStruct((H_A, 3, KB, QB), F32),
        name="bias_tiles",
    )(bucket, rel_bias)


def _proj_kernel(x_ref, w_ref, ws_ref, o_ref, os_ref, xb_ref):
    @pl.when(pl.program_id(1) == 0)
    def _():
        xb_ref[...] = x_ref[...].astype(BF16)
        os_ref[...] = jnp.dot(xb_ref[...], ws_ref[...], preferred_element_type=F32)

    o_ref[...] = jnp.dot(xb_ref[...], w_ref[...], preferred_element_type=F32).astype(BF16)


def _proj(x2d, w_main, w_small, tm=1024, tn=512):
    M = x2d.shape[0]
    return pl.pallas_call(
        _proj_kernel,
        grid=(M // tm, N_MAIN // tn),
        in_specs=[pl.BlockSpec((tm, D_MODEL), lambda i, j: (i, 0)),
                  pl.BlockSpec((D_MODEL, tn), lambda i, j: (0, j)),
                  pl.BlockSpec((D_MODEL, N_SMALL), lambda i, j: (0, 0))],
        out_specs=[pl.BlockSpec((tm, tn), lambda i, j: (i, j)),
                   pl.BlockSpec((tm, N_SMALL), lambda i, j: (i, 0))],
        out_shape=[jax.ShapeDtypeStruct((M, N_MAIN), BF16),
                   jax.ShapeDtypeStruct((M, N_SMALL), F32)],
        scratch_shapes=[pltpu.VMEM((tm, D_MODEL), BF16)],
        compiler_params=pltpu.CompilerParams(
            dimension_semantics=("arbitrary", "arbitrary"), vmem_limit_bytes=VMEM_LIMIT),
        name="proj",
    )(x2d, w_main, w_small)


def _prep_kernel(s_ref, kvg_ref, ig_ref, ib_ref, ckv_ref, ckvt_ref, kidx_ref, gt_ref):
    c = s_ref[:, 0:D_C]
    c = c * lax.rsqrt(jnp.mean(c * c, axis=-1, keepdims=True) + LN_EPS) * kvg_ref[...]
    ckv_ref[...] = c.astype(BF16)
    for r in range(ckvt_ref.shape[0]):
        ckvt_ref[r, 0:D_C, :] = c[r * KB:(r + 1) * KB, :].T.astype(BF16)
        ckvt_ref[r, D_C:D_C + ONES_ROWS, :] = jnp.ones((ONES_ROWS, KB), BF16)
    tile = s_ref[:, D_C:D_C + LANES]
    k = tile[:, S_KI:S_KI + D_IDX]
    mu = jnp.mean(k, axis=-1, keepdims=True)
    var = jnp.mean(jnp.square(k - mu), axis=-1, keepdims=True)
    kidx_ref[...] = ((k - mu) * lax.rsqrt(var + LN_EPS) * ig_ref[...] + ib_ref[...]).astype(BF16)
    gt_ref[...] = tile.T


def _prep(small, kv_g, idx_g, idx_b, tm=1024):
    M = small.shape[0]
    return pl.pallas_call(
        _prep_kernel,
        grid=(M // tm,),
        in_specs=[pl.BlockSpec((tm, N_SMALL), lambda i: (i, 0)),
                  pl.BlockSpec((1, D_C), lambda i: (0, 0)),
                  pl.BlockSpec((1, D_IDX), lambda i: (0, 0)),
                  pl.BlockSpec((1, D_IDX), lambda i: (0, 0))],
        out_specs=[pl.BlockSpec((tm, D_C), lambda i: (i, 0)),
                   pl.BlockSpec((tm // KB, D_C + ONES_ROWS, KB), lambda i: (i, 0, 0)),
                   pl.BlockSpec((tm, D_IDX), lambda i: (i, 0)),
                   pl.BlockSpec((LANES, tm), lambda i: (0, i))],
        out_shape=[jax.ShapeDtypeStruct((M, D_C), BF16),
                   jax.ShapeDtypeStruct((M // KB, D_C + ONES_ROWS, KB), BF16),
                   jax.ShapeDtypeStruct((M, D_IDX), BF16),
                   jax.ShapeDtypeStruct((LANES, M), F32)],
        name="prep",
    )(small, kv_g, idx_g, idx_b)


def _key_to_float(key):
    bits = jnp.where(key < 0, key ^ jnp.int32(0x7FFFFFFF), key)
    return lax.bitcast_convert_type(bits, F32)


def _dsa_kernel(qa_ref, za_ref, qi_ref, gt_ref, ckv_ref, ckvt_ref, kidx_ref, wukt_ref, wuvt_ref, bias_ref,
                y_ref, qall_ref, qr_ref, ha_ref, hb_ref, sc_ref, mb_ref, sa_ref, sb_ref, pa_ref, pb_ref,
                m_ref, al_ref, acc_ref, w_ref, cand_ref, thr_ref, cut_ref):
    qi = pl.program_id(1)
    nkb = qi + 1
    last_kb = mb_ref.shape[0] - 1

    for h in range(H_A):
        ql = lax.dot_general(wukt_ref[h], qa_ref[:, h * DH_A:(h + 1) * DH_A], (((1,), (1,)), ((), ())),
                             preferred_element_type=F32)
        qall_ref[h // (H_A // 2), :, (h % (H_A // 2)) * QB:(h % (H_A // 2) + 1) * QB] = (
            ql * (DH_A ** -0.5 * LOG2E)).astype(BF16)


    @pl.when(qi * QB < TOPK)
    def _():
        krow = lax.broadcasted_iota(jnp.int32, (KB, QB), 0)
        qcol = lax.broadcasted_iota(jnp.int32, (KB, QB), 1)
        mb_ref[0] = jnp.where(krow <= qcol, 0.0, NEG).astype(F32)

    @pl.when(qi * QB >= TOPK)
    def _():
        for h in range(H_IDX):
            qr_ref[h * QB:(h + 1) * QB, :] = qi_ref[:, h * D_IDX:(h + 1) * D_IDX]
        w_ref[...] = gt_ref[S_WI:S_WI + H_IDX, :] * ((D_IDX ** -0.5) * (H_IDX ** -0.5))
        krow = lax.broadcasted_iota(jnp.int32, (KI, LANES), 0)
        qcol = lax.broadcasted_iota(jnp.int32, (KI, LANES), 1)
        n_ki = nkb * (KB // KI)
        last_ki = sc_ref.shape[0] // KI - 1

        def head_dots(ki, dst_ref):
            k = kidx_ref[pl.ds(pl.multiple_of(jnp.minimum(ki, last_ki) * KI, KI), KI), :]
            dst_ref[...] = lax.dot_general(k, qr_ref[...], (((1,), (1,)), ((), ())),
                                           preferred_element_type=F32)

        def reduce_heads(src_ref, ki):
            for g in range(QB // LANES):
                lanes = slice(g * LANES, (g + 1) * LANES)
                acc = jnp.zeros((KI, LANES), F32)
                for h in range(H_IDX):
                    acc = acc + (jnp.maximum(src_ref[:, h * QB + g * LANES:h * QB + (g + 1) * LANES], 0.0)
                                 * w_ref[h:h + 1, lanes])
                sc_ref[pl.ds(pl.multiple_of(ki * KI, KI), KI), lanes] = jnp.where(
                    krow + (ki * KI - qi * QB - g * LANES) <= qcol, acc, -jnp.inf)

        pad_kb = jnp.minimum(nkb, last_kb)
        sc_ref[pl.ds(pl.multiple_of(pad_kb * KB, KB), KB), :] = jnp.full((KB, QB), -jnp.inf, F32)

        head_dots(0, ha_ref)

        def sc_body(j, carry):
            head_dots(2 * j + 1, hb_ref)
            reduce_heads(ha_ref, 2 * j)
            head_dots(2 * j + 2, ha_ref)
            reduce_heads(hb_ref, 2 * j + 1)
            return carry
        lax.fori_loop(0, n_ki // 2, sc_body, 0)

        n_cnt = (nkb * KB + CNT_ROWS - 1) // CNT_ROWS

        def count_where(pred):
            def body(c, acc):
                parts = []
                for g in range(QB // LANES):
                    lanes = slice(g * LANES, (g + 1) * LANES)
                    blk = sc_ref[pl.ds(pl.multiple_of(c * CNT_ROWS, CNT_ROWS), CNT_ROWS), lanes]
                    hit = jnp.where(pred(blk, c * CNT_ROWS, lanes), 1.0, 0.0).astype(F32)
                    parts.append(jnp.sum(hit.reshape(CNT_ROWS // CNT_ACC, CNT_ACC, LANES), axis=0))
                return acc + jnp.concatenate(parts, axis=1)
            acc = lax.fori_loop(0, n_cnt, body, jnp.zeros((CNT_ACC, QB), F32))
            return jnp.sum(acc, axis=0, keepdims=True)

        def bit_body(i, carry):
            u, c_ge = carry
            trial = u | lax.shift_left(jnp.int32(1), 31 - i)
            cand_ref[...] = _key_to_float(trial ^ jnp.int32(-2 ** 31))
            cnt = count_where(lambda blk, row0, lanes: blk >= cand_ref[:, lanes])
            ok = cnt >= float(TOPK)
            return jnp.where(ok, trial, u), jnp.where(ok, cnt, c_ge)
        u, c_ge = lax.fori_loop(0, 32, bit_body, (jnp.zeros((1, QB), jnp.int32),
                                                  jnp.full((1, QB), float(sc_ref.shape[0]), F32)))
        thr_ref[...] = _key_to_float(u ^ jnp.int32(-2 ** 31))
        has_ties = jnp.max(c_ge) > float(TOPK)

        @pl.when(jnp.logical_not(has_ties))
        def _():
            def mb_body(kb, carry):
                blk = sc_ref[pl.ds(pl.multiple_of(kb * KB, KB), KB), :]
                mb_ref[kb] = jnp.where(blk >= thr_ref[...], 0.0, NEG).astype(F32)
                return carry
            lax.fori_loop(0, nkb, mb_body, 0)

        @pl.when(has_ties)
        def _():
            c_gt = count_where(lambda blk, row0, lanes: blk > thr_ref[:, lanes])
            need = float(TOPK) - c_gt
            rows = lax.broadcasted_iota(jnp.int32, (CNT_ROWS, LANES), 0)
            n_bits = (sc_ref.shape[0] - 1).bit_length()

            def idx_body(i, cut):
                trial = cut | lax.shift_left(jnp.int32(1), n_bits - 1 - i)
                cut_ref[...] = trial
                before = count_where(lambda blk, row0, lanes: (blk == thr_ref[:, lanes])
                                     & (rows + row0 < cut_ref[:, lanes]))
                return jnp.where(before < need, trial, cut)
            cut_ref[...] = lax.fori_loop(0, n_bits, idx_body, jnp.zeros((1, QB), jnp.int32))

            def mb_body(kb, carry):
                for g in range(QB // LANES):
                    lanes = slice(g * LANES, (g + 1) * LANES)
                    blk = sc_ref[pl.ds(pl.multiple_of(kb * KB, KB), KB), lanes]
                    thr = thr_ref[:, lanes]
                    keep = (blk > thr) | ((blk == thr) & (rows[0:KB] + kb * KB <= cut_ref[:, lanes]))
                    mb_ref[kb, :, lanes] = jnp.where(keep, 0.0, NEG).astype(F32)
                return carry
            lax.fori_loop(0, nkb, mb_body, 0)

    m_ref[...] = jnp.full(m_ref.shape, NEG, F32)
    acc_ref[...] = jnp.zeros(acc_ref.shape, F32)

    hh = H_A // 2
    s_refs, p_refs = (sa_ref, sb_ref), (pa_ref, pb_ref)
    pb_ref[...] = jnp.zeros(pb_ref.shape, BF16)
    al_ref[...] = jnp.ones(al_ref.shape, F32)

    def logits(kb, half):
        kv = ckv_ref[pl.ds(pl.multiple_of(jnp.minimum(kb, last_kb) * KB, KB), KB), :]
        s_refs[half][...] = jnp.dot(kv, qall_ref[half], preferred_element_type=F32)

    def softmax(kb, half, with_bias):
        tile = jnp.clip(qi - kb, 0, 2)
        for j in range(hh):
            for g in range(QB // LANES):
                lanes = slice(g * LANES, (g + 1) * LANES)
                cols = slice(j * QB + g * LANES, j * QB + (g + 1) * LANES)
                x = s_refs[half][:, cols] + mb_ref[kb, :, lanes]
                if with_bias:
                    x = x + bias_ref[half * hh + j, tile, :, lanes]
                s_refs[half][:, cols] = x
                m_prev = m_ref[half, :, cols]
                m_blk = jnp.max(x.reshape(KB // CNT_ACC, CNT_ACC, LANES), axis=0)
                m_new = jnp.maximum(m_prev, jnp.max(m_blk, axis=0, keepdims=True))
                al_ref[half, :, cols] = jnp.exp2(m_prev - m_new)
                m_ref[half, :, cols] = m_new
                p_refs[half][:, cols] = jnp.exp2(s_refs[half][:, cols] - m_new).astype(BF16)

    def accumulate(kb, half):
        acc_ref[half] = acc_ref[half] * al_ref[half]
        acc_ref[half] += jnp.dot(ckvt_ref[jnp.maximum(kb, 0)], p_refs[half][...], preferred_element_type=F32)

    def sweep(first_kb, end_kb, with_bias):
        def body(kb, carry):
            logits(kb, 1)
            softmax(kb, 0, with_bias)
            accumulate(kb - 1, 1)
            logits(kb + 1, 0)
            softmax(kb, 1, with_bias)
            accumulate(kb, 0)
            return carry
        lax.fori_loop(first_kb, end_kb, body, 0)

    n_far = jnp.maximum(qi - 1, 0)
    logits(0, 0)
    sweep(0, n_far, False)
    sweep(n_far, nkb, True)
    accumulate(nkb - 1, 1)

    for h in range(H_A):
        half, cols = h // hh, slice((h % hh) * QB, (h % hh + 1) * QB)
        ya_t = jnp.dot(wuvt_ref[h], acc_ref[half, 0:D_C, cols].astype(BF16), preferred_element_type=F32)
        ya = (ya_t / acc_ref[half, D_C:D_C + 1, cols]).T
        z = za_ref[:, h * DH_A:(h + 1) * DH_A].astype(F32)
        y_ref[:, h * DH_A:(h + 1) * DH_A] = (ya * (z * jax.nn.sigmoid(z))).astype(BF16)


def _dsa(main, gate_t, ckv_n, ckv_t, kidx_n, w_uk_t, w_uv_t, bias, B, T):
    nq = T // QB
    return pl.pallas_call(
        _dsa_kernel,
        grid=(B, nq),
        in_specs=[pl.BlockSpec((QB, W_A), lambda b, q: (b * nq + q, C_QA // W_A)),
                  pl.BlockSpec((QB, W_A), lambda b, q: (b * nq + q, C_ZA // W_A)),
                  pl.BlockSpec((QB, H_IDX * D_IDX), lambda b, q: (b * nq + q, C_QI // (H_IDX * D_IDX))),
                  pl.BlockSpec((LANES, QB), lambda b, q: (0, b * nq + q)),
                  pl.BlockSpec((T, D_C), lambda b, q: (b, 0)),
                  pl.BlockSpec((T // KB, D_C + ONES_ROWS, KB), lambda b, q: (b, 0, 0)),
                  pl.BlockSpec((T, D_IDX), lambda b, q: (b, 0)),
                  pl.BlockSpec((H_A, D_C, DH_A), lambda b, q: (0, 0, 0)),
                  pl.BlockSpec((H_A, DH_A, D_C), lambda b, q: (0, 0, 0)),
                  pl.BlockSpec((H_A, 3, KB, QB), lambda b, q: (0, 0, 0, 0))],
        out_specs=pl.BlockSpec((QB, W_A), lambda b, q: (b * nq + q, 0)),
        out_shape=jax.ShapeDtypeStruct((B * T, W_A), BF16),
        scratch_shapes=[pltpu.VMEM((2, D_C, H_A // 2 * QB), BF16),
                        pltpu.VMEM((H_IDX * QB, D_IDX), BF16),
                        pltpu.VMEM((KI, H_IDX * QB), F32),
                        pltpu.VMEM((KI, H_IDX * QB), F32),
                        pltpu.VMEM((T, QB), F32),
                        pltpu.VMEM((T // KB, KB, QB), F32),
                        pltpu.VMEM((KB, H_A // 2 * QB), F32),
                        pltpu.VMEM((KB, H_A // 2 * QB), F32),
                        pltpu.VMEM((KB, H_A // 2 * QB), BF16),
                        pltpu.VMEM((KB, H_A // 2 * QB), BF16),
                        pltpu.VMEM((2, 1, H_A // 2 * QB), F32),
                        pltpu.VMEM((2, 1, H_A // 2 * QB), F32),
                        pltpu.VMEM((2, D_C + ONES_ROWS, H_A // 2 * QB), F32),
                        pltpu.VMEM((H_IDX, QB), F32),
                        pltpu.VMEM((1, QB), F32),
                        pltpu.VMEM((1, QB), F32),
                        pltpu.VMEM((1, QB), jnp.int32)],
        compiler_params=pltpu.CompilerParams(
            dimension_semantics=("arbitrary", "arbitrary"), vmem_limit_bytes=VMEM_LIMIT),
        name="dsa",
    )(main, main, main, gate_t, ckv_n, ckv_t, kidx_n, w_uk_t, w_uv_t, bias)


def _split_dot(tri, x):
    hi = x.astype(BF16)
    lo = (x - hi.astype(F32)).astype(BF16)
    return jnp.dot(tri, hi, preferred_element_type=F32) + jnp.dot(tri, lo, preferred_element_type=F32)


def _log_sigmoid(x):
    return jnp.minimum(x, 0.0) - jnp.log1p(jnp.exp(-jnp.abs(x)))


def _mlstm_kernel(q_ref, k_ref, qh_ref, kh_ref, v_ref, o_ref, z_ref, g_ref, gt_ref,
                  cw_ref, cb_ref, gbr_ref, gbc_ref, ng_ref, y_ref, ct_ref, n_ref, m_ref, xq_ref, xk_ref):
    c = pl.program_id(1)
    L = L_M

    @pl.when(c == 0)
    def _():
        ct_ref[...] = jnp.zeros(ct_ref.shape, F32)
        n_ref[...] = jnp.zeros(n_ref.shape, F32)
        m_ref[...] = jnp.zeros(m_ref.shape, F32)

    def conv_silu(x_ref, halo_ref, lo, xe_ref):
        xe_ref[0:HALO, :] = jnp.where(c > 0, halo_ref[...].astype(F32), 0.0)
        xe_ref[HALO:HALO + L, :] = x_ref[...].astype(F32)
        y = cb_ref[:, lo:lo + H_M * DK_M]
        for j in range(CONV_W):
            off = HALO - (CONV_W - 1) + j
            y = y + cw_ref[j:j + 1, lo:lo + H_M * DK_M] * xe_ref[off:off + L, :]
        return y * jax.nn.sigmoid(y)

    q_all = conv_silu(q_ref, qh_ref, 0, xq_ref)
    k_all = conv_silu(k_ref, kh_ref, H_M * DK_M, xk_ref) * (DK_M ** -0.5)

    gc = g_ref[...] + gbr_ref[...]
    gr = gt_ref[S_IM:S_IM + 2 * H_M, :] + gbc_ref[S_IM:S_IM + 2 * H_M, :]
    r = lax.broadcasted_iota(jnp.int32, (L, L), 0)
    s = lax.broadcasted_iota(jnp.int32, (L, L), 1)
    causal = s <= r
    tri_l = jnp.where(causal, 1.0, 0.0).astype(BF16)
    tri_u = jnp.where(r <= s, 1.0, 0.0).astype(BF16)
    b_cols = _split_dot(tri_l, _log_sigmoid(gc))
    b_rows = jnp.dot(_log_sigmoid(gr).astype(BF16), tri_u, preferred_element_type=F32) \
        + jnp.dot((_log_sigmoid(gr) - _log_sigmoid(gr).astype(BF16).astype(F32)).astype(BF16), tri_u,
                  preferred_element_type=F32)

    for h in range(H_M):
        q = q_all[:, h * DK_M:(h + 1) * DK_M]
        k = k_all[:, h * DK_M:(h + 1) * DK_M]
        v = v_ref[:, h * DV_M:(h + 1) * DV_M]
        qb = q.astype(BF16)
        b_c = b_cols[:, S_FM + h:S_FM + h + 1]
        i_c = gc[:, S_IM + h:S_IM + h + 1]
        b_r = b_rows[H_M + h:H_M + h + 1, :]
        i_r = gr[h:h + 1, :]
        m_prev = m_ref[h]
        ct = ct_ref[h]
        n = n_ref[h]

        log_d = jnp.where(causal, b_c - b_r + i_r, -jnp.inf)
        g = b_c + m_prev
        m_t = jnp.maximum(jnp.max(log_d, axis=-1, keepdims=True), g)
        qk = lax.dot_general(qb, k.astype(BF16), (((1,), (1,)), ((), ())), preferred_element_type=F32)
        s_mat = qk * jnp.exp(log_d - m_t)
        inter = jnp.exp(g - m_t)
        num = jnp.dot(s_mat.astype(BF16), v, preferred_element_type=F32) \
            + inter * jnp.dot(qb, ct.astype(BF16), preferred_element_type=F32)
        den = jnp.sum(s_mat, axis=-1, keepdims=True) + inter * jnp.sum(q * n, axis=-1, keepdims=True)
        hh = num / jnp.maximum(jnp.abs(den), jnp.exp(-m_t))

        b_last = b_c[L - 1:L, :]
        a_r = b_last - b_r + i_r
        m_new = jnp.maximum(b_last + m_prev, jnp.max(a_r, axis=-1, keepdims=True))
        decay = jnp.exp(b_last + m_prev - m_new)
        wgt_c = jnp.exp(b_last - b_c + i_c - m_new)
        kw = k * wgt_c
        ct_ref[h] = decay * ct + jnp.dot(kw.T.astype(BF16), v, preferred_element_type=F32)
        n_ref[h] = decay * n + jnp.sum(kw, axis=0, keepdims=True)
        m_ref[h] = m_new

        mu = jnp.mean(hh, axis=-1, keepdims=True)
        var = jnp.mean(jnp.square(hh - mu), axis=-1, keepdims=True)
        hn = (hh - mu) * lax.rsqrt(var + LN_EPS) * ng_ref[:, h * DV_M:(h + 1) * DV_M]
        og = o_ref[:, h * DV_M:(h + 1) * DV_M].astype(F32)
        zg = z_ref[:, h * DV_M:(h + 1) * DV_M].astype(F32)
        y_ref[:, h * DV_M:(h + 1) * DV_M] = (hn * jax.nn.sigmoid(og) * (zg * jax.nn.sigmoid(zg))).astype(BF16)


def _mlstm(main, small, gate_t, conv_w, conv_b, gb_row, gb_col, norm_g, B, T):
    nc = T // L_M
    hb = L_M // HALO
    qk_w = H_M * DK_M

    def halo_map(col):
        return lambda b, c: (jnp.maximum((b * nc + c) * hb - 1, 0), col)

    return pl.pallas_call(
        _mlstm_kernel,
        grid=(B, nc),
        in_specs=[pl.BlockSpec((L_M, qk_w), lambda b, c: (b * nc + c, C_QM // qk_w)),
                  pl.BlockSpec((L_M, qk_w), lambda b, c: (b * nc + c, C_KM // qk_w)),
                  pl.BlockSpec((HALO, qk_w), halo_map(C_QM // qk_w)),
                  pl.BlockSpec((HALO, qk_w), halo_map(C_KM // qk_w)),
                  pl.BlockSpec((L_M, W_M), lambda b, c: (b * nc + c, C_VM // W_M)),
                  pl.BlockSpec((L_M, W_M), lambda b, c: (b * nc + c, C_OM // W_M)),
                  pl.BlockSpec((L_M, W_M), lambda b, c: (b * nc + c, C_ZM // W_M)),
                  pl.BlockSpec((L_M, LANES), lambda b, c: (b * nc + c, D_C // LANES)),
                  pl.BlockSpec((LANES, L_M), lambda b, c: (0, b * nc + c)),
                  pl.BlockSpec((CONV_W, 2 * qk_w), lambda b, c: (0, 0)),
                  pl.BlockSpec((1, 2 * qk_w), lambda b, c: (0, 0)),
                  pl.BlockSpec((1, LANES), lambda b, c: (0, 0)),
                  pl.BlockSpec((LANES, 1), lambda b, c: (0, 0)),
                  pl.BlockSpec((1, W_M), lambda b, c: (0, 0))],
        out_specs=pl.BlockSpec((L_M, W_M), lambda b, c: (b * nc + c, 0)),
        out_shape=jax.ShapeDtypeStruct((B * T, W_M), BF16),
        scratch_shapes=[pltpu.VMEM((H_M, DK_M, DV_M), F32),
                        pltpu.VMEM((H_M, 1, DK_M), F32),
                        pltpu.VMEM((H_M, 1, 1), F32),
                        pltpu.VMEM((HALO + L_M, H_M * DK_M), F32),
                        pltpu.VMEM((HALO + L_M, H_M * DK_M), F32)],
        compiler_params=pltpu.CompilerParams(
            dimension_semantics=("arbitrary", "arbitrary"), vmem_limit_bytes=VMEM_LIMIT),
        name="mlstm",
    )(main, main, main, main, main, main, main, small, gate_t,
      conv_w, conv_b, gb_row, gb_col, norm_g)


def _out_kernel(ya_ref, ym_ref, x_ref, w_ref, g_ref, b_ref, o_ref):
    y = jnp.dot(ya_ref[...], w_ref[0:W_A, :], preferred_element_type=F32)
    y = y + jnp.dot(ym_ref[...], w_ref[W_A:W_A + W_M, :], preferred_element_type=F32)
    r = ALPHA * x_ref[...] + y
    mu = jnp.mean(r, axis=-1, keepdims=True)
    var = jnp.mean(jnp.square(r - mu), axis=-1, keepdims=True)
    o_ref[...] = (r - mu) * lax.rsqrt(var + LN_EPS) * g_ref[...] + b_ref[...]


def _out(ya, ym, x2d, w_out, ln_g, ln_b, tm=512):
    M = x2d.shape[0]
    return pl.pallas_call(
        _out_kernel,
        grid=(M // tm,),
        in_specs=[pl.BlockSpec((tm, W_A), lambda i: (i, 0)),
                  pl.BlockSpec((tm, W_M), lambda i: (i, 0)),
                  pl.BlockSpec((tm, D_MODEL), lambda i: (i, 0)),
                  pl.BlockSpec((W_A + W_M, D_MODEL), lambda i: (0, 0)),
                  pl.BlockSpec((1, D_MODEL), lambda i: (0, 0)),
                  pl.BlockSpec((1, D_MODEL), lambda i: (0, 0))],
        out_specs=pl.BlockSpec((tm, D_MODEL), lambda i: (i, 0)),
        out_shape=jax.ShapeDtypeStruct((M, D_MODEL), F32),
        compiler_params=pltpu.CompilerParams(
            dimension_semantics=("arbitrary",), vmem_limit_bytes=VMEM_LIMIT),
        name="out",
    )(ya, ym, x2d, w_out, ln_g, ln_b)


def _repack_w_in(w):
    offs = np.cumsum([0, W_A, D_C, W_A, H_IDX * D_IDX, D_IDX, H_IDX, H_M * DK_M, H_M * DK_M,
                      W_M, H_M, H_M, W_M, W_M])
    w = w.astype(BF16)
    seg = [w[:, offs[i]:offs[i + 1]] for i in range(13)]
    (q_a, c_kv, z_a, q_i, k_i, w_i, q_m, k_m, v_m, i_m, f_m, o_m, z_m) = seg
    main = jnp.concatenate([q_a, z_a, q_i, q_m, k_m, v_m, o_m, z_m], axis=1)
    pad = jnp.zeros((w.shape[0], N_SMALL - D_C - D_IDX - H_IDX - 2 * H_M), w.dtype)
    small = jnp.concatenate([c_kv, k_i, w_i, i_m, f_m, pad], axis=1)
    return main, small


def kernel(x, w_in, b_igate, b_fgate, kv_norm_g, w_uk, w_uv, idx_k_ln_g, idx_k_ln_b, rel_bias,
           conv_w, conv_b, mh_norm_g, w_out, ln_g, ln_b):
    B, T, D = x.shape
    assert D == D_MODEL and T % L_M == 0 and T % (2 * KB) == 0 and w_in.shape[0] == 1
    bias = _bias_tiles(rel_bias)
    x2d = x.reshape(B * T, D)
    w_main, w_small = _repack_w_in(w_in[0])
    main, small = _proj(x2d, w_main, w_small)
    ckv_n, ckv_t, kidx_n, gate_t = _prep(small, kv_norm_g[0][None], idx_k_ln_g[0][None], idx_k_ln_b[0][None])
    w_uk_t = jnp.transpose(w_uk[0], (0, 2, 1)).astype(BF16)
    w_uv_t = jnp.transpose(w_uv[0], (0, 2, 1)).astype(BF16)
    ya = _dsa(main, gate_t, ckv_n, ckv_t, kidx_n, w_uk_t, w_uv_t, bias, B, T)
    gb = jnp.zeros((LANES,), F32).at[S_IM:S_IM + H_M].set(b_igate[0]).at[S_FM:S_FM + H_M].set(b_fgate[0])
    ym = _mlstm(main, small, gate_t, conv_w[0], conv_b[0][None], gb[None, :], gb[:, None],
                mh_norm_g[0][None], B, T)
    out = _out(ya, ym, x2d, w_out[0].astype(BF16), ln_g[0][None], ln_b[0][None])
    return out.reshape(B, T, D)
```

```python
import functools
import math

import numpy as np
import jax
import jax.numpy as jnp
from jax import lax
from jax.experimental import pallas as pl
from jax.experimental.pallas import tpu as pltpu

F32 = jnp.float32
BF16 = jnp.bfloat16

D_MODEL = 2048
W_A = 1024
DH_A = 128
H_A = 8
D_C = 256
H_IDX = 16
D_IDX = 64
TOPK = 256
W_M = 1024
H_M = 4
DV_M = 256
DK_M = 128
CONV_W = 4
N_BUCKETS = 32
MAX_DIST = 128
ALPHA = 2.0 ** 0.25
LN_EPS = 1e-5

LANES = 128
SUBLANES = 8
VMEM_LIMIT = 56 * 1024 * 1024

QB = 256
KB = 256
KI = 128
CNT_ROWS = 512
CNT_ACC = 4 * SUBLANES
L_M = 256
HALO = 16
NEG = -1e30
LOG2E = math.log2(math.e)
ONES_ROWS = 16

C_QA, C_ZA, C_QI, C_QM, C_KM, C_VM, C_OM, C_ZM = 0, 1024, 2048, 3072, 3584, 4096, 5120, 6144
N_MAIN = 7168
N_SMALL = 384
S_KI, S_WI, S_IM, S_FM = 0, 64, 80, 84


def _t5_bucket_np(rel):
    max_exact = N_BUCKETS // 2
    n = np.maximum(rel, 0)
    nf = np.maximum(n, 1).astype(np.float32)
    large = max_exact + (np.log(nf / np.float32(max_exact)) / np.float32(math.log(MAX_DIST / max_exact))
                         * np.float32(N_BUCKETS - max_exact)).astype(np.int32)
    large = np.minimum(large, N_BUCKETS - 1)
    return np.where(n < max_exact, n, large).astype(np.int32)


FAR_BUCKET = int(_t5_bucket_np(np.array(2 * KB + 1)))


def _bucket_tiles():
    i = np.arange(QB)[None, :]
    j = np.arange(KB)[:, None]
    t0 = _t5_bucket_np(i - j)
    t1 = _t5_bucket_np(i - j + KB)
    assert (t5 := _t5_bucket_np(np.arange(KB + 1, 4096))).min() == t5.max() == FAR_BUCKET
    return np.stack([t0, t1]).astype(np.int32)


def _bias_kernel(bucket_ref, rb_ref, out_ref):
    h = pl.program_id(0)
    far = rb_ref[FAR_BUCKET, h]
    for k in range(2):
        bk = bucket_ref[k]
        acc = jnp.zeros((KB, QB), F32)
        for b in range(N_BUCKETS):
            acc = jnp.where(bk == b, rb_ref[b, h] - far, acc)
        out_ref[0, k] = acc * LOG2E
    out_ref[0, 2] = jnp.zeros((KB, QB), F32)


def _bias_tiles(rel_bias):
    bucket = jnp.asarray(_bucket_tiles())
    return pl.pallas_call(
        _bias_kernel,
        grid=(H_A,),
        in_specs=[pl.BlockSpec((2, KB, QB), lambda h: (0, 0, 0)),
                  pl.BlockSpec(memory_space=pltpu.SMEM)],
        out_specs=pl.BlockSpec((1, 3, KB, QB), lambda h: (h, 0, 0, 0)),
        out_shape=jax.ShapeDtypeStruct((H_A, 3, KB, QB), F32),
        name="bias_tiles",
    )(bucket, rel_bias)


def _proj_kernel(x_ref, w_ref, ws_ref, o_ref, os_ref, xb_ref):
    @pl.when(pl.program_id(1) == 0)
    def _():
        xb_ref[...] = x_ref[...].astype(BF16)
        os_ref[...] = jnp.dot(xb_ref[...], ws_ref[...], preferred_element_type=F32)

    o_ref[...] = jnp.dot(xb_ref[...], w_ref[...], preferred_element_type=F32).astype(BF16)


def _proj(x2d, w_main, w_small, tm=1024, tn=1024):
    M = x2d.shape[0]
    return pl.pallas_call(
        _proj_kernel,
        grid=(M // tm, N_MAIN // tn),
        in_specs=[pl.BlockSpec((tm, D_MODEL), lambda i, j: (i, 0)),
                  pl.BlockSpec((D_MODEL, tn), lambda i, j: (0, j)),
                  pl.BlockSpec((D_MODEL, N_SMALL), lambda i, j: (0, 0))],
        out_specs=[pl.BlockSpec((tm, tn), lambda i, j: (i, j)),
                   pl.BlockSpec((tm, N_SMALL), lambda i, j: (i, 0))],
        out_shape=[jax.ShapeDtypeStruct((M, N_MAIN), BF16),
                   jax.ShapeDtypeStruct((M, N_SMALL), F32)],
        scratch_shapes=[pltpu.VMEM((tm, D_MODEL), BF16)],
        compiler_params=pltpu.CompilerParams(
            dimension_semantics=("arbitrary", "arbitrary"), vmem_limit_bytes=VMEM_LIMIT),
        name="proj",
    )(x2d, w_main, w_small)


def _prep_kernel(s_ref, kvg_ref, ig_ref, ib_ref, ckv_ref, ckvt_ref, kidx_ref, gt_ref):
    c = s_ref[:, 0:D_C]
    c = c * lax.rsqrt(jnp.mean(c * c, axis=-1, keepdims=True) + LN_EPS) * kvg_ref[...]
    ckv_ref[...] = c.astype(BF16)
    for r in range(ckvt_ref.shape[0]):
        ckvt_ref[r, 0:D_C, :] = c[r * KB:(r + 1) * KB, :].T.astype(BF16)
        ckvt_ref[r, D_C:D_C + ONES_ROWS, :] = jnp.ones((ONES_ROWS, KB), BF16)
    tile = s_ref[:, D_C:D_C + LANES]
    k = tile[:, S_KI:S_KI + D_IDX]
    mu = jnp.mean(k, axis=-1, keepdims=True)
    var = jnp.mean(jnp.square(k - mu), axis=-1, keepdims=True)
    kidx_ref[...] = ((k - mu) * lax.rsqrt(var + LN_EPS) * ig_ref[...] + ib_ref[...]).astype(BF16)
    gt_ref[...] = tile.T


def _prep(small, kv_g, idx_g, idx_b, tm=1024):
    M = small.shape[0]
    return pl.pallas_call(
        _prep_kernel,
        grid=(M // tm,),
        in_specs=[pl.BlockSpec((tm, N_SMALL), lambda i: (i, 0)),
                  pl.BlockSpec((1, D_C), lambda i: (0, 0)),
                  pl.BlockSpec((1, D_IDX), lambda i: (0, 0)),
                  pl.BlockSpec((1, D_IDX), lambda i: (0, 0))],
        out_specs=[pl.BlockSpec((tm, D_C), lambda i: (i, 0)),
                   pl.BlockSpec((tm // KB, D_C + ONES_ROWS, KB), lambda i: (i, 0, 0)),
                   pl.BlockSpec((tm, D_IDX), lambda i: (i, 0)),
                   pl.BlockSpec((LANES, tm), lambda i: (0, i))],
        out_shape=[jax.ShapeDtypeStruct((M, D_C), BF16),
                   jax.ShapeDtypeStruct((M // KB, D_C + ONES_ROWS, KB), BF16),
                   jax.ShapeDtypeStruct((M, D_IDX), BF16),
                   jax.ShapeDtypeStruct((LANES, M), F32)],
        name="prep",
    )(small, kv_g, idx_g, idx_b)


def _key_to_float(key):
    bits = jnp.where(key < 0, key ^ jnp.int32(0x7FFFFFFF), key)
    return lax.bitcast_convert_type(bits, F32)


def _dsa_kernel(qa_ref, za_ref, qi_ref, gt_ref, ckv_ref, ckvt_ref, kidx_ref, wukt_ref, wuvt_ref, bias_ref,
                y_ref, qall_ref, qr_ref, ha_ref, hb_ref, sc_ref, mb_ref, sa_ref, sb_ref, pa_ref, pb_ref,
                ta_ref, tb_ref, m_ref, al_ref, acc_ref, w_ref, cand_ref, thr_ref, cut_ref):
    qi = pl.program_id(1)
    nkb = qi + 1
    last_kb = mb_ref.shape[0] - 1

    for h in range(H_A):
        ql = lax.dot_general(wukt_ref[h], qa_ref[:, h * DH_A:(h + 1) * DH_A], (((1,), (1,)), ((), ())),
                             preferred_element_type=F32)
        qall_ref[h // (H_A // 2), :, (h % (H_A // 2)) * QB:(h % (H_A // 2) + 1) * QB] = (
            ql * (DH_A ** -0.5 * LOG2E)).astype(BF16)


    @pl.when(qi * QB < TOPK)
    def _():
        krow = lax.broadcasted_iota(jnp.int32, (KB, QB), 0)
        qcol = lax.broadcasted_iota(jnp.int32, (KB, QB), 1)
        mb_ref[0] = jnp.where(krow <= qcol, 0.0, NEG).astype(F32)

    @pl.when(qi * QB >= TOPK)
    def _():
        for h in range(H_IDX):
            qr_ref[h * QB:(h + 1) * QB, :] = qi_ref[:, h * D_IDX:(h + 1) * D_IDX]
        w_ref[...] = gt_ref[S_WI:S_WI + H_IDX, :] * ((D_IDX ** -0.5) * (H_IDX ** -0.5))
        krow = lax.broadcasted_iota(jnp.int32, (KI, LANES), 0)
        qcol = lax.broadcasted_iota(jnp.int32, (KI, LANES), 1)
        n_ki = nkb * (KB // KI)
        last_ki = sc_ref.shape[0] // KI - 1

        def head_dots(ki, dst_ref):
            k = kidx_ref[pl.ds(pl.multiple_of(jnp.minimum(ki, last_ki) * KI, KI), KI), :]
            dst_ref[...] = lax.dot_general(k, qr_ref[...], (((1,), (1,)), ((), ())),
                                           preferred_element_type=F32)

        def reduce_heads(src_ref, ki):
            for g in range(QB // LANES):
                lanes = slice(g * LANES, (g + 1) * LANES)
                acc = jnp.zeros((KI, LANES), F32)
                for h in range(H_IDX):
                    acc = acc + (jnp.maximum(src_ref[:, h * QB + g * LANES:h * QB + (g + 1) * LANES], 0.0)
                                 * w_ref[h:h + 1, lanes])
                sc_ref[pl.ds(pl.multiple_of(ki * KI, KI), KI), lanes] = jnp.where(
                    krow + (ki * KI - qi * QB - g * LANES) <= qcol, acc, -jnp.inf)

        pad_kb = jnp.minimum(nkb, last_kb)
        sc_ref[pl.ds(pl.multiple_of(pad_kb * KB, KB), KB), :] = jnp.full((KB, QB), -jnp.inf, F32)

        head_dots(0, ha_ref)

        def sc_body(j, carry):
            head_dots(2 * j + 1, hb_ref)
            reduce_heads(ha_ref, 2 * j)
            head_dots(2 * j + 2, ha_ref)
            reduce_heads(hb_ref, 2 * j + 1)
            return carry
        lax.fori_loop(0, n_ki // 2, sc_body, 0)

        n_cnt = (nkb * KB + CNT_ROWS - 1) // CNT_ROWS

        def count_where(pred):
            def body(c, acc):
                parts = []
                for g in range(QB // LANES):
                    lanes = slice(g * LANES, (g + 1) * LANES)
                    blk = sc_ref[pl.ds(pl.multiple_of(c * CNT_ROWS, CNT_ROWS), CNT_ROWS), lanes]
                    hit = jnp.where(pred(blk, c * CNT_ROWS, lanes), 1.0, 0.0).astype(F32)
                    parts.append(jnp.sum(hit.reshape(CNT_ROWS // CNT_ACC, CNT_ACC, LANES), axis=0))
                return acc + jnp.concatenate(parts, axis=1)
            acc = lax.fori_loop(0, n_cnt, body, jnp.zeros((CNT_ACC, QB), F32))
            return jnp.sum(acc, axis=0, keepdims=True)

        def bit_body(i, carry):
            u, c_ge = carry
            trial = u | lax.shift_left(jnp.int32(1), 31 - i)
            cand_ref[...] = _key_to_float(trial ^ jnp.int32(-2 ** 31))
            cnt = count_where(lambda blk, row0, lanes: blk >= cand_ref[:, lanes])
            ok = cnt >= float(TOPK)
            return jnp.where(ok, trial, u), jnp.where(ok, cnt, c_ge)
        u, c_ge = lax.fori_loop(0, 32, bit_body, (jnp.zeros((1, QB), jnp.int32),
                                                  jnp.full((1, QB), float(sc_ref.shape[0]), F32)))
        thr_ref[...] = _key_to_float(u ^ jnp.int32(-2 ** 31))
        has_ties = jnp.max(c_ge) > float(TOPK)

        @pl.when(jnp.logical_not(has_ties))
        def _():
            def mb_body(kb, carry):
                blk = sc_ref[pl.ds(pl.multiple_of(kb * KB, KB), KB), :]
                mb_ref[kb] = jnp.where(blk >= thr_ref[...], 0.0, NEG).astype(F32)
                return carry
            lax.fori_loop(0, nkb, mb_body, 0)

        @pl.when(has_ties)
        def _():
            c_gt = count_where(lambda blk, row0, lanes: blk > thr_ref[:, lanes])
            need = float(TOPK) - c_gt
            rows = lax.broadcasted_iota(jnp.int32, (CNT_ROWS, LANES), 0)
            n_bits = (sc_ref.shape[0] - 1).bit_length()

            def idx_body(i, cut):
                trial = cut | lax.shift_left(jnp.int32(1), n_bits - 1 - i)
                cut_ref[...] = trial
                before = count_where(lambda blk, row0, lanes: (blk == thr_ref[:, lanes])
                                     & (rows + row0 < cut_ref[:, lanes]))
                return jnp.where(before < need, trial, cut)
            cut_ref[...] = lax.fori_loop(0, n_bits, idx_body, jnp.zeros((1, QB), jnp.int32))

            def mb_body(kb, carry):
                for g in range(QB // LANES):
                    lanes = slice(g * LANES, (g + 1) * LANES)
                    blk = sc_ref[pl.ds(pl.multiple_of(kb * KB, KB), KB), lanes]
                    thr = thr_ref[:, lanes]
                    keep = (blk > thr) | ((blk == thr) & (rows[0:KB] + kb * KB <= cut_ref[:, lanes]))
                    mb_ref[kb, :, lanes] = jnp.where(keep, 0.0, NEG).astype(F32)
                return carry
            lax.fori_loop(0, nkb, mb_body, 0)

    m_ref[...] = jnp.full(m_ref.shape, NEG, F32)
    acc_ref[...] = jnp.zeros(acc_ref.shape, F32)

    hh = H_A // 2
    s_refs, p_refs, t_refs = (sa_ref, sb_ref), (pa_ref, pb_ref), (ta_ref, tb_ref)
    pb_ref[...] = jnp.zeros(pb_ref.shape, BF16)
    al_ref[...] = jnp.ones(al_ref.shape, F32)

    def logits(kb, half):
        kv = ckv_ref[pl.ds(pl.multiple_of(jnp.minimum(kb, last_kb) * KB, KB), KB), :]
        s_refs[half][...] = jnp.dot(kv, qall_ref[half], preferred_element_type=F32)

    def softmax(kb, half, with_bias):
        tile = jnp.clip(qi - kb, 0, 2)
        for j in range(hh):
            for g in range(QB // LANES):
                lanes = slice(g * LANES, (g + 1) * LANES)
                cols = slice(j * QB + g * LANES, j * QB + (g + 1) * LANES)
                x = s_refs[half][:, cols] + mb_ref[kb, :, lanes]
                if with_bias:
                    x = x + bias_ref[half * hh + j, tile, :, lanes]
                m_prev = m_ref[half, :, cols]
                m_blk = jnp.max(x.reshape(KB // CNT_ACC, CNT_ACC, LANES), axis=0)
                m_new = jnp.maximum(m_prev, jnp.max(m_blk, axis=0, keepdims=True))
                al_ref[half, :, cols] = jnp.exp2(m_prev - m_new)
                m_ref[half, :, cols] = m_new
                p_refs[half][:, cols] = jnp.exp2(x - m_new).astype(BF16)

    def accumulate(kb, half):
        t_refs[half][...] = jnp.dot(ckvt_ref[jnp.maximum(kb, 0)], p_refs[half][...],
                                    preferred_element_type=F32)
        acc_ref[half] = acc_ref[half] * al_ref[half] + t_refs[half][...]

    def sweep(first_kb, end_kb, with_bias):
        def body(kb, carry):
            logits(kb, 1)
            softmax(kb, 0, with_bias)
            accumulate(kb - 1, 1)
            logits(kb + 1, 0)
            softmax(kb, 1, with_bias)
            accumulate(kb, 0)
            return carry
        lax.fori_loop(first_kb, end_kb, body, 0)

    n_far = jnp.maximum(qi - 1, 0)
    logits(0, 0)
    sweep(0, n_far, False)
    sweep(n_far, nkb, True)
    accumulate(nkb - 1, 1)

    for h in range(H_A):
        half, cols = h // hh, slice((h % hh) * QB, (h % hh + 1) * QB)
        ya_t = jnp.dot(wuvt_ref[h], acc_ref[half, 0:D_C, cols].astype(BF16), preferred_element_type=F32)
        ya = (ya_t / acc_ref[half, D_C:D_C + 1, cols]).T
        z = za_ref[:, h * DH_A:(h + 1) * DH_A].astype(F32)
        y_ref[:, h * DH_A:(h + 1) * DH_A] = (ya * (z * jax.nn.sigmoid(z))).astype(BF16)


def _dsa(main, gate_t, ckv_n, ckv_t, kidx_n, w_uk_t, w_uv_t, bias, B, T):
    nq = T // QB
    return pl.pallas_call(
        _dsa_kernel,
        grid=(B, nq),
        in_specs=[pl.BlockSpec((QB, W_A), lambda b, q: (b * nq + q, C_QA // W_A)),
                  pl.BlockSpec((QB, W_A), lambda b, q: (b * nq + q, C_ZA // W_A)),
                  pl.BlockSpec((QB, H_IDX * D_IDX), lambda b, q: (b * nq + q, C_QI // (H_IDX * D_IDX))),
                  pl.BlockSpec((LANES, QB), lambda b, q: (0, b * nq + q)),
                  pl.BlockSpec((T, D_C), lambda b, q: (b, 0)),
                  pl.BlockSpec((T // KB, D_C + ONES_ROWS, KB), lambda b, q: (b, 0, 0)),
                  pl.BlockSpec((T, D_IDX), lambda b, q: (b, 0)),
                  pl.BlockSpec((H_A, D_C, DH_A), lambda b, q: (0, 0, 0)),
                  pl.BlockSpec((H_A, DH_A, D_C), lambda b, q: (0, 0, 0)),
                  pl.BlockSpec((H_A, 3, KB, QB), lambda b, q: (0, 0, 0, 0))],
        out_specs=pl.BlockSpec((QB, W_A), lambda b, q: (b * nq + q, 0)),
        out_shape=jax.ShapeDtypeStruct((B * T, W_A), BF16),
        scratch_shapes=[pltpu.VMEM((2, D_C, H_A // 2 * QB), BF16),
                        pltpu.VMEM((H_IDX * QB, D_IDX), BF16),
                        pltpu.VMEM((KI, H_IDX * QB), F32),
                        pltpu.VMEM((KI, H_IDX * QB), F32),
                        pltpu.VMEM((T, QB), F32),
                        pltpu.VMEM((T // KB, KB, QB), F32),
                        pltpu.VMEM((KB, H_A // 2 * QB), F32),
                        pltpu.VMEM((KB, H_A // 2 * QB), F32),
                        pltpu.VMEM((KB, H_A // 2 * QB), BF16),
                        pltpu.VMEM((KB, H_A // 2 * QB), BF16),
                        pltpu.VMEM((D_C + ONES_ROWS, H_A // 2 * QB), F32),
                        pltpu.VMEM((D_C + ONES_ROWS, H_A // 2 * QB), F32),
                        pltpu.VMEM((2, 1, H_A // 2 * QB), F32),
                        pltpu.VMEM((2, 1, H_A // 2 * QB), F32),
                        pltpu.VMEM((2, D_C + ONES_ROWS, H_A // 2 * QB), F32),
                        pltpu.VMEM((H_IDX, QB), F32),
                        pltpu.VMEM((1, QB), F32),
                        pltpu.VMEM((1, QB), F32),
                        pltpu.VMEM((1, QB), jnp.int32)],
        compiler_params=pltpu.CompilerParams(
            dimension_semantics=("arbitrary", "arbitrary"), vmem_limit_bytes=VMEM_LIMIT),
        name="dsa",
    )(main, main, main, gate_t, ckv_n, ckv_t, kidx_n, w_uk_t, w_uv_t, bias)


def _split_dot(tri, x):
    hi = x.astype(BF16)
    lo = (x - hi.astype(F32)).astype(BF16)
    return jnp.dot(tri, hi, preferred_element_type=F32) + jnp.dot(tri, lo, preferred_element_type=F32)


def _log_sigmoid(x):
    return jnp.minimum(x, 0.0) - jnp.log1p(jnp.exp(-jnp.abs(x)))


def _mlstm_kernel(q_ref, k_ref, qh_ref, kh_ref, v_ref, o_ref, z_ref, g_ref, gt_ref,
                  cw_ref, cb_ref, gbr_ref, gbc_ref, ng_ref, y_ref, ct_ref, n_ref, m_ref, xq_ref, xk_ref):
    c = pl.program_id(1)
    L = L_M

    @pl.when(c == 0)
    def _():
        ct_ref[...] = jnp.zeros(ct_ref.shape, F32)
        n_ref[...] = jnp.zeros(n_ref.shape, F32)
        m_ref[...] = jnp.zeros(m_ref.shape, F32)

    def conv_silu(x_ref, halo_ref, lo, xe_ref):
        xe_ref[0:HALO, :] = jnp.where(c > 0, halo_ref[...].astype(F32), 0.0)
        xe_ref[HALO:HALO + L, :] = x_ref[...].astype(F32)
        y = cb_ref[:, lo:lo + H_M * DK_M]
        for j in range(CONV_W):
            off = HALO - (CONV_W - 1) + j
            y = y + cw_ref[j:j + 1, lo:lo + H_M * DK_M] * xe_ref[off:off + L, :]
        return y * jax.nn.sigmoid(y)

    q_all = conv_silu(q_ref, qh_ref, 0, xq_ref)
    k_all = conv_silu(k_ref, kh_ref, H_M * DK_M, xk_ref) * (DK_M ** -0.5)

    gc = g_ref[...] + gbr_ref[...]
    gr = gt_ref[S_IM:S_IM + 2 * H_M, :] + gbc_ref[S_IM:S_IM + 2 * H_M, :]
    r = lax.broadcasted_iota(jnp.int32, (L, L), 0)
    s = lax.broadcasted_iota(jnp.int32, (L, L), 1)
    causal = s <= r
    tri_l = jnp.where(causal, 1.0, 0.0).astype(BF16)
    tri_u = jnp.where(r <= s, 1.0, 0.0).astype(BF16)
    b_cols = _split_dot(tri_l, _log_sigmoid(gc))
    b_rows = jnp.dot(_log_sigmoid(gr).astype(BF16), tri_u, preferred_element_type=F32) \
        + jnp.dot((_log_sigmoid(gr) - _log_sigmoid(gr).astype(BF16).astype(F32)).astype(BF16), tri_u,
                  preferred_element_type=F32)

    for h in range(H_M):
        q = q_all[:, h * DK_M:(h + 1) * DK_M]
        k = k_all[:, h * DK_M:(h + 1) * DK_M]
        v = v_ref[:, h * DV_M:(h + 1) * DV_M]
        qb = q.astype(BF16)
        b_c = b_cols[:, S_FM + h:S_FM + h + 1]
        i_c = gc[:, S_IM + h:S_IM + h + 1]
        b_r = b_rows[H_M + h:H_M + h + 1, :]
        i_r = gr[h:h + 1, :]
        m_prev = m_ref[h]
        ct = ct_ref[h]
        n = n_ref[h]

        log_d = jnp.where(causal, b_c - b_r + i_r, -jnp.inf)
        g = b_c + m_prev
        m_t = jnp.maximum(jnp.max(log_d, axis=-1, keepdims=True), g)
        qk = lax.dot_general(qb, k.astype(BF16), (((1,), (1,)), ((), ())), preferred_element_type=F32)
        s_mat = qk * jnp.exp(log_d - m_t)
        inter = jnp.exp(g - m_t)
        num = jnp.dot(s_mat.astype(BF16), v, preferred_element_type=F32) \
            + inter * jnp.dot(qb, ct.astype(BF16), preferred_element_type=F32)
        den = jnp.sum(s_mat, axis=-1, keepdims=True) + inter * jnp.sum(q * n, axis=-1, keepdims=True)
        hh = num / jnp.maximum(jnp.abs(den), jnp.exp(-m_t))

        b_last = b_c[L - 1:L, :]
        a_r = b_last - b_r + i_r
        m_new = jnp.maximum(b_last + m_prev, jnp.max(a_r, axis=-1, keepdims=True))
        decay = jnp.exp(b_last + m_prev - m_new)
        wgt_c = jnp.exp(b_last - b_c + i_c - m_new)
        kw = k * wgt_c
        ct_ref[h] = decay * ct + jnp.dot(kw.T.astype(BF16), v, preferred_element_type=F32)
        n_ref[h] = decay * n + jnp.sum(kw, axis=0, keepdims=True)
        m_ref[h] = m_new

        mu = jnp.mean(hh, axis=-1, keepdims=True)
        var = jnp.mean(jnp.square(hh - mu), axis=-1, keepdims=True)
        hn = (hh - mu) * lax.rsqrt(var + LN_EPS) * ng_ref[:, h * DV_M:(h + 1) * DV_M]
        og = o_ref[:, h * DV_M:(h + 1) * DV_M].astype(F32)
        zg = z_ref[:, h * DV_M:(h + 1) * DV_M].astype(F32)
        y_ref[:, h * DV_M:(h + 1) * DV_M] = (hn * jax.nn.sigmoid(og) * (zg * jax.nn.sigmoid(zg))).astype(BF16)


def _mlstm(main, small, gate_t, conv_w, conv_b, gb_row, gb_col, norm_g, B, T):
    nc = T // L_M
    hb = L_M // HALO
    qk_w = H_M * DK_M

    def halo_map(col):
        return lambda b, c: (jnp.maximum((b * nc + c) * hb - 1, 0), col)

    return pl.pallas_call(
        _mlstm_kernel,
        grid=(B, nc),
        in_specs=[pl.BlockSpec((L_M, qk_w), lambda b, c: (b * nc + c, C_QM // qk_w)),
                  pl.BlockSpec((L_M, qk_w), lambda b, c: (b * nc + c, C_KM // qk_w)),
                  pl.BlockSpec((HALO, qk_w), halo_map(C_QM // qk_w)),
                  pl.BlockSpec((HALO, qk_w), halo_map(C_KM // qk_w)),
                  pl.BlockSpec((L_M, W_M), lambda b, c: (b * nc + c, C_VM // W_M)),
                  pl.BlockSpec((L_M, W_M), lambda b, c: (b * nc + c, C_OM // W_M)),
                  pl.BlockSpec((L_M, W_M), lambda b, c: (b * nc + c, C_ZM // W_M)),
                  pl.BlockSpec((L_M, LANES), lambda b, c: (b * nc + c, D_C // LANES)),
                  pl.BlockSpec((LANES, L_M), lambda b, c: (0, b * nc + c)),
                  pl.BlockSpec((CONV_W, 2 * qk_w), lambda b, c: (0, 0)),
                  pl.BlockSpec((1, 2 * qk_w), lambda b, c: (0, 0)),
                  pl.BlockSpec((1, LANES), lambda b, c: (0, 0)),
                  pl.BlockSpec((LANES, 1), lambda b, c: (0, 0)),
                  pl.BlockSpec((1, W_M), lambda b, c: (0, 0))],
        out_specs=pl.BlockSpec((L_M, W_M), lambda b, c: (b * nc + c, 0)),
        out_shape=jax.ShapeDtypeStruct((B * T, W_M), BF16),
        scratch_shapes=[pltpu.VMEM((H_M, DK_M, DV_M), F32),
                        pltpu.VMEM((H_M, 1, DK_M), F32),
                        pltpu.VMEM((H_M, 1, 1), F32),
                        pltpu.VMEM((HALO + L_M, H_M * DK_M), F32),
                        pltpu.VMEM((HALO + L_M, H_M * DK_M), F32)],
        compiler_params=pltpu.CompilerParams(
            dimension_semantics=("arbitrary", "arbitrary"), vmem_limit_bytes=VMEM_LIMIT),
        name="mlstm",
    )(main, main, main, main, main, main, main, small, gate_t,
      conv_w, conv_b, gb_row, gb_col, norm_g)


def _out_kernel(ya_ref, ym_ref, x_ref, w_ref, g_ref, b_ref, o_ref):
    y = jnp.dot(ya_ref[...], w_ref[0:W_A, :], preferred_element_type=F32)
    y = y + jnp.dot(ym_ref[...], w_ref[W_A:W_A + W_M, :], preferred_element_type=F32)
    r = ALPHA * x_ref[...] + y
    mu = jnp.mean(r, axis=-1, keepdims=True)
    var = jnp.mean(jnp.square(r - mu), axis=-1, keepdims=True)
    o_ref[...] = (r - mu) * lax.rsqrt(var + LN_EPS) * g_ref[...] + b_ref[...]


def _out(ya, ym, x2d, w_out, ln_g, ln_b, tm=512):
    M = x2d.shape[0]
    return pl.pallas_call(
        _out_kernel,
        grid=(M // tm,),
        in_specs=[pl.BlockSpec((tm, W_A), lambda i: (i, 0)),
                  pl.BlockSpec((tm, W_M), lambda i: (i, 0)),
                  pl.BlockSpec((tm, D_MODEL), lambda i: (i, 0)),
                  pl.BlockSpec((W_A + W_M, D_MODEL), lambda i: (0, 0)),
                  pl.BlockSpec((1, D_MODEL), lambda i: (0, 0)),
                  pl.BlockSpec((1, D_MODEL), lambda i: (0, 0))],
        out_specs=pl.BlockSpec((tm, D_MODEL), lambda i: (i, 0)),
        out_shape=jax.ShapeDtypeStruct((M, D_MODEL), F32),
        compiler_params=pltpu.CompilerParams(
            dimension_semantics=("arbitrary",), vmem_limit_bytes=VMEM_LIMIT),
        name="out",
    )(ya, ym, x2d, w_out, ln_g, ln_b)


_W_IN_SEGS = (("q_a", W_A), ("c_kv", D_C), ("z_a", W_A), ("q_i", H_IDX * D_IDX), ("k_i", D_IDX),
              ("w_i", H_IDX), ("q_m", H_M * DK_M), ("k_m", H_M * DK_M), ("v_m", W_M), ("i_m", H_M),
              ("f_m", H_M), ("o_m", W_M), ("z_m", W_M))
_MAIN_ORDER = ("q_a", "z_a", "q_i", "q_m", "k_m", "v_m", "o_m", "z_m")
_SMALL_ORDER = ("c_kv", "k_i", "w_i", "i_m", "f_m")


def _repack_kernel(w_ref, main_ref, small_ref):
    src, off = {}, 0
    for name, width in _W_IN_SEGS:
        src[name] = (off, width)
        off += width
    for dst_ref, order in ((main_ref, _MAIN_ORDER), (small_ref, _SMALL_ORDER)):
        dst = 0
        for name in order:
            lo, width = src[name]
            dst_ref[:, dst:dst + width] = w_ref[0, :, lo:lo + width].astype(BF16)
            dst += width
        if dst < dst_ref.shape[1]:
            dst_ref[:, dst:] = jnp.zeros((dst_ref.shape[0], dst_ref.shape[1] - dst), BF16)


def _repack_w_in(w_in, tr=256):
    n_cols = sum(width for _, width in _W_IN_SEGS)
    return pl.pallas_call(
        _repack_kernel,
        grid=(D_MODEL // tr,),
        in_specs=[pl.BlockSpec((1, tr, n_cols), lambda i: (0, i, 0))],
        out_specs=[pl.BlockSpec((tr, N_MAIN), lambda i: (i, 0)),
                   pl.BlockSpec((tr, N_SMALL), lambda i: (i, 0))],
        out_shape=[jax.ShapeDtypeStruct((D_MODEL, N_MAIN), BF16),
                   jax.ShapeDtypeStruct((D_MODEL, N_SMALL), BF16)],
        compiler_params=pltpu.CompilerParams(
            dimension_semantics=("arbitrary",), vmem_limit_bytes=VMEM_LIMIT),
        name="repack",
    )(w_in)


def kernel(x, w_in, b_igate, b_fgate, kv_norm_g, w_uk, w_uv, idx_k_ln_g, idx_k_ln_b, rel_bias,
           conv_w, conv_b, mh_norm_g, w_out, ln_g, ln_b):
    B, T, D = x.shape
    assert D == D_MODEL and T % L_M == 0 and T % (2 * KB) == 0 and w_in.shape[0] == 1
    bias = _bias_tiles(rel_bias)
    x2d = x.reshape(B * T, D)
    w_main, w_small = _repack_w_in(w_in)
    main, small = _proj(x2d, w_main, w_small)
    ckv_n, ckv_t, kidx_n, gate_t = _prep(small, kv_norm_g[0][None], idx_k_ln_g[0][None], idx_k_ln_b[0][None])
    w_uk_t = jnp.transpose(w_uk[0], (0, 2, 1)).astype(BF16)
    w_uv_t = jnp.transpose(w_uv[0], (0, 2, 1)).astype(BF16)
    ya = _dsa(main, gate_t, ckv_n, ckv_t, kidx_n, w_uk_t, w_uv_t, bias, B, T)
    gb = jnp.zeros((LANES,), F32).at[S_IM:S_IM + H_M].set(b_igate[0]).at[S_FM:S_FM + H_M].set(b_fgate[0])
    ym = _mlstm(main, small, gate_t, conv_w[0], conv_b[0][None], gb[None, :], gb[:, None],
                mh_norm_g[0][None], B, T)
    out = _out(ya, ym, x2d, w_out[0].astype(BF16), ln_g[0][None], ln_b[0][None])
    return out.reshape(B, T, D)
```

```python
import functools
import math

import numpy as np
import jax
import jax.numpy as jnp
from jax import lax
from jax.experimental import pallas as pl
from jax.experimental.pallas import tpu as pltpu

F32 = jnp.float32
BF16 = jnp.bfloat16

D_MODEL = 2048
W_A = 1024
DH_A = 128
H_A = 8
D_C = 256
H_IDX = 16
D_IDX = 64
TOPK = 256
W_M = 1024
H_M = 4
DV_M = 256
DK_M = 128
CONV_W = 4
N_BUCKETS = 32
MAX_DIST = 128
ALPHA = 2.0 ** 0.25
LN_EPS = 1e-5

LANES = 128
SUBLANES = 8
VMEM_LIMIT = 56 * 1024 * 1024

QB = 256
KB = 256
KI = 128
CNT_ROWS = 512
CNT_ACC = 4 * SUBLANES
L_M = 256
HALO = 16
NEG = -1e30
LOG2E = math.log2(math.e)
ONES_ROWS = 16

C_QA, C_ZA, C_QI, C_QM, C_KM, C_VM, C_OM, C_ZM = 0, 1024, 2048, 3072, 3584, 4096, 5120, 6144
N_MAIN = 7168
N_SMALL = 384
S_KI, S_WI, S_IM, S_FM = 0, 64, 80, 84


def _t5_bucket_np(rel):
    max_exact = N_BUCKETS // 2
    n = np.maximum(rel, 0)
    nf = np.maximum(n, 1).astype(np.float32)
    large = max_exact + (np.log(nf / np.float32(max_exact)) / np.float32(math.log(MAX_DIST / max_exact))
                         * np.float32(N_BUCKETS - max_exact)).astype(np.int32)
    large = np.minimum(large, N_BUCKETS - 1)
    return np.where(n < max_exact, n, large).astype(np.int32)


FAR_BUCKET = int(_t5_bucket_np(np.array(2 * KB + 1)))


def _bucket_tiles():
    i = np.arange(QB)[None, :]
    j = np.arange(KB)[:, None]
    t0 = _t5_bucket_np(i - j)
    t1 = _t5_bucket_np(i - j + KB)
    assert (t5 := _t5_bucket_np(np.arange(KB + 1, 4096))).min() == t5.max() == FAR_BUCKET
    return np.stack([t0, t1]).astype(np.int32)


def _bias_kernel(bucket_ref, rb_ref, out_ref):
    h = pl.program_id(0)
    far = rb_ref[FAR_BUCKET, h]
    for k in range(2):
        bk = bucket_ref[k]
        acc = jnp.zeros((KB, QB), F32)
        for b in range(N_BUCKETS):
            acc = jnp.where(bk == b, rb_ref[b, h] - far, acc)
        out_ref[0, k] = acc * LOG2E
    out_ref[0, 2] = jnp.zeros((KB, QB), F32)


def _bias_tiles(rel_bias):
    bucket = jnp.asarray(_bucket_tiles())
    return pl.pallas_call(
        _bias_kernel,
        grid=(H_A,),
        in_specs=[pl.BlockSpec((2, KB, QB), lambda h: (0, 0, 0)),
                  pl.BlockSpec(memory_space=pltpu.SMEM)],
        out_specs=pl.BlockSpec((1, 3, KB, QB), lambda h: (h, 0, 0, 0)),
        out_shape=jax.ShapeDtypeStruct((H_A, 3, KB, QB), F32),
        name="bias_tiles",
    )(bucket, rel_bias)


_NT = (((1,), (1,)), ((), ()))


def _proj_kernel(x_ref, w_ref, ws_ref, o_ref, os_ref, xb_ref):
    @pl.when(pl.program_id(1) == 0)
    def _():
        xb_ref[...] = x_ref[...].astype(BF16)
        os_ref[...] = lax.dot_general(xb_ref[...], ws_ref[...], _NT, preferred_element_type=F32)

    o_ref[...] = lax.dot_general(xb_ref[...], w_ref[...], _NT, preferred_element_type=F32).astype(BF16)


def _proj(x2d, w_main, w_small, tm=1024, tn=1024):
    M = x2d.shape[0]
    return pl.pallas_call(
        _proj_kernel,
        grid=(M // tm, N_MAIN // tn),
        in_specs=[pl.BlockSpec((tm, D_MODEL), lambda i, j: (i, 0)),
                  pl.BlockSpec((tn, D_MODEL), lambda i, j: (j, 0)),
                  pl.BlockSpec((N_SMALL, D_MODEL), lambda i, j: (0, 0))],
        out_specs=[pl.BlockSpec((tm, tn), lambda i, j: (i, j)),
                   pl.BlockSpec((tm, N_SMALL), lambda i, j: (i, 0))],
        out_shape=[jax.ShapeDtypeStruct((M, N_MAIN), BF16),
                   jax.ShapeDtypeStruct((M, N_SMALL), F32)],
        scratch_shapes=[pltpu.VMEM((tm, D_MODEL), BF16)],
        compiler_params=pltpu.CompilerParams(
            dimension_semantics=("arbitrary", "arbitrary"), vmem_limit_bytes=VMEM_LIMIT),
        name="proj",
    )(x2d, w_main, w_small)


def _prep_kernel(s_ref, kvg_ref, ig_ref, ib_ref, ckv_ref, ckvt_ref, kidx_ref, gt_ref):
    c = s_ref[:, 0:D_C]
    c = c * lax.rsqrt(jnp.mean(c * c, axis=-1, keepdims=True) + LN_EPS) * kvg_ref[...]
    ckv_ref[...] = c.astype(BF16)
    for r in range(ckvt_ref.shape[0]):
        ckvt_ref[r, 0:D_C, :] = c[r * KB:(r + 1) * KB, :].T.astype(BF16)
        ckvt_ref[r, D_C:D_C + ONES_ROWS, :] = jnp.ones((ONES_ROWS, KB), BF16)
    tile = s_ref[:, D_C:D_C + LANES]
    k = tile[:, S_KI:S_KI + D_IDX]
    mu = jnp.mean(k, axis=-1, keepdims=True)
    var = jnp.mean(jnp.square(k - mu), axis=-1, keepdims=True)
    kidx_ref[...] = ((k - mu) * lax.rsqrt(var + LN_EPS) * ig_ref[...] + ib_ref[...]).astype(BF16)
    gt_ref[...] = tile.T


def _prep(small, kv_g, idx_g, idx_b, tm=1024):
    M = small.shape[0]
    return pl.pallas_call(
        _prep_kernel,
        grid=(M // tm,),
        in_specs=[pl.BlockSpec((tm, N_SMALL), lambda i: (i, 0)),
                  pl.BlockSpec((1, D_C), lambda i: (0, 0)),
                  pl.BlockSpec((1, D_IDX), lambda i: (0, 0)),
                  pl.BlockSpec((1, D_IDX), lambda i: (0, 0))],
        out_specs=[pl.BlockSpec((tm, D_C), lambda i: (i, 0)),
                   pl.BlockSpec((tm // KB, D_C + ONES_ROWS, KB), lambda i: (i, 0, 0)),
                   pl.BlockSpec((tm, D_IDX), lambda i: (i, 0)),
                   pl.BlockSpec((LANES, tm), lambda i: (0, i))],
        out_shape=[jax.ShapeDtypeStruct((M, D_C), BF16),
                   jax.ShapeDtypeStruct((M // KB, D_C + ONES_ROWS, KB), BF16),
                   jax.ShapeDtypeStruct((M, D_IDX), BF16),
                   jax.ShapeDtypeStruct((LANES, M), F32)],
        name="prep",
    )(small, kv_g, idx_g, idx_b)


def _key_to_float(key):
    bits = jnp.where(key < 0, key ^ jnp.int32(0x7FFFFFFF), key)
    return lax.bitcast_convert_type(bits, F32)


def _dsa_kernel(qa_ref, za_ref, qi_ref, gt_ref, ckv_ref, ckvt_ref, kidx_ref, wukt_ref, wuvt_ref, bias_ref,
                y_ref, qall_ref, qr_ref, ha_ref, hb_ref, sc_ref, mb_ref, sa_ref, sb_ref, pa_ref, pb_ref,
                ta_ref, tb_ref, m_ref, al_ref, acc_ref, w_ref, cand_ref, thr_ref, cut_ref):
    qi = pl.program_id(1)
    nkb = qi + 1
    last_kb = mb_ref.shape[0] - 1

    for h in range(H_A):
        ql = lax.dot_general(wukt_ref[h], qa_ref[:, h * DH_A:(h + 1) * DH_A], (((1,), (1,)), ((), ())),
                             preferred_element_type=F32)
        qall_ref[h // (H_A // 2), :, (h % (H_A // 2)) * QB:(h % (H_A // 2) + 1) * QB] = (
            ql * (DH_A ** -0.5 * LOG2E)).astype(BF16)


    @pl.when(qi * QB < TOPK)
    def _():
        krow = lax.broadcasted_iota(jnp.int32, (KB, QB), 0)
        qcol = lax.broadcasted_iota(jnp.int32, (KB, QB), 1)
        mb_ref[0] = jnp.where(krow <= qcol, 0.0, NEG).astype(F32)

    @pl.when(qi * QB >= TOPK)
    def _():
        for h in range(H_IDX):
            qr_ref[h * QB:(h + 1) * QB, :] = qi_ref[:, h * D_IDX:(h + 1) * D_IDX]
        w_ref[...] = gt_ref[S_WI:S_WI + H_IDX, :] * ((D_IDX ** -0.5) * (H_IDX ** -0.5))
        krow = lax.broadcasted_iota(jnp.int32, (KI, LANES), 0)
        qcol = lax.broadcasted_iota(jnp.int32, (KI, LANES), 1)
        n_ki = nkb * (KB // KI)
        last_ki = sc_ref.shape[0] // KI - 1

        def head_dots(ki, dst_ref):
            k = kidx_ref[pl.ds(pl.multiple_of(jnp.minimum(ki, last_ki) * KI, KI), KI), :]
            dst_ref[...] = lax.dot_general(k, qr_ref[...], (((1,), (1,)), ((), ())),
                                           preferred_element_type=F32)

        def reduce_heads(src_ref, ki):
            for g in range(QB // LANES):
                lanes = slice(g * LANES, (g + 1) * LANES)
                acc = jnp.zeros((KI, LANES), F32)
                for h in range(H_IDX):
                    acc = acc + (jnp.maximum(src_ref[:, h * QB + g * LANES:h * QB + (g + 1) * LANES], 0.0)
                                 * w_ref[h:h + 1, lanes])
                sc_ref[pl.ds(pl.multiple_of(ki * KI, KI), KI), lanes] = jnp.where(
                    krow + (ki * KI - qi * QB - g * LANES) <= qcol, acc, -jnp.inf)

        pad_kb = jnp.minimum(nkb, last_kb)
        sc_ref[pl.ds(pl.multiple_of(pad_kb * KB, KB), KB), :] = jnp.full((KB, QB), -jnp.inf, F32)

        head_dots(0, ha_ref)

        def sc_body(j, carry):
            head_dots(2 * j + 1, hb_ref)
            reduce_heads(ha_ref, 2 * j)
            head_dots(2 * j + 2, ha_ref)
            reduce_heads(hb_ref, 2 * j + 1)
            return carry
        lax.fori_loop(0, n_ki // 2, sc_body, 0)

        n_cnt = (nkb * KB + CNT_ROWS - 1) // CNT_ROWS

        def count_where(pred):
            def body(c, acc):
                parts = []
                for g in range(QB // LANES):
                    lanes = slice(g * LANES, (g + 1) * LANES)
                    blk = sc_ref[pl.ds(pl.multiple_of(c * CNT_ROWS, CNT_ROWS), CNT_ROWS), lanes]
                    hit = jnp.where(pred(blk, c * CNT_ROWS, lanes), 1.0, 0.0).astype(F32)
                    parts.append(jnp.sum(hit.reshape(CNT_ROWS // CNT_ACC, CNT_ACC, LANES), axis=0))
                return acc + jnp.concatenate(parts, axis=1)
            acc = lax.fori_loop(0, n_cnt, body, jnp.zeros((CNT_ACC, QB), F32))
            return jnp.sum(acc, axis=0, keepdims=True)

        def bit_body(i, carry):
            u, c_ge = carry
            trial = u | lax.shift_left(jnp.int32(1), 31 - i)
            cand_ref[...] = _key_to_float(trial ^ jnp.int32(-2 ** 31))
            cnt = count_where(lambda blk, row0, lanes: blk >= cand_ref[:, lanes])
            ok = cnt >= float(TOPK)
            return jnp.where(ok, trial, u), jnp.where(ok, cnt, c_ge)
        u, c_ge = lax.fori_loop(0, 32, bit_body, (jnp.zeros((1, QB), jnp.int32),
                                                  jnp.full((1, QB), float(sc_ref.shape[0]), F32)))
        thr_ref[...] = _key_to_float(u ^ jnp.int32(-2 ** 31))
        has_ties = jnp.max(c_ge) > float(TOPK)

        @pl.when(jnp.logical_not(has_ties))
        def _():
            def mb_body(kb, carry):
                blk = sc_ref[pl.ds(pl.multiple_of(kb * KB, KB), KB), :]
                mb_ref[kb] = jnp.where(blk >= thr_ref[...], 0.0, NEG).astype(F32)
                return carry
            lax.fori_loop(0, nkb, mb_body, 0)

        @pl.when(has_ties)
        def _():
            c_gt = count_where(lambda blk, row0, lanes: blk > thr_ref[:, lanes])
            need = float(TOPK) - c_gt
            rows = lax.broadcasted_iota(jnp.int32, (CNT_ROWS, LANES), 0)
            n_bits = (sc_ref.shape[0] - 1).bit_length()

            def idx_body(i, cut):
                trial = cut | lax.shift_left(jnp.int32(1), n_bits - 1 - i)
                cut_ref[...] = trial
                before = count_where(lambda blk, row0, lanes: (blk == thr_ref[:, lanes])
                                     & (rows + row0 < cut_ref[:, lanes]))
                return jnp.where(before < need, trial, cut)
            cut_ref[...] = lax.fori_loop(0, n_bits, idx_body, jnp.zeros((1, QB), jnp.int32))

            def mb_body(kb, carry):
                for g in range(QB // LANES):
                    lanes = slice(g * LANES, (g + 1) * LANES)
                    blk = sc_ref[pl.ds(pl.multiple_of(kb * KB, KB), KB), lanes]
                    thr = thr_ref[:, lanes]
                    keep = (blk > thr) | ((blk == thr) & (rows[0:KB] + kb * KB <= cut_ref[:, lanes]))
                    mb_ref[kb, :, lanes] = jnp.where(keep, 0.0, NEG).astype(F32)
                return carry
            lax.fori_loop(0, nkb, mb_body, 0)

    m_ref[...] = jnp.full(m_ref.shape, NEG, F32)
    acc_ref[...] = jnp.zeros(acc_ref.shape, F32)

    hh = H_A // 2
    s_refs, p_refs, t_refs = (sa_ref, sb_ref), (pa_ref, pb_ref), (ta_ref, tb_ref)
    pb_ref[...] = jnp.zeros(pb_ref.shape, BF16)
    al_ref[...] = jnp.ones(al_ref.shape, F32)

    def logits(kb, half):
        kv = ckv_ref[pl.ds(pl.multiple_of(jnp.minimum(kb, last_kb) * KB, KB), KB), :]
        s_refs[half][...] = jnp.dot(kv, qall_ref[half], preferred_element_type=F32)

    def softmax(kb, half, with_bias):
        tile = jnp.clip(qi - kb, 0, 2)
        for j in range(hh):
            for g in range(QB // LANES):
                lanes = slice(g * LANES, (g + 1) * LANES)
                cols = slice(j * QB + g * LANES, j * QB + (g + 1) * LANES)
                x = s_refs[half][:, cols] + mb_ref[kb, :, lanes]
                if with_bias:
                    x = x + bias_ref[half * hh + j, tile, :, lanes]
                m_prev = m_ref[half, :, cols]
                m_blk = jnp.max(x.reshape(KB // CNT_ACC, CNT_ACC, LANES), axis=0)
                m_new = jnp.maximum(m_prev, jnp.max(m_blk, axis=0, keepdims=True))
                al_ref[half, :, cols] = jnp.exp2(m_prev - m_new)
                m_ref[half, :, cols] = m_new
                p_refs[half][:, cols] = jnp.exp2(x - m_new).astype(BF16)

    def accumulate(kb, half):
        t_refs[half][...] = jnp.dot(ckvt_ref[jnp.maximum(kb, 0)], p_refs[half][...],
                                    preferred_element_type=F32)
        acc_ref[half] = acc_ref[half] * al_ref[half] + t_refs[half][...]

    def sweep(first_kb, end_kb, with_bias):
        def body(kb, carry):
            logits(kb, 1)
            softmax(kb, 0, with_bias)
            accumulate(kb - 1, 1)
            logits(kb + 1, 0)
            softmax(kb, 1, with_bias)
            accumulate(kb, 0)
            return carry
        lax.fori_loop(first_kb, end_kb, body, 0)

    n_far = jnp.maximum(qi - 1, 0)
    logits(0, 0)
    sweep(0, n_far, False)
    sweep(n_far, nkb, True)
    accumulate(nkb - 1, 1)

    for h in range(H_A):
        half, cols = h // hh, slice((h % hh) * QB, (h % hh + 1) * QB)
        ya_t = jnp.dot(wuvt_ref[h], acc_ref[half, 0:D_C, cols].astype(BF16), preferred_element_type=F32)
        ya = (ya_t / acc_ref[half, D_C:D_C + 1, cols]).T
        z = za_ref[:, h * DH_A:(h + 1) * DH_A].astype(F32)
        y_ref[:, h * DH_A:(h + 1) * DH_A] = (ya * (z * jax.nn.sigmoid(z))).astype(BF16)


def _dsa(main, gate_t, ckv_n, ckv_t, kidx_n, w_uk_t, w_uv_t, bias, B, T):
    nq = T // QB
    return pl.pallas_call(
        _dsa_kernel,
        grid=(B, nq),
        in_specs=[pl.BlockSpec((QB, W_A), lambda b, q: (b * nq + q, C_QA // W_A)),
                  pl.BlockSpec((QB, W_A), lambda b, q: (b * nq + q, C_ZA // W_A)),
                  pl.BlockSpec((QB, H_IDX * D_IDX), lambda b, q: (b * nq + q, C_QI // (H_IDX * D_IDX))),
                  pl.BlockSpec((LANES, QB), lambda b, q: (0, b * nq + q)),
                  pl.BlockSpec((T, D_C), lambda b, q: (b, 0)),
                  pl.BlockSpec((T // KB, D_C + ONES_ROWS, KB), lambda b, q: (b, 0, 0)),
                  pl.BlockSpec((T, D_IDX), lambda b, q: (b, 0)),
                  pl.BlockSpec((H_A, D_C, DH_A), lambda b, q: (0, 0, 0)),
                  pl.BlockSpec((H_A, DH_A, D_C), lambda b, q: (0, 0, 0)),
                  pl.BlockSpec((H_A, 3, KB, QB), lambda b, q: (0, 0, 0, 0))],
        out_specs=pl.BlockSpec((QB, W_A), lambda b, q: (b * nq + q, 0)),
        out_shape=jax.ShapeDtypeStruct((B * T, W_A), BF16),
        scratch_shapes=[pltpu.VMEM((2, D_C, H_A // 2 * QB), BF16),
                        pltpu.VMEM((H_IDX * QB, D_IDX), BF16),
                        pltpu.VMEM((KI, H_IDX * QB), F32),
                        pltpu.VMEM((KI, H_IDX * QB), F32),
                        pltpu.VMEM((T, QB), F32),
                        pltpu.VMEM((T // KB, KB, QB), F32),
                        pltpu.VMEM((KB, H_A // 2 * QB), F32),
                        pltpu.VMEM((KB, H_A // 2 * QB), F32),
                        pltpu.VMEM((KB, H_A // 2 * QB), BF16),
                        pltpu.VMEM((KB, H_A // 2 * QB), BF16),
                        pltpu.VMEM((D_C + ONES_ROWS, H_A // 2 * QB), F32),
                        pltpu.VMEM((D_C + ONES_ROWS, H_A // 2 * QB), F32),
                        pltpu.VMEM((2, 1, H_A // 2 * QB), F32),
                        pltpu.VMEM((2, 1, H_A // 2 * QB), F32),
                        pltpu.VMEM((2, D_C + ONES_ROWS, H_A // 2 * QB), F32),
                        pltpu.VMEM((H_IDX, QB), F32),
                        pltpu.VMEM((1, QB), F32),
                        pltpu.VMEM((1, QB), F32),
                        pltpu.VMEM((1, QB), jnp.int32)],
        compiler_params=pltpu.CompilerParams(
            dimension_semantics=("arbitrary", "arbitrary"), vmem_limit_bytes=VMEM_LIMIT),
        name="dsa",
    )(main, main, main, gate_t, ckv_n, ckv_t, kidx_n, w_uk_t, w_uv_t, bias)


def _split_dot(tri, x):
    hi = x.astype(BF16)
    lo = (x - hi.astype(F32)).astype(BF16)
    return jnp.dot(tri, hi, preferred_element_type=F32) + jnp.dot(tri, lo, preferred_element_type=F32)


def _log_sigmoid(x):
    return jnp.minimum(x, 0.0) - jnp.log1p(jnp.exp(-jnp.abs(x)))


def _mlstm_kernel(q_ref, k_ref, qh_ref, kh_ref, v_ref, o_ref, z_ref, g_ref, gt_ref,
                  cw_ref, cb_ref, gbr_ref, gbc_ref, ng_ref, y_ref, ct_ref, n_ref, m_ref, xq_ref, xk_ref):
    c = pl.program_id(1)
    L = L_M

    @pl.when(c == 0)
    def _():
        ct_ref[...] = jnp.zeros(ct_ref.shape, F32)
        n_ref[...] = jnp.zeros(n_ref.shape, F32)
        m_ref[...] = jnp.zeros(m_ref.shape, F32)

    def conv_silu(x_ref, halo_ref, lo, xe_ref):
        xe_ref[0:HALO, :] = jnp.where(c > 0, halo_ref[...].astype(F32), 0.0)
        xe_ref[HALO:HALO + L, :] = x_ref[...].astype(F32)
        y = cb_ref[:, lo:lo + H_M * DK_M]
        for j in range(CONV_W):
            off = HALO - (CONV_W - 1) + j
            y = y + cw_ref[j:j + 1, lo:lo + H_M * DK_M] * xe_ref[off:off + L, :]
        return y * jax.nn.sigmoid(y)

    q_all = conv_silu(q_ref, qh_ref, 0, xq_ref)
    k_all = conv_silu(k_ref, kh_ref, H_M * DK_M, xk_ref) * (DK_M ** -0.5)

    gc = g_ref[...] + gbr_ref[...]
    gr = gt_ref[S_IM:S_IM + 2 * H_M, :] + gbc_ref[S_IM:S_IM + 2 * H_M, :]
    r = lax.broadcasted_iota(jnp.int32, (L, L), 0)
    s = lax.broadcasted_iota(jnp.int32, (L, L), 1)
    causal = s <= r
    tri_l = jnp.where(causal, 1.0, 0.0).astype(BF16)
    tri_u = jnp.where(r <= s, 1.0, 0.0).astype(BF16)
    b_cols = _split_dot(tri_l, _log_sigmoid(gc))
    b_rows = jnp.dot(_log_sigmoid(gr).astype(BF16), tri_u, preferred_element_type=F32) \
        + jnp.dot((_log_sigmoid(gr) - _log_sigmoid(gr).astype(BF16).astype(F32)).astype(BF16), tri_u,
                  preferred_element_type=F32)

    for h in range(H_M):
        q = q_all[:, h * DK_M:(h + 1) * DK_M]
        k = k_all[:, h * DK_M:(h + 1) * DK_M]
        v = v_ref[:, h * DV_M:(h + 1) * DV_M]
        qb = q.astype(BF16)
        b_c = b_cols[:, S_FM + h:S_FM + h + 1]
        i_c = gc[:, S_IM + h:S_IM + h + 1]
        b_r = b_rows[H_M + h:H_M + h + 1, :]
        i_r = gr[h:h + 1, :]
        m_prev = m_ref[h]
        ct = ct_ref[h]
        n = n_ref[h]

        log_d = jnp.where(causal, b_c - b_r + i_r, -jnp.inf)
        g = b_c + m_prev
        m_t = jnp.maximum(jnp.max(log_d, axis=-1, keepdims=True), g)
        qk = lax.dot_general(qb, k.astype(BF16), (((1,), (1,)), ((), ())), preferred_element_type=F32)
        s_mat = qk * jnp.exp(log_d - m_t)
        inter = jnp.exp(g - m_t)
        num = jnp.dot(s_mat.astype(BF16), v, preferred_element_type=F32) \
            + inter * jnp.dot(qb, ct.astype(BF16), preferred_element_type=F32)
        den = jnp.sum(s_mat, axis=-1, keepdims=True) + inter * jnp.sum(q * n, axis=-1, keepdims=True)
        hh = num / jnp.maximum(jnp.abs(den), jnp.exp(-m_t))

        b_last = b_c[L - 1:L, :]
        a_r = b_last - b_r + i_r
        m_new = jnp.maximum(b_last + m_prev, jnp.max(a_r, axis=-1, keepdims=True))
        decay = jnp.exp(b_last + m_prev - m_new)
        wgt_c = jnp.exp(b_last - b_c + i_c - m_new)
        kw = k * wgt_c
        ct_ref[h] = decay * ct + jnp.dot(kw.T.astype(BF16), v, preferred_element_type=F32)
        n_ref[h] = decay * n + jnp.sum(kw, axis=0, keepdims=True)
        m_ref[h] = m_new

        mu = jnp.mean(hh, axis=-1, keepdims=True)
        var = jnp.mean(jnp.square(hh - mu), axis=-1, keepdims=True)
        hn = (hh - mu) * lax.rsqrt(var + LN_EPS) * ng_ref[:, h * DV_M:(h + 1) * DV_M]
        og = o_ref[:, h * DV_M:(h + 1) * DV_M].astype(F32)
        zg = z_ref[:, h * DV_M:(h + 1) * DV_M].astype(F32)
        y_ref[:, h * DV_M:(h + 1) * DV_M] = (hn * jax.nn.sigmoid(og) * (zg * jax.nn.sigmoid(zg))).astype(BF16)


def _mlstm(main, small, gate_t, conv_w, conv_b, gb_row, gb_col, norm_g, B, T):
    nc = T // L_M
    hb = L_M // HALO
    qk_w = H_M * DK_M

    def halo_map(col):
        return lambda b, c: (jnp.maximum((b * nc + c) * hb - 1, 0), col)

    return pl.pallas_call(
        _mlstm_kernel,
        grid=(B, nc),
        in_specs=[pl.BlockSpec((L_M, qk_w), lambda b, c: (b * nc + c, C_QM // qk_w)),
                  pl.BlockSpec((L_M, qk_w), lambda b, c: (b * nc + c, C_KM // qk_w)),
                  pl.BlockSpec((HALO, qk_w), halo_map(C_QM // qk_w)),
                  pl.BlockSpec((HALO, qk_w), halo_map(C_KM // qk_w)),
                  pl.BlockSpec((L_M, W_M), lambda b, c: (b * nc + c, C_VM // W_M)),
                  pl.BlockSpec((L_M, W_M), lambda b, c: (b * nc + c, C_OM // W_M)),
                  pl.BlockSpec((L_M, W_M), lambda b, c: (b * nc + c, C_ZM // W_M)),
                  pl.BlockSpec((L_M, LANES), lambda b, c: (b * nc + c, D_C // LANES)),
                  pl.BlockSpec((LANES, L_M), lambda b, c: (0, b * nc + c)),
                  pl.BlockSpec((CONV_W, 2 * qk_w), lambda b, c: (0, 0)),
                  pl.BlockSpec((1, 2 * qk_w), lambda b, c: (0, 0)),
                  pl.BlockSpec((1, LANES), lambda b, c: (0, 0)),
                  pl.BlockSpec((LANES, 1), lambda b, c: (0, 0)),
                  pl.BlockSpec((1, W_M), lambda b, c: (0, 0))],
        out_specs=pl.BlockSpec((L_M, W_M), lambda b, c: (b * nc + c, 0)),
        out_shape=jax.ShapeDtypeStruct((B * T, W_M), BF16),
        scratch_shapes=[pltpu.VMEM((H_M, DK_M, DV_M), F32),
                        pltpu.VMEM((H_M, 1, DK_M), F32),
                        pltpu.VMEM((H_M, 1, 1), F32),
                        pltpu.VMEM((HALO + L_M, H_M * DK_M), F32),
                        pltpu.VMEM((HALO + L_M, H_M * DK_M), F32)],
        compiler_params=pltpu.CompilerParams(
            dimension_semantics=("arbitrary", "arbitrary"), vmem_limit_bytes=VMEM_LIMIT),
        name="mlstm",
    )(main, main, main, main, main, main, main, small, gate_t,
      conv_w, conv_b, gb_row, gb_col, norm_g)


def _out_kernel(ya_ref, ym_ref, x_ref, w_ref, g_ref, b_ref, o_ref):
    y = jnp.dot(ya_ref[...], w_ref[0:W_A, :], preferred_element_type=F32)
    y = y + jnp.dot(ym_ref[...], w_ref[W_A:W_A + W_M, :], preferred_element_type=F32)
    r = ALPHA * x_ref[...] + y
    mu = jnp.mean(r, axis=-1, keepdims=True)
    var = jnp.mean(jnp.square(r - mu), axis=-1, keepdims=True)
    o_ref[...] = (r - mu) * lax.rsqrt(var + LN_EPS) * g_ref[...] + b_ref[...]


def _out(ya, ym, x2d, w_out, ln_g, ln_b, tm=512):
    M = x2d.shape[0]
    return pl.pallas_call(
        _out_kernel,
        grid=(M // tm,),
        in_specs=[pl.BlockSpec((tm, W_A), lambda i: (i, 0)),
                  pl.BlockSpec((tm, W_M), lambda i: (i, 0)),
                  pl.BlockSpec((tm, D_MODEL), lambda i: (i, 0)),
                  pl.BlockSpec((W_A + W_M, D_MODEL), lambda i: (0, 0)),
                  pl.BlockSpec((1, D_MODEL), lambda i: (0, 0)),
                  pl.BlockSpec((1, D_MODEL), lambda i: (0, 0))],
        out_specs=pl.BlockSpec((tm, D_MODEL), lambda i: (i, 0)),
        out_shape=jax.ShapeDtypeStruct((M, D_MODEL), F32),
        compiler_params=pltpu.CompilerParams(
            dimension_semantics=("arbitrary",), vmem_limit_bytes=VMEM_LIMIT),
        name="out",
    )(ya, ym, x2d, w_out, ln_g, ln_b)


_W_IN_SEGS = (("q_a", W_A), ("c_kv", D_C), ("z_a", W_A), ("q_i", H_IDX * D_IDX), ("k_i", D_IDX),
              ("w_i", H_IDX), ("q_m", H_M * DK_M), ("k_m", H_M * DK_M), ("v_m", W_M), ("i_m", H_M),
              ("f_m", H_M), ("o_m", W_M), ("z_m", W_M))
_MAIN_ORDER = ("q_a", "z_a", "q_i", "q_m", "k_m", "v_m", "o_m", "z_m")
_SEG_NAMES = [name for name, _ in _W_IN_SEGS]
assert _SEG_NAMES.index("f_m") == _SEG_NAMES.index("i_m") + 1 and S_FM == S_IM + H_M


def _repack_kernel(wt_ref, main_ref, small_ref):
    src, off = {}, 0
    for name, width in _W_IN_SEGS:
        src[name] = (off, width)
        off += width
    dst = 0
    for name in _MAIN_ORDER:
        lo, width = src[name]
        main_ref[dst:dst + width, :] = wt_ref[lo:lo + width, :].astype(BF16)
        dst += width
    parts = [wt_ref[src[name][0]:src[name][0] + src[name][1], :] for name in ("c_kv", "k_i", "w_i")]
    lo = src["i_m"][0]
    parts.append(wt_ref[lo:lo + 2 * H_M, :])
    used = sum(p.shape[0] for p in parts)
    parts.append(jnp.zeros((N_SMALL - used, wt_ref.shape[1]), F32))
    small_ref[...] = jnp.concatenate(parts, axis=0).astype(BF16)


def _repack_w_in(w_in, tc=256):
    n_cols = sum(width for _, width in _W_IN_SEGS)
    wt = jnp.swapaxes(w_in, 1, 2)[0]
    return pl.pallas_call(
        _repack_kernel,
        grid=(D_MODEL // tc,),
        in_specs=[pl.BlockSpec((n_cols, tc), lambda i: (0, i))],
        out_specs=[pl.BlockSpec((N_MAIN, tc), lambda i: (0, i)),
                   pl.BlockSpec((N_SMALL, tc), lambda i: (0, i))],
        out_shape=[jax.ShapeDtypeStruct((N_MAIN, D_MODEL), BF16),
                   jax.ShapeDtypeStruct((N_SMALL, D_MODEL), BF16)],
        compiler_params=pltpu.CompilerParams(
            dimension_semantics=("arbitrary",), vmem_limit_bytes=VMEM_LIMIT),
        name="repack",
    )(wt)


def kernel(x, w_in, b_igate, b_fgate, kv_norm_g, w_uk, w_uv, idx_k_ln_g, idx_k_ln_b, rel_bias,
           conv_w, conv_b, mh_norm_g, w_out, ln_g, ln_b):
    B, T, D = x.shape
    assert D == D_MODEL and T % L_M == 0 and T % (2 * KB) == 0 and w_in.shape[0] == 1
    bias = _bias_tiles(rel_bias)
    x2d = x.reshape(B * T, D)
    w_main, w_small = _repack_w_in(w_in)
    main, small = _proj(x2d, w_main, w_small)
    ckv_n, ckv_t, kidx_n, gate_t = _prep(small, kv_norm_g[0][None], idx_k_ln_g[0][None], idx_k_ln_b[0][None])
    w_uk_t = jnp.transpose(w_uk[0], (0, 2, 1)).astype(BF16)
    w_uv_t = jnp.transpose(w_uv[0], (0, 2, 1)).astype(BF16)
    ya = _dsa(main, gate_t, ckv_n, ckv_t, kidx_n, w_uk_t, w_uv_t, bias, B, T)
    gb = jnp.zeros((LANES,), F32).at[S_IM:S_IM + H_M].set(b_igate[0]).at[S_FM:S_FM + H_M].set(b_fgate[0])
    ym = _mlstm(main, small, gate_t, conv_w[0], conv_b[0][None], gb[None, :], gb[:, None],
                mh_norm_g[0][None], B, T)
    out = _out(ya, ym, x2d, w_out[0].astype(BF16), ln_g[0][None], ln_b[0][None])
    return out.reshape(B, T, D)
```

```python
import functools
import math

import numpy as np
import jax
import jax.numpy as jnp
from jax import lax
from jax.experimental import pallas as pl
from jax.experimental.pallas import tpu as pltpu

F32 = jnp.float32
BF16 = jnp.bfloat16

D_MODEL = 2048
W_A = 1024
DH_A = 128
H_A = 8
D_C = 256
H_IDX = 16
D_IDX = 64
TOPK = 256
W_M = 1024
H_M = 4
DV_M = 256
DK_M = 128
CONV_W = 4
N_BUCKETS = 32
MAX_DIST = 128
ALPHA = 2.0 ** 0.25
LN_EPS = 1e-5

LANES = 128
SUBLANES = 8
VMEM_LIMIT = 56 * 1024 * 1024

QB = 256
KB = 256
KI = 128
CNT_ROWS = 512
CNT_ACC = 4 * SUBLANES
L_M = 256
HALO = 16
NEG = -1e30
LOG2E = math.log2(math.e)
ONES_ROWS = 16

C_QA, C_ZA, C_QI, C_QM, C_KM, C_VM, C_OM, C_ZM = 0, 1024, 2048, 3072, 3584, 4096, 5120, 6144
N_MAIN = 7168
N_SMALL = 384
S_KI, S_WI, S_IM, S_FM = 0, 64, 80, 84


def _t5_bucket_np(rel):
    max_exact = N_BUCKETS // 2
    n = np.maximum(rel, 0)
    nf = np.maximum(n, 1).astype(np.float32)
    large = max_exact + (np.log(nf / np.float32(max_exact)) / np.float32(math.log(MAX_DIST / max_exact))
                         * np.float32(N_BUCKETS - max_exact)).astype(np.int32)
    large = np.minimum(large, N_BUCKETS - 1)
    return np.where(n < max_exact, n, large).astype(np.int32)


FAR_BUCKET = int(_t5_bucket_np(np.array(2 * KB + 1)))


def _bucket_tiles():
    i = np.arange(QB)[None, :]
    j = np.arange(KB)[:, None]
    t0 = _t5_bucket_np(i - j)
    t1 = _t5_bucket_np(i - j + KB)
    assert (t5 := _t5_bucket_np(np.arange(KB + 1, 4096))).min() == t5.max() == FAR_BUCKET
    return np.stack([t0, t1]).astype(np.int32)


def _bias_kernel(bucket_ref, rb_ref, out_ref):
    h = pl.program_id(0)
    far = rb_ref[FAR_BUCKET, h]
    for k in range(2):
        bk = bucket_ref[k]
        acc = jnp.zeros((KB, QB), F32)
        for b in range(N_BUCKETS):
            acc = jnp.where(bk == b, rb_ref[b, h] - far, acc)
        out_ref[0, k] = acc * LOG2E
    out_ref[0, 2] = jnp.zeros((KB, QB), F32)


def _bias_tiles(rel_bias):
    bucket = jnp.asarray(_bucket_tiles())
    return pl.pallas_call(
        _bias_kernel,
        grid=(H_A,),
        in_specs=[pl.BlockSpec((2, KB, QB), lambda h: (0, 0, 0)),
                  pl.BlockSpec(memory_space=pltpu.SMEM)],
        out_specs=pl.BlockSpec((1, 3, KB, QB), lambda h: (h, 0, 0, 0)),
        out_shape=jax.ShapeDtypeStruct((H_A, 3, KB, QB), F32),
        name="bias_tiles",
    )(bucket, rel_bias)


_NT = (((1,), (1,)), ((), ()))


def _proj_kernel(x_ref, w_ref, ws_ref, o_ref, os_ref, xb_ref):
    @pl.when(pl.program_id(1) == 0)
    def _():
        xb_ref[...] = x_ref[...].astype(BF16)
        os_ref[...] = lax.dot_general(xb_ref[...], ws_ref[...], _NT, preferred_element_type=F32)

    o_ref[...] = lax.dot_general(xb_ref[...], w_ref[...], _NT, preferred_element_type=F32).astype(BF16)


def _proj(x2d, w_main, w_small, tm=1024, tn=1024):
    M = x2d.shape[0]
    return pl.pallas_call(
        _proj_kernel,
        grid=(M // tm, N_MAIN // tn),
        in_specs=[pl.BlockSpec((tm, D_MODEL), lambda i, j: (i, 0)),
                  pl.BlockSpec((tn, D_MODEL), lambda i, j: (j, 0)),
                  pl.BlockSpec((N_SMALL, D_MODEL), lambda i, j: (0, 0))],
        out_specs=[pl.BlockSpec((tm, tn), lambda i, j: (i, j)),
                   pl.BlockSpec((tm, N_SMALL), lambda i, j: (i, 0))],
        out_shape=[jax.ShapeDtypeStruct((M, N_MAIN), BF16),
                   jax.ShapeDtypeStruct((M, N_SMALL), F32)],
        scratch_shapes=[pltpu.VMEM((tm, D_MODEL), BF16)],
        compiler_params=pltpu.CompilerParams(
            dimension_semantics=("arbitrary", "arbitrary"), vmem_limit_bytes=VMEM_LIMIT),
        name="proj",
    )(x2d, w_main, w_small)


def _prep_kernel(s_ref, kvg_ref, ig_ref, ib_ref, ckv_ref, ckvt_ref, kidx_ref, gt_ref):
    c = s_ref[:, 0:D_C]
    c = c * lax.rsqrt(jnp.mean(c * c, axis=-1, keepdims=True) + LN_EPS) * kvg_ref[...]
    ckv_ref[...] = c.astype(BF16)
    for r in range(ckvt_ref.shape[0]):
        ckvt_ref[r, 0:D_C, :] = c[r * KB:(r + 1) * KB, :].T.astype(BF16)
        ckvt_ref[r, D_C:D_C + ONES_ROWS, :] = jnp.ones((ONES_ROWS, KB), BF16)
    tile = s_ref[:, D_C:D_C + LANES]
    k = tile[:, S_KI:S_KI + D_IDX]
    mu = jnp.mean(k, axis=-1, keepdims=True)
    var = jnp.mean(jnp.square(k - mu), axis=-1, keepdims=True)
    kidx_ref[...] = ((k - mu) * lax.rsqrt(var + LN_EPS) * ig_ref[...] + ib_ref[...]).astype(BF16)
    gt_ref[...] = tile.T


def _prep(small, kv_g, idx_g, idx_b, tm=1024):
    M = small.shape[0]
    return pl.pallas_call(
        _prep_kernel,
        grid=(M // tm,),
        in_specs=[pl.BlockSpec((tm, N_SMALL), lambda i: (i, 0)),
                  pl.BlockSpec((1, D_C), lambda i: (0, 0)),
                  pl.BlockSpec((1, D_IDX), lambda i: (0, 0)),
                  pl.BlockSpec((1, D_IDX), lambda i: (0, 0))],
        out_specs=[pl.BlockSpec((tm, D_C), lambda i: (i, 0)),
                   pl.BlockSpec((tm // KB, D_C + ONES_ROWS, KB), lambda i: (i, 0, 0)),
                   pl.BlockSpec((tm, D_IDX), lambda i: (i, 0)),
                   pl.BlockSpec((LANES, tm), lambda i: (0, i))],
        out_shape=[jax.ShapeDtypeStruct((M, D_C), BF16),
                   jax.ShapeDtypeStruct((M // KB, D_C + ONES_ROWS, KB), BF16),
                   jax.ShapeDtypeStruct((M, D_IDX), BF16),
                   jax.ShapeDtypeStruct((LANES, M), F32)],
        name="prep",
    )(small, kv_g, idx_g, idx_b)


def _key_to_float(key):
    bits = jnp.where(key < 0, key ^ jnp.int32(0x7FFFFFFF), key)
    return lax.bitcast_convert_type(bits, F32)


def _dsa_kernel(qa_ref, za_ref, qi_ref, gt_ref, ckv_ref, ckvt_ref, kidx_ref, wukt_ref, wuvt_ref, bias_ref,
                y_ref, qall_ref, qr_ref, ha_ref, hb_ref, sc_ref, mb_ref, sa_ref, sb_ref, pa_ref, pb_ref,
                ta_ref, tb_ref, m_ref, al_ref, acc_ref, w_ref, cand_ref, thr_ref, cut_ref):
    qi = pl.program_id(1)
    nkb = qi + 1
    last_kb = mb_ref.shape[0] - 1

    for h in range(H_A):
        ql = lax.dot_general(wukt_ref[h], qa_ref[:, h * DH_A:(h + 1) * DH_A], (((1,), (1,)), ((), ())),
                             preferred_element_type=F32)
        qall_ref[h // (H_A // 2), :, (h % (H_A // 2)) * QB:(h % (H_A // 2) + 1) * QB] = (
            ql * (DH_A ** -0.5 * LOG2E)).astype(BF16)


    @pl.when(qi * QB < TOPK)
    def _():
        krow = lax.broadcasted_iota(jnp.int32, (KB, QB), 0)
        qcol = lax.broadcasted_iota(jnp.int32, (KB, QB), 1)
        mb_ref[0] = jnp.where(krow <= qcol, 0.0, NEG).astype(F32)

    @pl.when(qi * QB >= TOPK)
    def _():
        for h in range(H_IDX):
            qr_ref[h * QB:(h + 1) * QB, :] = qi_ref[:, h * D_IDX:(h + 1) * D_IDX]
        w_ref[...] = gt_ref[S_WI:S_WI + H_IDX, :] * ((D_IDX ** -0.5) * (H_IDX ** -0.5))
        krow = lax.broadcasted_iota(jnp.int32, (KI, LANES), 0)
        qcol = lax.broadcasted_iota(jnp.int32, (KI, LANES), 1)
        n_ki = nkb * (KB // KI)
        last_ki = sc_ref.shape[0] // KI - 1

        def head_dots(ki, dst_ref):
            k = kidx_ref[pl.ds(pl.multiple_of(jnp.minimum(ki, last_ki) * KI, KI), KI), :]
            dst_ref[...] = lax.dot_general(k, qr_ref[...], (((1,), (1,)), ((), ())),
                                           preferred_element_type=F32)

        def reduce_heads(src_ref, ki):
            for g in range(QB // LANES):
                lanes = slice(g * LANES, (g + 1) * LANES)
                acc = jnp.zeros((KI, LANES), F32)
                for h in range(H_IDX):
                    acc = acc + (jnp.maximum(src_ref[:, h * QB + g * LANES:h * QB + (g + 1) * LANES], 0.0)
                                 * w_ref[h:h + 1, lanes])
                sc_ref[pl.ds(pl.multiple_of(ki * KI, KI), KI), lanes] = jnp.where(
                    krow + (ki * KI - qi * QB - g * LANES) <= qcol, acc, -jnp.inf)

        pad_kb = jnp.minimum(nkb, last_kb)
        sc_ref[pl.ds(pl.multiple_of(pad_kb * KB, KB), KB), :] = jnp.full((KB, QB), -jnp.inf, F32)

        head_dots(0, ha_ref)

        def sc_body(j, carry):
            head_dots(2 * j + 1, hb_ref)
            reduce_heads(ha_ref, 2 * j)
            head_dots(2 * j + 2, ha_ref)
            reduce_heads(hb_ref, 2 * j + 1)
            return carry
        lax.fori_loop(0, n_ki // 2, sc_body, 0)

        n_cnt = (nkb * KB + CNT_ROWS - 1) // CNT_ROWS

        def count_where(pred):
            def body(c, acc):
                parts = []
                for g in range(QB // LANES):
                    lanes = slice(g * LANES, (g + 1) * LANES)
                    blk = sc_ref[pl.ds(pl.multiple_of(c * CNT_ROWS, CNT_ROWS), CNT_ROWS), lanes]
                    hit = jnp.where(pred(blk, c * CNT_ROWS, lanes), 1.0, 0.0).astype(F32)
                    parts.append(jnp.sum(hit.reshape(CNT_ROWS // CNT_ACC, CNT_ACC, LANES), axis=0))
                return acc + jnp.concatenate(parts, axis=1)
            acc = lax.fori_loop(0, n_cnt, body, jnp.zeros((CNT_ACC, QB), F32))
            return jnp.sum(acc, axis=0, keepdims=True)

        def bit_body(i, carry):
            u, c_ge = carry
            trial = u | lax.shift_left(jnp.int32(1), 31 - i)
            cand_ref[...] = _key_to_float(trial ^ jnp.int32(-2 ** 31))
            cnt = count_where(lambda blk, row0, lanes: blk >= cand_ref[:, lanes])
            ok = cnt >= float(TOPK)
            return jnp.where(ok, trial, u), jnp.where(ok, cnt, c_ge)
        u, c_ge = lax.fori_loop(0, 32, bit_body, (jnp.zeros((1, QB), jnp.int32),
                                                  jnp.full((1, QB), float(sc_ref.shape[0]), F32)))
        thr_ref[...] = _key_to_float(u ^ jnp.int32(-2 ** 31))
        has_ties = jnp.max(c_ge) > float(TOPK)

        @pl.when(jnp.logical_not(has_ties))
        def _():
            def mb_body(kb, carry):
                blk = sc_ref[pl.ds(pl.multiple_of(kb * KB, KB), KB), :]
                mb_ref[kb] = jnp.where(blk >= thr_ref[...], 0.0, NEG).astype(F32)
                return carry
            lax.fori_loop(0, nkb, mb_body, 0)

        @pl.when(has_ties)
        def _():
            c_gt = count_where(lambda blk, row0, lanes: blk > thr_ref[:, lanes])
            need = float(TOPK) - c_gt
            rows = lax.broadcasted_iota(jnp.int32, (CNT_ROWS, LANES), 0)
            n_bits = (sc_ref.shape[0] - 1).bit_length()

            def idx_body(i, cut):
                trial = cut | lax.shift_left(jnp.int32(1), n_bits - 1 - i)
                cut_ref[...] = trial
                before = count_where(lambda blk, row0, lanes: (blk == thr_ref[:, lanes])
                                     & (rows + row0 < cut_ref[:, lanes]))
                return jnp.where(before < need, trial, cut)
            cut_ref[...] = lax.fori_loop(0, n_bits, idx_body, jnp.zeros((1, QB), jnp.int32))

            def mb_body(kb, carry):
                for g in range(QB // LANES):
                    lanes = slice(g * LANES, (g + 1) * LANES)
                    blk = sc_ref[pl.ds(pl.multiple_of(kb * KB, KB), KB), lanes]
                    thr = thr_ref[:, lanes]
                    keep = (blk > thr) | ((blk == thr) & (rows[0:KB] + kb * KB <= cut_ref[:, lanes]))
                    mb_ref[kb, :, lanes] = jnp.where(keep, 0.0, NEG).astype(F32)
                return carry
            lax.fori_loop(0, nkb, mb_body, 0)

    m_ref[...] = jnp.full(m_ref.shape, NEG, F32)
    acc_ref[...] = jnp.zeros(acc_ref.shape, F32)

    hh = H_A // 2
    s_refs, p_refs, t_refs = (sa_ref, sb_ref), (pa_ref, pb_ref), (ta_ref, tb_ref)
    pb_ref[...] = jnp.zeros(pb_ref.shape, BF16)
    al_ref[...] = jnp.ones(al_ref.shape, F32)

    def logits(kb, half):
        kv = ckv_ref[pl.ds(pl.multiple_of(jnp.minimum(kb, last_kb) * KB, KB), KB), :]
        s_refs[half][...] = jnp.dot(kv, qall_ref[half], preferred_element_type=F32)

    def softmax(kb, half, with_bias):
        tile = jnp.clip(qi - kb, 0, 2)
        for j in range(hh):
            for g in range(QB // LANES):
                lanes = slice(g * LANES, (g + 1) * LANES)
                cols = slice(j * QB + g * LANES, j * QB + (g + 1) * LANES)
                x = s_refs[half][:, cols] + mb_ref[kb, :, lanes]
                if with_bias:
                    x = x + bias_ref[half * hh + j, tile, :, lanes]
                m_prev = m_ref[half, :, cols]
                m_blk = jnp.max(x.reshape(KB // CNT_ACC, CNT_ACC, LANES), axis=0)
                m_new = jnp.maximum(m_prev, jnp.max(m_blk, axis=0, keepdims=True))
                al_ref[half, :, cols] = jnp.exp2(m_prev - m_new)
                m_ref[half, :, cols] = m_new
                p_refs[half][:, cols] = jnp.exp2(x - m_new).astype(BF16)

    def accumulate(kb, half):
        t_refs[half][...] = jnp.dot(ckvt_ref[jnp.maximum(kb, 0)], p_refs[half][...],
                                    preferred_element_type=F32)
        acc_ref[half] = acc_ref[half] * al_ref[half] + t_refs[half][...]

    def sweep(first_kb, end_kb, with_bias):
        def body(kb, carry):
            logits(kb, 1)
            softmax(kb, 0, with_bias)
            accumulate(kb - 1, 1)
            logits(kb + 1, 0)
            softmax(kb, 1, with_bias)
            accumulate(kb, 0)
            return carry
        lax.fori_loop(first_kb, end_kb, body, 0)

    n_far = jnp.maximum(qi - 1, 0)
    logits(0, 0)
    sweep(0, n_far, False)
    sweep(n_far, nkb, True)
    accumulate(nkb - 1, 1)

    for h in range(H_A):
        half, cols = h // hh, slice((h % hh) * QB, (h % hh + 1) * QB)
        ya_t = jnp.dot(wuvt_ref[h], acc_ref[half, 0:D_C, cols].astype(BF16), preferred_element_type=F32)
        ya = (ya_t / acc_ref[half, D_C:D_C + 1, cols]).T
        z = za_ref[:, h * DH_A:(h + 1) * DH_A].astype(F32)
        y_ref[:, h * DH_A:(h + 1) * DH_A] = (ya * (z * jax.nn.sigmoid(z))).astype(BF16)


def _dsa(main, gate_t, ckv_n, ckv_t, kidx_n, w_uk_t, w_uv_t, bias, B, T):
    nq = T // QB
    return pl.pallas_call(
        _dsa_kernel,
        grid=(B, nq),
        in_specs=[pl.BlockSpec((QB, W_A), lambda b, q: (b * nq + q, C_QA // W_A)),
                  pl.BlockSpec((QB, W_A), lambda b, q: (b * nq + q, C_ZA // W_A)),
                  pl.BlockSpec((QB, H_IDX * D_IDX), lambda b, q: (b * nq + q, C_QI // (H_IDX * D_IDX))),
                  pl.BlockSpec((LANES, QB), lambda b, q: (0, b * nq + q)),
                  pl.BlockSpec((T, D_C), lambda b, q: (b, 0)),
                  pl.BlockSpec((T // KB, D_C + ONES_ROWS, KB), lambda b, q: (b, 0, 0)),
                  pl.BlockSpec((T, D_IDX), lambda b, q: (b, 0)),
                  pl.BlockSpec((H_A, D_C, DH_A), lambda b, q: (0, 0, 0)),
                  pl.BlockSpec((H_A, DH_A, D_C), lambda b, q: (0, 0, 0)),
                  pl.BlockSpec((H_A, 3, KB, QB), lambda b, q: (0, 0, 0, 0))],
        out_specs=pl.BlockSpec((QB, W_A), lambda b, q: (b * nq + q, 0)),
        out_shape=jax.ShapeDtypeStruct((B * T, W_A), BF16),
        scratch_shapes=[pltpu.VMEM((2, D_C, H_A // 2 * QB), BF16),
                        pltpu.VMEM((H_IDX * QB, D_IDX), BF16),
                        pltpu.VMEM((KI, H_IDX * QB), F32),
                        pltpu.VMEM((KI, H_IDX * QB), F32),
                        pltpu.VMEM((T, QB), F32),
                        pltpu.VMEM((T // KB, KB, QB), F32),
                        pltpu.VMEM((KB, H_A // 2 * QB), F32),
                        pltpu.VMEM((KB, H_A // 2 * QB), F32),
                        pltpu.VMEM((KB, H_A // 2 * QB), BF16),
                        pltpu.VMEM((KB, H_A // 2 * QB), BF16),
                        pltpu.VMEM((D_C + ONES_ROWS, H_A // 2 * QB), F32),
                        pltpu.VMEM((D_C + ONES_ROWS, H_A // 2 * QB), F32),
                        pltpu.VMEM((2, 1, H_A // 2 * QB), F32),
                        pltpu.VMEM((2, 1, H_A // 2 * QB), F32),
                        pltpu.VMEM((2, D_C + ONES_ROWS, H_A // 2 * QB), F32),
                        pltpu.VMEM((H_IDX, QB), F32),
                        pltpu.VMEM((1, QB), F32),
                        pltpu.VMEM((1, QB), F32),
                        pltpu.VMEM((1, QB), jnp.int32)],
        compiler_params=pltpu.CompilerParams(
            dimension_semantics=("arbitrary", "arbitrary"), vmem_limit_bytes=VMEM_LIMIT),
        name="dsa",
    )(main, main, main, gate_t, ckv_n, ckv_t, kidx_n, w_uk_t, w_uv_t, bias)


def _split_dot(tri, x):
    hi = x.astype(BF16)
    lo = (x - hi.astype(F32)).astype(BF16)
    return jnp.dot(tri, hi, preferred_element_type=F32) + jnp.dot(tri, lo, preferred_element_type=F32)


def _log_sigmoid(x):
    return jnp.minimum(x, 0.0) - jnp.log1p(jnp.exp(-jnp.abs(x)))


def _mlstm_kernel(q_ref, k_ref, qh_ref, kh_ref, v_ref, o_ref, z_ref, g_ref, gt_ref,
                  cw_ref, cb_ref, gbr_ref, gbc_ref, ng_ref, y_ref, ct_ref, m_ref):
    c = pl.program_id(1)
    L = L_M

    @pl.when(c == 0)
    def _():
        ct_ref[...] = jnp.zeros(ct_ref.shape, F32)
        m_ref[...] = jnp.zeros(m_ref.shape, F32)

    r = lax.broadcasted_iota(jnp.int32, (L, L), 0)
    s = lax.broadcasted_iota(jnp.int32, (L, L), 1)
    causal = s <= r
    shifts = [jnp.where(r - s == d, 1.0, 0.0).astype(BF16) for d in range(1, CONV_W)]

    def conv_silu(x_ref, halo_ref, lo):
        x = x_ref[...]
        halo = jnp.where(c > 0, halo_ref[...].astype(F32), 0.0)
        w = cw_ref[:, lo:lo + H_M * DK_M]
        y = cb_ref[:, lo:lo + H_M * DK_M] + w[CONV_W - 1:CONV_W] * x.astype(F32)
        top = jnp.zeros((SUBLANES, H_M * DK_M), F32)
        for d in range(1, CONV_W):
            wd = w[CONV_W - 1 - d:CONV_W - d]
            y = y + wd * jnp.dot(shifts[d - 1], x, preferred_element_type=F32)
            top = top + wd * jnp.concatenate(
                [halo[HALO - d:HALO], jnp.zeros((SUBLANES - d, H_M * DK_M), F32)], axis=0)
        y = jnp.concatenate([y[0:SUBLANES] + top, y[SUBLANES:]], axis=0)
        return y * jax.nn.sigmoid(y)

    q_all = conv_silu(q_ref, qh_ref, 0)
    k_all = conv_silu(k_ref, kh_ref, H_M * DK_M) * (DK_M ** -0.5)

    gc = g_ref[...] + gbr_ref[...]
    gr = gt_ref[S_IM:S_IM + 2 * H_M, :] + gbc_ref[S_IM:S_IM + 2 * H_M, :]
    tri_l = jnp.where(causal, 1.0, 0.0).astype(BF16)
    tri_u = jnp.where(r <= s, 1.0, 0.0).astype(BF16)
    b_cols = _split_dot(tri_l, _log_sigmoid(gc) * LOG2E)
    lf_rows = _log_sigmoid(gr) * LOG2E
    b_rows = jnp.dot(lf_rows.astype(BF16), tri_u, preferred_element_type=F32) \
        + jnp.dot((lf_rows - lf_rows.astype(BF16).astype(F32)).astype(BF16), tri_u,
                  preferred_element_type=F32)
    gc = gc * LOG2E
    gr = gr * LOG2E
    ones = jnp.ones((L, LANES), BF16)

    for h in range(H_M):
        q = q_all[:, h * DK_M:(h + 1) * DK_M]
        k = k_all[:, h * DK_M:(h + 1) * DK_M]
        v = jnp.concatenate([v_ref[:, h * DV_M:(h + 1) * DV_M], ones], axis=1)
        qb = q.astype(BF16)
        b_c = b_cols[:, S_FM + h:S_FM + h + 1]
        i_c = gc[:, S_IM + h:S_IM + h + 1]
        b_r = b_rows[H_M + h:H_M + h + 1, :]
        i_r = gr[h:h + 1, :]
        m_prev = m_ref[h]
        ct = ct_ref[h]

        log_d = jnp.where(causal, b_c - b_r + i_r, -jnp.inf)
        g = b_c + m_prev
        m_t = jnp.maximum(jnp.max(log_d, axis=-1, keepdims=True), g)
        qk = lax.dot_general(qb, k.astype(BF16), _NT, preferred_element_type=F32)
        s_mat = qk * jnp.exp2(log_d - m_t)
        inter = jnp.exp2(g - m_t)
        num = jnp.dot(s_mat.astype(BF16), v, preferred_element_type=F32) \
            + inter * jnp.dot(qb, ct.astype(BF16), preferred_element_type=F32)
        den = jnp.maximum(jnp.abs(num[:, DV_M:]), jnp.exp2(-m_t))
        hh = num[:, 0:DV_M] / jnp.concatenate([den] * (DV_M // LANES), axis=1)

        b_last = b_c[L - 1:L, :]
        a_r = b_last - b_r + i_r
        m_new = jnp.maximum(b_last + m_prev, jnp.max(a_r, axis=-1, keepdims=True))
        decay = jnp.exp2(b_last + m_prev - m_new)
        wgt_c = jnp.exp2(b_last - b_c + i_c - m_new)
        kw = k * wgt_c
        ct_ref[h] = decay * ct + jnp.dot(kw.T.astype(BF16), v, preferred_element_type=F32)
        m_ref[h] = m_new

        mu = jnp.mean(hh, axis=-1, keepdims=True)
        var = jnp.mean(jnp.square(hh - mu), axis=-1, keepdims=True)
        hn = (hh - mu) * lax.rsqrt(var + LN_EPS) * ng_ref[:, h * DV_M:(h + 1) * DV_M]
        og = o_ref[:, h * DV_M:(h + 1) * DV_M].astype(F32)
        zg = z_ref[:, h * DV_M:(h + 1) * DV_M].astype(F32)
        y_ref[:, h * DV_M:(h + 1) * DV_M] = (hn * jax.nn.sigmoid(og) * (zg * jax.nn.sigmoid(zg))).astype(BF16)


def _mlstm(main, small, gate_t, conv_w, conv_b, gb_row, gb_col, norm_g, B, T):
    nc = T // L_M
    hb = L_M // HALO
    qk_w = H_M * DK_M

    def halo_map(col):
        return lambda b, c: (jnp.maximum((b * nc + c) * hb - 1, 0), col)

    return pl.pallas_call(
        _mlstm_kernel,
        grid=(B, nc),
        in_specs=[pl.BlockSpec((L_M, qk_w), lambda b, c: (b * nc + c, C_QM // qk_w)),
                  pl.BlockSpec((L_M, qk_w), lambda b, c: (b * nc + c, C_KM // qk_w)),
                  pl.BlockSpec((HALO, qk_w), halo_map(C_QM // qk_w)),
                  pl.BlockSpec((HALO, qk_w), halo_map(C_KM // qk_w)),
                  pl.BlockSpec((L_M, W_M), lambda b, c: (b * nc + c, C_VM // W_M)),
                  pl.BlockSpec((L_M, W_M), lambda b, c: (b * nc + c, C_OM // W_M)),
                  pl.BlockSpec((L_M, W_M), lambda b, c: (b * nc + c, C_ZM // W_M)),
                  pl.BlockSpec((L_M, LANES), lambda b, c: (b * nc + c, D_C // LANES)),
                  pl.BlockSpec((LANES, L_M), lambda b, c: (0, b * nc + c)),
                  pl.BlockSpec((CONV_W, 2 * qk_w), lambda b, c: (0, 0)),
                  pl.BlockSpec((1, 2 * qk_w), lambda b, c: (0, 0)),
                  pl.BlockSpec((1, LANES), lambda b, c: (0, 0)),
                  pl.BlockSpec((LANES, 1), lambda b, c: (0, 0)),
                  pl.BlockSpec((1, W_M), lambda b, c: (0, 0))],
        out_specs=pl.BlockSpec((L_M, W_M), lambda b, c: (b * nc + c, 0)),
        out_shape=jax.ShapeDtypeStruct((B * T, W_M), BF16),
        scratch_shapes=[pltpu.VMEM((H_M, DK_M, DV_M + LANES), F32),
                        pltpu.VMEM((H_M, 1, 1), F32)],
        compiler_params=pltpu.CompilerParams(
            dimension_semantics=("arbitrary", "arbitrary"), vmem_limit_bytes=VMEM_LIMIT),
        name="mlstm",
    )(main, main, main, main, main, main, main, small, gate_t,
      conv_w, conv_b, gb_row, gb_col, norm_g)


def _out_kernel(ya_ref, ym_ref, x_ref, w_ref, g_ref, b_ref, o_ref):
    y = jnp.dot(ya_ref[...], w_ref[0:W_A, :], preferred_element_type=F32)
    y = y + jnp.dot(ym_ref[...], w_ref[W_A:W_A + W_M, :], preferred_element_type=F32)
    r = ALPHA * x_ref[...] + y
    mu = jnp.mean(r, axis=-1, keepdims=True)
    var = jnp.mean(jnp.square(r - mu), axis=-1, keepdims=True)
    o_ref[...] = (r - mu) * lax.rsqrt(var + LN_EPS) * g_ref[...] + b_ref[...]


def _out(ya, ym, x2d, w_out, ln_g, ln_b, tm=512):
    M = x2d.shape[0]
    return pl.pallas_call(
        _out_kernel,
        grid=(M // tm,),
        in_specs=[pl.BlockSpec((tm, W_A), lambda i: (i, 0)),
                  pl.BlockSpec((tm, W_M), lambda i: (i, 0)),
                  pl.BlockSpec((tm, D_MODEL), lambda i: (i, 0)),
                  pl.BlockSpec((W_A + W_M, D_MODEL), lambda i: (0, 0)),
                  pl.BlockSpec((1, D_MODEL), lambda i: (0, 0)),
                  pl.BlockSpec((1, D_MODEL), lambda i: (0, 0))],
        out_specs=pl.BlockSpec((tm, D_MODEL), lambda i: (i, 0)),
        out_shape=jax.ShapeDtypeStruct((M, D_MODEL), F32),
        compiler_params=pltpu.CompilerParams(
            dimension_semantics=("arbitrary",), vmem_limit_bytes=VMEM_LIMIT),
        name="out",
    )(ya, ym, x2d, w_out, ln_g, ln_b)


_W_IN_SEGS = (("q_a", W_A), ("c_kv", D_C), ("z_a", W_A), ("q_i", H_IDX * D_IDX), ("k_i", D_IDX),
              ("w_i", H_IDX), ("q_m", H_M * DK_M), ("k_m", H_M * DK_M), ("v_m", W_M), ("i_m", H_M),
              ("f_m", H_M), ("o_m", W_M), ("z_m", W_M))
_MAIN_ORDER = ("q_a", "z_a", "q_i", "q_m", "k_m", "v_m", "o_m", "z_m")
_SEG_NAMES = [name for name, _ in _W_IN_SEGS]
assert _SEG_NAMES.index("f_m") == _SEG_NAMES.index("i_m") + 1 and S_FM == S_IM + H_M


def _repack_kernel(wt_ref, main_ref, small_ref):
    src, off = {}, 0
    for name, width in _W_IN_SEGS:
        src[name] = (off, width)
        off += width
    dst = 0
    for name in _MAIN_ORDER:
        lo, width = src[name]
        main_ref[dst:dst + width, :] = wt_ref[lo:lo + width, :].astype(BF16)
        dst += width
    parts = [wt_ref[src[name][0]:src[name][0] + src[name][1], :] for name in ("c_kv", "k_i", "w_i")]
    lo = src["i_m"][0]
    parts.append(wt_ref[lo:lo + 2 * H_M, :])
    used = sum(p.shape[0] for p in parts)
    parts.append(jnp.zeros((N_SMALL - used, wt_ref.shape[1]), F32))
    small_ref[...] = jnp.concatenate(parts, axis=0).astype(BF16)


def _repack_w_in(w_in, tc=256):
    n_cols = sum(width for _, width in _W_IN_SEGS)
    wt = jnp.swapaxes(w_in, 1, 2)[0]
    return pl.pallas_call(
        _repack_kernel,
        grid=(D_MODEL // tc,),
        in_specs=[pl.BlockSpec((n_cols, tc), lambda i: (0, i))],
        out_specs=[pl.BlockSpec((N_MAIN, tc), lambda i: (0, i)),
                   pl.BlockSpec((N_SMALL, tc), lambda i: (0, i))],
        out_shape=[jax.ShapeDtypeStruct((N_MAIN, D_MODEL), BF16),
                   jax.ShapeDtypeStruct((N_SMALL, D_MODEL), BF16)],
        compiler_params=pltpu.CompilerParams(
            dimension_semantics=("arbitrary",), vmem_limit_bytes=VMEM_LIMIT),
        name="repack",
    )(wt)


def kernel(x, w_in, b_igate, b_fgate, kv_norm_g, w_uk, w_uv, idx_k_ln_g, idx_k_ln_b, rel_bias,
           conv_w, conv_b, mh_norm_g, w_out, ln_g, ln_b):
    B, T, D = x.shape
    assert D == D_MODEL and T % L_M == 0 and T % (2 * KB) == 0 and w_in.shape[0] == 1
    bias = _bias_tiles(rel_bias)
    x2d = x.reshape(B * T, D)
    w_main, w_small = _repack_w_in(w_in)
    main, small = _proj(x2d, w_main, w_small)
    ckv_n, ckv_t, kidx_n, gate_t = _prep(small, kv_norm_g[0][None], idx_k_ln_g[0][None], idx_k_ln_b[0][None])
    w_uk_t = jnp.transpose(w_uk[0], (0, 2, 1)).astype(BF16)
    w_uv_t = jnp.transpose(w_uv[0], (0, 2, 1)).astype(BF16)
    ya = _dsa(main, gate_t, ckv_n, ckv_t, kidx_n, w_uk_t, w_uv_t, bias, B, T)
    gb = jnp.zeros((LANES,), F32).at[S_IM:S_IM + H_M].set(b_igate[0]).at[S_FM:S_FM + H_M].set(b_fgate[0])
    ym = _mlstm(main, small, gate_t, conv_w[0], conv_b[0][None], gb[None, :], gb[:, None],
                mh_norm_g[0][None], B, T)
    out = _out(ya, ym, x2d, w_out[0].astype(BF16), ln_g[0][None], ln_b[0][None])
    return out.reshape(B, T, D)
```

```python
import functools
import math

import numpy as np
import jax
import jax.numpy as jnp
from jax import lax
from jax.experimental import pallas as pl
from jax.experimental.pallas import tpu as pltpu

F32 = jnp.float32
BF16 = jnp.bfloat16

D_MODEL = 2048
W_A = 1024
DH_A = 128
H_A = 8
D_C = 256
H_IDX = 16
D_IDX = 64
TOPK = 256
W_M = 1024
H_M = 4
DV_M = 256
DK_M = 128
CONV_W = 4
N_BUCKETS = 32
MAX_DIST = 128
ALPHA = 2.0 ** 0.25
LN_EPS = 1e-5

LANES = 128
SUBLANES = 8
VMEM_LIMIT = 56 * 1024 * 1024

QB = 256
KB = 256
KI = 128
CNT_ROWS = 512
CNT_ACC = 4 * SUBLANES
L_M = 256
OUT_CHUNKS = 2
HALO = 16
NEG = -1e30
LOG2E = math.log2(math.e)
ONES_ROWS = 16

C_QA, C_ZA, C_QI, C_QM, C_KM, C_VM, C_OM, C_ZM = 0, 1024, 2048, 3072, 3584, 4096, 5120, 6144
N_MAIN = 7168
N_SMALL = 384
S_KI, S_WI, S_IM, S_FM = 0, 64, 80, 84


def _t5_bucket_np(rel):
    max_exact = N_BUCKETS // 2
    n = np.maximum(rel, 0)
    nf = np.maximum(n, 1).astype(np.float32)
    large = max_exact + (np.log(nf / np.float32(max_exact)) / np.float32(math.log(MAX_DIST / max_exact))
                         * np.float32(N_BUCKETS - max_exact)).astype(np.int32)
    large = np.minimum(large, N_BUCKETS - 1)
    return np.where(n < max_exact, n, large).astype(np.int32)


FAR_BUCKET = int(_t5_bucket_np(np.array(2 * KB + 1)))


def _bucket_tiles():
    i = np.arange(QB)[None, :]
    j = np.arange(KB)[:, None]
    t0 = _t5_bucket_np(i - j)
    t1 = _t5_bucket_np(i - j + KB)
    assert (t5 := _t5_bucket_np(np.arange(KB + 1, 4096))).min() == t5.max() == FAR_BUCKET
    return np.stack([t0, t1]).astype(np.int32)


def _bias_kernel(bucket_ref, rb_ref, out_ref):
    h = pl.program_id(0)
    far = rb_ref[FAR_BUCKET, h]
    for k in range(2):
        bk = bucket_ref[k]
        acc = jnp.zeros((KB, QB), F32)
        for b in range(N_BUCKETS):
            acc = jnp.where(bk == b, rb_ref[b, h] - far, acc)
        out_ref[0, k] = acc * LOG2E
    out_ref[0, 2] = jnp.zeros((KB, QB), F32)


def _bias_tiles(rel_bias):
    bucket = jnp.asarray(_bucket_tiles())
    return pl.pallas_call(
        _bias_kernel,
        grid=(H_A,),
        in_specs=[pl.BlockSpec((2, KB, QB), lambda h: (0, 0, 0)),
                  pl.BlockSpec(memory_space=pltpu.SMEM)],
        out_specs=pl.BlockSpec((1, 3, KB, QB), lambda h: (h, 0, 0, 0)),
        out_shape=jax.ShapeDtypeStruct((H_A, 3, KB, QB), F32),
        name="bias_tiles",
    )(bucket, rel_bias)


_NT = (((1,), (1,)), ((), ()))


def _proj_kernel(x_ref, w_ref, ws_ref, o_ref, os_ref, xb_ref):
    @pl.when(pl.program_id(1) == 0)
    def _():
        xb_ref[...] = x_ref[...].astype(BF16)
        os_ref[...] = lax.dot_general(xb_ref[...], ws_ref[...], _NT, preferred_element_type=F32)

    o_ref[...] = lax.dot_general(xb_ref[...], w_ref[...], _NT, preferred_element_type=F32).astype(BF16)


def _proj(x2d, w_main, w_small, tm=1024, tn=1024):
    M = x2d.shape[0]
    return pl.pallas_call(
        _proj_kernel,
        grid=(M // tm, N_MAIN // tn),
        in_specs=[pl.BlockSpec((tm, D_MODEL), lambda i, j: (i, 0)),
                  pl.BlockSpec((tn, D_MODEL), lambda i, j: (j, 0)),
                  pl.BlockSpec((N_SMALL, D_MODEL), lambda i, j: (0, 0))],
        out_specs=[pl.BlockSpec((tm, tn), lambda i, j: (i, j)),
                   pl.BlockSpec((tm, N_SMALL), lambda i, j: (i, 0))],
        out_shape=[jax.ShapeDtypeStruct((M, N_MAIN), BF16),
                   jax.ShapeDtypeStruct((M, N_SMALL), F32)],
        scratch_shapes=[pltpu.VMEM((tm, D_MODEL), BF16)],
        compiler_params=pltpu.CompilerParams(
            dimension_semantics=("arbitrary", "arbitrary"), vmem_limit_bytes=VMEM_LIMIT),
        name="proj",
    )(x2d, w_main, w_small)


def _prep_kernel(s_ref, kvg_ref, ig_ref, ib_ref, ckv_ref, ckvt_ref, kidx_ref, gt_ref):
    c = s_ref[:, 0:D_C]
    c = c * lax.rsqrt(jnp.mean(c * c, axis=-1, keepdims=True) + LN_EPS) * kvg_ref[...]
    ckv_ref[...] = c.astype(BF16)
    for r in range(ckvt_ref.shape[0]):
        ckvt_ref[r, 0:D_C, :] = c[r * KB:(r + 1) * KB, :].T.astype(BF16)
        ckvt_ref[r, D_C:D_C + ONES_ROWS, :] = jnp.ones((ONES_ROWS, KB), BF16)
    tile = s_ref[:, D_C:D_C + LANES]
    k = tile[:, S_KI:S_KI + D_IDX]
    mu = jnp.mean(k, axis=-1, keepdims=True)
    var = jnp.mean(jnp.square(k - mu), axis=-1, keepdims=True)
    kidx_ref[...] = ((k - mu) * lax.rsqrt(var + LN_EPS) * ig_ref[...] + ib_ref[...]).astype(BF16)
    gt_ref[...] = tile.T


def _prep(small, kv_g, idx_g, idx_b, tm=1024):
    M = small.shape[0]
    return pl.pallas_call(
        _prep_kernel,
        grid=(M // tm,),
        in_specs=[pl.BlockSpec((tm, N_SMALL), lambda i: (i, 0)),
                  pl.BlockSpec((1, D_C), lambda i: (0, 0)),
                  pl.BlockSpec((1, D_IDX), lambda i: (0, 0)),
                  pl.BlockSpec((1, D_IDX), lambda i: (0, 0))],
        out_specs=[pl.BlockSpec((tm, D_C), lambda i: (i, 0)),
                   pl.BlockSpec((tm // KB, D_C + ONES_ROWS, KB), lambda i: (i, 0, 0)),
                   pl.BlockSpec((tm, D_IDX), lambda i: (i, 0)),
                   pl.BlockSpec((LANES, tm), lambda i: (0, i))],
        out_shape=[jax.ShapeDtypeStruct((M, D_C), BF16),
                   jax.ShapeDtypeStruct((M // KB, D_C + ONES_ROWS, KB), BF16),
                   jax.ShapeDtypeStruct((M, D_IDX), BF16),
                   jax.ShapeDtypeStruct((LANES, M), F32)],
        name="prep",
    )(small, kv_g, idx_g, idx_b)


def _key_to_float(key):
    bits = jnp.where(key < 0, key ^ jnp.int32(0x7FFFFFFF), key)
    return lax.bitcast_convert_type(bits, F32)


def _dsa_kernel(qa_ref, za_ref, qi_ref, gt_ref, ckv_ref, ckvt_ref, kidx_ref, wukt_ref, wuvt_ref, bias_ref,
                y_ref, qall_ref, qr_ref, ha_ref, hb_ref, sc_ref, mb_ref, sa_ref, sb_ref, pa_ref, pb_ref,
                ta_ref, tb_ref, m_ref, al_ref, acc_ref, w_ref, cand_ref, thr_ref, cut_ref):
    qi = pl.program_id(1)
    nkb = qi + 1
    last_kb = mb_ref.shape[0] - 1

    for h in range(H_A):
        ql = lax.dot_general(wukt_ref[h], qa_ref[:, h * DH_A:(h + 1) * DH_A], (((1,), (1,)), ((), ())),
                             preferred_element_type=F32)
        qall_ref[h // (H_A // 2), :, (h % (H_A // 2)) * QB:(h % (H_A // 2) + 1) * QB] = (
            ql * (DH_A ** -0.5 * LOG2E)).astype(BF16)


    @pl.when(qi * QB < TOPK)
    def _():
        krow = lax.broadcasted_iota(jnp.int32, (KB, QB), 0)
        qcol = lax.broadcasted_iota(jnp.int32, (KB, QB), 1)
        mb_ref[0] = jnp.where(krow <= qcol, 0.0, NEG).astype(F32)

    @pl.when(qi * QB >= TOPK)
    def _():
        for h in range(H_IDX):
            qr_ref[h * QB:(h + 1) * QB, :] = qi_ref[:, h * D_IDX:(h + 1) * D_IDX]
        w_ref[...] = gt_ref[S_WI:S_WI + H_IDX, :] * ((D_IDX ** -0.5) * (H_IDX ** -0.5))
        krow = lax.broadcasted_iota(jnp.int32, (KI, LANES), 0)
        qcol = lax.broadcasted_iota(jnp.int32, (KI, LANES), 1)
        n_ki = nkb * (KB // KI)
        last_ki = sc_ref.shape[0] // KI - 1

        def head_dots(ki, dst_ref):
            k = kidx_ref[pl.ds(pl.multiple_of(jnp.minimum(ki, last_ki) * KI, KI), KI), :]
            dst_ref[...] = lax.dot_general(k, qr_ref[...], (((1,), (1,)), ((), ())),
                                           preferred_element_type=F32)

        def reduce_heads(src_ref, ki):
            for g in range(QB // LANES):
                lanes = slice(g * LANES, (g + 1) * LANES)
                acc = jnp.zeros((KI, LANES), F32)
                for h in range(H_IDX):
                    acc = acc + (jnp.maximum(src_ref[:, h * QB + g * LANES:h * QB + (g + 1) * LANES], 0.0)
                                 * w_ref[h:h + 1, lanes])
                sc_ref[pl.ds(pl.multiple_of(ki * KI, KI), KI), lanes] = jnp.where(
                    krow + (ki * KI - qi * QB - g * LANES) <= qcol, acc, -jnp.inf)

        pad_kb = jnp.minimum(nkb, last_kb)
        sc_ref[pl.ds(pl.multiple_of(pad_kb * KB, KB), KB), :] = jnp.full((KB, QB), -jnp.inf, F32)

        head_dots(0, ha_ref)

        def sc_body(j, carry):
            head_dots(2 * j + 1, hb_ref)
            reduce_heads(ha_ref, 2 * j)
            head_dots(2 * j + 2, ha_ref)
            reduce_heads(hb_ref, 2 * j + 1)
            return carry
        lax.fori_loop(0, n_ki // 2, sc_body, 0)

        n_cnt = (nkb * KB + CNT_ROWS - 1) // CNT_ROWS

        def count_where(pred):
            def body(c, acc):
                parts = []
                for g in range(QB // LANES):
                    lanes = slice(g * LANES, (g + 1) * LANES)
                    blk = sc_ref[pl.ds(pl.multiple_of(c * CNT_ROWS, CNT_ROWS), CNT_ROWS), lanes]
                    hit = jnp.where(pred(blk, c * CNT_ROWS, lanes), 1.0, 0.0).astype(F32)
                    parts.append(jnp.sum(hit.reshape(CNT_ROWS // CNT_ACC, CNT_ACC, LANES), axis=0))
                return acc + jnp.concatenate(parts, axis=1)
            acc = lax.fori_loop(0, n_cnt, body, jnp.zeros((CNT_ACC, QB), F32))
            return jnp.sum(acc, axis=0, keepdims=True)

        def bit_body(i, carry):
            u, c_ge = carry
            trial = u | lax.shift_left(jnp.int32(1), 31 - i)
            cand_ref[...] = _key_to_float(trial ^ jnp.int32(-2 ** 31))
            cnt = count_where(lambda blk, row0, lanes: blk >= cand_ref[:, lanes])
            ok = cnt >= float(TOPK)
            return jnp.where(ok, trial, u), jnp.where(ok, cnt, c_ge)
        u, c_ge = lax.fori_loop(0, 32, bit_body, (jnp.zeros((1, QB), jnp.int32),
                                                  jnp.full((1, QB), float(sc_ref.shape[0]), F32)))
        thr_ref[...] = _key_to_float(u ^ jnp.int32(-2 ** 31))
        has_ties = jnp.max(c_ge) > float(TOPK)

        @pl.when(jnp.logical_not(has_ties))
        def _():
            def mb_body(kb, carry):
                blk = sc_ref[pl.ds(pl.multiple_of(kb * KB, KB), KB), :]
                mb_ref[kb] = jnp.where(blk >= thr_ref[...], 0.0, NEG).astype(F32)
                return carry
            lax.fori_loop(0, nkb, mb_body, 0)

        @pl.when(has_ties)
        def _():
            c_gt = count_where(lambda blk, row0, lanes: blk > thr_ref[:, lanes])
            need = float(TOPK) - c_gt
            rows = lax.broadcasted_iota(jnp.int32, (CNT_ROWS, LANES), 0)
            n_bits = (sc_ref.shape[0] - 1).bit_length()

            def idx_body(i, cut):
                trial = cut | lax.shift_left(jnp.int32(1), n_bits - 1 - i)
                cut_ref[...] = trial
                before = count_where(lambda blk, row0, lanes: (blk == thr_ref[:, lanes])
                                     & (rows + row0 < cut_ref[:, lanes]))
                return jnp.where(before < need, trial, cut)
            cut_ref[...] = lax.fori_loop(0, n_bits, idx_body, jnp.zeros((1, QB), jnp.int32))

            def mb_body(kb, carry):
                for g in range(QB // LANES):
                    lanes = slice(g * LANES, (g + 1) * LANES)
                    blk = sc_ref[pl.ds(pl.multiple_of(kb * KB, KB), KB), lanes]
                    thr = thr_ref[:, lanes]
                    keep = (blk > thr) | ((blk == thr) & (rows[0:KB] + kb * KB <= cut_ref[:, lanes]))
                    mb_ref[kb, :, lanes] = jnp.where(keep, 0.0, NEG).astype(F32)
                return carry
            lax.fori_loop(0, nkb, mb_body, 0)

    m_ref[...] = jnp.full(m_ref.shape, NEG, F32)
    acc_ref[...] = jnp.zeros(acc_ref.shape, F32)

    hh = H_A // 2
    s_refs, p_refs, t_refs = (sa_ref, sb_ref), (pa_ref, pb_ref), (ta_ref, tb_ref)
    pb_ref[...] = jnp.zeros(pb_ref.shape, BF16)
    al_ref[...] = jnp.ones(al_ref.shape, F32)

    def logits(kb, half):
        kv = ckv_ref[pl.ds(pl.multiple_of(jnp.minimum(kb, last_kb) * KB, KB), KB), :]
        s_refs[half][...] = jnp.dot(kv, qall_ref[half], preferred_element_type=F32)

    def softmax(kb, half, with_bias):
        tile = jnp.clip(qi - kb, 0, 2)
        for j in range(hh):
            for g in range(QB // LANES):
                lanes = slice(g * LANES, (g + 1) * LANES)
                cols = slice(j * QB + g * LANES, j * QB + (g + 1) * LANES)
                x = s_refs[half][:, cols] + mb_ref[kb, :, lanes]
                if with_bias:
                    x = x + bias_ref[half * hh + j, tile, :, lanes]
                m_prev = m_ref[half, :, cols]
                m_blk = jnp.max(x.reshape(KB // CNT_ACC, CNT_ACC, LANES), axis=0)
                m_new = jnp.maximum(m_prev, jnp.max(m_blk, axis=0, keepdims=True))
                al_ref[half, :, cols] = jnp.exp2(m_prev - m_new)
                m_ref[half, :, cols] = m_new
                p_refs[half][:, cols] = jnp.exp2(x - m_new).astype(BF16)

    def accumulate(kb, half):
        t_refs[half][...] = jnp.dot(ckvt_ref[jnp.maximum(kb, 0)], p_refs[half][...],
                                    preferred_element_type=F32)
        acc_ref[half] = acc_ref[half] * al_ref[half] + t_refs[half][...]

    def sweep(first_kb, end_kb, with_bias):
        def body(kb, carry):
            logits(kb, 1)
            softmax(kb, 0, with_bias)
            accumulate(kb - 1, 1)
            logits(kb + 1, 0)
            softmax(kb, 1, with_bias)
            accumulate(kb, 0)
            return carry
        lax.fori_loop(first_kb, end_kb, body, 0)

    n_far = jnp.maximum(qi - 1, 0)
    logits(0, 0)
    sweep(0, n_far, False)
    sweep(n_far, nkb, True)
    accumulate(nkb - 1, 1)

    for h in range(H_A):
        half, cols = h // hh, slice((h % hh) * QB, (h % hh + 1) * QB)
        ya_t = jnp.dot(wuvt_ref[h], acc_ref[half, 0:D_C, cols].astype(BF16), preferred_element_type=F32)
        ya = (ya_t / acc_ref[half, D_C:D_C + 1, cols]).T
        z = za_ref[:, h * DH_A:(h + 1) * DH_A].astype(F32)
        y_ref[:, h * DH_A:(h + 1) * DH_A] = (ya * (z * jax.nn.sigmoid(z))).astype(BF16)


def _dsa(main, gate_t, ckv_n, ckv_t, kidx_n, w_uk_t, w_uv_t, bias, B, T):
    nq = T // QB
    return pl.pallas_call(
        _dsa_kernel,
        grid=(B, nq),
        in_specs=[pl.BlockSpec((QB, W_A), lambda b, q: (b * nq + q, C_QA // W_A)),
                  pl.BlockSpec((QB, W_A), lambda b, q: (b * nq + q, C_ZA // W_A)),
                  pl.BlockSpec((QB, H_IDX * D_IDX), lambda b, q: (b * nq + q, C_QI // (H_IDX * D_IDX))),
                  pl.BlockSpec((LANES, QB), lambda b, q: (0, b * nq + q)),
                  pl.BlockSpec((T, D_C), lambda b, q: (b, 0)),
                  pl.BlockSpec((T // KB, D_C + ONES_ROWS, KB), lambda b, q: (b, 0, 0)),
                  pl.BlockSpec((T, D_IDX), lambda b, q: (b, 0)),
                  pl.BlockSpec((H_A, D_C, DH_A), lambda b, q: (0, 0, 0)),
                  pl.BlockSpec((H_A, DH_A, D_C), lambda b, q: (0, 0, 0)),
                  pl.BlockSpec((H_A, 3, KB, QB), lambda b, q: (0, 0, 0, 0))],
        out_specs=pl.BlockSpec((QB, W_A), lambda b, q: (b * nq + q, 0)),
        out_shape=jax.ShapeDtypeStruct((B * T, W_A), BF16),
        scratch_shapes=[pltpu.VMEM((2, D_C, H_A // 2 * QB), BF16),
                        pltpu.VMEM((H_IDX * QB, D_IDX), BF16),
                        pltpu.VMEM((KI, H_IDX * QB), F32),
                        pltpu.VMEM((KI, H_IDX * QB), F32),
                        pltpu.VMEM((T, QB), F32),
                        pltpu.VMEM((T // KB, KB, QB), F32),
                        pltpu.VMEM((KB, H_A // 2 * QB), F32),
                        pltpu.VMEM((KB, H_A // 2 * QB), F32),
                        pltpu.VMEM((KB, H_A // 2 * QB), BF16),
                        pltpu.VMEM((KB, H_A // 2 * QB), BF16),
                        pltpu.VMEM((D_C + ONES_ROWS, H_A // 2 * QB), F32),
                        pltpu.VMEM((D_C + ONES_ROWS, H_A // 2 * QB), F32),
                        pltpu.VMEM((2, 1, H_A // 2 * QB), F32),
                        pltpu.VMEM((2, 1, H_A // 2 * QB), F32),
                        pltpu.VMEM((2, D_C + ONES_ROWS, H_A // 2 * QB), F32),
                        pltpu.VMEM((H_IDX, QB), F32),
                        pltpu.VMEM((1, QB), F32),
                        pltpu.VMEM((1, QB), F32),
                        pltpu.VMEM((1, QB), jnp.int32)],
        compiler_params=pltpu.CompilerParams(
            dimension_semantics=("arbitrary", "arbitrary"), vmem_limit_bytes=VMEM_LIMIT),
        name="dsa",
    )(main, main, main, gate_t, ckv_n, ckv_t, kidx_n, w_uk_t, w_uv_t, bias)


def _split_dot(tri, x):
    hi = x.astype(BF16)
    lo = (x - hi.astype(F32)).astype(BF16)
    return jnp.dot(tri, hi, preferred_element_type=F32) + jnp.dot(tri, lo, preferred_element_type=F32)


def _log_sigmoid(x):
    return jnp.minimum(x, 0.0) - jnp.log1p(jnp.exp(-jnp.abs(x)))


def _mlstm_kernel(q_ref, k_ref, qh_ref, kh_ref, v_ref, o_ref, z_ref, g_ref, gt_ref,
                  cw_ref, cb_ref, gbr_ref, gbc_ref, ng_ref, y_ref, ct_ref, m_ref):
    c = pl.program_id(1)
    L = L_M

    @pl.when(c == 0)
    def _():
        ct_ref[...] = jnp.zeros(ct_ref.shape, F32)
        m_ref[...] = jnp.zeros(m_ref.shape, F32)

    r = lax.broadcasted_iota(jnp.int32, (L, L), 0)
    s = lax.broadcasted_iota(jnp.int32, (L, L), 1)
    causal = s <= r
    shifts = [jnp.where(r - s == d, 1.0, 0.0).astype(BF16) for d in range(1, CONV_W)]

    def conv_silu(x_ref, halo_ref, lo):
        x = x_ref[...]
        halo = jnp.where(c > 0, halo_ref[...].astype(F32), 0.0)
        w = cw_ref[:, lo:lo + H_M * DK_M]
        y = cb_ref[:, lo:lo + H_M * DK_M] + w[CONV_W - 1:CONV_W] * x.astype(F32)
        top = jnp.zeros((SUBLANES, H_M * DK_M), F32)
        for d in range(1, CONV_W):
            wd = w[CONV_W - 1 - d:CONV_W - d]
            y = y + wd * jnp.dot(shifts[d - 1], x, preferred_element_type=F32)
            top = top + wd * jnp.concatenate(
                [halo[HALO - d:HALO], jnp.zeros((SUBLANES - d, H_M * DK_M), F32)], axis=0)
        y = jnp.concatenate([y[0:SUBLANES] + top, y[SUBLANES:]], axis=0)
        return y * jax.nn.sigmoid(y)

    q_all = conv_silu(q_ref, qh_ref, 0)
    k_all = conv_silu(k_ref, kh_ref, H_M * DK_M) * (DK_M ** -0.5)

    gc = g_ref[...] + gbr_ref[...]
    gr = gt_ref[S_IM:S_IM + 2 * H_M, :] + gbc_ref[S_IM:S_IM + 2 * H_M, :]
    tri_l = jnp.where(causal, 1.0, 0.0).astype(BF16)
    tri_u = jnp.where(r <= s, 1.0, 0.0).astype(BF16)
    b_cols = _split_dot(tri_l, _log_sigmoid(gc) * LOG2E)
    lf_rows = _log_sigmoid(gr) * LOG2E
    b_rows = jnp.dot(lf_rows.astype(BF16), tri_u, preferred_element_type=F32) \
        + jnp.dot((lf_rows - lf_rows.astype(BF16).astype(F32)).astype(BF16), tri_u,
                  preferred_element_type=F32)
    gc = gc * LOG2E
    gr = gr * LOG2E
    ones = jnp.ones((L, LANES), BF16)

    for h in range(H_M):
        q = q_all[:, h * DK_M:(h + 1) * DK_M]
        k = k_all[:, h * DK_M:(h + 1) * DK_M]
        v = jnp.concatenate([v_ref[:, h * DV_M:(h + 1) * DV_M], ones], axis=1)
        qb = q.astype(BF16)
        b_c = b_cols[:, S_FM + h:S_FM + h + 1]
        i_c = gc[:, S_IM + h:S_IM + h + 1]
        b_r = b_rows[H_M + h:H_M + h + 1, :]
        i_r = gr[h:h + 1, :]
        m_prev = m_ref[h]
        ct = ct_ref[h]

        log_d = jnp.where(causal, b_c - b_r + i_r, -jnp.inf)
        g = b_c + m_prev
        m_t = jnp.maximum(jnp.max(log_d, axis=-1, keepdims=True), g)
        qk = lax.dot_general(qb, k.astype(BF16), _NT, preferred_element_type=F32)
        s_mat = qk * jnp.exp2(log_d - m_t)
        inter = jnp.exp2(g - m_t)
        num = jnp.dot(s_mat.astype(BF16), v, preferred_element_type=F32) \
            + inter * jnp.dot(qb, ct.astype(BF16), preferred_element_type=F32)
        den = jnp.maximum(jnp.abs(num[:, DV_M:]), jnp.exp2(-m_t))
        hh = num[:, 0:DV_M] / jnp.concatenate([den] * (DV_M // LANES), axis=1)

        b_last = b_c[L - 1:L, :]
        a_r = b_last - b_r + i_r
        m_new = jnp.maximum(b_last + m_prev, jnp.max(a_r, axis=-1, keepdims=True))
        decay = jnp.exp2(b_last + m_prev - m_new)
        wgt_c = jnp.exp2(b_last - b_c + i_c - m_new)
        kw = k * wgt_c
        ct_ref[h] = decay * ct + jnp.dot(kw.T.astype(BF16), v, preferred_element_type=F32)
        m_ref[h] = m_new

        mu = jnp.mean(hh, axis=-1, keepdims=True)
        var = jnp.mean(jnp.square(hh - mu), axis=-1, keepdims=True)
        hn = (hh - mu) * lax.rsqrt(var + LN_EPS) * ng_ref[:, h * DV_M:(h + 1) * DV_M]
        og = o_ref[:, h * DV_M:(h + 1) * DV_M].astype(F32)
        zg = z_ref[:, h * DV_M:(h + 1) * DV_M].astype(F32)
        y_ref[:, h * DV_M:(h + 1) * DV_M] = (hn * jax.nn.sigmoid(og) * (zg * jax.nn.sigmoid(zg))).astype(BF16)


def _mlstm(main, small, gate_t, conv_w, conv_b, gb_row, gb_col, norm_g, B, T):
    nc = T // L_M
    hb = L_M // HALO
    qk_w = H_M * DK_M

    def halo_map(col):
        return lambda b, c: (jnp.maximum((b * nc + c) * hb - 1, 0), col)

    return pl.pallas_call(
        _mlstm_kernel,
        grid=(B, nc),
        in_specs=[pl.BlockSpec((L_M, qk_w), lambda b, c: (b * nc + c, C_QM // qk_w)),
                  pl.BlockSpec((L_M, qk_w), lambda b, c: (b * nc + c, C_KM // qk_w)),
                  pl.BlockSpec((HALO, qk_w), halo_map(C_QM // qk_w)),
                  pl.BlockSpec((HALO, qk_w), halo_map(C_KM // qk_w)),
                  pl.BlockSpec((L_M, W_M), lambda b, c: (b * nc + c, C_VM // W_M)),
                  pl.BlockSpec((L_M, W_M), lambda b, c: (b * nc + c, C_OM // W_M)),
                  pl.BlockSpec((L_M, W_M), lambda b, c: (b * nc + c, C_ZM // W_M)),
                  pl.BlockSpec((L_M, LANES), lambda b, c: (b * nc + c, D_C // LANES)),
                  pl.BlockSpec((LANES, L_M), lambda b, c: (0, b * nc + c)),
                  pl.BlockSpec((CONV_W, 2 * qk_w), lambda b, c: (0, 0)),
                  pl.BlockSpec((1, 2 * qk_w), lambda b, c: (0, 0)),
                  pl.BlockSpec((1, LANES), lambda b, c: (0, 0)),
                  pl.BlockSpec((LANES, 1), lambda b, c: (0, 0)),
                  pl.BlockSpec((1, W_M), lambda b, c: (0, 0))],
        out_specs=pl.BlockSpec((L_M, W_M), lambda b, c: (b * nc + c, 0)),
        out_shape=jax.ShapeDtypeStruct((B * T, W_M), BF16),
        scratch_shapes=[pltpu.VMEM((H_M, DK_M, DV_M + LANES), F32),
                        pltpu.VMEM((H_M, 1, 1), F32)],
        compiler_params=pltpu.CompilerParams(
            dimension_semantics=("arbitrary", "arbitrary"), vmem_limit_bytes=VMEM_LIMIT),
        name="mlstm",
    )(main, main, main, main, main, main, main, small, gate_t,
      conv_w, conv_b, gb_row, gb_col, norm_g)


def _out_kernel(ya0_ref, ym0_ref, ya1_ref, ym1_ref, ya2_ref, ym2_ref, x_ref, w_ref, g_ref, b_ref,
                o_ref, pa_ref, pb_ref):
    n_chunks, th = pa_ref.shape[0], pa_ref.shape[1]
    tr = th // n_chunks

    def project(ya_ref, ym_ref, dst_ref, c):
        dst_ref[c] = (jnp.dot(ya_ref[...], w_ref[c, 0:W_A, :], preferred_element_type=F32)
                      + jnp.dot(ym_ref[...], w_ref[c, W_A:W_A + W_M, :], preferred_element_type=F32))

    def norm(src_ref, base, c):
        cw = pa_ref.shape[2]
        cols = [slice(k * cw, (k + 1) * cw) for k in range(n_chunks)]
        src_rows = pl.ds(pl.multiple_of(c * tr, tr), tr)
        rows = pl.ds(pl.multiple_of(base + c * tr, tr), tr)
        r = [ALPHA * x_ref[rows, cols[k]] + src_ref[k, src_rows, :] for k in range(n_chunks)]
        mu = sum(jnp.sum(p, axis=-1, keepdims=True) for p in r) * (1.0 / D_MODEL)
        d = [p - mu for p in r]
        var = sum(jnp.sum(p * p, axis=-1, keepdims=True) for p in d) * (1.0 / D_MODEL)
        rstd = lax.rsqrt(var + LN_EPS)
        for k in range(n_chunks):
            o_ref[rows, cols[k]] = d[k] * rstd * g_ref[:, cols[k]] + b_ref[:, cols[k]]

    @pl.when(pl.program_id(0) == 0)
    def _():
        for c in range(n_chunks):
            project(ya0_ref, ym0_ref, pa_ref, c)

    def first(c, carry):
        project(ya1_ref, ym1_ref, pb_ref, c)
        norm(pa_ref, 0, c)
        return carry
    lax.fori_loop(0, n_chunks, first, 0)

    def second(c, carry):
        project(ya2_ref, ym2_ref, pa_ref, c)
        norm(pb_ref, th, c)
        return carry
    lax.fori_loop(0, n_chunks, second, 0)


def _out(ya, ym, x2d, w_out, ln_g, ln_b, tm=512):
    M = x2d.shape[0]
    th, n = tm // 2, M // tm
    last = 2 * n - 1
    n_chunks = OUT_CHUNKS
    cw = D_MODEL // n_chunks
    w3 = jnp.transpose(w_out.reshape(W_A + W_M, n_chunks, cw), (1, 0, 2))

    def half(fn):
        return [pl.BlockSpec((th, W_A), fn), pl.BlockSpec((th, W_M), fn)]

    return pl.pallas_call(
        _out_kernel,
        grid=(n,),
        in_specs=half(lambda i: (0, 0)) + half(lambda i: (2 * i + 1, 0))
        + half(lambda i: (jnp.minimum(2 * i + 2, last), 0))
        + [pl.BlockSpec((tm, D_MODEL), lambda i: (i, 0)),
           pl.BlockSpec((n_chunks, W_A + W_M, cw), lambda i: (0, 0, 0)),
           pl.BlockSpec((1, D_MODEL), lambda i: (0, 0)),
           pl.BlockSpec((1, D_MODEL), lambda i: (0, 0))],
        out_specs=pl.BlockSpec((tm, D_MODEL), lambda i: (i, 0)),
        out_shape=jax.ShapeDtypeStruct((M, D_MODEL), F32),
        scratch_shapes=[pltpu.VMEM((n_chunks, th, cw), F32),
                        pltpu.VMEM((n_chunks, th, cw), F32)],
        compiler_params=pltpu.CompilerParams(
            dimension_semantics=("arbitrary",), vmem_limit_bytes=VMEM_LIMIT),
        name="out",
    )(ya, ym, ya, ym, ya, ym, x2d, w3, ln_g, ln_b)


_W_IN_SEGS = (("q_a", W_A), ("c_kv", D_C), ("z_a", W_A), ("q_i", H_IDX * D_IDX), ("k_i", D_IDX),
              ("w_i", H_IDX), ("q_m", H_M * DK_M), ("k_m", H_M * DK_M), ("v_m", W_M), ("i_m", H_M),
              ("f_m", H_M), ("o_m", W_M), ("z_m", W_M))
_MAIN_ORDER = ("q_a", "z_a", "q_i", "q_m", "k_m", "v_m", "o_m", "z_m")
_SEG_NAMES = [name for name, _ in _W_IN_SEGS]
assert _SEG_NAMES.index("f_m") == _SEG_NAMES.index("i_m") + 1 and S_FM == S_IM + H_M


def _repack_kernel(wt_ref, main_ref, small_ref):
    src, off = {}, 0
    for name, width in _W_IN_SEGS:
        src[name] = (off, width)
        off += width
    dst = 0
    for name in _MAIN_ORDER:
        lo, width = src[name]
        main_ref[dst:dst + width, :] = wt_ref[lo:lo + width, :].astype(BF16)
        dst += width
    parts = [wt_ref[src[name][0]:src[name][0] + src[name][1], :] for name in ("c_kv", "k_i", "w_i")]
    lo = src["i_m"][0]
    parts.append(wt_ref[lo:lo + 2 * H_M, :])
    used = sum(p.shape[0] for p in parts)
    parts.append(jnp.zeros((N_SMALL - used, wt_ref.shape[1]), F32))
    small_ref[...] = jnp.concatenate(parts, axis=0).astype(BF16)


def _repack_w_in(w_in, tc=256):
    n_cols = sum(width for _, width in _W_IN_SEGS)
    wt = jnp.swapaxes(w_in, 1, 2)[0]
    return pl.pallas_call(
        _repack_kernel,
        grid=(D_MODEL // tc,),
        in_specs=[pl.BlockSpec((n_cols, tc), lambda i: (0, i))],
        out_specs=[pl.BlockSpec((N_MAIN, tc), lambda i: (0, i)),
                   pl.BlockSpec((N_SMALL, tc), lambda i: (0, i))],
        out_shape=[jax.ShapeDtypeStruct((N_MAIN, D_MODEL), BF16),
                   jax.ShapeDtypeStruct((N_SMALL, D_MODEL), BF16)],
        compiler_params=pltpu.CompilerParams(
            dimension_semantics=("arbitrary",), vmem_limit_bytes=VMEM_LIMIT),
        name="repack",
    )(wt)


def kernel(x, w_in, b_igate, b_fgate, kv_norm_g, w_uk, w_uv, idx_k_ln_g, idx_k_ln_b, rel_bias,
           conv_w, conv_b, mh_norm_g, w_out, ln_g, ln_b):
    B, T, D = x.shape
    assert D == D_MODEL and T % L_M == 0 and T % (2 * KB) == 0 and w_in.shape[0] == 1
    bias = _bias_tiles(rel_bias)
    x2d = x.reshape(B * T, D)
    w_main, w_small = _repack_w_in(w_in)
    main, small = _proj(x2d, w_main, w_small)
    ckv_n, ckv_t, kidx_n, gate_t = _prep(small, kv_norm_g[0][None], idx_k_ln_g[0][None], idx_k_ln_b[0][None])
    w_uk_t = jnp.transpose(w_uk[0], (0, 2, 1)).astype(BF16)
    w_uv_t = jnp.transpose(w_uv[0], (0, 2, 1)).astype(BF16)
    ya = _dsa(main, gate_t, ckv_n, ckv_t, kidx_n, w_uk_t, w_uv_t, bias, B, T)
    gb = jnp.zeros((LANES,), F32).at[S_IM:S_IM + H_M].set(b_igate[0]).at[S_FM:S_FM + H_M].set(b_fgate[0])
    ym = _mlstm(main, small, gate_t, conv_w[0], conv_b[0][None], gb[None, :], gb[:, None],
                mh_norm_g[0][None], B, T)
    out = _out(ya, ym, x2d, w_out[0].astype(BF16), ln_g[0][None], ln_b[0][None])
    return out.reshape(B, T, D)
```

```python
import functools
import math

import numpy as np
import jax
import jax.numpy as jnp
from jax import lax
from jax.experimental import pallas as pl
from jax.experimental.pallas import tpu as pltpu

F32 = jnp.float32
BF16 = jnp.bfloat16

D_MODEL = 2048
W_A = 1024
DH_A = 128
H_A = 8
D_C = 256
H_IDX = 16
D_IDX = 64
TOPK = 256
W_M = 1024
H_M = 4
DV_M = 256
DK_M = 128
CONV_W = 4
N_BUCKETS = 32
MAX_DIST = 128
ALPHA = 2.0 ** 0.25
LN_EPS = 1e-5

LANES = 128
SUBLANES = 8
VMEM_LIMIT = 56 * 1024 * 1024

QB = 256
KB = 256
KI = 128
CNT_ROWS = 512
CNT_ACC = 4 * SUBLANES
L_M = 256
HALO = 16
NEG = -1e30
KEY_POS_INF = 0x7F800000
KEY_NEG_INF = -KEY_POS_INF - 1
LOG2E = math.log2(math.e)
ONES_ROWS = 16

C_QA, C_ZA, C_QI, C_QM, C_KM, C_VM, C_OM, C_ZM = 0, 1024, 2048, 3072, 3584, 4096, 5120, 6144
N_MAIN = 7168
N_SMALL = 384
S_KI, S_WI, S_IM, S_FM = 0, 64, 80, 84


def _t5_bucket_np(rel):
    max_exact = N_BUCKETS // 2
    n = np.maximum(rel, 0)
    nf = np.maximum(n, 1).astype(np.float32)
    large = max_exact + (np.log(nf / np.float32(max_exact)) / np.float32(math.log(MAX_DIST / max_exact))
                         * np.float32(N_BUCKETS - max_exact)).astype(np.int32)
    large = np.minimum(large, N_BUCKETS - 1)
    return np.where(n < max_exact, n, large).astype(np.int32)


FAR_BUCKET = int(_t5_bucket_np(np.array(2 * KB + 1)))


def _bucket_tiles():
    i = np.arange(QB)[None, :]
    j = np.arange(KB)[:, None]
    t0 = _t5_bucket_np(i - j)
    t1 = _t5_bucket_np(i - j + KB)
    assert (t5 := _t5_bucket_np(np.arange(KB + 1, 4096))).min() == t5.max() == FAR_BUCKET
    return np.stack([t0, t1]).astype(np.int32)


def _bias_kernel(bucket_ref, rb_ref, out_ref):
    h = pl.program_id(0)
    far = rb_ref[FAR_BUCKET, h]
    for k in range(2):
        bk = bucket_ref[k]
        acc = jnp.zeros((KB, QB), F32)
        for b in range(N_BUCKETS):
            acc = jnp.where(bk == b, rb_ref[b, h] - far, acc)
        out_ref[0, k] = acc * LOG2E
    out_ref[0, 2] = jnp.zeros((KB, QB), F32)


def _bias_tiles(rel_bias):
    bucket = jnp.asarray(_bucket_tiles())
    return pl.pallas_call(
        _bias_kernel,
        grid=(H_A,),
        in_specs=[pl.BlockSpec((2, KB, QB), lambda h: (0, 0, 0)),
                  pl.BlockSpec(memory_space=pltpu.SMEM)],
        out_specs=pl.BlockSpec((1, 3, KB, QB), lambda h: (h, 0, 0, 0)),
        out_shape=jax.ShapeDtypeStruct((H_A, 3, KB, QB), F32),
        name="bias_tiles",
    )(bucket, rel_bias)


_NT = (((1,), (1,)), ((), ()))


def _proj_kernel(x_ref, w_ref, ws_ref, o_ref, os_ref, xb_ref):
    @pl.when(pl.program_id(1) == 0)
    def _():
        xb_ref[...] = x_ref[...].astype(BF16)
        os_ref[...] = lax.dot_general(xb_ref[...], ws_ref[...], _NT, preferred_element_type=F32)

    o_ref[...] = lax.dot_general(xb_ref[...], w_ref[...], _NT, preferred_element_type=F32).astype(BF16)


def _proj(x2d, w_main, w_small, tm=1024, tn=1024):
    M = x2d.shape[0]
    return pl.pallas_call(
        _proj_kernel,
        grid=(M // tm, N_MAIN // tn),
        in_specs=[pl.BlockSpec((tm, D_MODEL), lambda i, j: (i, 0)),
                  pl.BlockSpec((tn, D_MODEL), lambda i, j: (j, 0)),
                  pl.BlockSpec((N_SMALL, D_MODEL), lambda i, j: (0, 0))],
        out_specs=[pl.BlockSpec((tm, tn), lambda i, j: (i, j)),
                   pl.BlockSpec((tm, N_SMALL), lambda i, j: (i, 0))],
        out_shape=[jax.ShapeDtypeStruct((M, N_MAIN), BF16),
                   jax.ShapeDtypeStruct((M, N_SMALL), F32)],
        scratch_shapes=[pltpu.VMEM((tm, D_MODEL), BF16)],
        compiler_params=pltpu.CompilerParams(
            dimension_semantics=("arbitrary", "arbitrary"), vmem_limit_bytes=VMEM_LIMIT),
        name="proj",
    )(x2d, w_main, w_small)


def _prep_kernel(s_ref, kvg_ref, ig_ref, ib_ref, ckv_ref, ckvt_ref, kidx_ref, gt_ref):
    c = s_ref[:, 0:D_C]
    c = c * lax.rsqrt(jnp.mean(c * c, axis=-1, keepdims=True) + LN_EPS) * kvg_ref[...]
    ckv_ref[...] = c.astype(BF16)
    for r in range(ckvt_ref.shape[0]):
        ckvt_ref[r, 0:D_C, :] = c[r * KB:(r + 1) * KB, :].T.astype(BF16)
        ckvt_ref[r, D_C:D_C + ONES_ROWS, :] = jnp.ones((ONES_ROWS, KB), BF16)
    tile = s_ref[:, D_C:D_C + LANES]
    k = tile[:, S_KI:S_KI + D_IDX]
    mu = jnp.mean(k, axis=-1, keepdims=True)
    var = jnp.mean(jnp.square(k - mu), axis=-1, keepdims=True)
    kidx_ref[...] = ((k - mu) * lax.rsqrt(var + LN_EPS) * ig_ref[...] + ib_ref[...]).astype(BF16)
    gt_ref[...] = tile.T


def _prep(small, kv_g, idx_g, idx_b, tm=1024):
    M = small.shape[0]
    return pl.pallas_call(
        _prep_kernel,
        grid=(M // tm,),
        in_specs=[pl.BlockSpec((tm, N_SMALL), lambda i: (i, 0)),
                  pl.BlockSpec((1, D_C), lambda i: (0, 0)),
                  pl.BlockSpec((1, D_IDX), lambda i: (0, 0)),
                  pl.BlockSpec((1, D_IDX), lambda i: (0, 0))],
        out_specs=[pl.BlockSpec((tm, D_C), lambda i: (i, 0)),
                   pl.BlockSpec((tm // KB, D_C + ONES_ROWS, KB), lambda i: (i, 0, 0)),
                   pl.BlockSpec((tm, D_IDX), lambda i: (i, 0)),
                   pl.BlockSpec((LANES, tm), lambda i: (0, i))],
        out_shape=[jax.ShapeDtypeStruct((M, D_C), BF16),
                   jax.ShapeDtypeStruct((M // KB, D_C + ONES_ROWS, KB), BF16),
                   jax.ShapeDtypeStruct((M, D_IDX), BF16),
                   jax.ShapeDtypeStruct((LANES, M), F32)],
        name="prep",
    )(small, kv_g, idx_g, idx_b)


def _key_to_float(key):
    bits = jnp.where(key < 0, key ^ jnp.int32(0x7FFFFFFF), key)
    return lax.bitcast_convert_type(bits, F32)


def _dsa_kernel(qa_ref, za_ref, qi_ref, gt_ref, ckv_ref, ckvt_ref, kidx_ref, wukt_ref, wuvt_ref, bias_ref,
                y_ref, qall_ref, qr_ref, ha_ref, hb_ref, sc_ref, mb_ref, sa_ref, sb_ref, pa_ref, pb_ref,
                ta_ref, tb_ref, m_ref, al_ref, acc_ref, w_ref, cand_ref, thr_ref, cut_ref, sh_ref):
    qi = pl.program_id(1)
    nkb = qi + 1
    last_kb = mb_ref.shape[0] - 1

    for h in range(H_A):
        ql = lax.dot_general(wukt_ref[h], qa_ref[:, h * DH_A:(h + 1) * DH_A], (((1,), (1,)), ((), ())),
                             preferred_element_type=F32)
        qall_ref[h // (H_A // 2), :, (h % (H_A // 2)) * QB:(h % (H_A // 2) + 1) * QB] = (
            ql * (DH_A ** -0.5 * LOG2E)).astype(BF16)


    @pl.when(qi * QB < TOPK)
    def _():
        krow = lax.broadcasted_iota(jnp.int32, (KB, QB), 0)
        qcol = lax.broadcasted_iota(jnp.int32, (KB, QB), 1)
        mb_ref[0] = jnp.where(krow <= qcol, 0.0, NEG).astype(F32)

    @pl.when(qi * QB >= TOPK)
    def _():
        for h in range(H_IDX):
            qr_ref[h * QB:(h + 1) * QB, :] = qi_ref[:, h * D_IDX:(h + 1) * D_IDX]
        w_ref[...] = gt_ref[S_WI:S_WI + H_IDX, :] * ((D_IDX ** -0.5) * (H_IDX ** -0.5))
        krow = lax.broadcasted_iota(jnp.int32, (KI, LANES), 0)
        qcol = lax.broadcasted_iota(jnp.int32, (KI, LANES), 1)
        n_ki = nkb * (KB // KI)
        last_ki = sc_ref.shape[0] // KI - 1

        def head_dots(ki, dst_ref):
            k = kidx_ref[pl.ds(pl.multiple_of(jnp.minimum(ki, last_ki) * KI, KI), KI), :]
            dst_ref[...] = lax.dot_general(k, qr_ref[...], (((1,), (1,)), ((), ())),
                                           preferred_element_type=F32)

        def reduce_heads(src_ref, ki):
            for g in range(QB // LANES):
                lanes = slice(g * LANES, (g + 1) * LANES)
                acc = jnp.zeros((KI, LANES), F32)
                for h in range(H_IDX):
                    acc = acc + (jnp.maximum(src_ref[:, h * QB + g * LANES:h * QB + (g + 1) * LANES], 0.0)
                                 * w_ref[h:h + 1, lanes])
                val = jnp.where(krow + (ki * KI - qi * QB - g * LANES) <= qcol, acc, -jnp.inf)
                sc_ref[pl.ds(pl.multiple_of(ki * KI, KI), KI), lanes] = val
                sh_ref[pl.ds(pl.multiple_of(ki * KI, KI), KI), lanes] = val.astype(BF16)

        pad_kb = jnp.minimum(nkb, last_kb)
        sc_ref[pl.ds(pl.multiple_of(pad_kb * KB, KB), KB), :] = jnp.full((KB, QB), -jnp.inf, F32)
        sh_ref[pl.ds(pl.multiple_of(pad_kb * KB, KB), KB), :] = jnp.full((KB, QB), -jnp.inf, BF16)

        head_dots(0, ha_ref)

        def sc_body(j, carry):
            head_dots(2 * j + 1, hb_ref)
            reduce_heads(ha_ref, 2 * j)
            head_dots(2 * j + 2, ha_ref)
            reduce_heads(hb_ref, 2 * j + 1)
            return carry
        lax.fori_loop(0, n_ki // 2, sc_body, 0)

        n_cnt = (nkb * KB + CNT_ROWS - 1) // CNT_ROWS

        def count_where(pred):
            def body(c, acc):
                parts = []
                for g in range(QB // LANES):
                    lanes = slice(g * LANES, (g + 1) * LANES)
                    blk = sc_ref[pl.ds(pl.multiple_of(c * CNT_ROWS, CNT_ROWS), CNT_ROWS), lanes]
                    hit = jnp.where(pred(blk, c * CNT_ROWS, lanes), 1.0, 0.0).astype(F32)
                    parts.append(jnp.sum(hit.reshape(CNT_ROWS // CNT_ACC, CNT_ACC, LANES), axis=0))
                return acc + jnp.concatenate(parts, axis=1)
            acc = lax.fori_loop(0, n_cnt, body, jnp.zeros((CNT_ACC, QB), F32))
            return jnp.sum(acc, axis=0, keepdims=True)

        def count_ge_bf16():
            def body(c, acc):
                parts = []
                for g in range(QB // LANES):
                    lanes = slice(g * LANES, (g + 1) * LANES)
                    blk = sh_ref[pl.ds(pl.multiple_of(c * CNT_ROWS, CNT_ROWS), CNT_ROWS), lanes]
                    hit = jnp.where(blk >= cand_ref[:, lanes].astype(BF16),
                                    jnp.ones((), BF16), jnp.zeros((), BF16))
                    rows = [hit[r * CNT_ACC:(r + 1) * CNT_ACC] for r in range(CNT_ROWS // CNT_ACC)]
                    while len(rows) > 1:
                        rows = [a + b for a, b in zip(rows[0::2], rows[1::2])]
                    parts.append(rows[0])
                return acc + jnp.concatenate(parts, axis=1)
            assert (sc_ref.shape[0] // CNT_ACC) < 2 ** 8
            acc = lax.fori_loop(0, n_cnt, body, jnp.zeros((CNT_ACC, QB), BF16))
            return jnp.sum(acc.astype(F32), axis=0, keepdims=True)

        def coarse_body(i, u):
            trial = u | lax.shift_left(jnp.int32(1), 31 - i)
            key = trial ^ jnp.int32(-2 ** 31)
            bits = jnp.where(key < 0, (key ^ jnp.int32(0x7FFFFFFF)) & jnp.int32(-2 ** 16), key)
            cand_ref[...] = lax.bitcast_convert_type(bits, F32)
            return jnp.where(count_ge_bf16() >= float(TOPK), trial, u)
        u = lax.fori_loop(0, 16, coarse_body, jnp.zeros((1, QB), jnp.int32))
        key = u ^ jnp.int32(-2 ** 31)
        bits = jnp.where(key < 0, (key ^ jnp.int32(0x7FFFFFFF)) & jnp.int32(-2 ** 16), key)
        k_mid = jnp.where(bits < 0, bits ^ jnp.int32(0x7FFFFFFF), bits)

        def fine_body(i, carry):
            lo, hi, c_ge = carry
            mid = lo + lax.shift_right_arithmetic(hi - lo, jnp.int32(1))
            cand_ref[...] = _key_to_float(mid)
            cnt = count_where(lambda blk, row0, lanes: blk >= cand_ref[:, lanes])
            ok = cnt >= float(TOPK)
            return jnp.where(ok, mid, lo), jnp.where(ok, hi, mid), jnp.where(ok, cnt, c_ge)
        span = 2 ** 16
        lo0 = jnp.maximum(k_mid - span, jnp.int32(KEY_NEG_INF))
        hi0 = jnp.minimum(k_mid + span, jnp.int32(KEY_POS_INF)) + 1
        lo, _, c_ge = lax.fori_loop(0, (2 * span + 1).bit_length(), fine_body,
                                    (lo0, hi0, jnp.full((1, QB), float(sc_ref.shape[0]), F32)))
        thr_ref[...] = _key_to_float(lo)
        has_ties = jnp.max(c_ge) > float(TOPK)

        @pl.when(jnp.logical_not(has_ties))
        def _():
            def mb_body(kb, carry):
                blk = sc_ref[pl.ds(pl.multiple_of(kb * KB, KB), KB), :]
                mb_ref[kb] = jnp.where(blk >= thr_ref[...], 0.0, NEG).astype(F32)
                return carry
            lax.fori_loop(0, nkb, mb_body, 0)

        @pl.when(has_ties)
        def _():
            c_gt = count_where(lambda blk, row0, lanes: blk > thr_ref[:, lanes])
            need = float(TOPK) - c_gt
            rows = lax.broadcasted_iota(jnp.int32, (CNT_ROWS, LANES), 0)
            n_bits = (sc_ref.shape[0] - 1).bit_length()

            def idx_body(i, cut):
                trial = cut | lax.shift_left(jnp.int32(1), n_bits - 1 - i)
                cut_ref[...] = trial
                before = count_where(lambda blk, row0, lanes: (blk == thr_ref[:, lanes])
                                     & (rows + row0 < cut_ref[:, lanes]))
                return jnp.where(before < need, trial, cut)
            cut_ref[...] = lax.fori_loop(0, n_bits, idx_body, jnp.zeros((1, QB), jnp.int32))

            def mb_body(kb, carry):
                for g in range(QB // LANES):
                    lanes = slice(g * LANES, (g + 1) * LANES)
                    blk = sc_ref[pl.ds(pl.multiple_of(kb * KB, KB), KB), lanes]
                    thr = thr_ref[:, lanes]
                    keep = (blk > thr) | ((blk == thr) & (rows[0:KB] + kb * KB <= cut_ref[:, lanes]))
                    mb_ref[kb, :, lanes] = jnp.where(keep, 0.0, NEG).astype(F32)
                return carry
            lax.fori_loop(0, nkb, mb_body, 0)

    m_ref[...] = jnp.full(m_ref.shape, NEG, F32)
    acc_ref[...] = jnp.zeros(acc_ref.shape, F32)

    hh = H_A // 2
    s_refs, p_refs, t_refs = (sa_ref, sb_ref), (pa_ref, pb_ref), (ta_ref, tb_ref)
    pb_ref[...] = jnp.zeros(pb_ref.shape, BF16)
    al_ref[...] = jnp.ones(al_ref.shape, F32)

    def logits(kb, half):
        kv = ckv_ref[pl.ds(pl.multiple_of(jnp.minimum(kb, last_kb) * KB, KB), KB), :]
        s_refs[half][...] = jnp.dot(kv, qall_ref[half], preferred_element_type=F32)

    def softmax(kb, half, with_bias):
        tile = jnp.clip(qi - kb, 0, 2)
        for j in range(hh):
            for g in range(QB // LANES):
                lanes = slice(g * LANES, (g + 1) * LANES)
                cols = slice(j * QB + g * LANES, j * QB + (g + 1) * LANES)
                x = s_refs[half][:, cols] + mb_ref[kb, :, lanes]
                if with_bias:
                    x = x + bias_ref[half * hh + j, tile, :, lanes]
                m_prev = m_ref[half, :, cols]
                m_blk = jnp.max(x.reshape(KB // CNT_ACC, CNT_ACC, LANES), axis=0)
                m_new = jnp.maximum(m_prev, jnp.max(m_blk, axis=0, keepdims=True))
                al_ref[half, :, cols] = jnp.exp2(m_prev - m_new)
                m_ref[half, :, cols] = m_new
                p_refs[half][:, cols] = jnp.exp2(x - m_new).astype(BF16)

    def accumulate(kb, half):
        t_refs[half][...] = jnp.dot(ckvt_ref[jnp.maximum(kb, 0)], p_refs[half][...],
                                    preferred_element_type=F32)
        acc_ref[half] = acc_ref[half] * al_ref[half] + t_refs[half][...]

    def sweep(first_kb, end_kb, with_bias):
        def body(kb, carry):
            logits(kb, 1)
            softmax(kb, 0, with_bias)
            accumulate(kb - 1, 1)
            logits(kb + 1, 0)
            softmax(kb, 1, with_bias)
            accumulate(kb, 0)
            return carry
        lax.fori_loop(first_kb, end_kb, body, 0)

    n_far = jnp.maximum(qi - 1, 0)
    logits(0, 0)
    sweep(0, n_far, False)
    sweep(n_far, nkb, True)
    accumulate(nkb - 1, 1)

    for h in range(H_A):
        half, cols = h // hh, slice((h % hh) * QB, (h % hh + 1) * QB)
        ya_t = jnp.dot(wuvt_ref[h], acc_ref[half, 0:D_C, cols].astype(BF16), preferred_element_type=F32)
        ya = (ya_t / acc_ref[half, D_C:D_C + 1, cols]).T
        z = za_ref[:, h * DH_A:(h + 1) * DH_A].astype(F32)
        y_ref[:, h * DH_A:(h + 1) * DH_A] = (ya * (z * jax.nn.sigmoid(z))).astype(BF16)


def _dsa(main, gate_t, ckv_n, ckv_t, kidx_n, w_uk_t, w_uv_t, bias, B, T):
    nq = T // QB
    return pl.pallas_call(
        _dsa_kernel,
        grid=(B, nq),
        in_specs=[pl.BlockSpec((QB, W_A), lambda b, q: (b * nq + q, C_QA // W_A)),
                  pl.BlockSpec((QB, W_A), lambda b, q: (b * nq + q, C_ZA // W_A)),
                  pl.BlockSpec((QB, H_IDX * D_IDX), lambda b, q: (b * nq + q, C_QI // (H_IDX * D_IDX))),
                  pl.BlockSpec((LANES, QB), lambda b, q: (0, b * nq + q)),
                  pl.BlockSpec((T, D_C), lambda b, q: (b, 0)),
                  pl.BlockSpec((T // KB, D_C + ONES_ROWS, KB), lambda b, q: (b, 0, 0)),
                  pl.BlockSpec((T, D_IDX), lambda b, q: (b, 0)),
                  pl.BlockSpec((H_A, D_C, DH_A), lambda b, q: (0, 0, 0)),
                  pl.BlockSpec((H_A, DH_A, D_C), lambda b, q: (0, 0, 0)),
                  pl.BlockSpec((H_A, 3, KB, QB), lambda b, q: (0, 0, 0, 0))],
        out_specs=pl.BlockSpec((QB, W_A), lambda b, q: (b * nq + q, 0)),
        out_shape=jax.ShapeDtypeStruct((B * T, W_A), BF16),
        scratch_shapes=[pltpu.VMEM((2, D_C, H_A // 2 * QB), BF16),
                        pltpu.VMEM((H_IDX * QB, D_IDX), BF16),
                        pltpu.VMEM((KI, H_IDX * QB), F32),
                        pltpu.VMEM((KI, H_IDX * QB), F32),
                        pltpu.VMEM((T, QB), F32),
                        pltpu.VMEM((T // KB, KB, QB), F32),
                        pltpu.VMEM((KB, H_A // 2 * QB), F32),
                        pltpu.VMEM((KB, H_A // 2 * QB), F32),
                        pltpu.VMEM((KB, H_A // 2 * QB), BF16),
                        pltpu.VMEM((KB, H_A // 2 * QB), BF16),
                        pltpu.VMEM((D_C + ONES_ROWS, H_A // 2 * QB), F32),
                        pltpu.VMEM((D_C + ONES_ROWS, H_A // 2 * QB), F32),
                        pltpu.VMEM((2, 1, H_A // 2 * QB), F32),
                        pltpu.VMEM((2, 1, H_A // 2 * QB), F32),
                        pltpu.VMEM((2, D_C + ONES_ROWS, H_A // 2 * QB), F32),
                        pltpu.VMEM((H_IDX, QB), F32),
                        pltpu.VMEM((1, QB), F32),
                        pltpu.VMEM((1, QB), F32),
                        pltpu.VMEM((1, QB), jnp.int32),
                        pltpu.VMEM((T, QB), BF16)],
        compiler_params=pltpu.CompilerParams(
            dimension_semantics=("arbitrary", "arbitrary"), vmem_limit_bytes=VMEM_LIMIT),
        name="dsa",
    )(main, main, main, gate_t, ckv_n, ckv_t, kidx_n, w_uk_t, w_uv_t, bias)


def _split_dot(tri, x):
    hi = x.astype(BF16)
    lo = (x - hi.astype(F32)).astype(BF16)
    return jnp.dot(tri, hi, preferred_element_type=F32) + jnp.dot(tri, lo, preferred_element_type=F32)


def _log_sigmoid(x):
    return jnp.minimum(x, 0.0) - jnp.log1p(jnp.exp(-jnp.abs(x)))


def _mlstm_kernel(q_ref, k_ref, qh_ref, kh_ref, v_ref, o_ref, z_ref, g_ref, gt_ref,
                  cw_ref, cb_ref, gbr_ref, gbc_ref, ng_ref, y_ref, ct_ref, m_ref):
    c = pl.program_id(1)
    L = L_M

    @pl.when(c == 0)
    def _():
        ct_ref[...] = jnp.zeros(ct_ref.shape, F32)
        m_ref[...] = jnp.zeros(m_ref.shape, F32)

    r = lax.broadcasted_iota(jnp.int32, (L, L), 0)
    s = lax.broadcasted_iota(jnp.int32, (L, L), 1)
    causal = s <= r
    shifts = [jnp.where(r - s == d, 1.0, 0.0).astype(BF16) for d in range(1, CONV_W)]

    def conv_silu(x_ref, halo_ref, lo):
        x = x_ref[...]
        halo = jnp.where(c > 0, halo_ref[...].astype(F32), 0.0)
        w = cw_ref[:, lo:lo + H_M * DK_M]
        y = cb_ref[:, lo:lo + H_M * DK_M] + w[CONV_W - 1:CONV_W] * x.astype(F32)
        top = jnp.zeros((SUBLANES, H_M * DK_M), F32)
        for d in range(1, CONV_W):
            wd = w[CONV_W - 1 - d:CONV_W - d]
            y = y + wd * jnp.dot(shifts[d - 1], x, preferred_element_type=F32)
            top = top + wd * jnp.concatenate(
                [halo[HALO - d:HALO], jnp.zeros((SUBLANES - d, H_M * DK_M), F32)], axis=0)
        y = jnp.concatenate([y[0:SUBLANES] + top, y[SUBLANES:]], axis=0)
        return y * jax.nn.sigmoid(y)

    q_all = conv_silu(q_ref, qh_ref, 0)
    k_all = conv_silu(k_ref, kh_ref, H_M * DK_M) * (DK_M ** -0.5)

    gc = g_ref[...] + gbr_ref[...]
    gr = gt_ref[S_IM:S_IM + 2 * H_M, :] + gbc_ref[S_IM:S_IM + 2 * H_M, :]
    tri_l = jnp.where(causal, 1.0, 0.0).astype(BF16)
    tri_u = jnp.where(r <= s, 1.0, 0.0).astype(BF16)
    b_cols = _split_dot(tri_l, _log_sigmoid(gc) * LOG2E)
    lf_rows = _log_sigmoid(gr) * LOG2E
    b_rows = jnp.dot(lf_rows.astype(BF16), tri_u, preferred_element_type=F32) \
        + jnp.dot((lf_rows - lf_rows.astype(BF16).astype(F32)).astype(BF16), tri_u,
                  preferred_element_type=F32)
    gc = gc * LOG2E
    gr = gr * LOG2E
    ones = jnp.ones((L, LANES), BF16)

    for h in range(H_M):
        q = q_all[:, h * DK_M:(h + 1) * DK_M]
        k = k_all[:, h * DK_M:(h + 1) * DK_M]
        v = jnp.concatenate([v_ref[:, h * DV_M:(h + 1) * DV_M], ones], axis=1)
        qb = q.astype(BF16)
        b_c = b_cols[:, S_FM + h:S_FM + h + 1]
        i_c = gc[:, S_IM + h:S_IM + h + 1]
        b_r = b_rows[H_M + h:H_M + h + 1, :]
        i_r = gr[h:h + 1, :]
        m_prev = m_ref[h]
        ct = ct_ref[h]

        log_d = jnp.where(causal, b_c - b_r + i_r, -jnp.inf)
        g = b_c + m_prev
        m_t = jnp.maximum(jnp.max(log_d, axis=-1, keepdims=True), g)
        qk = lax.dot_general(qb, k.astype(BF16), _NT, preferred_element_type=F32)
        s_mat = qk * jnp.exp2(log_d - m_t)
        inter = jnp.exp2(g - m_t)
        num = jnp.dot(s_mat.astype(BF16), v, preferred_element_type=F32) \
            + inter * jnp.dot(qb, ct.astype(BF16), preferred_element_type=F32)
        den = jnp.maximum(jnp.abs(num[:, DV_M:]), jnp.exp2(-m_t))
        hh = num[:, 0:DV_M] / jnp.concatenate([den] * (DV_M // LANES), axis=1)

        b_last = b_c[L - 1:L, :]
        a_r = b_last - b_r + i_r
        m_new = jnp.maximum(b_last + m_prev, jnp.max(a_r, axis=-1, keepdims=True))
        decay = jnp.exp2(b_last + m_prev - m_new)
        wgt_c = jnp.exp2(b_last - b_c + i_c - m_new)
        kw = k * wgt_c
        ct_ref[h] = decay * ct + jnp.dot(kw.T.astype(BF16), v, preferred_element_type=F32)
        m_ref[h] = m_new

        mu = jnp.mean(hh, axis=-1, keepdims=True)
        var = jnp.mean(jnp.square(hh - mu), axis=-1, keepdims=True)
        hn = (hh - mu) * lax.rsqrt(var + LN_EPS) * ng_ref[:, h * DV_M:(h + 1) * DV_M]
        og = o_ref[:, h * DV_M:(h + 1) * DV_M].astype(F32)
        zg = z_ref[:, h * DV_M:(h + 1) * DV_M].astype(F32)
        y_ref[:, h * DV_M:(h + 1) * DV_M] = (hn * jax.nn.sigmoid(og) * (zg * jax.nn.sigmoid(zg))).astype(BF16)


def _mlstm(main, small, gate_t, conv_w, conv_b, gb_row, gb_col, norm_g, B, T):
    nc = T // L_M
    hb = L_M // HALO
    qk_w = H_M * DK_M

    def halo_map(col):
        return lambda b, c: (jnp.maximum((b * nc + c) * hb - 1, 0), col)

    return pl.pallas_call(
        _mlstm_kernel,
        grid=(B, nc),
        in_specs=[pl.BlockSpec((L_M, qk_w), lambda b, c: (b * nc + c, C_QM // qk_w)),
                  pl.BlockSpec((L_M, qk_w), lambda b, c: (b * nc + c, C_KM // qk_w)),
                  pl.BlockSpec((HALO, qk_w), halo_map(C_QM // qk_w)),
                  pl.BlockSpec((HALO, qk_w), halo_map(C_KM // qk_w)),
                  pl.BlockSpec((L_M, W_M), lambda b, c: (b * nc + c, C_VM // W_M)),
                  pl.BlockSpec((L_M, W_M), lambda b, c: (b * nc + c, C_OM // W_M)),
                  pl.BlockSpec((L_M, W_M), lambda b, c: (b * nc + c, C_ZM // W_M)),
                  pl.BlockSpec((L_M, LANES), lambda b, c: (b * nc + c, D_C // LANES)),
                  pl.BlockSpec((LANES, L_M), lambda b, c: (0, b * nc + c)),
                  pl.BlockSpec((CONV_W, 2 * qk_w), lambda b, c: (0, 0)),
                  pl.BlockSpec((1, 2 * qk_w), lambda b, c: (0, 0)),
                  pl.BlockSpec((1, LANES), lambda b, c: (0, 0)),
                  pl.BlockSpec((LANES, 1), lambda b, c: (0, 0)),
                  pl.BlockSpec((1, W_M), lambda b, c: (0, 0))],
        out_specs=pl.BlockSpec((L_M, W_M), lambda b, c: (b * nc + c, 0)),
        out_shape=jax.ShapeDtypeStruct((B * T, W_M), BF16),
        scratch_shapes=[pltpu.VMEM((H_M, DK_M, DV_M + LANES), F32),
                        pltpu.VMEM((H_M, 1, 1), F32)],
        compiler_params=pltpu.CompilerParams(
            dimension_semantics=("arbitrary", "arbitrary"), vmem_limit_bytes=VMEM_LIMIT),
        name="mlstm",
    )(main, main, main, main, main, main, main, small, gate_t,
      conv_w, conv_b, gb_row, gb_col, norm_g)


def _out_kernel(ya_ref, ym_ref, x_ref, w_ref, g_ref, b_ref, o_ref):
    y = jnp.dot(ya_ref[...], w_ref[0:W_A, :], preferred_element_type=F32)
    y = y + jnp.dot(ym_ref[...], w_ref[W_A:W_A + W_M, :], preferred_element_type=F32)
    r = ALPHA * x_ref[...] + y
    mu = jnp.mean(r, axis=-1, keepdims=True)
    var = jnp.mean(jnp.square(r - mu), axis=-1, keepdims=True)
    o_ref[...] = (r - mu) * lax.rsqrt(var + LN_EPS) * g_ref[...] + b_ref[...]


def _out(ya, ym, x2d, w_out, ln_g, ln_b, tm=512):
    M = x2d.shape[0]
    return pl.pallas_call(
        _out_kernel,
        grid=(M // tm,),
        in_specs=[pl.BlockSpec((tm, W_A), lambda i: (i, 0)),
                  pl.BlockSpec((tm, W_M), lambda i: (i, 0)),
                  pl.BlockSpec((tm, D_MODEL), lambda i: (i, 0)),
                  pl.BlockSpec((W_A + W_M, D_MODEL), lambda i: (0, 0)),
                  pl.BlockSpec((1, D_MODEL), lambda i: (0, 0)),
                  pl.BlockSpec((1, D_MODEL), lambda i: (0, 0))],
        out_specs=pl.BlockSpec((tm, D_MODEL), lambda i: (i, 0)),
        out_shape=jax.ShapeDtypeStruct((M, D_MODEL), F32),
        compiler_params=pltpu.CompilerParams(
            dimension_semantics=("arbitrary",), vmem_limit_bytes=VMEM_LIMIT),
        name="out",
    )(ya, ym, x2d, w_out, ln_g, ln_b)


_W_IN_SEGS = (("q_a", W_A), ("c_kv", D_C), ("z_a", W_A), ("q_i", H_IDX * D_IDX), ("k_i", D_IDX),
              ("w_i", H_IDX), ("q_m", H_M * DK_M), ("k_m", H_M * DK_M), ("v_m", W_M), ("i_m", H_M),
              ("f_m", H_M), ("o_m", W_M), ("z_m", W_M))
_MAIN_ORDER = ("q_a", "z_a", "q_i", "q_m", "k_m", "v_m", "o_m", "z_m")
_SEG_NAMES = [name for name, _ in _W_IN_SEGS]
assert _SEG_NAMES.index("f_m") == _SEG_NAMES.index("i_m") + 1 and S_FM == S_IM + H_M


def _repack_kernel(wt_ref, main_ref, small_ref):
    src, off = {}, 0
    for name, width in _W_IN_SEGS:
        src[name] = (off, width)
        off += width
    dst = 0
    for name in _MAIN_ORDER:
        lo, width = src[name]
        main_ref[dst:dst + width, :] = wt_ref[lo:lo + width, :].astype(BF16)
        dst += width
    parts = [wt_ref[src[name][0]:src[name][0] + src[name][1], :] for name in ("c_kv", "k_i", "w_i")]
    lo = src["i_m"][0]
    parts.append(wt_ref[lo:lo + 2 * H_M, :])
    used = sum(p.shape[0] for p in parts)
    parts.append(jnp.zeros((N_SMALL - used, wt_ref.shape[1]), F32))
    small_ref[...] = jnp.concatenate(parts, axis=0).astype(BF16)


def _repack_w_in(w_in, tc=256):
    n_cols = sum(width for _, width in _W_IN_SEGS)
    wt = jnp.swapaxes(w_in, 1, 2)[0]
    return pl.pallas_call(
        _repack_kernel,
        grid=(D_MODEL // tc,),
        in_specs=[pl.BlockSpec((n_cols, tc), lambda i: (0, i))],
        out_specs=[pl.BlockSpec((N_MAIN, tc), lambda i: (0, i)),
                   pl.BlockSpec((N_SMALL, tc), lambda i: (0, i))],
        out_shape=[jax.ShapeDtypeStruct((N_MAIN, D_MODEL), BF16),
                   jax.ShapeDtypeStruct((N_SMALL, D_MODEL), BF16)],
        compiler_params=pltpu.CompilerParams(
            dimension_semantics=("arbitrary",), vmem_limit_bytes=VMEM_LIMIT),
        name="repack",
    )(wt)


def kernel(x, w_in, b_igate, b_fgate, kv_norm_g, w_uk, w_uv, idx_k_ln_g, idx_k_ln_b, rel_bias,
           conv_w, conv_b, mh_norm_g, w_out, ln_g, ln_b):
    B, T, D = x.shape
    assert D == D_MODEL and T % L_M == 0 and T % (2 * KB) == 0 and w_in.shape[0] == 1
    bias = _bias_tiles(rel_bias)
    x2d = x.reshape(B * T, D)
    w_main, w_small = _repack_w_in(w_in)
    main, small = _proj(x2d, w_main, w_small)
    ckv_n, ckv_t, kidx_n, gate_t = _prep(small, kv_norm_g[0][None], idx_k_ln_g[0][None], idx_k_ln_b[0][None])
    w_uk_t = jnp.transpose(w_uk[0], (0, 2, 1)).astype(BF16)
    w_uv_t = jnp.transpose(w_uv[0], (0, 2, 1)).astype(BF16)
    ya = _dsa(main, gate_t, ckv_n, ckv_t, kidx_n, w_uk_t, w_uv_t, bias, B, T)
    gb = jnp.zeros((LANES,), F32).at[S_IM:S_IM + H_M].set(b_igate[0]).at[S_FM:S_FM + H_M].set(b_fgate[0])
    ym = _mlstm(main, small, gate_t, conv_w[0], conv_b[0][None], gb[None, :], gb[:, None],
                mh_norm_g[0][None], B, T)
    out = _out(ya, ym, x2d, w_out[0].astype(BF16), ln_g[0][None], ln_b[0][None])
    return out.reshape(B, T, D)
```

```python
import functools
import math

import numpy as np
import jax
import jax.numpy as jnp
from jax import lax
from jax.experimental import pallas as pl
from jax.experimental.pallas import tpu as pltpu

F32 = jnp.float32
BF16 = jnp.bfloat16

D_MODEL = 2048
W_A = 1024
DH_A = 128
H_A = 8
D_C = 256
H_IDX = 16
D_IDX = 64
TOPK = 256
W_M = 1024
H_M = 4
DV_M = 256
DK_M = 128
CONV_W = 4
N_BUCKETS = 32
MAX_DIST = 128
ALPHA = 2.0 ** 0.25
LN_EPS = 1e-5

LANES = 128
SUBLANES = 8
VMEM_LIMIT = 56 * 1024 * 1024

QB = 256
KB = 256
KI = 128
CNT_ROWS = 512
CNT_ACC = 4 * SUBLANES
L_M = 256
HALO = 16
NEG = -1e30
LOG2E = math.log2(math.e)
ONES_ROWS = 16

C_QA, C_ZA, C_QI, C_QM, C_KM, C_VM, C_OM, C_ZM = 0, 1024, 2048, 3072, 3584, 4096, 5120, 6144
N_MAIN = 7168
N_SMALL = 384
S_KI, S_WI, S_IM, S_FM = 0, 64, 80, 84


def _t5_bucket_np(rel):
    max_exact = N_BUCKETS // 2
    n = np.maximum(rel, 0)
    nf = np.maximum(n, 1).astype(np.float32)
    large = max_exact + (np.log(nf / np.float32(max_exact)) / np.float32(math.log(MAX_DIST / max_exact))
                         * np.float32(N_BUCKETS - max_exact)).astype(np.int32)
    large = np.minimum(large, N_BUCKETS - 1)
    return np.where(n < max_exact, n, large).astype(np.int32)


FAR_BUCKET = int(_t5_bucket_np(np.array(2 * KB + 1)))


def _bucket_tiles():
    i = np.arange(QB)[None, :]
    j = np.arange(KB)[:, None]
    t0 = _t5_bucket_np(i - j)
    t1 = _t5_bucket_np(i - j + KB)
    assert (t5 := _t5_bucket_np(np.arange(KB + 1, 4096))).min() == t5.max() == FAR_BUCKET
    return np.stack([t0, t1]).astype(np.int32)


def _bias_kernel(bucket_ref, rb_ref, out_ref):
    h = pl.program_id(0)
    far = rb_ref[FAR_BUCKET, h]
    for k in range(2):
        bk = bucket_ref[k]
        acc = jnp.zeros((KB, QB), F32)
        for b in range(N_BUCKETS):
            acc = jnp.where(bk == b, rb_ref[b, h] - far, acc)
        out_ref[0, k] = acc * LOG2E
    out_ref[0, 2] = jnp.zeros((KB, QB), F32)


def _bias_tiles(rel_bias):
    bucket = jnp.asarray(_bucket_tiles())
    return pl.pallas_call(
        _bias_kernel,
        grid=(H_A,),
        in_specs=[pl.BlockSpec((2, KB, QB), lambda h: (0, 0, 0)),
                  pl.BlockSpec(memory_space=pltpu.SMEM)],
        out_specs=pl.BlockSpec((1, 3, KB, QB), lambda h: (h, 0, 0, 0)),
        out_shape=jax.ShapeDtypeStruct((H_A, 3, KB, QB), F32),
        name="bias_tiles",
    )(bucket, rel_bias)


_NT = (((1,), (1,)), ((), ()))


def _proj_kernel(x_ref, w_ref, ws_ref, o_ref, os_ref, xb_ref):
    @pl.when(pl.program_id(1) == 0)
    def _():
        xb_ref[...] = x_ref[...].astype(BF16)
        os_ref[...] = lax.dot_general(xb_ref[...], ws_ref[...], _NT, preferred_element_type=F32)

    o_ref[...] = lax.dot_general(xb_ref[...], w_ref[...], _NT, preferred_element_type=F32).astype(BF16)


def _proj(x2d, w_main, w_small, tm=1024, tn=1024):
    M = x2d.shape[0]
    return pl.pallas_call(
        _proj_kernel,
        grid=(M // tm, N_MAIN // tn),
        in_specs=[pl.BlockSpec((tm, D_MODEL), lambda i, j: (i, 0)),
                  pl.BlockSpec((tn, D_MODEL), lambda i, j: (j, 0)),
                  pl.BlockSpec((N_SMALL, D_MODEL), lambda i, j: (0, 0))],
        out_specs=[pl.BlockSpec((tm, tn), lambda i, j: (i, j)),
                   pl.BlockSpec((tm, N_SMALL), lambda i, j: (i, 0))],
        out_shape=[jax.ShapeDtypeStruct((M, N_MAIN), BF16),
                   jax.ShapeDtypeStruct((M, N_SMALL), F32)],
        scratch_shapes=[pltpu.VMEM((tm, D_MODEL), BF16)],
        compiler_params=pltpu.CompilerParams(
            dimension_semantics=("arbitrary", "arbitrary"), vmem_limit_bytes=VMEM_LIMIT),
        name="proj",
    )(x2d, w_main, w_small)


def _prep_kernel(s_ref, kvg_ref, ig_ref, ib_ref, ckv_ref, ckvt_ref, kidx_ref, gt_ref):
    c = s_ref[:, 0:D_C]
    c = c * lax.rsqrt(jnp.mean(c * c, axis=-1, keepdims=True) + LN_EPS) * kvg_ref[...]
    ckv_ref[...] = c.astype(BF16)
    for r in range(ckvt_ref.shape[0]):
        ckvt_ref[r, 0:D_C, :] = c[r * KB:(r + 1) * KB, :].T.astype(BF16)
        ckvt_ref[r, D_C:D_C + ONES_ROWS, :] = jnp.ones((ONES_ROWS, KB), BF16)
    tile = s_ref[:, D_C:D_C + LANES]
    k = tile[:, S_KI:S_KI + D_IDX]
    mu = jnp.mean(k, axis=-1, keepdims=True)
    var = jnp.mean(jnp.square(k - mu), axis=-1, keepdims=True)
    kidx_ref[...] = ((k - mu) * lax.rsqrt(var + LN_EPS) * ig_ref[...] + ib_ref[...]).astype(BF16)
    gt_ref[...] = tile.T


def _prep(small, kv_g, idx_g, idx_b, tm=1024):
    M = small.shape[0]
    return pl.pallas_call(
        _prep_kernel,
        grid=(M // tm,),
        in_specs=[pl.BlockSpec((tm, N_SMALL), lambda i: (i, 0)),
                  pl.BlockSpec((1, D_C), lambda i: (0, 0)),
                  pl.BlockSpec((1, D_IDX), lambda i: (0, 0)),
                  pl.BlockSpec((1, D_IDX), lambda i: (0, 0))],
        out_specs=[pl.BlockSpec((tm, D_C), lambda i: (i, 0)),
                   pl.BlockSpec((tm // KB, D_C + ONES_ROWS, KB), lambda i: (i, 0, 0)),
                   pl.BlockSpec((tm, D_IDX), lambda i: (i, 0)),
                   pl.BlockSpec((LANES, tm), lambda i: (0, i))],
        out_shape=[jax.ShapeDtypeStruct((M, D_C), BF16),
                   jax.ShapeDtypeStruct((M // KB, D_C + ONES_ROWS, KB), BF16),
                   jax.ShapeDtypeStruct((M, D_IDX), BF16),
                   jax.ShapeDtypeStruct((LANES, M), F32)],
        name="prep",
    )(small, kv_g, idx_g, idx_b)


def _key_to_float(key):
    bits = jnp.where(key < 0, key ^ jnp.int32(0x7FFFFFFF), key)
    return lax.bitcast_convert_type(bits, F32)


def _dsa_kernel(qa_ref, za_ref, qi_ref, gt_ref, ckv_ref, ckvt_ref, kidx_ref, wukt_ref, wuvt_ref, bias_ref,
                y_ref, qall_ref, qr_ref, ha_ref, hb_ref, sc_ref, mb_ref, sa_ref, sb_ref, pa_ref, pb_ref,
                ta_ref, tb_ref, m_ref, al_ref, acc_ref, w_ref, cand_ref, thr_ref, cut_ref, cge_ref):
    qi = pl.program_id(1)
    nkb = qi + 1
    last_kb = mb_ref.shape[0] - 1

    for h in range(H_A):
        ql = lax.dot_general(wukt_ref[h], qa_ref[:, h * DH_A:(h + 1) * DH_A], (((1,), (1,)), ((), ())),
                             preferred_element_type=F32)
        qall_ref[h // (H_A // 2), :, (h % (H_A // 2)) * QB:(h % (H_A // 2) + 1) * QB] = (
            ql * (DH_A ** -0.5 * LOG2E)).astype(BF16)


    @pl.when(qi * QB < TOPK)
    def _():
        krow = lax.broadcasted_iota(jnp.int32, (KB, QB), 0)
        qcol = lax.broadcasted_iota(jnp.int32, (KB, QB), 1)
        mb_ref[0] = jnp.where(krow <= qcol, 0.0, NEG).astype(F32)

    @pl.when(qi * QB >= TOPK)
    def _():
        for h in range(H_IDX):
            qr_ref[h * QB:(h + 1) * QB, :] = qi_ref[:, h * D_IDX:(h + 1) * D_IDX]
        w_ref[...] = gt_ref[S_WI:S_WI + H_IDX, :] * ((D_IDX ** -0.5) * (H_IDX ** -0.5))
        krow = lax.broadcasted_iota(jnp.int32, (KI, LANES), 0)
        qcol = lax.broadcasted_iota(jnp.int32, (KI, LANES), 1)
        n_ki = nkb * (KB // KI)
        last_ki = sc_ref.shape[0] // KI - 1

        def head_dots(ki, dst_ref):
            k = kidx_ref[pl.ds(pl.multiple_of(jnp.minimum(ki, last_ki) * KI, KI), KI), :]
            dst_ref[...] = lax.dot_general(k, qr_ref[...], (((1,), (1,)), ((), ())),
                                           preferred_element_type=F32)

        def reduce_heads(src_ref, ki):
            for g in range(QB // LANES):
                lanes = slice(g * LANES, (g + 1) * LANES)
                acc = jnp.zeros((KI, LANES), F32)
                for h in range(H_IDX):
                    acc = acc + (jnp.maximum(src_ref[:, h * QB + g * LANES:h * QB + (g + 1) * LANES], 0.0)
                                 * w_ref[h:h + 1, lanes])
                sc_ref[pl.ds(pl.multiple_of(ki * KI, KI), KI), lanes] = jnp.where(
                    krow + (ki * KI - qi * QB - g * LANES) <= qcol, acc, -jnp.inf)

        pad_kb = jnp.minimum(nkb, last_kb)
        sc_ref[pl.ds(pl.multiple_of(pad_kb * KB, KB), KB), :] = jnp.full((KB, QB), -jnp.inf, F32)

        head_dots(0, ha_ref)

        def sc_body(j, carry):
            head_dots(2 * j + 1, hb_ref)
            reduce_heads(ha_ref, 2 * j)
            head_dots(2 * j + 2, ha_ref)
            reduce_heads(hb_ref, 2 * j + 1)
            return carry
        lax.fori_loop(0, n_ki // 2, sc_body, 0)

        n_cnt = (nkb * KB + CNT_ROWS - 1) // CNT_ROWS

        def count_where(pred, steps=None):
            def body(c, acc):
                parts = []
                for g in range(QB // LANES):
                    lanes = slice(g * LANES, (g + 1) * LANES)
                    if steps is None:
                        blk = sc_ref[pl.ds(pl.multiple_of(c * CNT_ROWS, CNT_ROWS), CNT_ROWS), lanes]
                    else:
                        blk = sc_ref[c * CNT_ROWS:(c + 1) * CNT_ROWS, lanes]
                    hit = jnp.where(pred(blk, c * CNT_ROWS, lanes), 1.0, 0.0).astype(F32)
                    parts.append(jnp.sum(hit.reshape(CNT_ROWS // CNT_ACC, CNT_ACC, LANES), axis=0))
                return acc + jnp.concatenate(parts, axis=1)
            acc = jnp.zeros((CNT_ACC, QB), F32)
            if steps is None:
                acc = lax.fori_loop(0, n_cnt, body, acc)
            else:
                for c in range(steps):
                    acc = body(c, acc)
            return jnp.sum(acc, axis=0, keepdims=True)

        def search(steps):
            def bit_body(i, carry):
                u, c_ge = carry
                trial = u | lax.shift_left(jnp.int32(1), 31 - i)
                cand_ref[...] = _key_to_float(trial ^ jnp.int32(-2 ** 31))
                cnt = count_where(lambda blk, row0, lanes: blk >= cand_ref[:, lanes], steps)
                ok = cnt >= float(TOPK)
                return jnp.where(ok, trial, u), jnp.where(ok, cnt, c_ge)
            u, c_ge = lax.fori_loop(0, 32, bit_body, (jnp.zeros((1, QB), jnp.int32),
                                                      jnp.full((1, QB), float(sc_ref.shape[0]), F32)))
            thr_ref[...] = _key_to_float(u ^ jnp.int32(-2 ** 31))
            cge_ref[...] = c_ge

        for steps in range(1, sc_ref.shape[0] // CNT_ROWS + 1):
            pl.when(n_cnt == steps)(functools.partial(search, steps))
        has_ties = jnp.max(cge_ref[...]) > float(TOPK)

        @pl.when(jnp.logical_not(has_ties))
        def _():
            def mb_body(kb, carry):
                blk = sc_ref[pl.ds(pl.multiple_of(kb * KB, KB), KB), :]
                mb_ref[kb] = jnp.where(blk >= thr_ref[...], 0.0, NEG).astype(F32)
                return carry
            lax.fori_loop(0, nkb, mb_body, 0)

        @pl.when(has_ties)
        def _():
            c_gt = count_where(lambda blk, row0, lanes: blk > thr_ref[:, lanes])
            need = float(TOPK) - c_gt
            rows = lax.broadcasted_iota(jnp.int32, (CNT_ROWS, LANES), 0)
            n_bits = (sc_ref.shape[0] - 1).bit_length()

            def idx_body(i, cut):
                trial = cut | lax.shift_left(jnp.int32(1), n_bits - 1 - i)
                cut_ref[...] = trial
                before = count_where(lambda blk, row0, lanes: (blk == thr_ref[:, lanes])
                                     & (rows + row0 < cut_ref[:, lanes]))
                return jnp.where(before < need, trial, cut)
            cut_ref[...] = lax.fori_loop(0, n_bits, idx_body, jnp.zeros((1, QB), jnp.int32))

            def mb_body(kb, carry):
                for g in range(QB // LANES):
                    lanes = slice(g * LANES, (g + 1) * LANES)
                    blk = sc_ref[pl.ds(pl.multiple_of(kb * KB, KB), KB), lanes]
                    thr = thr_ref[:, lanes]
                    keep = (blk > thr) | ((blk == thr) & (rows[0:KB] + kb * KB <= cut_ref[:, lanes]))
                    mb_ref[kb, :, lanes] = jnp.where(keep, 0.0, NEG).astype(F32)
                return carry
            lax.fori_loop(0, nkb, mb_body, 0)

    m_ref[...] = jnp.full(m_ref.shape, NEG, F32)
    acc_ref[...] = jnp.zeros(acc_ref.shape, F32)

    hh = H_A // 2
    s_refs, p_refs, t_refs = (sa_ref, sb_ref), (pa_ref, pb_ref), (ta_ref, tb_ref)
    pb_ref[...] = jnp.zeros(pb_ref.shape, BF16)
    al_ref[...] = jnp.ones(al_ref.shape, F32)

    def logits(kb, half):
        kv = ckv_ref[pl.ds(pl.multiple_of(jnp.minimum(kb, last_kb) * KB, KB), KB), :]
        s_refs[half][...] = jnp.dot(kv, qall_ref[half], preferred_element_type=F32)

    def softmax(kb, half, with_bias):
        tile = jnp.clip(qi - kb, 0, 2)
        for j in range(hh):
            for g in range(QB // LANES):
                lanes = slice(g * LANES, (g + 1) * LANES)
                cols = slice(j * QB + g * LANES, j * QB + (g + 1) * LANES)
                x = s_refs[half][:, cols] + mb_ref[kb, :, lanes]
                if with_bias:
                    x = x + bias_ref[half * hh + j, tile, :, lanes]
                m_prev = m_ref[half, :, cols]
                m_blk = jnp.max(x.reshape(KB // CNT_ACC, CNT_ACC, LANES), axis=0)
                m_new = jnp.maximum(m_prev, jnp.max(m_blk, axis=0, keepdims=True))
                al_ref[half, :, cols] = jnp.exp2(m_prev - m_new)
                m_ref[half, :, cols] = m_new
                p_refs[half][:, cols] = jnp.exp2(x - m_new).astype(BF16)

    def accumulate(kb, half):
        t_refs[half][...] = jnp.dot(ckvt_ref[jnp.maximum(kb, 0)], p_refs[half][...],
                                    preferred_element_type=F32)
        acc_ref[half] = acc_ref[half] * al_ref[half] + t_refs[half][...]

    def sweep(first_kb, end_kb, with_bias):
        def body(kb, carry):
            logits(kb, 1)
            softmax(kb, 0, with_bias)
            accumulate(kb - 1, 1)
            logits(kb + 1, 0)
            softmax(kb, 1, with_bias)
            accumulate(kb, 0)
            return carry
        lax.fori_loop(first_kb, end_kb, body, 0)

    n_far = jnp.maximum(qi - 1, 0)
    logits(0, 0)
    sweep(0, n_far, False)
    sweep(n_far, nkb, True)
    accumulate(nkb - 1, 1)

    for h in range(H_A):
        half, cols = h // hh, slice((h % hh) * QB, (h % hh + 1) * QB)
        ya_t = jnp.dot(wuvt_ref[h], acc_ref[half, 0:D_C, cols].astype(BF16), preferred_element_type=F32)
        ya = (ya_t / acc_ref[half, D_C:D_C + 1, cols]).T
        z = za_ref[:, h * DH_A:(h + 1) * DH_A].astype(F32)
        y_ref[:, h * DH_A:(h + 1) * DH_A] = (ya * (z * jax.nn.sigmoid(z))).astype(BF16)


def _dsa(main, gate_t, ckv_n, ckv_t, kidx_n, w_uk_t, w_uv_t, bias, B, T):
    nq = T // QB
    return pl.pallas_call(
        _dsa_kernel,
        grid=(B, nq),
        in_specs=[pl.BlockSpec((QB, W_A), lambda b, q: (b * nq + q, C_QA // W_A)),
                  pl.BlockSpec((QB, W_A), lambda b, q: (b * nq + q, C_ZA // W_A)),
                  pl.BlockSpec((QB, H_IDX * D_IDX), lambda b, q: (b * nq + q, C_QI // (H_IDX * D_IDX))),
                  pl.BlockSpec((LANES, QB), lambda b, q: (0, b * nq + q)),
                  pl.BlockSpec((T, D_C), lambda b, q: (b, 0)),
                  pl.BlockSpec((T // KB, D_C + ONES_ROWS, KB), lambda b, q: (b, 0, 0)),
                  pl.BlockSpec((T, D_IDX), lambda b, q: (b, 0)),
                  pl.BlockSpec((H_A, D_C, DH_A), lambda b, q: (0, 0, 0)),
                  pl.BlockSpec((H_A, DH_A, D_C), lambda b, q: (0, 0, 0)),
                  pl.BlockSpec((H_A, 3, KB, QB), lambda b, q: (0, 0, 0, 0))],
        out_specs=pl.BlockSpec((QB, W_A), lambda b, q: (b * nq + q, 0)),
        out_shape=jax.ShapeDtypeStruct((B * T, W_A), BF16),
        scratch_shapes=[pltpu.VMEM((2, D_C, H_A // 2 * QB), BF16),
                        pltpu.VMEM((H_IDX * QB, D_IDX), BF16),
                        pltpu.VMEM((KI, H_IDX * QB), F32),
                        pltpu.VMEM((KI, H_IDX * QB), F32),
                        pltpu.VMEM((T, QB), F32),
                        pltpu.VMEM((T // KB, KB, QB), F32),
                        pltpu.VMEM((KB, H_A // 2 * QB), F32),
                        pltpu.VMEM((KB, H_A // 2 * QB), F32),
                        pltpu.VMEM((KB, H_A // 2 * QB), BF16),
                        pltpu.VMEM((KB, H_A // 2 * QB), BF16),
                        pltpu.VMEM((D_C + ONES_ROWS, H_A // 2 * QB), F32),
                        pltpu.VMEM((D_C + ONES_ROWS, H_A // 2 * QB), F32),
                        pltpu.VMEM((2, 1, H_A // 2 * QB), F32),
                        pltpu.VMEM((2, 1, H_A // 2 * QB), F32),
                        pltpu.VMEM((2, D_C + ONES_ROWS, H_A // 2 * QB), F32),
                        pltpu.VMEM((H_IDX, QB), F32),
                        pltpu.VMEM((1, QB), F32),
                        pltpu.VMEM((1, QB), F32),
                        pltpu.VMEM((1, QB), jnp.int32),
                        pltpu.VMEM((1, QB), F32)],
        compiler_params=pltpu.CompilerParams(
            dimension_semantics=("arbitrary", "arbitrary"), vmem_limit_bytes=VMEM_LIMIT),
        name="dsa",
    )(main, main, main, gate_t, ckv_n, ckv_t, kidx_n, w_uk_t, w_uv_t, bias)


def _split_dot(tri, x):
    hi = x.astype(BF16)
    lo = (x - hi.astype(F32)).astype(BF16)
    return jnp.dot(tri, hi, preferred_element_type=F32) + jnp.dot(tri, lo, preferred_element_type=F32)


def _log_sigmoid(x):
    return jnp.minimum(x, 0.0) - jnp.log1p(jnp.exp(-jnp.abs(x)))


def _mlstm_kernel(q_ref, k_ref, qh_ref, kh_ref, v_ref, o_ref, z_ref, g_ref, gt_ref,
                  cw_ref, cb_ref, gbr_ref, gbc_ref, ng_ref, y_ref, ct_ref, m_ref):
    c = pl.program_id(1)
    L = L_M

    @pl.when(c == 0)
    def _():
        ct_ref[...] = jnp.zeros(ct_ref.shape, F32)
        m_ref[...] = jnp.zeros(m_ref.shape, F32)

    r = lax.broadcasted_iota(jnp.int32, (L, L), 0)
    s = lax.broadcasted_iota(jnp.int32, (L, L), 1)
    causal = s <= r
    shifts = [jnp.where(r - s == d, 1.0, 0.0).astype(BF16) for d in range(1, CONV_W)]

    def conv_silu(x_ref, halo_ref, lo):
        x = x_ref[...]
        halo = jnp.where(c > 0, halo_ref[...].astype(F32), 0.0)
        w = cw_ref[:, lo:lo + H_M * DK_M]
        y = cb_ref[:, lo:lo + H_M * DK_M] + w[CONV_W - 1:CONV_W] * x.astype(F32)
        top = jnp.zeros((SUBLANES, H_M * DK_M), F32)
        for d in range(1, CONV_W):
            wd = w[CONV_W - 1 - d:CONV_W - d]
            y = y + wd * jnp.dot(shifts[d - 1], x, preferred_element_type=F32)
            top = top + wd * jnp.concatenate(
                [halo[HALO - d:HALO], jnp.zeros((SUBLANES - d, H_M * DK_M), F32)], axis=0)
        y = jnp.concatenate([y[0:SUBLANES] + top, y[SUBLANES:]], axis=0)
        return y * jax.nn.sigmoid(y)

    q_all = conv_silu(q_ref, qh_ref, 0)
    k_all = conv_silu(k_ref, kh_ref, H_M * DK_M) * (DK_M ** -0.5)

    gc = g_ref[...] + gbr_ref[...]
    gr = gt_ref[S_IM:S_IM + 2 * H_M, :] + gbc_ref[S_IM:S_IM + 2 * H_M, :]
    tri_l = jnp.where(causal, 1.0, 0.0).astype(BF16)
    tri_u = jnp.where(r <= s, 1.0, 0.0).astype(BF16)
    b_cols = _split_dot(tri_l, _log_sigmoid(gc) * LOG2E)
    lf_rows = _log_sigmoid(gr) * LOG2E
    b_rows = jnp.dot(lf_rows.astype(BF16), tri_u, preferred_element_type=F32) \
        + jnp.dot((lf_rows - lf_rows.astype(BF16).astype(F32)).astype(BF16), tri_u,
                  preferred_element_type=F32)
    gc = gc * LOG2E
    gr = gr * LOG2E
    ones = jnp.ones((L, LANES), BF16)

    for h in range(H_M):
        q = q_all[:, h * DK_M:(h + 1) * DK_M]
        k = k_all[:, h * DK_M:(h + 1) * DK_M]
        v = jnp.concatenate([v_ref[:, h * DV_M:(h + 1) * DV_M], ones], axis=1)
        qb = q.astype(BF16)
        b_c = b_cols[:, S_FM + h:S_FM + h + 1]
        i_c = gc[:, S_IM + h:S_IM + h + 1]
        b_r = b_rows[H_M + h:H_M + h + 1, :]
        i_r = gr[h:h + 1, :]
        m_prev = m_ref[h]
        ct = ct_ref[h]

        log_d = jnp.where(causal, b_c - b_r + i_r, -jnp.inf)
        g = b_c + m_prev
        m_t = jnp.maximum(jnp.max(log_d, axis=-1, keepdims=True), g)
        qk = lax.dot_general(qb, k.astype(BF16), _NT, preferred_element_type=F32)
        s_mat = qk * jnp.exp2(log_d - m_t)
        inter = jnp.exp2(g - m_t)
        num = jnp.dot(s_mat.astype(BF16), v, preferred_element_type=F32) \
            + inter * jnp.dot(qb, ct.astype(BF16), preferred_element_type=F32)
        den = jnp.maximum(jnp.abs(num[:, DV_M:]), jnp.exp2(-m_t))
        hh = num[:, 0:DV_M] / jnp.concatenate([den] * (DV_M // LANES), axis=1)

        b_last = b_c[L - 1:L, :]
        a_r = b_last - b_r + i_r
        m_new = jnp.maximum(b_last + m_prev, jnp.max(a_r, axis=-1, keepdims=True))
        decay = jnp.exp2(b_last + m_prev - m_new)
        wgt_c = jnp.exp2(b_last - b_c + i_c - m_new)
        kw = k * wgt_c
        ct_ref[h] = decay * ct + jnp.dot(kw.T.astype(BF16), v, preferred_element_type=F32)
        m_ref[h] = m_new

        mu = jnp.mean(hh, axis=-1, keepdims=True)
        var = jnp.mean(jnp.square(hh - mu), axis=-1, keepdims=True)
        hn = (hh - mu) * lax.rsqrt(var + LN_EPS) * ng_ref[:, h * DV_M:(h + 1) * DV_M]
        og = o_ref[:, h * DV_M:(h + 1) * DV_M].astype(F32)
        zg = z_ref[:, h * DV_M:(h + 1) * DV_M].astype(F32)
        y_ref[:, h * DV_M:(h + 1) * DV_M] = (hn * jax.nn.sigmoid(og) * (zg * jax.nn.sigmoid(zg))).astype(BF16)


def _mlstm(main, small, gate_t, conv_w, conv_b, gb_row, gb_col, norm_g, B, T):
    nc = T // L_M
    hb = L_M // HALO
    qk_w = H_M * DK_M

    def halo_map(col):
        return lambda b, c: (jnp.maximum((b * nc + c) * hb - 1, 0), col)

    return pl.pallas_call(
        _mlstm_kernel,
        grid=(B, nc),
        in_specs=[pl.BlockSpec((L_M, qk_w), lambda b, c: (b * nc + c, C_QM // qk_w)),
                  pl.BlockSpec((L_M, qk_w), lambda b, c: (b * nc + c, C_KM // qk_w)),
                  pl.BlockSpec((HALO, qk_w), halo_map(C_QM // qk_w)),
                  pl.BlockSpec((HALO, qk_w), halo_map(C_KM // qk_w)),
                  pl.BlockSpec((L_M, W_M), lambda b, c: (b * nc + c, C_VM // W_M)),
                  pl.BlockSpec((L_M, W_M), lambda b, c: (b * nc + c, C_OM // W_M)),
                  pl.BlockSpec((L_M, W_M), lambda b, c: (b * nc + c, C_ZM // W_M)),
                  pl.BlockSpec((L_M, LANES), lambda b, c: (b * nc + c, D_C // LANES)),
                  pl.BlockSpec((LANES, L_M), lambda b, c: (0, b * nc + c)),
                  pl.BlockSpec((CONV_W, 2 * qk_w), lambda b, c: (0, 0)),
                  pl.BlockSpec((1, 2 * qk_w), lambda b, c: (0, 0)),
                  pl.BlockSpec((1, LANES), lambda b, c: (0, 0)),
                  pl.BlockSpec((LANES, 1), lambda b, c: (0, 0)),
                  pl.BlockSpec((1, W_M), lambda b, c: (0, 0))],
        out_specs=pl.BlockSpec((L_M, W_M), lambda b, c: (b * nc + c, 0)),
        out_shape=jax.ShapeDtypeStruct((B * T, W_M), BF16),
        scratch_shapes=[pltpu.VMEM((H_M, DK_M, DV_M + LANES), F32),
                        pltpu.VMEM((H_M, 1, 1), F32)],
        compiler_params=pltpu.CompilerParams(
            dimension_semantics=("arbitrary", "arbitrary"), vmem_limit_bytes=VMEM_LIMIT),
        name="mlstm",
    )(main, main, main, main, main, main, main, small, gate_t,
      conv_w, conv_b, gb_row, gb_col, norm_g)


def _out_kernel(ya_ref, ym_ref, x_ref, w_ref, g_ref, b_ref, o_ref):
    y = jnp.dot(ya_ref[...], w_ref[0:W_A, :], preferred_element_type=F32)
    y = y + jnp.dot(ym_ref[...], w_ref[W_A:W_A + W_M, :], preferred_element_type=F32)
    r = ALPHA * x_ref[...] + y
    mu = jnp.mean(r, axis=-1, keepdims=True)
    var = jnp.mean(jnp.square(r - mu), axis=-1, keepdims=True)
    o_ref[...] = (r - mu) * lax.rsqrt(var + LN_EPS) * g_ref[...] + b_ref[...]


def _out(ya, ym, x2d, w_out, ln_g, ln_b, tm=512):
    M = x2d.shape[0]
    return pl.pallas_call(
        _out_kernel,
        grid=(M // tm,),
        in_specs=[pl.BlockSpec((tm, W_A), lambda i: (i, 0)),
                  pl.BlockSpec((tm, W_M), lambda i: (i, 0)),
                  pl.BlockSpec((tm, D_MODEL), lambda i: (i, 0)),
                  pl.BlockSpec((W_A + W_M, D_MODEL), lambda i: (0, 0)),
                  pl.BlockSpec((1, D_MODEL), lambda i: (0, 0)),
                  pl.BlockSpec((1, D_MODEL), lambda i: (0, 0))],
        out_specs=pl.BlockSpec((tm, D_MODEL), lambda i: (i, 0)),
        out_shape=jax.ShapeDtypeStruct((M, D_MODEL), F32),
        compiler_params=pltpu.CompilerParams(
            dimension_semantics=("arbitrary",), vmem_limit_bytes=VMEM_LIMIT),
        name="out",
    )(ya, ym, x2d, w_out, ln_g, ln_b)


_W_IN_SEGS = (("q_a", W_A), ("c_kv", D_C), ("z_a", W_A), ("q_i", H_IDX * D_IDX), ("k_i", D_IDX),
              ("w_i", H_IDX), ("q_m", H_M * DK_M), ("k_m", H_M * DK_M), ("v_m", W_M), ("i_m", H_M),
              ("f_m", H_M), ("o_m", W_M), ("z_m", W_M))
_MAIN_ORDER = ("q_a", "z_a", "q_i", "q_m", "k_m", "v_m", "o_m", "z_m")
_SEG_NAMES = [name for name, _ in _W_IN_SEGS]
assert _SEG_NAMES.index("f_m") == _SEG_NAMES.index("i_m") + 1 and S_FM == S_IM + H_M


def _repack_kernel(wt_ref, main_ref, small_ref):
    src, off = {}, 0
    for name, width in _W_IN_SEGS:
        src[name] = (off, width)
        off += width
    dst = 0
    for name in _MAIN_ORDER:
        lo, width = src[name]
        main_ref[dst:dst + width, :] = wt_ref[lo:lo + width, :].astype(BF16)
        dst += width
    parts = [wt_ref[src[name][0]:src[name][0] + src[name][1], :] for name in ("c_kv", "k_i", "w_i")]
    lo = src["i_m"][0]
    parts.append(wt_ref[lo:lo + 2 * H_M, :])
    used = sum(p.shape[0] for p in parts)
    parts.append(jnp.zeros((N_SMALL - used, wt_ref.shape[1]), F32))
    small_ref[...] = jnp.concatenate(parts, axis=0).astype(BF16)


def _repack_w_in(w_in, tc=256):
    n_cols = sum(width for _, width in _W_IN_SEGS)
    wt = jnp.swapaxes(w_in, 1, 2)[0]
    return pl.pallas_call(
        _repack_kernel,
        grid=(D_MODEL // tc,),
        in_specs=[pl.BlockSpec((n_cols, tc), lambda i: (0, i))],
        out_specs=[pl.BlockSpec((N_MAIN, tc), lambda i: (0, i)),
                   pl.BlockSpec((N_SMALL, tc), lambda i: (0, i))],
        out_shape=[jax.ShapeDtypeStruct((N_MAIN, D_MODEL), BF16),
                   jax.ShapeDtypeStruct((N_SMALL, D_MODEL), BF16)],
        compiler_params=pltpu.CompilerParams(
            dimension_semantics=("arbitrary",), vmem_limit_bytes=VMEM_LIMIT),
        name="repack",
    )(wt)


def kernel(x, w_in, b_igate, b_fgate, kv_norm_g, w_uk, w_uv, idx_k_ln_g, idx_k_ln_b, rel_bias,
           conv_w, conv_b, mh_norm_g, w_out, ln_g, ln_b):
    B, T, D = x.shape
    assert D == D_MODEL and T % L_M == 0 and T % (2 * KB) == 0 and w_in.shape[0] == 1
    bias = _bias_tiles(rel_bias)
    x2d = x.reshape(B * T, D)
    w_main, w_small = _repack_w_in(w_in)
    main, small = _proj(x2d, w_main, w_small)
    ckv_n, ckv_t, kidx_n, gate_t = _prep(small, kv_norm_g[0][None], idx_k_ln_g[0][None], idx_k_ln_b[0][None])
    w_uk_t = jnp.transpose(w_uk[0], (0, 2, 1)).astype(BF16)
    w_uv_t = jnp.transpose(w_uv[0], (0, 2, 1)).astype(BF16)
    ya = _dsa(main, gate_t, ckv_n, ckv_t, kidx_n, w_uk_t, w_uv_t, bias, B, T)
    gb = jnp.zeros((LANES,), F32).at[S_IM:S_IM + H_M].set(b_igate[0]).at[S_FM:S_FM + H_M].set(b_fgate[0])
    ym = _mlstm(main, small, gate_t, conv_w[0], conv_b[0][None], gb[None, :], gb[:, None],
                mh_norm_g[0][None], B, T)
    out = _out(ya, ym, x2d, w_out[0].astype(BF16), ln_g[0][None], ln_b[0][None])
    return out.reshape(B, T, D)
```

```python
import functools
import math

import numpy as np
import jax
import jax.numpy as jnp
from jax import lax
from jax.experimental import pallas as pl
from jax.experimental.pallas import tpu as pltpu

F32 = jnp.float32
BF16 = jnp.bfloat16

D_MODEL = 2048
W_A = 1024
DH_A = 128
H_A = 8
D_C = 256
H_IDX = 16
D_IDX = 64
TOPK = 256
W_M = 1024
H_M = 4
DV_M = 256
DK_M = 128
CONV_W = 4
N_BUCKETS = 32
MAX_DIST = 128
ALPHA = 2.0 ** 0.25
LN_EPS = 1e-5

LANES = 128
SUBLANES = 8
VMEM_LIMIT = 56 * 1024 * 1024

QB = 256
KB = 256
KI = 128
CNT_ROWS = 512
CNT_ACC = 4 * SUBLANES
L_M = 256
HALO = 16
NEG = -1e30
LOG2E = math.log2(math.e)
ONES_ROWS = 16

C_QA, C_ZA, C_QI, C_QM, C_KM, C_VM, C_OM, C_ZM = 0, 1024, 2048, 3072, 3584, 4096, 5120, 6144
N_MAIN = 7168
N_SMALL = 384
S_KI, S_WI, S_IM, S_FM = 0, 64, 80, 84


def _t5_bucket_np(rel):
    max_exact = N_BUCKETS // 2
    n = np.maximum(rel, 0)
    nf = np.maximum(n, 1).astype(np.float32)
    large = max_exact + (np.log(nf / np.float32(max_exact)) / np.float32(math.log(MAX_DIST / max_exact))
                         * np.float32(N_BUCKETS - max_exact)).astype(np.int32)
    large = np.minimum(large, N_BUCKETS - 1)
    return np.where(n < max_exact, n, large).astype(np.int32)


FAR_BUCKET = int(_t5_bucket_np(np.array(2 * KB + 1)))


def _bucket_tiles():
    i = np.arange(QB)[None, :]
    j = np.arange(KB)[:, None]
    t0 = _t5_bucket_np(i - j)
    t1 = _t5_bucket_np(i - j + KB)
    assert (t5 := _t5_bucket_np(np.arange(KB + 1, 4096))).min() == t5.max() == FAR_BUCKET
    return np.stack([t0, t1]).astype(np.int32)


def _bias_kernel(bucket_ref, rb_ref, out_ref):
    h = pl.program_id(0)
    far = rb_ref[FAR_BUCKET, h]
    for k in range(2):
        bk = bucket_ref[k]
        acc = jnp.zeros((KB, QB), F32)
        for b in range(N_BUCKETS):
            acc = jnp.where(bk == b, rb_ref[b, h] - far, acc)
        out_ref[0, k] = acc * LOG2E
    out_ref[0, 2] = jnp.zeros((KB, QB), F32)


def _bias_tiles(rel_bias):
    bucket = jnp.asarray(_bucket_tiles())
    return pl.pallas_call(
        _bias_kernel,
        grid=(H_A,),
        in_specs=[pl.BlockSpec((2, KB, QB), lambda h: (0, 0, 0)),
                  pl.BlockSpec(memory_space=pltpu.SMEM)],
        out_specs=pl.BlockSpec((1, 3, KB, QB), lambda h: (h, 0, 0, 0)),
        out_shape=jax.ShapeDtypeStruct((H_A, 3, KB, QB), F32),
        name="bias_tiles",
    )(bucket, rel_bias)


_NT = (((1,), (1,)), ((), ()))


def _proj_kernel(x_ref, w_ref, ws_ref, o_ref, os_ref, xb_ref):
    @pl.when(pl.program_id(1) == 0)
    def _():
        xb_ref[...] = x_ref[...].astype(BF16)
        os_ref[...] = lax.dot_general(xb_ref[...], ws_ref[...], _NT, preferred_element_type=F32)

    o_ref[...] = lax.dot_general(xb_ref[...], w_ref[...], _NT, preferred_element_type=F32).astype(BF16)


def _proj(x2d, w_main, w_small, tm=1024, tn=1024):
    M = x2d.shape[0]
    return pl.pallas_call(
        _proj_kernel,
        grid=(M // tm, N_MAIN // tn),
        in_specs=[pl.BlockSpec((tm, D_MODEL), lambda i, j: (i, 0)),
                  pl.BlockSpec((tn, D_MODEL), lambda i, j: (j, 0)),
                  pl.BlockSpec((N_SMALL, D_MODEL), lambda i, j: (0, 0))],
        out_specs=[pl.BlockSpec((tm, tn), lambda i, j: (i, j)),
                   pl.BlockSpec((tm, N_SMALL), lambda i, j: (i, 0))],
        out_shape=[jax.ShapeDtypeStruct((M, N_MAIN), BF16),
                   jax.ShapeDtypeStruct((M, N_SMALL), F32)],
        scratch_shapes=[pltpu.VMEM((tm, D_MODEL), BF16)],
        compiler_params=pltpu.CompilerParams(
            dimension_semantics=("arbitrary", "arbitrary"), vmem_limit_bytes=VMEM_LIMIT),
        name="proj",
    )(x2d, w_main, w_small)


def _prep_kernel(s_ref, kvg_ref, ig_ref, ib_ref, ckv_ref, ckvt_ref, kidx_ref, gt_ref):
    c = s_ref[:, 0:D_C]
    c = c * lax.rsqrt(jnp.mean(c * c, axis=-1, keepdims=True) + LN_EPS) * kvg_ref[...]
    ckv_ref[...] = c.astype(BF16)
    for r in range(ckvt_ref.shape[0]):
        ckvt_ref[r, 0:D_C, :] = c[r * KB:(r + 1) * KB, :].T.astype(BF16)
        ckvt_ref[r, D_C:D_C + ONES_ROWS, :] = jnp.ones((ONES_ROWS, KB), BF16)
    tile = s_ref[:, D_C:D_C + LANES]
    k = tile[:, S_KI:S_KI + D_IDX]
    mu = jnp.mean(k, axis=-1, keepdims=True)
    var = jnp.mean(jnp.square(k - mu), axis=-1, keepdims=True)
    kidx_ref[...] = ((k - mu) * lax.rsqrt(var + LN_EPS) * ig_ref[...] + ib_ref[...]).astype(BF16)
    gt_ref[...] = tile.T


def _prep(small, kv_g, idx_g, idx_b, tm=1024):
    M = small.shape[0]
    return pl.pallas_call(
        _prep_kernel,
        grid=(M // tm,),
        in_specs=[pl.BlockSpec((tm, N_SMALL), lambda i: (i, 0)),
                  pl.BlockSpec((1, D_C), lambda i: (0, 0)),
                  pl.BlockSpec((1, D_IDX), lambda i: (0, 0)),
                  pl.BlockSpec((1, D_IDX), lambda i: (0, 0))],
        out_specs=[pl.BlockSpec((tm, D_C), lambda i: (i, 0)),
                   pl.BlockSpec((tm // KB, D_C + ONES_ROWS, KB), lambda i: (i, 0, 0)),
                   pl.BlockSpec((tm, D_IDX), lambda i: (i, 0)),
                   pl.BlockSpec((LANES, tm), lambda i: (0, i))],
        out_shape=[jax.ShapeDtypeStruct((M, D_C), BF16),
                   jax.ShapeDtypeStruct((M // KB, D_C + ONES_ROWS, KB), BF16),
                   jax.ShapeDtypeStruct((M, D_IDX), BF16),
                   jax.ShapeDtypeStruct((LANES, M), F32)],
        name="prep",
    )(small, kv_g, idx_g, idx_b)


def _key_to_float(key):
    bits = jnp.where(key < 0, key ^ jnp.int32(0x7FFFFFFF), key)
    return lax.bitcast_convert_type(bits, F32)


def _dsa_kernel(qa_ref, za_ref, qi_ref, gt_ref, ckv_ref, ckvt_ref, kidx_ref, wukt_ref, wuvt_ref, bias_ref,
                y_ref, qall_ref, qr_ref, ha_ref, hb_ref, sc_ref, mb_ref, sa_ref, sb_ref, pa_ref, pb_ref,
                ta_ref, tb_ref, m_ref, al_ref, acc_ref, w_ref, cand_ref, thr_ref, cut_ref, cge_ref):
    qi = pl.program_id(1)
    nkb = qi + 1
    last_kb = mb_ref.shape[0] - 1

    for h in range(H_A):
        ql = lax.dot_general(wukt_ref[h], qa_ref[:, h * DH_A:(h + 1) * DH_A], (((1,), (1,)), ((), ())),
                             preferred_element_type=F32)
        qall_ref[h // (H_A // 2), :, (h % (H_A // 2)) * QB:(h % (H_A // 2) + 1) * QB] = (
            ql * (DH_A ** -0.5 * LOG2E)).astype(BF16)


    @pl.when(qi * QB < TOPK)
    def _():
        krow = lax.broadcasted_iota(jnp.int32, (KB, QB), 0)
        qcol = lax.broadcasted_iota(jnp.int32, (KB, QB), 1)
        mb_ref[0] = jnp.where(krow <= qcol, 0.0, NEG).astype(F32)

    @pl.when(qi * QB >= TOPK)
    def _():
        for h in range(H_IDX):
            qr_ref[h * QB:(h + 1) * QB, :] = qi_ref[:, h * D_IDX:(h + 1) * D_IDX]
        w_ref[...] = gt_ref[S_WI:S_WI + H_IDX, :] * ((D_IDX ** -0.5) * (H_IDX ** -0.5))
        krow = lax.broadcasted_iota(jnp.int32, (KI, LANES), 0)
        qcol = lax.broadcasted_iota(jnp.int32, (KI, LANES), 1)
        n_ki = nkb * (KB // KI)
        last_ki = sc_ref.shape[0] // KI - 1

        def head_dots(ki, dst_ref):
            k = kidx_ref[pl.ds(pl.multiple_of(jnp.minimum(ki, last_ki) * KI, KI), KI), :]
            dst_ref[...] = lax.dot_general(k, qr_ref[...], (((1,), (1,)), ((), ())),
                                           preferred_element_type=F32)

        def reduce_heads(src_ref, ki):
            for g in range(QB // LANES):
                lanes = slice(g * LANES, (g + 1) * LANES)
                acc = jnp.zeros((KI, LANES), F32)
                for h in range(H_IDX):
                    acc = acc + (jnp.maximum(src_ref[:, h * QB + g * LANES:h * QB + (g + 1) * LANES], 0.0)
                                 * w_ref[h:h + 1, lanes])
                sc_ref[pl.ds(pl.multiple_of(ki * KI, KI), KI), lanes] = jnp.where(
                    krow + (ki * KI - qi * QB - g * LANES) <= qcol, acc, -jnp.inf)

        pad_kb = jnp.minimum(nkb, last_kb)
        sc_ref[pl.ds(pl.multiple_of(pad_kb * KB, KB), KB), :] = jnp.full((KB, QB), -jnp.inf, F32)

        head_dots(0, ha_ref)

        def sc_body(j, carry):
            head_dots(2 * j + 1, hb_ref)
            reduce_heads(ha_ref, 2 * j)
            head_dots(2 * j + 2, ha_ref)
            reduce_heads(hb_ref, 2 * j + 1)
            return carry
        lax.fori_loop(0, n_ki // 2, sc_body, 0)

        n_cnt = (nkb * KB + CNT_ROWS - 1) // CNT_ROWS

        def count_where(pred, steps=None):
            def body(c, acc):
                parts = []
                for g in range(QB // LANES):
                    lanes = slice(g * LANES, (g + 1) * LANES)
                    if steps is None:
                        blk = sc_ref[pl.ds(pl.multiple_of(c * CNT_ROWS, CNT_ROWS), CNT_ROWS), lanes]
                    else:
                        blk = sc_ref[c * CNT_ROWS:(c + 1) * CNT_ROWS, lanes]
                    hit = jnp.where(pred(blk, c * CNT_ROWS, lanes), 1.0, 0.0).astype(F32)
                    parts.append(jnp.sum(hit.reshape(CNT_ROWS // CNT_ACC, CNT_ACC, LANES), axis=0))
                return acc + jnp.concatenate(parts, axis=1)
            acc = jnp.zeros((CNT_ACC, QB), F32)
            if steps is None:
                acc = lax.fori_loop(0, n_cnt, body, acc)
            else:
                for c in range(steps):
                    acc = body(c, acc)
            return jnp.sum(acc, axis=0, keepdims=True)

        def search(steps):
            def bit_body(i, carry):
                u, c_ge = carry
                trial = u | lax.shift_left(jnp.int32(1), 31 - i)
                cand_ref[...] = _key_to_float(trial ^ jnp.int32(-2 ** 31))
                cnt = count_where(lambda blk, row0, lanes: blk >= cand_ref[:, lanes], steps)
                ok = cnt >= float(TOPK)
                return jnp.where(ok, trial, u), jnp.where(ok, cnt, c_ge)
            u, c_ge = lax.fori_loop(0, 32, bit_body, (jnp.zeros((1, QB), jnp.int32),
                                                      jnp.full((1, QB), float(sc_ref.shape[0]), F32)))
            thr_ref[...] = _key_to_float(u ^ jnp.int32(-2 ** 31))
            cge_ref[...] = c_ge

        for steps in range(1, sc_ref.shape[0] // CNT_ROWS + 1):
            pl.when(n_cnt == steps)(functools.partial(search, steps))
        has_ties = jnp.max(cge_ref[...]) > float(TOPK)

        @pl.when(jnp.logical_not(has_ties))
        def _():
            def mb_body(kb, carry):
                blk = sc_ref[pl.ds(pl.multiple_of(kb * KB, KB), KB), :]
                mb_ref[kb] = jnp.where(blk >= thr_ref[...], 0.0, NEG).astype(F32)
                return carry
            lax.fori_loop(0, nkb, mb_body, 0)

        @pl.when(has_ties)
        def _():
            c_gt = count_where(lambda blk, row0, lanes: blk > thr_ref[:, lanes])
            need = float(TOPK) - c_gt
            rows = lax.broadcasted_iota(jnp.int32, (CNT_ROWS, LANES), 0)
            n_bits = (sc_ref.shape[0] - 1).bit_length()

            def idx_body(i, cut):
                trial = cut | lax.shift_left(jnp.int32(1), n_bits - 1 - i)
                cut_ref[...] = trial
                before = count_where(lambda blk, row0, lanes: (blk == thr_ref[:, lanes])
                                     & (rows + row0 < cut_ref[:, lanes]))
                return jnp.where(before < need, trial, cut)
            cut_ref[...] = lax.fori_loop(0, n_bits, idx_body, jnp.zeros((1, QB), jnp.int32))

            def mb_body(kb, carry):
                for g in range(QB // LANES):
                    lanes = slice(g * LANES, (g + 1) * LANES)
                    blk = sc_ref[pl.ds(pl.multiple_of(kb * KB, KB), KB), lanes]
                    thr = thr_ref[:, lanes]
                    keep = (blk > thr) | ((blk == thr) & (rows[0:KB] + kb * KB <= cut_ref[:, lanes]))
                    mb_ref[kb, :, lanes] = jnp.where(keep, 0.0, NEG).astype(F32)
                return carry
            lax.fori_loop(0, nkb, mb_body, 0)

    m_ref[...] = jnp.full(m_ref.shape, NEG, F32)
    acc_ref[...] = jnp.zeros(acc_ref.shape, F32)

    hh = H_A // 2
    s_refs, p_refs, t_refs = (sa_ref, sb_ref), (pa_ref, pb_ref), (ta_ref, tb_ref)
    pb_ref[...] = jnp.zeros(pb_ref.shape, BF16)
    al_ref[...] = jnp.ones(al_ref.shape, F32)

    def logits(kb, half):
        kv = ckv_ref[pl.ds(pl.multiple_of(jnp.minimum(kb, last_kb) * KB, KB), KB), :]
        s_refs[half][...] = jnp.dot(kv, qall_ref[half], preferred_element_type=F32)

    def softmax(kb, half, with_bias):
        tile = jnp.clip(qi - kb, 0, 2)
        for j in range(hh):
            for g in range(QB // LANES):
                lanes = slice(g * LANES, (g + 1) * LANES)
                cols = slice(j * QB + g * LANES, j * QB + (g + 1) * LANES)
                x = s_refs[half][:, cols] + mb_ref[kb, :, lanes]
                if with_bias:
                    x = x + bias_ref[half * hh + j, tile, :, lanes]
                m_prev = m_ref[half, :, cols]
                m_blk = jnp.max(x.reshape(KB // CNT_ACC, CNT_ACC, LANES), axis=0)
                m_new = jnp.maximum(m_prev, jnp.max(m_blk, axis=0, keepdims=True))
                al_ref[half, :, cols] = jnp.exp2(m_prev - m_new)
                m_ref[half, :, cols] = m_new
                p_refs[half][:, cols] = jnp.exp2(x - m_new).astype(BF16)

    def accumulate(kb, half):
        t_refs[half][...] = jnp.dot(ckvt_ref[jnp.maximum(kb, 0)], p_refs[half][...],
                                    preferred_element_type=F32)
        acc_ref[half] = acc_ref[half] * al_ref[half] + t_refs[half][...]

    def sweep(first_kb, end_kb, with_bias):
        def body(kb, carry):
            logits(kb, 1)
            softmax(kb, 0, with_bias)
            accumulate(kb - 1, 1)
            logits(kb + 1, 0)
            softmax(kb, 1, with_bias)
            accumulate(kb, 0)
            return carry
        lax.fori_loop(first_kb, end_kb, body, 0)

    n_far = jnp.maximum(qi - 1, 0)
    logits(0, 0)
    sweep(0, nkb, True)
    accumulate(nkb - 1, 1)

    for h in range(H_A):
        half, cols = h // hh, slice((h % hh) * QB, (h % hh + 1) * QB)
        ya_t = jnp.dot(wuvt_ref[h], acc_ref[half, 0:D_C, cols].astype(BF16), preferred_element_type=F32)
        ya = (ya_t / acc_ref[half, D_C:D_C + 1, cols]).T
        z = za_ref[:, h * DH_A:(h + 1) * DH_A].astype(F32)
        y_ref[:, h * DH_A:(h + 1) * DH_A] = (ya * (z * jax.nn.sigmoid(z))).astype(BF16)


def _dsa(main, gate_t, ckv_n, ckv_t, kidx_n, w_uk_t, w_uv_t, bias, B, T):
    nq = T // QB
    return pl.pallas_call(
        _dsa_kernel,
        grid=(B, nq),
        in_specs=[pl.BlockSpec((QB, W_A), lambda b, q: (b * nq + q, C_QA // W_A)),
                  pl.BlockSpec((QB, W_A), lambda b, q: (b * nq + q, C_ZA // W_A)),
                  pl.BlockSpec((QB, H_IDX * D_IDX), lambda b, q: (b * nq + q, C_QI // (H_IDX * D_IDX))),
                  pl.BlockSpec((LANES, QB), lambda b, q: (0, b * nq + q)),
                  pl.BlockSpec((T, D_C), lambda b, q: (b, 0)),
                  pl.BlockSpec((T // KB, D_C + ONES_ROWS, KB), lambda b, q: (b, 0, 0)),
                  pl.BlockSpec((T, D_IDX), lambda b, q: (b, 0)),
                  pl.BlockSpec((H_A, D_C, DH_A), lambda b, q: (0, 0, 0)),
                  pl.BlockSpec((H_A, DH_A, D_C), lambda b, q: (0, 0, 0)),
                  pl.BlockSpec((H_A, 3, KB, QB), lambda b, q: (0, 0, 0, 0))],
        out_specs=pl.BlockSpec((QB, W_A), lambda b, q: (b * nq + q, 0)),
        out_shape=jax.ShapeDtypeStruct((B * T, W_A), BF16),
        scratch_shapes=[pltpu.VMEM((2, D_C, H_A // 2 * QB), BF16),
                        pltpu.VMEM((H_IDX * QB, D_IDX), BF16),
                        pltpu.VMEM((KI, H_IDX * QB), F32),
                        pltpu.VMEM((KI, H_IDX * QB), F32),
                        pltpu.VMEM((T, QB), F32),
                        pltpu.VMEM((T // KB, KB, QB), F32),
                        pltpu.VMEM((KB, H_A // 2 * QB), F32),
                        pltpu.VMEM((KB, H_A // 2 * QB), F32),
                        pltpu.VMEM((KB, H_A // 2 * QB), BF16),
                        pltpu.VMEM((KB, H_A // 2 * QB), BF16),
                        pltpu.VMEM((D_C + ONES_ROWS, H_A // 2 * QB), F32),
                        pltpu.VMEM((D_C + ONES_ROWS, H_A // 2 * QB), F32),
                        pltpu.VMEM((2, 1, H_A // 2 * QB), F32),
                        pltpu.VMEM((2, 1, H_A // 2 * QB), F32),
                        pltpu.VMEM((2, D_C + ONES_ROWS, H_A // 2 * QB), F32),
                        pltpu.VMEM((H_IDX, QB), F32),
                        pltpu.VMEM((1, QB), F32),
                        pltpu.VMEM((1, QB), F32),
                        pltpu.VMEM((1, QB), jnp.int32),
                        pltpu.VMEM((1, QB), F32)],
        compiler_params=pltpu.CompilerParams(
            dimension_semantics=("arbitrary", "arbitrary"), vmem_limit_bytes=VMEM_LIMIT),
        name="dsa",
    )(main, main, main, gate_t, ckv_n, ckv_t, kidx_n, w_uk_t, w_uv_t, bias)


def _split_dot(tri, x):
    hi = x.astype(BF16)
    lo = (x - hi.astype(F32)).astype(BF16)
    return jnp.dot(tri, hi, preferred_element_type=F32) + jnp.dot(tri, lo, preferred_element_type=F32)


def _log_sigmoid(x):
    return jnp.minimum(x, 0.0) - jnp.log1p(jnp.exp(-jnp.abs(x)))


def _mlstm_kernel(q_ref, k_ref, qh_ref, kh_ref, v_ref, o_ref, z_ref, g_ref, gt_ref,
                  cw_ref, cb_ref, gbr_ref, gbc_ref, ng_ref, y_ref, ct_ref, m_ref):
    c = pl.program_id(1)
    L = L_M

    @pl.when(c == 0)
    def _():
        ct_ref[...] = jnp.zeros(ct_ref.shape, F32)
        m_ref[...] = jnp.zeros(m_ref.shape, F32)

    r = lax.broadcasted_iota(jnp.int32, (L, L), 0)
    s = lax.broadcasted_iota(jnp.int32, (L, L), 1)
    causal = s <= r
    shifts = [jnp.where(r - s == d, 1.0, 0.0).astype(BF16) for d in range(1, CONV_W)]

    def conv_silu(x_ref, halo_ref, lo):
        x = x_ref[...]
        halo = jnp.where(c > 0, halo_ref[...].astype(F32), 0.0)
        w = cw_ref[:, lo:lo + H_M * DK_M]
        y = cb_ref[:, lo:lo + H_M * DK_M] + w[CONV_W - 1:CONV_W] * x.astype(F32)
        top = jnp.zeros((SUBLANES, H_M * DK_M), F32)
        for d in range(1, CONV_W):
            wd = w[CONV_W - 1 - d:CONV_W - d]
            y = y + wd * jnp.dot(shifts[d - 1], x, preferred_element_type=F32)
            top = top + wd * jnp.concatenate(
                [halo[HALO - d:HALO], jnp.zeros((SUBLANES - d, H_M * DK_M), F32)], axis=0)
        y = jnp.concatenate([y[0:SUBLANES] + top, y[SUBLANES:]], axis=0)
        return y * jax.nn.sigmoid(y)

    q_all = conv_silu(q_ref, qh_ref, 0)
    k_all = conv_silu(k_ref, kh_ref, H_M * DK_M) * (DK_M ** -0.5)

    gc = g_ref[...] + gbr_ref[...]
    gr = gt_ref[S_IM:S_IM + 2 * H_M, :] + gbc_ref[S_IM:S_IM + 2 * H_M, :]
    tri_l = jnp.where(causal, 1.0, 0.0).astype(BF16)
    tri_u = jnp.where(r <= s, 1.0, 0.0).astype(BF16)
    b_cols = _split_dot(tri_l, _log_sigmoid(gc) * LOG2E)
    lf_rows = _log_sigmoid(gr) * LOG2E
    b_rows = jnp.dot(lf_rows.astype(BF16), tri_u, preferred_element_type=F32) \
        + jnp.dot((lf_rows - lf_rows.astype(BF16).astype(F32)).astype(BF16), tri_u,
                  preferred_element_type=F32)
    gc = gc * LOG2E
    gr = gr * LOG2E
    ones = jnp.ones((L, LANES), BF16)

    for h in range(H_M):
        q = q_all[:, h * DK_M:(h + 1) * DK_M]
        k = k_all[:, h * DK_M:(h + 1) * DK_M]
        v = jnp.concatenate([v_ref[:, h * DV_M:(h + 1) * DV_M], ones], axis=1)
        qb = q.astype(BF16)
        b_c = b_cols[:, S_FM + h:S_FM + h + 1]
        i_c = gc[:, S_IM + h:S_IM + h + 1]
        b_r = b_rows[H_M + h:H_M + h + 1, :]
        i_r = gr[h:h + 1, :]
        m_prev = m_ref[h]
        ct = ct_ref[h]

        log_d = jnp.where(causal, b_c - b_r + i_r, -jnp.inf)
        g = b_c + m_prev
        m_t = jnp.maximum(jnp.max(log_d, axis=-1, keepdims=True), g)
        qk = lax.dot_general(qb, k.astype(BF16), _NT, preferred_element_type=F32)
        s_mat = qk * jnp.exp2(log_d - m_t)
        inter = jnp.exp2(g - m_t)
        num = jnp.dot(s_mat.astype(BF16), v, preferred_element_type=F32) \
            + inter * jnp.dot(qb, ct.astype(BF16), preferred_element_type=F32)
        den = jnp.maximum(jnp.abs(num[:, DV_M:]), jnp.exp2(-m_t))
        hh = num[:, 0:DV_M] / jnp.concatenate([den] * (DV_M // LANES), axis=1)

        b_last = b_c[L - 1:L, :]
        a_r = b_last - b_r + i_r
        m_new = jnp.maximum(b_last + m_prev, jnp.max(a_r, axis=-1, keepdims=True))
        decay = jnp.exp2(b_last + m_prev - m_new)
        wgt_c = jnp.exp2(b_last - b_c + i_c - m_new)
        kw = k * wgt_c
        ct_ref[h] = decay * ct + jnp.dot(kw.T.astype(BF16), v, preferred_element_type=F32)
        m_ref[h] = m_new

        mu = jnp.mean(hh, axis=-1, keepdims=True)
        var = jnp.mean(jnp.square(hh - mu), axis=-1, keepdims=True)
        hn = (hh - mu) * lax.rsqrt(var + LN_EPS) * ng_ref[:, h * DV_M:(h + 1) * DV_M]
        og = o_ref[:, h * DV_M:(h + 1) * DV_M].astype(F32)
        zg = z_ref[:, h * DV_M:(h + 1) * DV_M].astype(F32)
        y_ref[:, h * DV_M:(h + 1) * DV_M] = (hn * jax.nn.sigmoid(og) * (zg * jax.nn.sigmoid(zg))).astype(BF16)


def _mlstm(main, small, gate_t, conv_w, conv_b, gb_row, gb_col, norm_g, B, T):
    nc = T // L_M
    hb = L_M // HALO
    qk_w = H_M * DK_M

    def halo_map(col):
        return lambda b, c: (jnp.maximum((b * nc + c) * hb - 1, 0), col)

    return pl.pallas_call(
        _mlstm_kernel,
        grid=(B, nc),
        in_specs=[pl.BlockSpec((L_M, qk_w), lambda b, c: (b * nc + c, C_QM // qk_w)),
                  pl.BlockSpec((L_M, qk_w), lambda b, c: (b * nc + c, C_KM // qk_w)),
                  pl.BlockSpec((HALO, qk_w), halo_map(C_QM // qk_w)),
                  pl.BlockSpec((HALO, qk_w), halo_map(C_KM // qk_w)),
                  pl.BlockSpec((L_M, W_M), lambda b, c: (b * nc + c, C_VM // W_M)),
                  pl.BlockSpec((L_M, W_M), lambda b, c: (b * nc + c, C_OM // W_M)),
                  pl.BlockSpec((L_M, W_M), lambda b, c: (b * nc + c, C_ZM // W_M)),
                  pl.BlockSpec((L_M, LANES), lambda b, c: (b * nc + c, D_C // LANES)),
                  pl.BlockSpec((LANES, L_M), lambda b, c: (0, b * nc + c)),
                  pl.BlockSpec((CONV_W, 2 * qk_w), lambda b, c: (0, 0)),
                  pl.BlockSpec((1, 2 * qk_w), lambda b, c: (0, 0)),
                  pl.BlockSpec((1, LANES), lambda b, c: (0, 0)),
                  pl.BlockSpec((LANES, 1), lambda b, c: (0, 0)),
                  pl.BlockSpec((1, W_M), lambda b, c: (0, 0))],
        out_specs=pl.BlockSpec((L_M, W_M), lambda b, c: (b * nc + c, 0)),
        out_shape=jax.ShapeDtypeStruct((B * T, W_M), BF16),
        scratch_shapes=[pltpu.VMEM((H_M, DK_M, DV_M + LANES), F32),
                        pltpu.VMEM((H_M, 1, 1), F32)],
        compiler_params=pltpu.CompilerParams(
            dimension_semantics=("arbitrary", "arbitrary"), vmem_limit_bytes=VMEM_LIMIT),
        name="mlstm",
    )(main, main, main, main, main, main, main, small, gate_t,
      conv_w, conv_b, gb_row, gb_col, norm_g)


def _out_kernel(ya_ref, ym_ref, x_ref, w_ref, g_ref, b_ref, o_ref):
    y = jnp.dot(ya_ref[...], w_ref[0:W_A, :], preferred_element_type=F32)
    y = y + jnp.dot(ym_ref[...], w_ref[W_A:W_A + W_M, :], preferred_element_type=F32)
    r = ALPHA * x_ref[...] + y
    mu = jnp.mean(r, axis=-1, keepdims=True)
    var = jnp.mean(jnp.square(r - mu), axis=-1, keepdims=True)
    o_ref[...] = (r - mu) * lax.rsqrt(var + LN_EPS) * g_ref[...] + b_ref[...]


def _out(ya, ym, x2d, w_out, ln_g, ln_b, tm=512):
    M = x2d.shape[0]
    return pl.pallas_call(
        _out_kernel,
        grid=(M // tm,),
        in_specs=[pl.BlockSpec((tm, W_A), lambda i: (i, 0)),
                  pl.BlockSpec((tm, W_M), lambda i: (i, 0)),
                  pl.BlockSpec((tm, D_MODEL), lambda i: (i, 0)),
                  pl.BlockSpec((W_A + W_M, D_MODEL), lambda i: (0, 0)),
                  pl.BlockSpec((1, D_MODEL), lambda i: (0, 0)),
                  pl.BlockSpec((1, D_MODEL), lambda i: (0, 0))],
        out_specs=pl.BlockSpec((tm, D_MODEL), lambda i: (i, 0)),
        out_shape=jax.ShapeDtypeStruct((M, D_MODEL), F32),
        compiler_params=pltpu.CompilerParams(
            dimension_semantics=("arbitrary",), vmem_limit_bytes=VMEM_LIMIT),
        name="out",
    )(ya, ym, x2d, w_out, ln_g, ln_b)


_W_IN_SEGS = (("q_a", W_A), ("c_kv", D_C), ("z_a", W_A), ("q_i", H_IDX * D_IDX), ("k_i", D_IDX),
              ("w_i", H_IDX), ("q_m", H_M * DK_M), ("k_m", H_M * DK_M), ("v_m", W_M), ("i_m", H_M),
              ("f_m", H_M), ("o_m", W_M), ("z_m", W_M))
_MAIN_ORDER = ("q_a", "z_a", "q_i", "q_m", "k_m", "v_m", "o_m", "z_m")
_SEG_NAMES = [name for name, _ in _W_IN_SEGS]
assert _SEG_NAMES.index("f_m") == _SEG_NAMES.index("i_m") + 1 and S_FM == S_IM + H_M


def _repack_kernel(wt_ref, main_ref, small_ref):
    src, off = {}, 0
    for name, width in _W_IN_SEGS:
        src[name] = (off, width)
        off += width
    dst = 0
    for name in _MAIN_ORDER:
        lo, width = src[name]
        main_ref[dst:dst + width, :] = wt_ref[lo:lo + width, :].astype(BF16)
        dst += width
    parts = [wt_ref[src[name][0]:src[name][0] + src[name][1], :] for name in ("c_kv", "k_i", "w_i")]
    lo = src["i_m"][0]
    parts.append(wt_ref[lo:lo + 2 * H_M, :])
    used = sum(p.shape[0] for p in parts)
    parts.append(jnp.zeros((N_SMALL - used, wt_ref.shape[1]), F32))
    small_ref[...] = jnp.concatenate(parts, axis=0).astype(BF16)


def _repack_w_in(w_in, tc=256):
    n_cols = sum(width for _, width in _W_IN_SEGS)
    wt = jnp.swapaxes(w_in, 1, 2)[0]
    return pl.pallas_call(
        _repack_kernel,
        grid=(D_MODEL // tc,),
        in_specs=[pl.BlockSpec((n_cols, tc), lambda i: (0, i))],
        out_specs=[pl.BlockSpec((N_MAIN, tc), lambda i: (0, i)),
                   pl.BlockSpec((N_SMALL, tc), lambda i: (0, i))],
        out_shape=[jax.ShapeDtypeStruct((N_MAIN, D_MODEL), BF16),
                   jax.ShapeDtypeStruct((N_SMALL, D_MODEL), BF16)],
        compiler_params=pltpu.CompilerParams(
            dimension_semantics=("arbitrary",), vmem_limit_bytes=VMEM_LIMIT),
        name="repack",
    )(wt)


def kernel(x, w_in, b_igate, b_fgate, kv_norm_g, w_uk, w_uv, idx_k_ln_g, idx_k_ln_b, rel_bias,
           conv_w, conv_b, mh_norm_g, w_out, ln_g, ln_b):
    B, T, D = x.shape
    assert D == D_MODEL and T % L_M == 0 and T % (2 * KB) == 0 and w_in.shape[0] == 1
    bias = _bias_tiles(rel_bias)
    x2d = x.reshape(B * T, D)
    w_main, w_small = _repack_w_in(w_in)
    main, small = _proj(x2d, w_main, w_small)
    ckv_n, ckv_t, kidx_n, gate_t = _prep(small, kv_norm_g[0][None], idx_k_ln_g[0][None], idx_k_ln_b[0][None])
    w_uk_t = jnp.transpose(w_uk[0], (0, 2, 1)).astype(BF16)
    w_uv_t = jnp.transpose(w_uv[0], (0, 2, 1)).astype(BF16)
    ya = _dsa(main, gate_t, ckv_n, ckv_t, kidx_n, w_uk_t, w_uv_t, bias, B, T)
    gb = jnp.zeros((LANES,), F32).at[S_IM:S_IM + H_M].set(b_igate[0]).at[S_FM:S_FM + H_M].set(b_fgate[0])
    ym = _mlstm(main, small, gate_t, conv_w[0], conv_b[0][None], gb[None, :], gb[:, None],
                mh_norm_g[0][None], B, T)
    out = _out(ya, ym, x2d, w_out[0].astype(BF16), ln_g[0][None], ln_b[0][None])
    return out.reshape(B, T, D)
```

```python
import functools
import math

import numpy as np
import jax
import jax.numpy as jnp
from jax import lax
from jax.experimental import pallas as pl
from jax.experimental.pallas import tpu as pltpu

F32 = jnp.float32
BF16 = jnp.bfloat16

D_MODEL = 2048
W_A = 1024
DH_A = 128
H_A = 8
D_C = 256
H_IDX = 16
D_IDX = 64
TOPK = 256
W_M = 1024
H_M = 4
DV_M = 256
DK_M = 128
CONV_W = 4
N_BUCKETS = 32
MAX_DIST = 128
ALPHA = 2.0 ** 0.25
LN_EPS = 1e-5

LANES = 128
SUBLANES = 8
VMEM_LIMIT = 56 * 1024 * 1024

QB = 256
KB = 256
KI = 128
CNT_ROWS = 512
CNT_ACC = 4 * SUBLANES
L_M = 256
HALO = 16
NEG = -1e30
LOG2E = math.log2(math.e)
ONES_ROWS = 16

C_QA, C_ZA, C_QI, C_QM, C_KM, C_VM, C_OM, C_ZM = 0, 1024, 2048, 3072, 3584, 4096, 5120, 6144
N_MAIN = 7168
N_SMALL = 384
S_KI, S_WI, S_IM, S_FM = 0, 64, 80, 84


def _t5_bucket_np(rel):
    max_exact = N_BUCKETS // 2
    n = np.maximum(rel, 0)
    nf = np.maximum(n, 1).astype(np.float32)
    large = max_exact + (np.log(nf / np.float32(max_exact)) / np.float32(math.log(MAX_DIST / max_exact))
                         * np.float32(N_BUCKETS - max_exact)).astype(np.int32)
    large = np.minimum(large, N_BUCKETS - 1)
    return np.where(n < max_exact, n, large).astype(np.int32)


FAR_BUCKET = int(_t5_bucket_np(np.array(2 * KB + 1)))


def _bucket_tiles():
    i = np.arange(QB)[None, :]
    j = np.arange(KB)[:, None]
    t0 = _t5_bucket_np(i - j)
    t1 = _t5_bucket_np(i - j + KB)
    assert (t5 := _t5_bucket_np(np.arange(KB + 1, 4096))).min() == t5.max() == FAR_BUCKET
    return np.stack([t0, t1]).astype(np.int32)


def _bias_kernel(bucket_ref, rb_ref, out_ref):
    h = pl.program_id(0)
    far = rb_ref[FAR_BUCKET, h]
    for k in range(2):
        bk = bucket_ref[k]
        acc = jnp.zeros((KB, QB), F32)
        for b in range(N_BUCKETS):
            acc = jnp.where(bk == b, rb_ref[b, h] - far, acc)
        out_ref[0, k] = acc * LOG2E
    out_ref[0, 2] = jnp.zeros((KB, QB), F32)


def _bias_tiles(rel_bias):
    bucket = jnp.asarray(_bucket_tiles())
    return pl.pallas_call(
        _bias_kernel,
        grid=(H_A,),
        in_specs=[pl.BlockSpec((2, KB, QB), lambda h: (0, 0, 0)),
                  pl.BlockSpec(memory_space=pltpu.SMEM)],
        out_specs=pl.BlockSpec((1, 3, KB, QB), lambda h: (h, 0, 0, 0)),
        out_shape=jax.ShapeDtypeStruct((H_A, 3, KB, QB), F32),
        name="bias_tiles",
    )(bucket, rel_bias)


_NT = (((1,), (1,)), ((), ()))


def _proj_kernel(x_ref, w_ref, ws_ref, o_ref, os_ref, xb_ref):
    @pl.when(pl.program_id(1) == 0)
    def _():
        xb_ref[...] = x_ref[...].astype(BF16)
        os_ref[...] = lax.dot_general(xb_ref[...], ws_ref[...], _NT, preferred_element_type=F32)

    o_ref[...] = lax.dot_general(xb_ref[...], w_ref[...], _NT, preferred_element_type=F32).astype(BF16)


def _proj(x2d, w_main, w_small, tm=1024, tn=1024):
    M = x2d.shape[0]
    return pl.pallas_call(
        _proj_kernel,
        grid=(M // tm, N_MAIN // tn),
        in_specs=[pl.BlockSpec((tm, D_MODEL), lambda i, j: (i, 0)),
                  pl.BlockSpec((tn, D_MODEL), lambda i, j: (j, 0)),
                  pl.BlockSpec((N_SMALL, D_MODEL), lambda i, j: (0, 0))],
        out_specs=[pl.BlockSpec((tm, tn), lambda i, j: (i, j)),
                   pl.BlockSpec((tm, N_SMALL), lambda i, j: (i, 0))],
        out_shape=[jax.ShapeDtypeStruct((M, N_MAIN), BF16),
                   jax.ShapeDtypeStruct((M, N_SMALL), F32)],
        scratch_shapes=[pltpu.VMEM((tm, D_MODEL), BF16)],
        compiler_params=pltpu.CompilerParams(
            dimension_semantics=("arbitrary", "arbitrary"), vmem_limit_bytes=VMEM_LIMIT),
        name="proj",
    )(x2d, w_main, w_small)


def _prep_kernel(s_ref, kvg_ref, ig_ref, ib_ref, ckv_ref, ckvt_ref, kidx_ref, gt_ref):
    c = s_ref[:, 0:D_C]
    c = c * lax.rsqrt(jnp.mean(c * c, axis=-1, keepdims=True) + LN_EPS) * kvg_ref[...]
    ckv_ref[...] = c.astype(BF16)
    for r in range(ckvt_ref.shape[0]):
        ckvt_ref[r, 0:D_C, :] = c[r * KB:(r + 1) * KB, :].T.astype(BF16)
        ckvt_ref[r, D_C:D_C + ONES_ROWS, :] = jnp.ones((ONES_ROWS, KB), BF16)
    tile = s_ref[:, D_C:D_C + LANES]
    k = tile[:, S_KI:S_KI + D_IDX]
    mu = jnp.mean(k, axis=-1, keepdims=True)
    var = jnp.mean(jnp.square(k - mu), axis=-1, keepdims=True)
    kidx_ref[...] = ((k - mu) * lax.rsqrt(var + LN_EPS) * ig_ref[...] + ib_ref[...]).astype(BF16)
    gt_ref[...] = tile.T


def _prep(small, kv_g, idx_g, idx_b, tm=1024):
    M = small.shape[0]
    return pl.pallas_call(
        _prep_kernel,
        grid=(M // tm,),
        in_specs=[pl.BlockSpec((tm, N_SMALL), lambda i: (i, 0)),
                  pl.BlockSpec((1, D_C), lambda i: (0, 0)),
                  pl.BlockSpec((1, D_IDX), lambda i: (0, 0)),
                  pl.BlockSpec((1, D_IDX), lambda i: (0, 0))],
        out_specs=[pl.BlockSpec((tm, D_C), lambda i: (i, 0)),
                   pl.BlockSpec((tm // KB, D_C + ONES_ROWS, KB), lambda i: (i, 0, 0)),
                   pl.BlockSpec((tm, D_IDX), lambda i: (i, 0)),
                   pl.BlockSpec((LANES, tm), lambda i: (0, i))],
        out_shape=[jax.ShapeDtypeStruct((M, D_C), BF16),
                   jax.ShapeDtypeStruct((M // KB, D_C + ONES_ROWS, KB), BF16),
                   jax.ShapeDtypeStruct((M, D_IDX), BF16),
                   jax.ShapeDtypeStruct((LANES, M), F32)],
        name="prep",
    )(small, kv_g, idx_g, idx_b)


def _key_to_float(key):
    bits = jnp.where(key < 0, key ^ jnp.int32(0x7FFFFFFF), key)
    return lax.bitcast_convert_type(bits, F32)


def _dsa_kernel(qa_ref, za_ref, qi_ref, gt_ref, ckv_ref, ckvt_ref, kidx_ref, wukt_ref, wuvt_ref, bias_ref,
                y_ref, qall_ref, qr_ref, ha_ref, hb_ref, sc_ref, mb_ref, sa_ref, sb_ref, pa_ref, pb_ref,
                ta_ref, tb_ref, m_ref, al_ref, acc_ref, w_ref, cand_ref, thr_ref, cut_ref, cge_ref):
    qi = pl.program_id(1)
    nkb = qi + 1
    last_kb = mb_ref.shape[0] - 1

    for h in range(H_A):
        ql = lax.dot_general(wukt_ref[h], qa_ref[:, h * DH_A:(h + 1) * DH_A], (((1,), (1,)), ((), ())),
                             preferred_element_type=F32)
        qall_ref[h // (H_A // 2), :, (h % (H_A // 2)) * QB:(h % (H_A // 2) + 1) * QB] = (
            ql * (DH_A ** -0.5 * LOG2E)).astype(BF16)


    @pl.when(qi * QB < TOPK)
    def _():
        krow = lax.broadcasted_iota(jnp.int32, (KB, QB), 0)
        qcol = lax.broadcasted_iota(jnp.int32, (KB, QB), 1)
        mb_ref[0] = jnp.where(krow <= qcol, 0.0, NEG).astype(F32)

    @pl.when(qi * QB >= TOPK)
    def _():
        for h in range(H_IDX):
            qr_ref[h * QB:(h + 1) * QB, :] = qi_ref[:, h * D_IDX:(h + 1) * D_IDX]
        w_ref[...] = gt_ref[S_WI:S_WI + H_IDX, :] * ((D_IDX ** -0.5) * (H_IDX ** -0.5))
        krow = lax.broadcasted_iota(jnp.int32, (KI, LANES), 0)
        qcol = lax.broadcasted_iota(jnp.int32, (KI, LANES), 1)
        n_ki = nkb * (KB // KI)
        last_ki = sc_ref.shape[0] // KI - 1

        def head_dots(ki, dst_ref):
            k = kidx_ref[pl.ds(pl.multiple_of(jnp.minimum(ki, last_ki) * KI, KI), KI), :]
            dst_ref[...] = lax.dot_general(k, qr_ref[...], (((1,), (1,)), ((), ())),
                                           preferred_element_type=F32)

        def reduce_heads(src_ref, ki):
            for g in range(QB // LANES):
                lanes = slice(g * LANES, (g + 1) * LANES)
                acc = jnp.zeros((KI, LANES), F32)
                for h in range(H_IDX):
                    acc = acc + (jnp.maximum(src_ref[:, h * QB + g * LANES:h * QB + (g + 1) * LANES], 0.0)
                                 * w_ref[h:h + 1, lanes])
                sc_ref[pl.ds(pl.multiple_of(ki * KI, KI), KI), lanes] = jnp.where(
                    krow + (ki * KI - qi * QB - g * LANES) <= qcol, acc, -jnp.inf)

        pad_kb = jnp.minimum(nkb, last_kb)
        sc_ref[pl.ds(pl.multiple_of(pad_kb * KB, KB), KB), :] = jnp.full((KB, QB), -jnp.inf, F32)

        head_dots(0, ha_ref)

        def sc_body(j, carry):
            head_dots(2 * j + 1, hb_ref)
            reduce_heads(ha_ref, 2 * j)
            head_dots(2 * j + 2, ha_ref)
            reduce_heads(hb_ref, 2 * j + 1)
            return carry
        lax.fori_loop(0, n_ki // 2, sc_body, 0)

        n_cnt = (nkb * KB + CNT_ROWS - 1) // CNT_ROWS

        def count_where(pred, steps=None):
            def body(c, acc):
                parts = []
                for g in range(QB // LANES):
                    lanes = slice(g * LANES, (g + 1) * LANES)
                    if steps is None:
                        blk = sc_ref[pl.ds(pl.multiple_of(c * CNT_ROWS, CNT_ROWS), CNT_ROWS), lanes]
                    else:
                        blk = sc_ref[c * CNT_ROWS:(c + 1) * CNT_ROWS, lanes]
                    hit = jnp.where(pred(blk, c * CNT_ROWS, lanes), 1.0, 0.0).astype(F32)
                    parts.append(jnp.sum(hit.reshape(CNT_ROWS // CNT_ACC, CNT_ACC, LANES), axis=0))
                return acc + jnp.concatenate(parts, axis=1)
            acc = jnp.zeros((CNT_ACC, QB), F32)
            if steps is None:
                acc = lax.fori_loop(0, n_cnt, body, acc)
            else:
                for c in range(steps):
                    acc = body(c, acc)
            return jnp.sum(acc, axis=0, keepdims=True)

        def search(steps):
            def bit_body(i, carry):
                u, c_ge = carry
                trial = u | lax.shift_left(jnp.int32(1), 31 - i)
                cand_ref[...] = _key_to_float(trial ^ jnp.int32(-2 ** 31))
                cnt = count_where(lambda blk, row0, lanes: blk >= cand_ref[:, lanes], steps)
                ok = cnt >= float(TOPK)
                return jnp.where(ok, trial, u), jnp.where(ok, cnt, c_ge)
            u, c_ge = lax.fori_loop(0, 32, bit_body, (jnp.zeros((1, QB), jnp.int32),
                                                      jnp.full((1, QB), float(sc_ref.shape[0]), F32)))
            thr_ref[...] = _key_to_float(u ^ jnp.int32(-2 ** 31))
            cge_ref[...] = c_ge

        for steps in range(1, sc_ref.shape[0] // CNT_ROWS + 1):
            pl.when(n_cnt == steps)(functools.partial(search, steps))
        has_ties = jnp.max(cge_ref[...]) > float(TOPK)

        @pl.when(jnp.logical_not(has_ties))
        def _():
            def mb_body(kb, carry):
                blk = sc_ref[pl.ds(pl.multiple_of(kb * KB, KB), KB), :]
                mb_ref[kb] = jnp.where(blk >= thr_ref[...], 0.0, NEG).astype(F32)
                return carry
            lax.fori_loop(0, nkb, mb_body, 0)

        @pl.when(has_ties)
        def _():
            c_gt = count_where(lambda blk, row0, lanes: blk > thr_ref[:, lanes])
            need = float(TOPK) - c_gt
            rows = lax.broadcasted_iota(jnp.int32, (CNT_ROWS, LANES), 0)
            n_bits = (sc_ref.shape[0] - 1).bit_length()

            def idx_body(i, cut):
                trial = cut | lax.shift_left(jnp.int32(1), n_bits - 1 - i)
                cut_ref[...] = trial
                before = count_where(lambda blk, row0, lanes: (blk == thr_ref[:, lanes])
                                     & (rows + row0 < cut_ref[:, lanes]))
                return jnp.where(before < need, trial, cut)
            cut_ref[...] = lax.fori_loop(0, n_bits, idx_body, jnp.zeros((1, QB), jnp.int32))

            def mb_body(kb, carry):
                for g in range(QB // LANES):
                    lanes = slice(g * LANES, (g + 1) * LANES)
                    blk = sc_ref[pl.ds(pl.multiple_of(kb * KB, KB), KB), lanes]
                    thr = thr_ref[:, lanes]
                    keep = (blk > thr) | ((blk == thr) & (rows[0:KB] + kb * KB <= cut_ref[:, lanes]))
                    mb_ref[kb, :, lanes] = jnp.where(keep, 0.0, NEG).astype(F32)
                return carry
            lax.fori_loop(0, nkb, mb_body, 0)

    m_ref[...] = jnp.full(m_ref.shape, NEG, F32)
    acc_ref[...] = jnp.zeros(acc_ref.shape, F32)

    hh = H_A // 2
    s_refs, p_refs, t_refs = (sa_ref, sb_ref), (pa_ref, pb_ref), (ta_ref, tb_ref)
    pb_ref[...] = jnp.zeros(pb_ref.shape, BF16)
    al_ref[...] = jnp.ones(al_ref.shape, F32)

    def logits(kb, half):
        kv = ckv_ref[pl.ds(pl.multiple_of(jnp.minimum(kb, last_kb) * KB, KB), KB), :]
        s_refs[half][...] = jnp.dot(kv, qall_ref[half], preferred_element_type=F32)

    def softmax(kb, half, with_bias):
        tile = jnp.clip(qi - kb, 0, 2)
        for j in range(hh):
            for g in range(QB // LANES):
                lanes = slice(g * LANES, (g + 1) * LANES)
                cols = slice(j * QB + g * LANES, j * QB + (g + 1) * LANES)
                x = s_refs[half][:, cols] + mb_ref[kb, :, lanes]
                if with_bias:
                    x = x + bias_ref[half * hh + j, tile, :, lanes]
                m_prev = m_ref[half, :, cols]
                m_blk = jnp.max(x.reshape(KB // CNT_ACC, CNT_ACC, LANES), axis=0)
                m_new = jnp.maximum(m_prev, jnp.max(m_blk, axis=0, keepdims=True))
                al_ref[half, :, cols] = jnp.exp2(m_prev - m_new)
                m_ref[half, :, cols] = m_new
                p_refs[half][:, cols] = jnp.exp2(x - m_new).astype(BF16)

    def accumulate(kb, half):
        t_refs[half][...] = jnp.dot(ckvt_ref[jnp.maximum(kb, 0)], p_refs[half][...],
                                    preferred_element_type=F32)
        acc_ref[half] = acc_ref[half] * al_ref[half] + t_refs[half][...]

    def sweep(first_kb, end_kb, with_bias):
        def body(kb, carry):
            logits(kb, 1)
            softmax(kb, 0, with_bias)
            accumulate(kb - 1, 1)
            logits(kb + 1, 0)
            softmax(kb, 1, with_bias)
            accumulate(kb, 0)
            return carry
        lax.fori_loop(first_kb, end_kb, body, 0)

    n_far = jnp.maximum(qi - 1, 0)
    logits(0, 0)
    sweep(0, n_far, False)
    sweep(n_far, nkb, True)
    accumulate(nkb - 1, 1)

    for h in range(H_A):
        half, cols = h // hh, slice((h % hh) * QB, (h % hh + 1) * QB)
        ya_t = jnp.dot(wuvt_ref[h], acc_ref[half, 0:D_C, cols].astype(BF16), preferred_element_type=F32)
        ya = (ya_t / acc_ref[half, D_C:D_C + 1, cols]).T
        z = za_ref[:, h * DH_A:(h + 1) * DH_A].astype(F32)
        y_ref[:, h * DH_A:(h + 1) * DH_A] = (ya * (z * jax.nn.sigmoid(z))).astype(BF16)


def _dsa(main, gate_t, ckv_n, ckv_t, kidx_n, w_uk_t, w_uv_t, bias, B, T):
    nq = T // QB
    return pl.pallas_call(
        _dsa_kernel,
        grid=(B, nq),
        in_specs=[pl.BlockSpec((QB, W_A), lambda b, q: (b * nq + q, C_QA // W_A)),
                  pl.BlockSpec((QB, W_A), lambda b, q: (b * nq + q, C_ZA // W_A)),
                  pl.BlockSpec((QB, H_IDX * D_IDX), lambda b, q: (b * nq + q, C_QI // (H_IDX * D_IDX))),
                  pl.BlockSpec((LANES, QB), lambda b, q: (0, b * nq + q)),
                  pl.BlockSpec((T, D_C), lambda b, q: (b, 0)),
                  pl.BlockSpec((T // KB, D_C + ONES_ROWS, KB), lambda b, q: (b, 0, 0)),
                  pl.BlockSpec((T, D_IDX), lambda b, q: (b, 0)),
                  pl.BlockSpec((H_A, D_C, DH_A), lambda b, q: (0, 0, 0)),
                  pl.BlockSpec((H_A, DH_A, D_C), lambda b, q: (0, 0, 0)),
                  pl.BlockSpec((H_A, 3, KB, QB), lambda b, q: (0, 0, 0, 0))],
        out_specs=pl.BlockSpec((QB, W_A), lambda b, q: (b * nq + q, 0)),
        out_shape=jax.ShapeDtypeStruct((B * T, W_A), BF16),
        scratch_shapes=[pltpu.VMEM((2, D_C, H_A // 2 * QB), BF16),
                        pltpu.VMEM((H_IDX * QB, D_IDX), BF16),
                        pltpu.VMEM((KI, H_IDX * QB), F32),
                        pltpu.VMEM((KI, H_IDX * QB), F32),
                        pltpu.VMEM((T, QB), F32),
                        pltpu.VMEM((T // KB, KB, QB), F32),
                        pltpu.VMEM((KB, H_A // 2 * QB), F32),
                        pltpu.VMEM((KB, H_A // 2 * QB), F32),
                        pltpu.VMEM((KB, H_A // 2 * QB), BF16),
                        pltpu.VMEM((KB, H_A // 2 * QB), BF16),
                        pltpu.VMEM((D_C + ONES_ROWS, H_A // 2 * QB), F32),
                        pltpu.VMEM((D_C + ONES_ROWS, H_A // 2 * QB), F32),
                        pltpu.VMEM((2, 1, H_A // 2 * QB), F32),
                        pltpu.VMEM((2, 1, H_A // 2 * QB), F32),
                        pltpu.VMEM((2, D_C + ONES_ROWS, H_A // 2 * QB), F32),
                        pltpu.VMEM((H_IDX, QB), F32),
                        pltpu.VMEM((1, QB), F32),
                        pltpu.VMEM((1, QB), F32),
                        pltpu.VMEM((1, QB), jnp.int32),
                        pltpu.VMEM((1, QB), F32)],
        compiler_params=pltpu.CompilerParams(
            dimension_semantics=("arbitrary", "arbitrary"), vmem_limit_bytes=VMEM_LIMIT),
        name="dsa",
    )(main, main, main, gate_t, ckv_n, ckv_t, kidx_n, w_uk_t, w_uv_t, bias)


def _split_dot(tri, x):
    hi = x.astype(BF16)
    lo = (x - hi.astype(F32)).astype(BF16)
    return jnp.dot(tri, hi, preferred_element_type=F32) + jnp.dot(tri, lo, preferred_element_type=F32)


def _log_sigmoid(x):
    return jnp.minimum(x, 0.0) - jnp.log1p(jnp.exp(-jnp.abs(x)))


def _mlstm_out_kernel(q_ref, k_ref, qh_ref, kh_ref, v_ref, o_ref, z_ref, g_ref, gt_ref,
                      cw_ref, cb_ref, gbr_ref, gbc_ref, ng_ref, ya_ref, x_ref, w_ref, lg_ref, lb_ref,
                      out_ref, ct_ref, m_ref, y_ref, yp_ref, p_ref, *, n_chunks, n_steps):
    step = pl.program_id(0)
    c = jnp.minimum(step, n_steps - 1) % n_chunks
    L = L_M

    @pl.when(step == 0)
    def _():
        yp_ref[...] = jnp.zeros(yp_ref.shape, BF16)

    @pl.when(c == 0)
    def _():
        ct_ref[...] = jnp.zeros(ct_ref.shape, F32)
        m_ref[...] = jnp.zeros(m_ref.shape, F32)

    n_pc = 2 * H_M
    pw = D_MODEL // n_pc

    def project_chunk(j):
        cols = slice(j * pw, (j + 1) * pw)
        p_ref[:, cols] = (jnp.dot(ya_ref[...], w_ref[0:W_A, cols], preferred_element_type=F32)
                          + jnp.dot(yp_ref[...], w_ref[W_A:W_A + W_M, cols], preferred_element_type=F32))

    def norm_previous():
        res = ALPHA * x_ref[...] + p_ref[...]
        mean = jnp.mean(res, axis=-1, keepdims=True)
        var_r = jnp.mean(jnp.square(res - mean), axis=-1, keepdims=True)
        out_ref[...] = (res - mean) * lax.rsqrt(var_r + LN_EPS) * lg_ref[...] + lb_ref[...]

    pending = [functools.partial(project_chunk, j) for j in range(n_pc)] + [norm_previous]

    def emit_projection_work():
        if pending:
            pending.pop(0)()

    emit_projection_work()

    r = lax.broadcasted_iota(jnp.int32, (L, L), 0)
    s = lax.broadcasted_iota(jnp.int32, (L, L), 1)
    causal = s <= r
    shifts = [jnp.where(r - s == d, 1.0, 0.0).astype(BF16) for d in range(1, CONV_W)]

    def conv_silu(x_ref, halo_ref, lo):
        x = x_ref[...]
        halo = jnp.where(c > 0, halo_ref[...].astype(F32), 0.0)
        w = cw_ref[:, lo:lo + H_M * DK_M]
        y = cb_ref[:, lo:lo + H_M * DK_M] + w[CONV_W - 1:CONV_W] * x.astype(F32)
        top = jnp.zeros((SUBLANES, H_M * DK_M), F32)
        for d in range(1, CONV_W):
            wd = w[CONV_W - 1 - d:CONV_W - d]
            y = y + wd * jnp.dot(shifts[d - 1], x, preferred_element_type=F32)
            top = top + wd * jnp.concatenate(
                [halo[HALO - d:HALO], jnp.zeros((SUBLANES - d, H_M * DK_M), F32)], axis=0)
        y = jnp.concatenate([y[0:SUBLANES] + top, y[SUBLANES:]], axis=0)
        return y * jax.nn.sigmoid(y)

    q_all = conv_silu(q_ref, qh_ref, 0)
    emit_projection_work()
    k_all = conv_silu(k_ref, kh_ref, H_M * DK_M) * (DK_M ** -0.5)
    emit_projection_work()

    gc = g_ref[...] + gbr_ref[...]
    gr = gt_ref[S_IM:S_IM + 2 * H_M, :] + gbc_ref[S_IM:S_IM + 2 * H_M, :]
    tri_l = jnp.where(causal, 1.0, 0.0).astype(BF16)
    tri_u = jnp.where(r <= s, 1.0, 0.0).astype(BF16)
    b_cols = _split_dot(tri_l, _log_sigmoid(gc) * LOG2E)
    lf_rows = _log_sigmoid(gr) * LOG2E
    b_rows = jnp.dot(lf_rows.astype(BF16), tri_u, preferred_element_type=F32) \
        + jnp.dot((lf_rows - lf_rows.astype(BF16).astype(F32)).astype(BF16), tri_u,
                  preferred_element_type=F32)
    gc = gc * LOG2E
    gr = gr * LOG2E
    ones = jnp.ones((L, LANES), BF16)

    for h in range(H_M):
        emit_projection_work()
        q = q_all[:, h * DK_M:(h + 1) * DK_M]
        k = k_all[:, h * DK_M:(h + 1) * DK_M]
        v = jnp.concatenate([v_ref[:, h * DV_M:(h + 1) * DV_M], ones], axis=1)
        qb = q.astype(BF16)
        b_c = b_cols[:, S_FM + h:S_FM + h + 1]
        i_c = gc[:, S_IM + h:S_IM + h + 1]
        b_r = b_rows[H_M + h:H_M + h + 1, :]
        i_r = gr[h:h + 1, :]
        m_prev = m_ref[h]
        ct = ct_ref[h]

        log_d = jnp.where(causal, b_c - b_r + i_r, -jnp.inf)
        g = b_c + m_prev
        m_t = jnp.maximum(jnp.max(log_d, axis=-1, keepdims=True), g)
        qk = lax.dot_general(qb, k.astype(BF16), _NT, preferred_element_type=F32)
        s_mat = qk * jnp.exp2(log_d - m_t)
        inter = jnp.exp2(g - m_t)
        num = jnp.dot(s_mat.astype(BF16), v, preferred_element_type=F32) \
            + inter * jnp.dot(qb, ct.astype(BF16), preferred_element_type=F32)
        den = jnp.maximum(jnp.abs(num[:, DV_M:]), jnp.exp2(-m_t))
        hh = num[:, 0:DV_M] / jnp.concatenate([den] * (DV_M // LANES), axis=1)

        emit_projection_work()
        b_last = b_c[L - 1:L, :]
        a_r = b_last - b_r + i_r
        m_new = jnp.maximum(b_last + m_prev, jnp.max(a_r, axis=-1, keepdims=True))
        decay = jnp.exp2(b_last + m_prev - m_new)
        wgt_c = jnp.exp2(b_last - b_c + i_c - m_new)
        kw = k * wgt_c
        ct_ref[h] = decay * ct + jnp.dot(kw.T.astype(BF16), v, preferred_element_type=F32)
        m_ref[h] = m_new

        mu = jnp.mean(hh, axis=-1, keepdims=True)
        var = jnp.mean(jnp.square(hh - mu), axis=-1, keepdims=True)
        hn = (hh - mu) * lax.rsqrt(var + LN_EPS) * ng_ref[:, h * DV_M:(h + 1) * DV_M]
        og = o_ref[:, h * DV_M:(h + 1) * DV_M].astype(F32)
        zg = z_ref[:, h * DV_M:(h + 1) * DV_M].astype(F32)
        y_ref[:, h * DV_M:(h + 1) * DV_M] = (hn * jax.nn.sigmoid(og) * (zg * jax.nn.sigmoid(zg))).astype(BF16)

    assert not pending
    yp_ref[...] = y_ref[...]


def _mlstm_out(main, small, gate_t, conv_w, conv_b, gb_row, gb_col, norm_g, ya, x2d, w_out, ln_g, ln_b, B, T):
    nc = T // L_M
    hb = L_M // HALO
    qk_w = H_M * DK_M
    n_steps = B * nc

    def cur(col):
        return lambda s: (jnp.minimum(s, n_steps - 1), col)

    def prev(s):
        return (jnp.maximum(s - 1, 0), 0)

    def halo_map(col):
        return lambda s: (jnp.maximum(jnp.minimum(s, n_steps - 1) * hb - 1, 0), col)

    const = lambda s: (0, 0)
    return pl.pallas_call(
        functools.partial(_mlstm_out_kernel, n_chunks=nc, n_steps=n_steps),
        grid=(n_steps + 1,),
        in_specs=[pl.BlockSpec((L_M, qk_w), cur(C_QM // qk_w)),
                  pl.BlockSpec((L_M, qk_w), cur(C_KM // qk_w)),
                  pl.BlockSpec((HALO, qk_w), halo_map(C_QM // qk_w)),
                  pl.BlockSpec((HALO, qk_w), halo_map(C_KM // qk_w)),
                  pl.BlockSpec((L_M, W_M), cur(C_VM // W_M)),
                  pl.BlockSpec((L_M, W_M), cur(C_OM // W_M)),
                  pl.BlockSpec((L_M, W_M), cur(C_ZM // W_M)),
                  pl.BlockSpec((L_M, LANES), cur(D_C // LANES)),
                  pl.BlockSpec((LANES, L_M), lambda s: (0, jnp.minimum(s, n_steps - 1))),
                  pl.BlockSpec((CONV_W, 2 * qk_w), const),
                  pl.BlockSpec((1, 2 * qk_w), const),
                  pl.BlockSpec((1, LANES), const),
                  pl.BlockSpec((LANES, 1), const),
                  pl.BlockSpec((1, W_M), const),
                  pl.BlockSpec((L_M, W_A), prev),
                  pl.BlockSpec((L_M, D_MODEL), prev),
                  pl.BlockSpec((W_A + W_M, D_MODEL), const),
                  pl.BlockSpec((1, D_MODEL), const),
                  pl.BlockSpec((1, D_MODEL), const)],
        out_specs=pl.BlockSpec((L_M, D_MODEL), prev),
        out_shape=jax.ShapeDtypeStruct((B * T, D_MODEL), F32),
        scratch_shapes=[pltpu.VMEM((H_M, DK_M, DV_M + LANES), F32),
                        pltpu.VMEM((H_M, 1, 1), F32),
                        pltpu.VMEM((L_M, W_M), BF16),
                        pltpu.VMEM((L_M, W_M), BF16),
                        pltpu.VMEM((L_M, D_MODEL), F32)],
        compiler_params=pltpu.CompilerParams(
            dimension_semantics=("arbitrary",), vmem_limit_bytes=VMEM_LIMIT),
        name="mlstm_out",
    )(main, main, main, main, main, main, main, small, gate_t,
      conv_w, conv_b, gb_row, gb_col, norm_g, ya, x2d, w_out, ln_g, ln_b)


_W_IN_SEGS = (("q_a", W_A), ("c_kv", D_C), ("z_a", W_A), ("q_i", H_IDX * D_IDX), ("k_i", D_IDX),
              ("w_i", H_IDX), ("q_m", H_M * DK_M), ("k_m", H_M * DK_M), ("v_m", W_M), ("i_m", H_M),
              ("f_m", H_M), ("o_m", W_M), ("z_m", W_M))
_MAIN_ORDER = ("q_a", "z_a", "q_i", "q_m", "k_m", "v_m", "o_m", "z_m")
_SEG_NAMES = [name for name, _ in _W_IN_SEGS]
assert _SEG_NAMES.index("f_m") == _SEG_NAMES.index("i_m") + 1 and S_FM == S_IM + H_M


def _repack_kernel(wt_ref, main_ref, small_ref):
    src, off = {}, 0
    for name, width in _W_IN_SEGS:
        src[name] = (off, width)
        off += width
    dst = 0
    for name in _MAIN_ORDER:
        lo, width = src[name]
        main_ref[dst:dst + width, :] = wt_ref[lo:lo + width, :].astype(BF16)
        dst += width
    parts = [wt_ref[src[name][0]:src[name][0] + src[name][1], :] for name in ("c_kv", "k_i", "w_i")]
    lo = src["i_m"][0]
    parts.append(wt_ref[lo:lo + 2 * H_M, :])
    used = sum(p.shape[0] for p in parts)
    parts.append(jnp.zeros((N_SMALL - used, wt_ref.shape[1]), F32))
    small_ref[...] = jnp.concatenate(parts, axis=0).astype(BF16)


def _repack_w_in(w_in, tc=256):
    n_cols = sum(width for _, width in _W_IN_SEGS)
    wt = jnp.swapaxes(w_in, 1, 2)[0]
    return pl.pallas_call(
        _repack_kernel,
        grid=(D_MODEL // tc,),
        in_specs=[pl.BlockSpec((n_cols, tc), lambda i: (0, i))],
        out_specs=[pl.BlockSpec((N_MAIN, tc), lambda i: (0, i)),
                   pl.BlockSpec((N_SMALL, tc), lambda i: (0, i))],
        out_shape=[jax.ShapeDtypeStruct((N_MAIN, D_MODEL), BF16),
                   jax.ShapeDtypeStruct((N_SMALL, D_MODEL), BF16)],
        compiler_params=pltpu.CompilerParams(
            dimension_semantics=("arbitrary",), vmem_limit_bytes=VMEM_LIMIT),
        name="repack",
    )(wt)


def kernel(x, w_in, b_igate, b_fgate, kv_norm_g, w_uk, w_uv, idx_k_ln_g, idx_k_ln_b, rel_bias,
           conv_w, conv_b, mh_norm_g, w_out, ln_g, ln_b):
    B, T, D = x.shape
    assert D == D_MODEL and T % L_M == 0 and T % (2 * KB) == 0 and w_in.shape[0] == 1
    bias = _bias_tiles(rel_bias)
    x2d = x.reshape(B * T, D)
    w_main, w_small = _repack_w_in(w_in)
    main, small = _proj(x2d, w_main, w_small)
    ckv_n, ckv_t, kidx_n, gate_t = _prep(small, kv_norm_g[0][None], idx_k_ln_g[0][None], idx_k_ln_b[0][None])
    w_uk_t = jnp.transpose(w_uk[0], (0, 2, 1)).astype(BF16)
    w_uv_t = jnp.transpose(w_uv[0], (0, 2, 1)).astype(BF16)
    ya = _dsa(main, gate_t, ckv_n, ckv_t, kidx_n, w_uk_t, w_uv_t, bias, B, T)
    gb = jnp.zeros((LANES,), F32).at[S_IM:S_IM + H_M].set(b_igate[0]).at[S_FM:S_FM + H_M].set(b_fgate[0])
    out = _mlstm_out(main, small, gate_t, conv_w[0], conv_b[0][None], gb[None, :], gb[:, None],
                     mh_norm_g[0][None], ya, x2d, w_out[0].astype(BF16), ln_g[0][None], ln_b[0][None], B, T)
    return out.reshape(B, T, D)
```

```python
import functools
import math

import numpy as np
import jax
import jax.numpy as jnp
from jax import lax
from jax.experimental import pallas as pl
from jax.experimental.pallas import tpu as pltpu

F32 = jnp.float32
BF16 = jnp.bfloat16

D_MODEL = 2048
W_A = 1024
DH_A = 128
H_A = 8
D_C = 256
H_IDX = 16
D_IDX = 64
TOPK = 256
W_M = 1024
H_M = 4
DV_M = 256
DK_M = 128
CONV_W = 4
N_BUCKETS = 32
MAX_DIST = 128
ALPHA = 2.0 ** 0.25
LN_EPS = 1e-5

LANES = 128
SUBLANES = 8
VMEM_LIMIT = 56 * 1024 * 1024

QB = 256
KB = 256
KI = 128
CNT_ROWS = 512
CNT_ACC = 4 * SUBLANES
L_M = 256
HALO = 16
NEG = -1e30
LOG2E = math.log2(math.e)
ONES_ROWS = 16

C_QA, C_ZA, C_QI, C_QM, C_KM, C_VM, C_OM, C_ZM = 0, 1024, 2048, 3072, 3584, 4096, 5120, 6144
N_MAIN = 7168
N_SMALL = 384
S_KI, S_WI, S_IM, S_FM = 0, 64, 80, 84


def _t5_bucket_np(rel):
    max_exact = N_BUCKETS // 2
    n = np.maximum(rel, 0)
    nf = np.maximum(n, 1).astype(np.float32)
    large = max_exact + (np.log(nf / np.float32(max_exact)) / np.float32(math.log(MAX_DIST / max_exact))
                         * np.float32(N_BUCKETS - max_exact)).astype(np.int32)
    large = np.minimum(large, N_BUCKETS - 1)
    return np.where(n < max_exact, n, large).astype(np.int32)


FAR_BUCKET = int(_t5_bucket_np(np.array(2 * KB + 1)))


def _bucket_tiles():
    i = np.arange(QB)[None, :]
    j = np.arange(KB)[:, None]
    t0 = _t5_bucket_np(i - j)
    t1 = _t5_bucket_np(i - j + KB)
    assert (t5 := _t5_bucket_np(np.arange(KB + 1, 4096))).min() == t5.max() == FAR_BUCKET
    return np.stack([t0, t1]).astype(np.int32)


def _bias_kernel(bucket_ref, rb_ref, out_ref):
    h = pl.program_id(0)
    far = rb_ref[FAR_BUCKET, h]
    for k in range(2):
        bk = bucket_ref[k]
        acc = jnp.zeros((KB, QB), F32)
        for b in range(N_BUCKETS):
            acc = jnp.where(bk == b, rb_ref[b, h] - far, acc)
        out_ref[0, k] = acc * LOG2E


def _bias_tiles(rel_bias):
    bucket = jnp.asarray(_bucket_tiles())
    return pl.pallas_call(
        _bias_kernel,
        grid=(H_A,),
        in_specs=[pl.BlockSpec((2, KB, QB), lambda h: (0, 0, 0)),
                  pl.BlockSpec(memory_space=pltpu.SMEM)],
        out_specs=pl.BlockSpec((1, 2, KB, QB), lambda h: (h, 0, 0, 0)),
        out_shape=jax.ShapeDtypeStruct((H_A, 2, KB, QB), F32),
        name="bias_tiles",
    )(bucket, rel_bias)


_NT = (((1,), (1,)), ((), ()))


def _proj_kernel(x_ref, w_ref, ws_ref, o_ref, os_ref, xb_ref):
    @pl.when(pl.program_id(1) == 0)
    def _():
        xb_ref[...] = x_ref[...].astype(BF16)
        os_ref[...] = lax.dot_general(xb_ref[...], ws_ref[...], _NT, preferred_element_type=F32)

    o_ref[...] = lax.dot_general(xb_ref[...], w_ref[...], _NT, preferred_element_type=F32).astype(BF16)


def _proj(x2d, w_main, w_small, tm=1024, tn=1024):
    M = x2d.shape[0]
    return pl.pallas_call(
        _proj_kernel,
        grid=(M // tm, N_MAIN // tn),
        in_specs=[pl.BlockSpec((tm, D_MODEL), lambda i, j: (i, 0)),
                  pl.BlockSpec((tn, D_MODEL), lambda i, j: (j, 0)),
                  pl.BlockSpec((N_SMALL, D_MODEL), lambda i, j: (0, 0))],
        out_specs=[pl.BlockSpec((tm, tn), lambda i, j: (i, j)),
                   pl.BlockSpec((tm, N_SMALL), lambda i, j: (i, 0))],
        out_shape=[jax.ShapeDtypeStruct((M, N_MAIN), BF16),
                   jax.ShapeDtypeStruct((M, N_SMALL), F32)],
        scratch_shapes=[pltpu.VMEM((tm, D_MODEL), BF16)],
        compiler_params=pltpu.CompilerParams(
            dimension_semantics=("arbitrary", "arbitrary"), vmem_limit_bytes=VMEM_LIMIT),
        name="proj",
    )(x2d, w_main, w_small)


def _prep_kernel(s_ref, kvg_ref, ig_ref, ib_ref, ckv_ref, ckvt_ref, kidx_ref, gt_ref):
    c = s_ref[:, 0:D_C]
    c = c * lax.rsqrt(jnp.mean(c * c, axis=-1, keepdims=True) + LN_EPS) * kvg_ref[...]
    ckv_ref[...] = c.astype(BF16)
    for r in range(ckvt_ref.shape[0]):
        ckvt_ref[r, 0:D_C, :] = c[r * KB:(r + 1) * KB, :].T.astype(BF16)
        ckvt_ref[r, D_C:D_C + ONES_ROWS, :] = jnp.ones((ONES_ROWS, KB), BF16)
    tile = s_ref[:, D_C:D_C + LANES]
    k = tile[:, S_KI:S_KI + D_IDX]
    mu = jnp.mean(k, axis=-1, keepdims=True)
    var = jnp.mean(jnp.square(k - mu), axis=-1, keepdims=True)
    kidx_ref[...] = ((k - mu) * lax.rsqrt(var + LN_EPS) * ig_ref[...] + ib_ref[...]).astype(BF16)
    gt_ref[...] = tile.T


def _prep(small, kv_g, idx_g, idx_b, tm=1024):
    M = small.shape[0]
    return pl.pallas_call(
        _prep_kernel,
        grid=(M // tm,),
        in_specs=[pl.BlockSpec((tm, N_SMALL), lambda i: (i, 0)),
                  pl.BlockSpec((1, D_C), lambda i: (0, 0)),
                  pl.BlockSpec((1, D_IDX), lambda i: (0, 0)),
                  pl.BlockSpec((1, D_IDX), lambda i: (0, 0))],
        out_specs=[pl.BlockSpec((tm, D_C), lambda i: (i, 0)),
                   pl.BlockSpec((tm // KB, D_C + ONES_ROWS, KB), lambda i: (i, 0, 0)),
                   pl.BlockSpec((tm, D_IDX), lambda i: (i, 0)),
                   pl.BlockSpec((LANES, tm), lambda i: (0, i))],
        out_shape=[jax.ShapeDtypeStruct((M, D_C), BF16),
                   jax.ShapeDtypeStruct((M // KB, D_C + ONES_ROWS, KB), BF16),
                   jax.ShapeDtypeStruct((M, D_IDX), BF16),
                   jax.ShapeDtypeStruct((LANES, M), F32)],
        name="prep",
    )(small, kv_g, idx_g, idx_b)


def _key_to_float(key):
    bits = jnp.where(key < 0, key ^ jnp.int32(0x7FFFFFFF), key)
    return lax.bitcast_convert_type(bits, F32)


def _dsa_kernel(qa_ref, za_ref, qi_ref, gt_ref, ckv_ref, ckvt_ref, kidx_ref, wukt_ref, wuvt_ref, bias_ref,
                y_ref, qall_ref, qr_ref, ha_ref, hb_ref, sc_ref, mb_ref, sa_ref, sb_ref, pa_ref, pb_ref,
                ta_ref, tb_ref, m_ref, al_ref, acc_ref, w_ref, cand_ref, thr_ref, cut_ref, cge_ref):
    qi = pl.program_id(1)
    nkb = qi + 1
    last_kb = mb_ref.shape[0] - 1

    for h in range(H_A):
        ql = lax.dot_general(wukt_ref[h], qa_ref[:, h * DH_A:(h + 1) * DH_A], (((1,), (1,)), ((), ())),
                             preferred_element_type=F32)
        qall_ref[h // (H_A // 2), :, (h % (H_A // 2)) * QB:(h % (H_A // 2) + 1) * QB] = (
            ql * (DH_A ** -0.5 * LOG2E)).astype(BF16)


    @pl.when(qi * QB < TOPK)
    def _():
        krow = lax.broadcasted_iota(jnp.int32, (KB, QB), 0)
        qcol = lax.broadcasted_iota(jnp.int32, (KB, QB), 1)
        mb_ref[0] = jnp.where(krow <= qcol, 0.0, NEG).astype(F32)

    @pl.when(qi * QB >= TOPK)
    def _():
        for h in range(H_IDX):
            qr_ref[h * QB:(h + 1) * QB, :] = qi_ref[:, h * D_IDX:(h + 1) * D_IDX]
        w_ref[...] = gt_ref[S_WI:S_WI + H_IDX, :] * ((D_IDX ** -0.5) * (H_IDX ** -0.5))
        krow = lax.broadcasted_iota(jnp.int32, (KI, LANES), 0)
        qcol = lax.broadcasted_iota(jnp.int32, (KI, LANES), 1)
        n_ki = nkb * (KB // KI)
        last_ki = sc_ref.shape[0] // KI - 1

        def head_dots(ki, dst_ref):
            k = kidx_ref[pl.ds(pl.multiple_of(jnp.minimum(ki, last_ki) * KI, KI), KI), :]
            dst_ref[...] = lax.dot_general(k, qr_ref[...], (((1,), (1,)), ((), ())),
                                           preferred_element_type=F32)

        def reduce_heads(src_ref, ki):
            for g in range(QB // LANES):
                lanes = slice(g * LANES, (g + 1) * LANES)
                acc = jnp.zeros((KI, LANES), F32)
                for h in range(H_IDX):
                    acc = acc + (jnp.maximum(src_ref[:, h * QB + g * LANES:h * QB + (g + 1) * LANES], 0.0)
                                 * w_ref[h:h + 1, lanes])
                sc_ref[pl.ds(pl.multiple_of(ki * KI, KI), KI), lanes] = jnp.where(
                    krow + (ki * KI - qi * QB - g * LANES) <= qcol, acc, -jnp.inf)

        pad_kb = jnp.minimum(nkb, last_kb)
        sc_ref[pl.ds(pl.multiple_of(pad_kb * KB, KB), KB), :] = jnp.full((KB, QB), -jnp.inf, F32)

        head_dots(0, ha_ref)

        def sc_body(j, carry):
            head_dots(2 * j + 1, hb_ref)
            reduce_heads(ha_ref, 2 * j)
            head_dots(2 * j + 2, ha_ref)
            reduce_heads(hb_ref, 2 * j + 1)
            return carry
        lax.fori_loop(0, n_ki // 2, sc_body, 0)

        n_cnt = (nkb * KB + CNT_ROWS - 1) // CNT_ROWS

        def count_where(pred, steps=None):
            def body(c, acc):
                parts = []
                for g in range(QB // LANES):
                    lanes = slice(g * LANES, (g + 1) * LANES)
                    if steps is None:
                        blk = sc_ref[pl.ds(pl.multiple_of(c * CNT_ROWS, CNT_ROWS), CNT_ROWS), lanes]
                    else:
                        blk = sc_ref[c * CNT_ROWS:(c + 1) * CNT_ROWS, lanes]
                    hit = jnp.where(pred(blk, c * CNT_ROWS, lanes), 1.0, 0.0).astype(F32)
                    parts.append(jnp.sum(hit.reshape(CNT_ROWS // CNT_ACC, CNT_ACC, LANES), axis=0))
                return acc + jnp.concatenate(parts, axis=1)
            acc = jnp.zeros((CNT_ACC, QB), F32)
            if steps is None:
                acc = lax.fori_loop(0, n_cnt, body, acc)
            else:
                for c in range(steps):
                    acc = body(c, acc)
            return jnp.sum(acc, axis=0, keepdims=True)

        def search(steps):
            def bit_body(i, carry):
                u, c_ge = carry
                trial = u | lax.shift_left(jnp.int32(1), 31 - i)
                cand_ref[...] = _key_to_float(trial ^ jnp.int32(-2 ** 31))
                cnt = count_where(lambda blk, row0, lanes: blk >= cand_ref[:, lanes], steps)
                ok = cnt >= float(TOPK)
                return jnp.where(ok, trial, u), jnp.where(ok, cnt, c_ge)
            u, c_ge = lax.fori_loop(0, 32, bit_body, (jnp.zeros((1, QB), jnp.int32),
                                                      jnp.full((1, QB), float(sc_ref.shape[0]), F32)))
            thr_ref[...] = _key_to_float(u ^ jnp.int32(-2 ** 31))
            cge_ref[...] = c_ge

        for steps in range(1, sc_ref.shape[0] // CNT_ROWS + 1):
            pl.when(n_cnt == steps)(functools.partial(search, steps))
        has_ties = jnp.max(cge_ref[...]) > float(TOPK)

        @pl.when(jnp.logical_not(has_ties))
        def _():
            def mb_body(kb, carry):
                blk = sc_ref[pl.ds(pl.multiple_of(kb * KB, KB), KB), :]
                mb_ref[kb] = jnp.where(blk >= thr_ref[...], 0.0, NEG).astype(F32)
                return carry
            lax.fori_loop(0, nkb, mb_body, 0)

        @pl.when(has_ties)
        def _():
            c_gt = count_where(lambda blk, row0, lanes: blk > thr_ref[:, lanes])
            need = float(TOPK) - c_gt
            rows = lax.broadcasted_iota(jnp.int32, (CNT_ROWS, LANES), 0)
            n_bits = (sc_ref.shape[0] - 1).bit_length()

            def idx_body(i, cut):
                trial = cut | lax.shift_left(jnp.int32(1), n_bits - 1 - i)
                cut_ref[...] = trial
                before = count_where(lambda blk, row0, lanes: (blk == thr_ref[:, lanes])
                                     & (rows + row0 < cut_ref[:, lanes]))
                return jnp.where(before < need, trial, cut)
            cut_ref[...] = lax.fori_loop(0, n_bits, idx_body, jnp.zeros((1, QB), jnp.int32))

            def mb_body(kb, carry):
                for g in range(QB // LANES):
                    lanes = slice(g * LANES, (g + 1) * LANES)
                    blk = sc_ref[pl.ds(pl.multiple_of(kb * KB, KB), KB), lanes]
                    thr = thr_ref[:, lanes]
                    keep = (blk > thr) | ((blk == thr) & (rows[0:KB] + kb * KB <= cut_ref[:, lanes]))
                    mb_ref[kb, :, lanes] = jnp.where(keep, 0.0, NEG).astype(F32)
                return carry
            lax.fori_loop(0, nkb, mb_body, 0)

    m_ref[...] = jnp.full(m_ref.shape, NEG, F32)
    acc_ref[...] = jnp.zeros(acc_ref.shape, F32)

    hh = H_A // 2
    s_refs, p_refs, t_refs = (sa_ref, sb_ref), (pa_ref, pb_ref), (ta_ref, tb_ref)
    pb_ref[...] = jnp.zeros(pb_ref.shape, BF16)
    al_ref[...] = jnp.ones(al_ref.shape, F32)

    def logits(kb, half):
        kv = ckv_ref[pl.ds(pl.multiple_of(jnp.minimum(kb, last_kb) * KB, KB), KB), :]
        s_refs[half][...] = jnp.dot(kv, qall_ref[half], preferred_element_type=F32)

    def softmax(kb, half, with_bias):
        tile = jnp.clip(qi - kb, 0, 1)
        for j in range(hh):
            for g in range(QB // LANES):
                lanes = slice(g * LANES, (g + 1) * LANES)
                cols = slice(j * QB + g * LANES, j * QB + (g + 1) * LANES)
                x = s_refs[half][:, cols] + mb_ref[kb, :, lanes]
                if with_bias:
                    x = x + bias_ref[half * hh + j, tile, :, lanes]
                m_prev = m_ref[half, :, cols]
                m_blk = jnp.max(x.reshape(KB // CNT_ACC, CNT_ACC, LANES), axis=0)
                m_new = jnp.maximum(m_prev, jnp.max(m_blk, axis=0, keepdims=True))
                al_ref[half, :, cols] = jnp.exp2(m_prev - m_new)
                m_ref[half, :, cols] = m_new
                p_refs[half][:, cols] = jnp.exp2(x - m_new).astype(BF16)

    def accumulate(kb, half):
        t_refs[half][...] = jnp.dot(ckvt_ref[jnp.maximum(kb, 0)], p_refs[half][...],
                                    preferred_element_type=F32)
        acc_ref[half] = acc_ref[half] * al_ref[half] + t_refs[half][...]

    def sweep(first_kb, end_kb, with_bias):
        def body(kb, carry):
            logits(kb, 1)
            softmax(kb, 0, with_bias)
            accumulate(kb - 1, 1)
            logits(kb + 1, 0)
            softmax(kb, 1, with_bias)
            accumulate(kb, 0)
            return carry
        lax.fori_loop(first_kb, end_kb, body, 0)

    n_far = jnp.maximum(qi - 1, 0)
    logits(0, 0)
    sweep(0, n_far, False)
    sweep(n_far, nkb, True)
    accumulate(nkb - 1, 1)

    for h in range(H_A):
        half, cols = h // hh, slice((h % hh) * QB, (h % hh + 1) * QB)
        ya_t = jnp.dot(wuvt_ref[h], acc_ref[half, 0:D_C, cols].astype(BF16), preferred_element_type=F32)
        ya = (ya_t / acc_ref[half, D_C:D_C + 1, cols]).T
        z = za_ref[:, h * DH_A:(h + 1) * DH_A].astype(F32)
        y_ref[:, h * DH_A:(h + 1) * DH_A] = (ya * (z * jax.nn.sigmoid(z))).astype(BF16)


def _dsa(main, gate_t, ckv_n, ckv_t, kidx_n, w_uk_t, w_uv_t, bias, B, T):
    nq = T // QB
    return pl.pallas_call(
        _dsa_kernel,
        grid=(B, nq),
        in_specs=[pl.BlockSpec((QB, W_A), lambda b, q: (b * nq + q, C_QA // W_A)),
                  pl.BlockSpec((QB, W_A), lambda b, q: (b * nq + q, C_ZA // W_A)),
                  pl.BlockSpec((QB, H_IDX * D_IDX), lambda b, q: (b * nq + q, C_QI // (H_IDX * D_IDX))),
                  pl.BlockSpec((LANES, QB), lambda b, q: (0, b * nq + q)),
                  pl.BlockSpec((T, D_C), lambda b, q: (b, 0)),
                  pl.BlockSpec((T // KB, D_C + ONES_ROWS, KB), lambda b, q: (b, 0, 0)),
                  pl.BlockSpec((T, D_IDX), lambda b, q: (b, 0)),
                  pl.BlockSpec((H_A, D_C, DH_A), lambda b, q: (0, 0, 0)),
                  pl.BlockSpec((H_A, DH_A, D_C), lambda b, q: (0, 0, 0)),
                  pl.BlockSpec((H_A, 2, KB, QB), lambda b, q: (0, 0, 0, 0))],
        out_specs=pl.BlockSpec((QB, W_A), lambda b, q: (b * nq + q, 0)),
        out_shape=jax.ShapeDtypeStruct((B * T, W_A), BF16),
        scratch_shapes=[pltpu.VMEM((2, D_C, H_A // 2 * QB), BF16),
                        pltpu.VMEM((H_IDX * QB, D_IDX), BF16),
                        pltpu.VMEM((KI, H_IDX * QB), F32),
                        pltpu.VMEM((KI, H_IDX * QB), F32),
                        pltpu.VMEM((T, QB), F32),
                        pltpu.VMEM((T // KB, KB, QB), F32),
                        pltpu.VMEM((KB, H_A // 2 * QB), F32),
                        pltpu.VMEM((KB, H_A // 2 * QB), F32),
                        pltpu.VMEM((KB, H_A // 2 * QB), BF16),
                        pltpu.VMEM((KB, H_A // 2 * QB), BF16),
                        pltpu.VMEM((D_C + ONES_ROWS, H_A // 2 * QB), F32),
                        pltpu.VMEM((D_C + ONES_ROWS, H_A // 2 * QB), F32),
                        pltpu.VMEM((2, 1, H_A // 2 * QB), F32),
                        pltpu.VMEM((2, 1, H_A // 2 * QB), F32),
                        pltpu.VMEM((2, D_C + ONES_ROWS, H_A // 2 * QB), F32),
                        pltpu.VMEM((H_IDX, QB), F32),
                        pltpu.VMEM((1, QB), F32),
                        pltpu.VMEM((1, QB), F32),
                        pltpu.VMEM((1, QB), jnp.int32),
                        pltpu.VMEM((1, QB), F32)],
        compiler_params=pltpu.CompilerParams(
            dimension_semantics=("arbitrary", "arbitrary"), vmem_limit_bytes=VMEM_LIMIT),
        name="dsa",
    )(main, main, main, gate_t, ckv_n, ckv_t, kidx_n, w_uk_t, w_uv_t, bias)


def _split_dot(tri, x):
    hi = x.astype(BF16)
    lo = (x - hi.astype(F32)).astype(BF16)
    return jnp.dot(tri, hi, preferred_element_type=F32) + jnp.dot(tri, lo, preferred_element_type=F32)


def _log_sigmoid(x):
    return jnp.minimum(x, 0.0) - jnp.log1p(jnp.exp(-jnp.abs(x)))


def _mlstm_out_kernel(q_ref, k_ref, qh_ref, kh_ref, v_ref, o_ref, z_ref, g_ref, gt_ref,
                      cw_ref, cb_ref, gbr_ref, gbc_ref, ng_ref, ya_ref, x_ref, w_ref, lg_ref, lb_ref,
                      out_ref, ct_ref, m_ref, y_ref, yp_ref, p_ref, *, n_chunks, n_steps):
    step = pl.program_id(0)
    c = jnp.minimum(step, n_steps - 1) % n_chunks
    L = L_M

    @pl.when(step == 0)
    def _():
        yp_ref[...] = jnp.zeros(yp_ref.shape, BF16)

    @pl.when(c == 0)
    def _():
        ct_ref[...] = jnp.zeros(ct_ref.shape, F32)
        m_ref[...] = jnp.zeros(m_ref.shape, F32)

    n_pc = 2 * H_M
    pw = D_MODEL // n_pc

    def project_chunk(j):
        cols = slice(j * pw, (j + 1) * pw)
        p_ref[:, cols] = (jnp.dot(ya_ref[...], w_ref[0:W_A, cols], preferred_element_type=F32)
                          + jnp.dot(yp_ref[...], w_ref[W_A:W_A + W_M, cols], preferred_element_type=F32))

    def norm_previous():
        res = ALPHA * x_ref[...] + p_ref[...]
        mean = jnp.mean(res, axis=-1, keepdims=True)
        var_r = jnp.mean(jnp.square(res - mean), axis=-1, keepdims=True)
        out_ref[...] = (res - mean) * lax.rsqrt(var_r + LN_EPS) * lg_ref[...] + lb_ref[...]

    pending = [functools.partial(project_chunk, j) for j in range(n_pc)] + [norm_previous]

    def emit_projection_work():
        if pending:
            pending.pop(0)()

    emit_projection_work()

    r = lax.broadcasted_iota(jnp.int32, (L, L), 0)
    s = lax.broadcasted_iota(jnp.int32, (L, L), 1)
    causal = s <= r
    shifts = [jnp.where(r - s == d, 1.0, 0.0).astype(BF16) for d in range(1, CONV_W)]

    def conv_silu(x_ref, halo_ref, lo):
        x = x_ref[...]
        halo = jnp.where(c > 0, halo_ref[...].astype(F32), 0.0)
        w = cw_ref[:, lo:lo + H_M * DK_M]
        y = cb_ref[:, lo:lo + H_M * DK_M] + w[CONV_W - 1:CONV_W] * x.astype(F32)
        top = jnp.zeros((SUBLANES, H_M * DK_M), F32)
        for d in range(1, CONV_W):
            wd = w[CONV_W - 1 - d:CONV_W - d]
            y = y + wd * jnp.dot(shifts[d - 1], x, preferred_element_type=F32)
            top = top + wd * jnp.concatenate(
                [halo[HALO - d:HALO], jnp.zeros((SUBLANES - d, H_M * DK_M), F32)], axis=0)
        y = jnp.concatenate([y[0:SUBLANES] + top, y[SUBLANES:]], axis=0)
        return y * jax.nn.sigmoid(y)

    q_all = conv_silu(q_ref, qh_ref, 0)
    emit_projection_work()
    k_all = conv_silu(k_ref, kh_ref, H_M * DK_M) * (DK_M ** -0.5)
    emit_projection_work()

    gc = g_ref[...] + gbr_ref[...]
    gr = gt_ref[S_IM:S_IM + 2 * H_M, :] + gbc_ref[S_IM:S_IM + 2 * H_M, :]
    tri_l = jnp.where(causal, 1.0, 0.0).astype(BF16)
    tri_u = jnp.where(r <= s, 1.0, 0.0).astype(BF16)
    b_cols = _split_dot(tri_l, _log_sigmoid(gc) * LOG2E)
    lf_rows = _log_sigmoid(gr) * LOG2E
    b_rows = jnp.dot(lf_rows.astype(BF16), tri_u, preferred_element_type=F32) \
        + jnp.dot((lf_rows - lf_rows.astype(BF16).astype(F32)).astype(BF16), tri_u,
                  preferred_element_type=F32)
    gc = gc * LOG2E
    gr = gr * LOG2E
    ones = jnp.ones((L, LANES), BF16)
    emit_projection_work()

    for h in range(H_M):
        emit_projection_work()
        q = q_all[:, h * DK_M:(h + 1) * DK_M]
        k = k_all[:, h * DK_M:(h + 1) * DK_M]
        v = jnp.concatenate([v_ref[:, h * DV_M:(h + 1) * DV_M], ones], axis=1)
        qb = q.astype(BF16)
        b_c = b_cols[:, S_FM + h:S_FM + h + 1]
        i_c = gc[:, S_IM + h:S_IM + h + 1]
        b_r = b_rows[H_M + h:H_M + h + 1, :]
        i_r = gr[h:h + 1, :]
        m_prev = m_ref[h]
        ct = ct_ref[h]

        log_d = jnp.where(causal, b_c - b_r + i_r, -jnp.inf)
        g = b_c + m_prev
        m_t = jnp.maximum(jnp.max(log_d, axis=-1, keepdims=True), g)
        qk = lax.dot_general(qb, k.astype(BF16), _NT, preferred_element_type=F32)
        s_mat = qk * jnp.exp2(log_d - m_t)
        inter = jnp.exp2(g - m_t)
        num = jnp.dot(s_mat.astype(BF16), v, preferred_element_type=F32) \
            + inter * jnp.dot(qb, ct.astype(BF16), preferred_element_type=F32)
        den = jnp.maximum(jnp.abs(num[:, DV_M:]), jnp.exp2(-m_t))
        hh = num[:, 0:DV_M] / jnp.concatenate([den] * (DV_M // LANES), axis=1)

        emit_projection_work()
        b_last = b_c[L - 1:L, :]
        a_r = b_last - b_r + i_r
        m_new = jnp.maximum(b_last + m_prev, jnp.max(a_r, axis=-1, keepdims=True))
        decay = jnp.exp2(b_last + m_prev - m_new)
        wgt_c = jnp.exp2(b_last - b_c + i_c - m_new)
        kw = k * wgt_c
        ct_ref[h] = decay * ct + jnp.dot(kw.T.astype(BF16), v, preferred_element_type=F32)
        m_ref[h] = m_new

        mu = jnp.mean(hh, axis=-1, keepdims=True)
        var = jnp.mean(jnp.square(hh - mu), axis=-1, keepdims=True)
        hn = (hh - mu) * lax.rsqrt(var + LN_EPS) * ng_ref[:, h * DV_M:(h + 1) * DV_M]
        og = o_ref[:, h * DV_M:(h + 1) * DV_M].astype(F32)
        zg = z_ref[:, h * DV_M:(h + 1) * DV_M].astype(F32)
        y_ref[:, h * DV_M:(h + 1) * DV_M] = (hn * jax.nn.sigmoid(og) * (zg * jax.nn.sigmoid(zg))).astype(BF16)

    assert not pending
    yp_ref[...] = y_ref[...]


def _mlstm_out(main, small, gate_t, conv_w, conv_b, gb_row, gb_col, norm_g, ya, x2d, w_out, ln_g, ln_b, B, T):
    nc = T // L_M
    hb = L_M // HALO
    qk_w = H_M * DK_M
    n_steps = B * nc

    def cur(col):
        return lambda s: (jnp.minimum(s, n_steps - 1), col)

    def prev(s):
        return (jnp.maximum(s - 1, 0), 0)

    def halo_map(col):
        return lambda s: (jnp.maximum(jnp.minimum(s, n_steps - 1) * hb - 1, 0), col)

    const = lambda s: (0, 0)
    return pl.pallas_call(
        functools.partial(_mlstm_out_kernel, n_chunks=nc, n_steps=n_steps),
        grid=(n_steps + 1,),
        in_specs=[pl.BlockSpec((L_M, qk_w), cur(C_QM // qk_w)),
                  pl.BlockSpec((L_M, qk_w), cur(C_KM // qk_w)),
                  pl.BlockSpec((HALO, qk_w), halo_map(C_QM // qk_w)),
                  pl.BlockSpec((HALO, qk_w), halo_map(C_KM // qk_w)),
                  pl.BlockSpec((L_M, W_M), cur(C_VM // W_M)),
                  pl.BlockSpec((L_M, W_M), cur(C_OM // W_M)),
                  pl.BlockSpec((L_M, W_M), cur(C_ZM // W_M)),
                  pl.BlockSpec((L_M, LANES), cur(D_C // LANES)),
                  pl.BlockSpec((LANES, L_M), lambda s: (0, jnp.minimum(s, n_steps - 1))),
                  pl.BlockSpec((CONV_W, 2 * qk_w), const),
                  pl.BlockSpec((1, 2 * qk_w), const),
                  pl.BlockSpec((1, LANES), const),
                  pl.BlockSpec((LANES, 1), const),
                  pl.BlockSpec((1, W_M), const),
                  pl.BlockSpec((L_M, W_A), prev),
                  pl.BlockSpec((L_M, D_MODEL), prev),
                  pl.BlockSpec((W_A + W_M, D_MODEL), const),
                  pl.BlockSpec((1, D_MODEL), const),
                  pl.BlockSpec((1, D_MODEL), const)],
        out_specs=pl.BlockSpec((L_M, D_MODEL), prev),
        out_shape=jax.ShapeDtypeStruct((B * T, D_MODEL), F32),
        scratch_shapes=[pltpu.VMEM((H_M, DK_M, DV_M + LANES), F32),
                        pltpu.VMEM((H_M, 1, 1), F32),
                        pltpu.VMEM((L_M, W_M), BF16),
                        pltpu.VMEM((L_M, W_M), BF16),
                        pltpu.VMEM((L_M, D_MODEL), F32)],
        compiler_params=pltpu.CompilerParams(
            dimension_semantics=("arbitrary",), vmem_limit_bytes=VMEM_LIMIT),
        name="mlstm_out",
    )(main, main, main, main, main, main, main, small, gate_t,
      conv_w, conv_b, gb_row, gb_col, norm_g, ya, x2d, w_out, ln_g, ln_b)


_W_IN_SEGS = (("q_a", W_A), ("c_kv", D_C), ("z_a", W_A), ("q_i", H_IDX * D_IDX), ("k_i", D_IDX),
              ("w_i", H_IDX), ("q_m", H_M * DK_M), ("k_m", H_M * DK_M), ("v_m", W_M), ("i_m", H_M),
              ("f_m", H_M), ("o_m", W_M), ("z_m", W_M))
_MAIN_ORDER = ("q_a", "z_a", "q_i", "q_m", "k_m", "v_m", "o_m", "z_m")
_SEG_NAMES = [name for name, _ in _W_IN_SEGS]
assert _SEG_NAMES.index("f_m") == _SEG_NAMES.index("i_m") + 1 and S_FM == S_IM + H_M


def _repack_kernel(wt_ref, main_ref, small_ref):
    src, off = {}, 0
    for name, width in _W_IN_SEGS:
        src[name] = (off, width)
        off += width
    dst = 0
    for name in _MAIN_ORDER:
        lo, width = src[name]
        main_ref[dst:dst + width, :] = wt_ref[lo:lo + width, :].astype(BF16)
        dst += width
    parts = [wt_ref[src[name][0]:src[name][0] + src[name][1], :] for name in ("c_kv", "k_i", "w_i")]
    lo = src["i_m"][0]
    parts.append(wt_ref[lo:lo + 2 * H_M, :])
    used = sum(p.shape[0] for p in parts)
    parts.append(jnp.zeros((N_SMALL - used, wt_ref.shape[1]), F32))
    small_ref[...] = jnp.concatenate(parts, axis=0).astype(BF16)


def _repack_w_in(w_in, tc=256):
    n_cols = sum(width for _, width in _W_IN_SEGS)
    wt = jnp.swapaxes(w_in, 1, 2)[0]
    return pl.pallas_call(
        _repack_kernel,
        grid=(D_MODEL // tc,),
        in_specs=[pl.BlockSpec((n_cols, tc), lambda i: (0, i))],
        out_specs=[pl.BlockSpec((N_MAIN, tc), lambda i: (0, i)),
                   pl.BlockSpec((N_SMALL, tc), lambda i: (0, i))],
        out_shape=[jax.ShapeDtypeStruct((N_MAIN, D_MODEL), BF16),
                   jax.ShapeDtypeStruct((N_SMALL, D_MODEL), BF16)],
        compiler_params=pltpu.CompilerParams(
            dimension_semantics=("arbitrary",), vmem_limit_bytes=VMEM_LIMIT),
        name="repack",
    )(wt)


def kernel(x, w_in, b_igate, b_fgate, kv_norm_g, w_uk, w_uv, idx_k_ln_g, idx_k_ln_b, rel_bias,
           conv_w, conv_b, mh_norm_g, w_out, ln_g, ln_b):
    B, T, D = x.shape
    assert D == D_MODEL and T % L_M == 0 and T % (2 * KB) == 0 and w_in.shape[0] == 1
    bias = _bias_tiles(rel_bias)
    x2d = x.reshape(B * T, D)
    w_main, w_small = _repack_w_in(w_in)
    main, small = _proj(x2d, w_main, w_small)
    ckv_n, ckv_t, kidx_n, gate_t = _prep(small, kv_norm_g[0][None], idx_k_ln_g[0][None], idx_k_ln_b[0][None])
    w_uk_t = jnp.transpose(w_uk[0], (0, 2, 1)).astype(BF16)
    w_uv_t = jnp.transpose(w_uv[0], (0, 2, 1)).astype(BF16)
    ya = _dsa(main, gate_t, ckv_n, ckv_t, kidx_n, w_uk_t, w_uv_t, bias, B, T)
    gb = jnp.zeros((LANES,), F32).at[S_IM:S_IM + H_M].set(b_igate[0]).at[S_FM:S_FM + H_M].set(b_fgate[0])
    out = _mlstm_out(main, small, gate_t, conv_w[0], conv_b[0][None], gb[None, :], gb[:, None],
                     mh_norm_g[0][None], ya, x2d, w_out[0].astype(BF16), ln_g[0][None], ln_b[0][None], B, T)
    return out.reshape(B, T, D)
```

```python
import functools
import math

import numpy as np
import jax
import jax.numpy as jnp
from jax import lax
from jax.experimental import pallas as pl
from jax.experimental.pallas import tpu as pltpu

F32 = jnp.float32
BF16 = jnp.bfloat16

D_MODEL = 2048
W_A = 1024
DH_A = 128
H_A = 8
D_C = 256
H_IDX = 16
D_IDX = 64
TOPK = 256
W_M = 1024
H_M = 4
DV_M = 256
DK_M = 128
CONV_W = 4
N_BUCKETS = 32
MAX_DIST = 128
ALPHA = 2.0 ** 0.25
LN_EPS = 1e-5

LANES = 128
SUBLANES = 8
VMEM_LIMIT = 56 * 1024 * 1024

QB = 256
KB = 256
KI = 128
CNT_ROWS = 512
CNT_ACC = 4 * SUBLANES
L_M = 256
HALO = 16
NEG = -1e30
LOG2E = math.log2(math.e)
ONES_ROWS = 16

C_QA, C_ZA, C_QI, C_QM, C_KM, C_VM, C_OM, C_ZM = 0, 1024, 2048, 3072, 3584, 4096, 5120, 6144
N_MAIN = 7168
N_SMALL = 384
S_KI, S_WI, S_IM, S_FM = 0, 64, 80, 84


def _t5_bucket_np(rel):
    max_exact = N_BUCKETS // 2
    n = np.maximum(rel, 0)
    nf = np.maximum(n, 1).astype(np.float32)
    large = max_exact + (np.log(nf / np.float32(max_exact)) / np.float32(math.log(MAX_DIST / max_exact))
                         * np.float32(N_BUCKETS - max_exact)).astype(np.int32)
    large = np.minimum(large, N_BUCKETS - 1)
    return np.where(n < max_exact, n, large).astype(np.int32)


FAR_BUCKET = int(_t5_bucket_np(np.array(2 * KB + 1)))


def _bucket_tiles():
    i = np.arange(QB)[None, :]
    j = np.arange(KB)[:, None]
    t0 = _t5_bucket_np(i - j)
    t1 = _t5_bucket_np(i - j + KB)
    assert (t5 := _t5_bucket_np(np.arange(KB + 1, 4096))).min() == t5.max() == FAR_BUCKET
    return np.stack([t0, t1]).astype(np.int32)


def _bias_kernel(bucket_ref, rb_ref, out_ref):
    h = pl.program_id(0)
    far = rb_ref[FAR_BUCKET, h]
    for k in range(2):
        bk = bucket_ref[k]
        acc = jnp.zeros((KB, QB), F32)
        for b in range(N_BUCKETS):
            acc = jnp.where(bk == b, rb_ref[b, h] - far, acc)
        out_ref[0, k] = acc * LOG2E


def _bias_tiles(rel_bias):
    bucket = jnp.asarray(_bucket_tiles())
    return pl.pallas_call(
        _bias_kernel,
        grid=(H_A,),
        in_specs=[pl.BlockSpec((2, KB, QB), lambda h: (0, 0, 0)),
                  pl.BlockSpec(memory_space=pltpu.SMEM)],
        out_specs=pl.BlockSpec((1, 2, KB, QB), lambda h: (h, 0, 0, 0)),
        out_shape=jax.ShapeDtypeStruct((H_A, 2, KB, QB), F32),
        name="bias_tiles",
    )(bucket, rel_bias)


_NT = (((1,), (1,)), ((), ()))


def _proj_kernel(x_ref, w_ref, ws_ref, o_ref, os_ref, xb_ref):
    @pl.when(pl.program_id(1) == 0)
    def _():
        xb_ref[...] = x_ref[...].astype(BF16)
        os_ref[...] = lax.dot_general(xb_ref[...], ws_ref[...], _NT, preferred_element_type=F32)

    o_ref[...] = lax.dot_general(xb_ref[...], w_ref[...], _NT, preferred_element_type=F32).astype(BF16)


def _proj(x2d, w_main, w_small, tm=1024, tn=1024):
    M = x2d.shape[0]
    return pl.pallas_call(
        _proj_kernel,
        grid=(M // tm, N_MAIN // tn),
        in_specs=[pl.BlockSpec((tm, D_MODEL), lambda i, j: (i, 0)),
                  pl.BlockSpec((tn, D_MODEL), lambda i, j: (j, 0)),
                  pl.BlockSpec((N_SMALL, D_MODEL), lambda i, j: (0, 0))],
        out_specs=[pl.BlockSpec((tm, tn), lambda i, j: (i, j)),
                   pl.BlockSpec((tm, N_SMALL), lambda i, j: (i, 0))],
        out_shape=[jax.ShapeDtypeStruct((M, N_MAIN), BF16),
                   jax.ShapeDtypeStruct((M, N_SMALL), F32)],
        scratch_shapes=[pltpu.VMEM((tm, D_MODEL), BF16)],
        compiler_params=pltpu.CompilerParams(
            dimension_semantics=("arbitrary", "arbitrary"), vmem_limit_bytes=VMEM_LIMIT),
        name="proj",
    )(x2d, w_main, w_small)


def _prep_kernel(s_ref, kvg_ref, ig_ref, ib_ref, ckv_ref, ckvt_ref, kidx_ref, gt_ref):
    c = s_ref[:, 0:D_C]
    c = c * lax.rsqrt(jnp.mean(c * c, axis=-1, keepdims=True) + LN_EPS) * kvg_ref[...]
    ckv_ref[...] = c.astype(BF16)
    for r in range(ckvt_ref.shape[0]):
        ckvt_ref[r, 0:D_C, :] = c[r * KB:(r + 1) * KB, :].T.astype(BF16)
        ckvt_ref[r, D_C:D_C + ONES_ROWS, :] = jnp.ones((ONES_ROWS, KB), BF16)
    tile = s_ref[:, D_C:D_C + LANES]
    k = tile[:, S_KI:S_KI + D_IDX]
    mu = jnp.mean(k, axis=-1, keepdims=True)
    var = jnp.mean(jnp.square(k - mu), axis=-1, keepdims=True)
    kidx_ref[...] = ((k - mu) * lax.rsqrt(var + LN_EPS) * ig_ref[...] + ib_ref[...]).astype(BF16)
    gt_ref[...] = tile.T


def _prep(small, kv_g, idx_g, idx_b, tm=1024):
    M = small.shape[0]
    return pl.pallas_call(
        _prep_kernel,
        grid=(M // tm,),
        in_specs=[pl.BlockSpec((tm, N_SMALL), lambda i: (i, 0)),
                  pl.BlockSpec((1, D_C), lambda i: (0, 0)),
                  pl.BlockSpec((1, D_IDX), lambda i: (0, 0)),
                  pl.BlockSpec((1, D_IDX), lambda i: (0, 0))],
        out_specs=[pl.BlockSpec((tm, D_C), lambda i: (i, 0)),
                   pl.BlockSpec((tm // KB, D_C + ONES_ROWS, KB), lambda i: (i, 0, 0)),
                   pl.BlockSpec((tm, D_IDX), lambda i: (i, 0)),
                   pl.BlockSpec((LANES, tm), lambda i: (0, i))],
        out_shape=[jax.ShapeDtypeStruct((M, D_C), BF16),
                   jax.ShapeDtypeStruct((M // KB, D_C + ONES_ROWS, KB), BF16),
                   jax.ShapeDtypeStruct((M, D_IDX), BF16),
                   jax.ShapeDtypeStruct((LANES, M), F32)],
        name="prep",
    )(small, kv_g, idx_g, idx_b)


def _key_to_float(key):
    bits = jnp.where(key < 0, key ^ jnp.int32(0x7FFFFFFF), key)
    return lax.bitcast_convert_type(bits, F32)


def _dsa_kernel(qa_ref, za_ref, qi_ref, gt_ref, ckv_ref, ckvt_ref, kidx_ref, wukt_ref, wuvt_ref, bias_ref,
                y_ref, qall_ref, qr_ref, ha_ref, hb_ref, sc_ref, mb_ref, sa_ref, sb_ref, pa_ref, pb_ref,
                ta_ref, tb_ref, m_ref, al_ref, acc_ref, w_ref, cand_ref, thr_ref, cut_ref, cge_ref):
    qi = pl.program_id(1)
    nkb = qi + 1
    last_kb = mb_ref.shape[0] - 1

    for h in range(H_A):
        ql = lax.dot_general(wukt_ref[h], qa_ref[:, h * DH_A:(h + 1) * DH_A], (((1,), (1,)), ((), ())),
                             preferred_element_type=F32)
        qall_ref[h // (H_A // 2), :, (h % (H_A // 2)) * QB:(h % (H_A // 2) + 1) * QB] = (
            ql * (DH_A ** -0.5 * LOG2E)).astype(BF16)


    @pl.when(qi * QB < TOPK)
    def _():
        krow = lax.broadcasted_iota(jnp.int32, (KB, QB), 0)
        qcol = lax.broadcasted_iota(jnp.int32, (KB, QB), 1)
        mb_ref[0] = jnp.where(krow <= qcol, 0.0, NEG).astype(F32)

    @pl.when(qi * QB >= TOPK)
    def _():
        for h in range(H_IDX):
            qr_ref[h * QB:(h + 1) * QB, :] = qi_ref[:, h * D_IDX:(h + 1) * D_IDX]
        w_ref[...] = gt_ref[S_WI:S_WI + H_IDX, :] * ((D_IDX ** -0.5) * (H_IDX ** -0.5))
        krow = lax.broadcasted_iota(jnp.int32, (KI, LANES), 0)
        qcol = lax.broadcasted_iota(jnp.int32, (KI, LANES), 1)
        n_ki = nkb * (KB // KI)
        last_ki = sc_ref.shape[0] // KI - 1

        def head_dots(ki, dst_ref):
            k = kidx_ref[pl.ds(pl.multiple_of(jnp.minimum(ki, last_ki) * KI, KI), KI), :]
            dst_ref[...] = lax.dot_general(k, qr_ref[...], (((1,), (1,)), ((), ())),
                                           preferred_element_type=F32)

        def reduce_heads(src_ref, ki):
            for g in range(QB // LANES):
                lanes = slice(g * LANES, (g + 1) * LANES)
                acc = jnp.zeros((KI, LANES), F32)
                for h in range(H_IDX):
                    acc = acc + (jnp.maximum(src_ref[:, h * QB + g * LANES:h * QB + (g + 1) * LANES], 0.0)
                                 * w_ref[h:h + 1, lanes])
                sc_ref[pl.ds(pl.multiple_of(ki * KI, KI), KI), lanes] = jnp.where(
                    krow + (ki * KI - qi * QB - g * LANES) <= qcol, acc, -jnp.inf)

        pad_kb = jnp.minimum(nkb, last_kb)
        sc_ref[pl.ds(pl.multiple_of(pad_kb * KB, KB), KB), :] = jnp.full((KB, QB), -jnp.inf, F32)

        head_dots(0, ha_ref)

        def sc_body(j, carry):
            head_dots(2 * j + 1, hb_ref)
            reduce_heads(ha_ref, 2 * j)
            head_dots(2 * j + 2, ha_ref)
            reduce_heads(hb_ref, 2 * j + 1)
            return carry
        lax.fori_loop(0, n_ki // 2, sc_body, 0)

        n_cnt = (nkb * KB + CNT_ROWS - 1) // CNT_ROWS

        def count_where(pred, steps=None):
            def body(c, acc):
                parts = []
                for g in range(QB // LANES):
                    lanes = slice(g * LANES, (g + 1) * LANES)
                    if steps is None:
                        blk = sc_ref[pl.ds(pl.multiple_of(c * CNT_ROWS, CNT_ROWS), CNT_ROWS), lanes]
                    else:
                        blk = sc_ref[c * CNT_ROWS:(c + 1) * CNT_ROWS, lanes]
                    hit = jnp.where(pred(blk, c * CNT_ROWS, lanes), 1.0, 0.0).astype(F32)
                    parts.append(jnp.sum(hit.reshape(CNT_ROWS // CNT_ACC, CNT_ACC, LANES), axis=0))
                return acc + jnp.concatenate(parts, axis=1)
            acc = jnp.zeros((CNT_ACC, QB), F32)
            if steps is None:
                acc = lax.fori_loop(0, n_cnt, body, acc)
            else:
                for c in range(steps):
                    acc = body(c, acc)
            return jnp.sum(acc, axis=0, keepdims=True)

        def search(steps):
            def bit_body(i, carry):
                u, c_ge = carry
                trial = u | lax.shift_left(jnp.int32(1), 31 - i)
                cand_ref[...] = _key_to_float(trial ^ jnp.int32(-2 ** 31))
                cnt = count_where(lambda blk, row0, lanes: blk >= cand_ref[:, lanes], steps)
                ok = cnt >= float(TOPK)
                return jnp.where(ok, trial, u), jnp.where(ok, cnt, c_ge)
            u, c_ge = lax.fori_loop(0, 32, bit_body, (jnp.zeros((1, QB), jnp.int32),
                                                      jnp.full((1, QB), float(sc_ref.shape[0]), F32)))
            thr_ref[...] = _key_to_float(u ^ jnp.int32(-2 ** 31))
            cge_ref[...] = c_ge

        for steps in range(1, sc_ref.shape[0] // CNT_ROWS + 1):
            pl.when(n_cnt == steps)(functools.partial(search, steps))
        has_ties = jnp.max(cge_ref[...]) > float(TOPK)

        @pl.when(jnp.logical_not(has_ties))
        def _():
            def mb_body(kb, carry):
                blk = sc_ref[pl.ds(pl.multiple_of(kb * KB, KB), KB), :]
                mb_ref[kb] = jnp.where(blk >= thr_ref[...], 0.0, NEG).astype(F32)
                return carry
            lax.fori_loop(0, nkb, mb_body, 0)

        @pl.when(has_ties)
        def _():
            c_gt = count_where(lambda blk, row0, lanes: blk > thr_ref[:, lanes])
            need = float(TOPK) - c_gt
            rows = lax.broadcasted_iota(jnp.int32, (CNT_ROWS, LANES), 0)
            n_bits = (sc_ref.shape[0] - 1).bit_length()

            def idx_body(i, cut):
                trial = cut | lax.shift_left(jnp.int32(1), n_bits - 1 - i)
                cut_ref[...] = trial
                before = count_where(lambda blk, row0, lanes: (blk == thr_ref[:, lanes])
                                     & (rows + row0 < cut_ref[:, lanes]))
                return jnp.where(before < need, trial, cut)
            cut_ref[...] = lax.fori_loop(0, n_bits, idx_body, jnp.zeros((1, QB), jnp.int32))

            def mb_body(kb, carry):
                for g in range(QB // LANES):
                    lanes = slice(g * LANES, (g + 1) * LANES)
                    blk = sc_ref[pl.ds(pl.multiple_of(kb * KB, KB), KB), lanes]
                    thr = thr_ref[:, lanes]
                    keep = (blk > thr) | ((blk == thr) & (rows[0:KB] + kb * KB <= cut_ref[:, lanes]))
                    mb_ref[kb, :, lanes] = jnp.where(keep, 0.0, NEG).astype(F32)
                return carry
            lax.fori_loop(0, nkb, mb_body, 0)

    m_ref[...] = jnp.full(m_ref.shape, NEG, F32)
    acc_ref[...] = jnp.zeros(acc_ref.shape, F32)

    hh = H_A // 2
    s_refs, p_refs, t_refs = (sa_ref, sb_ref), (pa_ref, pb_ref), (ta_ref, tb_ref)
    pb_ref[...] = jnp.zeros(pb_ref.shape, BF16)
    al_ref[...] = jnp.ones(al_ref.shape, F32)

    parts = 2
    hp = hh // parts

    def logits(kb, half, part):
        cols = slice(part * hp * QB, (part + 1) * hp * QB)
        kv = ckv_ref[pl.ds(pl.multiple_of(jnp.minimum(kb, last_kb) * KB, KB), KB), :]
        s_refs[half][:, cols] = jnp.dot(kv, qall_ref[half, :, cols], preferred_element_type=F32)

    def softmax(kb, half, part, with_bias):
        tile = jnp.clip(qi - kb, 0, 1)
        for j in range(part * hp, (part + 1) * hp):
            for g in range(QB // LANES):
                lanes = slice(g * LANES, (g + 1) * LANES)
                cols = slice(j * QB + g * LANES, j * QB + (g + 1) * LANES)
                x = s_refs[half][:, cols] + mb_ref[kb, :, lanes]
                if with_bias:
                    x = x + bias_ref[half * hh + j, tile, :, lanes]
                m_prev = m_ref[half, :, cols]
                m_blk = jnp.max(x.reshape(KB // CNT_ACC, CNT_ACC, LANES), axis=0)
                m_new = jnp.maximum(m_prev, jnp.max(m_blk, axis=0, keepdims=True))
                al_ref[half, :, cols] = jnp.exp2(m_prev - m_new)
                m_ref[half, :, cols] = m_new
                p_refs[half][:, cols] = jnp.exp2(x - m_new).astype(BF16)

    def accumulate(kb, half, part):
        cols = slice(part * hp * QB, (part + 1) * hp * QB)
        t_refs[half][:, cols] = jnp.dot(ckvt_ref[jnp.maximum(kb, 0)], p_refs[half][:, cols],
                                        preferred_element_type=F32)
        acc_ref[half, :, cols] = acc_ref[half, :, cols] * al_ref[half, :, cols] + t_refs[half][:, cols]

    def sweep(first_kb, end_kb, with_bias):
        def body(kb, carry):
            for part in range(parts):
                logits(kb, 1, part)
                softmax(kb, 0, part, with_bias)
                accumulate(kb - 1, 1, part)
            for part in range(parts):
                logits(kb + 1, 0, part)
                softmax(kb, 1, part, with_bias)
                accumulate(kb, 0, part)
            return carry
        lax.fori_loop(first_kb, end_kb, body, 0)

    n_far = jnp.maximum(qi - 1, 0)
    for part in range(parts):
        logits(0, 0, part)
    sweep(0, n_far, False)
    sweep(n_far, nkb, True)
    for part in range(parts):
        accumulate(nkb - 1, 1, part)

    for h in range(H_A):
        half, cols = h // hh, slice((h % hh) * QB, (h % hh + 1) * QB)
        ya_t = jnp.dot(wuvt_ref[h], acc_ref[half, 0:D_C, cols].astype(BF16), preferred_element_type=F32)
        ya = (ya_t / acc_ref[half, D_C:D_C + 1, cols]).T
        z = za_ref[:, h * DH_A:(h + 1) * DH_A].astype(F32)
        y_ref[:, h * DH_A:(h + 1) * DH_A] = (ya * (z * jax.nn.sigmoid(z))).astype(BF16)


def _dsa(main, gate_t, ckv_n, ckv_t, kidx_n, w_uk_t, w_uv_t, bias, B, T):
    nq = T // QB
    return pl.pallas_call(
        _dsa_kernel,
        grid=(B, nq),
        in_specs=[pl.BlockSpec((QB, W_A), lambda b, q: (b * nq + q, C_QA // W_A)),
                  pl.BlockSpec((QB, W_A), lambda b, q: (b * nq + q, C_ZA // W_A)),
                  pl.BlockSpec((QB, H_IDX * D_IDX), lambda b, q: (b * nq + q, C_QI // (H_IDX * D_IDX))),
                  pl.BlockSpec((LANES, QB), lambda b, q: (0, b * nq + q)),
                  pl.BlockSpec((T, D_C), lambda b, q: (b, 0)),
                  pl.BlockSpec((T // KB, D_C + ONES_ROWS, KB), lambda b, q: (b, 0, 0)),
                  pl.BlockSpec((T, D_IDX), lambda b, q: (b, 0)),
                  pl.BlockSpec((H_A, D_C, DH_A), lambda b, q: (0, 0, 0)),
                  pl.BlockSpec((H_A, DH_A, D_C), lambda b, q: (0, 0, 0)),
                  pl.BlockSpec((H_A, 2, KB, QB), lambda b, q: (0, 0, 0, 0))],
        out_specs=pl.BlockSpec((QB, W_A), lambda b, q: (b * nq + q, 0)),
        out_shape=jax.ShapeDtypeStruct((B * T, W_A), BF16),
        scratch_shapes=[pltpu.VMEM((2, D_C, H_A // 2 * QB), BF16),
                        pltpu.VMEM((H_IDX * QB, D_IDX), BF16),
                        pltpu.VMEM((KI, H_IDX * QB), F32),
                        pltpu.VMEM((KI, H_IDX * QB), F32),
                        pltpu.VMEM((T, QB), F32),
                        pltpu.VMEM((T // KB, KB, QB), F32),
                        pltpu.VMEM((KB, H_A // 2 * QB), F32),
                        pltpu.VMEM((KB, H_A // 2 * QB), F32),
                        pltpu.VMEM((KB, H_A // 2 * QB), BF16),
                        pltpu.VMEM((KB, H_A // 2 * QB), BF16),
                        pltpu.VMEM((D_C + ONES_ROWS, H_A // 2 * QB), F32),
                        pltpu.VMEM((D_C + ONES_ROWS, H_A // 2 * QB), F32),
                        pltpu.VMEM((2, 1, H_A // 2 * QB), F32),
                        pltpu.VMEM((2, 1, H_A // 2 * QB), F32),
                        pltpu.VMEM((2, D_C + ONES_ROWS, H_A // 2 * QB), F32),
                        pltpu.VMEM((H_IDX, QB), F32),
                        pltpu.VMEM((1, QB), F32),
                        pltpu.VMEM((1, QB), F32),
                        pltpu.VMEM((1, QB), jnp.int32),
                        pltpu.VMEM((1, QB), F32)],
        compiler_params=pltpu.CompilerParams(
            dimension_semantics=("arbitrary", "arbitrary"), vmem_limit_bytes=VMEM_LIMIT),
        name="dsa",
    )(main, main, main, gate_t, ckv_n, ckv_t, kidx_n, w_uk_t, w_uv_t, bias)


def _split_dot(tri, x):
    hi = x.astype(BF16)
    lo = (x - hi.astype(F32)).astype(BF16)
    return jnp.dot(tri, hi, preferred_element_type=F32) + jnp.dot(tri, lo, preferred_element_type=F32)


def _log_sigmoid(x):
    return jnp.minimum(x, 0.0) - jnp.log1p(jnp.exp(-jnp.abs(x)))


def _mlstm_out_kernel(q_ref, k_ref, qh_ref, kh_ref, v_ref, o_ref, z_ref, g_ref, gt_ref,
                      cw_ref, cb_ref, gbr_ref, gbc_ref, ng_ref, ya_ref, x_ref, w_ref, lg_ref, lb_ref,
                      out_ref, ct_ref, m_ref, y_ref, yp_ref, p_ref, *, n_chunks, n_steps):
    step = pl.program_id(0)
    c = jnp.minimum(step, n_steps - 1) % n_chunks
    L = L_M

    @pl.when(step == 0)
    def _():
        yp_ref[...] = jnp.zeros(yp_ref.shape, BF16)

    @pl.when(c == 0)
    def _():
        ct_ref[...] = jnp.zeros(ct_ref.shape, F32)
        m_ref[...] = jnp.zeros(m_ref.shape, F32)

    n_pc = 2 * H_M
    pw = D_MODEL // n_pc

    def project_chunk(j):
        cols = slice(j * pw, (j + 1) * pw)
        p_ref[:, cols] = (jnp.dot(ya_ref[...], w_ref[0:W_A, cols], preferred_element_type=F32)
                          + jnp.dot(yp_ref[...], w_ref[W_A:W_A + W_M, cols], preferred_element_type=F32))

    def norm_previous():
        res = ALPHA * x_ref[...] + p_ref[...]
        mean = jnp.mean(res, axis=-1, keepdims=True)
        var_r = jnp.mean(jnp.square(res - mean), axis=-1, keepdims=True)
        out_ref[...] = (res - mean) * lax.rsqrt(var_r + LN_EPS) * lg_ref[...] + lb_ref[...]

    pending = [functools.partial(project_chunk, j) for j in range(n_pc)] + [norm_previous]

    def emit_projection_work():
        if pending:
            pending.pop(0)()

    emit_projection_work()

    r = lax.broadcasted_iota(jnp.int32, (L, L), 0)
    s = lax.broadcasted_iota(jnp.int32, (L, L), 1)
    causal = s <= r
    shifts = [jnp.where(r - s == d, 1.0, 0.0).astype(BF16) for d in range(1, CONV_W)]

    def conv_silu(x_ref, halo_ref, lo):
        x = x_ref[...]
        halo = jnp.where(c > 0, halo_ref[...].astype(F32), 0.0)
        w = cw_ref[:, lo:lo + H_M * DK_M]
        y = cb_ref[:, lo:lo + H_M * DK_M] + w[CONV_W - 1:CONV_W] * x.astype(F32)
        top = jnp.zeros((SUBLANES, H_M * DK_M), F32)
        for d in range(1, CONV_W):
            wd = w[CONV_W - 1 - d:CONV_W - d]
            y = y + wd * jnp.dot(shifts[d - 1], x, preferred_element_type=F32)
            top = top + wd * jnp.concatenate(
                [halo[HALO - d:HALO], jnp.zeros((SUBLANES - d, H_M * DK_M), F32)], axis=0)
        y = jnp.concatenate([y[0:SUBLANES] + top, y[SUBLANES:]], axis=0)
        return y * jax.nn.sigmoid(y)

    q_all = conv_silu(q_ref, qh_ref, 0)
    emit_projection_work()
    k_all = conv_silu(k_ref, kh_ref, H_M * DK_M) * (DK_M ** -0.5)
    emit_projection_work()

    gc = g_ref[...] + gbr_ref[...]
    gr = gt_ref[S_IM:S_IM + 2 * H_M, :] + gbc_ref[S_IM:S_IM + 2 * H_M, :]
    tri_l = jnp.where(causal, 1.0, 0.0).astype(BF16)
    tri_u = jnp.where(r <= s, 1.0, 0.0).astype(BF16)
    b_cols = _split_dot(tri_l, _log_sigmoid(gc) * LOG2E)
    lf_rows = _log_sigmoid(gr) * LOG2E
    b_rows = jnp.dot(lf_rows.astype(BF16), tri_u, preferred_element_type=F32) \
        + jnp.dot((lf_rows - lf_rows.astype(BF16).astype(F32)).astype(BF16), tri_u,
                  preferred_element_type=F32)
    gc = gc * LOG2E
    gr = gr * LOG2E
    ones = jnp.ones((L, LANES), BF16)
    emit_projection_work()

    for h in range(H_M):
        emit_projection_work()
        q = q_all[:, h * DK_M:(h + 1) * DK_M]
        k = k_all[:, h * DK_M:(h + 1) * DK_M]
        v = jnp.concatenate([v_ref[:, h * DV_M:(h + 1) * DV_M], ones], axis=1)
        qb = q.astype(BF16)
        b_c = b_cols[:, S_FM + h:S_FM + h + 1]
        i_c = gc[:, S_IM + h:S_IM + h + 1]
        b_r = b_rows[H_M + h:H_M + h + 1, :]
        i_r = gr[h:h + 1, :]
        m_prev = m_ref[h]
        ct = ct_ref[h]

        log_d = jnp.where(causal, b_c - b_r + i_r, -jnp.inf)
        g = b_c + m_prev
        m_t = jnp.maximum(jnp.max(log_d, axis=-1, keepdims=True), g)
        qk = lax.dot_general(qb, k.astype(BF16), _NT, preferred_element_type=F32)
        s_mat = qk * jnp.exp2(log_d - m_t)
        inter = jnp.exp2(g - m_t)
        num = jnp.dot(s_mat.astype(BF16), v, preferred_element_type=F32) \
            + inter * jnp.dot(qb, ct.astype(BF16), preferred_element_type=F32)
        den = jnp.maximum(jnp.abs(num[:, DV_M:]), jnp.exp2(-m_t))
        hh = num[:, 0:DV_M] / jnp.concatenate([den] * (DV_M // LANES), axis=1)

        emit_projection_work()
        b_last = b_c[L - 1:L, :]
        a_r = b_last - b_r + i_r
        m_new = jnp.maximum(b_last + m_prev, jnp.max(a_r, axis=-1, keepdims=True))
        decay = jnp.exp2(b_last + m_prev - m_new)
        wgt_c = jnp.exp2(b_last - b_c + i_c - m_new)
        kw = k * wgt_c
        ct_ref[h] = decay * ct + jnp.dot(kw.T.astype(BF16), v, preferred_element_type=F32)
        m_ref[h] = m_new

        mu = jnp.mean(hh, axis=-1, keepdims=True)
        var = jnp.mean(jnp.square(hh - mu), axis=-1, keepdims=True)
        hn = (hh - mu) * lax.rsqrt(var + LN_EPS) * ng_ref[:, h * DV_M:(h + 1) * DV_M]
        og = o_ref[:, h * DV_M:(h + 1) * DV_M].astype(F32)
        zg = z_ref[:, h * DV_M:(h + 1) * DV_M].astype(F32)
        y_ref[:, h * DV_M:(h + 1) * DV_M] = (hn * jax.nn.sigmoid(og) * (zg * jax.nn.sigmoid(zg))).astype(BF16)

    assert not pending
    yp_ref[...] = y_ref[...]


def _mlstm_out(main, small, gate_t, conv_w, conv_b, gb_row, gb_col, norm_g, ya, x2d, w_out, ln_g, ln_b, B, T):
    nc = T // L_M
    hb = L_M // HALO
    qk_w = H_M * DK_M
    n_steps = B * nc

    def cur(col):
        return lambda s: (jnp.minimum(s, n_steps - 1), col)

    def prev(s):
        return (jnp.maximum(s - 1, 0), 0)

    def halo_map(col):
        return lambda s: (jnp.maximum(jnp.minimum(s, n_steps - 1) * hb - 1, 0), col)

    const = lambda s: (0, 0)
    return pl.pallas_call(
        functools.partial(_mlstm_out_kernel, n_chunks=nc, n_steps=n_steps),
        grid=(n_steps + 1,),
        in_specs=[pl.BlockSpec((L_M, qk_w), cur(C_QM // qk_w)),
                  pl.BlockSpec((L_M, qk_w), cur(C_KM // qk_w)),
                  pl.BlockSpec((HALO, qk_w), halo_map(C_QM // qk_w)),
                  pl.BlockSpec((HALO, qk_w), halo_map(C_KM // qk_w)),
                  pl.BlockSpec((L_M, W_M), cur(C_VM // W_M)),
                  pl.BlockSpec((L_M, W_M), cur(C_OM // W_M)),
                  pl.BlockSpec((L_M, W_M), cur(C_ZM // W_M)),
                  pl.BlockSpec((L_M, LANES), cur(D_C // LANES)),
                  pl.BlockSpec((LANES, L_M), lambda s: (0, jnp.minimum(s, n_steps - 1))),
                  pl.BlockSpec((CONV_W, 2 * qk_w), const),
                  pl.BlockSpec((1, 2 * qk_w), const),
                  pl.BlockSpec((1, LANES), const),
                  pl.BlockSpec((LANES, 1), const),
                  pl.BlockSpec((1, W_M), const),
                  pl.BlockSpec((L_M, W_A), prev),
                  pl.BlockSpec((L_M, D_MODEL), prev),
                  pl.BlockSpec((W_A + W_M, D_MODEL), const),
                  pl.BlockSpec((1, D_MODEL), const),
                  pl.BlockSpec((1, D_MODEL), const)],
        out_specs=pl.BlockSpec((L_M, D_MODEL), prev),
        out_shape=jax.ShapeDtypeStruct((B * T, D_MODEL), F32),
        scratch_shapes=[pltpu.VMEM((H_M, DK_M, DV_M + LANES), F32),
                        pltpu.VMEM((H_M, 1, 1), F32),
                        pltpu.VMEM((L_M, W_M), BF16),
                        pltpu.VMEM((L_M, W_M), BF16),
                        pltpu.VMEM((L_M, D_MODEL), F32)],
        compiler_params=pltpu.CompilerParams(
            dimension_semantics=("arbitrary",), vmem_limit_bytes=VMEM_LIMIT),
        name="mlstm_out",
    )(main, main, main, main, main, main, main, small, gate_t,
      conv_w, conv_b, gb_row, gb_col, norm_g, ya, x2d, w_out, ln_g, ln_b)


_W_IN_SEGS = (("q_a", W_A), ("c_kv", D_C), ("z_a", W_A), ("q_i", H_IDX * D_IDX), ("k_i", D_IDX),
              ("w_i", H_IDX), ("q_m", H_M * DK_M), ("k_m", H_M * DK_M), ("v_m", W_M), ("i_m", H_M),
              ("f_m", H_M), ("o_m", W_M), ("z_m", W_M))
_MAIN_ORDER = ("q_a", "z_a", "q_i", "q_m", "k_m", "v_m", "o_m", "z_m")
_SEG_NAMES = [name for name, _ in _W_IN_SEGS]
assert _SEG_NAMES.index("f_m") == _SEG_NAMES.index("i_m") + 1 and S_FM == S_IM + H_M


def _repack_kernel(wt_ref, main_ref, small_ref):
    src, off = {}, 0
    for name, width in _W_IN_SEGS:
        src[name] = (off, width)
        off += width
    dst = 0
    for name in _MAIN_ORDER:
        lo, width = src[name]
        main_ref[dst:dst + width, :] = wt_ref[lo:lo + width, :].astype(BF16)
        dst += width
    parts = [wt_ref[src[name][0]:src[name][0] + src[name][1], :] for name in ("c_kv", "k_i", "w_i")]
    lo = src["i_m"][0]
    parts.append(wt_ref[lo:lo + 2 * H_M, :])
    used = sum(p.shape[0] for p in parts)
    parts.append(jnp.zeros((N_SMALL - used, wt_ref.shape[1]), F32))
    small_ref[...] = jnp.concatenate(parts, axis=0).astype(BF16)


def _repack_w_in(w_in, tc=256):
    n_cols = sum(width for _, width in _W_IN_SEGS)
    wt = jnp.swapaxes(w_in, 1, 2)[0]
    return pl.pallas_call(
        _repack_kernel,
        grid=(D_MODEL // tc,),
        in_specs=[pl.BlockSpec((n_cols, tc), lambda i: (0, i))],
        out_specs=[pl.BlockSpec((N_MAIN, tc), lambda i: (0, i)),
                   pl.BlockSpec((N_SMALL, tc), lambda i: (0, i))],
        out_shape=[jax.ShapeDtypeStruct((N_MAIN, D_MODEL), BF16),
                   jax.ShapeDtypeStruct((N_SMALL, D_MODEL), BF16)],
        compiler_params=pltpu.CompilerParams(
            dimension_semantics=("arbitrary",), vmem_limit_bytes=VMEM_LIMIT),
        name="repack",
    )(wt)


def kernel(x, w_in, b_igate, b_fgate, kv_norm_g, w_uk, w_uv, idx_k_ln_g, idx_k_ln_b, rel_bias,
           conv_w, conv_b, mh_norm_g, w_out, ln_g, ln_b):
    B, T, D = x.shape
    assert D == D_MODEL and T % L_M == 0 and T % (2 * KB) == 0 and w_in.shape[0] == 1
    bias = _bias_tiles(rel_bias)
    x2d = x.reshape(B * T, D)
    w_main, w_small = _repack_w_in(w_in)
    main, small = _proj(x2d, w_main, w_small)
    ckv_n, ckv_t, kidx_n, gate_t = _prep(small, kv_norm_g[0][None], idx_k_ln_g[0][None], idx_k_ln_b[0][None])
    w_uk_t = jnp.transpose(w_uk[0], (0, 2, 1)).astype(BF16)
    w_uv_t = jnp.transpose(w_uv[0], (0, 2, 1)).astype(BF16)
    ya = _dsa(main, gate_t, ckv_n, ckv_t, kidx_n, w_uk_t, w_uv_t, bias, B, T)
    gb = jnp.zeros((LANES,), F32).at[S_IM:S_IM + H_M].set(b_igate[0]).at[S_FM:S_FM + H_M].set(b_fgate[0])
    out = _mlstm_out(main, small, gate_t, conv_w[0], conv_b[0][None], gb[None, :], gb[:, None],
                     mh_norm_g[0][None], ya, x2d, w_out[0].astype(BF16), ln_g[0][None], ln_b[0][None], B, T)
    return out.reshape(B, T, D)
```

```python
import functools
import math

import numpy as np
import jax
import jax.numpy as jnp
from jax import lax
from jax.experimental import pallas as pl
from jax.experimental.pallas import tpu as pltpu

F32 = jnp.float32
BF16 = jnp.bfloat16

D_MODEL = 2048
W_A = 1024
DH_A = 128
H_A = 8
D_C = 256
H_IDX = 16
D_IDX = 64
TOPK = 256
W_M = 1024
H_M = 4
DV_M = 256
DK_M = 128
CONV_W = 4
N_BUCKETS = 32
MAX_DIST = 128
ALPHA = 2.0 ** 0.25
LN_EPS = 1e-5

LANES = 128
SUBLANES = 8
VMEM_LIMIT = 56 * 1024 * 1024

QB = 256
KB = 256
KI = 128
CNT_ROWS = 512
CNT_ACC = 4 * SUBLANES
L_M = 256
HALO = 16
NEG = -1e30
LOG2E = math.log2(math.e)
ONES_ROWS = 16

C_QA, C_ZA, C_QI, C_QM, C_KM, C_VM, C_OM, C_ZM = 0, 1024, 2048, 3072, 3584, 4096, 5120, 6144
N_MAIN = 7168
N_SMALL = 384
S_KI, S_WI, S_IM, S_FM = 0, 64, 80, 84


def _t5_bucket_np(rel):
    max_exact = N_BUCKETS // 2
    n = np.maximum(rel, 0)
    nf = np.maximum(n, 1).astype(np.float32)
    large = max_exact + (np.log(nf / np.float32(max_exact)) / np.float32(math.log(MAX_DIST / max_exact))
                         * np.float32(N_BUCKETS - max_exact)).astype(np.int32)
    large = np.minimum(large, N_BUCKETS - 1)
    return np.where(n < max_exact, n, large).astype(np.int32)


FAR_BUCKET = int(_t5_bucket_np(np.array(2 * KB + 1)))


def _bucket_tiles():
    i = np.arange(QB)[None, :]
    j = np.arange(KB)[:, None]
    t0 = _t5_bucket_np(i - j)
    t1 = _t5_bucket_np(i - j + KB)
    assert (t5 := _t5_bucket_np(np.arange(KB + 1, 4096))).min() == t5.max() == FAR_BUCKET
    return np.stack([t0, t1]).astype(np.int32)


def _bias_kernel(bucket_ref, rb_ref, out_ref):
    h = pl.program_id(0)
    far = rb_ref[FAR_BUCKET, h]
    for k in range(2):
        bk = bucket_ref[k]
        acc = jnp.zeros((KB, QB), F32)
        for b in range(N_BUCKETS):
            acc = jnp.where(bk == b, rb_ref[b, h] - far, acc)
        out_ref[0, k] = acc * LOG2E


def _bias_tiles(rel_bias):
    bucket = jnp.asarray(_bucket_tiles())
    return pl.pallas_call(
        _bias_kernel,
        grid=(H_A,),
        in_specs=[pl.BlockSpec((2, KB, QB), lambda h: (0, 0, 0)),
                  pl.BlockSpec(memory_space=pltpu.SMEM)],
        out_specs=pl.BlockSpec((1, 2, KB, QB), lambda h: (h, 0, 0, 0)),
        out_shape=jax.ShapeDtypeStruct((H_A, 2, KB, QB), F32),
        name="bias_tiles",
    )(bucket, rel_bias)


_NT = (((1,), (1,)), ((), ()))


def _proj_kernel(x_ref, w_ref, ws_ref, o_ref, os_ref, xb_ref):
    @pl.when(pl.program_id(1) == 0)
    def _():
        xb_ref[...] = x_ref[...].astype(BF16)
        os_ref[...] = lax.dot_general(xb_ref[...], ws_ref[...], _NT, preferred_element_type=F32)

    o_ref[...] = lax.dot_general(xb_ref[...], w_ref[...], _NT, preferred_element_type=F32).astype(BF16)


def _proj(x2d, w_main, w_small, tm=1024, tn=1024):
    M = x2d.shape[0]
    return pl.pallas_call(
        _proj_kernel,
        grid=(M // tm, N_MAIN // tn),
        in_specs=[pl.BlockSpec((tm, D_MODEL), lambda i, j: (i, 0)),
                  pl.BlockSpec((tn, D_MODEL), lambda i, j: (j, 0)),
                  pl.BlockSpec((N_SMALL, D_MODEL), lambda i, j: (0, 0))],
        out_specs=[pl.BlockSpec((tm, tn), lambda i, j: (i, j)),
                   pl.BlockSpec((tm, N_SMALL), lambda i, j: (i, 0))],
        out_shape=[jax.ShapeDtypeStruct((M, N_MAIN), BF16),
                   jax.ShapeDtypeStruct((M, N_SMALL), F32)],
        scratch_shapes=[pltpu.VMEM((tm, D_MODEL), BF16)],
        compiler_params=pltpu.CompilerParams(
            dimension_semantics=("arbitrary", "arbitrary"), vmem_limit_bytes=VMEM_LIMIT),
        name="proj",
    )(x2d, w_main, w_small)


def _prep_kernel(s_ref, kvg_ref, ig_ref, ib_ref, ckv_ref, ckvt_ref, kidx_ref, gt_ref):
    c = s_ref[:, 0:D_C]
    c = c * lax.rsqrt(jnp.mean(c * c, axis=-1, keepdims=True) + LN_EPS) * kvg_ref[...]
    ckv_ref[...] = c.astype(BF16)
    for r in range(ckvt_ref.shape[0]):
        ckvt_ref[r, 0:D_C, :] = c[r * KB:(r + 1) * KB, :].T.astype(BF16)
        ckvt_ref[r, D_C:D_C + ONES_ROWS, :] = jnp.ones((ONES_ROWS, KB), BF16)
    tile = s_ref[:, D_C:D_C + LANES]
    k = tile[:, S_KI:S_KI + D_IDX]
    mu = jnp.mean(k, axis=-1, keepdims=True)
    var = jnp.mean(jnp.square(k - mu), axis=-1, keepdims=True)
    kidx_ref[...] = ((k - mu) * lax.rsqrt(var + LN_EPS) * ig_ref[...] + ib_ref[...]).astype(BF16)
    gt_ref[...] = tile.T


def _prep(small, kv_g, idx_g, idx_b, tm=1024):
    M = small.shape[0]
    return pl.pallas_call(
        _prep_kernel,
        grid=(M // tm,),
        in_specs=[pl.BlockSpec((tm, N_SMALL), lambda i: (i, 0)),
                  pl.BlockSpec((1, D_C), lambda i: (0, 0)),
                  pl.BlockSpec((1, D_IDX), lambda i: (0, 0)),
                  pl.BlockSpec((1, D_IDX), lambda i: (0, 0))],
        out_specs=[pl.BlockSpec((tm, D_C), lambda i: (i, 0)),
                   pl.BlockSpec((tm // KB, D_C + ONES_ROWS, KB), lambda i: (i, 0, 0)),
                   pl.BlockSpec((tm, D_IDX), lambda i: (i, 0)),
                   pl.BlockSpec((LANES, tm), lambda i: (0, i))],
        out_shape=[jax.ShapeDtypeStruct((M, D_C), BF16),
                   jax.ShapeDtypeStruct((M // KB, D_C + ONES_ROWS, KB), BF16),
                   jax.ShapeDtypeStruct((M, D_IDX), BF16),
                   jax.ShapeDtypeStruct((LANES, M), F32)],
        name="prep",
    )(small, kv_g, idx_g, idx_b)


def _key_to_float(key):
    bits = jnp.where(key < 0, key ^ jnp.int32(0x7FFFFFFF), key)
    return lax.bitcast_convert_type(bits, F32)


def _dsa_kernel(qa_ref, za_ref, qi_ref, gt_ref, ckv_ref, ckvt_ref, kidx_ref, wukt_ref, wuvt_ref, bias_ref,
                y_ref, qall_ref, qr_ref, ha_ref, hb_ref, sc_ref, mb_ref, sa_ref, sb_ref, pa_ref, pb_ref,
                ta_ref, tb_ref, m_ref, al_ref, acc_ref, w_ref, cand_ref, thr_ref, cut_ref, cge_ref):
    qi = pl.program_id(1)
    nkb = qi + 1
    last_kb = mb_ref.shape[0] - 1

    for h in range(H_A):
        ql = lax.dot_general(wukt_ref[h], qa_ref[:, h * DH_A:(h + 1) * DH_A], (((1,), (1,)), ((), ())),
                             preferred_element_type=F32)
        qall_ref[h // (H_A // 2), :, (h % (H_A // 2)) * QB:(h % (H_A // 2) + 1) * QB] = (
            ql * (DH_A ** -0.5 * LOG2E)).astype(BF16)


    @pl.when(qi * QB < TOPK)
    def _():
        krow = lax.broadcasted_iota(jnp.int32, (KB, QB), 0)
        qcol = lax.broadcasted_iota(jnp.int32, (KB, QB), 1)
        mb_ref[0] = jnp.where(krow <= qcol, 0.0, NEG).astype(F32)

    @pl.when(qi * QB >= TOPK)
    def _():
        for h in range(H_IDX):
            qr_ref[h * QB:(h + 1) * QB, :] = qi_ref[:, h * D_IDX:(h + 1) * D_IDX]
        w_ref[...] = gt_ref[S_WI:S_WI + H_IDX, :] * ((D_IDX ** -0.5) * (H_IDX ** -0.5))
        krow = lax.broadcasted_iota(jnp.int32, (KI, LANES), 0)
        qcol = lax.broadcasted_iota(jnp.int32, (KI, LANES), 1)
        n_ki = nkb * (KB // KI)
        last_ki = sc_ref.shape[0] // KI - 1

        def head_dots(ki, dst_ref):
            k = kidx_ref[pl.ds(pl.multiple_of(jnp.minimum(ki, last_ki) * KI, KI), KI), :]
            dst_ref[...] = lax.dot_general(k, qr_ref[...], (((1,), (1,)), ((), ())),
                                           preferred_element_type=F32)

        def reduce_heads(src_ref, ki):
            for g in range(QB // LANES):
                lanes = slice(g * LANES, (g + 1) * LANES)
                acc = jnp.zeros((KI, LANES), F32)
                for h in range(H_IDX):
                    acc = acc + (jnp.maximum(src_ref[:, h * QB + g * LANES:h * QB + (g + 1) * LANES], 0.0)
                                 * w_ref[h:h + 1, lanes])
                sc_ref[pl.ds(pl.multiple_of(ki * KI, KI), KI), lanes] = jnp.where(
                    krow + (ki * KI - qi * QB - g * LANES) <= qcol, acc, -jnp.inf)

        pad_kb = jnp.minimum(nkb, last_kb)
        sc_ref[pl.ds(pl.multiple_of(pad_kb * KB, KB), KB), :] = jnp.full((KB, QB), -jnp.inf, F32)

        head_dots(0, ha_ref)

        def sc_body(j, carry):
            head_dots(2 * j + 1, hb_ref)
            reduce_heads(ha_ref, 2 * j)
            head_dots(2 * j + 2, ha_ref)
            reduce_heads(hb_ref, 2 * j + 1)
            return carry
        lax.fori_loop(0, n_ki // 2, sc_body, 0)

        n_cnt = (nkb * KB + CNT_ROWS - 1) // CNT_ROWS

        def count_where(pred, steps=None):
            def body(c, acc):
                parts = []
                for g in range(QB // LANES):
                    lanes = slice(g * LANES, (g + 1) * LANES)
                    if steps is None:
                        blk = sc_ref[pl.ds(pl.multiple_of(c * CNT_ROWS, CNT_ROWS), CNT_ROWS), lanes]
                    else:
                        blk = sc_ref[c * CNT_ROWS:(c + 1) * CNT_ROWS, lanes]
                    hit = jnp.where(pred(blk, c * CNT_ROWS, lanes), 1.0, 0.0).astype(F32)
                    parts.append(jnp.sum(hit.reshape(CNT_ROWS // CNT_ACC, CNT_ACC, LANES), axis=0))
                return acc + jnp.concatenate(parts, axis=1)
            acc = jnp.zeros((CNT_ACC, QB), F32)
            if steps is None:
                acc = lax.fori_loop(0, n_cnt, body, acc)
            else:
                for c in range(steps):
                    acc = body(c, acc)
            return jnp.sum(acc, axis=0, keepdims=True)

        def search(steps):
            def bit_body(i, carry):
                u, c_ge = carry
                trial = u | lax.shift_left(jnp.int32(1), 31 - i)
                cand_ref[...] = _key_to_float(trial ^ jnp.int32(-2 ** 31))
                cnt = count_where(lambda blk, row0, lanes: blk >= cand_ref[:, lanes], steps)
                ok = cnt >= float(TOPK)
                return jnp.where(ok, trial, u), jnp.where(ok, cnt, c_ge)
            u, c_ge = lax.fori_loop(0, 32, bit_body, (jnp.zeros((1, QB), jnp.int32),
                                                      jnp.full((1, QB), float(sc_ref.shape[0]), F32)))
            thr_ref[...] = _key_to_float(u ^ jnp.int32(-2 ** 31))
            cge_ref[...] = c_ge

        for steps in range(1, sc_ref.shape[0] // CNT_ROWS + 1):
            pl.when(n_cnt == steps)(functools.partial(search, steps))
        has_ties = jnp.max(cge_ref[...]) > float(TOPK)

        @pl.when(jnp.logical_not(has_ties))
        def _():
            def mb_body(kb, carry):
                blk = sc_ref[pl.ds(pl.multiple_of(kb * KB, KB), KB), :]
                mb_ref[kb] = jnp.where(blk >= thr_ref[...], 0.0, NEG).astype(F32)
                return carry
            lax.fori_loop(0, nkb, mb_body, 0)

        @pl.when(has_ties)
        def _():
            c_gt = count_where(lambda blk, row0, lanes: blk > thr_ref[:, lanes])
            need = float(TOPK) - c_gt
            rows = lax.broadcasted_iota(jnp.int32, (CNT_ROWS, LANES), 0)
            n_bits = (sc_ref.shape[0] - 1).bit_length()

            def idx_body(i, cut):
                trial = cut | lax.shift_left(jnp.int32(1), n_bits - 1 - i)
                cut_ref[...] = trial
                before = count_where(lambda blk, row0, lanes: (blk == thr_ref[:, lanes])
                                     & (rows + row0 < cut_ref[:, lanes]))
                return jnp.where(before < need, trial, cut)
            cut_ref[...] = lax.fori_loop(0, n_bits, idx_body, jnp.zeros((1, QB), jnp.int32))

            def mb_body(kb, carry):
                for g in range(QB // LANES):
                    lanes = slice(g * LANES, (g + 1) * LANES)
                    blk = sc_ref[pl.ds(pl.multiple_of(kb * KB, KB), KB), lanes]
                    thr = thr_ref[:, lanes]
                    keep = (blk > thr) | ((blk == thr) & (rows[0:KB] + kb * KB <= cut_ref[:, lanes]))
                    mb_ref[kb, :, lanes] = jnp.where(keep, 0.0, NEG).astype(F32)
                return carry
            lax.fori_loop(0, nkb, mb_body, 0)

    m_ref[...] = jnp.full(m_ref.shape, NEG, F32)
    acc_ref[...] = jnp.zeros(acc_ref.shape, F32)

    hh = H_A // 2
    s_refs, p_refs, t_refs = (sa_ref, sb_ref), (pa_ref, pb_ref), (ta_ref, tb_ref)
    pb_ref[...] = jnp.zeros(pb_ref.shape, BF16)
    al_ref[...] = jnp.ones(al_ref.shape, F32)

    def logits(kb, half):
        kv = ckv_ref[pl.ds(pl.multiple_of(jnp.minimum(kb, last_kb) * KB, KB), KB), :]
        s_refs[half][...] = jnp.dot(kv, qall_ref[half], preferred_element_type=F32)

    def softmax(kb, half, with_bias):
        tile = jnp.clip(qi - kb, 0, 1)
        for j in range(hh):
            for g in range(QB // LANES):
                lanes = slice(g * LANES, (g + 1) * LANES)
                cols = slice(j * QB + g * LANES, j * QB + (g + 1) * LANES)
                x = s_refs[half][:, cols] + mb_ref[kb, :, lanes]
                if with_bias:
                    x = x + bias_ref[half * hh + j, tile, :, lanes]
                m_prev = m_ref[half, :, cols]
                m_blk = jnp.max(x.reshape(KB // CNT_ACC, CNT_ACC, LANES), axis=0)
                m_new = jnp.maximum(m_prev, jnp.max(m_blk, axis=0, keepdims=True))
                al_ref[half, :, cols] = jnp.exp2(m_prev - m_new)
                m_ref[half, :, cols] = m_new
                p_refs[half][:, cols] = jnp.exp2(x - m_new).astype(BF16)

    def accumulate(kb, half):
        t_refs[half][...] = jnp.dot(ckvt_ref[jnp.maximum(kb, 0)], p_refs[half][...],
                                    preferred_element_type=F32)
        acc_ref[half] = acc_ref[half] * al_ref[half] + t_refs[half][...]

    def sweep(first_kb, end_kb, with_bias):
        def body(kb, carry):
            logits(kb, 1)
            softmax(kb, 0, with_bias)
            accumulate(kb - 1, 1)
            logits(kb + 1, 0)
            softmax(kb, 1, with_bias)
            accumulate(kb, 0)
            return carry
        lax.fori_loop(first_kb, end_kb, body, 0)

    n_far = jnp.maximum(qi - 1, 0)
    logits(0, 0)
    sweep(0, n_far, False)
    sweep(n_far, nkb, True)
    accumulate(nkb - 1, 1)

    for h in range(H_A):
        half, cols = h // hh, slice((h % hh) * QB, (h % hh + 1) * QB)
        ya_t = jnp.dot(wuvt_ref[h], acc_ref[half, 0:D_C, cols].astype(BF16), preferred_element_type=F32)
        ya = (ya_t / acc_ref[half, D_C:D_C + 1, cols]).T
        z = za_ref[:, h * DH_A:(h + 1) * DH_A].astype(F32)
        y_ref[:, h * DH_A:(h + 1) * DH_A] = (ya * (z * jax.nn.sigmoid(z))).astype(BF16)


def _dsa(main, gate_t, ckv_n, ckv_t, kidx_n, w_uk_t, w_uv_t, bias, B, T):
    nq = T // QB
    return pl.pallas_call(
        _dsa_kernel,
        grid=(B, nq),
        in_specs=[pl.BlockSpec((QB, W_A), lambda b, q: (b * nq + q, C_QA // W_A)),
                  pl.BlockSpec((QB, W_A), lambda b, q: (b * nq + q, C_ZA // W_A)),
                  pl.BlockSpec((QB, H_IDX * D_IDX), lambda b, q: (b * nq + q, C_QI // (H_IDX * D_IDX))),
                  pl.BlockSpec((LANES, QB), lambda b, q: (0, b * nq + q)),
                  pl.BlockSpec((T, D_C), lambda b, q: (b, 0)),
                  pl.BlockSpec((T // KB, D_C + ONES_ROWS, KB), lambda b, q: (b, 0, 0)),
                  pl.BlockSpec((T, D_IDX), lambda b, q: (b, 0)),
                  pl.BlockSpec((H_A, D_C, DH_A), lambda b, q: (0, 0, 0)),
                  pl.BlockSpec((H_A, DH_A, D_C), lambda b, q: (0, 0, 0)),
                  pl.BlockSpec((H_A, 2, KB, QB), lambda b, q: (0, 0, 0, 0))],
        out_specs=pl.BlockSpec((QB, W_A), lambda b, q: (b * nq + q, 0)),
        out_shape=jax.ShapeDtypeStruct((B * T, W_A), BF16),
        scratch_shapes=[pltpu.VMEM((2, D_C, H_A // 2 * QB), BF16),
                        pltpu.VMEM((H_IDX * QB, D_IDX), BF16),
                        pltpu.VMEM((KI, H_IDX * QB), F32),
                        pltpu.VMEM((KI, H_IDX * QB), F32),
                        pltpu.VMEM((T, QB), F32),
                        pltpu.VMEM((T // KB, KB, QB), F32),
                        pltpu.VMEM((KB, H_A // 2 * QB), F32),
                        pltpu.VMEM((KB, H_A // 2 * QB), F32),
                        pltpu.VMEM((KB, H_A // 2 * QB), BF16),
                        pltpu.VMEM((KB, H_A // 2 * QB), BF16),
                        pltpu.VMEM((D_C + ONES_ROWS, H_A // 2 * QB), F32),
                        pltpu.VMEM((D_C + ONES_ROWS, H_A // 2 * QB), F32),
                        pltpu.VMEM((2, 1, H_A // 2 * QB), F32),
                        pltpu.VMEM((2, 1, H_A // 2 * QB), F32),
                        pltpu.VMEM((2, D_C + ONES_ROWS, H_A // 2 * QB), F32),
                        pltpu.VMEM((H_IDX, QB), F32),
                        pltpu.VMEM((1, QB), F32),
                        pltpu.VMEM((1, QB), F32),
                        pltpu.VMEM((1, QB), jnp.int32),
                        pltpu.VMEM((1, QB), F32)],
        compiler_params=pltpu.CompilerParams(
            dimension_semantics=("arbitrary", "arbitrary"), vmem_limit_bytes=VMEM_LIMIT),
        name="dsa",
    )(main, main, main, gate_t, ckv_n, ckv_t, kidx_n, w_uk_t, w_uv_t, bias)


def _split_dot(tri, x):
    hi = x.astype(BF16)
    lo = (x - hi.astype(F32)).astype(BF16)
    return jnp.dot(tri, hi, preferred_element_type=F32) + jnp.dot(tri, lo, preferred_element_type=F32)


def _log_sigmoid(x):
    return jnp.minimum(x, 0.0) - jnp.log1p(jnp.exp(-jnp.abs(x)))


def _mlstm_out_kernel(q_ref, k_ref, qh_ref, kh_ref, v_ref, o_ref, z_ref, g_ref, gt_ref,
                      cw_ref, cb_ref, gbr_ref, gbc_ref, ng_ref, ya_ref, x_ref, w_ref, lg_ref, lb_ref,
                      out_ref, ct_ref, m_ref, y_ref, yp_ref, p_ref, *, n_chunks, n_steps):
    step = pl.program_id(0)
    c = jnp.minimum(step, n_steps - 1) % n_chunks
    L = L_M

    @pl.when(step == 0)
    def _():
        yp_ref[...] = jnp.zeros(yp_ref.shape, BF16)

    @pl.when(c == 0)
    def _():
        ct_ref[...] = jnp.zeros(ct_ref.shape, F32)
        m_ref[...] = jnp.zeros(m_ref.shape, F32)

    n_pc = 2 * H_M
    pw = D_MODEL // n_pc

    def project_chunk(j, part):
        cols = slice(j * pw, (j + 1) * pw)
        if part == 0:
            p_ref[:, cols] = jnp.dot(ya_ref[...], w_ref[0:W_A, cols], preferred_element_type=F32)
        else:
            p_ref[:, cols] += jnp.dot(yp_ref[...], w_ref[W_A:W_A + W_M, cols], preferred_element_type=F32)

    def norm_previous():
        res = ALPHA * x_ref[...] + p_ref[...]
        mean = jnp.mean(res, axis=-1, keepdims=True)
        var_r = jnp.mean(jnp.square(res - mean), axis=-1, keepdims=True)
        out_ref[...] = (res - mean) * lax.rsqrt(var_r + LN_EPS) * lg_ref[...] + lb_ref[...]

    pending = [functools.partial(project_chunk, j, part) for j in range(n_pc) for part in range(2)]
    pending.append(norm_previous)

    def emit_projection_work():
        if pending:
            pending.pop(0)()

    emit_projection_work()

    r = lax.broadcasted_iota(jnp.int32, (L, L), 0)
    s = lax.broadcasted_iota(jnp.int32, (L, L), 1)
    causal = s <= r
    shifts = [jnp.where(r - s == d, 1.0, 0.0).astype(BF16) for d in range(1, CONV_W)]

    def conv_silu(x_ref, halo_ref, lo):
        x = x_ref[...]
        halo = jnp.where(c > 0, halo_ref[...].astype(F32), 0.0)
        w = cw_ref[:, lo:lo + H_M * DK_M]
        y = cb_ref[:, lo:lo + H_M * DK_M] + w[CONV_W - 1:CONV_W] * x.astype(F32)
        top = jnp.zeros((SUBLANES, H_M * DK_M), F32)
        for d in range(1, CONV_W):
            wd = w[CONV_W - 1 - d:CONV_W - d]
            y = y + wd * jnp.dot(shifts[d - 1], x, preferred_element_type=F32)
            top = top + wd * jnp.concatenate(
                [halo[HALO - d:HALO], jnp.zeros((SUBLANES - d, H_M * DK_M), F32)], axis=0)
        y = jnp.concatenate([y[0:SUBLANES] + top, y[SUBLANES:]], axis=0)
        return y * jax.nn.sigmoid(y)

    q_all = conv_silu(q_ref, qh_ref, 0)
    emit_projection_work()
    k_all = conv_silu(k_ref, kh_ref, H_M * DK_M) * (DK_M ** -0.5)
    emit_projection_work()

    gc = g_ref[...] + gbr_ref[...]
    gr = gt_ref[S_IM:S_IM + 2 * H_M, :] + gbc_ref[S_IM:S_IM + 2 * H_M, :]
    tri_l = jnp.where(causal, 1.0, 0.0).astype(BF16)
    tri_u = jnp.where(r <= s, 1.0, 0.0).astype(BF16)
    b_cols = _split_dot(tri_l, _log_sigmoid(gc) * LOG2E)
    lf_rows = _log_sigmoid(gr) * LOG2E
    b_rows = jnp.dot(lf_rows.astype(BF16), tri_u, preferred_element_type=F32) \
        + jnp.dot((lf_rows - lf_rows.astype(BF16).astype(F32)).astype(BF16), tri_u,
                  preferred_element_type=F32)
    gc = gc * LOG2E
    gr = gr * LOG2E
    ones = jnp.ones((L, LANES), BF16)
    emit_projection_work()

    for h in range(H_M):
        emit_projection_work()
        q = q_all[:, h * DK_M:(h + 1) * DK_M]
        k = k_all[:, h * DK_M:(h + 1) * DK_M]
        v = jnp.concatenate([v_ref[:, h * DV_M:(h + 1) * DV_M], ones], axis=1)
        qb = q.astype(BF16)
        b_c = b_cols[:, S_FM + h:S_FM + h + 1]
        i_c = gc[:, S_IM + h:S_IM + h + 1]
        b_r = b_rows[H_M + h:H_M + h + 1, :]
        i_r = gr[h:h + 1, :]
        m_prev = m_ref[h]
        ct = ct_ref[h]

        emit_projection_work()
        log_d = jnp.where(causal, b_c - b_r + i_r, -jnp.inf)
        g = b_c + m_prev
        m_t = jnp.maximum(jnp.max(log_d, axis=-1, keepdims=True), g)
        qk = lax.dot_general(qb, k.astype(BF16), _NT, preferred_element_type=F32)
        s_mat = qk * jnp.exp2(log_d - m_t)
        inter = jnp.exp2(g - m_t)
        num = jnp.dot(s_mat.astype(BF16), v, preferred_element_type=F32) \
            + inter * jnp.dot(qb, ct.astype(BF16), preferred_element_type=F32)
        den = jnp.maximum(jnp.abs(num[:, DV_M:]), jnp.exp2(-m_t))
        hh = num[:, 0:DV_M] / jnp.concatenate([den] * (DV_M // LANES), axis=1)

        emit_projection_work()
        b_last = b_c[L - 1:L, :]
        a_r = b_last - b_r + i_r
        m_new = jnp.maximum(b_last + m_prev, jnp.max(a_r, axis=-1, keepdims=True))
        decay = jnp.exp2(b_last + m_prev - m_new)
        wgt_c = jnp.exp2(b_last - b_c + i_c - m_new)
        kw = k * wgt_c
        ct_ref[h] = decay * ct + jnp.dot(kw.T.astype(BF16), v, preferred_element_type=F32)
        m_ref[h] = m_new

        emit_projection_work()
        mu = jnp.mean(hh, axis=-1, keepdims=True)
        var = jnp.mean(jnp.square(hh - mu), axis=-1, keepdims=True)
        hn = (hh - mu) * lax.rsqrt(var + LN_EPS) * ng_ref[:, h * DV_M:(h + 1) * DV_M]
        og = o_ref[:, h * DV_M:(h + 1) * DV_M].astype(F32)
        zg = z_ref[:, h * DV_M:(h + 1) * DV_M].astype(F32)
        y_ref[:, h * DV_M:(h + 1) * DV_M] = (hn * jax.nn.sigmoid(og) * (zg * jax.nn.sigmoid(zg))).astype(BF16)

    assert not pending
    yp_ref[...] = y_ref[...]


def _mlstm_out(main, small, gate_t, conv_w, conv_b, gb_row, gb_col, norm_g, ya, x2d, w_out, ln_g, ln_b, B, T):
    nc = T // L_M
    hb = L_M // HALO
    qk_w = H_M * DK_M
    n_steps = B * nc

    def cur(col):
        return lambda s: (jnp.minimum(s, n_steps - 1), col)

    def prev(s):
        return (jnp.maximum(s - 1, 0), 0)

    def halo_map(col):
        return lambda s: (jnp.maximum(jnp.minimum(s, n_steps - 1) * hb - 1, 0), col)

    const = lambda s: (0, 0)
    return pl.pallas_call(
        functools.partial(_mlstm_out_kernel, n_chunks=nc, n_steps=n_steps),
        grid=(n_steps + 1,),
        in_specs=[pl.BlockSpec((L_M, qk_w), cur(C_QM // qk_w)),
                  pl.BlockSpec((L_M, qk_w), cur(C_KM // qk_w)),
                  pl.BlockSpec((HALO, qk_w), halo_map(C_QM // qk_w)),
                  pl.BlockSpec((HALO, qk_w), halo_map(C_KM // qk_w)),
                  pl.BlockSpec((L_M, W_M), cur(C_VM // W_M)),
                  pl.BlockSpec((L_M, W_M), cur(C_OM // W_M)),
                  pl.BlockSpec((L_M, W_M), cur(C_ZM // W_M)),
                  pl.BlockSpec((L_M, LANES), cur(D_C // LANES)),
                  pl.BlockSpec((LANES, L_M), lambda s: (0, jnp.minimum(s, n_steps - 1))),
                  pl.BlockSpec((CONV_W, 2 * qk_w), const),
                  pl.BlockSpec((1, 2 * qk_w), const),
                  pl.BlockSpec((1, LANES), const),
                  pl.BlockSpec((LANES, 1), const),
                  pl.BlockSpec((1, W_M), const),
                  pl.BlockSpec((L_M, W_A), prev),
                  pl.BlockSpec((L_M, D_MODEL), prev),
                  pl.BlockSpec((W_A + W_M, D_MODEL), const),
                  pl.BlockSpec((1, D_MODEL), const),
                  pl.BlockSpec((1, D_MODEL), const)],
        out_specs=pl.BlockSpec((L_M, D_MODEL), prev),
        out_shape=jax.ShapeDtypeStruct((B * T, D_MODEL), F32),
        scratch_shapes=[pltpu.VMEM((H_M, DK_M, DV_M + LANES), F32),
                        pltpu.VMEM((H_M, 1, 1), F32),
                        pltpu.VMEM((L_M, W_M), BF16),
                        pltpu.VMEM((L_M, W_M), BF16),
                        pltpu.VMEM((L_M, D_MODEL), F32)],
        compiler_params=pltpu.CompilerParams(
            dimension_semantics=("arbitrary",), vmem_limit_bytes=VMEM_LIMIT),
        name="mlstm_out",
    )(main, main, main, main, main, main, main, small, gate_t,
      conv_w, conv_b, gb_row, gb_col, norm_g, ya, x2d, w_out, ln_g, ln_b)


_W_IN_SEGS = (("q_a", W_A), ("c_kv", D_C), ("z_a", W_A), ("q_i", H_IDX * D_IDX), ("k_i", D_IDX),
              ("w_i", H_IDX), ("q_m", H_M * DK_M), ("k_m", H_M * DK_M), ("v_m", W_M), ("i_m", H_M),
              ("f_m", H_M), ("o_m", W_M), ("z_m", W_M))
_MAIN_ORDER = ("q_a", "z_a", "q_i", "q_m", "k_m", "v_m", "o_m", "z_m")
_SEG_NAMES = [name for name, _ in _W_IN_SEGS]
assert _SEG_NAMES.index("f_m") == _SEG_NAMES.index("i_m") + 1 and S_FM == S_IM + H_M


def _repack_kernel(wt_ref, main_ref, small_ref):
    src, off = {}, 0
    for name, width in _W_IN_SEGS:
        src[name] = (off, width)
        off += width
    dst = 0
    for name in _MAIN_ORDER:
        lo, width = src[name]
        main_ref[dst:dst + width, :] = wt_ref[lo:lo + width, :].astype(BF16)
        dst += width
    parts = [wt_ref[src[name][0]:src[name][0] + src[name][1], :] for name in ("c_kv", "k_i", "w_i")]
    lo = src["i_m"][0]
    parts.append(wt_ref[lo:lo + 2 * H_M, :])
    used = sum(p.shape[0] for p in parts)
    parts.append(jnp.zeros((N_SMALL - used, wt_ref.shape[1]), F32))
    small_ref[...] = jnp.concatenate(parts, axis=0).astype(BF16)


def _repack_w_in(w_in, tc=256):
    n_cols = sum(width for _, width in _W_IN_SEGS)
    wt = jnp.swapaxes(w_in, 1, 2)[0]
    return pl.pallas_call(
        _repack_kernel,
        grid=(D_MODEL // tc,),
        in_specs=[pl.BlockSpec((n_cols, tc), lambda i: (0, i))],
        out_specs=[pl.BlockSpec((N_MAIN, tc), lambda i: (0, i)),
                   pl.BlockSpec((N_SMALL, tc), lambda i: (0, i))],
        out_shape=[jax.ShapeDtypeStruct((N_MAIN, D_MODEL), BF16),
                   jax.ShapeDtypeStruct((N_SMALL, D_MODEL), BF16)],
        compiler_params=pltpu.CompilerParams(
            dimension_semantics=("arbitrary",), vmem_limit_bytes=VMEM_LIMIT),
        name="repack",
    )(wt)


def kernel(x, w_in, b_igate, b_fgate, kv_norm_g, w_uk, w_uv, idx_k_ln_g, idx_k_ln_b, rel_bias,
           conv_w, conv_b, mh_norm_g, w_out, ln_g, ln_b):
    B, T, D = x.shape
    assert D == D_MODEL and T % L_M == 0 and T % (2 * KB) == 0 and w_in.shape[0] == 1
    bias = _bias_tiles(rel_bias)
    x2d = x.reshape(B * T, D)
    w_main, w_small = _repack_w_in(w_in)
    main, small = _proj(x2d, w_main, w_small)
    ckv_n, ckv_t, kidx_n, gate_t = _prep(small, kv_norm_g[0][None], idx_k_ln_g[0][None], idx_k_ln_b[0][None])
    w_uk_t = jnp.transpose(w_uk[0], (0, 2, 1)).astype(BF16)
    w_uv_t = jnp.transpose(w_uv[0], (0, 2, 1)).astype(BF16)
    ya = _dsa(main, gate_t, ckv_n, ckv_t, kidx_n, w_uk_t, w_uv_t, bias, B, T)
    gb = jnp.zeros((LANES,), F32).at[S_IM:S_IM + H_M].set(b_igate[0]).at[S_FM:S_FM + H_M].set(b_fgate[0])
    out = _mlstm_out(main, small, gate_t, conv_w[0], conv_b[0][None], gb[None, :], gb[:, None],
                     mh_norm_g[0][None], ya, x2d, w_out[0].astype(BF16), ln_g[0][None], ln_b[0][None], B, T)
    return out.reshape(B, T, D)
```

```python
import functools
import math

import numpy as np
import jax
import jax.numpy as jnp
from jax import lax
from jax.experimental import pallas as pl
from jax.experimental.pallas import tpu as pltpu

F32 = jnp.float32
BF16 = jnp.bfloat16

D_MODEL = 2048
W_A = 1024
DH_A = 128
H_A = 8
D_C = 256
H_IDX = 16
D_IDX = 64
TOPK = 256
W_M = 1024
H_M = 4
DV_M = 256
DK_M = 128
CONV_W = 4
N_BUCKETS = 32
MAX_DIST = 128
ALPHA = 2.0 ** 0.25
LN_EPS = 1e-5

LANES = 128
SUBLANES = 8
VMEM_LIMIT = 56 * 1024 * 1024

QB = 256
KB = 256
KI = 128
CNT_ROWS = 256
CNT_ACC = 4 * SUBLANES
L_M = 256
HALO = 16
NEG = -1e30
LOG2E = math.log2(math.e)
ONES_ROWS = 16

C_QA, C_ZA, C_QI, C_QM, C_KM, C_VM, C_OM, C_ZM = 0, 1024, 2048, 3072, 3584, 4096, 5120, 6144
N_MAIN = 7168
N_SMALL = 384
S_KI, S_WI, S_IM, S_FM = 0, 64, 80, 84


def _t5_bucket_np(rel):
    max_exact = N_BUCKETS // 2
    n = np.maximum(rel, 0)
    nf = np.maximum(n, 1).astype(np.float32)
    large = max_exact + (np.log(nf / np.float32(max_exact)) / np.float32(math.log(MAX_DIST / max_exact))
                         * np.float32(N_BUCKETS - max_exact)).astype(np.int32)
    large = np.minimum(large, N_BUCKETS - 1)
    return np.where(n < max_exact, n, large).astype(np.int32)


FAR_BUCKET = int(_t5_bucket_np(np.array(2 * KB + 1)))


def _bucket_tiles():
    i = np.arange(QB)[None, :]
    j = np.arange(KB)[:, None]
    t0 = _t5_bucket_np(i - j)
    t1 = _t5_bucket_np(i - j + KB)
    assert (t5 := _t5_bucket_np(np.arange(KB + 1, 4096))).min() == t5.max() == FAR_BUCKET
    return np.stack([t0, t1]).astype(np.int32)


def _bias_kernel(bucket_ref, rb_ref, out_ref):
    h = pl.program_id(0)
    far = rb_ref[FAR_BUCKET, h]
    for k in range(2):
        bk = bucket_ref[k]
        acc = jnp.zeros((KB, QB), F32)
        for b in range(N_BUCKETS):
            acc = jnp.where(bk == b, rb_ref[b, h] - far, acc)
        out_ref[0, k] = acc * LOG2E


def _bias_tiles(rel_bias):
    bucket = jnp.asarray(_bucket_tiles())
    return pl.pallas_call(
        _bias_kernel,
        grid=(H_A,),
        in_specs=[pl.BlockSpec((2, KB, QB), lambda h: (0, 0, 0)),
                  pl.BlockSpec(memory_space=pltpu.SMEM)],
        out_specs=pl.BlockSpec((1, 2, KB, QB), lambda h: (h, 0, 0, 0)),
        out_shape=jax.ShapeDtypeStruct((H_A, 2, KB, QB), F32),
        name="bias_tiles",
    )(bucket, rel_bias)


_NT = (((1,), (1,)), ((), ()))


def _proj_kernel(x_ref, w_ref, ws_ref, o_ref, os_ref, xb_ref):
    @pl.when(pl.program_id(1) == 0)
    def _():
        xb_ref[...] = x_ref[...].astype(BF16)
        os_ref[...] = lax.dot_general(xb_ref[...], ws_ref[...], _NT, preferred_element_type=F32)

    o_ref[...] = lax.dot_general(xb_ref[...], w_ref[...], _NT, preferred_element_type=F32).astype(BF16)


def _proj(x2d, w_main, w_small, tm=1024, tn=1024):
    M = x2d.shape[0]
    return pl.pallas_call(
        _proj_kernel,
        grid=(M // tm, N_MAIN // tn),
        in_specs=[pl.BlockSpec((tm, D_MODEL), lambda i, j: (i, 0)),
                  pl.BlockSpec((tn, D_MODEL), lambda i, j: (j, 0)),
                  pl.BlockSpec((N_SMALL, D_MODEL), lambda i, j: (0, 0))],
        out_specs=[pl.BlockSpec((tm, tn), lambda i, j: (i, j)),
                   pl.BlockSpec((tm, N_SMALL), lambda i, j: (i, 0))],
        out_shape=[jax.ShapeDtypeStruct((M, N_MAIN), BF16),
                   jax.ShapeDtypeStruct((M, N_SMALL), F32)],
        scratch_shapes=[pltpu.VMEM((tm, D_MODEL), BF16)],
        compiler_params=pltpu.CompilerParams(
            dimension_semantics=("arbitrary", "arbitrary"), vmem_limit_bytes=VMEM_LIMIT),
        name="proj",
    )(x2d, w_main, w_small)


def _prep_kernel(s_ref, kvg_ref, ig_ref, ib_ref, ckv_ref, ckvt_ref, kidx_ref, gt_ref):
    c = s_ref[:, 0:D_C]
    c = c * lax.rsqrt(jnp.mean(c * c, axis=-1, keepdims=True) + LN_EPS) * kvg_ref[...]
    ckv_ref[...] = c.astype(BF16)
    for r in range(ckvt_ref.shape[0]):
        ckvt_ref[r, 0:D_C, :] = c[r * KB:(r + 1) * KB, :].T.astype(BF16)
        ckvt_ref[r, D_C:D_C + ONES_ROWS, :] = jnp.ones((ONES_ROWS, KB), BF16)
    tile = s_ref[:, D_C:D_C + LANES]
    k = tile[:, S_KI:S_KI + D_IDX]
    mu = jnp.mean(k, axis=-1, keepdims=True)
    var = jnp.mean(jnp.square(k - mu), axis=-1, keepdims=True)
    kidx_ref[...] = ((k - mu) * lax.rsqrt(var + LN_EPS) * ig_ref[...] + ib_ref[...]).astype(BF16)
    gt_ref[...] = tile.T


def _prep(small, kv_g, idx_g, idx_b, tm=1024):
    M = small.shape[0]
    return pl.pallas_call(
        _prep_kernel,
        grid=(M // tm,),
        in_specs=[pl.BlockSpec((tm, N_SMALL), lambda i: (i, 0)),
                  pl.BlockSpec((1, D_C), lambda i: (0, 0)),
                  pl.BlockSpec((1, D_IDX), lambda i: (0, 0)),
                  pl.BlockSpec((1, D_IDX), lambda i: (0, 0))],
        out_specs=[pl.BlockSpec((tm, D_C), lambda i: (i, 0)),
                   pl.BlockSpec((tm // KB, D_C + ONES_ROWS, KB), lambda i: (i, 0, 0)),
                   pl.BlockSpec((tm, D_IDX), lambda i: (i, 0)),
                   pl.BlockSpec((LANES, tm), lambda i: (0, i))],
        out_shape=[jax.ShapeDtypeStruct((M, D_C), BF16),
                   jax.ShapeDtypeStruct((M // KB, D_C + ONES_ROWS, KB), BF16),
                   jax.ShapeDtypeStruct((M, D_IDX), BF16),
                   jax.ShapeDtypeStruct((LANES, M), F32)],
        name="prep",
    )(small, kv_g, idx_g, idx_b)


def _key_to_float(key):
    bits = jnp.where(key < 0, key ^ jnp.int32(0x7FFFFFFF), key)
    return lax.bitcast_convert_type(bits, F32)


def _dsa_kernel(qa_ref, za_ref, qi_ref, gt_ref, ckv_ref, ckvt_ref, kidx_ref, wukt_ref, wuvt_ref, bias_ref,
                y_ref, qall_ref, qr_ref, ha_ref, hb_ref, sc_ref, mb_ref, sa_ref, sb_ref, pa_ref, pb_ref,
                ta_ref, tb_ref, m_ref, al_ref, acc_ref, w_ref, cand_ref, thr_ref, cut_ref, cge_ref):
    qi = pl.program_id(1)
    nkb = qi + 1
    last_kb = mb_ref.shape[0] - 1

    for h in range(H_A):
        ql = lax.dot_general(wukt_ref[h], qa_ref[:, h * DH_A:(h + 1) * DH_A], (((1,), (1,)), ((), ())),
                             preferred_element_type=F32)
        qall_ref[h // (H_A // 2), :, (h % (H_A // 2)) * QB:(h % (H_A // 2) + 1) * QB] = (
            ql * (DH_A ** -0.5 * LOG2E)).astype(BF16)


    @pl.when(qi * QB < TOPK)
    def _():
        krow = lax.broadcasted_iota(jnp.int32, (KB, QB), 0)
        qcol = lax.broadcasted_iota(jnp.int32, (KB, QB), 1)
        mb_ref[0] = jnp.where(krow <= qcol, 0.0, NEG).astype(F32)

    @pl.when(qi * QB >= TOPK)
    def _():
        for h in range(H_IDX):
            qr_ref[h * QB:(h + 1) * QB, :] = qi_ref[:, h * D_IDX:(h + 1) * D_IDX]
        w_ref[...] = gt_ref[S_WI:S_WI + H_IDX, :] * ((D_IDX ** -0.5) * (H_IDX ** -0.5))
        krow = lax.broadcasted_iota(jnp.int32, (KI, LANES), 0)
        qcol = lax.broadcasted_iota(jnp.int32, (KI, LANES), 1)
        n_ki = nkb * (KB // KI)
        last_ki = sc_ref.shape[0] // KI - 1

        def head_dots(ki, dst_ref):
            k = kidx_ref[pl.ds(pl.multiple_of(jnp.minimum(ki, last_ki) * KI, KI), KI), :]
            dst_ref[...] = lax.dot_general(k, qr_ref[...], (((1,), (1,)), ((), ())),
                                           preferred_element_type=F32)

        def reduce_heads(src_ref, ki):
            for g in range(QB // LANES):
                lanes = slice(g * LANES, (g + 1) * LANES)
                acc = jnp.zeros((KI, LANES), F32)
                for h in range(H_IDX):
                    acc = acc + (jnp.maximum(src_ref[:, h * QB + g * LANES:h * QB + (g + 1) * LANES], 0.0)
                                 * w_ref[h:h + 1, lanes])
                sc_ref[pl.ds(pl.multiple_of(ki * KI, KI), KI), lanes] = jnp.where(
                    krow + (ki * KI - qi * QB - g * LANES) <= qcol, acc, -jnp.inf)

        pad_kb = jnp.minimum(nkb, last_kb)
        sc_ref[pl.ds(pl.multiple_of(pad_kb * KB, KB), KB), :] = jnp.full((KB, QB), -jnp.inf, F32)

        head_dots(0, ha_ref)

        def sc_body(j, carry):
            head_dots(2 * j + 1, hb_ref)
            reduce_heads(ha_ref, 2 * j)
            head_dots(2 * j + 2, ha_ref)
            reduce_heads(hb_ref, 2 * j + 1)
            return carry
        lax.fori_loop(0, n_ki // 2, sc_body, 0)

        n_cnt = (nkb * KB + CNT_ROWS - 1) // CNT_ROWS

        def count_where(pred, steps=None):
            def body(c, acc):
                parts = []
                for g in range(QB // LANES):
                    lanes = slice(g * LANES, (g + 1) * LANES)
                    if steps is None:
                        blk = sc_ref[pl.ds(pl.multiple_of(c * CNT_ROWS, CNT_ROWS), CNT_ROWS), lanes]
                    else:
                        blk = sc_ref[c * CNT_ROWS:(c + 1) * CNT_ROWS, lanes]
                    hit = jnp.where(pred(blk, c * CNT_ROWS, lanes), 1.0, 0.0).astype(F32)
                    parts.append(jnp.sum(hit.reshape(CNT_ROWS // CNT_ACC, CNT_ACC, LANES), axis=0))
                return acc + jnp.concatenate(parts, axis=1)
            acc = jnp.zeros((CNT_ACC, QB), F32)
            if steps is None:
                acc = lax.fori_loop(0, n_cnt, body, acc)
            else:
                for c in range(steps):
                    acc = body(c, acc)
            return jnp.sum(acc, axis=0, keepdims=True)

        def search(steps):
            def bit_body(i, carry):
                u, c_ge = carry
                trial = u | lax.shift_left(jnp.int32(1), 31 - i)
                cand_ref[...] = _key_to_float(trial ^ jnp.int32(-2 ** 31))
                cnt = count_where(lambda blk, row0, lanes: blk >= cand_ref[:, lanes], steps)
                ok = cnt >= float(TOPK)
                return jnp.where(ok, trial, u), jnp.where(ok, cnt, c_ge)
            u, c_ge = lax.fori_loop(0, 32, bit_body, (jnp.zeros((1, QB), jnp.int32),
                                                      jnp.full((1, QB), float(sc_ref.shape[0]), F32)))
            thr_ref[...] = _key_to_float(u ^ jnp.int32(-2 ** 31))
            cge_ref[...] = c_ge

        for steps in range(1, sc_ref.shape[0] // CNT_ROWS + 1):
            pl.when(n_cnt == steps)(functools.partial(search, steps))
        has_ties = jnp.max(cge_ref[...]) > float(TOPK)

        @pl.when(jnp.logical_not(has_ties))
        def _():
            def mb_body(kb, carry):
                blk = sc_ref[pl.ds(pl.multiple_of(kb * KB, KB), KB), :]
                mb_ref[kb] = jnp.where(blk >= thr_ref[...], 0.0, NEG).astype(F32)
                return carry
            lax.fori_loop(0, nkb, mb_body, 0)

        @pl.when(has_ties)
        def _():
            c_gt = count_where(lambda blk, row0, lanes: blk > thr_ref[:, lanes])
            need = float(TOPK) - c_gt
            rows = lax.broadcasted_iota(jnp.int32, (CNT_ROWS, LANES), 0)
            n_bits = (sc_ref.shape[0] - 1).bit_length()

            def idx_body(i, cut):
                trial = cut | lax.shift_left(jnp.int32(1), n_bits - 1 - i)
                cut_ref[...] = trial
                before = count_where(lambda blk, row0, lanes: (blk == thr_ref[:, lanes])
                                     & (rows + row0 < cut_ref[:, lanes]))
                return jnp.where(before < need, trial, cut)
            cut_ref[...] = lax.fori_loop(0, n_bits, idx_body, jnp.zeros((1, QB), jnp.int32))

            def mb_body(kb, carry):
                for g in range(QB // LANES):
                    lanes = slice(g * LANES, (g + 1) * LANES)
                    blk = sc_ref[pl.ds(pl.multiple_of(kb * KB, KB), KB), lanes]
                    thr = thr_ref[:, lanes]
                    keep = (blk > thr) | ((blk == thr) & (rows[0:KB] + kb * KB <= cut_ref[:, lanes]))
                    mb_ref[kb, :, lanes] = jnp.where(keep, 0.0, NEG).astype(F32)
                return carry
            lax.fori_loop(0, nkb, mb_body, 0)

    m_ref[...] = jnp.full(m_ref.shape, NEG, F32)
    acc_ref[...] = jnp.zeros(acc_ref.shape, F32)

    hh = H_A // 2
    s_refs, p_refs, t_refs = (sa_ref, sb_ref), (pa_ref, pb_ref), (ta_ref, tb_ref)
    pb_ref[...] = jnp.zeros(pb_ref.shape, BF16)
    al_ref[...] = jnp.ones(al_ref.shape, F32)

    def logits(kb, half):
        kv = ckv_ref[pl.ds(pl.multiple_of(jnp.minimum(kb, last_kb) * KB, KB), KB), :]
        s_refs[half][...] = jnp.dot(kv, qall_ref[half], preferred_element_type=F32)

    def softmax(kb, half, with_bias):
        tile = jnp.clip(qi - kb, 0, 1)
        for j in range(hh):
            for g in range(QB // LANES):
                lanes = slice(g * LANES, (g + 1) * LANES)
                cols = slice(j * QB + g * LANES, j * QB + (g + 1) * LANES)
                x = s_refs[half][:, cols] + mb_ref[kb, :, lanes]
                if with_bias:
                    x = x + bias_ref[half * hh + j, tile, :, lanes]
                m_prev = m_ref[half, :, cols]
                m_blk = jnp.max(x.reshape(KB // CNT_ACC, CNT_ACC, LANES), axis=0)
                m_new = jnp.maximum(m_prev, jnp.max(m_blk, axis=0, keepdims=True))
                al_ref[half, :, cols] = jnp.exp2(m_prev - m_new)
                m_ref[half, :, cols] = m_new
                p_refs[half][:, cols] = jnp.exp2(x - m_new).astype(BF16)

    def accumulate(kb, half):
        t_refs[half][...] = jnp.dot(ckvt_ref[jnp.maximum(kb, 0)], p_refs[half][...],
                                    preferred_element_type=F32)
        acc_ref[half] = acc_ref[half] * al_ref[half] + t_refs[half][...]

    def sweep(first_kb, end_kb, with_bias):
        def body(kb, carry):
            logits(kb, 1)
            softmax(kb, 0, with_bias)
            accumulate(kb - 1, 1)
            logits(kb + 1, 0)
            softmax(kb, 1, with_bias)
            accumulate(kb, 0)
            return carry
        lax.fori_loop(first_kb, end_kb, body, 0)

    n_far = jnp.maximum(qi - 1, 0)
    logits(0, 0)
    sweep(0, n_far, False)
    sweep(n_far, nkb, True)
    accumulate(nkb - 1, 1)

    for h in range(H_A):
        half, cols = h // hh, slice((h % hh) * QB, (h % hh + 1) * QB)
        ya_t = jnp.dot(wuvt_ref[h], acc_ref[half, 0:D_C, cols].astype(BF16), preferred_element_type=F32)
        ya = (ya_t / acc_ref[half, D_C:D_C + 1, cols]).T
        z = za_ref[:, h * DH_A:(h + 1) * DH_A].astype(F32)
        y_ref[:, h * DH_A:(h + 1) * DH_A] = (ya * (z * jax.nn.sigmoid(z))).astype(BF16)


def _dsa(main, gate_t, ckv_n, ckv_t, kidx_n, w_uk_t, w_uv_t, bias, B, T):
    nq = T // QB
    return pl.pallas_call(
        _dsa_kernel,
        grid=(B, nq),
        in_specs=[pl.BlockSpec((QB, W_A), lambda b, q: (b * nq + q, C_QA // W_A)),
                  pl.BlockSpec((QB, W_A), lambda b, q: (b * nq + q, C_ZA // W_A)),
                  pl.BlockSpec((QB, H_IDX * D_IDX), lambda b, q: (b * nq + q, C_QI // (H_IDX * D_IDX))),
                  pl.BlockSpec((LANES, QB), lambda b, q: (0, b * nq + q)),
                  pl.BlockSpec((T, D_C), lambda b, q: (b, 0)),
                  pl.BlockSpec((T // KB, D_C + ONES_ROWS, KB), lambda b, q: (b, 0, 0)),
                  pl.BlockSpec((T, D_IDX), lambda b, q: (b, 0)),
                  pl.BlockSpec((H_A, D_C, DH_A), lambda b, q: (0, 0, 0)),
                  pl.BlockSpec((H_A, DH_A, D_C), lambda b, q: (0, 0, 0)),
                  pl.BlockSpec((H_A, 2, KB, QB), lambda b, q: (0, 0, 0, 0))],
        out_specs=pl.BlockSpec((QB, W_A), lambda b, q: (b * nq + q, 0)),
        out_shape=jax.ShapeDtypeStruct((B * T, W_A), BF16),
        scratch_shapes=[pltpu.VMEM((2, D_C, H_A // 2 * QB), BF16),
                        pltpu.VMEM((H_IDX * QB, D_IDX), BF16),
                        pltpu.VMEM((KI, H_IDX * QB), F32),
                        pltpu.VMEM((KI, H_IDX * QB), F32),
                        pltpu.VMEM((T, QB), F32),
                        pltpu.VMEM((T // KB, KB, QB), F32),
                        pltpu.VMEM((KB, H_A // 2 * QB), F32),
                        pltpu.VMEM((KB, H_A // 2 * QB), F32),
                        pltpu.VMEM((KB, H_A // 2 * QB), BF16),
                        pltpu.VMEM((KB, H_A // 2 * QB), BF16),
                        pltpu.VMEM((D_C + ONES_ROWS, H_A // 2 * QB), F32),
                        pltpu.VMEM((D_C + ONES_ROWS, H_A // 2 * QB), F32),
                        pltpu.VMEM((2, 1, H_A // 2 * QB), F32),
                        pltpu.VMEM((2, 1, H_A // 2 * QB), F32),
                        pltpu.VMEM((2, D_C + ONES_ROWS, H_A // 2 * QB), F32),
                        pltpu.VMEM((H_IDX, QB), F32),
                        pltpu.VMEM((1, QB), F32),
                        pltpu.VMEM((1, QB), F32),
                        pltpu.VMEM((1, QB), jnp.int32),
                        pltpu.VMEM((1, QB), F32)],
        compiler_params=pltpu.CompilerParams(
            dimension_semantics=("arbitrary", "arbitrary"), vmem_limit_bytes=VMEM_LIMIT),
        name="dsa",
    )(main, main, main, gate_t, ckv_n, ckv_t, kidx_n, w_uk_t, w_uv_t, bias)


def _split_dot(tri, x):
    hi = x.astype(BF16)
    lo = (x - hi.astype(F32)).astype(BF16)
    return jnp.dot(tri, hi, preferred_element_type=F32) + jnp.dot(tri, lo, preferred_element_type=F32)


def _log_sigmoid(x):
    return jnp.minimum(x, 0.0) - jnp.log1p(jnp.exp(-jnp.abs(x)))


def _mlstm_out_kernel(q_ref, k_ref, qh_ref, kh_ref, v_ref, o_ref, z_ref, g_ref, gt_ref,
                      cw_ref, cb_ref, gbr_ref, gbc_ref, ng_ref, ya_ref, x_ref, w_ref, lg_ref, lb_ref,
                      out_ref, ct_ref, m_ref, y_ref, yp_ref, p_ref, *, n_chunks, n_steps):
    step = pl.program_id(0)
    c = jnp.minimum(step, n_steps - 1) % n_chunks
    L = L_M

    @pl.when(step == 0)
    def _():
        yp_ref[...] = jnp.zeros(yp_ref.shape, BF16)

    @pl.when(c == 0)
    def _():
        ct_ref[...] = jnp.zeros(ct_ref.shape, F32)
        m_ref[...] = jnp.zeros(m_ref.shape, F32)

    n_pc = 2 * H_M
    pw = D_MODEL // n_pc

    def project_chunk(j):
        cols = slice(j * pw, (j + 1) * pw)
        p_ref[:, cols] = (jnp.dot(ya_ref[...], w_ref[0:W_A, cols], preferred_element_type=F32)
                          + jnp.dot(yp_ref[...], w_ref[W_A:W_A + W_M, cols], preferred_element_type=F32))

    def norm_previous():
        res = ALPHA * x_ref[...] + p_ref[...]
        mean = jnp.mean(res, axis=-1, keepdims=True)
        var_r = jnp.mean(jnp.square(res - mean), axis=-1, keepdims=True)
        out_ref[...] = (res - mean) * lax.rsqrt(var_r + LN_EPS) * lg_ref[...] + lb_ref[...]

    pending = [functools.partial(project_chunk, j) for j in range(n_pc)] + [norm_previous]

    def emit_projection_work():
        if pending:
            pending.pop(0)()

    emit_projection_work()

    r = lax.broadcasted_iota(jnp.int32, (L, L), 0)
    s = lax.broadcasted_iota(jnp.int32, (L, L), 1)
    causal = s <= r
    shifts = [jnp.where(r - s == d, 1.0, 0.0).astype(BF16) for d in range(1, CONV_W)]

    def conv_silu(x_ref, halo_ref, lo):
        x = x_ref[...]
        halo = jnp.where(c > 0, halo_ref[...].astype(F32), 0.0)
        w = cw_ref[:, lo:lo + H_M * DK_M]
        y = cb_ref[:, lo:lo + H_M * DK_M] + w[CONV_W - 1:CONV_W] * x.astype(F32)
        top = jnp.zeros((SUBLANES, H_M * DK_M), F32)
        for d in range(1, CONV_W):
            wd = w[CONV_W - 1 - d:CONV_W - d]
            y = y + wd * jnp.dot(shifts[d - 1], x, preferred_element_type=F32)
            top = top + wd * jnp.concatenate(
                [halo[HALO - d:HALO], jnp.zeros((SUBLANES - d, H_M * DK_M), F32)], axis=0)
        y = jnp.concatenate([y[0:SUBLANES] + top, y[SUBLANES:]], axis=0)
        return y * jax.nn.sigmoid(y)

    q_all = conv_silu(q_ref, qh_ref, 0)
    emit_projection_work()
    k_all = conv_silu(k_ref, kh_ref, H_M * DK_M) * (DK_M ** -0.5)
    emit_projection_work()

    gc = g_ref[...] + gbr_ref[...]
    gr = gt_ref[S_IM:S_IM + 2 * H_M, :] + gbc_ref[S_IM:S_IM + 2 * H_M, :]
    tri_l = jnp.where(causal, 1.0, 0.0).astype(BF16)
    tri_u = jnp.where(r <= s, 1.0, 0.0).astype(BF16)
    b_cols = _split_dot(tri_l, _log_sigmoid(gc) * LOG2E)
    lf_rows = _log_sigmoid(gr) * LOG2E
    b_rows = jnp.dot(lf_rows.astype(BF16), tri_u, preferred_element_type=F32) \
        + jnp.dot((lf_rows - lf_rows.astype(BF16).astype(F32)).astype(BF16), tri_u,
                  preferred_element_type=F32)
    gc = gc * LOG2E
    gr = gr * LOG2E
    ones = jnp.ones((L, LANES), BF16)
    emit_projection_work()

    for h in range(H_M):
        emit_projection_work()
        q = q_all[:, h * DK_M:(h + 1) * DK_M]
        k = k_all[:, h * DK_M:(h + 1) * DK_M]
        v = jnp.concatenate([v_ref[:, h * DV_M:(h + 1) * DV_M], ones], axis=1)
        qb = q.astype(BF16)
        b_c = b_cols[:, S_FM + h:S_FM + h + 1]
        i_c = gc[:, S_IM + h:S_IM + h + 1]
        b_r = b_rows[H_M + h:H_M + h + 1, :]
        i_r = gr[h:h + 1, :]
        m_prev = m_ref[h]
        ct = ct_ref[h]

        log_d = jnp.where(causal, b_c - b_r + i_r, -jnp.inf)
        g = b_c + m_prev
        m_t = jnp.maximum(jnp.max(log_d, axis=-1, keepdims=True), g)
        qk = lax.dot_general(qb, k.astype(BF16), _NT, preferred_element_type=F32)
        s_mat = qk * jnp.exp2(log_d - m_t)
        inter = jnp.exp2(g - m_t)
        num = jnp.dot(s_mat.astype(BF16), v, preferred_element_type=F32) \
            + inter * jnp.dot(qb, ct.astype(BF16), preferred_element_type=F32)
        den = jnp.maximum(jnp.abs(num[:, DV_M:]), jnp.exp2(-m_t))
        hh = num[:, 0:DV_M] / jnp.concatenate([den] * (DV_M // LANES), axis=1)

        emit_projection_work()
        b_last = b_c[L - 1:L, :]
        a_r = b_last - b_r + i_r
        m_new = jnp.maximum(b_last + m_prev, jnp.max(a_r, axis=-1, keepdims=True))
        decay = jnp.exp2(b_last + m_prev - m_new)
        wgt_c = jnp.exp2(b_last - b_c + i_c - m_new)
        kw = k * wgt_c
        ct_ref[h] = decay * ct + jnp.dot(kw.T.astype(BF16), v, preferred_element_type=F32)
        m_ref[h] = m_new

        mu = jnp.mean(hh, axis=-1, keepdims=True)
        var = jnp.mean(jnp.square(hh - mu), axis=-1, keepdims=True)
        hn = (hh - mu) * lax.rsqrt(var + LN_EPS) * ng_ref[:, h * DV_M:(h + 1) * DV_M]
        og = o_ref[:, h * DV_M:(h + 1) * DV_M].astype(F32)
        zg = z_ref[:, h * DV_M:(h + 1) * DV_M].astype(F32)
        y_ref[:, h * DV_M:(h + 1) * DV_M] = (hn * jax.nn.sigmoid(og) * (zg * jax.nn.sigmoid(zg))).astype(BF16)

    assert not pending
    yp_ref[...] = y_ref[...]


def _mlstm_out(main, small, gate_t, conv_w, conv_b, gb_row, gb_col, norm_g, ya, x2d, w_out, ln_g, ln_b, B, T):
    nc = T // L_M
    hb = L_M // HALO
    qk_w = H_M * DK_M
    n_steps = B * nc

    def cur(col):
        return lambda s: (jnp.minimum(s, n_steps - 1), col)

    def prev(s):
        return (jnp.maximum(s - 1, 0), 0)

    def halo_map(col):
        return lambda s: (jnp.maximum(jnp.minimum(s, n_steps - 1) * hb - 1, 0), col)

    const = lambda s: (0, 0)
    return pl.pallas_call(
        functools.partial(_mlstm_out_kernel, n_chunks=nc, n_steps=n_steps),
        grid=(n_steps + 1,),
        in_specs=[pl.BlockSpec((L_M, qk_w), cur(C_QM // qk_w)),
                  pl.BlockSpec((L_M, qk_w), cur(C_KM // qk_w)),
                  pl.BlockSpec((HALO, qk_w), halo_map(C_QM // qk_w)),
                  pl.BlockSpec((HALO, qk_w), halo_map(C_KM // qk_w)),
                  pl.BlockSpec((L_M, W_M), cur(C_VM // W_M)),
                  pl.BlockSpec((L_M, W_M), cur(C_OM // W_M)),
                  pl.BlockSpec((L_M, W_M), cur(C_ZM // W_M)),
                  pl.BlockSpec((L_M, LANES), cur(D_C // LANES)),
                  pl.BlockSpec((LANES, L_M), lambda s: (0, jnp.minimum(s, n_steps - 1))),
                  pl.BlockSpec((CONV_W, 2 * qk_w), const),
                  pl.BlockSpec((1, 2 * qk_w), const),
                  pl.BlockSpec((1, LANES), const),
                  pl.BlockSpec((LANES, 1), const),
                  pl.BlockSpec((1, W_M), const),
                  pl.BlockSpec((L_M, W_A), prev),
                  pl.BlockSpec((L_M, D_MODEL), prev),
                  pl.BlockSpec((W_A + W_M, D_MODEL), const),
                  pl.BlockSpec((1, D_MODEL), const),
                  pl.BlockSpec((1, D_MODEL), const)],
        out_specs=pl.BlockSpec((L_M, D_MODEL), prev),
        out_shape=jax.ShapeDtypeStruct((B * T, D_MODEL), F32),
        scratch_shapes=[pltpu.VMEM((H_M, DK_M, DV_M + LANES), F32),
                        pltpu.VMEM((H_M, 1, 1), F32),
                        pltpu.VMEM((L_M, W_M), BF16),
                        pltpu.VMEM((L_M, W_M), BF16),
                        pltpu.VMEM((L_M, D_MODEL), F32)],
        compiler_params=pltpu.CompilerParams(
            dimension_semantics=("arbitrary",), vmem_limit_bytes=VMEM_LIMIT),
        name="mlstm_out",
    )(main, main, main, main, main, main, main, small, gate_t,
      conv_w, conv_b, gb_row, gb_col, norm_g, ya, x2d, w_out, ln_g, ln_b)


_W_IN_SEGS = (("q_a", W_A), ("c_kv", D_C), ("z_a", W_A), ("q_i", H_IDX * D_IDX), ("k_i", D_IDX),
              ("w_i", H_IDX), ("q_m", H_M * DK_M), ("k_m", H_M * DK_M), ("v_m", W_M), ("i_m", H_M),
              ("f_m", H_M), ("o_m", W_M), ("z_m", W_M))
_MAIN_ORDER = ("q_a", "z_a", "q_i", "q_m", "k_m", "v_m", "o_m", "z_m")
_SEG_NAMES = [name for name, _ in _W_IN_SEGS]
assert _SEG_NAMES.index("f_m") == _SEG_NAMES.index("i_m") + 1 and S_FM == S_IM + H_M


def _repack_kernel(wt_ref, main_ref, small_ref):
    src, off = {}, 0
    for name, width in _W_IN_SEGS:
        src[name] = (off, width)
        off += width
    dst = 0
    for name in _MAIN_ORDER:
        lo, width = src[name]
        main_ref[dst:dst + width, :] = wt_ref[lo:lo + width, :].astype(BF16)
        dst += width
    parts = [wt_ref[src[name][0]:src[name][0] + src[name][1], :] for name in ("c_kv", "k_i", "w_i")]
    lo = src["i_m"][0]
    parts.append(wt_ref[lo:lo + 2 * H_M, :])
    used = sum(p.shape[0] for p in parts)
    parts.append(jnp.zeros((N_SMALL - used, wt_ref.shape[1]), F32))
    small_ref[...] = jnp.concatenate(parts, axis=0).astype(BF16)


def _repack_w_in(w_in, tc=256):
    n_cols = sum(width for _, width in _W_IN_SEGS)
    wt = jnp.swapaxes(w_in, 1, 2)[0]
    return pl.pallas_call(
        _repack_kernel,
        grid=(D_MODEL // tc,),
        in_specs=[pl.BlockSpec((n_cols, tc), lambda i: (0, i))],
        out_specs=[pl.BlockSpec((N_MAIN, tc), lambda i: (0, i)),
                   pl.BlockSpec((N_SMALL, tc), lambda i: (0, i))],
        out_shape=[jax.ShapeDtypeStruct((N_MAIN, D_MODEL), BF16),
                   jax.ShapeDtypeStruct((N_SMALL, D_MODEL), BF16)],
        compiler_params=pltpu.CompilerParams(
            dimension_semantics=("arbitrary",), vmem_limit_bytes=VMEM_LIMIT),
        name="repack",
    )(wt)


def kernel(x, w_in, b_igate, b_fgate, kv_norm_g, w_uk, w_uv, idx_k_ln_g, idx_k_ln_b, rel_bias,
           conv_w, conv_b, mh_norm_g, w_out, ln_g, ln_b):
    B, T, D = x.shape
    assert D == D_MODEL and T % L_M == 0 and T % (2 * KB) == 0 and w_in.shape[0] == 1
    bias = _bias_tiles(rel_bias)
    x2d = x.reshape(B * T, D)
    w_main, w_small = _repack_w_in(w_in)
    main, small = _proj(x2d, w_main, w_small)
    ckv_n, ckv_t, kidx_n, gate_t = _prep(small, kv_norm_g[0][None], idx_k_ln_g[0][None], idx_k_ln_b[0][None])
    w_uk_t = jnp.transpose(w_uk[0], (0, 2, 1)).astype(BF16)
    w_uv_t = jnp.transpose(w_uv[0], (0, 2, 1)).astype(BF16)
    ya = _dsa(main, gate_t, ckv_n, ckv_t, kidx_n, w_uk_t, w_uv_t, bias, B, T)
    gb = jnp.zeros((LANES,), F32).at[S_IM:S_IM + H_M].set(b_igate[0]).at[S_FM:S_FM + H_M].set(b_fgate[0])
    out = _mlstm_out(main, small, gate_t, conv_w[0], conv_b[0][None], gb[None, :], gb[:, None],
                     mh_norm_g[0][None], ya, x2d, w_out[0].astype(BF16), ln_g[0][None], ln_b[0][None], B, T)
    return out.reshape(B, T, D)
```

```python
import functools
import math

import numpy as np
import jax
import jax.numpy as jnp
from jax import lax
from jax.experimental import pallas as pl
from jax.experimental.pallas import tpu as pltpu

F32 = jnp.float32
BF16 = jnp.bfloat16

D_MODEL = 2048
W_A = 1024
DH_A = 128
H_A = 8
D_C = 256
H_IDX = 16
D_IDX = 64
TOPK = 256
W_M = 1024
H_M = 4
DV_M = 256
DK_M = 128
CONV_W = 4
N_BUCKETS = 32
MAX_DIST = 128
ALPHA = 2.0 ** 0.25
LN_EPS = 1e-5

LANES = 128
SUBLANES = 8
VMEM_LIMIT = 56 * 1024 * 1024

QB = 256
KB = 256
KI = 128
CNT_ROWS = 256
CNT_ACC = 4 * SUBLANES
L_M = 256
HALO = 16
NEG = -1e30
LOG2E = math.log2(math.e)
ONES_ROWS = 16

C_QA, C_ZA, C_QI, C_QM, C_KM, C_VM, C_OM, C_ZM = 0, 1024, 2048, 3072, 3584, 4096, 5120, 6144
N_MAIN = 7168
N_SMALL = 384
S_KI, S_WI, S_IM, S_FM = 0, 64, 80, 84


def _t5_bucket_np(rel):
    max_exact = N_BUCKETS // 2
    n = np.maximum(rel, 0)
    nf = np.maximum(n, 1).astype(np.float32)
    large = max_exact + (np.log(nf / np.float32(max_exact)) / np.float32(math.log(MAX_DIST / max_exact))
                         * np.float32(N_BUCKETS - max_exact)).astype(np.int32)
    large = np.minimum(large, N_BUCKETS - 1)
    return np.where(n < max_exact, n, large).astype(np.int32)


FAR_BUCKET = int(_t5_bucket_np(np.array(2 * KB + 1)))


def _bucket_tiles():
    i = np.arange(QB)[None, :]
    j = np.arange(KB)[:, None]
    t0 = _t5_bucket_np(i - j)
    t1 = _t5_bucket_np(i - j + KB)
    assert (t5 := _t5_bucket_np(np.arange(KB + 1, 4096))).min() == t5.max() == FAR_BUCKET
    return np.stack([t0, t1]).astype(np.int32)


def _bias_kernel(bucket_ref, rb_ref, out_ref):
    h = pl.program_id(0)
    far = rb_ref[FAR_BUCKET, h]
    for k in range(2):
        bk = bucket_ref[k]
        acc = jnp.zeros((KB, QB), F32)
        for b in range(N_BUCKETS):
            acc = jnp.where(bk == b, rb_ref[b, h] - far, acc)
        out_ref[0, k] = acc * LOG2E


def _bias_tiles(rel_bias):
    bucket = jnp.asarray(_bucket_tiles())
    return pl.pallas_call(
        _bias_kernel,
        grid=(H_A,),
        in_specs=[pl.BlockSpec((2, KB, QB), lambda h: (0, 0, 0)),
                  pl.BlockSpec(memory_space=pltpu.SMEM)],
        out_specs=pl.BlockSpec((1, 2, KB, QB), lambda h: (h, 0, 0, 0)),
        out_shape=jax.ShapeDtypeStruct((H_A, 2, KB, QB), F32),
        name="bias_tiles",
    )(bucket, rel_bias)


_NT = (((1,), (1,)), ((), ()))


def _proj_kernel(x_ref, w_ref, ws_ref, o_ref, os_ref, xb_ref):
    @pl.when(pl.program_id(1) == 0)
    def _():
        xb_ref[...] = x_ref[...].astype(BF16)
        os_ref[...] = lax.dot_general(xb_ref[...], ws_ref[...], _NT, preferred_element_type=F32)

    o_ref[...] = lax.dot_general(xb_ref[...], w_ref[...], _NT, preferred_element_type=F32).astype(BF16)


def _proj(x2d, w_main, w_small, tm=1024, tn=1024):
    M = x2d.shape[0]
    return pl.pallas_call(
        _proj_kernel,
        grid=(M // tm, N_MAIN // tn),
        in_specs=[pl.BlockSpec((tm, D_MODEL), lambda i, j: (i, 0)),
                  pl.BlockSpec((tn, D_MODEL), lambda i, j: (j, 0)),
                  pl.BlockSpec((N_SMALL, D_MODEL), lambda i, j: (0, 0))],
        out_specs=[pl.BlockSpec((tm, tn), lambda i, j: (i, j)),
                   pl.BlockSpec((tm, N_SMALL), lambda i, j: (i, 0))],
        out_shape=[jax.ShapeDtypeStruct((M, N_MAIN), BF16),
                   jax.ShapeDtypeStruct((M, N_SMALL), F32)],
        scratch_shapes=[pltpu.VMEM((tm, D_MODEL), BF16)],
        compiler_params=pltpu.CompilerParams(
            dimension_semantics=("arbitrary", "arbitrary"), vmem_limit_bytes=VMEM_LIMIT),
        name="proj",
    )(x2d, w_main, w_small)


def _prep_kernel(s_ref, kvg_ref, ig_ref, ib_ref, ckv_ref, ckvt_ref, kidx_ref, gt_ref):
    c = s_ref[:, 0:D_C]
    c = c * lax.rsqrt(jnp.mean(c * c, axis=-1, keepdims=True) + LN_EPS) * kvg_ref[...]
    ckv_ref[...] = c.astype(BF16)
    for r in range(ckvt_ref.shape[0]):
        ckvt_ref[r, 0:D_C, :] = c[r * KB:(r + 1) * KB, :].T.astype(BF16)
        ckvt_ref[r, D_C:D_C + ONES_ROWS, :] = jnp.ones((ONES_ROWS, KB), BF16)
    tile = s_ref[:, D_C:D_C + LANES]
    k = tile[:, S_KI:S_KI + D_IDX]
    mu = jnp.mean(k, axis=-1, keepdims=True)
    var = jnp.mean(jnp.square(k - mu), axis=-1, keepdims=True)
    kidx_ref[...] = ((k - mu) * lax.rsqrt(var + LN_EPS) * ig_ref[...] + ib_ref[...]).astype(BF16)
    gt_ref[...] = tile.T


def _prep(small, kv_g, idx_g, idx_b, tm=1024):
    M = small.shape[0]
    return pl.pallas_call(
        _prep_kernel,
        grid=(M // tm,),
        in_specs=[pl.BlockSpec((tm, N_SMALL), lambda i: (i, 0)),
                  pl.BlockSpec((1, D_C), lambda i: (0, 0)),
                  pl.BlockSpec((1, D_IDX), lambda i: (0, 0)),
                  pl.BlockSpec((1, D_IDX), lambda i: (0, 0))],
        out_specs=[pl.BlockSpec((tm, D_C), lambda i: (i, 0)),
                   pl.BlockSpec((tm // KB, D_C + ONES_ROWS, KB), lambda i: (i, 0, 0)),
                   pl.BlockSpec((tm, D_IDX), lambda i: (i, 0)),
                   pl.BlockSpec((LANES, tm), lambda i: (0, i))],
        out_shape=[jax.ShapeDtypeStruct((M, D_C), BF16),
                   jax.ShapeDtypeStruct((M // KB, D_C + ONES_ROWS, KB), BF16),
                   jax.ShapeDtypeStruct((M, D_IDX), BF16),
                   jax.ShapeDtypeStruct((LANES, M), F32)],
        name="prep",
    )(small, kv_g, idx_g, idx_b)


def _key_to_float(key):
    bits = jnp.where(key < 0, key ^ jnp.int32(0x7FFFFFFF), key)
    return lax.bitcast_convert_type(bits, F32)


def _dsa_kernel(qa_ref, za_ref, qi_ref, gt_ref, ckv_ref, ckvt_ref, kidx_ref, wukt_ref, wuvt_ref, bias_ref,
                y_ref, qall_ref, qr_ref, ha_ref, hb_ref, sc_ref, mb_ref, sa_ref, sb_ref, pa_ref, pb_ref,
                ta_ref, tb_ref, m_ref, al_ref, acc_ref, w_ref, cand_ref, thr_ref, cut_ref, cge_ref):
    qi = pl.program_id(1)
    nkb = qi + 1
    last_kb = mb_ref.shape[0] - 1

    for h in range(H_A):
        ql = lax.dot_general(wukt_ref[h], qa_ref[:, h * DH_A:(h + 1) * DH_A], (((1,), (1,)), ((), ())),
                             preferred_element_type=F32)
        qall_ref[h // (H_A // 2), :, (h % (H_A // 2)) * QB:(h % (H_A // 2) + 1) * QB] = (
            ql * (DH_A ** -0.5 * LOG2E)).astype(BF16)


    @pl.when(qi * QB < TOPK)
    def _():
        krow = lax.broadcasted_iota(jnp.int32, (KB, QB), 0)
        qcol = lax.broadcasted_iota(jnp.int32, (KB, QB), 1)
        mb_ref[0] = jnp.where(krow <= qcol, 0.0, NEG).astype(F32)

    @pl.when(qi * QB >= TOPK)
    def _():
        for h in range(H_IDX):
            qr_ref[h * QB:(h + 1) * QB, :] = qi_ref[:, h * D_IDX:(h + 1) * D_IDX]
        w_ref[...] = gt_ref[S_WI:S_WI + H_IDX, :] * ((D_IDX ** -0.5) * (H_IDX ** -0.5))
        krow = lax.broadcasted_iota(jnp.int32, (KI, LANES), 0)
        qcol = lax.broadcasted_iota(jnp.int32, (KI, LANES), 1)
        n_ki = nkb * (KB // KI)
        last_ki = sc_ref.shape[0] // KI - 1

        def head_dots(ki, dst_ref):
            k = kidx_ref[pl.ds(pl.multiple_of(jnp.minimum(ki, last_ki) * KI, KI), KI), :]
            dst_ref[...] = lax.dot_general(k, qr_ref[...], (((1,), (1,)), ((), ())),
                                           preferred_element_type=F32)

        def reduce_heads(src_ref, ki):
            for g in range(QB // LANES):
                lanes = slice(g * LANES, (g + 1) * LANES)
                acc = jnp.zeros((KI, LANES), F32)
                for h in range(H_IDX):
                    acc = acc + (jnp.maximum(src_ref[:, h * QB + g * LANES:h * QB + (g + 1) * LANES], 0.0)
                                 * w_ref[h:h + 1, lanes])
                sc_ref[pl.ds(pl.multiple_of(ki * KI, KI), KI), lanes] = jnp.where(
                    krow + (ki * KI - qi * QB - g * LANES) <= qcol, acc, -jnp.inf)

        assert CNT_ROWS == KB

        head_dots(0, ha_ref)

        def sc_body(j, carry):
            head_dots(2 * j + 1, hb_ref)
            reduce_heads(ha_ref, 2 * j)
            head_dots(2 * j + 2, ha_ref)
            reduce_heads(hb_ref, 2 * j + 1)
            return carry
        lax.fori_loop(0, n_ki // 2, sc_body, 0)

        n_cnt = (nkb * KB + CNT_ROWS - 1) // CNT_ROWS

        def count_where(pred, steps=None):
            def body(c, acc):
                parts = []
                for g in range(QB // LANES):
                    lanes = slice(g * LANES, (g + 1) * LANES)
                    rows = CNT_ROWS
                    if steps is None:
                        blk = sc_ref[pl.ds(pl.multiple_of(c * CNT_ROWS, CNT_ROWS), CNT_ROWS), lanes]
                    else:
                        if c == steps - 1:
                            rows = min(CNT_ROWS, (g + 1) * LANES)
                        blk = sc_ref[c * CNT_ROWS:c * CNT_ROWS + rows, lanes]
                    hit = jnp.where(pred(blk, c * CNT_ROWS, lanes), 1.0, 0.0).astype(F32)
                    parts.append(jnp.sum(hit.reshape(rows // CNT_ACC, CNT_ACC, LANES), axis=0))
                return acc + jnp.concatenate(parts, axis=1)
            acc = jnp.zeros((CNT_ACC, QB), F32)
            if steps is None:
                acc = lax.fori_loop(0, n_cnt, body, acc)
            else:
                for c in range(steps):
                    acc = body(c, acc)
            return jnp.sum(acc, axis=0, keepdims=True)

        def search(steps):
            def bit_body(i, carry):
                u, c_ge = carry
                trial = u | lax.shift_left(jnp.int32(1), 31 - i)
                cand_ref[...] = _key_to_float(trial ^ jnp.int32(-2 ** 31))
                cnt = count_where(lambda blk, row0, lanes: blk >= cand_ref[:, lanes], steps)
                ok = cnt >= float(TOPK)
                return jnp.where(ok, trial, u), jnp.where(ok, cnt, c_ge)
            u, c_ge = lax.fori_loop(0, 32, bit_body, (jnp.zeros((1, QB), jnp.int32),
                                                      jnp.full((1, QB), float(sc_ref.shape[0]), F32)))
            thr_ref[...] = _key_to_float(u ^ jnp.int32(-2 ** 31))
            cge_ref[...] = c_ge

        for steps in range(1, sc_ref.shape[0] // CNT_ROWS + 1):
            pl.when(n_cnt == steps)(functools.partial(search, steps))
        has_ties = jnp.max(cge_ref[...]) > float(TOPK)

        @pl.when(jnp.logical_not(has_ties))
        def _():
            def mb_body(kb, carry):
                blk = sc_ref[pl.ds(pl.multiple_of(kb * KB, KB), KB), :]
                mb_ref[kb] = jnp.where(blk >= thr_ref[...], 0.0, NEG).astype(F32)
                return carry
            lax.fori_loop(0, nkb, mb_body, 0)

        @pl.when(has_ties)
        def _():
            c_gt = count_where(lambda blk, row0, lanes: blk > thr_ref[:, lanes])
            need = float(TOPK) - c_gt
            rows = lax.broadcasted_iota(jnp.int32, (CNT_ROWS, LANES), 0)
            n_bits = (sc_ref.shape[0] - 1).bit_length()

            def idx_body(i, cut):
                trial = cut | lax.shift_left(jnp.int32(1), n_bits - 1 - i)
                cut_ref[...] = trial
                before = count_where(lambda blk, row0, lanes: (blk == thr_ref[:, lanes])
                                     & (rows + row0 < cut_ref[:, lanes]))
                return jnp.where(before < need, trial, cut)
            cut_ref[...] = lax.fori_loop(0, n_bits, idx_body, jnp.zeros((1, QB), jnp.int32))

            def mb_body(kb, carry):
                for g in range(QB // LANES):
                    lanes = slice(g * LANES, (g + 1) * LANES)
                    blk = sc_ref[pl.ds(pl.multiple_of(kb * KB, KB), KB), lanes]
                    thr = thr_ref[:, lanes]
                    keep = (blk > thr) | ((blk == thr) & (rows[0:KB] + kb * KB <= cut_ref[:, lanes]))
                    mb_ref[kb, :, lanes] = jnp.where(keep, 0.0, NEG).astype(F32)
                return carry
            lax.fori_loop(0, nkb, mb_body, 0)

    m_ref[...] = jnp.full(m_ref.shape, NEG, F32)
    acc_ref[...] = jnp.zeros(acc_ref.shape, F32)

    hh = H_A // 2
    s_refs, p_refs, t_refs = (sa_ref, sb_ref), (pa_ref, pb_ref), (ta_ref, tb_ref)
    pb_ref[...] = jnp.zeros(pb_ref.shape, BF16)
    al_ref[...] = jnp.ones(al_ref.shape, F32)

    def logits(kb, half):
        kv = ckv_ref[pl.ds(pl.multiple_of(jnp.minimum(kb, last_kb) * KB, KB), KB), :]
        s_refs[half][...] = jnp.dot(kv, qall_ref[half], preferred_element_type=F32)

    def softmax(kb, half, with_bias):
        tile = jnp.clip(qi - kb, 0, 1)
        for j in range(hh):
            for g in range(QB // LANES):
                lanes = slice(g * LANES, (g + 1) * LANES)
                cols = slice(j * QB + g * LANES, j * QB + (g + 1) * LANES)
                x = s_refs[half][:, cols] + mb_ref[kb, :, lanes]
                if with_bias:
                    x = x + bias_ref[half * hh + j, tile, :, lanes]
                m_prev = m_ref[half, :, cols]
                m_blk = jnp.max(x.reshape(KB // CNT_ACC, CNT_ACC, LANES), axis=0)
                m_new = jnp.maximum(m_prev, jnp.max(m_blk, axis=0, keepdims=True))
                al_ref[half, :, cols] = jnp.exp2(m_prev - m_new)
                m_ref[half, :, cols] = m_new
                p_refs[half][:, cols] = jnp.exp2(x - m_new).astype(BF16)

    def accumulate(kb, half):
        t_refs[half][...] = jnp.dot(ckvt_ref[jnp.maximum(kb, 0)], p_refs[half][...],
                                    preferred_element_type=F32)
        acc_ref[half] = acc_ref[half] * al_ref[half] + t_refs[half][...]

    def sweep(first_kb, end_kb, with_bias):
        def body(kb, carry):
            logits(kb, 1)
            softmax(kb, 0, with_bias)
            accumulate(kb - 1, 1)
            logits(kb + 1, 0)
            softmax(kb, 1, with_bias)
            accumulate(kb, 0)
            return carry
        lax.fori_loop(first_kb, end_kb, body, 0)

    n_far = jnp.maximum(qi - 1, 0)
    logits(0, 0)
    sweep(0, n_far, False)
    sweep(n_far, nkb, True)
    accumulate(nkb - 1, 1)

    for h in range(H_A):
        half, cols = h // hh, slice((h % hh) * QB, (h % hh + 1) * QB)
        ya_t = jnp.dot(wuvt_ref[h], acc_ref[half, 0:D_C, cols].astype(BF16), preferred_element_type=F32)
        ya = (ya_t / acc_ref[half, D_C:D_C + 1, cols]).T
        z = za_ref[:, h * DH_A:(h + 1) * DH_A].astype(F32)
        y_ref[:, h * DH_A:(h + 1) * DH_A] = (ya * (z * jax.nn.sigmoid(z))).astype(BF16)


def _dsa(main, gate_t, ckv_n, ckv_t, kidx_n, w_uk_t, w_uv_t, bias, B, T):
    nq = T // QB
    return pl.pallas_call(
        _dsa_kernel,
        grid=(B, nq),
        in_specs=[pl.BlockSpec((QB, W_A), lambda b, q: (b * nq + q, C_QA // W_A)),
                  pl.BlockSpec((QB, W_A), lambda b, q: (b * nq + q, C_ZA // W_A)),
                  pl.BlockSpec((QB, H_IDX * D_IDX), lambda b, q: (b * nq + q, C_QI // (H_IDX * D_IDX))),
                  pl.BlockSpec((LANES, QB), lambda b, q: (0, b * nq + q)),
                  pl.BlockSpec((T, D_C), lambda b, q: (b, 0)),
                  pl.BlockSpec((T // KB, D_C + ONES_ROWS, KB), lambda b, q: (b, 0, 0)),
                  pl.BlockSpec((T, D_IDX), lambda b, q: (b, 0)),
                  pl.BlockSpec((H_A, D_C, DH_A), lambda b, q: (0, 0, 0)),
                  pl.BlockSpec((H_A, DH_A, D_C), lambda b, q: (0, 0, 0)),
                  pl.BlockSpec((H_A, 2, KB, QB), lambda b, q: (0, 0, 0, 0))],
        out_specs=pl.BlockSpec((QB, W_A), lambda b, q: (b * nq + q, 0)),
        out_shape=jax.ShapeDtypeStruct((B * T, W_A), BF16),
        scratch_shapes=[pltpu.VMEM((2, D_C, H_A // 2 * QB), BF16),
                        pltpu.VMEM((H_IDX * QB, D_IDX), BF16),
                        pltpu.VMEM((KI, H_IDX * QB), F32),
                        pltpu.VMEM((KI, H_IDX * QB), F32),
                        pltpu.VMEM((T, QB), F32),
                        pltpu.VMEM((T // KB, KB, QB), F32),
                        pltpu.VMEM((KB, H_A // 2 * QB), F32),
                        pltpu.VMEM((KB, H_A // 2 * QB), F32),
                        pltpu.VMEM((KB, H_A // 2 * QB), BF16),
                        pltpu.VMEM((KB, H_A // 2 * QB), BF16),
                        pltpu.VMEM((D_C + ONES_ROWS, H_A // 2 * QB), F32),
                        pltpu.VMEM((D_C + ONES_ROWS, H_A // 2 * QB), F32),
                        pltpu.VMEM((2, 1, H_A // 2 * QB), F32),
                        pltpu.VMEM((2, 1, H_A // 2 * QB), F32),
                        pltpu.VMEM((2, D_C + ONES_ROWS, H_A // 2 * QB), F32),
                        pltpu.VMEM((H_IDX, QB), F32),
                        pltpu.VMEM((1, QB), F32),
                        pltpu.VMEM((1, QB), F32),
                        pltpu.VMEM((1, QB), jnp.int32),
                        pltpu.VMEM((1, QB), F32)],
        compiler_params=pltpu.CompilerParams(
            dimension_semantics=("arbitrary", "arbitrary"), vmem_limit_bytes=VMEM_LIMIT),
        name="dsa",
    )(main, main, main, gate_t, ckv_n, ckv_t, kidx_n, w_uk_t, w_uv_t, bias)


def _split_dot(tri, x):
    hi = x.astype(BF16)
    lo = (x - hi.astype(F32)).astype(BF16)
    return jnp.dot(tri, hi, preferred_element_type=F32) + jnp.dot(tri, lo, preferred_element_type=F32)


def _log_sigmoid(x):
    return jnp.minimum(x, 0.0) - jnp.log1p(jnp.exp(-jnp.abs(x)))


def _mlstm_out_kernel(q_ref, k_ref, qh_ref, kh_ref, v_ref, o_ref, z_ref, g_ref, gt_ref,
                      cw_ref, cb_ref, gbr_ref, gbc_ref, ng_ref, ya_ref, x_ref, w_ref, lg_ref, lb_ref,
                      out_ref, ct_ref, m_ref, y_ref, yp_ref, p_ref, *, n_chunks, n_steps):
    step = pl.program_id(0)
    c = jnp.minimum(step, n_steps - 1) % n_chunks
    L = L_M

    @pl.when(step == 0)
    def _():
        yp_ref[...] = jnp.zeros(yp_ref.shape, BF16)

    @pl.when(c == 0)
    def _():
        ct_ref[...] = jnp.zeros(ct_ref.shape, F32)
        m_ref[...] = jnp.zeros(m_ref.shape, F32)

    n_pc = 2 * H_M
    pw = D_MODEL // n_pc

    def project_chunk(j):
        cols = slice(j * pw, (j + 1) * pw)
        p_ref[:, cols] = (jnp.dot(ya_ref[...], w_ref[0:W_A, cols], preferred_element_type=F32)
                          + jnp.dot(yp_ref[...], w_ref[W_A:W_A + W_M, cols], preferred_element_type=F32))

    def norm_previous():
        res = ALPHA * x_ref[...] + p_ref[...]
        mean = jnp.mean(res, axis=-1, keepdims=True)
        var_r = jnp.mean(jnp.square(res - mean), axis=-1, keepdims=True)
        out_ref[...] = (res - mean) * lax.rsqrt(var_r + LN_EPS) * lg_ref[...] + lb_ref[...]

    pending = [functools.partial(project_chunk, j) for j in range(n_pc)] + [norm_previous]

    def emit_projection_work():
        if pending:
            pending.pop(0)()

    emit_projection_work()

    r = lax.broadcasted_iota(jnp.int32, (L, L), 0)
    s = lax.broadcasted_iota(jnp.int32, (L, L), 1)
    causal = s <= r
    shifts = [jnp.where(r - s == d, 1.0, 0.0).astype(BF16) for d in range(1, CONV_W)]

    def conv_silu(x_ref, halo_ref, lo):
        x = x_ref[...]
        halo = jnp.where(c > 0, halo_ref[...].astype(F32), 0.0)
        w = cw_ref[:, lo:lo + H_M * DK_M]
        y = cb_ref[:, lo:lo + H_M * DK_M] + w[CONV_W - 1:CONV_W] * x.astype(F32)
        top = jnp.zeros((SUBLANES, H_M * DK_M), F32)
        for d in range(1, CONV_W):
            wd = w[CONV_W - 1 - d:CONV_W - d]
            y = y + wd * jnp.dot(shifts[d - 1], x, preferred_element_type=F32)
            top = top + wd * jnp.concatenate(
                [halo[HALO - d:HALO], jnp.zeros((SUBLANES - d, H_M * DK_M), F32)], axis=0)
        y = jnp.concatenate([y[0:SUBLANES] + top, y[SUBLANES:]], axis=0)
        return y * jax.nn.sigmoid(y)

    q_all = conv_silu(q_ref, qh_ref, 0)
    emit_projection_work()
    k_all = conv_silu(k_ref, kh_ref, H_M * DK_M) * (DK_M ** -0.5)
    emit_projection_work()

    gc = g_ref[...] + gbr_ref[...]
    gr = gt_ref[S_IM:S_IM + 2 * H_M, :] + gbc_ref[S_IM:S_IM + 2 * H_M, :]
    tri_l = jnp.where(causal, 1.0, 0.0).astype(BF16)
    tri_u = jnp.where(r <= s, 1.0, 0.0).astype(BF16)
    b_cols = _split_dot(tri_l, _log_sigmoid(gc) * LOG2E)
    lf_rows = _log_sigmoid(gr) * LOG2E
    b_rows = jnp.dot(lf_rows.astype(BF16), tri_u, preferred_element_type=F32) \
        + jnp.dot((lf_rows - lf_rows.astype(BF16).astype(F32)).astype(BF16), tri_u,
                  preferred_element_type=F32)
    gc = gc * LOG2E
    gr = gr * LOG2E
    ones = jnp.ones((L, LANES), BF16)
    emit_projection_work()

    for h in range(H_M):
        emit_projection_work()
        q = q_all[:, h * DK_M:(h + 1) * DK_M]
        k = k_all[:, h * DK_M:(h + 1) * DK_M]
        v = jnp.concatenate([v_ref[:, h * DV_M:(h + 1) * DV_M], ones], axis=1)
        qb = q.astype(BF16)
        b_c = b_cols[:, S_FM + h:S_FM + h + 1]
        i_c = gc[:, S_IM + h:S_IM + h + 1]
        b_r = b_rows[H_M + h:H_M + h + 1, :]
        i_r = gr[h:h + 1, :]
        m_prev = m_ref[h]
        ct = ct_ref[h]

        log_d = jnp.where(causal, b_c - b_r + i_r, -jnp.inf)
        g = b_c + m_prev
        m_t = jnp.maximum(jnp.max(log_d, axis=-1, keepdims=True), g)
        qk = lax.dot_general(qb, k.astype(BF16), _NT, preferred_element_type=F32)
        s_mat = qk * jnp.exp2(log_d - m_t)
        inter = jnp.exp2(g - m_t)
        num = jnp.dot(s_mat.astype(BF16), v, preferred_element_type=F32) \
            + inter * jnp.dot(qb, ct.astype(BF16), preferred_element_type=F32)
        den = jnp.maximum(jnp.abs(num[:, DV_M:]), jnp.exp2(-m_t))
        hh = num[:, 0:DV_M] / jnp.concatenate([den] * (DV_M // LANES), axis=1)

        emit_projection_work()
        b_last = b_c[L - 1:L, :]
        a_r = b_last - b_r + i_r
        m_new = jnp.maximum(b_last + m_prev, jnp.max(a_r, axis=-1, keepdims=True))
        decay = jnp.exp2(b_last + m_prev - m_new)
        wgt_c = jnp.exp2(b_last - b_c + i_c - m_new)
        kw = k * wgt_c
        ct_ref[h] = decay * ct + jnp.dot(kw.T.astype(BF16), v, preferred_element_type=F32)
        m_ref[h] = m_new

        mu = jnp.mean(hh, axis=-1, keepdims=True)
        var = jnp.mean(jnp.square(hh - mu), axis=-1, keepdims=True)
        hn = (hh - mu) * lax.rsqrt(var + LN_EPS) * ng_ref[:, h * DV_M:(h + 1) * DV_M]
        og = o_ref[:, h * DV_M:(h + 1) * DV_M].astype(F32)
        zg = z_ref[:, h * DV_M:(h + 1) * DV_M].astype(F32)
        y_ref[:, h * DV_M:(h + 1) * DV_M] = (hn * jax.nn.sigmoid(og) * (zg * jax.nn.sigmoid(zg))).astype(BF16)

    assert not pending
    yp_ref[...] = y_ref[...]


def _mlstm_out(main, small, gate_t, conv_w, conv_b, gb_row, gb_col, norm_g, ya, x2d, w_out, ln_g, ln_b, B, T):
    nc = T // L_M
    hb = L_M // HALO
    qk_w = H_M * DK_M
    n_steps = B * nc

    def cur(col):
        return lambda s: (jnp.minimum(s, n_steps - 1), col)

    def prev(s):
        return (jnp.maximum(s - 1, 0), 0)

    def halo_map(col):
        return lambda s: (jnp.maximum(jnp.minimum(s, n_steps - 1) * hb - 1, 0), col)

    const = lambda s: (0, 0)
    return pl.pallas_call(
        functools.partial(_mlstm_out_kernel, n_chunks=nc, n_steps=n_steps),
        grid=(n_steps + 1,),
        in_specs=[pl.BlockSpec((L_M, qk_w), cur(C_QM // qk_w)),
                  pl.BlockSpec((L_M, qk_w), cur(C_KM // qk_w)),
                  pl.BlockSpec((HALO, qk_w), halo_map(C_QM // qk_w)),
                  pl.BlockSpec((HALO, qk_w), halo_map(C_KM // qk_w)),
                  pl.BlockSpec((L_M, W_M), cur(C_VM // W_M)),
                  pl.BlockSpec((L_M, W_M), cur(C_OM // W_M)),
                  pl.BlockSpec((L_M, W_M), cur(C_ZM // W_M)),
                  pl.BlockSpec((L_M, LANES), cur(D_C // LANES)),
                  pl.BlockSpec((LANES, L_M), lambda s: (0, jnp.minimum(s, n_steps - 1))),
                  pl.BlockSpec((CONV_W, 2 * qk_w), const),
                  pl.BlockSpec((1, 2 * qk_w), const),
                  pl.BlockSpec((1, LANES), const),
                  pl.BlockSpec((LANES, 1), const),
                  pl.BlockSpec((1, W_M), const),
                  pl.BlockSpec((L_M, W_A), prev),
                  pl.BlockSpec((L_M, D_MODEL), prev),
                  pl.BlockSpec((W_A + W_M, D_MODEL), const),
                  pl.BlockSpec((1, D_MODEL), const),
                  pl.BlockSpec((1, D_MODEL), const)],
        out_specs=pl.BlockSpec((L_M, D_MODEL), prev),
        out_shape=jax.ShapeDtypeStruct((B * T, D_MODEL), F32),
        scratch_shapes=[pltpu.VMEM((H_M, DK_M, DV_M + LANES), F32),
                        pltpu.VMEM((H_M, 1, 1), F32),
                        pltpu.VMEM((L_M, W_M), BF16),
                        pltpu.VMEM((L_M, W_M), BF16),
                        pltpu.VMEM((L_M, D_MODEL), F32)],
        compiler_params=pltpu.CompilerParams(
            dimension_semantics=("arbitrary",), vmem_limit_bytes=VMEM_LIMIT),
        name="mlstm_out",
    )(main, main, main, main, main, main, main, small, gate_t,
      conv_w, conv_b, gb_row, gb_col, norm_g, ya, x2d, w_out, ln_g, ln_b)


_W_IN_SEGS = (("q_a", W_A), ("c_kv", D_C), ("z_a", W_A), ("q_i", H_IDX * D_IDX), ("k_i", D_IDX),
              ("w_i", H_IDX), ("q_m", H_M * DK_M), ("k_m", H_M * DK_M), ("v_m", W_M), ("i_m", H_M),
              ("f_m", H_M), ("o_m", W_M), ("z_m", W_M))
_MAIN_ORDER = ("q_a", "z_a", "q_i", "q_m", "k_m", "v_m", "o_m", "z_m")
_SEG_NAMES = [name for name, _ in _W_IN_SEGS]
assert _SEG_NAMES.index("f_m") == _SEG_NAMES.index("i_m") + 1 and S_FM == S_IM + H_M


def _repack_kernel(wt_ref, main_ref, small_ref):
    src, off = {}, 0
    for name, width in _W_IN_SEGS:
        src[name] = (off, width)
        off += width
    dst = 0
    for name in _MAIN_ORDER:
        lo, width = src[name]
        main_ref[dst:dst + width, :] = wt_ref[lo:lo + width, :].astype(BF16)
        dst += width
    parts = [wt_ref[src[name][0]:src[name][0] + src[name][1], :] for name in ("c_kv", "k_i", "w_i")]
    lo = src["i_m"][0]
    parts.append(wt_ref[lo:lo + 2 * H_M, :])
    used = sum(p.shape[0] for p in parts)
    parts.append(jnp.zeros((N_SMALL - used, wt_ref.shape[1]), F32))
    small_ref[...] = jnp.concatenate(parts, axis=0).astype(BF16)


def _repack_w_in(w_in, tc=256):
    n_cols = sum(width for _, width in _W_IN_SEGS)
    wt = jnp.swapaxes(w_in, 1, 2)[0]
    return pl.pallas_call(
        _repack_kernel,
        grid=(D_MODEL // tc,),
        in_specs=[pl.BlockSpec((n_cols, tc), lambda i: (0, i))],
        out_specs=[pl.BlockSpec((N_MAIN, tc), lambda i: (0, i)),
                   pl.BlockSpec((N_SMALL, tc), lambda i: (0, i))],
        out_shape=[jax.ShapeDtypeStruct((N_MAIN, D_MODEL), BF16),
                   jax.ShapeDtypeStruct((N_SMALL, D_MODEL), BF16)],
        compiler_params=pltpu.CompilerParams(
            dimension_semantics=("arbitrary",), vmem_limit_bytes=VMEM_LIMIT),
        name="repack",
    )(wt)


def kernel(x, w_in, b_igate, b_fgate, kv_norm_g, w_uk, w_uv, idx_k_ln_g, idx_k_ln_b, rel_bias,
           conv_w, conv_b, mh_norm_g, w_out, ln_g, ln_b):
    B, T, D = x.shape
    assert D == D_MODEL and T % L_M == 0 and T % (2 * KB) == 0 and w_in.shape[0] == 1
    bias = _bias_tiles(rel_bias)
    x2d = x.reshape(B * T, D)
    w_main, w_small = _repack_w_in(w_in)
    main, small = _proj(x2d, w_main, w_small)
    ckv_n, ckv_t, kidx_n, gate_t = _prep(small, kv_norm_g[0][None], idx_k_ln_g[0][None], idx_k_ln_b[0][None])
    w_uk_t = jnp.transpose(w_uk[0], (0, 2, 1)).astype(BF16)
    w_uv_t = jnp.transpose(w_uv[0], (0, 2, 1)).astype(BF16)
    ya = _dsa(main, gate_t, ckv_n, ckv_t, kidx_n, w_uk_t, w_uv_t, bias, B, T)
    gb = jnp.zeros((LANES,), F32).at[S_IM:S_IM + H_M].set(b_igate[0]).at[S_FM:S_FM + H_M].set(b_fgate[0])
    out = _mlstm_out(main, small, gate_t, conv_w[0], conv_b[0][None], gb[None, :], gb[:, None],
                     mh_norm_g[0][None], ya, x2d, w_out[0].astype(BF16), ln_g[0][None], ln_b[0][None], B, T)
    return out.reshape(B, T, D)
```

```python
import functools
import math

import numpy as np
import jax
import jax.numpy as jnp
from jax import lax
from jax.experimental import pallas as pl
from jax.experimental.pallas import tpu as pltpu

F32 = jnp.float32
BF16 = jnp.bfloat16

D_MODEL = 2048
W_A = 1024
DH_A = 128
H_A = 8
D_C = 256
H_IDX = 16
D_IDX = 64
TOPK = 256
W_M = 1024
H_M = 4
DV_M = 256
DK_M = 128
CONV_W = 4
N_BUCKETS = 32
MAX_DIST = 128
ALPHA = 2.0 ** 0.25
LN_EPS = 1e-5

LANES = 128
SUBLANES = 8
VMEM_LIMIT = 56 * 1024 * 1024

QB = 256
KB = 256
KI = 128
CNT_ROWS = 256
CNT_ACC = 4 * SUBLANES
L_M = 256
HALO = 16
NEG = -1e30
LOG2E = math.log2(math.e)
ONES_ROWS = 16

C_QA, C_ZA, C_QI, C_QM, C_KM, C_VM, C_OM, C_ZM = 0, 1024, 2048, 3072, 3584, 4096, 5120, 6144
N_MAIN = 7168
N_SMALL = 384
S_KI, S_WI, S_IM, S_FM = 0, 64, 80, 84


def _t5_bucket_np(rel):
    max_exact = N_BUCKETS // 2
    n = np.maximum(rel, 0)
    nf = np.maximum(n, 1).astype(np.float32)
    large = max_exact + (np.log(nf / np.float32(max_exact)) / np.float32(math.log(MAX_DIST / max_exact))
                         * np.float32(N_BUCKETS - max_exact)).astype(np.int32)
    large = np.minimum(large, N_BUCKETS - 1)
    return np.where(n < max_exact, n, large).astype(np.int32)


FAR_BUCKET = int(_t5_bucket_np(np.array(2 * KB + 1)))


def _bucket_tiles():
    i = np.arange(QB)[None, :]
    j = np.arange(KB)[:, None]
    t0 = _t5_bucket_np(i - j)
    t1 = _t5_bucket_np(i - j + KB)
    assert (t5 := _t5_bucket_np(np.arange(KB + 1, 4096))).min() == t5.max() == FAR_BUCKET
    return np.stack([t0, t1]).astype(np.int32)


def _bias_kernel(bucket_ref, rb_ref, out_ref):
    h = pl.program_id(0)
    far = rb_ref[FAR_BUCKET, h]
    for k in range(2):
        bk = bucket_ref[k]
        acc = jnp.zeros((KB, QB), F32)
        for b in range(N_BUCKETS):
            acc = jnp.where(bk == b, rb_ref[b, h] - far, acc)
        out_ref[0, k] = acc * LOG2E


def _bias_tiles(rel_bias):
    bucket = jnp.asarray(_bucket_tiles())
    return pl.pallas_call(
        _bias_kernel,
        grid=(H_A,),
        in_specs=[pl.BlockSpec((2, KB, QB), lambda h: (0, 0, 0)),
                  pl.BlockSpec(memory_space=pltpu.SMEM)],
        out_specs=pl.BlockSpec((1, 2, KB, QB), lambda h: (h, 0, 0, 0)),
        out_shape=jax.ShapeDtypeStruct((H_A, 2, KB, QB), F32),
        name="bias_tiles",
    )(bucket, rel_bias)


_NT = (((1,), (1,)), ((), ()))


def _proj_kernel(x_ref, w_ref, ws_ref, o_ref, os_ref, xb_ref):
    @pl.when(pl.program_id(1) == 0)
    def _():
        xb_ref[...] = x_ref[...].astype(BF16)
        os_ref[...] = lax.dot_general(xb_ref[...], ws_ref[...], _NT, preferred_element_type=F32)

    o_ref[...] = lax.dot_general(xb_ref[...], w_ref[...], _NT, preferred_element_type=F32).astype(BF16)


def _proj(x2d, w_main, w_small, tm=1024, tn=1024):
    M = x2d.shape[0]
    return pl.pallas_call(
        _proj_kernel,
        grid=(M // tm, N_MAIN // tn),
        in_specs=[pl.BlockSpec((tm, D_MODEL), lambda i, j: (i, 0)),
                  pl.BlockSpec((tn, D_MODEL), lambda i, j: (j, 0)),
                  pl.BlockSpec((N_SMALL, D_MODEL), lambda i, j: (0, 0))],
        out_specs=[pl.BlockSpec((tm, tn), lambda i, j: (i, j)),
                   pl.BlockSpec((tm, N_SMALL), lambda i, j: (i, 0))],
        out_shape=[jax.ShapeDtypeStruct((M, N_MAIN), BF16),
                   jax.ShapeDtypeStruct((M, N_SMALL), F32)],
        scratch_shapes=[pltpu.VMEM((tm, D_MODEL), BF16)],
        compiler_params=pltpu.CompilerParams(
            dimension_semantics=("arbitrary", "arbitrary"), vmem_limit_bytes=VMEM_LIMIT),
        name="proj",
    )(x2d, w_main, w_small)


def _prep_kernel(s_ref, kvg_ref, ig_ref, ib_ref, ckv_ref, ckvt_ref, kidx_ref, gt_ref):
    c = s_ref[:, 0:D_C]
    c = c * lax.rsqrt(jnp.mean(c * c, axis=-1, keepdims=True) + LN_EPS) * kvg_ref[...]
    ckv_ref[...] = c.astype(BF16)
    for r in range(ckvt_ref.shape[0]):
        ckvt_ref[r, 0:D_C, :] = c[r * KB:(r + 1) * KB, :].T.astype(BF16)
        ckvt_ref[r, D_C:D_C + ONES_ROWS, :] = jnp.ones((ONES_ROWS, KB), BF16)
    tile = s_ref[:, D_C:D_C + LANES]
    k = tile[:, S_KI:S_KI + D_IDX]
    mu = jnp.mean(k, axis=-1, keepdims=True)
    var = jnp.mean(jnp.square(k - mu), axis=-1, keepdims=True)
    kidx_ref[...] = ((k - mu) * lax.rsqrt(var + LN_EPS) * ig_ref[...] + ib_ref[...]).astype(BF16)
    gt_ref[...] = tile.T


def _prep(small, kv_g, idx_g, idx_b, tm=1024):
    M = small.shape[0]
    return pl.pallas_call(
        _prep_kernel,
        grid=(M // tm,),
        in_specs=[pl.BlockSpec((tm, N_SMALL), lambda i: (i, 0)),
                  pl.BlockSpec((1, D_C), lambda i: (0, 0)),
                  pl.BlockSpec((1, D_IDX), lambda i: (0, 0)),
                  pl.BlockSpec((1, D_IDX), lambda i: (0, 0))],
        out_specs=[pl.BlockSpec((tm, D_C), lambda i: (i, 0)),
                   pl.BlockSpec((tm // KB, D_C + ONES_ROWS, KB), lambda i: (i, 0, 0)),
                   pl.BlockSpec((tm, D_IDX), lambda i: (i, 0)),
                   pl.BlockSpec((LANES, tm), lambda i: (0, i))],
        out_shape=[jax.ShapeDtypeStruct((M, D_C), BF16),
                   jax.ShapeDtypeStruct((M // KB, D_C + ONES_ROWS, KB), BF16),
                   jax.ShapeDtypeStruct((M, D_IDX), BF16),
                   jax.ShapeDtypeStruct((LANES, M), F32)],
        name="prep",
    )(small, kv_g, idx_g, idx_b)


def _key_to_float(key):
    bits = jnp.where(key < 0, key ^ jnp.int32(0x7FFFFFFF), key)
    return lax.bitcast_convert_type(bits, F32)


def _dsa_kernel(qa_ref, za_ref, qi_ref, gt_ref, ckv_ref, ckvt_ref, kidx_ref, wukt_ref, wuvt_ref, bias_ref,
                y_ref, qall_ref, qr_ref, ha_ref, hb_ref, sc_ref, mb_ref, sa_ref, sb_ref, pa_ref, pb_ref,
                ta_ref, tb_ref, m_ref, al_ref, acc_ref, w_ref, cand_ref, thr_ref, cut_ref, cge_ref):
    qi = pl.program_id(1)
    nkb = qi + 1
    last_kb = mb_ref.shape[0] - 1

    for h in range(H_A):
        ql = lax.dot_general(wukt_ref[h], qa_ref[:, h * DH_A:(h + 1) * DH_A], (((1,), (1,)), ((), ())),
                             preferred_element_type=F32)
        qall_ref[h // (H_A // 2), :, (h % (H_A // 2)) * QB:(h % (H_A // 2) + 1) * QB] = (
            ql * (DH_A ** -0.5 * LOG2E)).astype(BF16)


    @pl.when(qi * QB < TOPK)
    def _():
        krow = lax.broadcasted_iota(jnp.int32, (KB, QB), 0)
        qcol = lax.broadcasted_iota(jnp.int32, (KB, QB), 1)
        mb_ref[0] = jnp.where(krow <= qcol, 0.0, NEG).astype(F32)

    @pl.when(qi * QB >= TOPK)
    def _():
        for h in range(H_IDX):
            qr_ref[h * QB:(h + 1) * QB, :] = qi_ref[:, h * D_IDX:(h + 1) * D_IDX]
        w_ref[...] = gt_ref[S_WI:S_WI + H_IDX, :] * ((D_IDX ** -0.5) * (H_IDX ** -0.5))
        krow = lax.broadcasted_iota(jnp.int32, (KI, LANES), 0)
        qcol = lax.broadcasted_iota(jnp.int32, (KI, LANES), 1)
        n_ki = nkb * (KB // KI)
        last_ki = sc_ref.shape[0] // KI - 1

        def head_dots(ki, dst_ref):
            k = kidx_ref[pl.ds(pl.multiple_of(jnp.minimum(ki, last_ki) * KI, KI), KI), :]
            dst_ref[...] = lax.dot_general(k, qr_ref[...], (((1,), (1,)), ((), ())),
                                           preferred_element_type=F32)

        def reduce_heads(src_ref, ki):
            for g in range(QB // LANES):
                lanes = slice(g * LANES, (g + 1) * LANES)
                acc = jnp.zeros((KI, LANES), F32)
                for h in range(H_IDX):
                    acc = acc + (jnp.maximum(src_ref[:, h * QB + g * LANES:h * QB + (g + 1) * LANES], 0.0)
                                 * w_ref[h:h + 1, lanes])
                sc_ref[pl.ds(pl.multiple_of(ki * KI, KI), KI), lanes] = jnp.where(
                    krow + (ki * KI - qi * QB - g * LANES) <= qcol, acc, -jnp.inf)

        assert CNT_ROWS == KB

        head_dots(0, ha_ref)

        def sc_body(j, carry):
            head_dots(2 * j + 1, hb_ref)
            reduce_heads(ha_ref, 2 * j)
            head_dots(2 * j + 2, ha_ref)
            reduce_heads(hb_ref, 2 * j + 1)
            return carry
        lax.fori_loop(0, n_ki // 2, sc_body, 0)

        n_cnt = (nkb * KB + CNT_ROWS - 1) // CNT_ROWS

        def count_where(pred, steps=None):
            def body(c, acc):
                parts = []
                for g in range(QB // LANES):
                    lanes = slice(g * LANES, (g + 1) * LANES)
                    rows = CNT_ROWS
                    if steps is None:
                        blk = sc_ref[pl.ds(pl.multiple_of(c * CNT_ROWS, CNT_ROWS), CNT_ROWS), lanes]
                    else:
                        if c == steps - 1:
                            rows = min(CNT_ROWS, (g + 1) * LANES)
                        blk = sc_ref[c * CNT_ROWS:c * CNT_ROWS + rows, lanes]
                    hit = jnp.where(pred(blk, c * CNT_ROWS, lanes), 1.0, 0.0).astype(F32)
                    parts.append(jnp.sum(hit.reshape(rows // CNT_ACC, CNT_ACC, LANES), axis=0))
                return acc + jnp.concatenate(parts, axis=1)
            acc = jnp.zeros((CNT_ACC, QB), F32)
            if steps is None:
                acc = lax.fori_loop(0, n_cnt, body, acc)
            else:
                for c in range(steps):
                    acc = body(c, acc)
            return jnp.sum(acc, axis=0, keepdims=True)

        def search(steps):
            n_g = QB // LANES

            def count_ge(g, cand):
                lanes = slice(g * LANES, (g + 1) * LANES)
                acc = jnp.zeros((CNT_ACC, LANES), F32)
                for c in range(steps):
                    rows = CNT_ROWS if c < steps - 1 else min(CNT_ROWS, (g + 1) * LANES)
                    hit = jnp.where(sc_ref[c * CNT_ROWS:c * CNT_ROWS + rows, lanes] >= cand, 1.0, 0.0)
                    acc = acc + jnp.sum(hit.astype(F32).reshape(rows // CNT_ACC, CNT_ACC, LANES), axis=0)
                return jnp.sum(acc, axis=0, keepdims=True)

            def bit_body(i, carry):
                new = []
                for g in range(n_g):
                    u, c_ge = carry[g]
                    trial = u | lax.shift_left(jnp.int32(1), 31 - i)
                    cnt = count_ge(g, _key_to_float(trial ^ jnp.int32(-2 ** 31)))
                    ok = cnt >= float(TOPK)
                    new.append((jnp.where(ok, trial, u), jnp.where(ok, cnt, c_ge)))
                return tuple(new)
            init = tuple((jnp.zeros((1, LANES), jnp.int32),
                          jnp.full((1, LANES), float(sc_ref.shape[0]), F32)) for _ in range(n_g))
            for g, (u, c_ge) in enumerate(lax.fori_loop(0, 32, bit_body, init)):
                lanes = slice(g * LANES, (g + 1) * LANES)
                thr_ref[:, lanes] = _key_to_float(u ^ jnp.int32(-2 ** 31))
                cge_ref[:, lanes] = c_ge

        for steps in range(1, sc_ref.shape[0] // CNT_ROWS + 1):
            pl.when(n_cnt == steps)(functools.partial(search, steps))
        has_ties = jnp.max(cge_ref[...]) > float(TOPK)

        @pl.when(jnp.logical_not(has_ties))
        def _():
            def mb_body(kb, carry):
                blk = sc_ref[pl.ds(pl.multiple_of(kb * KB, KB), KB), :]
                mb_ref[kb] = jnp.where(blk >= thr_ref[...], 0.0, NEG).astype(F32)
                return carry
            lax.fori_loop(0, nkb, mb_body, 0)

        @pl.when(has_ties)
        def _():
            c_gt = count_where(lambda blk, row0, lanes: blk > thr_ref[:, lanes])
            need = float(TOPK) - c_gt
            rows = lax.broadcasted_iota(jnp.int32, (CNT_ROWS, LANES), 0)
            n_bits = (sc_ref.shape[0] - 1).bit_length()

            def idx_body(i, cut):
                trial = cut | lax.shift_left(jnp.int32(1), n_bits - 1 - i)
                cut_ref[...] = trial
                before = count_where(lambda blk, row0, lanes: (blk == thr_ref[:, lanes])
                                     & (rows + row0 < cut_ref[:, lanes]))
                return jnp.where(before < need, trial, cut)
            cut_ref[...] = lax.fori_loop(0, n_bits, idx_body, jnp.zeros((1, QB), jnp.int32))

            def mb_body(kb, carry):
                for g in range(QB // LANES):
                    lanes = slice(g * LANES, (g + 1) * LANES)
                    blk = sc_ref[pl.ds(pl.multiple_of(kb * KB, KB), KB), lanes]
                    thr = thr_ref[:, lanes]
                    keep = (blk > thr) | ((blk == thr) & (rows[0:KB] + kb * KB <= cut_ref[:, lanes]))
                    mb_ref[kb, :, lanes] = jnp.where(keep, 0.0, NEG).astype(F32)
                return carry
            lax.fori_loop(0, nkb, mb_body, 0)

    m_ref[...] = jnp.full(m_ref.shape, NEG, F32)
    acc_ref[...] = jnp.zeros(acc_ref.shape, F32)

    hh = H_A // 2
    s_refs, p_refs, t_refs = (sa_ref, sb_ref), (pa_ref, pb_ref), (ta_ref, tb_ref)
    pb_ref[...] = jnp.zeros(pb_ref.shape, BF16)
    al_ref[...] = jnp.ones(al_ref.shape, F32)

    def logits(kb, half):
        kv = ckv_ref[pl.ds(pl.multiple_of(jnp.minimum(kb, last_kb) * KB, KB), KB), :]
        s_refs[half][...] = jnp.dot(kv, qall_ref[half], preferred_element_type=F32)

    def softmax(kb, half, with_bias):
        tile = jnp.clip(qi - kb, 0, 1)
        for j in range(hh):
            for g in range(QB // LANES):
                lanes = slice(g * LANES, (g + 1) * LANES)
                cols = slice(j * QB + g * LANES, j * QB + (g + 1) * LANES)
                x = s_refs[half][:, cols] + mb_ref[kb, :, lanes]
                if with_bias:
                    x = x + bias_ref[half * hh + j, tile, :, lanes]
                m_prev = m_ref[half, :, cols]
                m_blk = jnp.max(x.reshape(KB // CNT_ACC, CNT_ACC, LANES), axis=0)
                m_new = jnp.maximum(m_prev, jnp.max(m_blk, axis=0, keepdims=True))
                al_ref[half, :, cols] = jnp.exp2(m_prev - m_new)
                m_ref[half, :, cols] = m_new
                p_refs[half][:, cols] = jnp.exp2(x - m_new).astype(BF16)

    def accumulate(kb, half):
        t_refs[half][...] = jnp.dot(ckvt_ref[jnp.maximum(kb, 0)], p_refs[half][...],
                                    preferred_element_type=F32)
        acc_ref[half] = acc_ref[half] * al_ref[half] + t_refs[half][...]

    def sweep(first_kb, end_kb, with_bias):
        def body(kb, carry):
            logits(kb, 1)
            softmax(kb, 0, with_bias)
            accumulate(kb - 1, 1)
            logits(kb + 1, 0)
            softmax(kb, 1, with_bias)
            accumulate(kb, 0)
            return carry
        lax.fori_loop(first_kb, end_kb, body, 0)

    n_far = jnp.maximum(qi - 1, 0)
    logits(0, 0)
    sweep(0, n_far, False)
    sweep(n_far, nkb, True)
    accumulate(nkb - 1, 1)

    for h in range(H_A):
        half, cols = h // hh, slice((h % hh) * QB, (h % hh + 1) * QB)
        ya_t = jnp.dot(wuvt_ref[h], acc_ref[half, 0:D_C, cols].astype(BF16), preferred_element_type=F32)
        ya = (ya_t / acc_ref[half, D_C:D_C + 1, cols]).T
        z = za_ref[:, h * DH_A:(h + 1) * DH_A].astype(F32)
        y_ref[:, h * DH_A:(h + 1) * DH_A] = (ya * (z * jax.nn.sigmoid(z))).astype(BF16)


def _dsa(main, gate_t, ckv_n, ckv_t, kidx_n, w_uk_t, w_uv_t, bias, B, T):
    nq = T // QB
    return pl.pallas_call(
        _dsa_kernel,
        grid=(B, nq),
        in_specs=[pl.BlockSpec((QB, W_A), lambda b, q: (b * nq + q, C_QA // W_A)),
                  pl.BlockSpec((QB, W_A), lambda b, q: (b * nq + q, C_ZA // W_A)),
                  pl.BlockSpec((QB, H_IDX * D_IDX), lambda b, q: (b * nq + q, C_QI // (H_IDX * D_IDX))),
                  pl.BlockSpec((LANES, QB), lambda b, q: (0, b * nq + q)),
                  pl.BlockSpec((T, D_C), lambda b, q: (b, 0)),
                  pl.BlockSpec((T // KB, D_C + ONES_ROWS, KB), lambda b, q: (b, 0, 0)),
                  pl.BlockSpec((T, D_IDX), lambda b, q: (b, 0)),
                  pl.BlockSpec((H_A, D_C, DH_A), lambda b, q: (0, 0, 0)),
                  pl.BlockSpec((H_A, DH_A, D_C), lambda b, q: (0, 0, 0)),
                  pl.BlockSpec((H_A, 2, KB, QB), lambda b, q: (0, 0, 0, 0))],
        out_specs=pl.BlockSpec((QB, W_A), lambda b, q: (b * nq + q, 0)),
        out_shape=jax.ShapeDtypeStruct((B * T, W_A), BF16),
        scratch_shapes=[pltpu.VMEM((2, D_C, H_A // 2 * QB), BF16),
                        pltpu.VMEM((H_IDX * QB, D_IDX), BF16),
                        pltpu.VMEM((KI, H_IDX * QB), F32),
                        pltpu.VMEM((KI, H_IDX * QB), F32),
                        pltpu.VMEM((T, QB), F32),
                        pltpu.VMEM((T // KB, KB, QB), F32),
                        pltpu.VMEM((KB, H_A // 2 * QB), F32),
                        pltpu.VMEM((KB, H_A // 2 * QB), F32),
                        pltpu.VMEM((KB, H_A // 2 * QB), BF16),
                        pltpu.VMEM((KB, H_A // 2 * QB), BF16),
                        pltpu.VMEM((D_C + ONES_ROWS, H_A // 2 * QB), F32),
                        pltpu.VMEM((D_C + ONES_ROWS, H_A // 2 * QB), F32),
                        pltpu.VMEM((2, 1, H_A // 2 * QB), F32),
                        pltpu.VMEM((2, 1, H_A // 2 * QB), F32),
                        pltpu.VMEM((2, D_C + ONES_ROWS, H_A // 2 * QB), F32),
                        pltpu.VMEM((H_IDX, QB), F32),
                        pltpu.VMEM((1, QB), F32),
                        pltpu.VMEM((1, QB), F32),
                        pltpu.VMEM((1, QB), jnp.int32),
                        pltpu.VMEM((1, QB), F32)],
        compiler_params=pltpu.CompilerParams(
            dimension_semantics=("arbitrary", "arbitrary"), vmem_limit_bytes=VMEM_LIMIT),
        name="dsa",
    )(main, main, main, gate_t, ckv_n, ckv_t, kidx_n, w_uk_t, w_uv_t, bias)


def _split_dot(tri, x):
    hi = x.astype(BF16)
    lo = (x - hi.astype(F32)).astype(BF16)
    return jnp.dot(tri, hi, preferred_element_type=F32) + jnp.dot(tri, lo, preferred_element_type=F32)


def _log_sigmoid(x):
    return jnp.minimum(x, 0.0) - jnp.log1p(jnp.exp(-jnp.abs(x)))


def _mlstm_out_kernel(q_ref, k_ref, qh_ref, kh_ref, v_ref, o_ref, z_ref, g_ref, gt_ref,
                      cw_ref, cb_ref, gbr_ref, gbc_ref, ng_ref, ya_ref, x_ref, w_ref, lg_ref, lb_ref,
                      out_ref, ct_ref, m_ref, y_ref, yp_ref, p_ref, *, n_chunks, n_steps):
    step = pl.program_id(0)
    c = jnp.minimum(step, n_steps - 1) % n_chunks
    L = L_M

    @pl.when(step == 0)
    def _():
        yp_ref[...] = jnp.zeros(yp_ref.shape, BF16)

    @pl.when(c == 0)
    def _():
        ct_ref[...] = jnp.zeros(ct_ref.shape, F32)
        m_ref[...] = jnp.zeros(m_ref.shape, F32)

    n_pc = 2 * H_M
    pw = D_MODEL // n_pc

    def project_chunk(j):
        cols = slice(j * pw, (j + 1) * pw)
        p_ref[:, cols] = (jnp.dot(ya_ref[...], w_ref[0:W_A, cols], preferred_element_type=F32)
                          + jnp.dot(yp_ref[...], w_ref[W_A:W_A + W_M, cols], preferred_element_type=F32))

    def norm_previous():
        res = ALPHA * x_ref[...] + p_ref[...]
        mean = jnp.mean(res, axis=-1, keepdims=True)
        var_r = jnp.mean(jnp.square(res - mean), axis=-1, keepdims=True)
        out_ref[...] = (res - mean) * lax.rsqrt(var_r + LN_EPS) * lg_ref[...] + lb_ref[...]

    pending = [functools.partial(project_chunk, j) for j in range(n_pc)] + [norm_previous]

    def emit_projection_work():
        if pending:
            pending.pop(0)()

    emit_projection_work()

    r = lax.broadcasted_iota(jnp.int32, (L, L), 0)
    s = lax.broadcasted_iota(jnp.int32, (L, L), 1)
    causal = s <= r
    shifts = [jnp.where(r - s == d, 1.0, 0.0).astype(BF16) for d in range(1, CONV_W)]

    def conv_silu(x_ref, halo_ref, lo):
        x = x_ref[...]
        halo = jnp.where(c > 0, halo_ref[...].astype(F32), 0.0)
        w = cw_ref[:, lo:lo + H_M * DK_M]
        y = cb_ref[:, lo:lo + H_M * DK_M] + w[CONV_W - 1:CONV_W] * x.astype(F32)
        top = jnp.zeros((SUBLANES, H_M * DK_M), F32)
        for d in range(1, CONV_W):
            wd = w[CONV_W - 1 - d:CONV_W - d]
            y = y + wd * jnp.dot(shifts[d - 1], x, preferred_element_type=F32)
            top = top + wd * jnp.concatenate(
                [halo[HALO - d:HALO], jnp.zeros((SUBLANES - d, H_M * DK_M), F32)], axis=0)
        y = jnp.concatenate([y[0:SUBLANES] + top, y[SUBLANES:]], axis=0)
        return y * jax.nn.sigmoid(y)

    q_all = conv_silu(q_ref, qh_ref, 0)
    emit_projection_work()
    k_all = conv_silu(k_ref, kh_ref, H_M * DK_M) * (DK_M ** -0.5)
    emit_projection_work()

    gc = g_ref[...] + gbr_ref[...]
    gr = gt_ref[S_IM:S_IM + 2 * H_M, :] + gbc_ref[S_IM:S_IM + 2 * H_M, :]
    tri_l = jnp.where(causal, 1.0, 0.0).astype(BF16)
    tri_u = jnp.where(r <= s, 1.0, 0.0).astype(BF16)
    b_cols = _split_dot(tri_l, _log_sigmoid(gc) * LOG2E)
    lf_rows = _log_sigmoid(gr) * LOG2E
    b_rows = jnp.dot(lf_rows.astype(BF16), tri_u, preferred_element_type=F32) \
        + jnp.dot((lf_rows - lf_rows.astype(BF16).astype(F32)).astype(BF16), tri_u,
                  preferred_element_type=F32)
    gc = gc * LOG2E
    gr = gr * LOG2E
    ones = jnp.ones((L, LANES), BF16)
    emit_projection_work()

    for h in range(H_M):
        emit_projection_work()
        q = q_all[:, h * DK_M:(h + 1) * DK_M]
        k = k_all[:, h * DK_M:(h + 1) * DK_M]
        v = jnp.concatenate([v_ref[:, h * DV_M:(h + 1) * DV_M], ones], axis=1)
        qb = q.astype(BF16)
        b_c = b_cols[:, S_FM + h:S_FM + h + 1]
        i_c = gc[:, S_IM + h:S_IM + h + 1]
        b_r = b_rows[H_M + h:H_M + h + 1, :]
        i_r = gr[h:h + 1, :]
        m_prev = m_ref[h]
        ct = ct_ref[h]

        log_d = jnp.where(causal, b_c - b_r + i_r, -jnp.inf)
        g = b_c + m_prev
        m_t = jnp.maximum(jnp.max(log_d, axis=-1, keepdims=True), g)
        qk = lax.dot_general(qb, k.astype(BF16), _NT, preferred_element_type=F32)
        s_mat = qk * jnp.exp2(log_d - m_t)
        inter = jnp.exp2(g - m_t)
        num = jnp.dot(s_mat.astype(BF16), v, preferred_element_type=F32) \
            + inter * jnp.dot(qb, ct.astype(BF16), preferred_element_type=F32)
        den = jnp.maximum(jnp.abs(num[:, DV_M:]), jnp.exp2(-m_t))
        hh = num[:, 0:DV_M] / jnp.concatenate([den] * (DV_M // LANES), axis=1)

        emit_projection_work()
        b_last = b_c[L - 1:L, :]
        a_r = b_last - b_r + i_r
        m_new = jnp.maximum(b_last + m_prev, jnp.max(a_r, axis=-1, keepdims=True))
        decay = jnp.exp2(b_last + m_prev - m_new)
        wgt_c = jnp.exp2(b_last - b_c + i_c - m_new)
        kw = k * wgt_c
        ct_ref[h] = decay * ct + jnp.dot(kw.T.astype(BF16), v, preferred_element_type=F32)
        m_ref[h] = m_new

        mu = jnp.mean(hh, axis=-1, keepdims=True)
        var = jnp.mean(jnp.square(hh - mu), axis=-1, keepdims=True)
        hn = (hh - mu) * lax.rsqrt(var + LN_EPS) * ng_ref[:, h * DV_M:(h + 1) * DV_M]
        og = o_ref[:, h * DV_M:(h + 1) * DV_M].astype(F32)
        zg = z_ref[:, h * DV_M:(h + 1) * DV_M].astype(F32)
        y_ref[:, h * DV_M:(h + 1) * DV_M] = (hn * jax.nn.sigmoid(og) * (zg * jax.nn.sigmoid(zg))).astype(BF16)

    assert not pending
    yp_ref[...] = y_ref[...]


def _mlstm_out(main, small, gate_t, conv_w, conv_b, gb_row, gb_col, norm_g, ya, x2d, w_out, ln_g, ln_b, B, T):
    nc = T // L_M
    hb = L_M // HALO
    qk_w = H_M * DK_M
    n_steps = B * nc

    def cur(col):
        return lambda s: (jnp.minimum(s, n_steps - 1), col)

    def prev(s):
        return (jnp.maximum(s - 1, 0), 0)

    def halo_map(col):
        return lambda s: (jnp.maximum(jnp.minimum(s, n_steps - 1) * hb - 1, 0), col)

    const = lambda s: (0, 0)
    return pl.pallas_call(
        functools.partial(_mlstm_out_kernel, n_chunks=nc, n_steps=n_steps),
        grid=(n_steps + 1,),
        in_specs=[pl.BlockSpec((L_M, qk_w), cur(C_QM // qk_w)),
                  pl.BlockSpec((L_M, qk_w), cur(C_KM // qk_w)),
                  pl.BlockSpec((HALO, qk_w), halo_map(C_QM // qk_w)),
                  pl.BlockSpec((HALO, qk_w), halo_map(C_KM // qk_w)),
                  pl.BlockSpec((L_M, W_M), cur(C_VM // W_M)),
                  pl.BlockSpec((L_M, W_M), cur(C_OM // W_M)),
                  pl.BlockSpec((L_M, W_M), cur(C_ZM // W_M)),
                  pl.BlockSpec((L_M, LANES), cur(D_C // LANES)),
                  pl.BlockSpec((LANES, L_M), lambda s: (0, jnp.minimum(s, n_steps - 1))),
                  pl.BlockSpec((CONV_W, 2 * qk_w), const),
                  pl.BlockSpec((1, 2 * qk_w), const),
                  pl.BlockSpec((1, LANES), const),
                  pl.BlockSpec((LANES, 1), const),
                  pl.BlockSpec((1, W_M), const),
                  pl.BlockSpec((L_M, W_A), prev),
                  pl.BlockSpec((L_M, D_MODEL), prev),
                  pl.BlockSpec((W_A + W_M, D_MODEL), const),
                  pl.BlockSpec((1, D_MODEL), const),
                  pl.BlockSpec((1, D_MODEL), const)],
        out_specs=pl.BlockSpec((L_M, D_MODEL), prev),
        out_shape=jax.ShapeDtypeStruct((B * T, D_MODEL), F32),
        scratch_shapes=[pltpu.VMEM((H_M, DK_M, DV_M + LANES), F32),
                        pltpu.VMEM((H_M, 1, 1), F32),
                        pltpu.VMEM((L_M, W_M), BF16),
                        pltpu.VMEM((L_M, W_M), BF16),
                        pltpu.VMEM((L_M, D_MODEL), F32)],
        compiler_params=pltpu.CompilerParams(
            dimension_semantics=("arbitrary",), vmem_limit_bytes=VMEM_LIMIT),
        name="mlstm_out",
    )(main, main, main, main, main, main, main, small, gate_t,
      conv_w, conv_b, gb_row, gb_col, norm_g, ya, x2d, w_out, ln_g, ln_b)


_W_IN_SEGS = (("q_a", W_A), ("c_kv", D_C), ("z_a", W_A), ("q_i", H_IDX * D_IDX), ("k_i", D_IDX),
              ("w_i", H_IDX), ("q_m", H_M * DK_M), ("k_m", H_M * DK_M), ("v_m", W_M), ("i_m", H_M),
              ("f_m", H_M), ("o_m", W_M), ("z_m", W_M))
_MAIN_ORDER = ("q_a", "z_a", "q_i", "q_m", "k_m", "v_m", "o_m", "z_m")
_SEG_NAMES = [name for name, _ in _W_IN_SEGS]
assert _SEG_NAMES.index("f_m") == _SEG_NAMES.index("i_m") + 1 and S_FM == S_IM + H_M


def _repack_kernel(wt_ref, main_ref, small_ref):
    src, off = {}, 0
    for name, width in _W_IN_SEGS:
        src[name] = (off, width)
        off += width
    dst = 0
    for name in _MAIN_ORDER:
        lo, width = src[name]
        main_ref[dst:dst + width, :] = wt_ref[lo:lo + width, :].astype(BF16)
        dst += width
    parts = [wt_ref[src[name][0]:src[name][0] + src[name][1], :] for name in ("c_kv", "k_i", "w_i")]
    lo = src["i_m"][0]
    parts.append(wt_ref[lo:lo + 2 * H_M, :])
    used = sum(p.shape[0] for p in parts)
    parts.append(jnp.zeros((N_SMALL - used, wt_ref.shape[1]), F32))
    small_ref[...] = jnp.concatenate(parts, axis=0).astype(BF16)


def _repack_w_in(w_in, tc=256):
    n_cols = sum(width for _, width in _W_IN_SEGS)
    wt = jnp.swapaxes(w_in, 1, 2)[0]
    return pl.pallas_call(
        _repack_kernel,
        grid=(D_MODEL // tc,),
        in_specs=[pl.BlockSpec((n_cols, tc), lambda i: (0, i))],
        out_specs=[pl.BlockSpec((N_MAIN, tc), lambda i: (0, i)),
                   pl.BlockSpec((N_SMALL, tc), lambda i: (0, i))],
        out_shape=[jax.ShapeDtypeStruct((N_MAIN, D_MODEL), BF16),
                   jax.ShapeDtypeStruct((N_SMALL, D_MODEL), BF16)],
        compiler_params=pltpu.CompilerParams(
            dimension_semantics=("arbitrary",), vmem_limit_bytes=VMEM_LIMIT),
        name="repack",
    )(wt)


def kernel(x, w_in, b_igate, b_fgate, kv_norm_g, w_uk, w_uv, idx_k_ln_g, idx_k_ln_b, rel_bias,
           conv_w, conv_b, mh_norm_g, w_out, ln_g, ln_b):
    B, T, D = x.shape
    assert D == D_MODEL and T % L_M == 0 and T % (2 * KB) == 0 and w_in.shape[0] == 1
    bias = _bias_tiles(rel_bias)
    x2d = x.reshape(B * T, D)
    w_main, w_small = _repack_w_in(w_in)
    main, small = _proj(x2d, w_main, w_small)
    ckv_n, ckv_t, kidx_n, gate_t = _prep(small, kv_norm_g[0][None], idx_k_ln_g[0][None], idx_k_ln_b[0][None])
    w_uk_t = jnp.transpose(w_uk[0], (0, 2, 1)).astype(BF16)
    w_uv_t = jnp.transpose(w_uv[0], (0, 2, 1)).astype(BF16)
    ya = _dsa(main, gate_t, ckv_n, ckv_t, kidx_n, w_uk_t, w_uv_t, bias, B, T)
    gb = jnp.zeros((LANES,), F32).at[S_IM:S_IM + H_M].set(b_igate[0]).at[S_FM:S_FM + H_M].set(b_fgate[0])
    out = _mlstm_out(main, small, gate_t, conv_w[0], conv_b[0][None], gb[None, :], gb[:, None],
                     mh_norm_g[0][None], ya, x2d, w_out[0].astype(BF16), ln_g[0][None], ln_b[0][None], B, T)
    return out.reshape(B, T, D)
```

```python
import functools
import math

import numpy as np
import jax
import jax.numpy as jnp
from jax import lax
from jax.experimental import pallas as pl
from jax.experimental.pallas import tpu as pltpu

F32 = jnp.float32
BF16 = jnp.bfloat16

D_MODEL = 2048
W_A = 1024
DH_A = 128
H_A = 8
D_C = 256
H_IDX = 16
D_IDX = 64
TOPK = 256
W_M = 1024
H_M = 4
DV_M = 256
DK_M = 128
CONV_W = 4
N_BUCKETS = 32
MAX_DIST = 128
ALPHA = 2.0 ** 0.25
LN_EPS = 1e-5

LANES = 128
SUBLANES = 8
VMEM_LIMIT = 56 * 1024 * 1024

QB = 256
KB = 256
KI = 128
CNT_ROWS = 256
CNT_ACC = 4 * SUBLANES
L_M = 256
HALO = 16
NEG = -1e30
LOG2E = math.log2(math.e)
ONES_ROWS = 16

C_QA, C_ZA, C_QI, C_QM, C_KM, C_VM, C_OM, C_ZM = 0, 1024, 2048, 3072, 3584, 4096, 5120, 6144
N_MAIN = 7168
N_SMALL = 384
S_KI, S_WI, S_IM, S_FM = 0, 64, 80, 84


def _t5_bucket_np(rel):
    max_exact = N_BUCKETS // 2
    n = np.maximum(rel, 0)
    nf = np.maximum(n, 1).astype(np.float32)
    large = max_exact + (np.log(nf / np.float32(max_exact)) / np.float32(math.log(MAX_DIST / max_exact))
                         * np.float32(N_BUCKETS - max_exact)).astype(np.int32)
    large = np.minimum(large, N_BUCKETS - 1)
    return np.where(n < max_exact, n, large).astype(np.int32)


FAR_BUCKET = int(_t5_bucket_np(np.array(2 * KB + 1)))


def _bucket_tiles():
    i = np.arange(QB)[None, :]
    j = np.arange(KB)[:, None]
    t0 = _t5_bucket_np(i - j)
    t1 = _t5_bucket_np(i - j + KB)
    assert (t5 := _t5_bucket_np(np.arange(KB + 1, 4096))).min() == t5.max() == FAR_BUCKET
    return np.stack([t0, t1]).astype(np.int32)


def _bias_kernel(bucket_ref, rb_ref, out_ref):
    h = pl.program_id(0)
    far = rb_ref[FAR_BUCKET, h]
    for k in range(2):
        bk = bucket_ref[k]
        acc = jnp.zeros((KB, QB), F32)
        for b in range(N_BUCKETS):
            acc = jnp.where(bk == b, rb_ref[b, h] - far, acc)
        out_ref[0, k] = acc * LOG2E


def _bias_tiles(rel_bias):
    bucket = jnp.asarray(_bucket_tiles())
    return pl.pallas_call(
        _bias_kernel,
        grid=(H_A,),
        in_specs=[pl.BlockSpec((2, KB, QB), lambda h: (0, 0, 0)),
                  pl.BlockSpec(memory_space=pltpu.SMEM)],
        out_specs=pl.BlockSpec((1, 2, KB, QB), lambda h: (h, 0, 0, 0)),
        out_shape=jax.ShapeDtypeStruct((H_A, 2, KB, QB), F32),
        name="bias_tiles",
    )(bucket, rel_bias)


_NT = (((1,), (1,)), ((), ()))


def _proj_kernel(x_ref, w_ref, ws_ref, o_ref, os_ref, xb_ref):
    @pl.when(pl.program_id(1) == 0)
    def _():
        xb_ref[...] = x_ref[...].astype(BF16)
        os_ref[...] = lax.dot_general(xb_ref[...], ws_ref[...], _NT, preferred_element_type=F32)

    o_ref[...] = lax.dot_general(xb_ref[...], w_ref[...], _NT, preferred_element_type=F32).astype(BF16)


def _proj(x2d, w_main, w_small, tm=1024, tn=1792):
    M = x2d.shape[0]
    return pl.pallas_call(
        _proj_kernel,
        grid=(M // tm, N_MAIN // tn),
        in_specs=[pl.BlockSpec((tm, D_MODEL), lambda i, j: (i, 0)),
                  pl.BlockSpec((tn, D_MODEL), lambda i, j: (j, 0)),
                  pl.BlockSpec((N_SMALL, D_MODEL), lambda i, j: (0, 0))],
        out_specs=[pl.BlockSpec((tm, tn), lambda i, j: (i, j)),
                   pl.BlockSpec((tm, N_SMALL), lambda i, j: (i, 0))],
        out_shape=[jax.ShapeDtypeStruct((M, N_MAIN), BF16),
                   jax.ShapeDtypeStruct((M, N_SMALL), F32)],
        scratch_shapes=[pltpu.VMEM((tm, D_MODEL), BF16)],
        compiler_params=pltpu.CompilerParams(
            dimension_semantics=("arbitrary", "arbitrary"), vmem_limit_bytes=VMEM_LIMIT),
        name="proj",
    )(x2d, w_main, w_small)


def _prep_kernel(s_ref, kvg_ref, ig_ref, ib_ref, ckv_ref, ckvt_ref, kidx_ref, gt_ref):
    c = s_ref[:, 0:D_C]
    c = c * lax.rsqrt(jnp.mean(c * c, axis=-1, keepdims=True) + LN_EPS) * kvg_ref[...]
    ckv_ref[...] = c.astype(BF16)
    for r in range(ckvt_ref.shape[0]):
        ckvt_ref[r, 0:D_C, :] = c[r * KB:(r + 1) * KB, :].T.astype(BF16)
        ckvt_ref[r, D_C:D_C + ONES_ROWS, :] = jnp.ones((ONES_ROWS, KB), BF16)
    tile = s_ref[:, D_C:D_C + LANES]
    k = tile[:, S_KI:S_KI + D_IDX]
    mu = jnp.mean(k, axis=-1, keepdims=True)
    var = jnp.mean(jnp.square(k - mu), axis=-1, keepdims=True)
    kidx_ref[...] = ((k - mu) * lax.rsqrt(var + LN_EPS) * ig_ref[...] + ib_ref[...]).astype(BF16)
    gt_ref[...] = tile.T


def _prep(small, kv_g, idx_g, idx_b, tm=1024):
    M = small.shape[0]
    return pl.pallas_call(
        _prep_kernel,
        grid=(M // tm,),
        in_specs=[pl.BlockSpec((tm, N_SMALL), lambda i: (i, 0)),
                  pl.BlockSpec((1, D_C), lambda i: (0, 0)),
                  pl.BlockSpec((1, D_IDX), lambda i: (0, 0)),
                  pl.BlockSpec((1, D_IDX), lambda i: (0, 0))],
        out_specs=[pl.BlockSpec((tm, D_C), lambda i: (i, 0)),
                   pl.BlockSpec((tm // KB, D_C + ONES_ROWS, KB), lambda i: (i, 0, 0)),
                   pl.BlockSpec((tm, D_IDX), lambda i: (i, 0)),
                   pl.BlockSpec((LANES, tm), lambda i: (0, i))],
        out_shape=[jax.ShapeDtypeStruct((M, D_C), BF16),
                   jax.ShapeDtypeStruct((M // KB, D_C + ONES_ROWS, KB), BF16),
                   jax.ShapeDtypeStruct((M, D_IDX), BF16),
                   jax.ShapeDtypeStruct((LANES, M), F32)],
        name="prep",
    )(small, kv_g, idx_g, idx_b)


def _key_to_float(key):
    bits = jnp.where(key < 0, key ^ jnp.int32(0x7FFFFFFF), key)
    return lax.bitcast_convert_type(bits, F32)


def _dsa_kernel(qa_ref, za_ref, qi_ref, gt_ref, ckv_ref, ckvt_ref, kidx_ref, wukt_ref, wuvt_ref, bias_ref,
                y_ref, qall_ref, qr_ref, ha_ref, hb_ref, sc_ref, mb_ref, sa_ref, sb_ref, pa_ref, pb_ref,
                ta_ref, tb_ref, m_ref, al_ref, acc_ref, w_ref, thr_ref, cut_ref, cge_ref):
    qi = pl.program_id(1)
    nkb = qi + 1
    last_kb = mb_ref.shape[0] - 1

    for h in range(H_A):
        ql = lax.dot_general(wukt_ref[h], qa_ref[:, h * DH_A:(h + 1) * DH_A], (((1,), (1,)), ((), ())),
                             preferred_element_type=F32)
        qall_ref[h // (H_A // 2), :, (h % (H_A // 2)) * QB:(h % (H_A // 2) + 1) * QB] = (
            ql * (DH_A ** -0.5 * LOG2E)).astype(BF16)


    @pl.when(qi * QB < TOPK)
    def _():
        krow = lax.broadcasted_iota(jnp.int32, (KB, QB), 0)
        qcol = lax.broadcasted_iota(jnp.int32, (KB, QB), 1)
        mb_ref[0] = jnp.where(krow <= qcol, 0.0, NEG).astype(F32)

    @pl.when(qi * QB >= TOPK)
    def _():
        for h in range(H_IDX):
            qr_ref[h * QB:(h + 1) * QB, :] = qi_ref[:, h * D_IDX:(h + 1) * D_IDX]
        w_ref[...] = gt_ref[S_WI:S_WI + H_IDX, :] * ((D_IDX ** -0.5) * (H_IDX ** -0.5))
        krow = lax.broadcasted_iota(jnp.int32, (KI, LANES), 0)
        qcol = lax.broadcasted_iota(jnp.int32, (KI, LANES), 1)
        n_ki = nkb * (KB // KI)
        last_ki = sc_ref.shape[0] // KI - 1

        def head_dots(ki, dst_ref):
            k = kidx_ref[pl.ds(pl.multiple_of(jnp.minimum(ki, last_ki) * KI, KI), KI), :]
            dst_ref[...] = lax.dot_general(k, qr_ref[...], (((1,), (1,)), ((), ())),
                                           preferred_element_type=F32)

        def reduce_heads(src_ref, ki):
            for g in range(QB // LANES):
                lanes = slice(g * LANES, (g + 1) * LANES)
                acc = jnp.zeros((KI, LANES), F32)
                for h in range(H_IDX):
                    acc = acc + (jnp.maximum(src_ref[:, h * QB + g * LANES:h * QB + (g + 1) * LANES], 0.0)
                                 * w_ref[h:h + 1, lanes])
                sc_ref[pl.ds(pl.multiple_of(ki * KI, KI), KI), lanes] = jnp.where(
                    krow + (ki * KI - qi * QB - g * LANES) <= qcol, acc, -jnp.inf)

        assert CNT_ROWS == KB

        head_dots(0, ha_ref)

        def sc_body(j, carry):
            head_dots(2 * j + 1, hb_ref)
            reduce_heads(ha_ref, 2 * j)
            head_dots(2 * j + 2, ha_ref)
            reduce_heads(hb_ref, 2 * j + 1)
            return carry
        lax.fori_loop(0, n_ki // 2, sc_body, 0)

        n_cnt = (nkb * KB + CNT_ROWS - 1) // CNT_ROWS

        def count_where(pred, steps=None):
            def body(c, acc):
                parts = []
                for g in range(QB // LANES):
                    lanes = slice(g * LANES, (g + 1) * LANES)
                    rows = CNT_ROWS
                    if steps is None:
                        blk = sc_ref[pl.ds(pl.multiple_of(c * CNT_ROWS, CNT_ROWS), CNT_ROWS), lanes]
                    else:
                        if c == steps - 1:
                            rows = min(CNT_ROWS, (g + 1) * LANES)
                        blk = sc_ref[c * CNT_ROWS:c * CNT_ROWS + rows, lanes]
                    hit = jnp.where(pred(blk, c * CNT_ROWS, lanes), 1.0, 0.0).astype(F32)
                    parts.append(jnp.sum(hit.reshape(rows // CNT_ACC, CNT_ACC, LANES), axis=0))
                return acc + jnp.concatenate(parts, axis=1)
            acc = jnp.zeros((CNT_ACC, QB), F32)
            if steps is None:
                acc = lax.fori_loop(0, n_cnt, body, acc)
            else:
                for c in range(steps):
                    acc = body(c, acc)
            return jnp.sum(acc, axis=0, keepdims=True)

        def search(steps):
            n_g = QB // LANES

            def count_ge(g, cand):
                lanes = slice(g * LANES, (g + 1) * LANES)
                acc = jnp.zeros((CNT_ACC, LANES), F32)
                for c in range(steps):
                    rows = CNT_ROWS if c < steps - 1 else min(CNT_ROWS, (g + 1) * LANES)
                    hit = jnp.where(sc_ref[c * CNT_ROWS:c * CNT_ROWS + rows, lanes] >= cand, 1.0, 0.0)
                    acc = acc + jnp.sum(hit.astype(F32).reshape(rows // CNT_ACC, CNT_ACC, LANES), axis=0)
                return jnp.sum(acc, axis=0, keepdims=True)

            def bit_body(i, carry):
                new = []
                for g in range(n_g):
                    u, c_ge = carry[g]
                    trial = u | lax.shift_left(jnp.int32(1), 31 - i)
                    cnt = count_ge(g, _key_to_float(trial ^ jnp.int32(-2 ** 31)))
                    ok = cnt >= float(TOPK)
                    new.append((jnp.where(ok, trial, u), jnp.where(ok, cnt, c_ge)))
                return tuple(new)
            init = tuple((jnp.zeros((1, LANES), jnp.int32),
                          jnp.full((1, LANES), float(sc_ref.shape[0]), F32)) for _ in range(n_g))
            for g, (u, c_ge) in enumerate(lax.fori_loop(0, 32, bit_body, init)):
                lanes = slice(g * LANES, (g + 1) * LANES)
                thr_ref[:, lanes] = _key_to_float(u ^ jnp.int32(-2 ** 31))
                cge_ref[:, lanes] = c_ge

        for steps in range(1, sc_ref.shape[0] // CNT_ROWS + 1):
            pl.when(n_cnt == steps)(functools.partial(search, steps))
        has_ties = jnp.max(cge_ref[...]) > float(TOPK)

        @pl.when(jnp.logical_not(has_ties))
        def _():
            def mb_body(kb, carry):
                blk = sc_ref[pl.ds(pl.multiple_of(kb * KB, KB), KB), :]
                mb_ref[kb] = jnp.where(blk >= thr_ref[...], 0.0, NEG).astype(F32)
                return carry
            lax.fori_loop(0, nkb, mb_body, 0)

        @pl.when(has_ties)
        def _():
            c_gt = count_where(lambda blk, row0, lanes: blk > thr_ref[:, lanes])
            need = float(TOPK) - c_gt
            rows = lax.broadcasted_iota(jnp.int32, (CNT_ROWS, LANES), 0)
            n_bits = (sc_ref.shape[0] - 1).bit_length()

            def idx_body(i, cut):
                trial = cut | lax.shift_left(jnp.int32(1), n_bits - 1 - i)
                cut_ref[...] = trial
                before = count_where(lambda blk, row0, lanes: (blk == thr_ref[:, lanes])
                                     & (rows + row0 < cut_ref[:, lanes]))
                return jnp.where(before < need, trial, cut)
            cut_ref[...] = lax.fori_loop(0, n_bits, idx_body, jnp.zeros((1, QB), jnp.int32))

            def mb_body(kb, carry):
                for g in range(QB // LANES):
                    lanes = slice(g * LANES, (g + 1) * LANES)
                    blk = sc_ref[pl.ds(pl.multiple_of(kb * KB, KB), KB), lanes]
                    thr = thr_ref[:, lanes]
                    keep = (blk > thr) | ((blk == thr) & (rows[0:KB] + kb * KB <= cut_ref[:, lanes]))
                    mb_ref[kb, :, lanes] = jnp.where(keep, 0.0, NEG).astype(F32)
                return carry
            lax.fori_loop(0, nkb, mb_body, 0)

    m_ref[...] = jnp.full(m_ref.shape, NEG, F32)
    acc_ref[...] = jnp.zeros(acc_ref.shape, F32)

    hh = H_A // 2
    s_refs, p_refs, t_refs = (sa_ref, sb_ref), (pa_ref, pb_ref), (ta_ref, tb_ref)
    pb_ref[...] = jnp.zeros(pb_ref.shape, BF16)
    al_ref[...] = jnp.ones(al_ref.shape, F32)

    def logits(kb, half):
        kv = ckv_ref[pl.ds(pl.multiple_of(jnp.minimum(kb, last_kb) * KB, KB), KB), :]
        s_refs[half][...] = jnp.dot(kv, qall_ref[half], preferred_element_type=F32)

    def softmax(kb, half, with_bias):
        tile = jnp.clip(qi - kb, 0, 1)
        for j in range(hh):
            for g in range(QB // LANES):
                lanes = slice(g * LANES, (g + 1) * LANES)
                cols = slice(j * QB + g * LANES, j * QB + (g + 1) * LANES)
                x = s_refs[half][:, cols] + mb_ref[kb, :, lanes]
                if with_bias:
                    x = x + bias_ref[half * hh + j, tile, :, lanes]
                m_prev = m_ref[half, :, cols]
                m_blk = jnp.max(x.reshape(KB // CNT_ACC, CNT_ACC, LANES), axis=0)
                m_new = jnp.maximum(m_prev, jnp.max(m_blk, axis=0, keepdims=True))
                al_ref[half, :, cols] = jnp.exp2(m_prev - m_new)
                m_ref[half, :, cols] = m_new
                p_refs[half][:, cols] = jnp.exp2(x - m_new).astype(BF16)

    def accumulate(kb, half):
        t_refs[half][...] = jnp.dot(ckvt_ref[jnp.maximum(kb, 0)], p_refs[half][...],
                                    preferred_element_type=F32)
        acc_ref[half] = acc_ref[half] * al_ref[half] + t_refs[half][...]

    def sweep(first_kb, end_kb, with_bias):
        def body(kb, carry):
            logits(kb, 1)
            softmax(kb, 0, with_bias)
            accumulate(kb - 1, 1)
            logits(kb + 1, 0)
            softmax(kb, 1, with_bias)
            accumulate(kb, 0)
            return carry
        lax.fori_loop(first_kb, end_kb, body, 0)

    n_far = jnp.maximum(qi - 1, 0)
    logits(0, 0)
    sweep(0, n_far, False)
    sweep(n_far, nkb, True)
    accumulate(nkb - 1, 1)

    for h in range(H_A):
        half, cols = h // hh, slice((h % hh) * QB, (h % hh + 1) * QB)
        ya_t = jnp.dot(wuvt_ref[h], acc_ref[half, 0:D_C, cols].astype(BF16), preferred_element_type=F32)
        ya = (ya_t / acc_ref[half, D_C:D_C + 1, cols]).T
        z = za_ref[:, h * DH_A:(h + 1) * DH_A].astype(F32)
        y_ref[:, h * DH_A:(h + 1) * DH_A] = (ya * (z * jax.nn.sigmoid(z))).astype(BF16)


def _dsa(main, gate_t, ckv_n, ckv_t, kidx_n, w_uk_t, w_uv_t, bias, B, T):
    nq = T // QB
    return pl.pallas_call(
        _dsa_kernel,
        grid=(B, nq),
        in_specs=[pl.BlockSpec((QB, W_A), lambda b, q: (b * nq + q, C_QA // W_A)),
                  pl.BlockSpec((QB, W_A), lambda b, q: (b * nq + q, C_ZA // W_A)),
                  pl.BlockSpec((QB, H_IDX * D_IDX), lambda b, q: (b * nq + q, C_QI // (H_IDX * D_IDX))),
                  pl.BlockSpec((LANES, QB), lambda b, q: (0, b * nq + q)),
                  pl.BlockSpec((T, D_C), lambda b, q: (b, 0)),
                  pl.BlockSpec((T // KB, D_C + ONES_ROWS, KB), lambda b, q: (b, 0, 0)),
                  pl.BlockSpec((T, D_IDX), lambda b, q: (b, 0)),
                  pl.BlockSpec((H_A, D_C, DH_A), lambda b, q: (0, 0, 0)),
                  pl.BlockSpec((H_A, DH_A, D_C), lambda b, q: (0, 0, 0)),
                  pl.BlockSpec((H_A, 2, KB, QB), lambda b, q: (0, 0, 0, 0))],
        out_specs=pl.BlockSpec((QB, W_A), lambda b, q: (b * nq + q, 0)),
        out_shape=jax.ShapeDtypeStruct((B * T, W_A), BF16),
        scratch_shapes=[pltpu.VMEM((2, D_C, H_A // 2 * QB), BF16),
                        pltpu.VMEM((H_IDX * QB, D_IDX), BF16),
                        pltpu.VMEM((KI, H_IDX * QB), F32),
                        pltpu.VMEM((KI, H_IDX * QB), F32),
                        pltpu.VMEM((T, QB), F32),
                        pltpu.VMEM((T // KB, KB, QB), F32),
                        pltpu.VMEM((KB, H_A // 2 * QB), F32),
                        pltpu.VMEM((KB, H_A // 2 * QB), F32),
                        pltpu.VMEM((KB, H_A // 2 * QB), BF16),
                        pltpu.VMEM((KB, H_A // 2 * QB), BF16),
                        pltpu.VMEM((D_C + ONES_ROWS, H_A // 2 * QB), F32),
                        pltpu.VMEM((D_C + ONES_ROWS, H_A // 2 * QB), F32),
                        pltpu.VMEM((2, 1, H_A // 2 * QB), F32),
                        pltpu.VMEM((2, 1, H_A // 2 * QB), F32),
                        pltpu.VMEM((2, D_C + ONES_ROWS, H_A // 2 * QB), F32),
                        pltpu.VMEM((H_IDX, QB), F32),
                        pltpu.VMEM((1, QB), F32),
                        pltpu.VMEM((1, QB), jnp.int32),
                        pltpu.VMEM((1, QB), F32)],
        compiler_params=pltpu.CompilerParams(
            dimension_semantics=("arbitrary", "arbitrary"), vmem_limit_bytes=VMEM_LIMIT),
        name="dsa",
    )(main, main, main, gate_t, ckv_n, ckv_t, kidx_n, w_uk_t, w_uv_t, bias)


def _split_dot(tri, x):
    hi = x.astype(BF16)
    lo = (x - hi.astype(F32)).astype(BF16)
    return jnp.dot(tri, hi, preferred_element_type=F32) + jnp.dot(tri, lo, preferred_element_type=F32)


def _log_sigmoid(x):
    return jnp.minimum(x, 0.0) - jnp.log1p(jnp.exp(-jnp.abs(x)))


def _mlstm_out_kernel(q_ref, k_ref, qh_ref, kh_ref, v_ref, o_ref, z_ref, g_ref, gt_ref,
                      cw_ref, cb_ref, gbr_ref, gbc_ref, ng_ref, ya_ref, x_ref, w_ref, lg_ref, lb_ref,
                      out_ref, ct_ref, m_ref, y_ref, yp_ref, p_ref, *, n_chunks, n_steps):
    step = pl.program_id(0)
    c = jnp.minimum(step, n_steps - 1) % n_chunks
    L = L_M

    @pl.when(step == 0)
    def _():
        yp_ref[...] = jnp.zeros(yp_ref.shape, BF16)

    @pl.when(c == 0)
    def _():
        ct_ref[...] = jnp.zeros(ct_ref.shape, F32)
        m_ref[...] = jnp.zeros(m_ref.shape, F32)

    n_pc = 2 * H_M
    pw = D_MODEL // n_pc

    def project_chunk(j):
        cols = slice(j * pw, (j + 1) * pw)
        p_ref[:, cols] = (jnp.dot(ya_ref[...], w_ref[0:W_A, cols], preferred_element_type=F32)
                          + jnp.dot(yp_ref[...], w_ref[W_A:W_A + W_M, cols], preferred_element_type=F32))

    def norm_previous():
        res = ALPHA * x_ref[...] + p_ref[...]
        mean = jnp.mean(res, axis=-1, keepdims=True)
        var_r = jnp.mean(jnp.square(res - mean), axis=-1, keepdims=True)
        out_ref[...] = (res - mean) * lax.rsqrt(var_r + LN_EPS) * lg_ref[...] + lb_ref[...]

    pending = [functools.partial(project_chunk, j) for j in range(n_pc)] + [norm_previous]

    def emit_projection_work():
        if pending:
            pending.pop(0)()

    emit_projection_work()

    r = lax.broadcasted_iota(jnp.int32, (L, L), 0)
    s = lax.broadcasted_iota(jnp.int32, (L, L), 1)
    causal = s <= r
    shifts = [jnp.where(r - s == d, 1.0, 0.0).astype(BF16) for d in range(1, CONV_W)]

    def conv_silu(x_ref, halo_ref, lo):
        x = x_ref[...]
        halo = jnp.where(c > 0, halo_ref[...].astype(F32), 0.0)
        w = cw_ref[:, lo:lo + H_M * DK_M]
        y = cb_ref[:, lo:lo + H_M * DK_M] + w[CONV_W - 1:CONV_W] * x.astype(F32)
        top = jnp.zeros((SUBLANES, H_M * DK_M), F32)
        for d in range(1, CONV_W):
            wd = w[CONV_W - 1 - d:CONV_W - d]
            y = y + wd * jnp.dot(shifts[d - 1], x, preferred_element_type=F32)
            top = top + wd * jnp.concatenate(
                [halo[HALO - d:HALO], jnp.zeros((SUBLANES - d, H_M * DK_M), F32)], axis=0)
        y = jnp.concatenate([y[0:SUBLANES] + top, y[SUBLANES:]], axis=0)
        return y * jax.nn.sigmoid(y)

    q_all = conv_silu(q_ref, qh_ref, 0)
    emit_projection_work()
    k_all = conv_silu(k_ref, kh_ref, H_M * DK_M) * (DK_M ** -0.5)
    emit_projection_work()

    gc = g_ref[...] + gbr_ref[...]
    gr = gt_ref[S_IM:S_IM + 2 * H_M, :] + gbc_ref[S_IM:S_IM + 2 * H_M, :]
    tri_l = jnp.where(causal, 1.0, 0.0).astype(BF16)
    tri_u = jnp.where(r <= s, 1.0, 0.0).astype(BF16)
    b_cols = _split_dot(tri_l, _log_sigmoid(gc) * LOG2E)
    lf_rows = _log_sigmoid(gr) * LOG2E
    b_rows = jnp.dot(lf_rows.astype(BF16), tri_u, preferred_element_type=F32) \
        + jnp.dot((lf_rows - lf_rows.astype(BF16).astype(F32)).astype(BF16), tri_u,
                  preferred_element_type=F32)
    gc = gc * LOG2E
    gr = gr * LOG2E
    ones = jnp.ones((L, LANES), BF16)
    emit_projection_work()

    for h in range(H_M):
        emit_projection_work()
        q = q_all[:, h * DK_M:(h + 1) * DK_M]
        k = k_all[:, h * DK_M:(h + 1) * DK_M]
        v = jnp.concatenate([v_ref[:, h * DV_M:(h + 1) * DV_M], ones], axis=1)
        qb = q.astype(BF16)
        b_c = b_cols[:, S_FM + h:S_FM + h + 1]
        i_c = gc[:, S_IM + h:S_IM + h + 1]
        b_r = b_rows[H_M + h:H_M + h + 1, :]
        i_r = gr[h:h + 1, :]
        m_prev = m_ref[h]
        ct = ct_ref[h]

        log_d = jnp.where(causal, b_c - b_r + i_r, -jnp.inf)
        g = b_c + m_prev
        m_t = jnp.maximum(jnp.max(log_d, axis=-1, keepdims=True), g)
        qk = lax.dot_general(qb, k.astype(BF16), _NT, preferred_element_type=F32)
        s_mat = qk * jnp.exp2(log_d - m_t)
        inter = jnp.exp2(g - m_t)
        num = jnp.dot(s_mat.astype(BF16), v, preferred_element_type=F32) \
            + inter * jnp.dot(qb, ct.astype(BF16), preferred_element_type=F32)
        den = jnp.maximum(jnp.abs(num[:, DV_M:]), jnp.exp2(-m_t))
        hh = num[:, 0:DV_M] / jnp.concatenate([den] * (DV_M // LANES), axis=1)

        emit_projection_work()
        b_last = b_c[L - 1:L, :]
        a_r = b_last - b_r + i_r
        m_new = jnp.maximum(b_last + m_prev, jnp.max(a_r, axis=-1, keepdims=True))
        decay = jnp.exp2(b_last + m_prev - m_new)
        wgt_c = jnp.exp2(b_last - b_c + i_c - m_new)
        kw = k * wgt_c
        ct_ref[h] = decay * ct + jnp.dot(kw.T.astype(BF16), v, preferred_element_type=F32)
        m_ref[h] = m_new

        mu = jnp.mean(hh, axis=-1, keepdims=True)
        var = jnp.mean(jnp.square(hh - mu), axis=-1, keepdims=True)
        hn = (hh - mu) * lax.rsqrt(var + LN_EPS) * ng_ref[:, h * DV_M:(h + 1) * DV_M]
        og = o_ref[:, h * DV_M:(h + 1) * DV_M].astype(F32)
        zg = z_ref[:, h * DV_M:(h + 1) * DV_M].astype(F32)
        y_ref[:, h * DV_M:(h + 1) * DV_M] = (hn * jax.nn.sigmoid(og) * (zg * jax.nn.sigmoid(zg))).astype(BF16)

    assert not pending
    yp_ref[...] = y_ref[...]


def _mlstm_out(main, small, gate_t, conv_w, conv_b, gb_row, gb_col, norm_g, ya, x2d, w_out, ln_g, ln_b, B, T):
    nc = T // L_M
    hb = L_M // HALO
    qk_w = H_M * DK_M
    n_steps = B * nc

    def cur(col):
        return lambda s: (jnp.minimum(s, n_steps - 1), col)

    def prev(s):
        return (jnp.maximum(s - 1, 0), 0)

    def halo_map(col):
        return lambda s: (jnp.maximum(jnp.minimum(s, n_steps - 1) * hb - 1, 0), col)

    const = lambda s: (0, 0)
    return pl.pallas_call(
        functools.partial(_mlstm_out_kernel, n_chunks=nc, n_steps=n_steps),
        grid=(n_steps + 1,),
        in_specs=[pl.BlockSpec((L_M, qk_w), cur(C_QM // qk_w)),
                  pl.BlockSpec((L_M, qk_w), cur(C_KM // qk_w)),
                  pl.BlockSpec((HALO, qk_w), halo_map(C_QM // qk_w)),
                  pl.BlockSpec((HALO, qk_w), halo_map(C_KM // qk_w)),
                  pl.BlockSpec((L_M, W_M), cur(C_VM // W_M)),
                  pl.BlockSpec((L_M, W_M), cur(C_OM // W_M)),
                  pl.BlockSpec((L_M, W_M), cur(C_ZM // W_M)),
                  pl.BlockSpec((L_M, LANES), cur(D_C // LANES)),
                  pl.BlockSpec((LANES, L_M), lambda s: (0, jnp.minimum(s, n_steps - 1))),
                  pl.BlockSpec((CONV_W, 2 * qk_w), const),
                  pl.BlockSpec((1, 2 * qk_w), const),
                  pl.BlockSpec((1, LANES), const),
                  pl.BlockSpec((LANES, 1), const),
                  pl.BlockSpec((1, W_M), const),
                  pl.BlockSpec((L_M, W_A), prev),
                  pl.BlockSpec((L_M, D_MODEL), prev),
                  pl.BlockSpec((W_A + W_M, D_MODEL), const),
                  pl.BlockSpec((1, D_MODEL), const),
                  pl.BlockSpec((1, D_MODEL), const)],
        out_specs=pl.BlockSpec((L_M, D_MODEL), prev),
        out_shape=jax.ShapeDtypeStruct((B * T, D_MODEL), F32),
        scratch_shapes=[pltpu.VMEM((H_M, DK_M, DV_M + LANES), F32),
                        pltpu.VMEM((H_M, 1, 1), F32),
                        pltpu.VMEM((L_M, W_M), BF16),
                        pltpu.VMEM((L_M, W_M), BF16),
                        pltpu.VMEM((L_M, D_MODEL), F32)],
        compiler_params=pltpu.CompilerParams(
            dimension_semantics=("arbitrary",), vmem_limit_bytes=VMEM_LIMIT),
        name="mlstm_out",
    )(main, main, main, main, main, main, main, small, gate_t,
      conv_w, conv_b, gb_row, gb_col, norm_g, ya, x2d, w_out, ln_g, ln_b)


_W_IN_SEGS = (("q_a", W_A), ("c_kv", D_C), ("z_a", W_A), ("q_i", H_IDX * D_IDX), ("k_i", D_IDX),
              ("w_i", H_IDX), ("q_m", H_M * DK_M), ("k_m", H_M * DK_M), ("v_m", W_M), ("i_m", H_M),
              ("f_m", H_M), ("o_m", W_M), ("z_m", W_M))
_MAIN_ORDER = ("q_a", "z_a", "q_i", "q_m", "k_m", "v_m", "o_m", "z_m")
_SEG_NAMES = [name for name, _ in _W_IN_SEGS]
assert _SEG_NAMES.index("f_m") == _SEG_NAMES.index("i_m") + 1 and S_FM == S_IM + H_M


def _repack_kernel(wt_ref, main_ref, small_ref):
    src, off = {}, 0
    for name, width in _W_IN_SEGS:
        src[name] = (off, width)
        off += width
    dst = 0
    for name in _MAIN_ORDER:
        lo, width = src[name]
        main_ref[dst:dst + width, :] = wt_ref[lo:lo + width, :].astype(BF16)
        dst += width
    parts = [wt_ref[src[name][0]:src[name][0] + src[name][1], :] for name in ("c_kv", "k_i", "w_i")]
    lo = src["i_m"][0]
    parts.append(wt_ref[lo:lo + 2 * H_M, :])
    used = sum(p.shape[0] for p in parts)
    parts.append(jnp.zeros((N_SMALL - used, wt_ref.shape[1]), F32))
    small_ref[...] = jnp.concatenate(parts, axis=0).astype(BF16)


def _repack_w_in(w_in, tc=256):
    n_cols = sum(width for _, width in _W_IN_SEGS)
    wt = jnp.swapaxes(w_in, 1, 2)[0]
    return pl.pallas_call(
        _repack_kernel,
        grid=(D_MODEL // tc,),
        in_specs=[pl.BlockSpec((n_cols, tc), lambda i: (0, i))],
        out_specs=[pl.BlockSpec((N_MAIN, tc), lambda i: (0, i)),
                   pl.BlockSpec((N_SMALL, tc), lambda i: (0, i))],
        out_shape=[jax.ShapeDtypeStruct((N_MAIN, D_MODEL), BF16),
                   jax.ShapeDtypeStruct((N_SMALL, D_MODEL), BF16)],
        compiler_params=pltpu.CompilerParams(
            dimension_semantics=("arbitrary",), vmem_limit_bytes=VMEM_LIMIT),
        name="repack",
    )(wt)


def kernel(x, w_in, b_igate, b_fgate, kv_norm_g, w_uk, w_uv, idx_k_ln_g, idx_k_ln_b, rel_bias,
           conv_w, conv_b, mh_norm_g, w_out, ln_g, ln_b):
    B, T, D = x.shape
    assert D == D_MODEL and T % L_M == 0 and T % (2 * KB) == 0 and w_in.shape[0] == 1
    bias = _bias_tiles(rel_bias)
    x2d = x.reshape(B * T, D)
    w_main, w_small = _repack_w_in(w_in)
    main, small = _proj(x2d, w_main, w_small)
    ckv_n, ckv_t, kidx_n, gate_t = _prep(small, kv_norm_g[0][None], idx_k_ln_g[0][None], idx_k_ln_b[0][None])
    w_uk_t = jnp.transpose(w_uk[0], (0, 2, 1)).astype(BF16)
    w_uv_t = jnp.transpose(w_uv[0], (0, 2, 1)).astype(BF16)
    ya = _dsa(main, gate_t, ckv_n, ckv_t, kidx_n, w_uk_t, w_uv_t, bias, B, T)
    gb = jnp.zeros((LANES,), F32).at[S_IM:S_IM + H_M].set(b_igate[0]).at[S_FM:S_FM + H_M].set(b_fgate[0])
    out = _mlstm_out(main, small, gate_t, conv_w[0], conv_b[0][None], gb[None, :], gb[:, None],
                     mh_norm_g[0][None], ya, x2d, w_out[0].astype(BF16), ln_g[0][None], ln_b[0][None], B, T)
    return out.reshape(B, T, D)
```

```python
import functools
import math

import numpy as np
import jax
import jax.numpy as jnp
from jax import lax
from jax.experimental import pallas as pl
from jax.experimental.pallas import tpu as pltpu

F32 = jnp.float32
BF16 = jnp.bfloat16

D_MODEL = 2048
W_A = 1024
DH_A = 128
H_A = 8
D_C = 256
H_IDX = 16
D_IDX = 64
TOPK = 256
W_M = 1024
H_M = 4
DV_M = 256
DK_M = 128
CONV_W = 4
N_BUCKETS = 32
MAX_DIST = 128
ALPHA = 2.0 ** 0.25
LN_EPS = 1e-5

LANES = 128
SUBLANES = 8
VMEM_LIMIT = 56 * 1024 * 1024

QB = 256
KB = 256
KI = 128
CNT_ROWS = 256
CNT_ACC = 4 * SUBLANES
L_M = 256
HALO = 16
NEG = -1e30
LOG2E = math.log2(math.e)
ONES_ROWS = 16

C_QA, C_ZA, C_QI, C_QM, C_KM, C_VM, C_OM, C_ZM = 0, 1024, 2048, 3072, 3584, 4096, 5120, 6144
N_MAIN = 7168
N_SMALL = 384
S_KI, S_WI, S_IM, S_FM = 0, 64, 80, 84


def _t5_bucket_np(rel):
    max_exact = N_BUCKETS // 2
    n = np.maximum(rel, 0)
    nf = np.maximum(n, 1).astype(np.float32)
    large = max_exact + (np.log(nf / np.float32(max_exact)) / np.float32(math.log(MAX_DIST / max_exact))
                         * np.float32(N_BUCKETS - max_exact)).astype(np.int32)
    large = np.minimum(large, N_BUCKETS - 1)
    return np.where(n < max_exact, n, large).astype(np.int32)


FAR_BUCKET = int(_t5_bucket_np(np.array(2 * KB + 1)))


def _bucket_tiles():
    i = np.arange(QB)[None, :]
    j = np.arange(KB)[:, None]
    t0 = _t5_bucket_np(i - j)
    t1 = _t5_bucket_np(i - j + KB)
    assert (t5 := _t5_bucket_np(np.arange(KB + 1, 4096))).min() == t5.max() == FAR_BUCKET
    return np.stack([t0, t1]).astype(np.int32)


def _bias_kernel(bucket_ref, rb_ref, out_ref):
    h = pl.program_id(0)
    far = rb_ref[FAR_BUCKET, h]
    for k in range(2):
        bk = bucket_ref[k]
        acc = jnp.zeros((KB, QB), F32)
        for b in range(N_BUCKETS):
            acc = jnp.where(bk == b, rb_ref[b, h] - far, acc)
        out_ref[0, k] = acc * LOG2E


def _bias_tiles(rel_bias):
    bucket = jnp.asarray(_bucket_tiles())
    return pl.pallas_call(
        _bias_kernel,
        grid=(H_A,),
        in_specs=[pl.BlockSpec((2, KB, QB), lambda h: (0, 0, 0)),
                  pl.BlockSpec(memory_space=pltpu.SMEM)],
        out_specs=pl.BlockSpec((1, 2, KB, QB), lambda h: (h, 0, 0, 0)),
        out_shape=jax.ShapeDtypeStruct((H_A, 2, KB, QB), F32),
        name="bias_tiles",
    )(bucket, rel_bias)


_NT = (((1,), (1,)), ((), ()))


def _proj_kernel(x_ref, w_ref, ws_ref, o_ref, os_ref, xb_ref):
    @pl.when(pl.program_id(1) == 0)
    def _():
        xb_ref[...] = x_ref[...].astype(BF16)
        os_ref[...] = lax.dot_general(xb_ref[...], ws_ref[...], _NT, preferred_element_type=F32)

    o_ref[...] = lax.dot_general(xb_ref[...], w_ref[...], _NT, preferred_element_type=F32).astype(BF16)


def _proj(x2d, w_main, w_small, tm=1024, tn=1792):
    M = x2d.shape[0]
    return pl.pallas_call(
        _proj_kernel,
        grid=(M // tm, N_MAIN // tn),
        in_specs=[pl.BlockSpec((tm, D_MODEL), lambda i, j: (i, 0)),
                  pl.BlockSpec((tn, D_MODEL), lambda i, j: (j, 0)),
                  pl.BlockSpec((N_SMALL, D_MODEL), lambda i, j: (0, 0))],
        out_specs=[pl.BlockSpec((tm, tn), lambda i, j: (i, j)),
                   pl.BlockSpec((tm, N_SMALL), lambda i, j: (i, 0))],
        out_shape=[jax.ShapeDtypeStruct((M, N_MAIN), BF16),
                   jax.ShapeDtypeStruct((M, N_SMALL), F32)],
        scratch_shapes=[pltpu.VMEM((tm, D_MODEL), BF16)],
        compiler_params=pltpu.CompilerParams(
            dimension_semantics=("arbitrary", "arbitrary"), vmem_limit_bytes=VMEM_LIMIT),
        name="proj",
    )(x2d, w_main, w_small)


def _prep_kernel(s_ref, kvg_ref, ig_ref, ib_ref, ckv_ref, ckvt_ref, kidx_ref, gt_ref):
    c = s_ref[:, 0:D_C]
    c = c * lax.rsqrt(jnp.mean(c * c, axis=-1, keepdims=True) + LN_EPS) * kvg_ref[...]
    ckv_ref[...] = c.astype(BF16)
    for r in range(ckvt_ref.shape[0]):
        ckvt_ref[r, 0:D_C, :] = c[r * KB:(r + 1) * KB, :].T.astype(BF16)
        ckvt_ref[r, D_C:D_C + ONES_ROWS, :] = jnp.ones((ONES_ROWS, KB), BF16)
    tile = s_ref[:, D_C:D_C + LANES]
    k = tile[:, S_KI:S_KI + D_IDX]
    mu = jnp.mean(k, axis=-1, keepdims=True)
    var = jnp.mean(jnp.square(k - mu), axis=-1, keepdims=True)
    kidx_ref[...] = ((k - mu) * lax.rsqrt(var + LN_EPS) * ig_ref[...] + ib_ref[...]).astype(BF16)
    gt_ref[...] = tile.T


def _prep(small, kv_g, idx_g, idx_b, tm=2048):
    M = small.shape[0]
    return pl.pallas_call(
        _prep_kernel,
        grid=(M // tm,),
        in_specs=[pl.BlockSpec((tm, N_SMALL), lambda i: (i, 0)),
                  pl.BlockSpec((1, D_C), lambda i: (0, 0)),
                  pl.BlockSpec((1, D_IDX), lambda i: (0, 0)),
                  pl.BlockSpec((1, D_IDX), lambda i: (0, 0))],
        out_specs=[pl.BlockSpec((tm, D_C), lambda i: (i, 0)),
                   pl.BlockSpec((tm // KB, D_C + ONES_ROWS, KB), lambda i: (i, 0, 0)),
                   pl.BlockSpec((tm, D_IDX), lambda i: (i, 0)),
                   pl.BlockSpec((LANES, tm), lambda i: (0, i))],
        out_shape=[jax.ShapeDtypeStruct((M, D_C), BF16),
                   jax.ShapeDtypeStruct((M // KB, D_C + ONES_ROWS, KB), BF16),
                   jax.ShapeDtypeStruct((M, D_IDX), BF16),
                   jax.ShapeDtypeStruct((LANES, M), F32)],
        name="prep",
    )(small, kv_g, idx_g, idx_b)


def _key_to_float(key):
    bits = jnp.where(key < 0, key ^ jnp.int32(0x7FFFFFFF), key)
    return lax.bitcast_convert_type(bits, F32)


def _dsa_kernel(qa_ref, za_ref, qi_ref, gt_ref, ckv_ref, ckvt_ref, kidx_ref, wukt_ref, wuvt_ref, bias_ref,
                y_ref, qall_ref, qr_ref, ha_ref, hb_ref, sc_ref, mb_ref, sa_ref, sb_ref, pa_ref, pb_ref,
                ta_ref, tb_ref, m_ref, al_ref, acc_ref, w_ref, thr_ref, cut_ref, cge_ref):
    qi = pl.program_id(1)
    nkb = qi + 1
    last_kb = mb_ref.shape[0] - 1

    for h in range(H_A):
        ql = lax.dot_general(wukt_ref[h], qa_ref[:, h * DH_A:(h + 1) * DH_A], (((1,), (1,)), ((), ())),
                             preferred_element_type=F32)
        qall_ref[h // (H_A // 2), :, (h % (H_A // 2)) * QB:(h % (H_A // 2) + 1) * QB] = (
            ql * (DH_A ** -0.5 * LOG2E)).astype(BF16)


    @pl.when(qi * QB < TOPK)
    def _():
        krow = lax.broadcasted_iota(jnp.int32, (KB, QB), 0)
        qcol = lax.broadcasted_iota(jnp.int32, (KB, QB), 1)
        mb_ref[0] = jnp.where(krow <= qcol, 0.0, NEG).astype(F32)

    @pl.when(qi * QB >= TOPK)
    def _():
        for h in range(H_IDX):
            qr_ref[h * QB:(h + 1) * QB, :] = qi_ref[:, h * D_IDX:(h + 1) * D_IDX]
        w_ref[...] = gt_ref[S_WI:S_WI + H_IDX, :] * ((D_IDX ** -0.5) * (H_IDX ** -0.5))
        krow = lax.broadcasted_iota(jnp.int32, (KI, LANES), 0)
        qcol = lax.broadcasted_iota(jnp.int32, (KI, LANES), 1)
        n_ki = nkb * (KB // KI)
        last_ki = sc_ref.shape[1] // KI - 1

        def head_dots(ki, dst_ref):
            k = kidx_ref[pl.ds(pl.multiple_of(jnp.minimum(ki, last_ki) * KI, KI), KI), :]
            dst_ref[...] = lax.dot_general(k, qr_ref[...], (((1,), (1,)), ((), ())),
                                           preferred_element_type=F32)

        def reduce_heads(src_ref, ki):
            for g in range(QB // LANES):
                lanes = slice(g * LANES, (g + 1) * LANES)
                acc = jnp.zeros((KI, LANES), F32)
                for h in range(H_IDX):
                    acc = acc + (jnp.maximum(src_ref[:, h * QB + g * LANES:h * QB + (g + 1) * LANES], 0.0)
                                 * w_ref[h:h + 1, lanes])
                sc_ref[g, pl.ds(pl.multiple_of(ki * KI, KI), KI), :] = jnp.where(
                    krow + (ki * KI - qi * QB - g * LANES) <= qcol, acc, -jnp.inf)

        assert CNT_ROWS == KB

        head_dots(0, ha_ref)

        def sc_body(j, carry):
            head_dots(2 * j + 1, hb_ref)
            reduce_heads(ha_ref, 2 * j)
            head_dots(2 * j + 2, ha_ref)
            reduce_heads(hb_ref, 2 * j + 1)
            return carry
        lax.fori_loop(0, n_ki // 2, sc_body, 0)

        n_cnt = (nkb * KB + CNT_ROWS - 1) // CNT_ROWS

        def count_where(pred, steps=None):
            def body(c, acc):
                parts = []
                for g in range(QB // LANES):
                    lanes = slice(g * LANES, (g + 1) * LANES)
                    rows = CNT_ROWS
                    if steps is None:
                        blk = sc_ref[g, pl.ds(pl.multiple_of(c * CNT_ROWS, CNT_ROWS), CNT_ROWS), :]
                    else:
                        if c == steps - 1:
                            rows = min(CNT_ROWS, (g + 1) * LANES)
                        blk = sc_ref[g, c * CNT_ROWS:c * CNT_ROWS + rows, :]
                    hit = jnp.where(pred(blk, c * CNT_ROWS, lanes), 1.0, 0.0).astype(F32)
                    parts.append(jnp.sum(hit.reshape(rows // CNT_ACC, CNT_ACC, LANES), axis=0))
                return acc + jnp.concatenate(parts, axis=1)
            acc = jnp.zeros((CNT_ACC, QB), F32)
            if steps is None:
                acc = lax.fori_loop(0, n_cnt, body, acc)
            else:
                for c in range(steps):
                    acc = body(c, acc)
            return jnp.sum(acc, axis=0, keepdims=True)

        def search(steps):
            n_g = QB // LANES

            def count_ge(g, cand):
                lanes = slice(g * LANES, (g + 1) * LANES)
                acc = jnp.zeros((CNT_ACC, LANES), F32)
                for c in range(steps):
                    rows = CNT_ROWS if c < steps - 1 else min(CNT_ROWS, (g + 1) * LANES)
                    hit = jnp.where(sc_ref[g, c * CNT_ROWS:c * CNT_ROWS + rows, :] >= cand, 1.0, 0.0)
                    acc = acc + jnp.sum(hit.astype(F32).reshape(rows // CNT_ACC, CNT_ACC, LANES), axis=0)
                return jnp.sum(acc, axis=0, keepdims=True)

            def bit_body(i, carry):
                new = []
                for g in range(n_g):
                    u, c_ge = carry[g]
                    trial = u | lax.shift_left(jnp.int32(1), 31 - i)
                    cnt = count_ge(g, _key_to_float(trial ^ jnp.int32(-2 ** 31)))
                    ok = cnt >= float(TOPK)
                    new.append((jnp.where(ok, trial, u), jnp.where(ok, cnt, c_ge)))
                return tuple(new)
            init = tuple((jnp.zeros((1, LANES), jnp.int32),
                          jnp.full((1, LANES), float(sc_ref.shape[1]), F32)) for _ in range(n_g))
            for g, (u, c_ge) in enumerate(lax.fori_loop(0, 32, bit_body, init)):
                lanes = slice(g * LANES, (g + 1) * LANES)
                thr_ref[:, lanes] = _key_to_float(u ^ jnp.int32(-2 ** 31))
                cge_ref[:, lanes] = c_ge

        for steps in range(1, sc_ref.shape[1] // CNT_ROWS + 1):
            pl.when(n_cnt == steps)(functools.partial(search, steps))
        has_ties = jnp.max(cge_ref[...]) > float(TOPK)

        @pl.when(jnp.logical_not(has_ties))
        def _():
            def mb_body(kb, carry):
                for g in range(QB // LANES):
                    lanes = slice(g * LANES, (g + 1) * LANES)
                    blk = sc_ref[g, pl.ds(pl.multiple_of(kb * KB, KB), KB), :]
                    mb_ref[kb, :, lanes] = jnp.where(blk >= thr_ref[:, lanes], 0.0, NEG).astype(F32)
                return carry
            lax.fori_loop(0, nkb, mb_body, 0)

        @pl.when(has_ties)
        def _():
            c_gt = count_where(lambda blk, row0, lanes: blk > thr_ref[:, lanes])
            need = float(TOPK) - c_gt
            rows = lax.broadcasted_iota(jnp.int32, (CNT_ROWS, LANES), 0)
            n_bits = (sc_ref.shape[1] - 1).bit_length()

            def idx_body(i, cut):
                trial = cut | lax.shift_left(jnp.int32(1), n_bits - 1 - i)
                cut_ref[...] = trial
                before = count_where(lambda blk, row0, lanes: (blk == thr_ref[:, lanes])
                                     & (rows + row0 < cut_ref[:, lanes]))
                return jnp.where(before < need, trial, cut)
            cut_ref[...] = lax.fori_loop(0, n_bits, idx_body, jnp.zeros((1, QB), jnp.int32))

            def mb_body(kb, carry):
                for g in range(QB // LANES):
                    lanes = slice(g * LANES, (g + 1) * LANES)
                    blk = sc_ref[g, pl.ds(pl.multiple_of(kb * KB, KB), KB), :]
                    thr = thr_ref[:, lanes]
                    keep = (blk > thr) | ((blk == thr) & (rows[0:KB] + kb * KB <= cut_ref[:, lanes]))
                    mb_ref[kb, :, lanes] = jnp.where(keep, 0.0, NEG).astype(F32)
                return carry
            lax.fori_loop(0, nkb, mb_body, 0)

    m_ref[...] = jnp.full(m_ref.shape, NEG, F32)
    acc_ref[...] = jnp.zeros(acc_ref.shape, F32)

    hh = H_A // 2
    s_refs, p_refs, t_refs = (sa_ref, sb_ref), (pa_ref, pb_ref), (ta_ref, tb_ref)
    pb_ref[...] = jnp.zeros(pb_ref.shape, BF16)
    al_ref[...] = jnp.ones(al_ref.shape, F32)

    def logits(kb, half):
        kv = ckv_ref[pl.ds(pl.multiple_of(jnp.minimum(kb, last_kb) * KB, KB), KB), :]
        s_refs[half][...] = jnp.dot(kv, qall_ref[half], preferred_element_type=F32)

    def softmax(kb, half, with_bias):
        tile = jnp.clip(qi - kb, 0, 1)
        for j in range(hh):
            for g in range(QB // LANES):
                lanes = slice(g * LANES, (g + 1) * LANES)
                cols = slice(j * QB + g * LANES, j * QB + (g + 1) * LANES)
                x = s_refs[half][:, cols] + mb_ref[kb, :, lanes]
                if with_bias:
                    x = x + bias_ref[half * hh + j, tile, :, lanes]
                m_prev = m_ref[half, :, cols]
                m_blk = jnp.max(x.reshape(KB // CNT_ACC, CNT_ACC, LANES), axis=0)
                m_new = jnp.maximum(m_prev, jnp.max(m_blk, axis=0, keepdims=True))
                al_ref[half, :, cols] = jnp.exp2(m_prev - m_new)
                m_ref[half, :, cols] = m_new
                p_refs[half][:, cols] = jnp.exp2(x - m_new).astype(BF16)

    def accumulate(kb, half):
        t_refs[half][...] = jnp.dot(ckvt_ref[jnp.maximum(kb, 0)], p_refs[half][...],
                                    preferred_element_type=F32)
        acc_ref[half] = acc_ref[half] * al_ref[half] + t_refs[half][...]

    def sweep(first_kb, end_kb, with_bias):
        def body(kb, carry):
            logits(kb, 1)
            softmax(kb, 0, with_bias)
            accumulate(kb - 1, 1)
            logits(kb + 1, 0)
            softmax(kb, 1, with_bias)
            accumulate(kb, 0)
            return carry
        lax.fori_loop(first_kb, end_kb, body, 0)

    n_far = jnp.maximum(qi - 1, 0)
    logits(0, 0)
    sweep(0, n_far, False)
    sweep(n_far, nkb, True)
    accumulate(nkb - 1, 1)

    for h in range(H_A):
        half, cols = h // hh, slice((h % hh) * QB, (h % hh + 1) * QB)
        ya_t = jnp.dot(wuvt_ref[h], acc_ref[half, 0:D_C, cols].astype(BF16), preferred_element_type=F32)
        ya = (ya_t / acc_ref[half, D_C:D_C + 1, cols]).T
        z = za_ref[:, h * DH_A:(h + 1) * DH_A].astype(F32)
        y_ref[:, h * DH_A:(h + 1) * DH_A] = (ya * (z * jax.nn.sigmoid(z))).astype(BF16)


def _dsa(main, gate_t, ckv_n, ckv_t, kidx_n, w_uk_t, w_uv_t, bias, B, T):
    nq = T // QB
    return pl.pallas_call(
        _dsa_kernel,
        grid=(B, nq),
        in_specs=[pl.BlockSpec((QB, W_A), lambda b, q: (b * nq + q, C_QA // W_A)),
                  pl.BlockSpec((QB, W_A), lambda b, q: (b * nq + q, C_ZA // W_A)),
                  pl.BlockSpec((QB, H_IDX * D_IDX), lambda b, q: (b * nq + q, C_QI // (H_IDX * D_IDX))),
                  pl.BlockSpec((LANES, QB), lambda b, q: (0, b * nq + q)),
                  pl.BlockSpec((T, D_C), lambda b, q: (b, 0)),
                  pl.BlockSpec((T // KB, D_C + ONES_ROWS, KB), lambda b, q: (b, 0, 0)),
                  pl.BlockSpec((T, D_IDX), lambda b, q: (b, 0)),
                  pl.BlockSpec((H_A, D_C, DH_A), lambda b, q: (0, 0, 0)),
                  pl.BlockSpec((H_A, DH_A, D_C), lambda b, q: (0, 0, 0)),
                  pl.BlockSpec((H_A, 2, KB, QB), lambda b, q: (0, 0, 0, 0))],
        out_specs=pl.BlockSpec((QB, W_A), lambda b, q: (b * nq + q, 0)),
        out_shape=jax.ShapeDtypeStruct((B * T, W_A), BF16),
        scratch_shapes=[pltpu.VMEM((2, D_C, H_A // 2 * QB), BF16),
                        pltpu.VMEM((H_IDX * QB, D_IDX), BF16),
                        pltpu.VMEM((KI, H_IDX * QB), F32),
                        pltpu.VMEM((KI, H_IDX * QB), F32),
                        pltpu.VMEM((QB // LANES, T, LANES), F32),
                        pltpu.VMEM((T // KB, KB, QB), F32),
                        pltpu.VMEM((KB, H_A // 2 * QB), F32),
                        pltpu.VMEM((KB, H_A // 2 * QB), F32),
                        pltpu.VMEM((KB, H_A // 2 * QB), BF16),
                        pltpu.VMEM((KB, H_A // 2 * QB), BF16),
                        pltpu.VMEM((D_C + ONES_ROWS, H_A // 2 * QB), F32),
                        pltpu.VMEM((D_C + ONES_ROWS, H_A // 2 * QB), F32),
                        pltpu.VMEM((2, 1, H_A // 2 * QB), F32),
                        pltpu.VMEM((2, 1, H_A // 2 * QB), F32),
                        pltpu.VMEM((2, D_C + ONES_ROWS, H_A // 2 * QB), F32),
                        pltpu.VMEM((H_IDX, QB), F32),
                        pltpu.VMEM((1, QB), F32),
                        pltpu.VMEM((1, QB), jnp.int32),
                        pltpu.VMEM((1, QB), F32)],
        compiler_params=pltpu.CompilerParams(
            dimension_semantics=("arbitrary", "arbitrary"), vmem_limit_bytes=VMEM_LIMIT),
        name="dsa",
    )(main, main, main, gate_t, ckv_n, ckv_t, kidx_n, w_uk_t, w_uv_t, bias)


def _split_dot(tri, x):
    hi = x.astype(BF16)
    lo = (x - hi.astype(F32)).astype(BF16)
    return jnp.dot(tri, hi, preferred_element_type=F32) + jnp.dot(tri, lo, preferred_element_type=F32)


def _log_sigmoid(x):
    return jnp.minimum(x, 0.0) - jnp.log1p(jnp.exp(-jnp.abs(x)))


def _mlstm_out_kernel(q_ref, k_ref, qh_ref, kh_ref, v_ref, o_ref, z_ref, g_ref, gt_ref,
                      cw_ref, cb_ref, gbr_ref, gbc_ref, ng_ref, ya_ref, x_ref, w_ref, lg_ref, lb_ref,
                      out_ref, ct_ref, m_ref, y_ref, yp_ref, p_ref, *, n_chunks, n_steps):
    step = pl.program_id(0)
    c = jnp.minimum(step, n_steps - 1) % n_chunks
    L = L_M

    @pl.when(step == 0)
    def _():
        yp_ref[...] = jnp.zeros(yp_ref.shape, BF16)

    @pl.when(c == 0)
    def _():
        ct_ref[...] = jnp.zeros(ct_ref.shape, F32)
        m_ref[...] = jnp.zeros(m_ref.shape, F32)

    n_pc = 2 * H_M
    pw = D_MODEL // n_pc

    def project_chunk(j):
        cols = slice(j * pw, (j + 1) * pw)
        p_ref[:, cols] = (jnp.dot(ya_ref[...], w_ref[0:W_A, cols], preferred_element_type=F32)
                          + jnp.dot(yp_ref[...], w_ref[W_A:W_A + W_M, cols], preferred_element_type=F32))

    def norm_previous():
        res = ALPHA * x_ref[...] + p_ref[...]
        mean = jnp.mean(res, axis=-1, keepdims=True)
        var_r = jnp.mean(jnp.square(res - mean), axis=-1, keepdims=True)
        out_ref[...] = (res - mean) * lax.rsqrt(var_r + LN_EPS) * lg_ref[...] + lb_ref[...]

    pending = [functools.partial(project_chunk, j) for j in range(n_pc)] + [norm_previous]

    def emit_projection_work():
        if pending:
            pending.pop(0)()

    emit_projection_work()

    r = lax.broadcasted_iota(jnp.int32, (L, L), 0)
    s = lax.broadcasted_iota(jnp.int32, (L, L), 1)
    causal = s <= r
    shifts = [jnp.where(r - s == d, 1.0, 0.0).astype(BF16) for d in range(1, CONV_W)]

    def conv_silu(x_ref, halo_ref, lo):
        x = x_ref[...]
        halo = jnp.where(c > 0, halo_ref[...].astype(F32), 0.0)
        w = cw_ref[:, lo:lo + H_M * DK_M]
        y = cb_ref[:, lo:lo + H_M * DK_M] + w[CONV_W - 1:CONV_W] * x.astype(F32)
        top = jnp.zeros((SUBLANES, H_M * DK_M), F32)
        for d in range(1, CONV_W):
            wd = w[CONV_W - 1 - d:CONV_W - d]
            y = y + wd * jnp.dot(shifts[d - 1], x, preferred_element_type=F32)
            top = top + wd * jnp.concatenate(
                [halo[HALO - d:HALO], jnp.zeros((SUBLANES - d, H_M * DK_M), F32)], axis=0)
        y = jnp.concatenate([y[0:SUBLANES] + top, y[SUBLANES:]], axis=0)
        return y * jax.nn.sigmoid(y)

    q_all = conv_silu(q_ref, qh_ref, 0)
    emit_projection_work()
    k_all = conv_silu(k_ref, kh_ref, H_M * DK_M) * (DK_M ** -0.5)
    emit_projection_work()

    gc = g_ref[...] + gbr_ref[...]
    gr = gt_ref[S_IM:S_IM + 2 * H_M, :] + gbc_ref[S_IM:S_IM + 2 * H_M, :]
    tri_l = jnp.where(causal, 1.0, 0.0).astype(BF16)
    tri_u = jnp.where(r <= s, 1.0, 0.0).astype(BF16)
    b_cols = _split_dot(tri_l, _log_sigmoid(gc) * LOG2E)
    lf_rows = _log_sigmoid(gr) * LOG2E
    b_rows = jnp.dot(lf_rows.astype(BF16), tri_u, preferred_element_type=F32) \
        + jnp.dot((lf_rows - lf_rows.astype(BF16).astype(F32)).astype(BF16), tri_u,
                  preferred_element_type=F32)
    gc = gc * LOG2E
    gr = gr * LOG2E
    ones = jnp.ones((L, LANES), BF16)
    emit_projection_work()

    for h in range(H_M):
        emit_projection_work()
        q = q_all[:, h * DK_M:(h + 1) * DK_M]
        k = k_all[:, h * DK_M:(h + 1) * DK_M]
        v = jnp.concatenate([v_ref[:, h * DV_M:(h + 1) * DV_M], ones], axis=1)
        qb = q.astype(BF16)
        b_c = b_cols[:, S_FM + h:S_FM + h + 1]
        i_c = gc[:, S_IM + h:S_IM + h + 1]
        b_r = b_rows[H_M + h:H_M + h + 1, :]
        i_r = gr[h:h + 1, :]
        m_prev = m_ref[h]
        ct = ct_ref[h]

        log_d = jnp.where(causal, b_c - b_r + i_r, -jnp.inf)
        g = b_c + m_prev
        m_t = jnp.maximum(jnp.max(log_d, axis=-1, keepdims=True), g)
        qk = lax.dot_general(qb, k.astype(BF16), _NT, preferred_element_type=F32)
        s_mat = qk * jnp.exp2(log_d - m_t)
        inter = jnp.exp2(g - m_t)
        num = jnp.dot(s_mat.astype(BF16), v, preferred_element_type=F32) \
            + inter * jnp.dot(qb, ct.astype(BF16), preferred_element_type=F32)
        den = jnp.maximum(jnp.abs(num[:, DV_M:]), jnp.exp2(-m_t))
        hh = num[:, 0:DV_M] / jnp.concatenate([den] * (DV_M // LANES), axis=1)

        emit_projection_work()
        b_last = b_c[L - 1:L, :]
        a_r = b_last - b_r + i_r
        m_new = jnp.maximum(b_last + m_prev, jnp.max(a_r, axis=-1, keepdims=True))
        decay = jnp.exp2(b_last + m_prev - m_new)
        wgt_c = jnp.exp2(b_last - b_c + i_c - m_new)
        kw = k * wgt_c
        ct_ref[h] = decay * ct + jnp.dot(kw.T.astype(BF16), v, preferred_element_type=F32)
        m_ref[h] = m_new

        mu = jnp.mean(hh, axis=-1, keepdims=True)
        var = jnp.mean(jnp.square(hh - mu), axis=-1, keepdims=True)
        hn = (hh - mu) * lax.rsqrt(var + LN_EPS) * ng_ref[:, h * DV_M:(h + 1) * DV_M]
        og = o_ref[:, h * DV_M:(h + 1) * DV_M].astype(F32)
        zg = z_ref[:, h * DV_M:(h + 1) * DV_M].astype(F32)
        y_ref[:, h * DV_M:(h + 1) * DV_M] = (hn * jax.nn.sigmoid(og) * (zg * jax.nn.sigmoid(zg))).astype(BF16)

    assert not pending
    yp_ref[...] = y_ref[...]


def _mlstm_out(main, small, gate_t, conv_w, conv_b, gb_row, gb_col, norm_g, ya, x2d, w_out, ln_g, ln_b, B, T):
    nc = T // L_M
    hb = L_M // HALO
    qk_w = H_M * DK_M
    n_steps = B * nc

    def cur(col):
        return lambda s: (jnp.minimum(s, n_steps - 1), col)

    def prev(s):
        return (jnp.maximum(s - 1, 0), 0)

    def halo_map(col):
        return lambda s: (jnp.maximum(jnp.minimum(s, n_steps - 1) * hb - 1, 0), col)

    const = lambda s: (0, 0)
    return pl.pallas_call(
        functools.partial(_mlstm_out_kernel, n_chunks=nc, n_steps=n_steps),
        grid=(n_steps + 1,),
        in_specs=[pl.BlockSpec((L_M, qk_w), cur(C_QM // qk_w)),
                  pl.BlockSpec((L_M, qk_w), cur(C_KM // qk_w)),
                  pl.BlockSpec((HALO, qk_w), halo_map(C_QM // qk_w)),
                  pl.BlockSpec((HALO, qk_w), halo_map(C_KM // qk_w)),
                  pl.BlockSpec((L_M, W_M), cur(C_VM // W_M)),
                  pl.BlockSpec((L_M, W_M), cur(C_OM // W_M)),
                  pl.BlockSpec((L_M, W_M), cur(C_ZM // W_M)),
                  pl.BlockSpec((L_M, LANES), cur(D_C // LANES)),
                  pl.BlockSpec((LANES, L_M), lambda s: (0, jnp.minimum(s, n_steps - 1))),
                  pl.BlockSpec((CONV_W, 2 * qk_w), const),
                  pl.BlockSpec((1, 2 * qk_w), const),
                  pl.BlockSpec((1, LANES), const),
                  pl.BlockSpec((LANES, 1), const),
                  pl.BlockSpec((1, W_M), const),
                  pl.BlockSpec((L_M, W_A), prev),
                  pl.BlockSpec((L_M, D_MODEL), prev),
                  pl.BlockSpec((W_A + W_M, D_MODEL), const),
                  pl.BlockSpec((1, D_MODEL), const),
                  pl.BlockSpec((1, D_MODEL), const)],
        out_specs=pl.BlockSpec((L_M, D_MODEL), prev),
        out_shape=jax.ShapeDtypeStruct((B * T, D_MODEL), F32),
        scratch_shapes=[pltpu.VMEM((H_M, DK_M, DV_M + LANES), F32),
                        pltpu.VMEM((H_M, 1, 1), F32),
                        pltpu.VMEM((L_M, W_M), BF16),
                        pltpu.VMEM((L_M, W_M), BF16),
                        pltpu.VMEM((L_M, D_MODEL), F32)],
        compiler_params=pltpu.CompilerParams(
            dimension_semantics=("arbitrary",), vmem_limit_bytes=VMEM_LIMIT),
        name="mlstm_out",
    )(main, main, main, main, main, main, main, small, gate_t,
      conv_w, conv_b, gb_row, gb_col, norm_g, ya, x2d, w_out, ln_g, ln_b)


_W_IN_SEGS = (("q_a", W_A), ("c_kv", D_C), ("z_a", W_A), ("q_i", H_IDX * D_IDX), ("k_i", D_IDX),
              ("w_i", H_IDX), ("q_m", H_M * DK_M), ("k_m", H_M * DK_M), ("v_m", W_M), ("i_m", H_M),
              ("f_m", H_M), ("o_m", W_M), ("z_m", W_M))
_MAIN_ORDER = ("q_a", "z_a", "q_i", "q_m", "k_m", "v_m", "o_m", "z_m")
_SEG_NAMES = [name for name, _ in _W_IN_SEGS]
assert _SEG_NAMES.index("f_m") == _SEG_NAMES.index("i_m") + 1 and S_FM == S_IM + H_M


def _repack_kernel(wt_ref, main_ref, small_ref):
    src, off = {}, 0
    for name, width in _W_IN_SEGS:
        src[name] = (off, width)
        off += width
    dst = 0
    for name in _MAIN_ORDER:
        lo, width = src[name]
        main_ref[dst:dst + width, :] = wt_ref[lo:lo + width, :].astype(BF16)
        dst += width
    parts = [wt_ref[src[name][0]:src[name][0] + src[name][1], :] for name in ("c_kv", "k_i", "w_i")]
    lo = src["i_m"][0]
    parts.append(wt_ref[lo:lo + 2 * H_M, :])
    used = sum(p.shape[0] for p in parts)
    parts.append(jnp.zeros((N_SMALL - used, wt_ref.shape[1]), F32))
    small_ref[...] = jnp.concatenate(parts, axis=0).astype(BF16)


def _repack_w_in(w_in, tc=512):
    n_cols = sum(width for _, width in _W_IN_SEGS)
    wt = jnp.swapaxes(w_in, 1, 2)[0]
    return pl.pallas_call(
        _repack_kernel,
        grid=(D_MODEL // tc,),
        in_specs=[pl.BlockSpec((n_cols, tc), lambda i: (0, i))],
        out_specs=[pl.BlockSpec((N_MAIN, tc), lambda i: (0, i)),
                   pl.BlockSpec((N_SMALL, tc), lambda i: (0, i))],
        out_shape=[jax.ShapeDtypeStruct((N_MAIN, D_MODEL), BF16),
                   jax.ShapeDtypeStruct((N_SMALL, D_MODEL), BF16)],
        compiler_params=pltpu.CompilerParams(
            dimension_semantics=("arbitrary",), vmem_limit_bytes=VMEM_LIMIT),
        name="repack",
    )(wt)


def kernel(x, w_in, b_igate, b_fgate, kv_norm_g, w_uk, w_uv, idx_k_ln_g, idx_k_ln_b, rel_bias,
           conv_w, conv_b, mh_norm_g, w_out, ln_g, ln_b):
    B, T, D = x.shape
    assert D == D_MODEL and T % L_M == 0 and T % (2 * KB) == 0 and w_in.shape[0] == 1
    bias = _bias_tiles(rel_bias)
    x2d = x.reshape(B * T, D)
    w_main, w_small = _repack_w_in(w_in)
    main, small = _proj(x2d, w_main, w_small)
    ckv_n, ckv_t, kidx_n, gate_t = _prep(small, kv_norm_g[0][None], idx_k_ln_g[0][None], idx_k_ln_b[0][None])
    w_uk_t = jnp.transpose(w_uk[0], (0, 2, 1)).astype(BF16)
    w_uv_t = jnp.transpose(w_uv[0], (0, 2, 1)).astype(BF16)
    ya = _dsa(main, gate_t, ckv_n, ckv_t, kidx_n, w_uk_t, w_uv_t, bias, B, T)
    gb = jnp.zeros((LANES,), F32).at[S_IM:S_IM + H_M].set(b_igate[0]).at[S_FM:S_FM + H_M].set(b_fgate[0])
    out = _mlstm_out(main, small, gate_t, conv_w[0], conv_b[0][None], gb[None, :], gb[:, None],
                     mh_norm_g[0][None], ya, x2d, w_out[0].astype(BF16), ln_g[0][None], ln_b[0][None], B, T)
    return out.reshape(B, T, D)
```

```python
import functools
import math

import numpy as np
import jax
import jax.numpy as jnp
from jax import lax
from jax.experimental import pallas as pl
from jax.experimental.pallas import tpu as pltpu

F32 = jnp.float32
BF16 = jnp.bfloat16

D_MODEL = 2048
W_A = 1024
DH_A = 128
H_A = 8
D_C = 256
H_IDX = 16
D_IDX = 64
TOPK = 256
W_M = 1024
H_M = 4
DV_M = 256
DK_M = 128
CONV_W = 4
N_BUCKETS = 32
MAX_DIST = 128
ALPHA = 2.0 ** 0.25
LN_EPS = 1e-5

LANES = 128
SUBLANES = 8
VMEM_LIMIT = 56 * 1024 * 1024

QB = 256
KB = 256
KI = 128
CNT_ROWS = 256
CNT_ACC = 4 * SUBLANES
L_M = 256
HALO = 16
NEG = -1e30
LOG2E = math.log2(math.e)
ONES_ROWS = 16

C_QA, C_ZA, C_QI, C_QM, C_KM, C_VM, C_OM, C_ZM = 0, 1024, 2048, 3072, 3584, 4096, 5120, 6144
N_MAIN = 7168
N_SMALL = 384
S_KI, S_WI, S_IM, S_FM = 0, 64, 80, 84


def _t5_bucket_np(rel):
    max_exact = N_BUCKETS // 2
    n = np.maximum(rel, 0)
    nf = np.maximum(n, 1).astype(np.float32)
    large = max_exact + (np.log(nf / np.float32(max_exact)) / np.float32(math.log(MAX_DIST / max_exact))
                         * np.float32(N_BUCKETS - max_exact)).astype(np.int32)
    large = np.minimum(large, N_BUCKETS - 1)
    return np.where(n < max_exact, n, large).astype(np.int32)


FAR_BUCKET = int(_t5_bucket_np(np.array(2 * KB + 1)))


def _bucket_tiles():
    i = np.arange(QB)[None, :]
    j = np.arange(KB)[:, None]
    t0 = _t5_bucket_np(i - j)
    t1 = _t5_bucket_np(i - j + KB)
    assert (t5 := _t5_bucket_np(np.arange(KB + 1, 4096))).min() == t5.max() == FAR_BUCKET
    return np.stack([t0, t1]).astype(np.int32)


def _bias_kernel(bucket_ref, rb_ref, out_ref):
    h = pl.program_id(0)
    far = rb_ref[FAR_BUCKET, h]
    for k in range(2):
        bk = bucket_ref[k]
        acc = jnp.zeros((KB, QB), F32)
        for b in range(N_BUCKETS):
            acc = jnp.where(bk == b, rb_ref[b, h] - far, acc)
        out_ref[0, k] = acc * LOG2E


def _bias_tiles(rel_bias):
    bucket = jnp.asarray(_bucket_tiles())
    return pl.pallas_call(
        _bias_kernel,
        grid=(H_A,),
        in_specs=[pl.BlockSpec((2, KB, QB), lambda h: (0, 0, 0)),
                  pl.BlockSpec(memory_space=pltpu.SMEM)],
        out_specs=pl.BlockSpec((1, 2, KB, QB), lambda h: (h, 0, 0, 0)),
        out_shape=jax.ShapeDtypeStruct((H_A, 2, KB, QB), F32),
        name="bias_tiles",
    )(bucket, rel_bias)


_NT = (((1,), (1,)), ((), ()))


def _proj_kernel(x_ref, w_ref, ws_ref, o_ref, os_ref, xb_ref):
    @pl.when(pl.program_id(1) == 0)
    def _():
        xb_ref[...] = x_ref[...].astype(BF16)
        os_ref[...] = lax.dot_general(xb_ref[...], ws_ref[...], _NT, preferred_element_type=F32)

    o_ref[...] = lax.dot_general(xb_ref[...], w_ref[...], _NT, preferred_element_type=F32).astype(BF16)


def _proj(x2d, w_main, w_small, tm=1024, tn=1792):
    M = x2d.shape[0]
    return pl.pallas_call(
        _proj_kernel,
        grid=(M // tm, N_MAIN // tn),
        in_specs=[pl.BlockSpec((tm, D_MODEL), lambda i, j: (i, 0)),
                  pl.BlockSpec((tn, D_MODEL), lambda i, j: (j, 0)),
                  pl.BlockSpec((N_SMALL, D_MODEL), lambda i, j: (0, 0))],
        out_specs=[pl.BlockSpec((tm, tn), lambda i, j: (i, j)),
                   pl.BlockSpec((tm, N_SMALL), lambda i, j: (i, 0))],
        out_shape=[jax.ShapeDtypeStruct((M, N_MAIN), BF16),
                   jax.ShapeDtypeStruct((M, N_SMALL), F32)],
        scratch_shapes=[pltpu.VMEM((tm, D_MODEL), BF16)],
        compiler_params=pltpu.CompilerParams(
            dimension_semantics=("arbitrary", "arbitrary"), vmem_limit_bytes=VMEM_LIMIT),
        name="proj",
    )(x2d, w_main, w_small)


def _prep_kernel(s_ref, kvg_ref, ig_ref, ib_ref, ckv_ref, ckvt_ref, kidx_ref, gt_ref):
    c = s_ref[:, 0:D_C]
    c = c * lax.rsqrt(jnp.mean(c * c, axis=-1, keepdims=True) + LN_EPS) * kvg_ref[...]
    ckv_ref[...] = c.astype(BF16)
    for r in range(ckvt_ref.shape[0]):
        ckvt_ref[r, 0:D_C, :] = c[r * KB:(r + 1) * KB, :].T.astype(BF16)
        ckvt_ref[r, D_C:D_C + ONES_ROWS, :] = jnp.ones((ONES_ROWS, KB), BF16)
    tile = s_ref[:, D_C:D_C + LANES]
    k = tile[:, S_KI:S_KI + D_IDX]
    mu = jnp.mean(k, axis=-1, keepdims=True)
    var = jnp.mean(jnp.square(k - mu), axis=-1, keepdims=True)
    kidx_ref[...] = ((k - mu) * lax.rsqrt(var + LN_EPS) * ig_ref[...] + ib_ref[...]).astype(BF16)
    gt_ref[...] = tile.T


def _prep(small, kv_g, idx_g, idx_b, tm=2048):
    M = small.shape[0]
    return pl.pallas_call(
        _prep_kernel,
        grid=(M // tm,),
        in_specs=[pl.BlockSpec((tm, N_SMALL), lambda i: (i, 0)),
                  pl.BlockSpec((1, D_C), lambda i: (0, 0)),
                  pl.BlockSpec((1, D_IDX), lambda i: (0, 0)),
                  pl.BlockSpec((1, D_IDX), lambda i: (0, 0))],
        out_specs=[pl.BlockSpec((tm, D_C), lambda i: (i, 0)),
                   pl.BlockSpec((tm // KB, D_C + ONES_ROWS, KB), lambda i: (i, 0, 0)),
                   pl.BlockSpec((tm, D_IDX), lambda i: (i, 0)),
                   pl.BlockSpec((LANES, tm), lambda i: (0, i))],
        out_shape=[jax.ShapeDtypeStruct((M, D_C), BF16),
                   jax.ShapeDtypeStruct((M // KB, D_C + ONES_ROWS, KB), BF16),
                   jax.ShapeDtypeStruct((M, D_IDX), BF16),
                   jax.ShapeDtypeStruct((LANES, M), F32)],
        name="prep",
    )(small, kv_g, idx_g, idx_b)


def _key_to_float(key):
    bits = jnp.where(key < 0, key ^ jnp.int32(0x7FFFFFFF), key)
    return lax.bitcast_convert_type(bits, F32)


def _dsa_kernel(qa_ref, za_ref, qi_ref, gt_ref, ckv_ref, ckvt_ref, kidx_ref, wukt_ref, wuvt_ref, bias_ref,
                y_ref, qall_ref, qr_ref, ha_ref, hb_ref, sc_ref, mb_ref, sa_ref, sb_ref, pa_ref, pb_ref,
                ta_ref, tb_ref, m_ref, al_ref, acc_ref, w_ref, thr_ref, cut_ref, cge_ref):
    qi = pl.program_id(1)
    nkb = qi + 1
    last_kb = mb_ref.shape[0] - 1

    for h in range(H_A):
        ql = lax.dot_general(wukt_ref[h], qa_ref[:, h * DH_A:(h + 1) * DH_A], (((1,), (1,)), ((), ())),
                             preferred_element_type=F32)
        qall_ref[h // (H_A // 2), :, (h % (H_A // 2)) * QB:(h % (H_A // 2) + 1) * QB] = (
            ql * (DH_A ** -0.5 * LOG2E)).astype(BF16)


    @pl.when(qi * QB < TOPK)
    def _():
        krow = lax.broadcasted_iota(jnp.int32, (KB, QB), 0)
        qcol = lax.broadcasted_iota(jnp.int32, (KB, QB), 1)
        mb_ref[0] = jnp.where(krow <= qcol, 0.0, NEG).astype(F32)

    @pl.when(qi * QB >= TOPK)
    def _():
        for h in range(H_IDX):
            qr_ref[h * QB:(h + 1) * QB, :] = qi_ref[:, h * D_IDX:(h + 1) * D_IDX]
        w_ref[...] = gt_ref[S_WI:S_WI + H_IDX, :] * ((D_IDX ** -0.5) * (H_IDX ** -0.5))
        krow = lax.broadcasted_iota(jnp.int32, (KI, LANES), 0)
        qcol = lax.broadcasted_iota(jnp.int32, (KI, LANES), 1)
        n_ki = nkb * (KB // KI)
        last_ki = sc_ref.shape[0] // KI - 1

        def head_dots(ki, dst_ref):
            k = kidx_ref[pl.ds(pl.multiple_of(jnp.minimum(ki, last_ki) * KI, KI), KI), :]
            dst_ref[...] = lax.dot_general(k, qr_ref[...], (((1,), (1,)), ((), ())),
                                           preferred_element_type=F32)

        def reduce_heads(src_ref, ki):
            for g in range(QB // LANES):
                lanes = slice(g * LANES, (g + 1) * LANES)
                acc = jnp.zeros((KI, LANES), F32)
                for h in range(H_IDX):
                    acc = acc + (jnp.maximum(src_ref[:, h * QB + g * LANES:h * QB + (g + 1) * LANES], 0.0)
                                 * w_ref[h:h + 1, lanes])
                sc_ref[pl.ds(pl.multiple_of(ki * KI, KI), KI), lanes] = jnp.where(
                    krow + (ki * KI - qi * QB - g * LANES) <= qcol, acc, -jnp.inf)

        assert CNT_ROWS == KB

        head_dots(0, ha_ref)

        def sc_body(j, carry):
            head_dots(2 * j + 1, hb_ref)
            reduce_heads(ha_ref, 2 * j)
            head_dots(2 * j + 2, ha_ref)
            reduce_heads(hb_ref, 2 * j + 1)
            return carry
        lax.fori_loop(0, n_ki // 2, sc_body, 0)

        n_cnt = (nkb * KB + CNT_ROWS - 1) // CNT_ROWS

        def count_where(pred, steps=None):
            def body(c, acc):
                parts = []
                for g in range(QB // LANES):
                    lanes = slice(g * LANES, (g + 1) * LANES)
                    rows = CNT_ROWS
                    if steps is None:
                        blk = sc_ref[pl.ds(pl.multiple_of(c * CNT_ROWS, CNT_ROWS), CNT_ROWS), lanes]
                    else:
                        if c == steps - 1:
                            rows = min(CNT_ROWS, (g + 1) * LANES)
                        blk = sc_ref[c * CNT_ROWS:c * CNT_ROWS + rows, lanes]
                    hit = jnp.where(pred(blk, c * CNT_ROWS, lanes), 1.0, 0.0).astype(F32)
                    parts.append(jnp.sum(hit.reshape(rows // CNT_ACC, CNT_ACC, LANES), axis=0))
                return acc + jnp.concatenate(parts, axis=1)
            acc = jnp.zeros((CNT_ACC, QB), F32)
            if steps is None:
                acc = lax.fori_loop(0, n_cnt, body, acc)
            else:
                for c in range(steps):
                    acc = body(c, acc)
            return jnp.sum(acc, axis=0, keepdims=True)

        def search(steps):
            n_g = QB // LANES

            def count_ge(g, cand):
                lanes = slice(g * LANES, (g + 1) * LANES)
                acc = jnp.zeros((CNT_ACC, LANES), F32)
                for c in range(steps):
                    rows = CNT_ROWS if c < steps - 1 else min(CNT_ROWS, (g + 1) * LANES)
                    hit = jnp.where(sc_ref[c * CNT_ROWS:c * CNT_ROWS + rows, lanes] >= cand, 1.0, 0.0)
                    acc = acc + jnp.sum(hit.astype(F32).reshape(rows // CNT_ACC, CNT_ACC, LANES), axis=0)
                return jnp.sum(acc, axis=0, keepdims=True)

            def bit_body(i, carry):
                new = []
                for g in range(n_g):
                    u, c_ge = carry[g]
                    trial = u | lax.shift_left(jnp.int32(1), 31 - i)
                    cnt = count_ge(g, _key_to_float(trial ^ jnp.int32(-2 ** 31)))
                    ok = cnt >= float(TOPK)
                    new.append((jnp.where(ok, trial, u), jnp.where(ok, cnt, c_ge)))
                return tuple(new)
            init = tuple((jnp.zeros((1, LANES), jnp.int32),
                          jnp.full((1, LANES), float(sc_ref.shape[0]), F32)) for _ in range(n_g))
            for g, (u, c_ge) in enumerate(lax.fori_loop(0, 32, bit_body, init)):
                lanes = slice(g * LANES, (g + 1) * LANES)
                thr_ref[:, lanes] = _key_to_float(u ^ jnp.int32(-2 ** 31))
                cge_ref[:, lanes] = c_ge

        for steps in range(1, sc_ref.shape[0] // CNT_ROWS + 1):
            pl.when(n_cnt == steps)(functools.partial(search, steps))
        has_ties = jnp.max(cge_ref[...]) > float(TOPK)

        @pl.when(jnp.logical_not(has_ties))
        def _():
            def mb_body(kb, carry):
                blk = sc_ref[pl.ds(pl.multiple_of(kb * KB, KB), KB), :]
                mb_ref[kb] = jnp.where(blk >= thr_ref[...], 0.0, NEG).astype(F32)
                return carry
            lax.fori_loop(0, nkb, mb_body, 0)

        @pl.when(has_ties)
        def _():
            c_gt = count_where(lambda blk, row0, lanes: blk > thr_ref[:, lanes])
            need = float(TOPK) - c_gt
            rows = lax.broadcasted_iota(jnp.int32, (CNT_ROWS, LANES), 0)
            n_bits = (sc_ref.shape[0] - 1).bit_length()

            def idx_body(i, cut):
                trial = cut | lax.shift_left(jnp.int32(1), n_bits - 1 - i)
                cut_ref[...] = trial
                before = count_where(lambda blk, row0, lanes: (blk == thr_ref[:, lanes])
                                     & (rows + row0 < cut_ref[:, lanes]))
                return jnp.where(before < need, trial, cut)
            cut_ref[...] = lax.fori_loop(0, n_bits, idx_body, jnp.zeros((1, QB), jnp.int32))

            def mb_body(kb, carry):
                for g in range(QB // LANES):
                    lanes = slice(g * LANES, (g + 1) * LANES)
                    blk = sc_ref[pl.ds(pl.multiple_of(kb * KB, KB), KB), lanes]
                    thr = thr_ref[:, lanes]
                    keep = (blk > thr) | ((blk == thr) & (rows[0:KB] + kb * KB <= cut_ref[:, lanes]))
                    mb_ref[kb, :, lanes] = jnp.where(keep, 0.0, NEG).astype(F32)
                return carry
            lax.fori_loop(0, nkb, mb_body, 0)

    m_ref[...] = jnp.full(m_ref.shape, NEG, F32)
    acc_ref[...] = jnp.zeros(acc_ref.shape, F32)

    hh = H_A // 2
    s_refs, p_refs, t_refs = (sa_ref, sb_ref), (pa_ref, pb_ref), (ta_ref, tb_ref)
    pb_ref[...] = jnp.zeros(pb_ref.shape, BF16)
    al_ref[...] = jnp.ones(al_ref.shape, F32)

    def logits(kb, half):
        kv = ckv_ref[pl.ds(pl.multiple_of(jnp.minimum(kb, last_kb) * KB, KB), KB), :]
        s_refs[half][...] = jnp.dot(kv, qall_ref[half], preferred_element_type=F32)

    def softmax(kb, half, with_bias):
        tile = jnp.clip(qi - kb, 0, 1)
        for j in range(hh):
            for g in range(QB // LANES):
                lanes = slice(g * LANES, (g + 1) * LANES)
                cols = slice(j * QB + g * LANES, j * QB + (g + 1) * LANES)
                x = s_refs[half][:, cols] + mb_ref[kb, :, lanes]
                if with_bias:
                    x = x + bias_ref[half * hh + j, tile, :, lanes]
                m_prev = m_ref[half, :, cols]
                m_blk = jnp.max(x.reshape(KB // CNT_ACC, CNT_ACC, LANES), axis=0)
                m_new = jnp.maximum(m_prev, jnp.max(m_blk, axis=0, keepdims=True))
                al_ref[half, :, cols] = jnp.exp2(m_prev - m_new)
                m_ref[half, :, cols] = m_new
                p_refs[half][:, cols] = jnp.exp2(x - m_new).astype(BF16)

    def accumulate(kb, half):
        acc_ref[half] = acc_ref[half] * al_ref[half] + jnp.dot(
            ckvt_ref[jnp.maximum(kb, 0)], p_refs[half][...], preferred_element_type=F32)

    def sweep(first_kb, end_kb, with_bias):
        def body(kb, carry):
            logits(kb, 1)
            softmax(kb, 0, with_bias)
            accumulate(kb - 1, 1)
            logits(kb + 1, 0)
            softmax(kb, 1, with_bias)
            accumulate(kb, 0)
            return carry
        lax.fori_loop(first_kb, end_kb, body, 0)

    n_far = jnp.maximum(qi - 1, 0)
    logits(0, 0)
    sweep(0, n_far, False)
    sweep(n_far, nkb, True)
    accumulate(nkb - 1, 1)

    for h in range(H_A):
        half, cols = h // hh, slice((h % hh) * QB, (h % hh + 1) * QB)
        ya_t = jnp.dot(wuvt_ref[h], acc_ref[half, 0:D_C, cols].astype(BF16), preferred_element_type=F32)
        ya = (ya_t / acc_ref[half, D_C:D_C + 1, cols]).T
        z = za_ref[:, h * DH_A:(h + 1) * DH_A].astype(F32)
        y_ref[:, h * DH_A:(h + 1) * DH_A] = (ya * (z * jax.nn.sigmoid(z))).astype(BF16)


def _dsa(main, gate_t, ckv_n, ckv_t, kidx_n, w_uk_t, w_uv_t, bias, B, T):
    nq = T // QB
    return pl.pallas_call(
        _dsa_kernel,
        grid=(B, nq),
        in_specs=[pl.BlockSpec((QB, W_A), lambda b, q: (b * nq + q, C_QA // W_A)),
                  pl.BlockSpec((QB, W_A), lambda b, q: (b * nq + q, C_ZA // W_A)),
                  pl.BlockSpec((QB, H_IDX * D_IDX), lambda b, q: (b * nq + q, C_QI // (H_IDX * D_IDX))),
                  pl.BlockSpec((LANES, QB), lambda b, q: (0, b * nq + q)),
                  pl.BlockSpec((T, D_C), lambda b, q: (b, 0)),
                  pl.BlockSpec((T // KB, D_C + ONES_ROWS, KB), lambda b, q: (b, 0, 0)),
                  pl.BlockSpec((T, D_IDX), lambda b, q: (b, 0)),
                  pl.BlockSpec((H_A, D_C, DH_A), lambda b, q: (0, 0, 0)),
                  pl.BlockSpec((H_A, DH_A, D_C), lambda b, q: (0, 0, 0)),
                  pl.BlockSpec((H_A, 2, KB, QB), lambda b, q: (0, 0, 0, 0))],
        out_specs=pl.BlockSpec((QB, W_A), lambda b, q: (b * nq + q, 0)),
        out_shape=jax.ShapeDtypeStruct((B * T, W_A), BF16),
        scratch_shapes=[pltpu.VMEM((2, D_C, H_A // 2 * QB), BF16),
                        pltpu.VMEM((H_IDX * QB, D_IDX), BF16),
                        pltpu.VMEM((KI, H_IDX * QB), F32),
                        pltpu.VMEM((KI, H_IDX * QB), F32),
                        pltpu.VMEM((T, QB), F32),
                        pltpu.VMEM((T // KB, KB, QB), F32),
                        pltpu.VMEM((KB, H_A // 2 * QB), F32),
                        pltpu.VMEM((KB, H_A // 2 * QB), F32),
                        pltpu.VMEM((KB, H_A // 2 * QB), BF16),
                        pltpu.VMEM((KB, H_A // 2 * QB), BF16),
                        pltpu.VMEM((D_C + ONES_ROWS, H_A // 2 * QB), F32),
                        pltpu.VMEM((D_C + ONES_ROWS, H_A // 2 * QB), F32),
                        pltpu.VMEM((2, 1, H_A // 2 * QB), F32),
                        pltpu.VMEM((2, 1, H_A // 2 * QB), F32),
                        pltpu.VMEM((2, D_C + ONES_ROWS, H_A // 2 * QB), F32),
                        pltpu.VMEM((H_IDX, QB), F32),
                        pltpu.VMEM((1, QB), F32),
                        pltpu.VMEM((1, QB), jnp.int32),
                        pltpu.VMEM((1, QB), F32)],
        compiler_params=pltpu.CompilerParams(
            dimension_semantics=("arbitrary", "arbitrary"), vmem_limit_bytes=VMEM_LIMIT),
        name="dsa",
    )(main, main, main, gate_t, ckv_n, ckv_t, kidx_n, w_uk_t, w_uv_t, bias)


def _split_dot(tri, x):
    hi = x.astype(BF16)
    lo = (x - hi.astype(F32)).astype(BF16)
    return jnp.dot(tri, hi, preferred_element_type=F32) + jnp.dot(tri, lo, preferred_element_type=F32)


def _log_sigmoid(x):
    return jnp.minimum(x, 0.0) - jnp.log1p(jnp.exp(-jnp.abs(x)))


def _mlstm_out_kernel(q_ref, k_ref, qh_ref, kh_ref, v_ref, o_ref, z_ref, g_ref, gt_ref,
                      cw_ref, cb_ref, gbr_ref, gbc_ref, ng_ref, ya_ref, x_ref, w_ref, lg_ref, lb_ref,
                      out_ref, ct_ref, m_ref, y_ref, yp_ref, p_ref, *, n_chunks, n_steps):
    step = pl.program_id(0)
    c = jnp.minimum(step, n_steps - 1) % n_chunks
    L = L_M

    @pl.when(step == 0)
    def _():
        yp_ref[...] = jnp.zeros(yp_ref.shape, BF16)

    @pl.when(c == 0)
    def _():
        ct_ref[...] = jnp.zeros(ct_ref.shape, F32)
        m_ref[...] = jnp.zeros(m_ref.shape, F32)

    n_pc = 2 * H_M
    pw = D_MODEL // n_pc

    def project_chunk(j):
        cols = slice(j * pw, (j + 1) * pw)
        p_ref[:, cols] = (jnp.dot(ya_ref[...], w_ref[0:W_A, cols], preferred_element_type=F32)
                          + jnp.dot(yp_ref[...], w_ref[W_A:W_A + W_M, cols], preferred_element_type=F32))

    def norm_previous():
        res = ALPHA * x_ref[...] + p_ref[...]
        mean = jnp.mean(res, axis=-1, keepdims=True)
        var_r = jnp.mean(jnp.square(res - mean), axis=-1, keepdims=True)
        out_ref[...] = (res - mean) * lax.rsqrt(var_r + LN_EPS) * lg_ref[...] + lb_ref[...]

    pending = [functools.partial(project_chunk, j) for j in range(n_pc)] + [norm_previous]

    def emit_projection_work():
        if pending:
            pending.pop(0)()

    emit_projection_work()

    r = lax.broadcasted_iota(jnp.int32, (L, L), 0)
    s = lax.broadcasted_iota(jnp.int32, (L, L), 1)
    causal = s <= r
    shifts = [jnp.where(r - s == d, 1.0, 0.0).astype(BF16) for d in range(1, CONV_W)]

    def conv_silu(x_ref, halo_ref, lo):
        x = x_ref[...]
        halo = jnp.where(c > 0, halo_ref[...].astype(F32), 0.0)
        w = cw_ref[:, lo:lo + H_M * DK_M]
        y = cb_ref[:, lo:lo + H_M * DK_M] + w[CONV_W - 1:CONV_W] * x.astype(F32)
        top = jnp.zeros((SUBLANES, H_M * DK_M), F32)
        for d in range(1, CONV_W):
            wd = w[CONV_W - 1 - d:CONV_W - d]
            y = y + wd * jnp.dot(shifts[d - 1], x, preferred_element_type=F32)
            top = top + wd * jnp.concatenate(
                [halo[HALO - d:HALO], jnp.zeros((SUBLANES - d, H_M * DK_M), F32)], axis=0)
        y = jnp.concatenate([y[0:SUBLANES] + top, y[SUBLANES:]], axis=0)
        return y * jax.nn.sigmoid(y)

    q_all = conv_silu(q_ref, qh_ref, 0)
    emit_projection_work()
    k_all = conv_silu(k_ref, kh_ref, H_M * DK_M) * (DK_M ** -0.5)
    emit_projection_work()

    gc = g_ref[...] + gbr_ref[...]
    gr = gt_ref[S_IM:S_IM + 2 * H_M, :] + gbc_ref[S_IM:S_IM + 2 * H_M, :]
    tri_l = jnp.where(causal, 1.0, 0.0).astype(BF16)
    tri_u = jnp.where(r <= s, 1.0, 0.0).astype(BF16)
    b_cols = _split_dot(tri_l, _log_sigmoid(gc) * LOG2E)
    lf_rows = _log_sigmoid(gr) * LOG2E
    b_rows = jnp.dot(lf_rows.astype(BF16), tri_u, preferred_element_type=F32) \
        + jnp.dot((lf_rows - lf_rows.astype(BF16).astype(F32)).astype(BF16), tri_u,
                  preferred_element_type=F32)
    gc = gc * LOG2E
    gr = gr * LOG2E
    ones = jnp.ones((L, LANES), BF16)
    emit_projection_work()

    for h in range(H_M):
        emit_projection_work()
        q = q_all[:, h * DK_M:(h + 1) * DK_M]
        k = k_all[:, h * DK_M:(h + 1) * DK_M]
        v = jnp.concatenate([v_ref[:, h * DV_M:(h + 1) * DV_M], ones], axis=1)
        qb = q.astype(BF16)
        b_c = b_cols[:, S_FM + h:S_FM + h + 1]
        i_c = gc[:, S_IM + h:S_IM + h + 1]
        b_r = b_rows[H_M + h:H_M + h + 1, :]
        i_r = gr[h:h + 1, :]
        m_prev = m_ref[h]
        ct = ct_ref[h]

        log_d = jnp.where(causal, b_c - b_r + i_r, -jnp.inf)
        g = b_c + m_prev
        m_t = jnp.maximum(jnp.max(log_d, axis=-1, keepdims=True), g)
        qk = lax.dot_general(qb, k.astype(BF16), _NT, preferred_element_type=F32)
        s_mat = qk * jnp.exp2(log_d - m_t)
        inter = jnp.exp2(g - m_t)
        num = jnp.dot(s_mat.astype(BF16), v, preferred_element_type=F32) \
            + inter * jnp.dot(qb, ct.astype(BF16), preferred_element_type=F32)
        den = jnp.maximum(jnp.abs(num[:, DV_M:]), jnp.exp2(-m_t))
        hh = num[:, 0:DV_M] / jnp.concatenate([den] * (DV_M // LANES), axis=1)

        emit_projection_work()
        b_last = b_c[L - 1:L, :]
        a_r = b_last - b_r + i_r
        m_new = jnp.maximum(b_last + m_prev, jnp.max(a_r, axis=-1, keepdims=True))
        decay = jnp.exp2(b_last + m_prev - m_new)
        wgt_c = jnp.exp2(b_last - b_c + i_c - m_new)
        kw = k * wgt_c
        ct_ref[h] = decay * ct + jnp.dot(kw.T.astype(BF16), v, preferred_element_type=F32)
        m_ref[h] = m_new

        mu = jnp.mean(hh, axis=-1, keepdims=True)
        var = jnp.mean(jnp.square(hh - mu), axis=-1, keepdims=True)
        hn = (hh - mu) * lax.rsqrt(var + LN_EPS) * ng_ref[:, h * DV_M:(h + 1) * DV_M]
        og = o_ref[:, h * DV_M:(h + 1) * DV_M].astype(F32)
        zg = z_ref[:, h * DV_M:(h + 1) * DV_M].astype(F32)
        y_ref[:, h * DV_M:(h + 1) * DV_M] = (hn * jax.nn.sigmoid(og) * (zg * jax.nn.sigmoid(zg))).astype(BF16)

    assert not pending
    yp_ref[...] = y_ref[...]


def _mlstm_out(main, small, gate_t, conv_w, conv_b, gb_row, gb_col, norm_g, ya, x2d, w_out, ln_g, ln_b, B, T):
    nc = T // L_M
    hb = L_M // HALO
    qk_w = H_M * DK_M
    n_steps = B * nc

    def cur(col):
        return lambda s: (jnp.minimum(s, n_steps - 1), col)

    def prev(s):
        return (jnp.maximum(s - 1, 0), 0)

    def halo_map(col):
        return lambda s: (jnp.maximum(jnp.minimum(s, n_steps - 1) * hb - 1, 0), col)

    const = lambda s: (0, 0)
    return pl.pallas_call(
        functools.partial(_mlstm_out_kernel, n_chunks=nc, n_steps=n_steps),
        grid=(n_steps + 1,),
        in_specs=[pl.BlockSpec((L_M, qk_w), cur(C_QM // qk_w)),
                  pl.BlockSpec((L_M, qk_w), cur(C_KM // qk_w)),
                  pl.BlockSpec((HALO, qk_w), halo_map(C_QM // qk_w)),
                  pl.BlockSpec((HALO, qk_w), halo_map(C_KM // qk_w)),
                  pl.BlockSpec((L_M, W_M), cur(C_VM // W_M)),
                  pl.BlockSpec((L_M, W_M), cur(C_OM // W_M)),
                  pl.BlockSpec((L_M, W_M), cur(C_ZM // W_M)),
                  pl.BlockSpec((L_M, LANES), cur(D_C // LANES)),
                  pl.BlockSpec((LANES, L_M), lambda s: (0, jnp.minimum(s, n_steps - 1))),
                  pl.BlockSpec((CONV_W, 2 * qk_w), const),
                  pl.BlockSpec((1, 2 * qk_w), const),
                  pl.BlockSpec((1, LANES), const),
                  pl.BlockSpec((LANES, 1), const),
                  pl.BlockSpec((1, W_M), const),
                  pl.BlockSpec((L_M, W_A), prev),
                  pl.BlockSpec((L_M, D_MODEL), prev),
                  pl.BlockSpec((W_A + W_M, D_MODEL), const),
                  pl.BlockSpec((1, D_MODEL), const),
                  pl.BlockSpec((1, D_MODEL), const)],
        out_specs=pl.BlockSpec((L_M, D_MODEL), prev),
        out_shape=jax.ShapeDtypeStruct((B * T, D_MODEL), F32),
        scratch_shapes=[pltpu.VMEM((H_M, DK_M, DV_M + LANES), F32),
                        pltpu.VMEM((H_M, 1, 1), F32),
                        pltpu.VMEM((L_M, W_M), BF16),
                        pltpu.VMEM((L_M, W_M), BF16),
                        pltpu.VMEM((L_M, D_MODEL), F32)],
        compiler_params=pltpu.CompilerParams(
            dimension_semantics=("arbitrary",), vmem_limit_bytes=VMEM_LIMIT),
        name="mlstm_out",
    )(main, main, main, main, main, main, main, small, gate_t,
      conv_w, conv_b, gb_row, gb_col, norm_g, ya, x2d, w_out, ln_g, ln_b)


_W_IN_SEGS = (("q_a", W_A), ("c_kv", D_C), ("z_a", W_A), ("q_i", H_IDX * D_IDX), ("k_i", D_IDX),
              ("w_i", H_IDX), ("q_m", H_M * DK_M), ("k_m", H_M * DK_M), ("v_m", W_M), ("i_m", H_M),
              ("f_m", H_M), ("o_m", W_M), ("z_m", W_M))
_MAIN_ORDER = ("q_a", "z_a", "q_i", "q_m", "k_m", "v_m", "o_m", "z_m")
_SEG_NAMES = [name for name, _ in _W_IN_SEGS]
assert _SEG_NAMES.index("f_m") == _SEG_NAMES.index("i_m") + 1 and S_FM == S_IM + H_M


def _repack_kernel(wt_ref, main_ref, small_ref):
    src, off = {}, 0
    for name, width in _W_IN_SEGS:
        src[name] = (off, width)
        off += width
    dst = 0
    for name in _MAIN_ORDER:
        lo, width = src[name]
        main_ref[dst:dst + width, :] = wt_ref[lo:lo + width, :].astype(BF16)
        dst += width
    parts = [wt_ref[src[name][0]:src[name][0] + src[name][1], :] for name in ("c_kv", "k_i", "w_i")]
    lo = src["i_m"][0]
    parts.append(wt_ref[lo:lo + 2 * H_M, :])
    used = sum(p.shape[0] for p in parts)
    parts.append(jnp.zeros((N_SMALL - used, wt_ref.shape[1]), F32))
    small_ref[...] = jnp.concatenate(parts, axis=0).astype(BF16)


def _repack_w_in(w_in, tc=256):
    n_cols = sum(width for _, width in _W_IN_SEGS)
    wt = jnp.swapaxes(w_in, 1, 2)[0]
    return pl.pallas_call(
        _repack_kernel,
        grid=(D_MODEL // tc,),
        in_specs=[pl.BlockSpec((n_cols, tc), lambda i: (0, i))],
        out_specs=[pl.BlockSpec((N_MAIN, tc), lambda i: (0, i)),
                   pl.BlockSpec((N_SMALL, tc), lambda i: (0, i))],
        out_shape=[jax.ShapeDtypeStruct((N_MAIN, D_MODEL), BF16),
                   jax.ShapeDtypeStruct((N_SMALL, D_MODEL), BF16)],
        compiler_params=pltpu.CompilerParams(
            dimension_semantics=("arbitrary",), vmem_limit_bytes=VMEM_LIMIT),
        name="repack",
    )(wt)


def kernel(x, w_in, b_igate, b_fgate, kv_norm_g, w_uk, w_uv, idx_k_ln_g, idx_k_ln_b, rel_bias,
           conv_w, conv_b, mh_norm_g, w_out, ln_g, ln_b):
    B, T, D = x.shape
    assert D == D_MODEL and T % L_M == 0 and T % (2 * KB) == 0 and w_in.shape[0] == 1
    bias = _bias_tiles(rel_bias)
    x2d = x.reshape(B * T, D)
    w_main, w_small = _repack_w_in(w_in)
    main, small = _proj(x2d, w_main, w_small)
    ckv_n, ckv_t, kidx_n, gate_t = _prep(small, kv_norm_g[0][None], idx_k_ln_g[0][None], idx_k_ln_b[0][None])
    w_uk_t = jnp.transpose(w_uk[0], (0, 2, 1)).astype(BF16)
    w_uv_t = jnp.transpose(w_uv[0], (0, 2, 1)).astype(BF16)
    ya = _dsa(main, gate_t, ckv_n, ckv_t, kidx_n, w_uk_t, w_uv_t, bias, B, T)
    gb = jnp.zeros((LANES,), F32).at[S_IM:S_IM + H_M].set(b_igate[0]).at[S_FM:S_FM + H_M].set(b_fgate[0])
    out = _mlstm_out(main, small, gate_t, conv_w[0], conv_b[0][None], gb[None, :], gb[:, None],
                     mh_norm_g[0][None], ya, x2d, w_out[0].astype(BF16), ln_g[0][None], ln_b[0][None], B, T)
    return out.reshape(B, T, D)
```

```python
import functools
import math

import numpy as np
import jax
import jax.numpy as jnp
from jax import lax
from jax.experimental import pallas as pl
from jax.experimental.pallas import tpu as pltpu

F32 = jnp.float32
BF16 = jnp.bfloat16

D_MODEL = 2048
W_A = 1024
DH_A = 128
H_A = 8
D_C = 256
H_IDX = 16
D_IDX = 64
TOPK = 256
W_M = 1024
H_M = 4
DV_M = 256
DK_M = 128
CONV_W = 4
N_BUCKETS = 32
MAX_DIST = 128
ALPHA = 2.0 ** 0.25
LN_EPS = 1e-5

LANES = 128
SUBLANES = 8
VMEM_LIMIT = 56 * 1024 * 1024

QB = 256
KB = 256
KI = 128
CNT_ROWS = 256
CNT_ACC = 4 * SUBLANES
L_M = 256
HALO = 16
NEG = -1e30
LOG2E = math.log2(math.e)
ONES_ROWS = 16

C_QA, C_ZA, C_QI, C_QM, C_KM, C_VM, C_OM, C_ZM = 0, 1024, 2048, 3072, 3584, 4096, 5120, 6144
N_MAIN = 7168
N_SMALL = 384
S_KI, S_WI, S_IM, S_FM = 0, 64, 80, 84


def _t5_bucket_np(rel):
    max_exact = N_BUCKETS // 2
    n = np.maximum(rel, 0)
    nf = np.maximum(n, 1).astype(np.float32)
    large = max_exact + (np.log(nf / np.float32(max_exact)) / np.float32(math.log(MAX_DIST / max_exact))
                         * np.float32(N_BUCKETS - max_exact)).astype(np.int32)
    large = np.minimum(large, N_BUCKETS - 1)
    return np.where(n < max_exact, n, large).astype(np.int32)


FAR_BUCKET = int(_t5_bucket_np(np.array(2 * KB + 1)))


def _bucket_tiles():
    i = np.arange(QB)[None, :]
    j = np.arange(KB)[:, None]
    t0 = _t5_bucket_np(i - j)
    t1 = _t5_bucket_np(i - j + KB)
    assert (t5 := _t5_bucket_np(np.arange(KB + 1, 4096))).min() == t5.max() == FAR_BUCKET
    return np.stack([t0, t1]).astype(np.int32)


def _bias_kernel(bucket_ref, rb_ref, out_ref):
    h = pl.program_id(0)
    far = rb_ref[FAR_BUCKET, h]
    for k in range(2):
        bk = bucket_ref[k]
        acc = jnp.zeros((KB, QB), F32)
        for b in range(N_BUCKETS):
            acc = jnp.where(bk == b, rb_ref[b, h] - far, acc)
        out_ref[0, k] = acc * LOG2E


def _bias_tiles(rel_bias):
    bucket = jnp.asarray(_bucket_tiles())
    return pl.pallas_call(
        _bias_kernel,
        grid=(H_A,),
        in_specs=[pl.BlockSpec((2, KB, QB), lambda h: (0, 0, 0)),
                  pl.BlockSpec(memory_space=pltpu.SMEM)],
        out_specs=pl.BlockSpec((1, 2, KB, QB), lambda h: (h, 0, 0, 0)),
        out_shape=jax.ShapeDtypeStruct((H_A, 2, KB, QB), F32),
        name="bias_tiles",
    )(bucket, rel_bias)


_NT = (((1,), (1,)), ((), ()))


def _proj_kernel(x_ref, w_ref, ws_ref, o_ref, os_ref, xb_ref):
    @pl.when(pl.program_id(1) == 0)
    def _():
        xb_ref[...] = x_ref[...].astype(BF16)
        os_ref[...] = lax.dot_general(xb_ref[...], ws_ref[...], _NT, preferred_element_type=F32)

    o_ref[...] = lax.dot_general(xb_ref[...], w_ref[...], _NT, preferred_element_type=F32).astype(BF16)


def _proj(x2d, w_main, w_small, tm=1024, tn=1792):
    M = x2d.shape[0]
    return pl.pallas_call(
        _proj_kernel,
        grid=(M // tm, N_MAIN // tn),
        in_specs=[pl.BlockSpec((tm, D_MODEL), lambda i, j: (i, 0)),
                  pl.BlockSpec((tn, D_MODEL), lambda i, j: (j, 0)),
                  pl.BlockSpec((N_SMALL, D_MODEL), lambda i, j: (0, 0))],
        out_specs=[pl.BlockSpec((tm, tn), lambda i, j: (i, j)),
                   pl.BlockSpec((tm, N_SMALL), lambda i, j: (i, 0))],
        out_shape=[jax.ShapeDtypeStruct((M, N_MAIN), BF16),
                   jax.ShapeDtypeStruct((M, N_SMALL), F32)],
        scratch_shapes=[pltpu.VMEM((tm, D_MODEL), BF16)],
        compiler_params=pltpu.CompilerParams(
            dimension_semantics=("arbitrary", "arbitrary"), vmem_limit_bytes=VMEM_LIMIT),
        name="proj",
    )(x2d, w_main, w_small)


def _prep_kernel(s_ref, kvg_ref, ig_ref, ib_ref, ckv_ref, ckvt_ref, kidx_ref, gt_ref):
    c = s_ref[:, 0:D_C]
    c = c * lax.rsqrt(jnp.mean(c * c, axis=-1, keepdims=True) + LN_EPS) * kvg_ref[...]
    ckv_ref[...] = c.astype(BF16)
    for r in range(ckvt_ref.shape[0]):
        ckvt_ref[r, 0:D_C, :] = c[r * KB:(r + 1) * KB, :].T.astype(BF16)
        ckvt_ref[r, D_C:D_C + ONES_ROWS, :] = jnp.ones((ONES_ROWS, KB), BF16)
    tile = s_ref[:, D_C:D_C + LANES]
    k = tile[:, S_KI:S_KI + D_IDX]
    mu = jnp.mean(k, axis=-1, keepdims=True)
    var = jnp.mean(jnp.square(k - mu), axis=-1, keepdims=True)
    kidx_ref[...] = ((k - mu) * lax.rsqrt(var + LN_EPS) * ig_ref[...] + ib_ref[...]).astype(BF16)
    gt_ref[...] = tile.T


def _prep(small, kv_g, idx_g, idx_b, tm=2048):
    M = small.shape[0]
    return pl.pallas_call(
        _prep_kernel,
        grid=(M // tm,),
        in_specs=[pl.BlockSpec((tm, N_SMALL), lambda i: (i, 0)),
                  pl.BlockSpec((1, D_C), lambda i: (0, 0)),
                  pl.BlockSpec((1, D_IDX), lambda i: (0, 0)),
                  pl.BlockSpec((1, D_IDX), lambda i: (0, 0))],
        out_specs=[pl.BlockSpec((tm, D_C), lambda i: (i, 0)),
                   pl.BlockSpec((tm // KB, D_C + ONES_ROWS, KB), lambda i: (i, 0, 0)),
                   pl.BlockSpec((tm, D_IDX), lambda i: (i, 0)),
                   pl.BlockSpec((LANES, tm), lambda i: (0, i))],
        out_shape=[jax.ShapeDtypeStruct((M, D_C), BF16),
                   jax.ShapeDtypeStruct((M // KB, D_C + ONES_ROWS, KB), BF16),
                   jax.ShapeDtypeStruct((M, D_IDX), BF16),
                   jax.ShapeDtypeStruct((LANES, M), F32)],
        name="prep",
    )(small, kv_g, idx_g, idx_b)


def _key_to_float(key):
    bits = jnp.where(key < 0, key ^ jnp.int32(0x7FFFFFFF), key)
    return lax.bitcast_convert_type(bits, F32)


def _dsa_kernel(qa_ref, za_ref, qi_ref, gt_ref, ckv_ref, ckvt_ref, kidx_ref, wukt_ref, wuvt_ref, bias_ref,
                y_ref, qall_ref, qr_ref, ha_ref, hb_ref, sc_ref, mb_ref, sa_ref, sb_ref, pa_ref, pb_ref,
                ta_ref, tb_ref, m_ref, al_ref, acc_ref, w_ref, thr_ref, cut_ref, cge_ref):
    qi = pl.program_id(1)
    nkb = qi + 1
    last_kb = mb_ref.shape[0] - 1

    for h in range(H_A):
        ql = lax.dot_general(wukt_ref[h], qa_ref[:, h * DH_A:(h + 1) * DH_A], (((1,), (1,)), ((), ())),
                             preferred_element_type=F32)
        qall_ref[h // (H_A // 2), :, (h % (H_A // 2)) * QB:(h % (H_A // 2) + 1) * QB] = (
            ql * (DH_A ** -0.5 * LOG2E)).astype(BF16)


    @pl.when(qi * QB < TOPK)
    def _():
        krow = lax.broadcasted_iota(jnp.int32, (KB, QB), 0)
        qcol = lax.broadcasted_iota(jnp.int32, (KB, QB), 1)
        mb_ref[0] = jnp.where(krow <= qcol, 0.0, NEG).astype(F32)

    @pl.when(qi * QB >= TOPK)
    def _():
        for h in range(H_IDX):
            qr_ref[h * QB:(h + 1) * QB, :] = qi_ref[:, h * D_IDX:(h + 1) * D_IDX]
        w_ref[...] = gt_ref[S_WI:S_WI + H_IDX, :] * ((D_IDX ** -0.5) * (H_IDX ** -0.5))
        krow = lax.broadcasted_iota(jnp.int32, (KI, LANES), 0)
        qcol = lax.broadcasted_iota(jnp.int32, (KI, LANES), 1)
        n_ki = nkb * (KB // KI)
        last_ki = sc_ref.shape[0] // KI - 1

        def head_dots(ki, dst_ref):
            k = kidx_ref[pl.ds(pl.multiple_of(jnp.minimum(ki, last_ki) * KI, KI), KI), :]
            dst_ref[...] = lax.dot_general(k, qr_ref[...], (((1,), (1,)), ((), ())),
                                           preferred_element_type=F32)

        def reduce_heads(src_ref, ki):
            for g in range(QB // LANES):
                lanes = slice(g * LANES, (g + 1) * LANES)
                acc = jnp.zeros((KI, LANES), F32)
                for h in range(H_IDX):
                    acc = acc + (jnp.maximum(src_ref[:, h * QB + g * LANES:h * QB + (g + 1) * LANES], 0.0)
                                 * w_ref[h:h + 1, lanes])
                sc_ref[pl.ds(pl.multiple_of(ki * KI, KI), KI), lanes] = jnp.where(
                    krow + (ki * KI - qi * QB - g * LANES) <= qcol, acc, -jnp.inf)

        assert CNT_ROWS == KB

        head_dots(0, ha_ref)

        def sc_body(j, carry):
            head_dots(2 * j + 1, hb_ref)
            reduce_heads(ha_ref, 2 * j)
            head_dots(2 * j + 2, ha_ref)
            reduce_heads(hb_ref, 2 * j + 1)
            return carry
        lax.fori_loop(0, n_ki // 2, sc_body, 0)

        n_cnt = (nkb * KB + CNT_ROWS - 1) // CNT_ROWS

        def count_where(pred, steps=None):
            def body(c, acc):
                parts = []
                for g in range(QB // LANES):
                    lanes = slice(g * LANES, (g + 1) * LANES)
                    rows = CNT_ROWS
                    if steps is None:
                        blk = sc_ref[pl.ds(pl.multiple_of(c * CNT_ROWS, CNT_ROWS), CNT_ROWS), lanes]
                    else:
                        if c == steps - 1:
                            rows = min(CNT_ROWS, (g + 1) * LANES)
                        blk = sc_ref[c * CNT_ROWS:c * CNT_ROWS + rows, lanes]
                    hit = jnp.where(pred(blk, c * CNT_ROWS, lanes), 1.0, 0.0).astype(F32)
                    parts.append(jnp.sum(hit.reshape(rows // CNT_ACC, CNT_ACC, LANES), axis=0))
                return acc + jnp.concatenate(parts, axis=1)
            acc = jnp.zeros((CNT_ACC, QB), F32)
            if steps is None:
                acc = lax.fori_loop(0, n_cnt, body, acc)
            else:
                for c in range(steps):
                    acc = body(c, acc)
            return jnp.sum(acc, axis=0, keepdims=True)

        def search(steps):
            n_g = QB // LANES

            def count_ge(g, cand):
                lanes = slice(g * LANES, (g + 1) * LANES)
                acc = jnp.zeros((CNT_ACC, LANES), F32)
                for c in range(steps):
                    rows = CNT_ROWS if c < steps - 1 else min(CNT_ROWS, (g + 1) * LANES)
                    hit = jnp.where(sc_ref[c * CNT_ROWS:c * CNT_ROWS + rows, lanes] >= cand, 1.0, 0.0)
                    acc = acc + jnp.sum(hit.astype(F32).reshape(rows // CNT_ACC, CNT_ACC, LANES), axis=0)
                return jnp.sum(acc, axis=0, keepdims=True)

            def bit_body(i, carry):
                new = []
                for g in range(n_g):
                    u, c_ge = carry[g]
                    trial = u | lax.shift_left(jnp.int32(1), 31 - i)
                    cnt = count_ge(g, _key_to_float(trial ^ jnp.int32(-2 ** 31)))
                    ok = cnt >= float(TOPK)
                    new.append((jnp.where(ok, trial, u), jnp.where(ok, cnt, c_ge)))
                return tuple(new)
            init = tuple((jnp.zeros((1, LANES), jnp.int32),
                          jnp.full((1, LANES), float(sc_ref.shape[0]), F32)) for _ in range(n_g))
            for g, (u, c_ge) in enumerate(lax.fori_loop(0, 32, bit_body, init)):
                lanes = slice(g * LANES, (g + 1) * LANES)
                thr_ref[:, lanes] = _key_to_float(u ^ jnp.int32(-2 ** 31))
                cge_ref[:, lanes] = c_ge

        for steps in range(1, sc_ref.shape[0] // CNT_ROWS + 1):
            pl.when(n_cnt == steps)(functools.partial(search, steps))
        has_ties = jnp.max(cge_ref[...]) > float(TOPK)

        @pl.when(jnp.logical_not(has_ties))
        def _():
            def mb_body(kb, carry):
                blk = sc_ref[pl.ds(pl.multiple_of(kb * KB, KB), KB), :]
                mb_ref[kb] = jnp.where(blk >= thr_ref[...], 0.0, NEG).astype(F32)
                return carry
            lax.fori_loop(0, nkb, mb_body, 0)

        @pl.when(has_ties)
        def _():
            c_gt = count_where(lambda blk, row0, lanes: blk > thr_ref[:, lanes])
            need = float(TOPK) - c_gt
            rows = lax.broadcasted_iota(jnp.int32, (CNT_ROWS, LANES), 0)
            n_bits = (sc_ref.shape[0] - 1).bit_length()

            def idx_body(i, cut):
                trial = cut | lax.shift_left(jnp.int32(1), n_bits - 1 - i)
                cut_ref[...] = trial
                before = count_where(lambda blk, row0, lanes: (blk == thr_ref[:, lanes])
                                     & (rows + row0 < cut_ref[:, lanes]))
                return jnp.where(before < need, trial, cut)
            cut_ref[...] = lax.fori_loop(0, n_bits, idx_body, jnp.zeros((1, QB), jnp.int32))

            def mb_body(kb, carry):
                for g in range(QB // LANES):
                    lanes = slice(g * LANES, (g + 1) * LANES)
                    blk = sc_ref[pl.ds(pl.multiple_of(kb * KB, KB), KB), lanes]
                    thr = thr_ref[:, lanes]
                    keep = (blk > thr) | ((blk == thr) & (rows[0:KB] + kb * KB <= cut_ref[:, lanes]))
                    mb_ref[kb, :, lanes] = jnp.where(keep, 0.0, NEG).astype(F32)
                return carry
            lax.fori_loop(0, nkb, mb_body, 0)

    m_ref[...] = jnp.full(m_ref.shape, NEG, F32)
    acc_ref[...] = jnp.zeros(acc_ref.shape, F32)

    hh = H_A // 2
    s_refs, p_refs, t_refs = (sa_ref, sb_ref), (pa_ref, pb_ref), (ta_ref, tb_ref)
    pb_ref[...] = jnp.zeros(pb_ref.shape, BF16)
    al_ref[...] = jnp.ones(al_ref.shape, F32)

    def logits(kb, half):
        kv = ckv_ref[pl.ds(pl.multiple_of(jnp.minimum(kb, last_kb) * KB, KB), KB), :]
        res = jnp.dot(kv, qall_ref[half], preferred_element_type=F32)
        for u in range(res.shape[1] // LANES):
            s_refs[half][u] = res[:, u * LANES:(u + 1) * LANES]

    def softmax(kb, half, with_bias):
        tile = jnp.clip(qi - kb, 0, 1)
        for j in range(hh):
            for g in range(QB // LANES):
                lanes = slice(g * LANES, (g + 1) * LANES)
                cols = slice(j * QB + g * LANES, j * QB + (g + 1) * LANES)
                x = s_refs[half][j * (QB // LANES) + g] + mb_ref[kb, :, lanes]
                if with_bias:
                    x = x + bias_ref[half * hh + j, tile, :, lanes]
                m_prev = m_ref[half, :, cols]
                m_blk = jnp.max(x.reshape(KB // CNT_ACC, CNT_ACC, LANES), axis=0)
                m_new = jnp.maximum(m_prev, jnp.max(m_blk, axis=0, keepdims=True))
                al_ref[half, :, cols] = jnp.exp2(m_prev - m_new)
                m_ref[half, :, cols] = m_new
                p_refs[half][:, cols] = jnp.exp2(x - m_new).astype(BF16)

    def accumulate(kb, half):
        acc_ref[half] = acc_ref[half] * al_ref[half] + jnp.dot(
            ckvt_ref[jnp.maximum(kb, 0)], p_refs[half][...], preferred_element_type=F32)

    def sweep(first_kb, end_kb, with_bias):
        def body(kb, carry):
            logits(kb, 1)
            softmax(kb, 0, with_bias)
            accumulate(kb - 1, 1)
            logits(kb + 1, 0)
            softmax(kb, 1, with_bias)
            accumulate(kb, 0)
            return carry
        lax.fori_loop(first_kb, end_kb, body, 0)

    n_far = jnp.maximum(qi - 1, 0)
    logits(0, 0)
    sweep(0, n_far, False)
    sweep(n_far, nkb, True)
    accumulate(nkb - 1, 1)

    for h in range(H_A):
        half, cols = h // hh, slice((h % hh) * QB, (h % hh + 1) * QB)
        ya_t = jnp.dot(wuvt_ref[h], acc_ref[half, 0:D_C, cols].astype(BF16), preferred_element_type=F32)
        ya = (ya_t / acc_ref[half, D_C:D_C + 1, cols]).T
        z = za_ref[:, h * DH_A:(h + 1) * DH_A].astype(F32)
        y_ref[:, h * DH_A:(h + 1) * DH_A] = (ya * (z * jax.nn.sigmoid(z))).astype(BF16)


def _dsa(main, gate_t, ckv_n, ckv_t, kidx_n, w_uk_t, w_uv_t, bias, B, T):
    nq = T // QB
    return pl.pallas_call(
        _dsa_kernel,
        grid=(B, nq),
        in_specs=[pl.BlockSpec((QB, W_A), lambda b, q: (b * nq + q, C_QA // W_A)),
                  pl.BlockSpec((QB, W_A), lambda b, q: (b * nq + q, C_ZA // W_A)),
                  pl.BlockSpec((QB, H_IDX * D_IDX), lambda b, q: (b * nq + q, C_QI // (H_IDX * D_IDX))),
                  pl.BlockSpec((LANES, QB), lambda b, q: (0, b * nq + q)),
                  pl.BlockSpec((T, D_C), lambda b, q: (b, 0)),
                  pl.BlockSpec((T // KB, D_C + ONES_ROWS, KB), lambda b, q: (b, 0, 0)),
                  pl.BlockSpec((T, D_IDX), lambda b, q: (b, 0)),
                  pl.BlockSpec((H_A, D_C, DH_A), lambda b, q: (0, 0, 0)),
                  pl.BlockSpec((H_A, DH_A, D_C), lambda b, q: (0, 0, 0)),
                  pl.BlockSpec((H_A, 2, KB, QB), lambda b, q: (0, 0, 0, 0))],
        out_specs=pl.BlockSpec((QB, W_A), lambda b, q: (b * nq + q, 0)),
        out_shape=jax.ShapeDtypeStruct((B * T, W_A), BF16),
        scratch_shapes=[pltpu.VMEM((2, D_C, H_A // 2 * QB), BF16),
                        pltpu.VMEM((H_IDX * QB, D_IDX), BF16),
                        pltpu.VMEM((KI, H_IDX * QB), F32),
                        pltpu.VMEM((KI, H_IDX * QB), F32),
                        pltpu.VMEM((T, QB), F32),
                        pltpu.VMEM((T // KB, KB, QB), F32),
                        pltpu.VMEM((H_A // 2 * QB // LANES, KB, LANES), F32),
                        pltpu.VMEM((H_A // 2 * QB // LANES, KB, LANES), F32),
                        pltpu.VMEM((KB, H_A // 2 * QB), BF16),
                        pltpu.VMEM((KB, H_A // 2 * QB), BF16),
                        pltpu.VMEM((D_C + ONES_ROWS, H_A // 2 * QB), F32),
                        pltpu.VMEM((D_C + ONES_ROWS, H_A // 2 * QB), F32),
                        pltpu.VMEM((2, 1, H_A // 2 * QB), F32),
                        pltpu.VMEM((2, 1, H_A // 2 * QB), F32),
                        pltpu.VMEM((2, D_C + ONES_ROWS, H_A // 2 * QB), F32),
                        pltpu.VMEM((H_IDX, QB), F32),
                        pltpu.VMEM((1, QB), F32),
                        pltpu.VMEM((1, QB), jnp.int32),
                        pltpu.VMEM((1, QB), F32)],
        compiler_params=pltpu.CompilerParams(
            dimension_semantics=("arbitrary", "arbitrary"), vmem_limit_bytes=VMEM_LIMIT),
        name="dsa",
    )(main, main, main, gate_t, ckv_n, ckv_t, kidx_n, w_uk_t, w_uv_t, bias)


def _split_dot(tri, x):
    hi = x.astype(BF16)
    lo = (x - hi.astype(F32)).astype(BF16)
    return jnp.dot(tri, hi, preferred_element_type=F32) + jnp.dot(tri, lo, preferred_element_type=F32)


def _log_sigmoid(x):
    return jnp.minimum(x, 0.0) - jnp.log1p(jnp.exp(-jnp.abs(x)))


def _mlstm_out_kernel(q_ref, k_ref, qh_ref, kh_ref, v_ref, o_ref, z_ref, g_ref, gt_ref,
                      cw_ref, cb_ref, gbr_ref, gbc_ref, ng_ref, ya_ref, x_ref, w_ref, lg_ref, lb_ref,
                      out_ref, ct_ref, m_ref, y_ref, yp_ref, p_ref, *, n_chunks, n_steps):
    step = pl.program_id(0)
    c = jnp.minimum(step, n_steps - 1) % n_chunks
    L = L_M

    @pl.when(step == 0)
    def _():
        yp_ref[...] = jnp.zeros(yp_ref.shape, BF16)

    @pl.when(c == 0)
    def _():
        ct_ref[...] = jnp.zeros(ct_ref.shape, F32)
        m_ref[...] = jnp.zeros(m_ref.shape, F32)

    n_pc = 2 * H_M
    pw = D_MODEL // n_pc

    def project_chunk(j):
        cols = slice(j * pw, (j + 1) * pw)
        p_ref[:, cols] = (jnp.dot(ya_ref[...], w_ref[0:W_A, cols], preferred_element_type=F32)
                          + jnp.dot(yp_ref[...], w_ref[W_A:W_A + W_M, cols], preferred_element_type=F32))

    def norm_previous():
        res = ALPHA * x_ref[...] + p_ref[...]
        mean = jnp.mean(res, axis=-1, keepdims=True)
        var_r = jnp.mean(jnp.square(res - mean), axis=-1, keepdims=True)
        out_ref[...] = (res - mean) * lax.rsqrt(var_r + LN_EPS) * lg_ref[...] + lb_ref[...]

    pending = [functools.partial(project_chunk, j) for j in range(n_pc)] + [norm_previous]

    def emit_projection_work():
        if pending:
            pending.pop(0)()

    emit_projection_work()

    r = lax.broadcasted_iota(jnp.int32, (L, L), 0)
    s = lax.broadcasted_iota(jnp.int32, (L, L), 1)
    causal = s <= r
    shifts = [jnp.where(r - s == d, 1.0, 0.0).astype(BF16) for d in range(1, CONV_W)]

    def conv_silu(x_ref, halo_ref, lo):
        x = x_ref[...]
        halo = jnp.where(c > 0, halo_ref[...].astype(F32), 0.0)
        w = cw_ref[:, lo:lo + H_M * DK_M]
        y = cb_ref[:, lo:lo + H_M * DK_M] + w[CONV_W - 1:CONV_W] * x.astype(F32)
        top = jnp.zeros((SUBLANES, H_M * DK_M), F32)
        for d in range(1, CONV_W):
            wd = w[CONV_W - 1 - d:CONV_W - d]
            y = y + wd * jnp.dot(shifts[d - 1], x, preferred_element_type=F32)
            top = top + wd * jnp.concatenate(
                [halo[HALO - d:HALO], jnp.zeros((SUBLANES - d, H_M * DK_M), F32)], axis=0)
        y = jnp.concatenate([y[0:SUBLANES] + top, y[SUBLANES:]], axis=0)
        return y * jax.nn.sigmoid(y)

    q_all = conv_silu(q_ref, qh_ref, 0)
    emit_projection_work()
    k_all = conv_silu(k_ref, kh_ref, H_M * DK_M) * (DK_M ** -0.5)
    emit_projection_work()

    gc = g_ref[...] + gbr_ref[...]
    gr = gt_ref[S_IM:S_IM + 2 * H_M, :] + gbc_ref[S_IM:S_IM + 2 * H_M, :]
    tri_l = jnp.where(causal, 1.0, 0.0).astype(BF16)
    tri_u = jnp.where(r <= s, 1.0, 0.0).astype(BF16)
    b_cols = _split_dot(tri_l, _log_sigmoid(gc) * LOG2E)
    lf_rows = _log_sigmoid(gr) * LOG2E
    b_rows = jnp.dot(lf_rows.astype(BF16), tri_u, preferred_element_type=F32) \
        + jnp.dot((lf_rows - lf_rows.astype(BF16).astype(F32)).astype(BF16), tri_u,
                  preferred_element_type=F32)
    gc = gc * LOG2E
    gr = gr * LOG2E
    ones = jnp.ones((L, LANES), BF16)
    emit_projection_work()

    for h in range(H_M):
        emit_projection_work()
        q = q_all[:, h * DK_M:(h + 1) * DK_M]
        k = k_all[:, h * DK_M:(h + 1) * DK_M]
        v = jnp.concatenate([v_ref[:, h * DV_M:(h + 1) * DV_M], ones], axis=1)
        qb = q.astype(BF16)
        b_c = b_cols[:, S_FM + h:S_FM + h + 1]
        i_c = gc[:, S_IM + h:S_IM + h + 1]
        b_r = b_rows[H_M + h:H_M + h + 1, :]
        i_r = gr[h:h + 1, :]
        m_prev = m_ref[h]
        ct = ct_ref[h]

        log_d = jnp.where(causal, b_c - b_r + i_r, -jnp.inf)
        g = b_c + m_prev
        m_t = jnp.maximum(jnp.max(log_d, axis=-1, keepdims=True), g)
        qk = lax.dot_general(qb, k.astype(BF16), _NT, preferred_element_type=F32)
        s_mat = qk * jnp.exp2(log_d - m_t)
        inter = jnp.exp2(g - m_t)
        num = jnp.dot(s_mat.astype(BF16), v, preferred_element_type=F32) \
            + inter * jnp.dot(qb, ct.astype(BF16), preferred_element_type=F32)
        den = jnp.maximum(jnp.abs(num[:, DV_M:]), jnp.exp2(-m_t))
        hh = num[:, 0:DV_M] / jnp.concatenate([den] * (DV_M // LANES), axis=1)

        emit_projection_work()
        b_last = b_c[L - 1:L, :]
        a_r = b_last - b_r + i_r
        m_new = jnp.maximum(b_last + m_prev, jnp.max(a_r, axis=-1, keepdims=True))
        decay = jnp.exp2(b_last + m_prev - m_new)
        wgt_c = jnp.exp2(b_last - b_c + i_c - m_new)
        kw = k * wgt_c
        ct_ref[h] = decay * ct + jnp.dot(kw.T.astype(BF16), v, preferred_element_type=F32)
        m_ref[h] = m_new

        mu = jnp.mean(hh, axis=-1, keepdims=True)
        var = jnp.mean(jnp.square(hh - mu), axis=-1, keepdims=True)
        hn = (hh - mu) * lax.rsqrt(var + LN_EPS) * ng_ref[:, h * DV_M:(h + 1) * DV_M]
        og = o_ref[:, h * DV_M:(h + 1) * DV_M].astype(F32)
        zg = z_ref[:, h * DV_M:(h + 1) * DV_M].astype(F32)
        y_ref[:, h * DV_M:(h + 1) * DV_M] = (hn * jax.nn.sigmoid(og) * (zg * jax.nn.sigmoid(zg))).astype(BF16)

    assert not pending
    yp_ref[...] = y_ref[...]


def _mlstm_out(main, small, gate_t, conv_w, conv_b, gb_row, gb_col, norm_g, ya, x2d, w_out, ln_g, ln_b, B, T):
    nc = T // L_M
    hb = L_M // HALO
    qk_w = H_M * DK_M
    n_steps = B * nc

    def cur(col):
        return lambda s: (jnp.minimum(s, n_steps - 1), col)

    def prev(s):
        return (jnp.maximum(s - 1, 0), 0)

    def halo_map(col):
        return lambda s: (jnp.maximum(jnp.minimum(s, n_steps - 1) * hb - 1, 0), col)

    const = lambda s: (0, 0)
    return pl.pallas_call(
        functools.partial(_mlstm_out_kernel, n_chunks=nc, n_steps=n_steps),
        grid=(n_steps + 1,),
        in_specs=[pl.BlockSpec((L_M, qk_w), cur(C_QM // qk_w)),
                  pl.BlockSpec((L_M, qk_w), cur(C_KM // qk_w)),
                  pl.BlockSpec((HALO, qk_w), halo_map(C_QM // qk_w)),
                  pl.BlockSpec((HALO, qk_w), halo_map(C_KM // qk_w)),
                  pl.BlockSpec((L_M, W_M), cur(C_VM // W_M)),
                  pl.BlockSpec((L_M, W_M), cur(C_OM // W_M)),
                  pl.BlockSpec((L_M, W_M), cur(C_ZM // W_M)),
                  pl.BlockSpec((L_M, LANES), cur(D_C // LANES)),
                  pl.BlockSpec((LANES, L_M), lambda s: (0, jnp.minimum(s, n_steps - 1))),
                  pl.BlockSpec((CONV_W, 2 * qk_w), const),
                  pl.BlockSpec((1, 2 * qk_w), const),
                  pl.BlockSpec((1, LANES), const),
                  pl.BlockSpec((LANES, 1), const),
                  pl.BlockSpec((1, W_M), const),
                  pl.BlockSpec((L_M, W_A), prev),
                  pl.BlockSpec((L_M, D_MODEL), prev),
                  pl.BlockSpec((W_A + W_M, D_MODEL), const),
                  pl.BlockSpec((1, D_MODEL), const),
                  pl.BlockSpec((1, D_MODEL), const)],
        out_specs=pl.BlockSpec((L_M, D_MODEL), prev),
        out_shape=jax.ShapeDtypeStruct((B * T, D_MODEL), F32),
        scratch_shapes=[pltpu.VMEM((H_M, DK_M, DV_M + LANES), F32),
                        pltpu.VMEM((H_M, 1, 1), F32),
                        pltpu.VMEM((L_M, W_M), BF16),
                        pltpu.VMEM((L_M, W_M), BF16),
                        pltpu.VMEM((L_M, D_MODEL), F32)],
        compiler_params=pltpu.CompilerParams(
            dimension_semantics=("arbitrary",), vmem_limit_bytes=VMEM_LIMIT),
        name="mlstm_out",
    )(main, main, main, main, main, main, main, small, gate_t,
      conv_w, conv_b, gb_row, gb_col, norm_g, ya, x2d, w_out, ln_g, ln_b)


_W_IN_SEGS = (("q_a", W_A), ("c_kv", D_C), ("z_a", W_A), ("q_i", H_IDX * D_IDX), ("k_i", D_IDX),
              ("w_i", H_IDX), ("q_m", H_M * DK_M), ("k_m", H_M * DK_M), ("v_m", W_M), ("i_m", H_M),
              ("f_m", H_M), ("o_m", W_M), ("z_m", W_M))
_MAIN_ORDER = ("q_a", "z_a", "q_i", "q_m", "k_m", "v_m", "o_m", "z_m")
_SEG_NAMES = [name for name, _ in _W_IN_SEGS]
assert _SEG_NAMES.index("f_m") == _SEG_NAMES.index("i_m") + 1 and S_FM == S_IM + H_M


def _repack_kernel(wt_ref, main_ref, small_ref):
    src, off = {}, 0
    for name, width in _W_IN_SEGS:
        src[name] = (off, width)
        off += width
    dst = 0
    for name in _MAIN_ORDER:
        lo, width = src[name]
        main_ref[dst:dst + width, :] = wt_ref[lo:lo + width, :].astype(BF16)
        dst += width
    parts = [wt_ref[src[name][0]:src[name][0] + src[name][1], :] for name in ("c_kv", "k_i", "w_i")]
    lo = src["i_m"][0]
    parts.append(wt_ref[lo:lo + 2 * H_M, :])
    used = sum(p.shape[0] for p in parts)
    parts.append(jnp.zeros((N_SMALL - used, wt_ref.shape[1]), F32))
    small_ref[...] = jnp.concatenate(parts, axis=0).astype(BF16)


def _repack_w_in(w_in, tc=256):
    n_cols = sum(width for _, width in _W_IN_SEGS)
    wt = jnp.swapaxes(w_in, 1, 2)[0]
    return pl.pallas_call(
        _repack_kernel,
        grid=(D_MODEL // tc,),
        in_specs=[pl.BlockSpec((n_cols, tc), lambda i: (0, i))],
        out_specs=[pl.BlockSpec((N_MAIN, tc), lambda i: (0, i)),
                   pl.BlockSpec((N_SMALL, tc), lambda i: (0, i))],
        out_shape=[jax.ShapeDtypeStruct((N_MAIN, D_MODEL), BF16),
                   jax.ShapeDtypeStruct((N_SMALL, D_MODEL), BF16)],
        compiler_params=pltpu.CompilerParams(
            dimension_semantics=("arbitrary",), vmem_limit_bytes=VMEM_LIMIT),
        name="repack",
    )(wt)


def kernel(x, w_in, b_igate, b_fgate, kv_norm_g, w_uk, w_uv, idx_k_ln_g, idx_k_ln_b, rel_bias,
           conv_w, conv_b, mh_norm_g, w_out, ln_g, ln_b):
    B, T, D = x.shape
    assert D == D_MODEL and T % L_M == 0 and T % (2 * KB) == 0 and w_in.shape[0] == 1
    bias = _bias_tiles(rel_bias)
    x2d = x.reshape(B * T, D)
    w_main, w_small = _repack_w_in(w_in)
    main, small = _proj(x2d, w_main, w_small)
    ckv_n, ckv_t, kidx_n, gate_t = _prep(small, kv_norm_g[0][None], idx_k_ln_g[0][None], idx_k_ln_b[0][None])
    w_uk_t = jnp.transpose(w_uk[0], (0, 2, 1)).astype(BF16)
    w_uv_t = jnp.transpose(w_uv[0], (0, 2, 1)).astype(BF16)
    ya = _dsa(main, gate_t, ckv_n, ckv_t, kidx_n, w_uk_t, w_uv_t, bias, B, T)
    gb = jnp.zeros((LANES,), F32).at[S_IM:S_IM + H_M].set(b_igate[0]).at[S_FM:S_FM + H_M].set(b_fgate[0])
    out = _mlstm_out(main, small, gate_t, conv_w[0], conv_b[0][None], gb[None, :], gb[:, None],
                     mh_norm_g[0][None], ya, x2d, w_out[0].astype(BF16), ln_g[0][None], ln_b[0][None], B, T)
    return out.reshape(B, T, D)
```

```python
import functools
import math

import numpy as np
import jax
import jax.numpy as jnp
from jax import lax
from jax.experimental import pallas as pl
from jax.experimental.pallas import tpu as pltpu

F32 = jnp.float32
BF16 = jnp.bfloat16

D_MODEL = 2048
W_A = 1024
DH_A = 128
H_A = 8
D_C = 256
H_IDX = 16
D_IDX = 64
TOPK = 256
W_M = 1024
H_M = 4
DV_M = 256
DK_M = 128
CONV_W = 4
N_BUCKETS = 32
MAX_DIST = 128
ALPHA = 2.0 ** 0.25
LN_EPS = 1e-5

LANES = 128
SUBLANES = 8
VMEM_LIMIT = 56 * 1024 * 1024

QB = 256
KB = 256
KI = 128
CNT_ROWS = 256
CNT_ACC = 4 * SUBLANES
L_M = 256
HALO = 16
NEG = -1e30
LOG2E = math.log2(math.e)
ONES_ROWS = 16

C_QA, C_ZA, C_QI, C_QM, C_KM, C_VM, C_OM, C_ZM = 0, 1024, 2048, 3072, 3584, 4096, 5120, 6144
N_MAIN = 7168
N_SMALL = 384
S_KI, S_WI, S_IM, S_FM = 0, 64, 80, 84


def _t5_bucket_np(rel):
    max_exact = N_BUCKETS // 2
    n = np.maximum(rel, 0)
    nf = np.maximum(n, 1).astype(np.float32)
    large = max_exact + (np.log(nf / np.float32(max_exact)) / np.float32(math.log(MAX_DIST / max_exact))
                         * np.float32(N_BUCKETS - max_exact)).astype(np.int32)
    large = np.minimum(large, N_BUCKETS - 1)
    return np.where(n < max_exact, n, large).astype(np.int32)


FAR_BUCKET = int(_t5_bucket_np(np.array(2 * KB + 1)))


def _bucket_tiles():
    i = np.arange(QB)[None, :]
    j = np.arange(KB)[:, None]
    t0 = _t5_bucket_np(i - j)
    t1 = _t5_bucket_np(i - j + KB)
    assert (t5 := _t5_bucket_np(np.arange(KB + 1, 4096))).min() == t5.max() == FAR_BUCKET
    return np.stack([t0, t1]).astype(np.int32)


def _bias_kernel(bucket_ref, rb_ref, out_ref):
    h = pl.program_id(0)
    far = rb_ref[FAR_BUCKET, h]
    for k in range(2):
        bk = bucket_ref[k]
        acc = jnp.zeros((KB, QB), F32)
        for b in range(N_BUCKETS):
            acc = jnp.where(bk == b, rb_ref[b, h] - far, acc)
        for g in range(QB // LANES):
            out_ref[0, k, g] = acc[:, g * LANES:(g + 1) * LANES] * LOG2E


def _bias_tiles(rel_bias):
    bucket = jnp.asarray(_bucket_tiles())
    return pl.pallas_call(
        _bias_kernel,
        grid=(H_A,),
        in_specs=[pl.BlockSpec((2, KB, QB), lambda h: (0, 0, 0)),
                  pl.BlockSpec(memory_space=pltpu.SMEM)],
        out_specs=pl.BlockSpec((1, 2, QB // LANES, KB, LANES), lambda h: (h, 0, 0, 0, 0)),
        out_shape=jax.ShapeDtypeStruct((H_A, 2, QB // LANES, KB, LANES), F32),
        name="bias_tiles",
    )(bucket, rel_bias)


_NT = (((1,), (1,)), ((), ()))


def _proj_kernel(x_ref, w_ref, ws_ref, o_ref, os_ref, xb_ref):
    @pl.when(pl.program_id(1) == 0)
    def _():
        xb_ref[...] = x_ref[...].astype(BF16)
        os_ref[...] = lax.dot_general(xb_ref[...], ws_ref[...], _NT, preferred_element_type=F32)

    o_ref[...] = lax.dot_general(xb_ref[...], w_ref[...], _NT, preferred_element_type=F32).astype(BF16)


def _proj(x2d, w_main, w_small, tm=1024, tn=1792):
    M = x2d.shape[0]
    return pl.pallas_call(
        _proj_kernel,
        grid=(M // tm, N_MAIN // tn),
        in_specs=[pl.BlockSpec((tm, D_MODEL), lambda i, j: (i, 0)),
                  pl.BlockSpec((tn, D_MODEL), lambda i, j: (j, 0)),
                  pl.BlockSpec((N_SMALL, D_MODEL), lambda i, j: (0, 0))],
        out_specs=[pl.BlockSpec((tm, tn), lambda i, j: (i, j)),
                   pl.BlockSpec((tm, N_SMALL), lambda i, j: (i, 0))],
        out_shape=[jax.ShapeDtypeStruct((M, N_MAIN), BF16),
                   jax.ShapeDtypeStruct((M, N_SMALL), F32)],
        scratch_shapes=[pltpu.VMEM((tm, D_MODEL), BF16)],
        compiler_params=pltpu.CompilerParams(
            dimension_semantics=("arbitrary", "arbitrary"), vmem_limit_bytes=VMEM_LIMIT),
        name="proj",
    )(x2d, w_main, w_small)


def _prep_kernel(s_ref, kvg_ref, ig_ref, ib_ref, ckv_ref, ckvt_ref, kidx_ref, gt_ref):
    c = s_ref[:, 0:D_C]
    c = c * lax.rsqrt(jnp.mean(c * c, axis=-1, keepdims=True) + LN_EPS) * kvg_ref[...]
    ckv_ref[...] = c.astype(BF16)
    for r in range(ckvt_ref.shape[0]):
        ckvt_ref[r, 0:D_C, :] = c[r * KB:(r + 1) * KB, :].T.astype(BF16)
        ckvt_ref[r, D_C:D_C + ONES_ROWS, :] = jnp.ones((ONES_ROWS, KB), BF16)
    tile = s_ref[:, D_C:D_C + LANES]
    k = tile[:, S_KI:S_KI + D_IDX]
    mu = jnp.mean(k, axis=-1, keepdims=True)
    var = jnp.mean(jnp.square(k - mu), axis=-1, keepdims=True)
    kidx_ref[...] = ((k - mu) * lax.rsqrt(var + LN_EPS) * ig_ref[...] + ib_ref[...]).astype(BF16)
    gt_ref[...] = tile.T


def _prep(small, kv_g, idx_g, idx_b, tm=2048):
    M = small.shape[0]
    return pl.pallas_call(
        _prep_kernel,
        grid=(M // tm,),
        in_specs=[pl.BlockSpec((tm, N_SMALL), lambda i: (i, 0)),
                  pl.BlockSpec((1, D_C), lambda i: (0, 0)),
                  pl.BlockSpec((1, D_IDX), lambda i: (0, 0)),
                  pl.BlockSpec((1, D_IDX), lambda i: (0, 0))],
        out_specs=[pl.BlockSpec((tm, D_C), lambda i: (i, 0)),
                   pl.BlockSpec((tm // KB, D_C + ONES_ROWS, KB), lambda i: (i, 0, 0)),
                   pl.BlockSpec((tm, D_IDX), lambda i: (i, 0)),
                   pl.BlockSpec((LANES, tm), lambda i: (0, i))],
        out_shape=[jax.ShapeDtypeStruct((M, D_C), BF16),
                   jax.ShapeDtypeStruct((M // KB, D_C + ONES_ROWS, KB), BF16),
                   jax.ShapeDtypeStruct((M, D_IDX), BF16),
                   jax.ShapeDtypeStruct((LANES, M), F32)],
        name="prep",
    )(small, kv_g, idx_g, idx_b)


def _key_to_float(key):
    bits = jnp.where(key < 0, key ^ jnp.int32(0x7FFFFFFF), key)
    return lax.bitcast_convert_type(bits, F32)


def _dsa_kernel(qa_ref, za_ref, qi_ref, gt_ref, ckv_ref, ckvt_ref, kidx_ref, wukt_ref, wuvt_ref, bias_ref,
                y_ref, qall_ref, qr_ref, ha_ref, hb_ref, sc_ref, mb_ref, sa_ref, sb_ref, pa_ref, pb_ref,
                ta_ref, tb_ref, m_ref, al_ref, acc_ref, w_ref, thr_ref, cut_ref, cge_ref):
    qi = pl.program_id(1)
    nkb = qi + 1
    last_kb = mb_ref.shape[0] - 1

    for h in range(H_A):
        ql = lax.dot_general(wukt_ref[h], qa_ref[:, h * DH_A:(h + 1) * DH_A], (((1,), (1,)), ((), ())),
                             preferred_element_type=F32)
        qall_ref[h // (H_A // 2), :, (h % (H_A // 2)) * QB:(h % (H_A // 2) + 1) * QB] = (
            ql * (DH_A ** -0.5 * LOG2E)).astype(BF16)


    @pl.when(qi * QB < TOPK)
    def _():
        krow = lax.broadcasted_iota(jnp.int32, (KB, LANES), 0)
        qcol = lax.broadcasted_iota(jnp.int32, (KB, LANES), 1)
        for g in range(QB // LANES):
            mb_ref[0, g] = jnp.where(krow <= qcol + g * LANES, 0.0, NEG).astype(F32)

    @pl.when(qi * QB >= TOPK)
    def _():
        for h in range(H_IDX):
            qr_ref[h * QB:(h + 1) * QB, :] = qi_ref[:, h * D_IDX:(h + 1) * D_IDX]
        w_ref[...] = gt_ref[S_WI:S_WI + H_IDX, :] * ((D_IDX ** -0.5) * (H_IDX ** -0.5))
        krow = lax.broadcasted_iota(jnp.int32, (KI, LANES), 0)
        qcol = lax.broadcasted_iota(jnp.int32, (KI, LANES), 1)
        n_ki = nkb * (KB // KI)
        last_ki = sc_ref.shape[0] // KI - 1

        def head_dots(ki, dst_ref):
            k = kidx_ref[pl.ds(pl.multiple_of(jnp.minimum(ki, last_ki) * KI, KI), KI), :]
            dst_ref[...] = lax.dot_general(k, qr_ref[...], (((1,), (1,)), ((), ())),
                                           preferred_element_type=F32)

        def reduce_heads(src_ref, ki):
            for g in range(QB // LANES):
                lanes = slice(g * LANES, (g + 1) * LANES)
                acc = jnp.zeros((KI, LANES), F32)
                for h in range(H_IDX):
                    acc = acc + (jnp.maximum(src_ref[:, h * QB + g * LANES:h * QB + (g + 1) * LANES], 0.0)
                                 * w_ref[h:h + 1, lanes])
                sc_ref[pl.ds(pl.multiple_of(ki * KI, KI), KI), lanes] = jnp.where(
                    krow + (ki * KI - qi * QB - g * LANES) <= qcol, acc, -jnp.inf)

        assert CNT_ROWS == KB

        head_dots(0, ha_ref)

        def sc_body(j, carry):
            head_dots(2 * j + 1, hb_ref)
            reduce_heads(ha_ref, 2 * j)
            head_dots(2 * j + 2, ha_ref)
            reduce_heads(hb_ref, 2 * j + 1)
            return carry
        lax.fori_loop(0, n_ki // 2, sc_body, 0)

        n_cnt = (nkb * KB + CNT_ROWS - 1) // CNT_ROWS

        def count_where(pred, steps=None):
            def body(c, acc):
                parts = []
                for g in range(QB // LANES):
                    lanes = slice(g * LANES, (g + 1) * LANES)
                    rows = CNT_ROWS
                    if steps is None:
                        blk = sc_ref[pl.ds(pl.multiple_of(c * CNT_ROWS, CNT_ROWS), CNT_ROWS), lanes]
                    else:
                        if c == steps - 1:
                            rows = min(CNT_ROWS, (g + 1) * LANES)
                        blk = sc_ref[c * CNT_ROWS:c * CNT_ROWS + rows, lanes]
                    hit = jnp.where(pred(blk, c * CNT_ROWS, lanes), 1.0, 0.0).astype(F32)
                    parts.append(jnp.sum(hit.reshape(rows // CNT_ACC, CNT_ACC, LANES), axis=0))
                return acc + jnp.concatenate(parts, axis=1)
            acc = jnp.zeros((CNT_ACC, QB), F32)
            if steps is None:
                acc = lax.fori_loop(0, n_cnt, body, acc)
            else:
                for c in range(steps):
                    acc = body(c, acc)
            return jnp.sum(acc, axis=0, keepdims=True)

        def search(steps):
            n_g = QB // LANES

            def count_ge(g, cand):
                lanes = slice(g * LANES, (g + 1) * LANES)
                acc = jnp.zeros((CNT_ACC, LANES), F32)
                for c in range(steps):
                    rows = CNT_ROWS if c < steps - 1 else min(CNT_ROWS, (g + 1) * LANES)
                    hit = jnp.where(sc_ref[c * CNT_ROWS:c * CNT_ROWS + rows, lanes] >= cand, 1.0, 0.0)
                    acc = acc + jnp.sum(hit.astype(F32).reshape(rows // CNT_ACC, CNT_ACC, LANES), axis=0)
                return jnp.sum(acc, axis=0, keepdims=True)

            def bit_body(i, carry):
                new = []
                for g in range(n_g):
                    u, c_ge = carry[g]
                    trial = u | lax.shift_left(jnp.int32(1), 31 - i)
                    cnt = count_ge(g, _key_to_float(trial ^ jnp.int32(-2 ** 31)))
                    ok = cnt >= float(TOPK)
                    new.append((jnp.where(ok, trial, u), jnp.where(ok, cnt, c_ge)))
                return tuple(new)
            init = tuple((jnp.zeros((1, LANES), jnp.int32),
                          jnp.full((1, LANES), float(sc_ref.shape[0]), F32)) for _ in range(n_g))
            for g, (u, c_ge) in enumerate(lax.fori_loop(0, 32, bit_body, init)):
                lanes = slice(g * LANES, (g + 1) * LANES)
                thr_ref[:, lanes] = _key_to_float(u ^ jnp.int32(-2 ** 31))
                cge_ref[:, lanes] = c_ge

        for steps in range(1, sc_ref.shape[0] // CNT_ROWS + 1):
            pl.when(n_cnt == steps)(functools.partial(search, steps))
        has_ties = jnp.max(cge_ref[...]) > float(TOPK)

        @pl.when(jnp.logical_not(has_ties))
        def _():
            def mb_body(kb, carry):
                for g in range(QB // LANES):
                    lanes = slice(g * LANES, (g + 1) * LANES)
                    blk = sc_ref[pl.ds(pl.multiple_of(kb * KB, KB), KB), lanes]
                    mb_ref[kb, g] = jnp.where(blk >= thr_ref[:, lanes], 0.0, NEG).astype(F32)
                return carry
            lax.fori_loop(0, nkb, mb_body, 0)

        @pl.when(has_ties)
        def _():
            c_gt = count_where(lambda blk, row0, lanes: blk > thr_ref[:, lanes])
            need = float(TOPK) - c_gt
            rows = lax.broadcasted_iota(jnp.int32, (CNT_ROWS, LANES), 0)
            n_bits = (sc_ref.shape[0] - 1).bit_length()

            def idx_body(i, cut):
                trial = cut | lax.shift_left(jnp.int32(1), n_bits - 1 - i)
                cut_ref[...] = trial
                before = count_where(lambda blk, row0, lanes: (blk == thr_ref[:, lanes])
                                     & (rows + row0 < cut_ref[:, lanes]))
                return jnp.where(before < need, trial, cut)
            cut_ref[...] = lax.fori_loop(0, n_bits, idx_body, jnp.zeros((1, QB), jnp.int32))

            def mb_body(kb, carry):
                for g in range(QB // LANES):
                    lanes = slice(g * LANES, (g + 1) * LANES)
                    blk = sc_ref[pl.ds(pl.multiple_of(kb * KB, KB), KB), lanes]
                    thr = thr_ref[:, lanes]
                    keep = (blk > thr) | ((blk == thr) & (rows[0:KB] + kb * KB <= cut_ref[:, lanes]))
                    mb_ref[kb, g] = jnp.where(keep, 0.0, NEG).astype(F32)
                return carry
            lax.fori_loop(0, nkb, mb_body, 0)

    m_ref[...] = jnp.full(m_ref.shape, NEG, F32)
    acc_ref[...] = jnp.zeros(acc_ref.shape, F32)

    hh = H_A // 2
    s_refs, p_refs, t_refs = (sa_ref, sb_ref), (pa_ref, pb_ref), (ta_ref, tb_ref)
    pb_ref[...] = jnp.zeros(pb_ref.shape, BF16)
    al_ref[...] = jnp.ones(al_ref.shape, F32)

    def logits(kb, half):
        kv = ckv_ref[pl.ds(pl.multiple_of(jnp.minimum(kb, last_kb) * KB, KB), KB), :]
        res = jnp.dot(kv, qall_ref[half], preferred_element_type=F32)
        for u in range(res.shape[1] // LANES):
            s_refs[half][u] = res[:, u * LANES:(u + 1) * LANES]

    def softmax(kb, half, with_bias):
        tile = jnp.clip(qi - kb, 0, 1)
        for j in range(hh):
            for g in range(QB // LANES):
                lanes = slice(g * LANES, (g + 1) * LANES)
                cols = slice(j * QB + g * LANES, j * QB + (g + 1) * LANES)
                x = s_refs[half][j * (QB // LANES) + g] + mb_ref[kb, g]
                if with_bias:
                    x = x + bias_ref[half * hh + j, tile, g]
                m_prev = m_ref[half, :, cols]
                m_blk = jnp.max(x.reshape(KB // CNT_ACC, CNT_ACC, LANES), axis=0)
                m_new = jnp.maximum(m_prev, jnp.max(m_blk, axis=0, keepdims=True))
                al_ref[half, :, cols] = jnp.exp2(m_prev - m_new)
                m_ref[half, :, cols] = m_new
                p_refs[half][:, cols] = jnp.exp2(x - m_new).astype(BF16)

    def accumulate(kb, half):
        acc_ref[half] = acc_ref[half] * al_ref[half] + jnp.dot(
            ckvt_ref[jnp.maximum(kb, 0)], p_refs[half][...], preferred_element_type=F32)

    def sweep(first_kb, end_kb, with_bias):
        def body(kb, carry):
            logits(kb, 1)
            softmax(kb, 0, with_bias)
            accumulate(kb - 1, 1)
            logits(kb + 1, 0)
            softmax(kb, 1, with_bias)
            accumulate(kb, 0)
            return carry
        lax.fori_loop(first_kb, end_kb, body, 0)

    n_far = jnp.maximum(qi - 1, 0)
    logits(0, 0)
    sweep(0, n_far, False)
    sweep(n_far, nkb, True)
    accumulate(nkb - 1, 1)

    for h in range(H_A):
        half, cols = h // hh, slice((h % hh) * QB, (h % hh + 1) * QB)
        ya_t = jnp.dot(wuvt_ref[h], acc_ref[half, 0:D_C, cols].astype(BF16), preferred_element_type=F32)
        ya = (ya_t / acc_ref[half, D_C:D_C + 1, cols]).T
        z = za_ref[:, h * DH_A:(h + 1) * DH_A].astype(F32)
        y_ref[:, h * DH_A:(h + 1) * DH_A] = (ya * (z * jax.nn.sigmoid(z))).astype(BF16)


def _dsa(main, gate_t, ckv_n, ckv_t, kidx_n, w_uk_t, w_uv_t, bias, B, T):
    nq = T // QB
    return pl.pallas_call(
        _dsa_kernel,
        grid=(B, nq),
        in_specs=[pl.BlockSpec((QB, W_A), lambda b, q: (b * nq + q, C_QA // W_A)),
                  pl.BlockSpec((QB, W_A), lambda b, q: (b * nq + q, C_ZA // W_A)),
                  pl.BlockSpec((QB, H_IDX * D_IDX), lambda b, q: (b * nq + q, C_QI // (H_IDX * D_IDX))),
                  pl.BlockSpec((LANES, QB), lambda b, q: (0, b * nq + q)),
                  pl.BlockSpec((T, D_C), lambda b, q: (b, 0)),
                  pl.BlockSpec((T // KB, D_C + ONES_ROWS, KB), lambda b, q: (b, 0, 0)),
                  pl.BlockSpec((T, D_IDX), lambda b, q: (b, 0)),
                  pl.BlockSpec((H_A, D_C, DH_A), lambda b, q: (0, 0, 0)),
                  pl.BlockSpec((H_A, DH_A, D_C), lambda b, q: (0, 0, 0)),
                  pl.BlockSpec((H_A, 2, QB // LANES, KB, LANES), lambda b, q: (0, 0, 0, 0, 0))],
        out_specs=pl.BlockSpec((QB, W_A), lambda b, q: (b * nq + q, 0)),
        out_shape=jax.ShapeDtypeStruct((B * T, W_A), BF16),
        scratch_shapes=[pltpu.VMEM((2, D_C, H_A // 2 * QB), BF16),
                        pltpu.VMEM((H_IDX * QB, D_IDX), BF16),
                        pltpu.VMEM((KI, H_IDX * QB), F32),
                        pltpu.VMEM((KI, H_IDX * QB), F32),
                        pltpu.VMEM((T, QB), F32),
                        pltpu.VMEM((T // KB, QB // LANES, KB, LANES), F32),
                        pltpu.VMEM((H_A // 2 * QB // LANES, KB, LANES), F32),
                        pltpu.VMEM((H_A // 2 * QB // LANES, KB, LANES), F32),
                        pltpu.VMEM((KB, H_A // 2 * QB), BF16),
                        pltpu.VMEM((KB, H_A // 2 * QB), BF16),
                        pltpu.VMEM((D_C + ONES_ROWS, H_A // 2 * QB), F32),
                        pltpu.VMEM((D_C + ONES_ROWS, H_A // 2 * QB), F32),
                        pltpu.VMEM((2, 1, H_A // 2 * QB), F32),
                        pltpu.VMEM((2, 1, H_A // 2 * QB), F32),
                        pltpu.VMEM((2, D_C + ONES_ROWS, H_A // 2 * QB), F32),
                        pltpu.VMEM((H_IDX, QB), F32),
                        pltpu.VMEM((1, QB), F32),
                        pltpu.VMEM((1, QB), jnp.int32),
                        pltpu.VMEM((1, QB), F32)],
        compiler_params=pltpu.CompilerParams(
            dimension_semantics=("arbitrary", "arbitrary"), vmem_limit_bytes=VMEM_LIMIT),
        name="dsa",
    )(main, main, main, gate_t, ckv_n, ckv_t, kidx_n, w_uk_t, w_uv_t, bias)


def _split_dot(tri, x):
    hi = x.astype(BF16)
    lo = (x - hi.astype(F32)).astype(BF16)
    return jnp.dot(tri, hi, preferred_element_type=F32) + jnp.dot(tri, lo, preferred_element_type=F32)


def _log_sigmoid(x):
    return jnp.minimum(x, 0.0) - jnp.log1p(jnp.exp(-jnp.abs(x)))


def _mlstm_out_kernel(q_ref, k_ref, qh_ref, kh_ref, v_ref, o_ref, z_ref, g_ref, gt_ref,
                      cw_ref, cb_ref, gbr_ref, gbc_ref, ng_ref, ya_ref, x_ref, w_ref, lg_ref, lb_ref,
                      out_ref, ct_ref, m_ref, y_ref, yp_ref, p_ref, *, n_chunks, n_steps):
    step = pl.program_id(0)
    c = jnp.minimum(step, n_steps - 1) % n_chunks
    L = L_M

    @pl.when(step == 0)
    def _():
        yp_ref[...] = jnp.zeros(yp_ref.shape, BF16)

    @pl.when(c == 0)
    def _():
        ct_ref[...] = jnp.zeros(ct_ref.shape, F32)
        m_ref[...] = jnp.zeros(m_ref.shape, F32)

    n_pc = 2 * H_M
    pw = D_MODEL // n_pc

    def project_chunk(j):
        cols = slice(j * pw, (j + 1) * pw)
        p_ref[:, cols] = (jnp.dot(ya_ref[...], w_ref[0:W_A, cols], preferred_element_type=F32)
                          + jnp.dot(yp_ref[...], w_ref[W_A:W_A + W_M, cols], preferred_element_type=F32))

    def norm_previous():
        res = ALPHA * x_ref[...] + p_ref[...]
        mean = jnp.mean(res, axis=-1, keepdims=True)
        var_r = jnp.mean(jnp.square(res - mean), axis=-1, keepdims=True)
        out_ref[...] = (res - mean) * lax.rsqrt(var_r + LN_EPS) * lg_ref[...] + lb_ref[...]

    pending = [functools.partial(project_chunk, j) for j in range(n_pc)] + [norm_previous]

    def emit_projection_work():
        if pending:
            pending.pop(0)()

    emit_projection_work()

    r = lax.broadcasted_iota(jnp.int32, (L, L), 0)
    s = lax.broadcasted_iota(jnp.int32, (L, L), 1)
    causal = s <= r
    shifts = [jnp.where(r - s == d, 1.0, 0.0).astype(BF16) for d in range(1, CONV_W)]

    def conv_silu(x_ref, halo_ref, lo):
        x = x_ref[...]
        halo = jnp.where(c > 0, halo_ref[...].astype(F32), 0.0)
        w = cw_ref[:, lo:lo + H_M * DK_M]
        y = cb_ref[:, lo:lo + H_M * DK_M] + w[CONV_W - 1:CONV_W] * x.astype(F32)
        top = jnp.zeros((SUBLANES, H_M * DK_M), F32)
        for d in range(1, CONV_W):
            wd = w[CONV_W - 1 - d:CONV_W - d]
            y = y + wd * jnp.dot(shifts[d - 1], x, preferred_element_type=F32)
            top = top + wd * jnp.concatenate(
                [halo[HALO - d:HALO], jnp.zeros((SUBLANES - d, H_M * DK_M), F32)], axis=0)
        y = jnp.concatenate([y[0:SUBLANES] + top, y[SUBLANES:]], axis=0)
        return y * jax.nn.sigmoid(y)

    q_all = conv_silu(q_ref, qh_ref, 0)
    emit_projection_work()
    k_all = conv_silu(k_ref, kh_ref, H_M * DK_M) * (DK_M ** -0.5)
    emit_projection_work()

    gc = g_ref[...] + gbr_ref[...]
    gr = gt_ref[S_IM:S_IM + 2 * H_M, :] + gbc_ref[S_IM:S_IM + 2 * H_M, :]
    tri_l = jnp.where(causal, 1.0, 0.0).astype(BF16)
    tri_u = jnp.where(r <= s, 1.0, 0.0).astype(BF16)
    b_cols = _split_dot(tri_l, _log_sigmoid(gc) * LOG2E)
    lf_rows = _log_sigmoid(gr) * LOG2E
    b_rows = jnp.dot(lf_rows.astype(BF16), tri_u, preferred_element_type=F32) \
        + jnp.dot((lf_rows - lf_rows.astype(BF16).astype(F32)).astype(BF16), tri_u,
                  preferred_element_type=F32)
    gc = gc * LOG2E
    gr = gr * LOG2E
    ones = jnp.ones((L, LANES), BF16)
    emit_projection_work()

    for h in range(H_M):
        emit_projection_work()
        q = q_all[:, h * DK_M:(h + 1) * DK_M]
        k = k_all[:, h * DK_M:(h + 1) * DK_M]
        v = jnp.concatenate([v_ref[:, h * DV_M:(h + 1) * DV_M], ones], axis=1)
        qb = q.astype(BF16)
        b_c = b_cols[:, S_FM + h:S_FM + h + 1]
        i_c = gc[:, S_IM + h:S_IM + h + 1]
        b_r = b_rows[H_M + h:H_M + h + 1, :]
        i_r = gr[h:h + 1, :]
        m_prev = m_ref[h]
        ct = ct_ref[h]

        log_d = jnp.where(causal, b_c - b_r + i_r, -jnp.inf)
        g = b_c + m_prev
        m_t = jnp.maximum(jnp.max(log_d, axis=-1, keepdims=True), g)
        qk = lax.dot_general(qb, k.astype(BF16), _NT, preferred_element_type=F32)
        s_mat = qk * jnp.exp2(log_d - m_t)
        inter = jnp.exp2(g - m_t)
        num = jnp.dot(s_mat.astype(BF16), v, preferred_element_type=F32) \
            + inter * jnp.dot(qb, ct.astype(BF16), preferred_element_type=F32)
        den = jnp.maximum(jnp.abs(num[:, DV_M:]), jnp.exp2(-m_t))
        hh = num[:, 0:DV_M] / jnp.concatenate([den] * (DV_M // LANES), axis=1)

        emit_projection_work()
        b_last = b_c[L - 1:L, :]
        a_r = b_last - b_r + i_r
        m_new = jnp.maximum(b_last + m_prev, jnp.max(a_r, axis=-1, keepdims=True))
        decay = jnp.exp2(b_last + m_prev - m_new)
        wgt_c = jnp.exp2(b_last - b_c + i_c - m_new)
        kw = k * wgt_c
        ct_ref[h] = decay * ct + jnp.dot(kw.T.astype(BF16), v, preferred_element_type=F32)
        m_ref[h] = m_new

        mu = jnp.mean(hh, axis=-1, keepdims=True)
        var = jnp.mean(jnp.square(hh - mu), axis=-1, keepdims=True)
        hn = (hh - mu) * lax.rsqrt(var + LN_EPS) * ng_ref[:, h * DV_M:(h + 1) * DV_M]
        og = o_ref[:, h * DV_M:(h + 1) * DV_M].astype(F32)
        zg = z_ref[:, h * DV_M:(h + 1) * DV_M].astype(F32)
        y_ref[:, h * DV_M:(h + 1) * DV_M] = (hn * jax.nn.sigmoid(og) * (zg * jax.nn.sigmoid(zg))).astype(BF16)

    assert not pending
    yp_ref[...] = y_ref[...]


def _mlstm_out(main, small, gate_t, conv_w, conv_b, gb_row, gb_col, norm_g, ya, x2d, w_out, ln_g, ln_b, B, T):
    nc = T // L_M
    hb = L_M // HALO
    qk_w = H_M * DK_M
    n_steps = B * nc

    def cur(col):
        return lambda s: (jnp.minimum(s, n_steps - 1), col)

    def prev(s):
        return (jnp.maximum(s - 1, 0), 0)

    def halo_map(col):
        return lambda s: (jnp.maximum(jnp.minimum(s, n_steps - 1) * hb - 1, 0), col)

    const = lambda s: (0, 0)
    return pl.pallas_call(
        functools.partial(_mlstm_out_kernel, n_chunks=nc, n_steps=n_steps),
        grid=(n_steps + 1,),
        in_specs=[pl.BlockSpec((L_M, qk_w), cur(C_QM // qk_w)),
                  pl.BlockSpec((L_M, qk_w), cur(C_KM // qk_w)),
                  pl.BlockSpec((HALO, qk_w), halo_map(C_QM // qk_w)),
                  pl.BlockSpec((HALO, qk_w), halo_map(C_KM // qk_w)),
                  pl.BlockSpec((L_M, W_M), cur(C_VM // W_M)),
                  pl.BlockSpec((L_M, W_M), cur(C_OM // W_M)),
                  pl.BlockSpec((L_M, W_M), cur(C_ZM // W_M)),
                  pl.BlockSpec((L_M, LANES), cur(D_C // LANES)),
                  pl.BlockSpec((LANES, L_M), lambda s: (0, jnp.minimum(s, n_steps - 1))),
                  pl.BlockSpec((CONV_W, 2 * qk_w), const),
                  pl.BlockSpec((1, 2 * qk_w), const),
                  pl.BlockSpec((1, LANES), const),
                  pl.BlockSpec((LANES, 1), const),
                  pl.BlockSpec((1, W_M), const),
                  pl.BlockSpec((L_M, W_A), prev),
                  pl.BlockSpec((L_M, D_MODEL), prev),
                  pl.BlockSpec((W_A + W_M, D_MODEL), const),
                  pl.BlockSpec((1, D_MODEL), const),
                  pl.BlockSpec((1, D_MODEL), const)],
        out_specs=pl.BlockSpec((L_M, D_MODEL), prev),
        out_shape=jax.ShapeDtypeStruct((B * T, D_MODEL), F32),
        scratch_shapes=[pltpu.VMEM((H_M, DK_M, DV_M + LANES), F32),
                        pltpu.VMEM((H_M, 1, 1), F32),
                        pltpu.VMEM((L_M, W_M), BF16),
                        pltpu.VMEM((L_M, W_M), BF16),
                        pltpu.VMEM((L_M, D_MODEL), F32)],
        compiler_params=pltpu.CompilerParams(
            dimension_semantics=("arbitrary",), vmem_limit_bytes=VMEM_LIMIT),
        name="mlstm_out",
    )(main, main, main, main, main, main, main, small, gate_t,
      conv_w, conv_b, gb_row, gb_col, norm_g, ya, x2d, w_out, ln_g, ln_b)


_W_IN_SEGS = (("q_a", W_A), ("c_kv", D_C), ("z_a", W_A), ("q_i", H_IDX * D_IDX), ("k_i", D_IDX),
              ("w_i", H_IDX), ("q_m", H_M * DK_M), ("k_m", H_M * DK_M), ("v_m", W_M), ("i_m", H_M),
              ("f_m", H_M), ("o_m", W_M), ("z_m", W_M))
_MAIN_ORDER = ("q_a", "z_a", "q_i", "q_m", "k_m", "v_m", "o_m", "z_m")
_SEG_NAMES = [name for name, _ in _W_IN_SEGS]
assert _SEG_NAMES.index("f_m") == _SEG_NAMES.index("i_m") + 1 and S_FM == S_IM + H_M


def _repack_kernel(wt_ref, main_ref, small_ref):
    src, off = {}, 0
    for name, width in _W_IN_SEGS:
        src[name] = (off, width)
        off += width
    dst = 0
    for name in _MAIN_ORDER:
        lo, width = src[name]
        main_ref[dst:dst + width, :] = wt_ref[lo:lo + width, :].astype(BF16)
        dst += width
    parts = [wt_ref[src[name][0]:src[name][0] + src[name][1], :] for name in ("c_kv", "k_i", "w_i")]
    lo = src["i_m"][0]
    parts.append(wt_ref[lo:lo + 2 * H_M, :])
    used = sum(p.shape[0] for p in parts)
    parts.append(jnp.zeros((N_SMALL - used, wt_ref.shape[1]), F32))
    small_ref[...] = jnp.concatenate(parts, axis=0).astype(BF16)


def _repack_w_in(w_in, tc=256):
    n_cols = sum(width for _, width in _W_IN_SEGS)
    wt = jnp.swapaxes(w_in, 1, 2)[0]
    return pl.pallas_call(
        _repack_kernel,
        grid=(D_MODEL // tc,),
        in_specs=[pl.BlockSpec((n_cols, tc), lambda i: (0, i))],
        out_specs=[pl.BlockSpec((N_MAIN, tc), lambda i: (0, i)),
                   pl.BlockSpec((N_SMALL, tc), lambda i: (0, i))],
        out_shape=[jax.ShapeDtypeStruct((N_MAIN, D_MODEL), BF16),
                   jax.ShapeDtypeStruct((N_SMALL, D_MODEL), BF16)],
        compiler_params=pltpu.CompilerParams(
            dimension_semantics=("arbitrary",), vmem_limit_bytes=VMEM_LIMIT),
        name="repack",
    )(wt)


def kernel(x, w_in, b_igate, b_fgate, kv_norm_g, w_uk, w_uv, idx_k_ln_g, idx_k_ln_b, rel_bias,
           conv_w, conv_b, mh_norm_g, w_out, ln_g, ln_b):
    B, T, D = x.shape
    assert D == D_MODEL and T % L_M == 0 and T % (2 * KB) == 0 and w_in.shape[0] == 1
    bias = _bias_tiles(rel_bias)
    x2d = x.reshape(B * T, D)
    w_main, w_small = _repack_w_in(w_in)
    main, small = _proj(x2d, w_main, w_small)
    ckv_n, ckv_t, kidx_n, gate_t = _prep(small, kv_norm_g[0][None], idx_k_ln_g[0][None], idx_k_ln_b[0][None])
    w_uk_t = jnp.transpose(w_uk[0], (0, 2, 1)).astype(BF16)
    w_uv_t = jnp.transpose(w_uv[0], (0, 2, 1)).astype(BF16)
    ya = _dsa(main, gate_t, ckv_n, ckv_t, kidx_n, w_uk_t, w_uv_t, bias, B, T)
    gb = jnp.zeros((LANES,), F32).at[S_IM:S_IM + H_M].set(b_igate[0]).at[S_FM:S_FM + H_M].set(b_fgate[0])
    out = _mlstm_out(main, small, gate_t, conv_w[0], conv_b[0][None], gb[None, :], gb[:, None],
                     mh_norm_g[0][None], ya, x2d, w_out[0].astype(BF16), ln_g[0][None], ln_b[0][None], B, T)
    return out.reshape(B, T, D)
```

```python
import functools
import math

import numpy as np
import jax
import jax.numpy as jnp
from jax import lax
from jax.experimental import pallas as pl
from jax.experimental.pallas import tpu as pltpu

F32 = jnp.float32
BF16 = jnp.bfloat16

D_MODEL = 2048
W_A = 1024
DH_A = 128
H_A = 8
D_C = 256
H_IDX = 16
D_IDX = 64
TOPK = 256
W_M = 1024
H_M = 4
DV_M = 256
DK_M = 128
CONV_W = 4
N_BUCKETS = 32
MAX_DIST = 128
ALPHA = 2.0 ** 0.25
LN_EPS = 1e-5

LANES = 128
SUBLANES = 8
VMEM_LIMIT = 56 * 1024 * 1024

QB = 256
KB = 256
KI = 128
CNT_ROWS = 256
CNT_ACC = 4 * SUBLANES
L_M = 256
HALO = 16
NEG = -1e30
LOG2E = math.log2(math.e)
ONES_ROWS = 16

C_QA, C_ZA, C_QI, C_QM, C_KM, C_VM, C_OM, C_ZM = 0, 1024, 2048, 3072, 3584, 4096, 5120, 6144
N_MAIN = 7168
N_SMALL = 384
S_KI, S_WI, S_IM, S_FM = 0, 64, 80, 84


def _t5_bucket_np(rel):
    max_exact = N_BUCKETS // 2
    n = np.maximum(rel, 0)
    nf = np.maximum(n, 1).astype(np.float32)
    large = max_exact + (np.log(nf / np.float32(max_exact)) / np.float32(math.log(MAX_DIST / max_exact))
                         * np.float32(N_BUCKETS - max_exact)).astype(np.int32)
    large = np.minimum(large, N_BUCKETS - 1)
    return np.where(n < max_exact, n, large).astype(np.int32)


FAR_BUCKET = int(_t5_bucket_np(np.array(2 * KB + 1)))


def _bucket_tiles():
    i = np.arange(QB)[None, :]
    j = np.arange(KB)[:, None]
    t0 = _t5_bucket_np(i - j)
    t1 = _t5_bucket_np(i - j + KB)
    assert (t5 := _t5_bucket_np(np.arange(KB + 1, 4096))).min() == t5.max() == FAR_BUCKET
    return np.stack([t0, t1]).astype(np.int32)


def _bias_kernel(bucket_ref, rb_ref, out_ref):
    h = pl.program_id(0)
    far = rb_ref[FAR_BUCKET, h]
    for k in range(2):
        bk = bucket_ref[k]
        acc = jnp.zeros((KB, QB), F32)
        for b in range(N_BUCKETS):
            acc = jnp.where(bk == b, rb_ref[b, h] - far, acc)
        out_ref[0, k] = acc * LOG2E


def _bias_tiles(rel_bias):
    bucket = jnp.asarray(_bucket_tiles())
    return pl.pallas_call(
        _bias_kernel,
        grid=(H_A,),
        in_specs=[pl.BlockSpec((2, KB, QB), lambda h: (0, 0, 0)),
                  pl.BlockSpec(memory_space=pltpu.SMEM)],
        out_specs=pl.BlockSpec((1, 2, KB, QB), lambda h: (h, 0, 0, 0)),
        out_shape=jax.ShapeDtypeStruct((H_A, 2, KB, QB), F32),
        name="bias_tiles",
    )(bucket, rel_bias)


_NT = (((1,), (1,)), ((), ()))


def _proj_kernel(x_ref, w_ref, ws_ref, o_ref, os_ref, xb_ref):
    @pl.when(pl.program_id(1) == 0)
    def _():
        xb_ref[...] = x_ref[...].astype(BF16)
        os_ref[...] = lax.dot_general(xb_ref[...], ws_ref[...], _NT, preferred_element_type=F32)

    o_ref[...] = lax.dot_general(xb_ref[...], w_ref[...], _NT, preferred_element_type=F32).astype(BF16)


def _proj(x2d, w_main, w_small, tm=1024, tn=1792):
    M = x2d.shape[0]
    return pl.pallas_call(
        _proj_kernel,
        grid=(M // tm, N_MAIN // tn),
        in_specs=[pl.BlockSpec((tm, D_MODEL), lambda i, j: (i, 0)),
                  pl.BlockSpec((tn, D_MODEL), lambda i, j: (j, 0)),
                  pl.BlockSpec((N_SMALL, D_MODEL), lambda i, j: (0, 0))],
        out_specs=[pl.BlockSpec((tm, tn), lambda i, j: (i, j)),
                   pl.BlockSpec((tm, N_SMALL), lambda i, j: (i, 0))],
        out_shape=[jax.ShapeDtypeStruct((M, N_MAIN), BF16),
                   jax.ShapeDtypeStruct((M, N_SMALL), F32)],
        scratch_shapes=[pltpu.VMEM((tm, D_MODEL), BF16)],
        compiler_params=pltpu.CompilerParams(
            dimension_semantics=("arbitrary", "arbitrary"), vmem_limit_bytes=VMEM_LIMIT),
        name="proj",
    )(x2d, w_main, w_small)


def _prep_kernel(s_ref, kvg_ref, ig_ref, ib_ref, ckv_ref, ckvt_ref, kidx_ref, gt_ref):
    c = s_ref[:, 0:D_C]
    c = c * lax.rsqrt(jnp.mean(c * c, axis=-1, keepdims=True) + LN_EPS) * kvg_ref[...]
    ckv_ref[...] = c.astype(BF16)
    for r in range(ckvt_ref.shape[0]):
        ckvt_ref[r, 0:D_C, :] = c[r * KB:(r + 1) * KB, :].T.astype(BF16)
        ckvt_ref[r, D_C:D_C + ONES_ROWS, :] = jnp.ones((ONES_ROWS, KB), BF16)
    tile = s_ref[:, D_C:D_C + LANES]
    k = tile[:, S_KI:S_KI + D_IDX]
    mu = jnp.mean(k, axis=-1, keepdims=True)
    var = jnp.mean(jnp.square(k - mu), axis=-1, keepdims=True)
    kidx_ref[...] = ((k - mu) * lax.rsqrt(var + LN_EPS) * ig_ref[...] + ib_ref[...]).astype(BF16)
    gt_ref[...] = tile.T


def _prep(small, kv_g, idx_g, idx_b, tm=2048):
    M = small.shape[0]
    return pl.pallas_call(
        _prep_kernel,
        grid=(M // tm,),
        in_specs=[pl.BlockSpec((tm, N_SMALL), lambda i: (i, 0)),
                  pl.BlockSpec((1, D_C), lambda i: (0, 0)),
                  pl.BlockSpec((1, D_IDX), lambda i: (0, 0)),
                  pl.BlockSpec((1, D_IDX), lambda i: (0, 0))],
        out_specs=[pl.BlockSpec((tm, D_C), lambda i: (i, 0)),
                   pl.BlockSpec((tm // KB, D_C + ONES_ROWS, KB), lambda i: (i, 0, 0)),
                   pl.BlockSpec((tm, D_IDX), lambda i: (i, 0)),
                   pl.BlockSpec((LANES, tm), lambda i: (0, i))],
        out_shape=[jax.ShapeDtypeStruct((M, D_C), BF16),
                   jax.ShapeDtypeStruct((M // KB, D_C + ONES_ROWS, KB), BF16),
                   jax.ShapeDtypeStruct((M, D_IDX), BF16),
                   jax.ShapeDtypeStruct((LANES, M), F32)],
        name="prep",
    )(small, kv_g, idx_g, idx_b)


def _key_to_float(key):
    bits = jnp.where(key < 0, key ^ jnp.int32(0x7FFFFFFF), key)
    return lax.bitcast_convert_type(bits, F32)


def _dsa_kernel(qa_ref, za_ref, qi_ref, gt_ref, ckv_ref, ckvt_ref, kidx_ref, wukt_ref, wuvt_ref, bias_ref,
                y_ref, qall_ref, qr_ref, ha_ref, hb_ref, sc_ref, mb_ref, sa_ref, sb_ref, pa_ref, pb_ref,
                m_ref, al_ref, acc_ref, w_ref, thr_ref, cut_ref, cge_ref):
    qi = pl.program_id(1)
    nkb = qi + 1
    last_kb = mb_ref.shape[0] - 1

    for h in range(H_A):
        ql = lax.dot_general(wukt_ref[h], qa_ref[:, h * DH_A:(h + 1) * DH_A], (((1,), (1,)), ((), ())),
                             preferred_element_type=F32)
        qall_ref[h // (H_A // 2), :, (h % (H_A // 2)) * QB:(h % (H_A // 2) + 1) * QB] = (
            ql * (DH_A ** -0.5 * LOG2E)).astype(BF16)


    @pl.when(qi * QB < TOPK)
    def _():
        krow = lax.broadcasted_iota(jnp.int32, (KB, QB), 0)
        qcol = lax.broadcasted_iota(jnp.int32, (KB, QB), 1)
        mb_ref[0] = jnp.where(krow <= qcol, 0.0, NEG).astype(F32)

    @pl.when(qi * QB >= TOPK)
    def _():
        for h in range(H_IDX):
            qr_ref[h * QB:(h + 1) * QB, :] = qi_ref[:, h * D_IDX:(h + 1) * D_IDX]
        w_ref[...] = gt_ref[S_WI:S_WI + H_IDX, :] * ((D_IDX ** -0.5) * (H_IDX ** -0.5))
        krow = lax.broadcasted_iota(jnp.int32, (KI, LANES), 0)
        qcol = lax.broadcasted_iota(jnp.int32, (KI, LANES), 1)
        n_ki = nkb * (KB // KI)
        last_ki = sc_ref.shape[0] // KI - 1

        def head_dots(ki, dst_ref):
            k = kidx_ref[pl.ds(pl.multiple_of(jnp.minimum(ki, last_ki) * KI, KI), KI), :]
            dst_ref[...] = lax.dot_general(k, qr_ref[...], (((1,), (1,)), ((), ())),
                                           preferred_element_type=F32)

        def reduce_heads(src_ref, ki):
            for g in range(QB // LANES):
                lanes = slice(g * LANES, (g + 1) * LANES)
                acc = jnp.zeros((KI, LANES), F32)
                for h in range(H_IDX):
                    acc = acc + (jnp.maximum(src_ref[:, h * QB + g * LANES:h * QB + (g + 1) * LANES], 0.0)
                                 * w_ref[h:h + 1, lanes])
                sc_ref[pl.ds(pl.multiple_of(ki * KI, KI), KI), lanes] = jnp.where(
                    krow + (ki * KI - qi * QB - g * LANES) <= qcol, acc, -jnp.inf)

        assert CNT_ROWS == KB

        head_dots(0, ha_ref)

        def sc_body(j, carry):
            head_dots(2 * j + 1, hb_ref)
            reduce_heads(ha_ref, 2 * j)
            head_dots(2 * j + 2, ha_ref)
            reduce_heads(hb_ref, 2 * j + 1)
            return carry
        lax.fori_loop(0, n_ki // 2, sc_body, 0)

        n_cnt = (nkb * KB + CNT_ROWS - 1) // CNT_ROWS

        def count_where(pred, steps=None):
            def body(c, acc):
                parts = []
                for g in range(QB // LANES):
                    lanes = slice(g * LANES, (g + 1) * LANES)
                    rows = CNT_ROWS
                    if steps is None:
                        blk = sc_ref[pl.ds(pl.multiple_of(c * CNT_ROWS, CNT_ROWS), CNT_ROWS), lanes]
                    else:
                        if c == steps - 1:
                            rows = min(CNT_ROWS, (g + 1) * LANES)
                        blk = sc_ref[c * CNT_ROWS:c * CNT_ROWS + rows, lanes]
                    hit = jnp.where(pred(blk, c * CNT_ROWS, lanes), 1.0, 0.0).astype(F32)
                    parts.append(jnp.sum(hit.reshape(rows // CNT_ACC, CNT_ACC, LANES), axis=0))
                return acc + jnp.concatenate(parts, axis=1)
            acc = jnp.zeros((CNT_ACC, QB), F32)
            if steps is None:
                acc = lax.fori_loop(0, n_cnt, body, acc)
            else:
                for c in range(steps):
                    acc = body(c, acc)
            return jnp.sum(acc, axis=0, keepdims=True)

        def search(steps):
            n_g = QB // LANES

            def count_ge(g, cand):
                lanes = slice(g * LANES, (g + 1) * LANES)
                acc = jnp.zeros((CNT_ACC, LANES), F32)
                for c in range(steps):
                    rows = CNT_ROWS if c < steps - 1 else min(CNT_ROWS, (g + 1) * LANES)
                    hit = jnp.where(sc_ref[c * CNT_ROWS:c * CNT_ROWS + rows, lanes] >= cand, 1.0, 0.0)
                    acc = acc + jnp.sum(hit.astype(F32).reshape(rows // CNT_ACC, CNT_ACC, LANES), axis=0)
                return jnp.sum(acc, axis=0, keepdims=True)

            def bit_body(i, carry):
                new = []
                for g in range(n_g):
                    u, c_ge = carry[g]
                    trial = u | lax.shift_left(jnp.int32(1), 31 - i)
                    cnt = count_ge(g, _key_to_float(trial ^ jnp.int32(-2 ** 31)))
                    ok = cnt >= float(TOPK)
                    new.append((jnp.where(ok, trial, u), jnp.where(ok, cnt, c_ge)))
                return tuple(new)
            init = tuple((jnp.zeros((1, LANES), jnp.int32),
                          jnp.full((1, LANES), float(sc_ref.shape[0]), F32)) for _ in range(n_g))
            for g, (u, c_ge) in enumerate(lax.fori_loop(0, 32, bit_body, init)):
                lanes = slice(g * LANES, (g + 1) * LANES)
                thr_ref[:, lanes] = _key_to_float(u ^ jnp.int32(-2 ** 31))
                cge_ref[:, lanes] = c_ge

        for steps in range(1, sc_ref.shape[0] // CNT_ROWS + 1):
            pl.when(n_cnt == steps)(functools.partial(search, steps))
        has_ties = jnp.max(cge_ref[...]) > float(TOPK)

        @pl.when(jnp.logical_not(has_ties))
        def _():
            def mb_body(kb, carry):
                blk = sc_ref[pl.ds(pl.multiple_of(kb * KB, KB), KB), :]
                mb_ref[kb] = jnp.where(blk >= thr_ref[...], 0.0, NEG).astype(F32)
                return carry
            lax.fori_loop(0, nkb, mb_body, 0)

        @pl.when(has_ties)
        def _():
            c_gt = count_where(lambda blk, row0, lanes: blk > thr_ref[:, lanes])
            need = float(TOPK) - c_gt
            rows = lax.broadcasted_iota(jnp.int32, (CNT_ROWS, LANES), 0)
            n_bits = (sc_ref.shape[0] - 1).bit_length()

            def idx_body(i, cut):
                trial = cut | lax.shift_left(jnp.int32(1), n_bits - 1 - i)
                cut_ref[...] = trial
                before = count_where(lambda blk, row0, lanes: (blk == thr_ref[:, lanes])
                                     & (rows + row0 < cut_ref[:, lanes]))
                return jnp.where(before < need, trial, cut)
            cut_ref[...] = lax.fori_loop(0, n_bits, idx_body, jnp.zeros((1, QB), jnp.int32))

            def mb_body(kb, carry):
                for g in range(QB // LANES):
                    lanes = slice(g * LANES, (g + 1) * LANES)
                    blk = sc_ref[pl.ds(pl.multiple_of(kb * KB, KB), KB), lanes]
                    thr = thr_ref[:, lanes]
                    keep = (blk > thr) | ((blk == thr) & (rows[0:KB] + kb * KB <= cut_ref[:, lanes]))
                    mb_ref[kb, :, lanes] = jnp.where(keep, 0.0, NEG).astype(F32)
                return carry
            lax.fori_loop(0, nkb, mb_body, 0)

    m_ref[...] = jnp.full(m_ref.shape, NEG, F32)
    acc_ref[...] = jnp.zeros(acc_ref.shape, F32)

    hh = H_A // 2
    s_refs, p_refs = (sa_ref, sb_ref), (pa_ref, pb_ref)
    pb_ref[...] = jnp.zeros(pb_ref.shape, BF16)
    al_ref[...] = jnp.ones(al_ref.shape, F32)

    def logits(kb, half):
        kv = ckv_ref[pl.ds(pl.multiple_of(jnp.minimum(kb, last_kb) * KB, KB), KB), :]
        res = jnp.dot(kv, qall_ref[half], preferred_element_type=F32)
        for u in range(res.shape[1] // LANES):
            s_refs[half][u] = res[:, u * LANES:(u + 1) * LANES]

    def softmax(kb, half, with_bias):
        tile = jnp.clip(qi - kb, 0, 1)
        for j in range(hh):
            for g in range(QB // LANES):
                lanes = slice(g * LANES, (g + 1) * LANES)
                cols = slice(j * QB + g * LANES, j * QB + (g + 1) * LANES)
                x = s_refs[half][j * (QB // LANES) + g] + mb_ref[kb, :, lanes]
                if with_bias:
                    x = x + bias_ref[half * hh + j, tile, :, lanes]
                m_prev = m_ref[half, :, cols]
                m_blk = jnp.max(x.reshape(KB // CNT_ACC, CNT_ACC, LANES), axis=0)
                m_new = jnp.maximum(m_prev, jnp.max(m_blk, axis=0, keepdims=True))
                al_ref[half, :, cols] = jnp.exp2(m_prev - m_new)
                m_ref[half, :, cols] = m_new
                p_refs[half][:, cols] = jnp.exp2(x - m_new).astype(BF16)

    def accumulate(kb, half):
        acc_ref[half] = acc_ref[half] * al_ref[half] + jnp.dot(
            ckvt_ref[jnp.maximum(kb, 0)], p_refs[half][...], preferred_element_type=F32)

    def sweep(first_kb, end_kb, with_bias):
        def body(kb, carry):
            logits(kb, 1)
            softmax(kb, 0, with_bias)
            accumulate(kb - 1, 1)
            logits(kb + 1, 0)
            softmax(kb, 1, with_bias)
            accumulate(kb, 0)
            return carry
        lax.fori_loop(first_kb, end_kb, body, 0)

    n_far = jnp.maximum(qi - 1, 0)
    logits(0, 0)
    sweep(0, n_far, False)
    sweep(n_far, nkb, True)
    accumulate(nkb - 1, 1)

    for h in range(H_A):
        half, cols = h // hh, slice((h % hh) * QB, (h % hh + 1) * QB)
        ya_t = jnp.dot(wuvt_ref[h], acc_ref[half, 0:D_C, cols].astype(BF16), preferred_element_type=F32)
        ya = (ya_t / acc_ref[half, D_C:D_C + 1, cols]).T
        z = za_ref[:, h * DH_A:(h + 1) * DH_A].astype(F32)
        y_ref[:, h * DH_A:(h + 1) * DH_A] = (ya * (z * jax.nn.sigmoid(z))).astype(BF16)


def _dsa(main, gate_t, ckv_n, ckv_t, kidx_n, w_uk_t, w_uv_t, bias, B, T):
    nq = T // QB
    return pl.pallas_call(
        _dsa_kernel,
        grid=(B, nq),
        in_specs=[pl.BlockSpec((QB, W_A), lambda b, q: (b * nq + q, C_QA // W_A)),
                  pl.BlockSpec((QB, W_A), lambda b, q: (b * nq + q, C_ZA // W_A)),
                  pl.BlockSpec((QB, H_IDX * D_IDX), lambda b, q: (b * nq + q, C_QI // (H_IDX * D_IDX))),
                  pl.BlockSpec((LANES, QB), lambda b, q: (0, b * nq + q)),
                  pl.BlockSpec((T, D_C), lambda b, q: (b, 0)),
                  pl.BlockSpec((T // KB, D_C + ONES_ROWS, KB), lambda b, q: (b, 0, 0)),
                  pl.BlockSpec((T, D_IDX), lambda b, q: (b, 0)),
                  pl.BlockSpec((H_A, D_C, DH_A), lambda b, q: (0, 0, 0)),
                  pl.BlockSpec((H_A, DH_A, D_C), lambda b, q: (0, 0, 0)),
                  pl.BlockSpec((H_A, 2, KB, QB), lambda b, q: (0, 0, 0, 0))],
        out_specs=pl.BlockSpec((QB, W_A), lambda b, q: (b * nq + q, 0)),
        out_shape=jax.ShapeDtypeStruct((B * T, W_A), BF16),
        scratch_shapes=[pltpu.VMEM((2, D_C, H_A // 2 * QB), BF16),
                        pltpu.VMEM((H_IDX * QB, D_IDX), BF16),
                        pltpu.VMEM((KI, H_IDX * QB), F32),
                        pltpu.VMEM((KI, H_IDX * QB), F32),
                        pltpu.VMEM((T, QB), F32),
                        pltpu.VMEM((T // KB, KB, QB), F32),
                        pltpu.VMEM((H_A // 2 * QB // LANES, KB, LANES), F32),
                        pltpu.VMEM((H_A // 2 * QB // LANES, KB, LANES), F32),
                        pltpu.VMEM((KB, H_A // 2 * QB), BF16),
                        pltpu.VMEM((KB, H_A // 2 * QB), BF16),
                        pltpu.VMEM((2, 1, H_A // 2 * QB), F32),
                        pltpu.VMEM((2, 1, H_A // 2 * QB), F32),
                        pltpu.VMEM((2, D_C + ONES_ROWS, H_A // 2 * QB), F32),
                        pltpu.VMEM((H_IDX, QB), F32),
                        pltpu.VMEM((1, QB), F32),
                        pltpu.VMEM((1, QB), jnp.int32),
                        pltpu.VMEM((1, QB), F32)],
        compiler_params=pltpu.CompilerParams(
            dimension_semantics=("arbitrary", "arbitrary"), vmem_limit_bytes=VMEM_LIMIT),
        name="dsa",
    )(main, main, main, gate_t, ckv_n, ckv_t, kidx_n, w_uk_t, w_uv_t, bias)


def _split_dot(tri, x):
    hi = x.astype(BF16)
    lo = (x - hi.astype(F32)).astype(BF16)
    return jnp.dot(tri, hi, preferred_element_type=F32) + jnp.dot(tri, lo, preferred_element_type=F32)


def _log_sigmoid(x):
    return jnp.minimum(x, 0.0) - jnp.log1p(jnp.exp(-jnp.abs(x)))


def _mlstm_out_kernel(q_ref, k_ref, qh_ref, kh_ref, v_ref, o_ref, z_ref, g_ref, gt_ref,
                      cw_ref, cb_ref, gbr_ref, gbc_ref, ng_ref, ya_ref, x_ref, w_ref, lg_ref, lb_ref,
                      out_ref, ct_ref, m_ref, y_ref, yp_ref, p_ref, *, n_chunks, n_steps):
    step = pl.program_id(0)
    c = jnp.minimum(step, n_steps - 1) % n_chunks
    L = L_M

    @pl.when(step == 0)
    def _():
        yp_ref[...] = jnp.zeros(yp_ref.shape, BF16)

    @pl.when(c == 0)
    def _():
        ct_ref[...] = jnp.zeros(ct_ref.shape, F32)
        m_ref[...] = jnp.zeros(m_ref.shape, F32)

    n_pc = 2 * H_M
    pw = D_MODEL // n_pc

    def project_chunk(j):
        cols = slice(j * pw, (j + 1) * pw)
        p_ref[:, cols] = (jnp.dot(ya_ref[...], w_ref[0:W_A, cols], preferred_element_type=F32)
                          + jnp.dot(yp_ref[...], w_ref[W_A:W_A + W_M, cols], preferred_element_type=F32))

    def norm_previous():
        res = ALPHA * x_ref[...] + p_ref[...]
        mean = jnp.mean(res, axis=-1, keepdims=True)
        var_r = jnp.mean(jnp.square(res - mean), axis=-1, keepdims=True)
        out_ref[...] = (res - mean) * lax.rsqrt(var_r + LN_EPS) * lg_ref[...] + lb_ref[...]

    pending = [functools.partial(project_chunk, j) for j in range(n_pc)] + [norm_previous]

    def emit_projection_work():
        if pending:
            pending.pop(0)()

    emit_projection_work()

    r = lax.broadcasted_iota(jnp.int32, (L, L), 0)
    s = lax.broadcasted_iota(jnp.int32, (L, L), 1)
    causal = s <= r
    shifts = [jnp.where(r - s == d, 1.0, 0.0).astype(BF16) for d in range(1, CONV_W)]

    def conv_silu(x_ref, halo_ref, lo):
        x = x_ref[...]
        halo = jnp.where(c > 0, halo_ref[...].astype(F32), 0.0)
        w = cw_ref[:, lo:lo + H_M * DK_M]
        y = cb_ref[:, lo:lo + H_M * DK_M] + w[CONV_W - 1:CONV_W] * x.astype(F32)
        top = jnp.zeros((SUBLANES, H_M * DK_M), F32)
        for d in range(1, CONV_W):
            wd = w[CONV_W - 1 - d:CONV_W - d]
            y = y + wd * jnp.dot(shifts[d - 1], x, preferred_element_type=F32)
            top = top + wd * jnp.concatenate(
                [halo[HALO - d:HALO], jnp.zeros((SUBLANES - d, H_M * DK_M), F32)], axis=0)
        y = jnp.concatenate([y[0:SUBLANES] + top, y[SUBLANES:]], axis=0)
        return y * jax.nn.sigmoid(y)

    q_all = conv_silu(q_ref, qh_ref, 0)
    emit_projection_work()
    k_all = conv_silu(k_ref, kh_ref, H_M * DK_M) * (DK_M ** -0.5)
    emit_projection_work()

    gc = g_ref[...] + gbr_ref[...]
    gr = gt_ref[S_IM:S_IM + 2 * H_M, :] + gbc_ref[S_IM:S_IM + 2 * H_M, :]
    tri_l = jnp.where(causal, 1.0, 0.0).astype(BF16)
    tri_u = jnp.where(r <= s, 1.0, 0.0).astype(BF16)
    b_cols = _split_dot(tri_l, _log_sigmoid(gc) * LOG2E)
    lf_rows = _log_sigmoid(gr) * LOG2E
    b_rows = jnp.dot(lf_rows.astype(BF16), tri_u, preferred_element_type=F32) \
        + jnp.dot((lf_rows - lf_rows.astype(BF16).astype(F32)).astype(BF16), tri_u,
                  preferred_element_type=F32)
    gc = gc * LOG2E
    gr = gr * LOG2E
    ones = jnp.ones((L, LANES), BF16)
    emit_projection_work()

    for h in range(H_M):
        emit_projection_work()
        q = q_all[:, h * DK_M:(h + 1) * DK_M]
        k = k_all[:, h * DK_M:(h + 1) * DK_M]
        v = jnp.concatenate([v_ref[:, h * DV_M:(h + 1) * DV_M], ones], axis=1)
        qb = q.astype(BF16)
        b_c = b_cols[:, S_FM + h:S_FM + h + 1]
        i_c = gc[:, S_IM + h:S_IM + h + 1]
        b_r = b_rows[H_M + h:H_M + h + 1, :]
        i_r = gr[h:h + 1, :]
        m_prev = m_ref[h]
        ct = ct_ref[h]

        log_d = jnp.where(causal, b_c - b_r + i_r, -jnp.inf)
        g = b_c + m_prev
        m_t = jnp.maximum(jnp.max(log_d, axis=-1, keepdims=True), g)
        qk = lax.dot_general(qb, k.astype(BF16), _NT, preferred_element_type=F32)
        s_mat = qk * jnp.exp2(log_d - m_t)
        inter = jnp.exp2(g - m_t)
        num = jnp.dot(s_mat.astype(BF16), v, preferred_element_type=F32) \
            + inter * jnp.dot(qb, ct.astype(BF16), preferred_element_type=F32)
        den = jnp.maximum(jnp.abs(num[:, DV_M:]), jnp.exp2(-m_t))
        hh = num[:, 0:DV_M] / jnp.concatenate([den] * (DV_M // LANES), axis=1)

        emit_projection_work()
        b_last = b_c[L - 1:L, :]
        a_r = b_last - b_r + i_r
        m_new = jnp.maximum(b_last + m_prev, jnp.max(a_r, axis=-1, keepdims=True))
        decay = jnp.exp2(b_last + m_prev - m_new)
        wgt_c = jnp.exp2(b_last - b_c + i_c - m_new)
        kw = k * wgt_c
        ct_ref[h] = decay * ct + jnp.dot(kw.T.astype(BF16), v, preferred_element_type=F32)
        m_ref[h] = m_new

        mu = jnp.mean(hh, axis=-1, keepdims=True)
        var = jnp.mean(jnp.square(hh - mu), axis=-1, keepdims=True)
        hn = (hh - mu) * lax.rsqrt(var + LN_EPS) * ng_ref[:, h * DV_M:(h + 1) * DV_M]
        og = o_ref[:, h * DV_M:(h + 1) * DV_M].astype(F32)
        zg = z_ref[:, h * DV_M:(h + 1) * DV_M].astype(F32)
        y_ref[:, h * DV_M:(h + 1) * DV_M] = (hn * jax.nn.sigmoid(og) * (zg * jax.nn.sigmoid(zg))).astype(BF16)

    assert not pending
    yp_ref[...] = y_ref[...]


def _mlstm_out(main, small, gate_t, conv_w, conv_b, gb_row, gb_col, norm_g, ya, x2d, w_out, ln_g, ln_b, B, T):
    nc = T // L_M
    hb = L_M // HALO
    qk_w = H_M * DK_M
    n_steps = B * nc

    def cur(col):
        return lambda s: (jnp.minimum(s, n_steps - 1), col)

    def prev(s):
        return (jnp.maximum(s - 1, 0), 0)

    def halo_map(col):
        return lambda s: (jnp.maximum(jnp.minimum(s, n_steps - 1) * hb - 1, 0), col)

    const = lambda s: (0, 0)
    return pl.pallas_call(
        functools.partial(_mlstm_out_kernel, n_chunks=nc, n_steps=n_steps),
        grid=(n_steps + 1,),
        in_specs=[pl.BlockSpec((L_M, qk_w), cur(C_QM // qk_w)),
                  pl.BlockSpec((L_M, qk_w), cur(C_KM // qk_w)),
                  pl.BlockSpec((HALO, qk_w), halo_map(C_QM // qk_w)),
                  pl.BlockSpec((HALO, qk_w), halo_map(C_KM // qk_w)),
                  pl.BlockSpec((L_M, W_M), cur(C_VM // W_M)),
                  pl.BlockSpec((L_M, W_M), cur(C_OM // W_M)),
                  pl.BlockSpec((L_M, W_M), cur(C_ZM // W_M)),
                  pl.BlockSpec((L_M, LANES), cur(D_C // LANES)),
                  pl.BlockSpec((LANES, L_M), lambda s: (0, jnp.minimum(s, n_steps - 1))),
                  pl.BlockSpec((CONV_W, 2 * qk_w), const),
                  pl.BlockSpec((1, 2 * qk_w), const),
                  pl.BlockSpec((1, LANES), const),
                  pl.BlockSpec((LANES, 1), const),
                  pl.BlockSpec((1, W_M), const),
                  pl.BlockSpec((L_M, W_A), prev),
                  pl.BlockSpec((L_M, D_MODEL), prev),
                  pl.BlockSpec((W_A + W_M, D_MODEL), const),
                  pl.BlockSpec((1, D_MODEL), const),
                  pl.BlockSpec((1, D_MODEL), const)],
        out_specs=pl.BlockSpec((L_M, D_MODEL), prev),
        out_shape=jax.ShapeDtypeStruct((B * T, D_MODEL), F32),
        scratch_shapes=[pltpu.VMEM((H_M, DK_M, DV_M + LANES), F32),
                        pltpu.VMEM((H_M, 1, 1), F32),
                        pltpu.VMEM((L_M, W_M), BF16),
                        pltpu.VMEM((L_M, W_M), BF16),
                        pltpu.VMEM((L_M, D_MODEL), F32)],
        compiler_params=pltpu.CompilerParams(
            dimension_semantics=("arbitrary",), vmem_limit_bytes=VMEM_LIMIT),
        name="mlstm_out",
    )(main, main, main, main, main, main, main, small, gate_t,
      conv_w, conv_b, gb_row, gb_col, norm_g, ya, x2d, w_out, ln_g, ln_b)


_W_IN_SEGS = (("q_a", W_A), ("c_kv", D_C), ("z_a", W_A), ("q_i", H_IDX * D_IDX), ("k_i", D_IDX),
              ("w_i", H_IDX), ("q_m", H_M * DK_M), ("k_m", H_M * DK_M), ("v_m", W_M), ("i_m", H_M),
              ("f_m", H_M), ("o_m", W_M), ("z_m", W_M))
_MAIN_ORDER = ("q_a", "z_a", "q_i", "q_m", "k_m", "v_m", "o_m", "z_m")
_SEG_NAMES = [name for name, _ in _W_IN_SEGS]
assert _SEG_NAMES.index("f_m") == _SEG_NAMES.index("i_m") + 1 and S_FM == S_IM + H_M


def _repack_kernel(wt_ref, main_ref, small_ref):
    src, off = {}, 0
    for name, width in _W_IN_SEGS:
        src[name] = (off, width)
        off += width
    dst = 0
    for name in _MAIN_ORDER:
        lo, width = src[name]
        main_ref[dst:dst + width, :] = wt_ref[lo:lo + width, :].astype(BF16)
        dst += width
    parts = [wt_ref[src[name][0]:src[name][0] + src[name][1], :] for name in ("c_kv", "k_i", "w_i")]
    lo = src["i_m"][0]
    parts.append(wt_ref[lo:lo + 2 * H_M, :])
    used = sum(p.shape[0] for p in parts)
    parts.append(jnp.zeros((N_SMALL - used, wt_ref.shape[1]), F32))
    small_ref[...] = jnp.concatenate(parts, axis=0).astype(BF16)


def _repack_w_in(w_in, tc=256):
    n_cols = sum(width for _, width in _W_IN_SEGS)
    wt = jnp.swapaxes(w_in, 1, 2)[0]
    return pl.pallas_call(
        _repack_kernel,
        grid=(D_MODEL // tc,),
        in_specs=[pl.BlockSpec((n_cols, tc), lambda i: (0, i))],
        out_specs=[pl.BlockSpec((N_MAIN, tc), lambda i: (0, i)),
                   pl.BlockSpec((N_SMALL, tc), lambda i: (0, i))],
        out_shape=[jax.ShapeDtypeStruct((N_MAIN, D_MODEL), BF16),
                   jax.ShapeDtypeStruct((N_SMALL, D_MODEL), BF16)],
        compiler_params=pltpu.CompilerParams(
            dimension_semantics=("arbitrary",), vmem_limit_bytes=VMEM_LIMIT),
        name="repack",
    )(wt)


def kernel(x, w_in, b_igate, b_fgate, kv_norm_g, w_uk, w_uv, idx_k_ln_g, idx_k_ln_b, rel_bias,
           conv_w, conv_b, mh_norm_g, w_out, ln_g, ln_b):
    B, T, D = x.shape
    assert D == D_MODEL and T % L_M == 0 and T % (2 * KB) == 0 and w_in.shape[0] == 1
    bias = _bias_tiles(rel_bias)
    x2d = x.reshape(B * T, D)
    w_main, w_small = _repack_w_in(w_in)
    main, small = _proj(x2d, w_main, w_small)
    ckv_n, ckv_t, kidx_n, gate_t = _prep(small, kv_norm_g[0][None], idx_k_ln_g[0][None], idx_k_ln_b[0][None])
    w_uk_t = jnp.transpose(w_uk[0], (0, 2, 1)).astype(BF16)
    w_uv_t = jnp.transpose(w_uv[0], (0, 2, 1)).astype(BF16)
    ya = _dsa(main, gate_t, ckv_n, ckv_t, kidx_n, w_uk_t, w_uv_t, bias, B, T)
    gb = jnp.zeros((LANES,), F32).at[S_IM:S_IM + H_M].set(b_igate[0]).at[S_FM:S_FM + H_M].set(b_fgate[0])
    out = _mlstm_out(main, small, gate_t, conv_w[0], conv_b[0][None], gb[None, :], gb[:, None],
                     mh_norm_g[0][None], ya, x2d, w_out[0].astype(BF16), ln_g[0][None], ln_b[0][None], B, T)
    return out.reshape(B, T, D)
```

```python
import functools
import math

import numpy as np
import jax
import jax.numpy as jnp
from jax import lax
from jax.experimental import pallas as pl
from jax.experimental.pallas import tpu as pltpu

F32 = jnp.float32
BF16 = jnp.bfloat16

D_MODEL = 2048
W_A = 1024
DH_A = 128
H_A = 8
D_C = 256
H_IDX = 16
D_IDX = 64
TOPK = 256
W_M = 1024
H_M = 4
DV_M = 256
DK_M = 128
CONV_W = 4
N_BUCKETS = 32
MAX_DIST = 128
ALPHA = 2.0 ** 0.25
LN_EPS = 1e-5

LANES = 128
SUBLANES = 8
VMEM_LIMIT = 56 * 1024 * 1024

QB = 256
KB = 256
KI = 128
CNT_ROWS = 256
CNT_ACC = 4 * SUBLANES
L_M = 256
HALO = 16
NEG = -1e30
LOG2E = math.log2(math.e)
ONES_ROWS = 16

C_QA, C_ZA, C_QI, C_QM, C_KM, C_VM, C_OM, C_ZM = 0, 1024, 2048, 3072, 3584, 4096, 5120, 6144
N_MAIN = 7168
N_SMALL = 384
S_KI, S_WI, S_IM, S_FM = 0, 64, 80, 84


def _t5_bucket_np(rel):
    max_exact = N_BUCKETS // 2
    n = np.maximum(rel, 0)
    nf = np.maximum(n, 1).astype(np.float32)
    large = max_exact + (np.log(nf / np.float32(max_exact)) / np.float32(math.log(MAX_DIST / max_exact))
                         * np.float32(N_BUCKETS - max_exact)).astype(np.int32)
    large = np.minimum(large, N_BUCKETS - 1)
    return np.where(n < max_exact, n, large).astype(np.int32)


FAR_BUCKET = int(_t5_bucket_np(np.array(2 * KB + 1)))


def _bucket_tiles():
    i = np.arange(QB)[None, :]
    j = np.arange(KB)[:, None]
    t0 = _t5_bucket_np(i - j)
    t1 = _t5_bucket_np(i - j + KB)
    assert (t5 := _t5_bucket_np(np.arange(KB + 1, 4096))).min() == t5.max() == FAR_BUCKET
    return np.stack([t0, t1]).astype(np.int32)


def _bias_kernel(bucket_ref, rb_ref, out_ref):
    h = pl.program_id(0)
    far = rb_ref[FAR_BUCKET, h]
    for k in range(2):
        bk = bucket_ref[k]
        acc = jnp.zeros((KB, QB), F32)
        for b in range(N_BUCKETS):
            acc = jnp.where(bk == b, rb_ref[b, h] - far, acc)
        out_ref[0, k] = acc * LOG2E


def _bias_tiles(rel_bias):
    bucket = jnp.asarray(_bucket_tiles())
    return pl.pallas_call(
        _bias_kernel,
        grid=(H_A,),
        in_specs=[pl.BlockSpec((2, KB, QB), lambda h: (0, 0, 0)),
                  pl.BlockSpec(memory_space=pltpu.SMEM)],
        out_specs=pl.BlockSpec((1, 2, KB, QB), lambda h: (h, 0, 0, 0)),
        out_shape=jax.ShapeDtypeStruct((H_A, 2, KB, QB), F32),
        name="bias_tiles",
    )(bucket, rel_bias)


_NT = (((1,), (1,)), ((), ()))


def _proj_kernel(x_ref, w_ref, ws_ref, o_ref, os_ref, xb_ref):
    @pl.when(pl.program_id(1) == 0)
    def _():
        xb_ref[...] = x_ref[...].astype(BF16)
        os_ref[...] = lax.dot_general(xb_ref[...], ws_ref[...], _NT, preferred_element_type=F32)

    o_ref[...] = lax.dot_general(xb_ref[...], w_ref[...], _NT, preferred_element_type=F32).astype(BF16)


def _proj(x2d, w_main, w_small, tm=1024, tn=1792):
    M = x2d.shape[0]
    return pl.pallas_call(
        _proj_kernel,
        grid=(M // tm, N_MAIN // tn),
        in_specs=[pl.BlockSpec((tm, D_MODEL), lambda i, j: (i, 0)),
                  pl.BlockSpec((tn, D_MODEL), lambda i, j: (j, 0)),
                  pl.BlockSpec((N_SMALL, D_MODEL), lambda i, j: (0, 0))],
        out_specs=[pl.BlockSpec((tm, tn), lambda i, j: (i, j)),
                   pl.BlockSpec((tm, N_SMALL), lambda i, j: (i, 0))],
        out_shape=[jax.ShapeDtypeStruct((M, N_MAIN), BF16),
                   jax.ShapeDtypeStruct((M, N_SMALL), F32)],
        scratch_shapes=[pltpu.VMEM((tm, D_MODEL), BF16)],
        compiler_params=pltpu.CompilerParams(
            dimension_semantics=("arbitrary", "arbitrary"), vmem_limit_bytes=VMEM_LIMIT),
        name="proj",
    )(x2d, w_main, w_small)


def _prep_kernel(s_ref, kvg_ref, ig_ref, ib_ref, ckv_ref, ckvt_ref, kidx_ref, gt_ref):
    c = s_ref[:, 0:D_C]
    c = c * lax.rsqrt(jnp.mean(c * c, axis=-1, keepdims=True) + LN_EPS) * kvg_ref[...]
    ckv_ref[...] = c.astype(BF16)
    for r in range(ckvt_ref.shape[0]):
        ckvt_ref[r, 0:D_C, :] = c[r * KB:(r + 1) * KB, :].T.astype(BF16)
        ckvt_ref[r, D_C:D_C + ONES_ROWS, :] = jnp.ones((ONES_ROWS, KB), BF16)
    tile = s_ref[:, D_C:D_C + LANES]
    k = tile[:, S_KI:S_KI + D_IDX]
    mu = jnp.mean(k, axis=-1, keepdims=True)
    var = jnp.mean(jnp.square(k - mu), axis=-1, keepdims=True)
    kidx_ref[...] = ((k - mu) * lax.rsqrt(var + LN_EPS) * ig_ref[...] + ib_ref[...]).astype(BF16)
    gt_ref[...] = tile.T


def _prep(small, kv_g, idx_g, idx_b, tm=2048):
    M = small.shape[0]
    return pl.pallas_call(
        _prep_kernel,
        grid=(M // tm,),
        in_specs=[pl.BlockSpec((tm, N_SMALL), lambda i: (i, 0)),
                  pl.BlockSpec((1, D_C), lambda i: (0, 0)),
                  pl.BlockSpec((1, D_IDX), lambda i: (0, 0)),
                  pl.BlockSpec((1, D_IDX), lambda i: (0, 0))],
        out_specs=[pl.BlockSpec((tm, D_C), lambda i: (i, 0)),
                   pl.BlockSpec((tm // KB, D_C + ONES_ROWS, KB), lambda i: (i, 0, 0)),
                   pl.BlockSpec((tm, D_IDX), lambda i: (i, 0)),
                   pl.BlockSpec((LANES, tm), lambda i: (0, i))],
        out_shape=[jax.ShapeDtypeStruct((M, D_C), BF16),
                   jax.ShapeDtypeStruct((M // KB, D_C + ONES_ROWS, KB), BF16),
                   jax.ShapeDtypeStruct((M, D_IDX), BF16),
                   jax.ShapeDtypeStruct((LANES, M), F32)],
        name="prep",
    )(small, kv_g, idx_g, idx_b)


def _key_to_float(key):
    bits = jnp.where(key < 0, key ^ jnp.int32(0x7FFFFFFF), key)
    return lax.bitcast_convert_type(bits, F32)


def _dsa_kernel(qa_ref, za_ref, qi_ref, gt_ref, ckv_ref, ckvt_ref, kidx_ref, wukt_ref, wuvt_ref, bias_ref,
                y_ref, qall_ref, qr_ref, ha_ref, hb_ref, sc_ref, mb_ref, sa_ref, sb_ref, pa_ref, pb_ref,
                m_ref, mk_ref, al_ref, acc_ref, w_ref, thr_ref, cut_ref, cge_ref):
    qi = pl.program_id(1)
    nkb = qi + 1
    last_kb = mb_ref.shape[0] - 1

    for h in range(H_A):
        ql = lax.dot_general(wukt_ref[h], qa_ref[:, h * DH_A:(h + 1) * DH_A], (((1,), (1,)), ((), ())),
                             preferred_element_type=F32)
        qall_ref[h // (H_A // 2), :, (h % (H_A // 2)) * QB:(h % (H_A // 2) + 1) * QB] = (
            ql * (DH_A ** -0.5 * LOG2E)).astype(BF16)


    @pl.when(qi * QB < TOPK)
    def _():
        krow = lax.broadcasted_iota(jnp.int32, (KB, QB), 0)
        qcol = lax.broadcasted_iota(jnp.int32, (KB, QB), 1)
        mb_ref[0] = jnp.where(krow <= qcol, 0.0, NEG).astype(F32)

    @pl.when(qi * QB >= TOPK)
    def _():
        for h in range(H_IDX):
            qr_ref[h * QB:(h + 1) * QB, :] = qi_ref[:, h * D_IDX:(h + 1) * D_IDX]
        w_ref[...] = gt_ref[S_WI:S_WI + H_IDX, :] * ((D_IDX ** -0.5) * (H_IDX ** -0.5))
        krow = lax.broadcasted_iota(jnp.int32, (KI, LANES), 0)
        qcol = lax.broadcasted_iota(jnp.int32, (KI, LANES), 1)
        n_ki = nkb * (KB // KI)
        last_ki = sc_ref.shape[0] // KI - 1

        def head_dots(ki, dst_ref):
            k = kidx_ref[pl.ds(pl.multiple_of(jnp.minimum(ki, last_ki) * KI, KI), KI), :]
            dst_ref[...] = lax.dot_general(k, qr_ref[...], (((1,), (1,)), ((), ())),
                                           preferred_element_type=F32)

        def reduce_heads(src_ref, ki):
            for g in range(QB // LANES):
                lanes = slice(g * LANES, (g + 1) * LANES)
                acc = jnp.zeros((KI, LANES), F32)
                for h in range(H_IDX):
                    acc = acc + (jnp.maximum(src_ref[:, h * QB + g * LANES:h * QB + (g + 1) * LANES], 0.0)
                                 * w_ref[h:h + 1, lanes])
                sc_ref[pl.ds(pl.multiple_of(ki * KI, KI), KI), lanes] = jnp.where(
                    krow + (ki * KI - qi * QB - g * LANES) <= qcol, acc, -jnp.inf)

        assert CNT_ROWS == KB

        head_dots(0, ha_ref)

        def sc_body(j, carry):
            head_dots(2 * j + 1, hb_ref)
            reduce_heads(ha_ref, 2 * j)
            head_dots(2 * j + 2, ha_ref)
            reduce_heads(hb_ref, 2 * j + 1)
            return carry
        lax.fori_loop(0, n_ki // 2, sc_body, 0)

        n_cnt = (nkb * KB + CNT_ROWS - 1) // CNT_ROWS

        def count_where(pred, steps=None):
            def body(c, acc):
                parts = []
                for g in range(QB // LANES):
                    lanes = slice(g * LANES, (g + 1) * LANES)
                    rows = CNT_ROWS
                    if steps is None:
                        blk = sc_ref[pl.ds(pl.multiple_of(c * CNT_ROWS, CNT_ROWS), CNT_ROWS), lanes]
                    else:
                        if c == steps - 1:
                            rows = min(CNT_ROWS, (g + 1) * LANES)
                        blk = sc_ref[c * CNT_ROWS:c * CNT_ROWS + rows, lanes]
                    hit = jnp.where(pred(blk, c * CNT_ROWS, lanes), 1.0, 0.0).astype(F32)
                    parts.append(jnp.sum(hit.reshape(rows // CNT_ACC, CNT_ACC, LANES), axis=0))
                return acc + jnp.concatenate(parts, axis=1)
            acc = jnp.zeros((CNT_ACC, QB), F32)
            if steps is None:
                acc = lax.fori_loop(0, n_cnt, body, acc)
            else:
                for c in range(steps):
                    acc = body(c, acc)
            return jnp.sum(acc, axis=0, keepdims=True)

        def search(steps):
            n_g = QB // LANES

            def count_ge(g, cand):
                lanes = slice(g * LANES, (g + 1) * LANES)
                acc = jnp.zeros((CNT_ACC, LANES), F32)
                for c in range(steps):
                    rows = CNT_ROWS if c < steps - 1 else min(CNT_ROWS, (g + 1) * LANES)
                    hit = jnp.where(sc_ref[c * CNT_ROWS:c * CNT_ROWS + rows, lanes] >= cand, 1.0, 0.0)
                    acc = acc + jnp.sum(hit.astype(F32).reshape(rows // CNT_ACC, CNT_ACC, LANES), axis=0)
                return jnp.sum(acc, axis=0, keepdims=True)

            def bit_body(i, carry):
                new = []
                for g in range(n_g):
                    u, c_ge = carry[g]
                    trial = u | lax.shift_left(jnp.int32(1), 31 - i)
                    cnt = count_ge(g, _key_to_float(trial ^ jnp.int32(-2 ** 31)))
                    ok = cnt >= float(TOPK)
                    new.append((jnp.where(ok, trial, u), jnp.where(ok, cnt, c_ge)))
                return tuple(new)
            init = tuple((jnp.zeros((1, LANES), jnp.int32),
                          jnp.full((1, LANES), float(sc_ref.shape[0]), F32)) for _ in range(n_g))
            for g, (u, c_ge) in enumerate(lax.fori_loop(0, 32, bit_body, init)):
                lanes = slice(g * LANES, (g + 1) * LANES)
                thr_ref[:, lanes] = _key_to_float(u ^ jnp.int32(-2 ** 31))
                cge_ref[:, lanes] = c_ge

        for steps in range(1, sc_ref.shape[0] // CNT_ROWS + 1):
            pl.when(n_cnt == steps)(functools.partial(search, steps))
        has_ties = jnp.max(cge_ref[...]) > float(TOPK)

        @pl.when(jnp.logical_not(has_ties))
        def _():
            def mb_body(kb, carry):
                blk = sc_ref[pl.ds(pl.multiple_of(kb * KB, KB), KB), :]
                mb_ref[kb] = jnp.where(blk >= thr_ref[...], 0.0, NEG).astype(F32)
                return carry
            lax.fori_loop(0, nkb, mb_body, 0)

        @pl.when(has_ties)
        def _():
            c_gt = count_where(lambda blk, row0, lanes: blk > thr_ref[:, lanes])
            need = float(TOPK) - c_gt
            rows = lax.broadcasted_iota(jnp.int32, (CNT_ROWS, LANES), 0)
            n_bits = (sc_ref.shape[0] - 1).bit_length()

            def idx_body(i, cut):
                trial = cut | lax.shift_left(jnp.int32(1), n_bits - 1 - i)
                cut_ref[...] = trial
                before = count_where(lambda blk, row0, lanes: (blk == thr_ref[:, lanes])
                                     & (rows + row0 < cut_ref[:, lanes]))
                return jnp.where(before < need, trial, cut)
            cut_ref[...] = lax.fori_loop(0, n_bits, idx_body, jnp.zeros((1, QB), jnp.int32))

            def mb_body(kb, carry):
                for g in range(QB // LANES):
                    lanes = slice(g * LANES, (g + 1) * LANES)
                    blk = sc_ref[pl.ds(pl.multiple_of(kb * KB, KB), KB), lanes]
                    thr = thr_ref[:, lanes]
                    keep = (blk > thr) | ((blk == thr) & (rows[0:KB] + kb * KB <= cut_ref[:, lanes]))
                    mb_ref[kb, :, lanes] = jnp.where(keep, 0.0, NEG).astype(F32)
                return carry
            lax.fori_loop(0, nkb, mb_body, 0)

    m_ref[...] = jnp.full(m_ref.shape, NEG, F32)
    acc_ref[...] = jnp.zeros(acc_ref.shape, F32)

    hh = H_A // 2
    s_refs, p_refs = (sa_ref, sb_ref), (pa_ref, pb_ref)
    pb_ref[...] = jnp.zeros(pb_ref.shape, BF16)
    al_ref[...] = jnp.ones(al_ref.shape, F32)

    n_far = jnp.maximum(qi - 1, 0)

    def logits(kb, half, bias_mode):
        kbc = jnp.minimum(kb, qi)
        kv = ckv_ref[pl.ds(pl.multiple_of(kbc * KB, KB), KB), :]
        res = jnp.dot(kv, qall_ref[half], preferred_element_type=F32)
        tile = jnp.clip(qi - kbc, 0, 1)
        for u in range(res.shape[1] // LANES):
            j, g = divmod(u, QB // LANES)
            lanes = slice(g * LANES, (g + 1) * LANES)
            x = res[:, u * LANES:(u + 1) * LANES] + mb_ref[kbc, :, lanes]
            if bias_mode != "none":
                b = bias_ref[half * hh + j, tile, :, lanes]
                if bias_mode == "near_only":
                    b = jnp.where(kb >= n_far, b, 0.0)
                x = x + b
            s_refs[half][u] = x
            m_blk = jnp.max(x.reshape(KB // CNT_ACC, CNT_ACC, LANES), axis=0)
            mk_ref[half, :, u * LANES:(u + 1) * LANES] = jnp.max(m_blk, axis=0, keepdims=True)

    def softmax(half):
        for u in range(hh * QB // LANES):
            cols = slice(u * LANES, (u + 1) * LANES)
            m_prev = m_ref[half, :, cols]
            m_new = jnp.maximum(m_prev, mk_ref[half, :, cols])
            al_ref[half, :, cols] = jnp.exp2(m_prev - m_new)
            m_ref[half, :, cols] = m_new
            p_refs[half][:, cols] = jnp.exp2(s_refs[half][u] - m_new).astype(BF16)

    def accumulate(kb, half):
        acc_ref[half] = acc_ref[half] * al_ref[half] + jnp.dot(
            ckvt_ref[jnp.maximum(kb, 0)], p_refs[half][...], preferred_element_type=F32)

    def sweep(first_kb, end_kb, with_bias):
        def body(kb, carry):
            logits(kb, 1, "always" if with_bias else "none")
            softmax(0)
            accumulate(kb - 1, 1)
            logits(kb + 1, 0, "always" if with_bias else "near_only")
            softmax(1)
            accumulate(kb, 0)
            return carry
        lax.fori_loop(first_kb, end_kb, body, 0)

    logits(0, 0, "near_only")
    sweep(0, n_far, False)
    sweep(n_far, nkb, True)
    accumulate(nkb - 1, 1)

    for h in range(H_A):
        half, cols = h // hh, slice((h % hh) * QB, (h % hh + 1) * QB)
        ya_t = jnp.dot(wuvt_ref[h], acc_ref[half, 0:D_C, cols].astype(BF16), preferred_element_type=F32)
        ya = (ya_t / acc_ref[half, D_C:D_C + 1, cols]).T
        z = za_ref[:, h * DH_A:(h + 1) * DH_A].astype(F32)
        y_ref[:, h * DH_A:(h + 1) * DH_A] = (ya * (z * jax.nn.sigmoid(z))).astype(BF16)


def _dsa(main, gate_t, ckv_n, ckv_t, kidx_n, w_uk_t, w_uv_t, bias, B, T):
    nq = T // QB
    return pl.pallas_call(
        _dsa_kernel,
        grid=(B, nq),
        in_specs=[pl.BlockSpec((QB, W_A), lambda b, q: (b * nq + q, C_QA // W_A)),
                  pl.BlockSpec((QB, W_A), lambda b, q: (b * nq + q, C_ZA // W_A)),
                  pl.BlockSpec((QB, H_IDX * D_IDX), lambda b, q: (b * nq + q, C_QI // (H_IDX * D_IDX))),
                  pl.BlockSpec((LANES, QB), lambda b, q: (0, b * nq + q)),
                  pl.BlockSpec((T, D_C), lambda b, q: (b, 0)),
                  pl.BlockSpec((T // KB, D_C + ONES_ROWS, KB), lambda b, q: (b, 0, 0)),
                  pl.BlockSpec((T, D_IDX), lambda b, q: (b, 0)),
                  pl.BlockSpec((H_A, D_C, DH_A), lambda b, q: (0, 0, 0)),
                  pl.BlockSpec((H_A, DH_A, D_C), lambda b, q: (0, 0, 0)),
                  pl.BlockSpec((H_A, 2, KB, QB), lambda b, q: (0, 0, 0, 0))],
        out_specs=pl.BlockSpec((QB, W_A), lambda b, q: (b * nq + q, 0)),
        out_shape=jax.ShapeDtypeStruct((B * T, W_A), BF16),
        scratch_shapes=[pltpu.VMEM((2, D_C, H_A // 2 * QB), BF16),
                        pltpu.VMEM((H_IDX * QB, D_IDX), BF16),
                        pltpu.VMEM((KI, H_IDX * QB), F32),
                        pltpu.VMEM((KI, H_IDX * QB), F32),
                        pltpu.VMEM((T, QB), F32),
                        pltpu.VMEM((T // KB, KB, QB), F32),
                        pltpu.VMEM((H_A // 2 * QB // LANES, KB, LANES), F32),
                        pltpu.VMEM((H_A // 2 * QB // LANES, KB, LANES), F32),
                        pltpu.VMEM((KB, H_A // 2 * QB), BF16),
                        pltpu.VMEM((KB, H_A // 2 * QB), BF16),
                        pltpu.VMEM((2, 1, H_A // 2 * QB), F32),
                        pltpu.VMEM((2, 1, H_A // 2 * QB), F32),
                        pltpu.VMEM((2, 1, H_A // 2 * QB), F32),
                        pltpu.VMEM((2, D_C + ONES_ROWS, H_A // 2 * QB), F32),
                        pltpu.VMEM((H_IDX, QB), F32),
                        pltpu.VMEM((1, QB), F32),
                        pltpu.VMEM((1, QB), jnp.int32),
                        pltpu.VMEM((1, QB), F32)],
        compiler_params=pltpu.CompilerParams(
            dimension_semantics=("arbitrary", "arbitrary"), vmem_limit_bytes=VMEM_LIMIT),
        name="dsa",
    )(main, main, main, gate_t, ckv_n, ckv_t, kidx_n, w_uk_t, w_uv_t, bias)


def _split_dot(tri, x):
    hi = x.astype(BF16)
    lo = (x - hi.astype(F32)).astype(BF16)
    return jnp.dot(tri, hi, preferred_element_type=F32) + jnp.dot(tri, lo, preferred_element_type=F32)


def _log_sigmoid(x):
    return jnp.minimum(x, 0.0) - jnp.log1p(jnp.exp(-jnp.abs(x)))


def _mlstm_out_kernel(q_ref, k_ref, qh_ref, kh_ref, v_ref, o_ref, z_ref, g_ref, gt_ref,
                      cw_ref, cb_ref, gbr_ref, gbc_ref, ng_ref, ya_ref, x_ref, w_ref, lg_ref, lb_ref,
                      out_ref, ct_ref, m_ref, y_ref, yp_ref, p_ref, *, n_chunks, n_steps):
    step = pl.program_id(0)
    c = jnp.minimum(step, n_steps - 1) % n_chunks
    L = L_M

    @pl.when(step == 0)
    def _():
        yp_ref[...] = jnp.zeros(yp_ref.shape, BF16)

    @pl.when(c == 0)
    def _():
        ct_ref[...] = jnp.zeros(ct_ref.shape, F32)
        m_ref[...] = jnp.zeros(m_ref.shape, F32)

    n_pc = 2 * H_M
    pw = D_MODEL // n_pc

    def project_chunk(j):
        cols = slice(j * pw, (j + 1) * pw)
        p_ref[:, cols] = (jnp.dot(ya_ref[...], w_ref[0:W_A, cols], preferred_element_type=F32)
                          + jnp.dot(yp_ref[...], w_ref[W_A:W_A + W_M, cols], preferred_element_type=F32))

    def norm_previous():
        res = ALPHA * x_ref[...] + p_ref[...]
        mean = jnp.mean(res, axis=-1, keepdims=True)
        var_r = jnp.mean(jnp.square(res - mean), axis=-1, keepdims=True)
        out_ref[...] = (res - mean) * lax.rsqrt(var_r + LN_EPS) * lg_ref[...] + lb_ref[...]

    pending = [functools.partial(project_chunk, j) for j in range(n_pc)] + [norm_previous]

    def emit_projection_work():
        if pending:
            pending.pop(0)()

    emit_projection_work()

    r = lax.broadcasted_iota(jnp.int32, (L, L), 0)
    s = lax.broadcasted_iota(jnp.int32, (L, L), 1)
    causal = s <= r
    shifts = [jnp.where(r - s == d, 1.0, 0.0).astype(BF16) for d in range(1, CONV_W)]

    def conv_silu(x_ref, halo_ref, lo):
        x = x_ref[...]
        halo = jnp.where(c > 0, halo_ref[...].astype(F32), 0.0)
        w = cw_ref[:, lo:lo + H_M * DK_M]
        y = cb_ref[:, lo:lo + H_M * DK_M] + w[CONV_W - 1:CONV_W] * x.astype(F32)
        top = jnp.zeros((SUBLANES, H_M * DK_M), F32)
        for d in range(1, CONV_W):
            wd = w[CONV_W - 1 - d:CONV_W - d]
            y = y + wd * jnp.dot(shifts[d - 1], x, preferred_element_type=F32)
            top = top + wd * jnp.concatenate(
                [halo[HALO - d:HALO], jnp.zeros((SUBLANES - d, H_M * DK_M), F32)], axis=0)
        y = jnp.concatenate([y[0:SUBLANES] + top, y[SUBLANES:]], axis=0)
        return y * jax.nn.sigmoid(y)

    q_all = conv_silu(q_ref, qh_ref, 0)
    emit_projection_work()
    k_all = conv_silu(k_ref, kh_ref, H_M * DK_M) * (DK_M ** -0.5)
    emit_projection_work()

    gc = g_ref[...] + gbr_ref[...]
    gr = gt_ref[S_IM:S_IM + 2 * H_M, :] + gbc_ref[S_IM:S_IM + 2 * H_M, :]
    tri_l = jnp.where(causal, 1.0, 0.0).astype(BF16)
    tri_u = jnp.where(r <= s, 1.0, 0.0).astype(BF16)
    b_cols = _split_dot(tri_l, _log_sigmoid(gc) * LOG2E)
    lf_rows = _log_sigmoid(gr) * LOG2E
    b_rows = jnp.dot(lf_rows.astype(BF16), tri_u, preferred_element_type=F32) \
        + jnp.dot((lf_rows - lf_rows.astype(BF16).astype(F32)).astype(BF16), tri_u,
                  preferred_element_type=F32)
    gc = gc * LOG2E
    gr = gr * LOG2E
    ones = jnp.ones((L, LANES), BF16)
    emit_projection_work()

    for h in range(H_M):
        emit_projection_work()
        q = q_all[:, h * DK_M:(h + 1) * DK_M]
        k = k_all[:, h * DK_M:(h + 1) * DK_M]
        v = jnp.concatenate([v_ref[:, h * DV_M:(h + 1) * DV_M], ones], axis=1)
        qb = q.astype(BF16)
        b_c = b_cols[:, S_FM + h:S_FM + h + 1]
        i_c = gc[:, S_IM + h:S_IM + h + 1]
        b_r = b_rows[H_M + h:H_M + h + 1, :]
        i_r = gr[h:h + 1, :]
        m_prev = m_ref[h]
        ct = ct_ref[h]

        log_d = jnp.where(causal, b_c - b_r + i_r, -jnp.inf)
        g = b_c + m_prev
        m_t = jnp.maximum(jnp.max(log_d, axis=-1, keepdims=True), g)
        qk = lax.dot_general(qb, k.astype(BF16), _NT, preferred_element_type=F32)
        s_mat = qk * jnp.exp2(log_d - m_t)
        inter = jnp.exp2(g - m_t)
        num = jnp.dot(s_mat.astype(BF16), v, preferred_element_type=F32) \
            + inter * jnp.dot(qb, ct.astype(BF16), preferred_element_type=F32)
        den = jnp.maximum(jnp.abs(num[:, DV_M:]), jnp.exp2(-m_t))
        hh = num[:, 0:DV_M] / jnp.concatenate([den] * (DV_M // LANES), axis=1)

        emit_projection_work()
        b_last = b_c[L - 1:L, :]
        a_r = b_last - b_r + i_r
        m_new = jnp.maximum(b_last + m_prev, jnp.max(a_r, axis=-1, keepdims=True))
        decay = jnp.exp2(b_last + m_prev - m_new)
        wgt_c = jnp.exp2(b_last - b_c + i_c - m_new)
        kw = k * wgt_c
        ct_ref[h] = decay * ct + jnp.dot(kw.T.astype(BF16), v, preferred_element_type=F32)
        m_ref[h] = m_new

        mu = jnp.mean(hh, axis=-1, keepdims=True)
        var = jnp.mean(jnp.square(hh - mu), axis=-1, keepdims=True)
        hn = (hh - mu) * lax.rsqrt(var + LN_EPS) * ng_ref[:, h * DV_M:(h + 1) * DV_M]
        og = o_ref[:, h * DV_M:(h + 1) * DV_M].astype(F32)
        zg = z_ref[:, h * DV_M:(h + 1) * DV_M].astype(F32)
        y_ref[:, h * DV_M:(h + 1) * DV_M] = (hn * jax.nn.sigmoid(og) * (zg * jax.nn.sigmoid(zg))).astype(BF16)

    assert not pending
    yp_ref[...] = y_ref[...]


def _mlstm_out(main, small, gate_t, conv_w, conv_b, gb_row, gb_col, norm_g, ya, x2d, w_out, ln_g, ln_b, B, T):
    nc = T // L_M
    hb = L_M // HALO
    qk_w = H_M * DK_M
    n_steps = B * nc

    def cur(col):
        return lambda s: (jnp.minimum(s, n_steps - 1), col)

    def prev(s):
        return (jnp.maximum(s - 1, 0), 0)

    def halo_map(col):
        return lambda s: (jnp.maximum(jnp.minimum(s, n_steps - 1) * hb - 1, 0), col)

    const = lambda s: (0, 0)
    return pl.pallas_call(
        functools.partial(_mlstm_out_kernel, n_chunks=nc, n_steps=n_steps),
        grid=(n_steps + 1,),
        in_specs=[pl.BlockSpec((L_M, qk_w), cur(C_QM // qk_w)),
                  pl.BlockSpec((L_M, qk_w), cur(C_KM // qk_w)),
                  pl.BlockSpec((HALO, qk_w), halo_map(C_QM // qk_w)),
                  pl.BlockSpec((HALO, qk_w), halo_map(C_KM // qk_w)),
                  pl.BlockSpec((L_M, W_M), cur(C_VM // W_M)),
                  pl.BlockSpec((L_M, W_M), cur(C_OM // W_M)),
                  pl.BlockSpec((L_M, W_M), cur(C_ZM // W_M)),
                  pl.BlockSpec((L_M, LANES), cur(D_C // LANES)),
                  pl.BlockSpec((LANES, L_M), lambda s: (0, jnp.minimum(s, n_steps - 1))),
                  pl.BlockSpec((CONV_W, 2 * qk_w), const),
                  pl.BlockSpec((1, 2 * qk_w), const),
                  pl.BlockSpec((1, LANES), const),
                  pl.BlockSpec((LANES, 1), const),
                  pl.BlockSpec((1, W_M), const),
                  pl.BlockSpec((L_M, W_A), prev),
                  pl.BlockSpec((L_M, D_MODEL), prev),
                  pl.BlockSpec((W_A + W_M, D_MODEL), const),
                  pl.BlockSpec((1, D_MODEL), const),
                  pl.BlockSpec((1, D_MODEL), const)],
        out_specs=pl.BlockSpec((L_M, D_MODEL), prev),
        out_shape=jax.ShapeDtypeStruct((B * T, D_MODEL), F32),
        scratch_shapes=[pltpu.VMEM((H_M, DK_M, DV_M + LANES), F32),
                        pltpu.VMEM((H_M, 1, 1), F32),
                        pltpu.VMEM((L_M, W_M), BF16),
                        pltpu.VMEM((L_M, W_M), BF16),
                        pltpu.VMEM((L_M, D_MODEL), F32)],
        compiler_params=pltpu.CompilerParams(
            dimension_semantics=("arbitrary",), vmem_limit_bytes=VMEM_LIMIT),
        name="mlstm_out",
    )(main, main, main, main, main, main, main, small, gate_t,
      conv_w, conv_b, gb_row, gb_col, norm_g, ya, x2d, w_out, ln_g, ln_b)


_W_IN_SEGS = (("q_a", W_A), ("c_kv", D_C), ("z_a", W_A), ("q_i", H_IDX * D_IDX), ("k_i", D_IDX),
              ("w_i", H_IDX), ("q_m", H_M * DK_M), ("k_m", H_M * DK_M), ("v_m", W_M), ("i_m", H_M),
              ("f_m", H_M), ("o_m", W_M), ("z_m", W_M))
_MAIN_ORDER = ("q_a", "z_a", "q_i", "q_m", "k_m", "v_m", "o_m", "z_m")
_SEG_NAMES = [name for name, _ in _W_IN_SEGS]
assert _SEG_NAMES.index("f_m") == _SEG_NAMES.index("i_m") + 1 and S_FM == S_IM + H_M


def _repack_kernel(wt_ref, main_ref, small_ref):
    src, off = {}, 0
    for name, width in _W_IN_SEGS:
        src[name] = (off, width)
        off += width
    dst = 0
    for name in _MAIN_ORDER:
        lo, width = src[name]
        main_ref[dst:dst + width, :] = wt_ref[lo:lo + width, :].astype(BF16)
        dst += width
    parts = [wt_ref[src[name][0]:src[name][0] + src[name][1], :] for name in ("c_kv", "k_i", "w_i")]
    lo = src["i_m"][0]
    parts.append(wt_ref[lo:lo + 2 * H_M, :])
    used = sum(p.shape[0] for p in parts)
    parts.append(jnp.zeros((N_SMALL - used, wt_ref.shape[1]), F32))
    small_ref[...] = jnp.concatenate(parts, axis=0).astype(BF16)


def _repack_w_in(w_in, tc=256):
    n_cols = sum(width for _, width in _W_IN_SEGS)
    wt = jnp.swapaxes(w_in, 1, 2)[0]
    return pl.pallas_call(
        _repack_kernel,
        grid=(D_MODEL // tc,),
        in_specs=[pl.BlockSpec((n_cols, tc), lambda i: (0, i))],
        out_specs=[pl.BlockSpec((N_MAIN, tc), lambda i: (0, i)),
                   pl.BlockSpec((N_SMALL, tc), lambda i: (0, i))],
        out_shape=[jax.ShapeDtypeStruct((N_MAIN, D_MODEL), BF16),
                   jax.ShapeDtypeStruct((N_SMALL, D_MODEL), BF16)],
        compiler_params=pltpu.CompilerParams(
            dimension_semantics=("arbitrary",), vmem_limit_bytes=VMEM_LIMIT),
        name="repack",
    )(wt)


def kernel(x, w_in, b_igate, b_fgate, kv_norm_g, w_uk, w_uv, idx_k_ln_g, idx_k_ln_b, rel_bias,
           conv_w, conv_b, mh_norm_g, w_out, ln_g, ln_b):
    B, T, D = x.shape
    assert D == D_MODEL and T % L_M == 0 and T % (2 * KB) == 0 and w_in.shape[0] == 1
    bias = _bias_tiles(rel_bias)
    x2d = x.reshape(B * T, D)
    w_main, w_small = _repack_w_in(w_in)
    main, small = _proj(x2d, w_main, w_small)
    ckv_n, ckv_t, kidx_n, gate_t = _prep(small, kv_norm_g[0][None], idx_k_ln_g[0][None], idx_k_ln_b[0][None])
    w_uk_t = jnp.transpose(w_uk[0], (0, 2, 1)).astype(BF16)
    w_uv_t = jnp.transpose(w_uv[0], (0, 2, 1)).astype(BF16)
    ya = _dsa(main, gate_t, ckv_n, ckv_t, kidx_n, w_uk_t, w_uv_t, bias, B, T)
    gb = jnp.zeros((LANES,), F32).at[S_IM:S_IM + H_M].set(b_igate[0]).at[S_FM:S_FM + H_M].set(b_fgate[0])
    out = _mlstm_out(main, small, gate_t, conv_w[0], conv_b[0][None], gb[None, :], gb[:, None],
                     mh_norm_g[0][None], ya, x2d, w_out[0].astype(BF16), ln_g[0][None], ln_b[0][None], B, T)
    return out.reshape(B, T, D)
```

```python
import functools
import math

import numpy as np
import jax
import jax.numpy as jnp
from jax import lax
from jax.experimental import pallas as pl
from jax.experimental.pallas import tpu as pltpu

F32 = jnp.float32
BF16 = jnp.bfloat16

D_MODEL = 2048
W_A = 1024
DH_A = 128
H_A = 8
D_C = 256
H_IDX = 16
D_IDX = 64
TOPK = 256
W_M = 1024
H_M = 4
DV_M = 256
DK_M = 128
CONV_W = 4
N_BUCKETS = 32
MAX_DIST = 128
ALPHA = 2.0 ** 0.25
LN_EPS = 1e-5

LANES = 128
SUBLANES = 8
VMEM_LIMIT = 56 * 1024 * 1024

QB = 256
KB = 256
KI = 128
CNT_ROWS = 256
CNT_ACC = 4 * SUBLANES
L_M = 256
HALO = 16
NEG = -1e30
LOG2E = math.log2(math.e)
ONES_ROWS = 16

C_QA, C_ZA, C_QI, C_QM, C_KM, C_VM, C_OM, C_ZM = 0, 1024, 2048, 3072, 3584, 4096, 5120, 6144
N_MAIN = 7168
N_SMALL = 384
S_KI, S_WI, S_IM, S_FM = 0, 64, 80, 84


def _t5_bucket_np(rel):
    max_exact = N_BUCKETS // 2
    n = np.maximum(rel, 0)
    nf = np.maximum(n, 1).astype(np.float32)
    large = max_exact + (np.log(nf / np.float32(max_exact)) / np.float32(math.log(MAX_DIST / max_exact))
                         * np.float32(N_BUCKETS - max_exact)).astype(np.int32)
    large = np.minimum(large, N_BUCKETS - 1)
    return np.where(n < max_exact, n, large).astype(np.int32)


FAR_BUCKET = int(_t5_bucket_np(np.array(2 * KB + 1)))


def _bucket_tiles():
    i = np.arange(QB)[None, :]
    j = np.arange(KB)[:, None]
    t0 = _t5_bucket_np(i - j)
    t1 = _t5_bucket_np(i - j + KB)
    assert (t5 := _t5_bucket_np(np.arange(KB + 1, 4096))).min() == t5.max() == FAR_BUCKET
    return np.stack([t0, t1]).astype(np.int32)


def _bias_kernel(bucket_ref, rb_ref, out_ref):
    h = pl.program_id(0)
    far = rb_ref[FAR_BUCKET, h]
    for k in range(2):
        bk = bucket_ref[k]
        acc = jnp.zeros((KB, QB), F32)
        for b in range(N_BUCKETS):
            acc = jnp.where(bk == b, rb_ref[b, h] - far, acc)
        out_ref[0, k] = acc * LOG2E


def _bias_tiles(rel_bias):
    bucket = jnp.asarray(_bucket_tiles())
    return pl.pallas_call(
        _bias_kernel,
        grid=(H_A,),
        in_specs=[pl.BlockSpec((2, KB, QB), lambda h: (0, 0, 0)),
                  pl.BlockSpec(memory_space=pltpu.SMEM)],
        out_specs=pl.BlockSpec((1, 2, KB, QB), lambda h: (h, 0, 0, 0)),
        out_shape=jax.ShapeDtypeStruct((H_A, 2, KB, QB), F32),
        name="bias_tiles",
    )(bucket, rel_bias)


_NT = (((1,), (1,)), ((), ()))


def _proj_kernel(x0_ref, xn_ref, w_ref, ws_ref, o_ref, os_ref, xb_ref):
    i = pl.program_id(0)
    j = pl.program_id(1)
    cur = i % 2
    tq = xn_ref.shape[0]

    @pl.when((i == 0) & (j == 0))
    def _():
        xb_ref[0] = x0_ref[...].astype(BF16)

    rows = pl.ds(pl.multiple_of(j * tq, tq), tq)
    xb_ref[1 - cur, rows, :] = xn_ref[...].astype(BF16)
    os_ref[rows, :] = lax.dot_general(xb_ref[cur, rows, :], ws_ref[...], _NT, preferred_element_type=F32)
    o_ref[...] = lax.dot_general(xb_ref[cur], w_ref[...], _NT, preferred_element_type=F32).astype(BF16)


def _proj(x2d, w_main, w_small, tm=1024, tn=1792):
    M = x2d.shape[0]
    n_j = N_MAIN // tn
    tq = tm // n_j
    last_slice = M // tq - 1
    return pl.pallas_call(
        _proj_kernel,
        grid=(M // tm, n_j),
        in_specs=[pl.BlockSpec((tm, D_MODEL), lambda i, j: (0, 0), pipeline_mode=pl.Buffered(1)),
                  pl.BlockSpec((tq, D_MODEL), lambda i, j: (jnp.minimum((i + 1) * n_j + j, last_slice), 0)),
                  pl.BlockSpec((tn, D_MODEL), lambda i, j: (j, 0)),
                  pl.BlockSpec((N_SMALL, D_MODEL), lambda i, j: (0, 0))],
        out_specs=[pl.BlockSpec((tm, tn), lambda i, j: (i, j)),
                   pl.BlockSpec((tm, N_SMALL), lambda i, j: (i, 0))],
        out_shape=[jax.ShapeDtypeStruct((M, N_MAIN), BF16),
                   jax.ShapeDtypeStruct((M, N_SMALL), F32)],
        scratch_shapes=[pltpu.VMEM((2, tm, D_MODEL), BF16)],
        compiler_params=pltpu.CompilerParams(
            dimension_semantics=("arbitrary", "arbitrary"), vmem_limit_bytes=VMEM_LIMIT),
        name="proj",
    )(x2d, x2d, w_main, w_small)


def _prep_kernel(s_ref, kvg_ref, ig_ref, ib_ref, ckv_ref, ckvt_ref, kidx_ref, gt_ref):
    c = s_ref[:, 0:D_C]
    c = c * lax.rsqrt(jnp.mean(c * c, axis=-1, keepdims=True) + LN_EPS) * kvg_ref[...]
    ckv_ref[...] = c.astype(BF16)
    for r in range(ckvt_ref.shape[0]):
        ckvt_ref[r, 0:D_C, :] = c[r * KB:(r + 1) * KB, :].T.astype(BF16)
        ckvt_ref[r, D_C:D_C + ONES_ROWS, :] = jnp.ones((ONES_ROWS, KB), BF16)
    tile = s_ref[:, D_C:D_C + LANES]
    k = tile[:, S_KI:S_KI + D_IDX]
    mu = jnp.mean(k, axis=-1, keepdims=True)
    var = jnp.mean(jnp.square(k - mu), axis=-1, keepdims=True)
    kidx_ref[...] = ((k - mu) * lax.rsqrt(var + LN_EPS) * ig_ref[...] + ib_ref[...]).astype(BF16)
    gt_ref[...] = tile.T


def _prep(small, kv_g, idx_g, idx_b, tm=2048):
    M = small.shape[0]
    return pl.pallas_call(
        _prep_kernel,
        grid=(M // tm,),
        in_specs=[pl.BlockSpec((tm, N_SMALL), lambda i: (i, 0)),
                  pl.BlockSpec((1, D_C), lambda i: (0, 0)),
                  pl.BlockSpec((1, D_IDX), lambda i: (0, 0)),
                  pl.BlockSpec((1, D_IDX), lambda i: (0, 0))],
        out_specs=[pl.BlockSpec((tm, D_C), lambda i: (i, 0)),
                   pl.BlockSpec((tm // KB, D_C + ONES_ROWS, KB), lambda i: (i, 0, 0)),
                   pl.BlockSpec((tm, D_IDX), lambda i: (i, 0)),
                   pl.BlockSpec((LANES, tm), lambda i: (0, i))],
        out_shape=[jax.ShapeDtypeStruct((M, D_C), BF16),
                   jax.ShapeDtypeStruct((M // KB, D_C + ONES_ROWS, KB), BF16),
                   jax.ShapeDtypeStruct((M, D_IDX), BF16),
                   jax.ShapeDtypeStruct((LANES, M), F32)],
        name="prep",
    )(small, kv_g, idx_g, idx_b)


def _key_to_float(key):
    bits = jnp.where(key < 0, key ^ jnp.int32(0x7FFFFFFF), key)
    return lax.bitcast_convert_type(bits, F32)


def _dsa_kernel(qa_ref, za_ref, qi_ref, gt_ref, ckv_ref, ckvt_ref, kidx_ref, wukt_ref, wuvt_ref, bias_ref,
                y_ref, qall_ref, qr_ref, ha_ref, hb_ref, sc_ref, mb_ref, sa_ref, sb_ref, pa_ref, pb_ref,
                m_ref, mk_ref, al_ref, acc_ref, w_ref, thr_ref, cut_ref, cge_ref):
    qi = pl.program_id(1)
    nkb = qi + 1
    last_kb = mb_ref.shape[0] - 1

    for h in range(H_A):
        ql = lax.dot_general(wukt_ref[h], qa_ref[:, h * DH_A:(h + 1) * DH_A], (((1,), (1,)), ((), ())),
                             preferred_element_type=F32)
        qall_ref[h // (H_A // 2), :, (h % (H_A // 2)) * QB:(h % (H_A // 2) + 1) * QB] = (
            ql * (DH_A ** -0.5 * LOG2E)).astype(BF16)


    @pl.when(qi * QB < TOPK)
    def _():
        krow = lax.broadcasted_iota(jnp.int32, (KB, QB), 0)
        qcol = lax.broadcasted_iota(jnp.int32, (KB, QB), 1)
        mb_ref[0] = jnp.where(krow <= qcol, 0.0, NEG).astype(F32)

    @pl.when(qi * QB >= TOPK)
    def _():
        for h in range(H_IDX):
            qr_ref[h * QB:(h + 1) * QB, :] = qi_ref[:, h * D_IDX:(h + 1) * D_IDX]
        w_ref[...] = gt_ref[S_WI:S_WI + H_IDX, :] * ((D_IDX ** -0.5) * (H_IDX ** -0.5))
        krow = lax.broadcasted_iota(jnp.int32, (KI, LANES), 0)
        qcol = lax.broadcasted_iota(jnp.int32, (KI, LANES), 1)
        n_ki = nkb * (KB // KI)
        last_ki = sc_ref.shape[0] // KI - 1

        def head_dots(ki, dst_ref):
            k = kidx_ref[pl.ds(pl.multiple_of(jnp.minimum(ki, last_ki) * KI, KI), KI), :]
            dst_ref[...] = lax.dot_general(k, qr_ref[...], (((1,), (1,)), ((), ())),
                                           preferred_element_type=F32)

        def reduce_heads(src_ref, ki):
            for g in range(QB // LANES):
                lanes = slice(g * LANES, (g + 1) * LANES)
                acc = jnp.zeros((KI, LANES), F32)
                for h in range(H_IDX):
                    acc = acc + (jnp.maximum(src_ref[:, h * QB + g * LANES:h * QB + (g + 1) * LANES], 0.0)
                                 * w_ref[h:h + 1, lanes])
                sc_ref[pl.ds(pl.multiple_of(ki * KI, KI), KI), lanes] = jnp.where(
                    krow + (ki * KI - qi * QB - g * LANES) <= qcol, acc, -jnp.inf)

        assert CNT_ROWS == KB

        head_dots(0, ha_ref)

        def sc_body(j, carry):
            head_dots(2 * j + 1, hb_ref)
            reduce_heads(ha_ref, 2 * j)
            head_dots(2 * j + 2, ha_ref)
            reduce_heads(hb_ref, 2 * j + 1)
            return carry
        lax.fori_loop(0, n_ki // 2, sc_body, 0)

        n_cnt = (nkb * KB + CNT_ROWS - 1) // CNT_ROWS

        def count_where(pred, steps=None):
            def body(c, acc):
                parts = []
                for g in range(QB // LANES):
                    lanes = slice(g * LANES, (g + 1) * LANES)
                    rows = CNT_ROWS
                    if steps is None:
                        blk = sc_ref[pl.ds(pl.multiple_of(c * CNT_ROWS, CNT_ROWS), CNT_ROWS), lanes]
                    else:
                        if c == steps - 1:
                            rows = min(CNT_ROWS, (g + 1) * LANES)
                        blk = sc_ref[c * CNT_ROWS:c * CNT_ROWS + rows, lanes]
                    hit = jnp.where(pred(blk, c * CNT_ROWS, lanes), 1.0, 0.0).astype(F32)
                    parts.append(jnp.sum(hit.reshape(rows // CNT_ACC, CNT_ACC, LANES), axis=0))
                return acc + jnp.concatenate(parts, axis=1)
            acc = jnp.zeros((CNT_ACC, QB), F32)
            if steps is None:
                acc = lax.fori_loop(0, n_cnt, body, acc)
            else:
                for c in range(steps):
                    acc = body(c, acc)
            return jnp.sum(acc, axis=0, keepdims=True)

        def search(steps):
            n_g = QB // LANES

            def count_ge(g, cand):
                lanes = slice(g * LANES, (g + 1) * LANES)
                acc = jnp.zeros((CNT_ACC, LANES), F32)
                for c in range(steps):
                    rows = CNT_ROWS if c < steps - 1 else min(CNT_ROWS, (g + 1) * LANES)
                    hit = jnp.where(sc_ref[c * CNT_ROWS:c * CNT_ROWS + rows, lanes] >= cand, 1.0, 0.0)
                    acc = acc + jnp.sum(hit.astype(F32).reshape(rows // CNT_ACC, CNT_ACC, LANES), axis=0)
                return jnp.sum(acc, axis=0, keepdims=True)

            def bit_body(i, carry):
                new = []
                for g in range(n_g):
                    u, c_ge = carry[g]
                    trial = u | lax.shift_left(jnp.int32(1), 31 - i)
                    cnt = count_ge(g, _key_to_float(trial ^ jnp.int32(-2 ** 31)))
                    ok = cnt >= float(TOPK)
                    new.append((jnp.where(ok, trial, u), jnp.where(ok, cnt, c_ge)))
                return tuple(new)
            init = tuple((jnp.zeros((1, LANES), jnp.int32),
                          jnp.full((1, LANES), float(sc_ref.shape[0]), F32)) for _ in range(n_g))
            for g, (u, c_ge) in enumerate(lax.fori_loop(0, 32, bit_body, init)):
                lanes = slice(g * LANES, (g + 1) * LANES)
                thr_ref[:, lanes] = _key_to_float(u ^ jnp.int32(-2 ** 31))
                cge_ref[:, lanes] = c_ge

        for steps in range(1, sc_ref.shape[0] // CNT_ROWS + 1):
            pl.when(n_cnt == steps)(functools.partial(search, steps))
        has_ties = jnp.max(cge_ref[...]) > float(TOPK)

        @pl.when(jnp.logical_not(has_ties))
        def _():
            def mb_body(kb, carry):
                blk = sc_ref[pl.ds(pl.multiple_of(kb * KB, KB), KB), :]
                mb_ref[kb] = jnp.where(blk >= thr_ref[...], 0.0, NEG).astype(F32)
                return carry
            lax.fori_loop(0, nkb, mb_body, 0)

        @pl.when(has_ties)
        def _():
            c_gt = count_where(lambda blk, row0, lanes: blk > thr_ref[:, lanes])
            need = float(TOPK) - c_gt
            rows = lax.broadcasted_iota(jnp.int32, (CNT_ROWS, LANES), 0)
            n_bits = (sc_ref.shape[0] - 1).bit_length()

            def idx_body(i, cut):
                trial = cut | lax.shift_left(jnp.int32(1), n_bits - 1 - i)
                cut_ref[...] = trial
                before = count_where(lambda blk, row0, lanes: (blk == thr_ref[:, lanes])
                                     & (rows + row0 < cut_ref[:, lanes]))
                return jnp.where(before < need, trial, cut)
            cut_ref[...] = lax.fori_loop(0, n_bits, idx_body, jnp.zeros((1, QB), jnp.int32))

            def mb_body(kb, carry):
                for g in range(QB // LANES):
                    lanes = slice(g * LANES, (g + 1) * LANES)
                    blk = sc_ref[pl.ds(pl.multiple_of(kb * KB, KB), KB), lanes]
                    thr = thr_ref[:, lanes]
                    keep = (blk > thr) | ((blk == thr) & (rows[0:KB] + kb * KB <= cut_ref[:, lanes]))
                    mb_ref[kb, :, lanes] = jnp.where(keep, 0.0, NEG).astype(F32)
                return carry
            lax.fori_loop(0, nkb, mb_body, 0)

    m_ref[...] = jnp.full(m_ref.shape, NEG, F32)
    acc_ref[...] = jnp.zeros(acc_ref.shape, F32)

    hh = H_A // 2
    s_refs, p_refs = (sa_ref, sb_ref), (pa_ref, pb_ref)
    pb_ref[...] = jnp.zeros(pb_ref.shape, BF16)
    al_ref[...] = jnp.ones(al_ref.shape, F32)

    n_far = jnp.maximum(qi - 1, 0)

    def logits(kb, half, bias_mode):
        kbc = jnp.minimum(kb, qi)
        kv = ckv_ref[pl.ds(pl.multiple_of(kbc * KB, KB), KB), :]
        res = jnp.dot(kv, qall_ref[half], preferred_element_type=F32)
        tile = jnp.clip(qi - kbc, 0, 1)
        for u in range(res.shape[1] // LANES):
            j, g = divmod(u, QB // LANES)
            lanes = slice(g * LANES, (g + 1) * LANES)
            x = res[:, u * LANES:(u + 1) * LANES] + mb_ref[kbc, :, lanes]
            if bias_mode != "none":
                b = bias_ref[half * hh + j, tile, :, lanes]
                if bias_mode == "near_only":
                    b = jnp.where(kb >= n_far, b, 0.0)
                x = x + b
            s_refs[half][u] = x
            m_blk = jnp.max(x.reshape(KB // CNT_ACC, CNT_ACC, LANES), axis=0)
            mk_ref[half, :, u * LANES:(u + 1) * LANES] = jnp.max(m_blk, axis=0, keepdims=True)

    def softmax(half):
        for u in range(hh * QB // LANES):
            cols = slice(u * LANES, (u + 1) * LANES)
            m_prev = m_ref[half, :, cols]
            m_new = jnp.maximum(m_prev, mk_ref[half, :, cols])
            al_ref[half, :, cols] = jnp.exp2(m_prev - m_new)
            m_ref[half, :, cols] = m_new
            p_refs[half][:, cols] = jnp.exp2(s_refs[half][u] - m_new).astype(BF16)

    def accumulate(kb, half):
        acc_ref[half] = acc_ref[half] * al_ref[half] + jnp.dot(
            ckvt_ref[jnp.maximum(kb, 0)], p_refs[half][...], preferred_element_type=F32)

    def sweep(first_kb, end_kb, with_bias):
        def body(kb, carry):
            logits(kb, 1, "always" if with_bias else "none")
            softmax(0)
            accumulate(kb - 1, 1)
            logits(kb + 1, 0, "always" if with_bias else "near_only")
            softmax(1)
            accumulate(kb, 0)
            return carry
        lax.fori_loop(first_kb, end_kb, body, 0)

    logits(0, 0, "near_only")
    sweep(0, n_far, False)
    sweep(n_far, nkb, True)
    accumulate(nkb - 1, 1)

    for h in range(H_A):
        half, cols = h // hh, slice((h % hh) * QB, (h % hh + 1) * QB)
        ya_t = jnp.dot(wuvt_ref[h], acc_ref[half, 0:D_C, cols].astype(BF16), preferred_element_type=F32)
        ya = (ya_t / acc_ref[half, D_C:D_C + 1, cols]).T
        z = za_ref[:, h * DH_A:(h + 1) * DH_A].astype(F32)
        y_ref[:, h * DH_A:(h + 1) * DH_A] = (ya * (z * jax.nn.sigmoid(z))).astype(BF16)


def _dsa(main, gate_t, ckv_n, ckv_t, kidx_n, w_uk_t, w_uv_t, bias, B, T):
    nq = T // QB
    return pl.pallas_call(
        _dsa_kernel,
        grid=(B, nq),
        in_specs=[pl.BlockSpec((QB, W_A), lambda b, q: (b * nq + q, C_QA // W_A)),
                  pl.BlockSpec((QB, W_A), lambda b, q: (b * nq + q, C_ZA // W_A)),
                  pl.BlockSpec((QB, H_IDX * D_IDX), lambda b, q: (b * nq + q, C_QI // (H_IDX * D_IDX))),
                  pl.BlockSpec((LANES, QB), lambda b, q: (0, b * nq + q)),
                  pl.BlockSpec((T, D_C), lambda b, q: (b, 0)),
                  pl.BlockSpec((T // KB, D_C + ONES_ROWS, KB), lambda b, q: (b, 0, 0)),
                  pl.BlockSpec((T, D_IDX), lambda b, q: (b, 0)),
                  pl.BlockSpec((H_A, D_C, DH_A), lambda b, q: (0, 0, 0)),
                  pl.BlockSpec((H_A, DH_A, D_C), lambda b, q: (0, 0, 0)),
                  pl.BlockSpec((H_A, 2, KB, QB), lambda b, q: (0, 0, 0, 0))],
        out_specs=pl.BlockSpec((QB, W_A), lambda b, q: (b * nq + q, 0)),
        out_shape=jax.ShapeDtypeStruct((B * T, W_A), BF16),
        scratch_shapes=[pltpu.VMEM((2, D_C, H_A // 2 * QB), BF16),
                        pltpu.VMEM((H_IDX * QB, D_IDX), BF16),
                        pltpu.VMEM((KI, H_IDX * QB), F32),
                        pltpu.VMEM((KI, H_IDX * QB), F32),
                        pltpu.VMEM((T, QB), F32),
                        pltpu.VMEM((T // KB, KB, QB), F32),
                        pltpu.VMEM((H_A // 2 * QB // LANES, KB, LANES), F32),
                        pltpu.VMEM((H_A // 2 * QB // LANES, KB, LANES), F32),
                        pltpu.VMEM((KB, H_A // 2 * QB), BF16),
                        pltpu.VMEM((KB, H_A // 2 * QB), BF16),
                        pltpu.VMEM((2, 1, H_A // 2 * QB), F32),
                        pltpu.VMEM((2, 1, H_A // 2 * QB), F32),
                        pltpu.VMEM((2, 1, H_A // 2 * QB), F32),
                        pltpu.VMEM((2, D_C + ONES_ROWS, H_A // 2 * QB), F32),
                        pltpu.VMEM((H_IDX, QB), F32),
                        pltpu.VMEM((1, QB), F32),
                        pltpu.VMEM((1, QB), jnp.int32),
                        pltpu.VMEM((1, QB), F32)],
        compiler_params=pltpu.CompilerParams(
            dimension_semantics=("arbitrary", "arbitrary"), vmem_limit_bytes=VMEM_LIMIT),
        name="dsa",
    )(main, main, main, gate_t, ckv_n, ckv_t, kidx_n, w_uk_t, w_uv_t, bias)


def _split_dot(tri, x):
    hi = x.astype(BF16)
    lo = (x - hi.astype(F32)).astype(BF16)
    return jnp.dot(tri, hi, preferred_element_type=F32) + jnp.dot(tri, lo, preferred_element_type=F32)


def _log_sigmoid(x):
    return jnp.minimum(x, 0.0) - jnp.log1p(jnp.exp(-jnp.abs(x)))


def _mlstm_out_kernel(q_ref, k_ref, qh_ref, kh_ref, v_ref, o_ref, z_ref, g_ref, gt_ref,
                      cw_ref, cb_ref, gbr_ref, gbc_ref, ng_ref, ya_ref, x_ref, w_ref, lg_ref, lb_ref,
                      out_ref, ct_ref, m_ref, y_ref, yp_ref, p_ref, *, n_chunks, n_steps):
    step = pl.program_id(0)
    c = jnp.minimum(step, n_steps - 1) % n_chunks
    L = L_M

    @pl.when(step == 0)
    def _():
        yp_ref[...] = jnp.zeros(yp_ref.shape, BF16)

    @pl.when(c == 0)
    def _():
        ct_ref[...] = jnp.zeros(ct_ref.shape, F32)
        m_ref[...] = jnp.zeros(m_ref.shape, F32)

    n_pc = 2 * H_M
    pw = D_MODEL // n_pc

    def project_chunk(j):
        cols = slice(j * pw, (j + 1) * pw)
        p_ref[:, cols] = (jnp.dot(ya_ref[...], w_ref[0:W_A, cols], preferred_element_type=F32)
                          + jnp.dot(yp_ref[...], w_ref[W_A:W_A + W_M, cols], preferred_element_type=F32))

    def norm_previous():
        res = ALPHA * x_ref[...] + p_ref[...]
        mean = jnp.mean(res, axis=-1, keepdims=True)
        var_r = jnp.mean(jnp.square(res - mean), axis=-1, keepdims=True)
        out_ref[...] = (res - mean) * lax.rsqrt(var_r + LN_EPS) * lg_ref[...] + lb_ref[...]

    pending = [functools.partial(project_chunk, j) for j in range(n_pc)] + [norm_previous]

    def emit_projection_work():
        if pending:
            pending.pop(0)()

    emit_projection_work()

    r = lax.broadcasted_iota(jnp.int32, (L, L), 0)
    s = lax.broadcasted_iota(jnp.int32, (L, L), 1)
    causal = s <= r
    shifts = [jnp.where(r - s == d, 1.0, 0.0).astype(BF16) for d in range(1, CONV_W)]

    def conv_silu(x_ref, halo_ref, lo):
        x = x_ref[...]
        halo = jnp.where(c > 0, halo_ref[...].astype(F32), 0.0)
        w = cw_ref[:, lo:lo + H_M * DK_M]
        y = cb_ref[:, lo:lo + H_M * DK_M] + w[CONV_W - 1:CONV_W] * x.astype(F32)
        top = jnp.zeros((SUBLANES, H_M * DK_M), F32)
        for d in range(1, CONV_W):
            wd = w[CONV_W - 1 - d:CONV_W - d]
            y = y + wd * jnp.dot(shifts[d - 1], x, preferred_element_type=F32)
            top = top + wd * jnp.concatenate(
                [halo[HALO - d:HALO], jnp.zeros((SUBLANES - d, H_M * DK_M), F32)], axis=0)
        y = jnp.concatenate([y[0:SUBLANES] + top, y[SUBLANES:]], axis=0)
        return y * jax.nn.sigmoid(y)

    q_all = conv_silu(q_ref, qh_ref, 0)
    emit_projection_work()
    k_all = conv_silu(k_ref, kh_ref, H_M * DK_M) * (DK_M ** -0.5)
    emit_projection_work()

    gc = g_ref[...] + gbr_ref[...]
    gr = gt_ref[S_IM:S_IM + 2 * H_M, :] + gbc_ref[S_IM:S_IM + 2 * H_M, :]
    tri_l = jnp.where(causal, 1.0, 0.0).astype(BF16)
    tri_u = jnp.where(r <= s, 1.0, 0.0).astype(BF16)
    b_cols = _split_dot(tri_l, _log_sigmoid(gc) * LOG2E)
    lf_rows = _log_sigmoid(gr) * LOG2E
    b_rows = jnp.dot(lf_rows.astype(BF16), tri_u, preferred_element_type=F32) \
        + jnp.dot((lf_rows - lf_rows.astype(BF16).astype(F32)).astype(BF16), tri_u,
                  preferred_element_type=F32)
    gc = gc * LOG2E
    gr = gr * LOG2E
    ones = jnp.ones((L, LANES), BF16)
    emit_projection_work()

    for h in range(H_M):
        emit_projection_work()
        q = q_all[:, h * DK_M:(h + 1) * DK_M]
        k = k_all[:, h * DK_M:(h + 1) * DK_M]
        v = jnp.concatenate([v_ref[:, h * DV_M:(h + 1) * DV_M], ones], axis=1)
        qb = q.astype(BF16)
        b_c = b_cols[:, S_FM + h:S_FM + h + 1]
        i_c = gc[:, S_IM + h:S_IM + h + 1]
        b_r = b_rows[H_M + h:H_M + h + 1, :]
        i_r = gr[h:h + 1, :]
        m_prev = m_ref[h]
        ct = ct_ref[h]

        log_d = jnp.where(causal, b_c - b_r + i_r, -jnp.inf)
        g = b_c + m_prev
        m_t = jnp.maximum(jnp.max(log_d, axis=-1, keepdims=True), g)
        qk = lax.dot_general(qb, k.astype(BF16), _NT, preferred_element_type=F32)
        s_mat = qk * jnp.exp2(log_d - m_t)
        inter = jnp.exp2(g - m_t)
        num = jnp.dot(s_mat.astype(BF16), v, preferred_element_type=F32) \
            + inter * jnp.dot(qb, ct.astype(BF16), preferred_element_type=F32)
        den = jnp.maximum(jnp.abs(num[:, DV_M:]), jnp.exp2(-m_t))
        hh = num[:, 0:DV_M] / jnp.concatenate([den] * (DV_M // LANES), axis=1)

        emit_projection_work()
        b_last = b_c[L - 1:L, :]
        a_r = b_last - b_r + i_r
        m_new = jnp.maximum(b_last + m_prev, jnp.max(a_r, axis=-1, keepdims=True))
        decay = jnp.exp2(b_last + m_prev - m_new)
        wgt_c = jnp.exp2(b_last - b_c + i_c - m_new)
        kw = k * wgt_c
        ct_ref[h] = decay * ct + jnp.dot(kw.T.astype(BF16), v, preferred_element_type=F32)
        m_ref[h] = m_new

        mu = jnp.mean(hh, axis=-1, keepdims=True)
        var = jnp.mean(jnp.square(hh - mu), axis=-1, keepdims=True)
        hn = (hh - mu) * lax.rsqrt(var + LN_EPS) * ng_ref[:, h * DV_M:(h + 1) * DV_M]
        og = o_ref[:, h * DV_M:(h + 1) * DV_M].astype(F32)
        zg = z_ref[:, h * DV_M:(h + 1) * DV_M].astype(F32)
        y_ref[:, h * DV_M:(h + 1) * DV_M] = (hn * jax.nn.sigmoid(og) * (zg * jax.nn.sigmoid(zg))).astype(BF16)

    assert not pending
    yp_ref[...] = y_ref[...]


def _mlstm_out(main, small, gate_t, conv_w, conv_b, gb_row, gb_col, norm_g, ya, x2d, w_out, ln_g, ln_b, B, T):
    nc = T // L_M
    hb = L_M // HALO
    qk_w = H_M * DK_M
    n_steps = B * nc

    def cur(col):
        return lambda s: (jnp.minimum(s, n_steps - 1), col)

    def prev(s):
        return (jnp.maximum(s - 1, 0), 0)

    def halo_map(col):
        return lambda s: (jnp.maximum(jnp.minimum(s, n_steps - 1) * hb - 1, 0), col)

    const = lambda s: (0, 0)
    return pl.pallas_call(
        functools.partial(_mlstm_out_kernel, n_chunks=nc, n_steps=n_steps),
        grid=(n_steps + 1,),
        in_specs=[pl.BlockSpec((L_M, qk_w), cur(C_QM // qk_w)),
                  pl.BlockSpec((L_M, qk_w), cur(C_KM // qk_w)),
                  pl.BlockSpec((HALO, qk_w), halo_map(C_QM // qk_w)),
                  pl.BlockSpec((HALO, qk_w), halo_map(C_KM // qk_w)),
                  pl.BlockSpec((L_M, W_M), cur(C_VM // W_M)),
                  pl.BlockSpec((L_M, W_M), cur(C_OM // W_M)),
                  pl.BlockSpec((L_M, W_M), cur(C_ZM // W_M)),
                  pl.BlockSpec((L_M, LANES), cur(D_C // LANES)),
                  pl.BlockSpec((LANES, L_M), lambda s: (0, jnp.minimum(s, n_steps - 1))),
                  pl.BlockSpec((CONV_W, 2 * qk_w), const),
                  pl.BlockSpec((1, 2 * qk_w), const),
                  pl.BlockSpec((1, LANES), const),
                  pl.BlockSpec((LANES, 1), const),
                  pl.BlockSpec((1, W_M), const),
                  pl.BlockSpec((L_M, W_A), prev),
                  pl.BlockSpec((L_M, D_MODEL), prev),
                  pl.BlockSpec((W_A + W_M, D_MODEL), const),
                  pl.BlockSpec((1, D_MODEL), const),
                  pl.BlockSpec((1, D_MODEL), const)],
        out_specs=pl.BlockSpec((L_M, D_MODEL), prev),
        out_shape=jax.ShapeDtypeStruct((B * T, D_MODEL), F32),
        scratch_shapes=[pltpu.VMEM((H_M, DK_M, DV_M + LANES), F32),
                        pltpu.VMEM((H_M, 1, 1), F32),
                        pltpu.VMEM((L_M, W_M), BF16),
                        pltpu.VMEM((L_M, W_M), BF16),
                        pltpu.VMEM((L_M, D_MODEL), F32)],
        compiler_params=pltpu.CompilerParams(
            dimension_semantics=("arbitrary",), vmem_limit_bytes=VMEM_LIMIT),
        name="mlstm_out",
    )(main, main, main, main, main, main, main, small, gate_t,
      conv_w, conv_b, gb_row, gb_col, norm_g, ya, x2d, w_out, ln_g, ln_b)


_W_IN_SEGS = (("q_a", W_A), ("c_kv", D_C), ("z_a", W_A), ("q_i", H_IDX * D_IDX), ("k_i", D_IDX),
              ("w_i", H_IDX), ("q_m", H_M * DK_M), ("k_m", H_M * DK_M), ("v_m", W_M), ("i_m", H_M),
              ("f_m", H_M), ("o_m", W_M), ("z_m", W_M))
_MAIN_ORDER = ("q_a", "z_a", "q_i", "q_m", "k_m", "v_m", "o_m", "z_m")
_SEG_NAMES = [name for name, _ in _W_IN_SEGS]
assert _SEG_NAMES.index("f_m") == _SEG_NAMES.index("i_m") + 1 and S_FM == S_IM + H_M


def _repack_kernel(wt_ref, main_ref, small_ref):
    src, off = {}, 0
    for name, width in _W_IN_SEGS:
        src[name] = (off, width)
        off += width
    dst = 0
    for name in _MAIN_ORDER:
        lo, width = src[name]
        main_ref[dst:dst + width, :] = wt_ref[lo:lo + width, :].astype(BF16)
        dst += width
    parts = [wt_ref[src[name][0]:src[name][0] + src[name][1], :] for name in ("c_kv", "k_i", "w_i")]
    lo = src["i_m"][0]
    parts.append(wt_ref[lo:lo + 2 * H_M, :])
    used = sum(p.shape[0] for p in parts)
    parts.append(jnp.zeros((N_SMALL - used, wt_ref.shape[1]), F32))
    small_ref[...] = jnp.concatenate(parts, axis=0).astype(BF16)


def _repack_w_in(w_in, tc=256):
    n_cols = sum(width for _, width in _W_IN_SEGS)
    wt = jnp.swapaxes(w_in, 1, 2)[0]
    return pl.pallas_call(
        _repack_kernel,
        grid=(D_MODEL // tc,),
        in_specs=[pl.BlockSpec((n_cols, tc), lambda i: (0, i))],
        out_specs=[pl.BlockSpec((N_MAIN, tc), lambda i: (0, i)),
                   pl.BlockSpec((N_SMALL, tc), lambda i: (0, i))],
        out_shape=[jax.ShapeDtypeStruct((N_MAIN, D_MODEL), BF16),
                   jax.ShapeDtypeStruct((N_SMALL, D_MODEL), BF16)],
        compiler_params=pltpu.CompilerParams(
            dimension_semantics=("arbitrary",), vmem_limit_bytes=VMEM_LIMIT),
        name="repack",
    )(wt)


def kernel(x, w_in, b_igate, b_fgate, kv_norm_g, w_uk, w_uv, idx_k_ln_g, idx_k_ln_b, rel_bias,
           conv_w, conv_b, mh_norm_g, w_out, ln_g, ln_b):
    B, T, D = x.shape
    assert D == D_MODEL and T % L_M == 0 and T % (2 * KB) == 0 and w_in.shape[0] == 1
    bias = _bias_tiles(rel_bias)
    x2d = x.reshape(B * T, D)
    w_main, w_small = _repack_w_in(w_in)
    main, small = _proj(x2d, w_main, w_small)
    ckv_n, ckv_t, kidx_n, gate_t = _prep(small, kv_norm_g[0][None], idx_k_ln_g[0][None], idx_k_ln_b[0][None])
    w_uk_t = jnp.transpose(w_uk[0], (0, 2, 1)).astype(BF16)
    w_uv_t = jnp.transpose(w_uv[0], (0, 2, 1)).astype(BF16)
    ya = _dsa(main, gate_t, ckv_n, ckv_t, kidx_n, w_uk_t, w_uv_t, bias, B, T)
    gb = jnp.zeros((LANES,), F32).at[S_IM:S_IM + H_M].set(b_igate[0]).at[S_FM:S_FM + H_M].set(b_fgate[0])
    out = _mlstm_out(main, small, gate_t, conv_w[0], conv_b[0][None], gb[None, :], gb[:, None],
                     mh_norm_g[0][None], ya, x2d, w_out[0].astype(BF16), ln_g[0][None], ln_b[0][None], B, T)
    return out.reshape(B, T, D)
```
